```python
import jax, jax.numpy as jnp
from jax import lax
import numpy as np

D_MODEL = 1024
BATCH = 4
SEQ = 4096
DEPTH = 1

MEM_LEN = 256
D_MIX = D_MODEL
DN_HEAD_DIM = 128
DN_HEADS = (D_MIX // 2) // DN_HEAD_DIM
D_DN = DN_HEADS * DN_HEAD_DIM
D_RG = D_MIX - D_DN
RG_BLOCKS = 4
RG_BLOCK_W = D_RG // RG_BLOCKS
RG_C = 8.0
CONV_WIDTH = 4
CHUNK = 64
XA_HEADS = 4
XA_HEAD_DIM = D_MODEL // XA_HEADS
N_EXPERTS = 32
TOP_K = 4
D_FF = D_MODEL
SWIGLU_LIMIT = 7.0
SWIGLU_ALPHA = 1.702
MOE_BLOCK = 128
EPS = 1e-6

kernel_name = "hybrid_deltanet_rglru_memxattn_moe"


def rmsnorm(x, g):
    xf = x.astype(jnp.float32)
    y = xf * lax.rsqrt(jnp.mean(xf * xf, axis=-1, keepdims=True) + EPS)
    return (y * g.astype(jnp.float32)).astype(x.dtype)


def l2norm(x):
    return x * lax.rsqrt(jnp.sum(x * x, axis=-1, keepdims=True) + EPS)


def causal_dwconv(x, w):
    K, C = w.shape
    return lax.conv_general_dilated(
        x, w[:, None, :].astype(x.dtype), window_strides=(1,), padding=[(K - 1, 0)],
        dimension_numbers=('NWC', 'WIO', 'NWC'), feature_group_count=C)


def gated_delta_rule(q, k, v, g, beta):
    B, T, H, DK = q.shape
    DV = v.shape[-1]
    C = CHUNK
    N = T // C
    q = q * (DK ** -0.5)

    def chunk(t):
        return t.reshape(B, N, C, H, -1).transpose(0, 3, 1, 2, 4)

    qc, kc, vc = chunk(q), chunk(k), chunk(v)
    gc = g.reshape(B, N, C, H).transpose(0, 3, 1, 2)
    bc = beta.reshape(B, N, C, H).transpose(0, 3, 1, 2)
    gcum = jnp.cumsum(gc, axis=-1)
    causal = jnp.tril(jnp.ones((C, C), bool))
    strict = jnp.tril(jnp.ones((C, C), bool), k=-1)
    diff = gcum[..., :, None] - gcum[..., None, :]
    decay = jnp.where(causal, jnp.exp(jnp.where(causal, diff, 0.0)), 0.0)
    k_beta = kc * bc[..., None]
    A = jnp.where(strict, jnp.einsum('bhncd,bhnsd->bhncs', k_beta, kc) * decay, 0.0)
    lower = A + jnp.eye(C, dtype=A.dtype)
    rhs = jnp.concatenate([vc * bc[..., None], k_beta * jnp.exp(gcum)[..., None]], axis=-1)
    sol = lax.linalg.triangular_solve(lower, rhs, left_side=True, lower=True, unit_diagonal=True)
    u, w = sol[..., :DV], sol[..., DV:]
    qk = jnp.einsum('bhncd,bhnsd->bhncs', qc, kc) * decay

    def step(S, inp):
        q_i, k_i, u_i, w_i, qk_i, g_i = inp
        v_new = u_i - jnp.matmul(w_i, S)
        o = jnp.matmul(q_i * jnp.exp(g_i)[..., None], S) + jnp.matmul(qk_i, v_new)
        g_last = g_i[..., -1]
        k_dec = k_i * jnp.exp(g_last[..., None] - g_i)[..., None]
        S = S * jnp.exp(g_last)[..., None, None] + jnp.einsum('bhcd,bhce->bhde', k_dec, v_new)
        return S, o

    mv = lambda t: jnp.moveaxis(t, 2, 0)
    S0 = jnp.zeros((B, H, DK, DV), jnp.float32)
    _, o = lax.scan(step, S0, (mv(qc), mv(kc), mv(u), mv(w), mv(qk), mv(gcum)))
    return o.transpose(1, 0, 3, 2, 4).reshape(B, T, H, DV)


def rg_lru(xr, w_a, b_a, w_x, b_x, lam):
    B, T, _ = xr.shape
    xf = xr.astype(jnp.float32)
    xb = xf.reshape(B, T, RG_BLOCKS, RG_BLOCK_W)
    r = jax.nn.sigmoid(jnp.einsum('btni,nio->btno', xb, w_a.astype(jnp.float32)) + b_a.astype(jnp.float32))
    i = jax.nn.sigmoid(jnp.einsum('btni,nio->btno', xb, w_x.astype(jnp.float32)) + b_x.astype(jnp.float32))
    r = r.reshape(B, T, D_RG)
    i = i.reshape(B, T, D_RG)
    log_a = RG_C * r * jax.nn.log_sigmoid(lam.astype(jnp.float32))
    a = jnp.exp(log_a)
    mult = jnp.sqrt(-jnp.expm1(2.0 * log_a))
    mult = jnp.where(jnp.arange(T)[None, :, None] == 0, 1.0, mult)
    bx = mult * (i * xf)

    def combine(c1, c2):
        a1, b1 = c1
        a2, b2 = c2
        return a1 * a2, a2 * b1 + b2

    _, h = lax.associative_scan(combine, (a, bx), axis=1)
    return h


def hybrid_mixer(u, w_in, dn_conv, dn_a_log, dn_dt_bias, dn_norm, rg_conv, rg_conv_b,
                 rg_w_a, rg_b_a, rg_w_x, rg_b_x, rg_lambda, w_out):
    B, T, _ = u.shape
    proj = u @ w_in
    sizes = (D_DN, D_DN, D_DN, D_DN, DN_HEADS, DN_HEADS, D_RG, D_RG)
    cuts = [int(c) for c in np.cumsum(sizes)[:-1]]
    q, k, v, z, b, a, rx, ry = jnp.split(proj, cuts, axis=-1)

    qkv = jax.nn.silu(causal_dwconv(jnp.concatenate([q, k, v], axis=-1), dn_conv)).astype(jnp.float32)
    q, k, v = jnp.split(qkv, 3, axis=-1)
    heads = lambda t: t.reshape(B, T, DN_HEADS, DN_HEAD_DIM)
    q, k, v = l2norm(heads(q)), l2norm(heads(k)), heads(v)
    beta = jax.nn.sigmoid(b.astype(jnp.float32))
    g = -jnp.exp(dn_a_log.astype(jnp.float32)) * jax.nn.softplus(
        a.astype(jnp.float32) + dn_dt_bias.astype(jnp.float32))
    o = gated_delta_rule(q, k, v, g, beta)
    o = rmsnorm(o, dn_norm) * jax.nn.silu(heads(z.astype(jnp.float32)))
    dn_out = o.reshape(B, T, D_DN).astype(u.dtype)

    xr = causal_dwconv(rx, rg_conv) + rg_conv_b.astype(rx.dtype)
    h = rg_lru(xr, rg_w_a, rg_b_a, rg_w_x, rg_b_x, rg_lambda)
    rg_out = (h * jax.nn.gelu(ry.astype(jnp.float32))).astype(u.dtype)

    return jnp.concatenate([dn_out, rg_out], axis=-1) @ w_out


def memory_cross_attention(hn, mem_n, w_cq, w_ckv, w_co):
    B, T, D = hn.shape
    M = mem_n.shape[1]
    q = (hn @ w_cq).reshape(B, T, XA_HEADS, XA_HEAD_DIM)
    k, v = jnp.split(mem_n @ w_ckv, 2, axis=-1)
    k = k.reshape(B, M, XA_HEADS, XA_HEAD_DIM)
    v = v.reshape(B, M, XA_HEADS, XA_HEAD_DIM)
    s = jnp.einsum('bthd,bmhd->bhtm', q, k).astype(jnp.float32) * (XA_HEAD_DIM ** -0.5)
    p = jax.nn.softmax(s, axis=-1).astype(hn.dtype)
    o = jnp.einsum('bhtm,bmhd->bthd', p, v).reshape(B, T, D)
    return o @ w_co


def moe_ffn(hn, w_router, b_router, w_gate, b_gate, w_up, b_up, w_down, b_down):
    B, T, D = hn.shape
    n_tok = B * T
    xt = hn.reshape(n_tok, D)
    logits = xt.astype(jnp.float32) @ w_router.astype(jnp.float32) + b_router.astype(jnp.float32)
    top_logit, top_e = lax.top_k(logits, TOP_K)
    gate = jax.nn.softmax(top_logit, axis=-1)
    n_assign = n_tok * TOP_K
    flat_e = top_e.reshape(-1)
    order = jnp.argsort(flat_e)
    sorted_e = flat_e[order]
    sorted_tok = order // TOP_K
    sorted_gate = gate.reshape(-1)[order]
    counts = jnp.bincount(flat_e, length=N_EXPERTS)
    padded = (counts + MOE_BLOCK - 1) // MOE_BLOCK * MOE_BLOCK
    pad_end = jnp.cumsum(padded)
    pad_start = pad_end - padded
    start = jnp.cumsum(counts) - counts
    dest = pad_start[sorted_e] + jnp.arange(n_assign) - start[sorted_e]
    n_blocks = -(-n_assign // MOE_BLOCK) + N_EXPERTS
    n_rows = n_blocks * MOE_BLOCK
    row_tok = jnp.full((n_rows,), n_tok, jnp.int32).at[dest].set(sorted_tok.astype(jnp.int32))
    row_gate = jnp.zeros((n_rows,), xt.dtype).at[dest].set(sorted_gate.astype(xt.dtype))
    block_e = jnp.minimum(jnp.searchsorted(pad_end, jnp.arange(n_blocks) * MOE_BLOCK, side='right'),
                          N_EXPERTS - 1)
    x_pad = jnp.concatenate([xt, jnp.zeros((1, D), xt.dtype)], axis=0)
    xb = x_pad[row_tok].reshape(n_blocks, MOE_BLOCK, D)

    def expert_block(args):
        xblk, e = args
        gt = jnp.minimum(xblk @ w_gate[e] + b_gate[e], SWIGLU_LIMIT)
        up = jnp.clip(xblk @ w_up[e] + b_up[e], -SWIGLU_LIMIT, SWIGLU_LIMIT)
        hid = (up + 1.0) * (gt * jax.nn.sigmoid(SWIGLU_ALPHA * gt))
        return hid @ w_down[e] + b_down[e]

    yb = lax.map(expert_block, (xb, block_e))
    y = yb.reshape(n_rows, D) * row_gate[:, None]
    out = jax.ops.segment_sum(y, row_tok, num_segments=n_tok + 1)[:n_tok]
    return out.reshape(B, T, D)


def setup_inputs(seed: int = 0) -> dict:
    key = jax.random.key(seed)
    ks = jax.random.split(key, 32)
    L = DEPTH
    f32 = jnp.float32
    nrm = lambda kk, shape, scale: jax.random.normal(kk, shape, f32) * scale
    gain = lambda kk, shape: 1.0 + 0.02 * jax.random.normal(kk, shape, f32)
    d_in = 4 * D_DN + 2 * DN_HEADS + 2 * D_RG
    dt = jnp.exp(jax.random.uniform(ks[6], (L, DN_HEADS), f32, np.log(1e-3), np.log(1e-1)))
    a8 = jax.random.uniform(ks[14], (L, D_RG), f32, 0.9, 0.999)
    s = a8 ** (1.0 / RG_C)
    return {
        'x': nrm(ks[0], (BATCH, SEQ, D_MODEL), 1.0),
        'mem': nrm(ks[1], (BATCH, MEM_LEN, D_MODEL), 1.0),
        'norm_mix': gain(ks[2], (L, D_MODEL)),
        'w_in': nrm(ks[3], (L, D_MODEL, d_in), D_MODEL ** -0.5),
        'dn_conv': nrm(ks[4], (L, CONV_WIDTH, 3 * D_DN), CONV_WIDTH ** -0.5),
        'dn_a_log': jnp.log(jax.random.uniform(ks[5], (L, DN_HEADS), f32, 1.0, 16.0)),
        'dn_dt_bias': dt + jnp.log(-jnp.expm1(-dt)),
        'dn_norm': gain(ks[7], (L, DN_HEAD_DIM)),
        'rg_conv': nrm(ks[8], (L, CONV_WIDTH, D_RG), CONV_WIDTH ** -0.5),
        'rg_conv_b': nrm(ks[9], (L, D_RG), 0.01),
        'rg_w_a': nrm(ks[10], (L, RG_BLOCKS, RG_BLOCK_W, RG_BLOCK_W), RG_BLOCK_W ** -0.5),
        'rg_b_a': nrm(ks[11], (L, RG_BLOCKS, RG_BLOCK_W), 0.01),
        'rg_w_x': nrm(ks[12], (L, RG_BLOCKS, RG_BLOCK_W, RG_BLOCK_W), RG_BLOCK_W ** -0.5),
        'rg_b_x': nrm(ks[13], (L, RG_BLOCKS, RG_BLOCK_W), 0.01),
        'rg_lambda': jnp.log(s) - jnp.log1p(-s),
        'w_out': nrm(ks[15], (L, D_MIX, D_MODEL), D_MIX ** -0.5),
        'norm_cross': gain(ks[16], (L, D_MODEL)),
        'norm_mem': gain(ks[17], (L, D_MODEL)),
        'w_cq': nrm(ks[18], (L, D_MODEL, D_MODEL), D_MODEL ** -0.5),
        'w_ckv': nrm(ks[19], (L, D_MODEL, 2 * D_MODEL), D_MODEL ** -0.5),
        'w_co': nrm(ks[20], (L, D_MODEL, D_MODEL), D_MODEL ** -0.5),
        'norm_moe': gain(ks[21], (L, D_MODEL)),
        'w_router': nrm(ks[22], (L, D_MODEL, N_EXPERTS), D_MODEL ** -0.5),
        'b_router': nrm(ks[23], (L, N_EXPERTS), 0.01),
        'w_gate': nrm(ks[24], (L, N_EXPERTS, D_MODEL, D_FF), D_MODEL ** -0.5),
        'b_gate': nrm(ks[25], (L, N_EXPERTS, D_FF), 0.01),
        'w_up': nrm(ks[26], (L, N_EXPERTS, D_MODEL, D_FF), D_MODEL ** -0.5),
        'b_up': nrm(ks[27], (L, N_EXPERTS, D_FF), 0.01),
        'w_down': nrm(ks[28], (L, N_EXPERTS, D_FF, D_MODEL), D_FF ** -0.5),
        'b_down': nrm(ks[29], (L, N_EXPERTS, D_MODEL), 0.01),
        'norm_final': gain(ks[30], (D_MODEL,)),
    }


def reference(x, mem, norm_mix, w_in, dn_conv, dn_a_log, dn_dt_bias, dn_norm, rg_conv, rg_conv_b,
              rg_w_a, rg_b_a, rg_w_x, rg_b_x, rg_lambda, w_out, norm_cross, norm_mem, w_cq, w_ckv,
              w_co, norm_moe, w_router, b_router, w_gate, b_gate, w_up, b_up, w_down, b_down,
              norm_final):
    h = x
    for l in range(DEPTH):
        h = h + hybrid_mixer(rmsnorm(h, norm_mix[l]), w_in[l], dn_conv[l], dn_a_log[l], dn_dt_bias[l],
                             dn_norm[l], rg_conv[l], rg_conv_b[l], rg_w_a[l], rg_b_a[l], rg_w_x[l],
                             rg_b_x[l], rg_lambda[l], w_out[l])
        h = h + memory_cross_attention(rmsnorm(h, norm_cross[l]), rmsnorm(mem, norm_mem[l]),
                                       w_cq[l], w_ckv[l], w_co[l])
        h = h + moe_ffn(rmsnorm(h, norm_moe[l]), w_router[l], b_router[l], w_gate[l], b_gate[l],
                        w_up[l], b_up[l], w_down[l], b_down[l])
    return rmsnorm(h, norm_final)
```

```python
import functools

import jax
import jax.numpy as jnp
from jax import lax
from jax.experimental import pallas as pl
from jax.experimental.pallas import tpu as pltpu

F32 = jnp.float32
BF16 = jnp.bfloat16
I32 = jnp.int32

EPS = 1e-6
LANES = 128
SUBLANES = 8
VMEM_LIMIT = 48 * 1024 * 1024

DN_HEADS = 4
DN_HEAD_DIM = 128
D_DN = DN_HEADS * DN_HEAD_DIM
D_RG = 512
RG_BLOCKS = 4
RG_C = 8.0
CONV_WIDTH = 4
XA_HEADS = 4
N_EXPERTS = 32
TOP_K = 4
SWIGLU_LIMIT = 7.0
SWIGLU_ALPHA = 1.702

DN_CHUNK = 128
INV_BASE = 16
MOE_BM = 256

COL_RX = 4 * D_DN
COL_RY = COL_RX + D_RG
COL_GB = COL_RY + D_RG
PROJ_W = COL_GB + LANES


def _cparams(n_axes=1):
    return pltpu.CompilerParams(
        dimension_semantics=("arbitrary",) * n_axes, vmem_limit_bytes=VMEM_LIMIT)


def _mm(a, b):
    return jnp.dot(a.astype(BF16), b.astype(BF16), preferred_element_type=F32)


def _mm_nt(a, b):
    return lax.dot_general(a.astype(BF16), b.astype(BF16), (((1,), (1,)), ((), ())),
                           preferred_element_type=F32)


def _mm_tn(a, b):
    return lax.dot_general(a.astype(BF16), b.astype(BF16), (((0,), (0,)), ((), ())),
                           preferred_element_type=F32)


def _rmsnorm(x, g):
    return x * lax.rsqrt(jnp.mean(x * x, axis=-1, keepdims=True) + EPS) * g


def _sigmoid(x):
    return 1.0 / (1.0 + jnp.exp(-x))


def _softplus(x):
    return jnp.maximum(x, 0.0) + jnp.log1p(jnp.exp(-jnp.abs(x)))


def _in_proj_kernel(x_ref, g_ref, w_ref, wbat_ref, prow_ref, pcol_ref, proj_ref, gbt_ref):
    u = _rmsnorm(x_ref[...], g_ref[...]).astype(BF16)
    for c0 in range(0, COL_GB, 512):
        proj_ref[:, c0:c0 + 512] = jnp.dot(u, w_ref[:, c0:c0 + 512], preferred_element_type=F32)
    ba = jnp.dot(u, w_ref[:, COL_GB:PROJ_W], preferred_element_type=F32)
    lane = lax.broadcasted_iota(I32, ba.shape, 1)
    g = -jnp.exp(prow_ref[0:1, :]) * _softplus(ba + prow_ref[1:2, :])
    proj_ref[:, COL_GB:PROJ_W] = jnp.where(lane < DN_HEADS, _sigmoid(ba), g)
    bat = lax.dot_general(wbat_ref[...], u, (((1,), (1,)), ((), ())), preferred_element_type=F32)
    row = lax.broadcasted_iota(I32, bat.shape, 0)
    gt = -jnp.exp(pcol_ref[:, 0:1]) * _softplus(bat + pcol_ref[:, 1:2])
    gbt_ref[...] = jnp.where(row < DN_HEADS, _sigmoid(bat), gt)


def _in_proj(x2, gain, w_cat, wbat, prow, pcol, tm=512):
    n, d = x2.shape
    return pl.pallas_call(
        _in_proj_kernel,
        grid=(n // tm,),
        in_specs=[
            pl.BlockSpec((tm, d), lambda i: (i, 0)),
            pl.BlockSpec((1, d), lambda i: (0, 0)),
            pl.BlockSpec((d, PROJ_W), lambda i: (0, 0)),
            pl.BlockSpec((2 * DN_HEADS, d), lambda i: (0, 0)),
            pl.BlockSpec((SUBLANES, LANES), lambda i: (0, 0)),
            pl.BlockSpec((2 * DN_HEADS, LANES), lambda i: (0, 0)),
        ],
        out_specs=[
            pl.BlockSpec((tm, PROJ_W), lambda i: (i, 0)),
            pl.BlockSpec((2 * DN_HEADS, tm), lambda i: (0, i)),
        ],
        out_shape=[
            jax.ShapeDtypeStruct((n, PROJ_W), F32),
            jax.ShapeDtypeStruct((2 * DN_HEADS, n), F32),
        ],
        compiler_params=_cparams(1),
        name="in_proj",
    )(x2, gain, w_cat, wbat, prow, pcol)


def _unit_lower_inverse(a, row, col):
    c = a.shape[0]
    eye = (row == col).astype(F32)
    shift = INV_BASE.bit_length() - 1
    diag = jnp.where((row >> shift) == (col >> shift), a, 0.0)
    p = eye - diag
    pw = diag
    for _ in range(shift - 1):
        pw = _mm(pw, pw)
        p = p + _mm(p, pw)
    s = INV_BASE
    while s < c:
        sh = s.bit_length() - 1
        off = ((row >> (sh + 1)) == (col >> (sh + 1))) & ((row >> sh) != (col >> sh))
        p = p - _mm(_mm(p, jnp.where(off, a, 0.0)), p)
        s *= 2
    return p


def _deltanet_kernel(q_ref, k_ref, v_ref, z_ref, gb_ref, gbt_ref, conv_ref, norm_ref, o_ref,
                     s_ref, halo_ref, ext_ref, act_ref, gcc_ref, gcr_ref, *, tb):
    t = pl.program_id(1)
    c = DN_CHUNK
    dh = DN_HEAD_DIM

    @pl.when(t == 0)
    def _():
        s_ref[...] = jnp.zeros_like(s_ref)
        halo_ref[...] = jnp.zeros_like(halo_ref)

    ext_ref[0:SUBLANES, :] = halo_ref[...]
    ext_ref[SUBLANES:, 0:D_DN] = q_ref[...]
    ext_ref[SUBLANES:, D_DN:2 * D_DN] = k_ref[...]
    ext_ref[SUBLANES:, 2 * D_DN:] = v_ref[...]
    halo_ref[...] = ext_ref[tb:tb + SUBLANES, :]
    for grp in range(3 * DN_HEADS):
        cs = slice(grp * dh, (grp + 1) * dh)
        y = jnp.zeros((tb, dh), F32)
        for j in range(CONV_WIDTH):
            off = SUBLANES - (CONV_WIDTH - 1) + j
            y = y + conv_ref[j:j + 1, cs] * ext_ref[off:off + tb, cs]
        y = y * _sigmoid(y)
        if grp < 2 * DN_HEADS:
            y = y * lax.rsqrt(jnp.sum(y * y, axis=-1, keepdims=True) + EPS)
        if grp < DN_HEADS:
            y = y * (dh ** -0.5)
        act_ref[:, cs] = y

    gcol = gb_ref[...]
    rpos = lax.broadcasted_iota(I32, gcol.shape, 0) & (c - 1)
    d = 1
    while d < c:
        gcol = gcol + jnp.where(rpos >= d, pltpu.roll(gcol, d, 0), 0.0)
        d *= 2
    gcc_ref[...] = gcol
    grow = gbt_ref[...]
    lpos = lax.broadcasted_iota(I32, grow.shape, 1) & (c - 1)
    d = 1
    while d < c:
        grow = grow + jnp.where(lpos >= d, pltpu.roll(grow, d, 1), 0.0)
        d *= 2
    gcr_ref[...] = grow

    row = lax.broadcasted_iota(I32, (c, c), 0)
    col = lax.broadcasted_iota(I32, (c, c), 1)
    causal = row >= col
    strict = row > col
    gain = norm_ref[...]

    def chunk(ci, carry):
        r0 = pl.multiple_of(ci * c, c)
        rows = pl.ds(r0, c)
        for h in range(DN_HEADS):
            q = act_ref[rows, h * dh:(h + 1) * dh]
            k = act_ref[rows, D_DN + h * dh:D_DN + (h + 1) * dh]
            v = act_ref[rows, 2 * D_DN + h * dh:2 * D_DN + (h + 1) * dh]
            beta = gb_ref[rows, h:h + 1]
            gc = gcc_ref[rows, DN_HEADS + h:DN_HEADS + h + 1]
            gr = gcr_ref[DN_HEADS + h:DN_HEADS + h + 1, rows]
            g_last = gc[c - 1:c, :]
            decay = jnp.where(causal, jnp.exp(jnp.where(causal, gc - gr, 0.0)), 0.0)
            kb = k * beta
            a = jnp.where(strict, _mm_nt(kb, k) * decay, 0.0)
            tinv = _unit_lower_inverse(a, row, col)
            egc = jnp.exp(gc)
            u = _mm(tinv, v * beta)
            w = _mm(tinv, kb * egc)
            qk = _mm_nt(q, k) * decay
            s = s_ref[h]
            v_new = u - _mm(w, s)
            o = _mm(q * egc, s) + _mm(qk, v_new)
            k_dec = k * jnp.exp(g_last - gc)
            s_ref[h] = s * jnp.exp(g_last) + _mm_tn(k_dec, v_new)
            o = _rmsnorm(o, gain)
            zz = z_ref[rows, h * dh:(h + 1) * dh]
            o_ref[rows, h * dh:(h + 1) * dh] = (o * (zz * _sigmoid(zz))).astype(o_ref.dtype)
        return carry

    lax.fori_loop(0, tb // c, chunk, 0)


def _deltanet(proj, gbt, dn_conv, dn_norm, batch, seq, tb=512):
    n = proj.shape[0]
    nt = seq // tb
    blk = lambda j: pl.BlockSpec((tb, D_DN), lambda b, t, j=j: (b * nt + t, j))
    return pl.pallas_call(
        functools.partial(_deltanet_kernel, tb=tb),
        grid=(batch, nt),
        in_specs=[
            blk(0), blk(1), blk(2), blk(3),
            pl.BlockSpec((tb, LANES), lambda b, t: (b * nt + t, COL_GB // LANES)),
            pl.BlockSpec((2 * DN_HEADS, tb), lambda b, t: (0, b * nt + t)),
            pl.BlockSpec((CONV_WIDTH, 3 * D_DN), lambda b, t: (0, 0)),
            pl.BlockSpec((1, DN_HEAD_DIM), lambda b, t: (0, 0)),
        ],
        out_specs=pl.BlockSpec((tb, D_DN), lambda b, t: (b * nt + t, 0)),
        out_shape=jax.ShapeDtypeStruct((n, D_DN), BF16),
        scratch_shapes=[
            pltpu.VMEM((DN_HEADS, DN_HEAD_DIM, DN_HEAD_DIM), F32),
            pltpu.VMEM((SUBLANES, 3 * D_DN), F32),
            pltpu.VMEM((tb + SUBLANES, 3 * D_DN), F32),
            pltpu.VMEM((tb, 3 * D_DN), F32),
            pltpu.VMEM((tb, LANES), F32),
            pltpu.VMEM((2 * DN_HEADS, tb), F32),
        ],
        compiler_params=_cparams(2),
        name="deltanet",
    )(proj, proj, proj, proj, proj, gbt, dn_conv, dn_norm)


def _gelu_tanh(x):
    return 0.5 * x * (1.0 + jnp.tanh(0.7978845608028654 * (x + 0.044715 * (x * x * x))))


def _rglru_kernel(rx_ref, ry_ref, conv_ref, convb_ref, wa_ref, ba_ref, wx_ref, bx_ref, lam_ref,
                  o_ref, hc_ref, halo_ref, ext_ref, a_ref, b_ref, h_ref, *, tb):
    t = pl.program_id(1)
    bw = D_RG // RG_BLOCKS

    @pl.when(t == 0)
    def _():
        hc_ref[...] = jnp.zeros_like(hc_ref)
        halo_ref[...] = jnp.zeros_like(halo_ref)

    ext_ref[0:SUBLANES, :] = halo_ref[...]
    ext_ref[SUBLANES:, :] = rx_ref[...]
    halo_ref[...] = ext_ref[tb:tb + SUBLANES, :]
    xr = jnp.zeros((tb, D_RG), F32) + convb_ref[...]
    for j in range(CONV_WIDTH):
        off = SUBLANES - (CONV_WIDTH - 1) + j
        xr = xr + conv_ref[j:j + 1, :] * ext_ref[off:off + tb, :]

    lam = lam_ref[...]
    log_sig = -_softplus(-lam)
    rowi = lax.broadcasted_iota(I32, (tb, bw), 0)
    seq_start_row = jnp.where(t == 0, 0, -1)
    for nb in range(RG_BLOCKS):
        cs = slice(nb * bw, (nb + 1) * bw)
        xb = xr[:, cs]
        r = _sigmoid(_mm(xb, wa_ref[nb]) + ba_ref[:, cs])
        gi = _sigmoid(_mm(xb, wx_ref[nb]) + bx_ref[:, cs])
        log_a = RG_C * r * log_sig[:, cs]
        a = jnp.exp(log_a)
        mult = jnp.sqrt(jnp.tanh(-log_a) * (1.0 + a * a))
        mult = jnp.where(rowi == seq_start_row, 1.0, mult)
        bx = mult * (gi * xb)
        rpos = rowi & (SUBLANES - 1)
        d = 1
        while d < SUBLANES:
            m = rpos >= d
            bx = jnp.where(m, a * pltpu.roll(bx, d, 0) + bx, bx)
            a = jnp.where(m, a * pltpu.roll(a, d, 0), a)
            d *= 2
        a_ref[:, cs] = a
        b_ref[:, cs] = bx

    def group(i, h_prev):
        rows = pl.ds(pl.multiple_of(i * SUBLANES, SUBLANES), SUBLANES)
        ht = a_ref[rows, :] * h_prev + b_ref[rows, :]
        h_ref[rows, :] = ht
        return ht[SUBLANES - 1:SUBLANES, :]

    hc_ref[...] = lax.fori_loop(0, tb // SUBLANES, group, hc_ref[...], unroll=8)
    o_ref[...] = (h_ref[...] * _gelu_tanh(ry_ref[...])).astype(o_ref.dtype)


def _rglru(proj, rg_conv, rg_conv_b, w_a, b_a, w_x, b_x, lam, batch, seq, tb=512):
    n = proj.shape[0]
    nt = seq // tb
    bw = D_RG // RG_BLOCKS
    full = lambda shape: pl.BlockSpec(shape, lambda b, t: (0,) * len(shape))
    return pl.pallas_call(
        functools.partial(_rglru_kernel, tb=tb),
        grid=(batch, nt),
        in_specs=[
            pl.BlockSpec((tb, D_RG), lambda b, t: (b * nt + t, COL_RX // D_RG)),
            pl.BlockSpec((tb, D_RG), lambda b, t: (b * nt + t, COL_RY // D_RG)),
            full((CONV_WIDTH, D_RG)), full((1, D_RG)),
            full((RG_BLOCKS, bw, bw)), full((1, D_RG)),
            full((RG_BLOCKS, bw, bw)), full((1, D_RG)),
            full((1, D_RG)),
        ],
        out_specs=pl.BlockSpec((tb, D_RG), lambda b, t: (b * nt + t, 0)),
        out_shape=jax.ShapeDtypeStruct((n, D_RG), BF16),
        scratch_shapes=[
            pltpu.VMEM((1, D_RG), F32),
            pltpu.VMEM((SUBLANES, D_RG), F32),
            pltpu.VMEM((tb + SUBLANES, D_RG), F32),
            pltpu.VMEM((tb, D_RG), F32),
            pltpu.VMEM((tb, D_RG), F32),
            pltpu.VMEM((tb, D_RG), F32),
        ],
        compiler_params=_cparams(2),
        name="rglru",
    )(proj, proj, rg_conv, rg_conv_b, w_a, b_a, w_x, b_x, lam)


def _mem_kv_kernel(m_ref, g_ref, w_ref, o_ref):
    mn = _rmsnorm(m_ref[...], g_ref[...]).astype(BF16)
    o_ref[...] = jnp.dot(mn, w_ref[...], preferred_element_type=F32).astype(o_ref.dtype)


def _mem_kv(mem2, gain, w_ckv, tm=256):
    n, d = mem2.shape
    return pl.pallas_call(
        _mem_kv_kernel,
        grid=(n // tm,),
        in_specs=[
            pl.BlockSpec((tm, d), lambda i: (i, 0)),
            pl.BlockSpec((1, d), lambda i: (0, 0)),
            pl.BlockSpec((d, 2 * d), lambda i: (0, 0)),
        ],
        out_specs=pl.BlockSpec((tm, 2 * d), lambda i: (i, 0)),
        out_shape=jax.ShapeDtypeStruct((n, 2 * d), BF16),
        compiler_params=_cparams(1),
        name="mem_kv",
    )(mem2, gain, w_ckv)


ROUTE_E = 0
ROUTE_RANK = TOP_K
ROUTE_GATE = 2 * TOP_K


def _post_mix_kernel(dn_ref, rg_ref, x_ref, wo_ref, gx_ref, wq_ref, kv_ref, wco_ref, gm_ref,
                     wr_ref, br_ref, h2_ref, xn_ref, route_ref, cnt_ref, carry_ref):
    i = pl.program_id(0)
    tm, d = x_ref.shape
    hd = d // XA_HEADS

    @pl.when(i == 0)
    def _():
        carry_ref[...] = jnp.zeros_like(carry_ref)

    h1 = (x_ref[...] + jnp.dot(dn_ref[...], wo_ref[0:D_DN, :], preferred_element_type=F32)
          + jnp.dot(rg_ref[...], wo_ref[D_DN:, :], preferred_element_type=F32))

    hn = _rmsnorm(h1, gx_ref[...]).astype(BF16)
    q = jnp.dot(hn, wq_ref[...], preferred_element_type=F32)
    heads = []
    for hh in range(XA_HEADS):
        cs = slice(hh * hd, (hh + 1) * hd)
        s = _mm_nt(q[:, cs], kv_ref[:, cs]) * (hd ** -0.5)
        p = jnp.exp(s - jnp.max(s, axis=-1, keepdims=True))
        p = p / jnp.sum(p, axis=-1, keepdims=True)
        heads.append(_mm(p, kv_ref[:, d + hh * hd:d + (hh + 1) * hd]).astype(BF16))
    o = jnp.concatenate(heads, axis=1)
    h2 = h1 + jnp.dot(o, wco_ref[...], preferred_element_type=F32)
    h2_ref[...] = h2

    xn = _rmsnorm(h2, gm_ref[...])
    xn_ref[...] = xn
    logits = jnp.dot(xn, wr_ref[...], preferred_element_type=F32,
                     precision=lax.Precision.HIGHEST) + br_ref[...]
    lane = lax.broadcasted_iota(I32, logits.shape, 1)
    lane_f = lane.astype(F32)
    neg = jnp.float32(-jnp.inf)
    lg = jnp.where(lane < N_EXPERTS, logits, neg)
    vals, idxs, hots = [], [], []
    for _ in range(TOP_K):
        m = jnp.max(lg, axis=-1, keepdims=True)
        idx = jnp.min(jnp.where(lg == m, lane_f, float(LANES)), axis=-1, keepdims=True)
        hot = lane_f == idx
        lg = jnp.where(hot, neg, lg)
        vals.append(m)
        idxs.append(idx)
        hots.append(hot)
    es = [jnp.exp(v - vals[0]) for v in vals]
    den = es[0] + es[1] + es[2] + es[3]
    gates = [e / den for e in es]

    chosen = jnp.zeros(logits.shape, F32)
    for hot in hots:
        chosen = chosen + jnp.where(hot, 1.0, 0.0)
    r2 = lax.broadcasted_iota(I32, (tm, tm), 0)
    c2 = lax.broadcasted_iota(I32, (tm, tm), 1)
    before = _mm(jnp.where(r2 > c2, 1.0, 0.0), chosen) + carry_ref[0:1, :]
    ranks = [jnp.sum(jnp.where(hot, before, 0.0), axis=-1, keepdims=True) for hot in hots]
    carry_ref[...] = carry_ref[...] + jnp.sum(chosen, axis=0, keepdims=True)
    cnt_ref[...] = carry_ref[...]

    rec = jnp.zeros(logits.shape, F32)
    for kk in range(TOP_K):
        rec = jnp.where(lane == ROUTE_E + kk, idxs[kk], rec)
        rec = jnp.where(lane == ROUTE_RANK + kk, ranks[kk], rec)
        rec = jnp.where(lane == ROUTE_GATE + kk, gates[kk], rec)
    route_ref[...] = rec


def _post_mix(dn, rg, x2, w_out, g_cross, w_cq, kv, w_co, g_moe, w_r, b_r, seq, mem_len, tm=256):
    n, d = x2.shape
    per_b = seq // tm
    full = lambda shape: pl.BlockSpec(shape, lambda i: (0,) * len(shape))
    return pl.pallas_call(
        _post_mix_kernel,
        grid=(n // tm,),
        in_specs=[
            pl.BlockSpec((tm, D_DN), lambda i: (i, 0)),
            pl.BlockSpec((tm, D_RG), lambda i: (i, 0)),
            pl.BlockSpec((tm, d), lambda i: (i, 0)),
            full((d, d)), full((1, d)), full((d, d)),
            pl.BlockSpec((mem_len, 2 * d), lambda i: (i // per_b, 0)),
            full((d, d)), full((1, d)),
            full((d, LANES)), full((1, LANES)),
        ],
        out_specs=[
            pl.BlockSpec((tm, d), lambda i: (i, 0)),
            pl.BlockSpec((tm, d), lambda i: (i, 0)),
            pl.BlockSpec((tm, LANES), lambda i: (i, 0)),
            pl.BlockSpec((SUBLANES, LANES), lambda i: (0, 0)),
        ],
        out_shape=[
            jax.ShapeDtypeStruct((n, d), F32),
            jax.ShapeDtypeStruct((n, d), F32),
            jax.ShapeDtypeStruct((n, LANES), F32),
            jax.ShapeDtypeStruct((SUBLANES, LANES), F32),
        ],
        scratch_shapes=[pltpu.VMEM((SUBLANES, LANES), F32)],
        compiler_params=_cparams(1),
        name="post_mix",
    )(dn, rg, x2, w_out, g_cross, w_cq, kv, w_co, g_moe, w_r, b_r)


def _dest_kernel(route_ref, start_ref, dest_ref):
    rec = route_ref[...]
    lane = lax.broadcasted_iota(I32, rec.shape, 1)
    lane_f = lane.astype(F32)
    start = start_ref[0:1, :]
    out = jnp.zeros(rec.shape, F32)
    for kk in range(TOP_K):
        e = rec[:, ROUTE_E + kk:ROUTE_E + kk + 1]
        base = jnp.sum(jnp.where(lane_f == e, start, 0.0), axis=-1, keepdims=True)
        out = jnp.where(lane == kk, base + rec[:, ROUTE_RANK + kk:ROUTE_RANK + kk + 1], out)
    dest_ref[...] = out.astype(I32)


def _dest(route, start_row, tm=2048):
    n = route.shape[0]
    return pl.pallas_call(
        _dest_kernel,
        grid=(n // tm,),
        in_specs=[
            pl.BlockSpec((tm, LANES), lambda i: (i, 0)),
            pl.BlockSpec((SUBLANES, LANES), lambda i: (0, 0)),
        ],
        out_specs=pl.BlockSpec((tm, LANES), lambda i: (i, 0)),
        out_shape=jax.ShapeDtypeStruct((n, LANES), I32),
        compiler_params=_cparams(1),
        name="dest",
    )(route, start_row)


def _dispatch_kernel(meta_ref, dest_ref, xn_ref, xs_ref, zero_ref, sem, zsem, *, tm, n_blocks):
    i = pl.program_id(0)
    d = xn_ref.shape[1]

    def row_copy(j):
        tok = j // TOP_K
        return pltpu.make_async_copy(
            xn_ref.at[pl.ds(tok, 1)], xs_ref.at[pl.ds(dest_ref[0, 0, j], 1)], sem)

    def issue(j, carry):
        row_copy(j).start()
        return carry

    lax.fori_loop(0, tm * TOP_K, issue, 0, unroll=8)

    @pl.when(i == 0)
    def _():
        zero_ref[...] = jnp.zeros_like(zero_ref)

        def pad_copy(r):
            return pltpu.make_async_copy(zero_ref.at[pl.ds(0, 1)], xs_ref.at[pl.ds(r, 1)], zsem)

        def per_expert(e, total):
            r0 = meta_ref[e]
            cnt = meta_ref[N_EXPERTS + e]

            def one(r, carry):
                pad_copy(r0 + r).start()
                return carry

            lax.fori_loop(0, cnt, one, 0)
            return total + cnt

        total = lax.fori_loop(0, N_EXPERTS, per_expert, 0)

        def drain(r, carry):
            pad_copy(0).wait()
            return carry

        lax.fori_loop(0, total, drain, 0)

        def tail_copy(b):
            return pltpu.make_async_copy(
                zero_ref, xs_ref.at[pl.ds(pl.multiple_of(b * MOE_BM, MOE_BM), MOE_BM)], zsem)

        n_used = meta_ref[2 * N_EXPERTS]

        def tail(b, carry):
            tail_copy(b).start()
            return carry

        lax.fori_loop(n_used, n_blocks, tail, 0)

        def tail_drain(b, carry):
            tail_copy(0).wait()
            return carry

        lax.fori_loop(n_used, n_blocks, tail_drain, 0)

    for _ in range(TOP_K):
        pltpu.make_async_copy(xn_ref, xs_ref.at[pl.ds(0, tm)], sem).wait()


def _dispatch(meta, dest3, xn, n_rows, tm=256):
    n, d = xn.shape
    n_blocks = n_rows // MOE_BM
    grid_spec = pltpu.PrefetchScalarGridSpec(
        num_scalar_prefetch=1,
        grid=(n // tm,),
        in_specs=[
            pl.BlockSpec((1, 1, tm * TOP_K), lambda i, meta: (i, 0, 0), memory_space=pltpu.SMEM),
            pl.BlockSpec((tm, d), lambda i, meta: (i, 0)),
        ],
        out_specs=pl.BlockSpec(memory_space=pl.ANY),
        scratch_shapes=[
            pltpu.VMEM((MOE_BM, d), F32),
            pltpu.SemaphoreType.DMA,
            pltpu.SemaphoreType.DMA,
        ],
    )
    return pl.pallas_call(
        functools.partial(_dispatch_kernel, tm=tm, n_blocks=n_blocks),
        grid_spec=grid_spec,
        out_shape=jax.ShapeDtypeStruct((n_rows, d), F32),
        compiler_params=_cparams(1),
        name="dispatch",
    )(meta, dest3, xn)


def _experts_kernel(be_ref, nu_ref, x_ref, wg_ref, bg_ref, wu_ref, bu_ref, wd_ref, bd_ref, y_ref,
                    wgb_ref, wub_ref, wdb_ref):
    i = pl.program_id(0)
    used = i < nu_ref[0]
    changed = (i == 0) | (be_ref[i] != be_ref[jnp.maximum(i - 1, 0)])

    @pl.when(used & changed)
    def _():
        wgb_ref[...] = wg_ref[...].astype(BF16)
        wub_ref[...] = wu_ref[...].astype(BF16)
        wdb_ref[...] = wd_ref[...].astype(BF16)

    @pl.when(used)
    def _():
        x = x_ref[...].astype(BF16)
        gt = jnp.minimum(jnp.dot(x, wgb_ref[...], preferred_element_type=F32) + bg_ref[...],
                         SWIGLU_LIMIT)
        up = jnp.clip(jnp.dot(x, wub_ref[...], preferred_element_type=F32) + bu_ref[...],
                      -SWIGLU_LIMIT, SWIGLU_LIMIT)
        hid = (up + 1.0) * (gt * _sigmoid(SWIGLU_ALPHA * gt))
        y_ref[...] = (jnp.dot(hid.astype(BF16), wdb_ref[...], preferred_element_type=F32)
                      + bd_ref[...])

    @pl.when(jnp.logical_not(used))
    def _():
        y_ref[...] = jnp.zeros_like(y_ref)


def _experts(block_e, n_used, xs, w_gate, b_gate, w_up, b_up, w_down, b_down):
    n_rows, d = xs.shape
    n_blocks = n_rows // MOE_BM
    d_ff = w_gate.shape[2]
    wspec = lambda k, m: pl.BlockSpec((None, k, m), lambda i, be, nu: (be[i], 0, 0))
    grid_spec = pltpu.PrefetchScalarGridSpec(
        num_scalar_prefetch=2,
        grid=(n_blocks,),
        in_specs=[
            pl.BlockSpec((MOE_BM, d), lambda i, be, nu: (jnp.minimum(i, nu[0] - 1), 0)),
            wspec(d, d_ff), wspec(1, d_ff), wspec(d, d_ff), wspec(1, d_ff),
            wspec(d_ff, d), wspec(1, d),
        ],
        out_specs=pl.BlockSpec((MOE_BM, d), lambda i, be, nu: (i, 0)),
        scratch_shapes=[
            pltpu.VMEM((d, d_ff), BF16), pltpu.VMEM((d, d_ff), BF16), pltpu.VMEM((d_ff, d), BF16),
        ],
    )
    return pl.pallas_call(
        _experts_kernel,
        grid_spec=grid_spec,
        out_shape=jax.ShapeDtypeStruct((n_rows, d), F32),
        compiler_params=_cparams(1),
        name="experts",
    )(block_e, n_used, xs, w_gate, b_gate, w_up, b_up, w_down, b_down)


def _combine_kernel(dcur_ref, dnext_ref, h2_ref, route_ref, gain_ref, ys_ref, o_ref,
                    buf_ref, sem, *, tm):
    i = pl.program_id(0)
    nsteps = pl.num_programs(0)
    slot = i & 1

    def row_copy(dref, s, j):
        tok = j // TOP_K
        kk = j % TOP_K
        return pltpu.make_async_copy(
            ys_ref.at[pl.ds(dref[0, 0, j], 1)], buf_ref.at[s, kk, pl.ds(tok, 1)], sem.at[s])

    def issue_all(dref, s):
        def issue(j, carry):
            row_copy(dref, s, j).start()
            return carry
        lax.fori_loop(0, tm * TOP_K, issue, 0, unroll=8)

    @pl.when(i == 0)
    def _():
        issue_all(dcur_ref, 0)

    @pl.when(i + 1 < nsteps)
    def _():
        issue_all(dnext_ref, 1 - slot)

    for kk in range(TOP_K):
        pltpu.make_async_copy(ys_ref.at[pl.ds(0, tm)], buf_ref.at[slot, kk], sem.at[slot]).wait()

    rec = route_ref[...]
    acc = h2_ref[...]
    for kk in range(TOP_K):
        acc = acc + rec[:, ROUTE_GATE + kk:ROUTE_GATE + kk + 1] * buf_ref[slot, kk]
    o_ref[...] = _rmsnorm(acc, gain_ref[...])


def _combine(dest3, h2, route, gain, ys, tm=128):
    n, d = h2.shape
    nsteps = n // tm
    return pl.pallas_call(
        functools.partial(_combine_kernel, tm=tm),
        grid=(nsteps,),
        in_specs=[
            pl.BlockSpec((1, 1, tm * TOP_K), lambda i: (i, 0, 0), memory_space=pltpu.SMEM),
            pl.BlockSpec((1, 1, tm * TOP_K), lambda i: (jnp.minimum(i + 1, nsteps - 1), 0, 0),
                         memory_space=pltpu.SMEM),
            pl.BlockSpec((tm, d), lambda i: (i, 0)),
            pl.BlockSpec((tm, LANES), lambda i: (i, 0)),
            pl.BlockSpec((1, d), lambda i: (0, 0)),
            pl.BlockSpec(memory_space=pl.ANY),
        ],
        out_specs=pl.BlockSpec((tm, d), lambda i: (i, 0)),
        out_shape=jax.ShapeDtypeStruct((n, d), F32),
        scratch_shapes=[
            pltpu.VMEM((2, TOP_K, tm, d), F32),
            pltpu.SemaphoreType.DMA((2,)),
        ],
        compiler_params=_cparams(1),
        name="combine",
    )(dest3, dest3, h2, route, gain, ys)


def _lane_row(vec, offset):
    row = jnp.zeros((LANES,), F32).at[offset:offset + vec.shape[0]].set(vec.astype(F32))
    return jnp.zeros((SUBLANES, LANES), F32).at[0].set(row)


def kernel(x, mem, norm_mix, w_in, dn_conv, dn_a_log, dn_dt_bias, dn_norm, rg_conv, rg_conv_b, rg_w_a, rg_b_a, rg_w_x, rg_b_x, rg_lambda, w_out, norm_cross, norm_mem, w_cq, w_ckv, w_co, norm_moe, w_router, b_router, w_gate, b_gate, w_up, b_up, w_down, b_down, norm_final):
    batch, seq, d = x.shape
    mem_len = mem.shape[1]
    n = batch * seq
    assert w_in.shape[0] == 1, "single-layer trunk"
    x2 = x.reshape(n, d)

    wi = w_in[0]
    n_gate = 2 * DN_HEADS
    w_cat = jnp.concatenate(
        [wi[:, :4 * D_DN], wi[:, 4 * D_DN + n_gate:],
         jnp.pad(wi[:, 4 * D_DN:4 * D_DN + n_gate], ((0, 0), (0, LANES - n_gate)))],
        axis=1).astype(BF16)
    wbat = wi[:, 4 * D_DN:4 * D_DN + n_gate].T.astype(BF16)
    prow = (jnp.zeros((SUBLANES, LANES), F32)
            .at[0, DN_HEADS:n_gate].set(dn_a_log[0]).at[1, DN_HEADS:n_gate].set(dn_dt_bias[0]))
    pcol = (jnp.zeros((n_gate, LANES), F32)
            .at[DN_HEADS:, 0].set(dn_a_log[0]).at[DN_HEADS:, 1].set(dn_dt_bias[0]))

    proj, gbt = _in_proj(x2, norm_mix, w_cat, wbat, prow, pcol)
    dn = _deltanet(proj, gbt, dn_conv[0], dn_norm, batch, seq)
    rg = _rglru(proj, rg_conv[0], rg_conv_b, rg_w_a[0].astype(BF16), rg_b_a[0].reshape(1, D_RG),
                rg_w_x[0].astype(BF16), rg_b_x[0].reshape(1, D_RG), rg_lambda, batch, seq)
    kv = _mem_kv(mem.reshape(batch * mem_len, d), norm_mem, w_ckv[0].astype(BF16))

    w_r = jnp.pad(w_router[0], ((0, 0), (0, LANES - N_EXPERTS)))
    b_r = jnp.pad(b_router, ((0, 0), (0, LANES - N_EXPERTS)))
    h2, xn, route, counts = _post_mix(
        dn, rg, x2, w_out[0].astype(BF16), norm_cross, w_cq[0].astype(BF16), kv,
        w_co[0].astype(BF16), norm_moe, w_r, b_r, seq, mem_len)

    n_blocks = n * TOP_K // MOE_BM + N_EXPERTS
    n_rows = n_blocks * MOE_BM
    cnt = counts[0, :N_EXPERTS].astype(I32)
    padded = (cnt + MOE_BM - 1) // MOE_BM * MOE_BM
    pad_end = jnp.cumsum(padded)
    pad_start = pad_end - padded
    n_used = (pad_end[-1:] // MOE_BM).astype(I32)
    block_e = jnp.minimum(
        jnp.sum(pad_end[None, :] <= (jnp.arange(n_blocks, dtype=I32) * MOE_BM)[:, None], axis=1),
        N_EXPERTS - 1).astype(I32)
    meta = jnp.concatenate([pad_start + cnt, padded - cnt, n_used]).astype(I32)

    dest = _dest(route, _lane_row(pad_start, 0))
    dest4 = dest[:, :TOP_K]
    xs = _dispatch(meta, dest4.reshape(n // 256, 1, 256 * TOP_K), xn, n_rows, tm=256)
    ys = _experts(block_e, n_used, xs, w_gate[0], b_gate[0][:, None, :], w_up[0],
                  b_up[0][:, None, :], w_down[0], b_down[0][:, None, :])
    out = _combine(dest4.reshape(n // 128, 1, 128 * TOP_K), h2, route,
                   norm_final.reshape(1, d), ys, tm=128)
    return out.reshape(batch, seq, d)
```

```python
import functools

import jax
import jax.numpy as jnp
from jax import lax
from jax.experimental import pallas as pl
from jax.experimental.pallas import tpu as pltpu

F32 = jnp.float32
BF16 = jnp.bfloat16
I32 = jnp.int32

EPS = 1e-6
LANES = 128
SUBLANES = 8
VMEM_LIMIT = 48 * 1024 * 1024

DN_HEADS = 4
DN_HEAD_DIM = 128
D_DN = DN_HEADS * DN_HEAD_DIM
D_RG = 512
RG_BLOCKS = 4
RG_C = 8.0
CONV_WIDTH = 4
XA_HEADS = 4
N_EXPERTS = 32
TOP_K = 4
SWIGLU_LIMIT = 7.0
SWIGLU_ALPHA = 1.702

DN_CHUNK = 128
INV_BASE = 16
MOE_BM = 256

COL_RX = 4 * D_DN
COL_RY = COL_RX + D_RG
COL_GB = COL_RY + D_RG
PROJ_W = COL_GB + LANES


def _cparams(n_axes=1):
    return pltpu.CompilerParams(
        dimension_semantics=("arbitrary",) * n_axes, vmem_limit_bytes=VMEM_LIMIT)


def _mm(a, b):
    return jnp.dot(a.astype(BF16), b.astype(BF16), preferred_element_type=F32)


def _mm_nt(a, b):
    return lax.dot_general(a.astype(BF16), b.astype(BF16), (((1,), (1,)), ((), ())),
                           preferred_element_type=F32)


def _rmsnorm(x, g):
    return x * lax.rsqrt(jnp.mean(x * x, axis=-1, keepdims=True) + EPS) * g


def _sigmoid(x):
    return 1.0 / (1.0 + jnp.exp(-x))


def _softplus(x):
    return jnp.maximum(x, 0.0) + jnp.log1p(jnp.exp(-jnp.abs(x)))


def _load_row_tiles(ref, rows, d):
    rt = d // LANES
    return jnp.concatenate([ref[pl.ds(s, rows, stride=rt), :] for s in range(rt)], axis=1)


def _store_row_tiles(ref, val):
    rows, d = val.shape
    rt = d // LANES
    for s in range(rt):
        ref[pl.ds(s, rows, stride=rt), :] = val[:, s * LANES:(s + 1) * LANES]


def _in_proj_kernel(x_ref, g_ref, w_ref, wbat_ref, prow_ref, pcol_ref, proj_ref, gbt_ref):
    u = _rmsnorm(x_ref[...], g_ref[...]).astype(BF16)
    for c0 in range(0, COL_GB, 512):
        proj_ref[:, c0:c0 + 512] = jnp.dot(u, w_ref[:, c0:c0 + 512], preferred_element_type=F32)
    ba = jnp.dot(u, w_ref[:, COL_GB:PROJ_W], preferred_element_type=F32)
    lane = lax.broadcasted_iota(I32, ba.shape, 1)
    g = -jnp.exp(prow_ref[0:1, :]) * _softplus(ba + prow_ref[1:2, :])
    proj_ref[:, COL_GB:PROJ_W] = jnp.where(lane < DN_HEADS, _sigmoid(ba), g)
    bat = lax.dot_general(wbat_ref[...], u, (((1,), (1,)), ((), ())), preferred_element_type=F32)
    row = lax.broadcasted_iota(I32, bat.shape, 0)
    gt = -jnp.exp(pcol_ref[:, 0:1]) * _softplus(bat + pcol_ref[:, 1:2])
    gbt_ref[...] = jnp.where(row < DN_HEADS, _sigmoid(bat), gt)


def _in_proj(x2, gain, w_cat, wbat, prow, pcol, tm=512):
    n, d = x2.shape
    return pl.pallas_call(
        _in_proj_kernel,
        grid=(n // tm,),
        in_specs=[
            pl.BlockSpec((tm, d), lambda i: (i, 0)),
            pl.BlockSpec((1, d), lambda i: (0, 0)),
            pl.BlockSpec((d, PROJ_W), lambda i: (0, 0)),
            pl.BlockSpec((2 * DN_HEADS, d), lambda i: (0, 0)),
            pl.BlockSpec((SUBLANES, LANES), lambda i: (0, 0)),
            pl.BlockSpec((2 * DN_HEADS, LANES), lambda i: (0, 0)),
        ],
        out_specs=[
            pl.BlockSpec((tm, PROJ_W), lambda i: (i, 0)),
            pl.BlockSpec((2 * DN_HEADS, tm), lambda i: (0, i)),
        ],
        out_shape=[
            jax.ShapeDtypeStruct((n, PROJ_W), F32),
            jax.ShapeDtypeStruct((2 * DN_HEADS, n), F32),
        ],
        compiler_params=_cparams(1),
        name="in_proj",
    )(x2, gain, w_cat, wbat, prow, pcol)


def _deltanet_kernel(q_ref, k_ref, v_ref, z_ref, gb_ref, gbt_ref, conv_ref, norm_ref, o_ref,
                     s_ref, halo_ref, ext_ref, act_ref, gcc_ref, gcr_ref,
                     a_ref, p_ref, d_ref, qk_ref, rhs_ref, u_ref, wq_ref, kdt_ref, *, tb):
    t = pl.program_id(1)
    c = DN_CHUNK
    dh = DN_HEAD_DIM

    @pl.when(t == 0)
    def _():
        s_ref[...] = jnp.zeros_like(s_ref)
        halo_ref[...] = jnp.zeros_like(halo_ref)

    ext_ref[0:SUBLANES, :] = halo_ref[...]
    ext_ref[SUBLANES:, 0:D_DN] = q_ref[...]
    ext_ref[SUBLANES:, D_DN:2 * D_DN] = k_ref[...]
    ext_ref[SUBLANES:, 2 * D_DN:] = v_ref[...]
    halo_ref[...] = ext_ref[tb:tb + SUBLANES, :]
    for grp in range(3 * DN_HEADS):
        cs = slice(grp * dh, (grp + 1) * dh)
        y = jnp.zeros((tb, dh), F32)
        for j in range(CONV_WIDTH):
            off = SUBLANES - (CONV_WIDTH - 1) + j
            y = y + conv_ref[j:j + 1, cs] * ext_ref[off:off + tb, cs]
        y = y * _sigmoid(y)
        if grp < 2 * DN_HEADS:
            y = y * lax.rsqrt(jnp.sum(y * y, axis=-1, keepdims=True) + EPS)
        if grp < DN_HEADS:
            y = y * (dh ** -0.5)
        act_ref[:, cs] = y

    gcol = gb_ref[...]
    rpos = lax.broadcasted_iota(I32, gcol.shape, 0) & (c - 1)
    d = 1
    while d < c:
        gcol = gcol + jnp.where(rpos >= d, pltpu.roll(gcol, d, 0), 0.0)
        d *= 2
    gcc_ref[...] = gcol
    grow = gbt_ref[...]
    lpos = lax.broadcasted_iota(I32, grow.shape, 1) & (c - 1)
    d = 1
    while d < c:
        grow = grow + jnp.where(lpos >= d, pltpu.roll(grow, d, 1), 0.0)
        d *= 2
    gcr_ref[...] = grow

    row = lax.broadcasted_iota(I32, (c, c), 0)
    col = lax.broadcasted_iota(I32, (c, c), 1)
    causal = row >= col
    strict = row > col
    eye = jnp.where(row == col, 1.0, 0.0)
    gain = norm_ref[...]
    n_chunks = tb // c
    probs = [(ci, h) for ci in range(n_chunks) for h in range(DN_HEADS)]

    decay_last = []
    for p, (ci, h) in enumerate(probs):
        rows = slice(ci * c, (ci + 1) * c)
        q = act_ref[rows, h * dh:(h + 1) * dh]
        k = act_ref[rows, D_DN + h * dh:D_DN + (h + 1) * dh]
        v = act_ref[rows, 2 * D_DN + h * dh:2 * D_DN + (h + 1) * dh]
        beta = gb_ref[rows, h:h + 1]
        gc = gcc_ref[rows, DN_HEADS + h:DN_HEADS + h + 1]
        gr = gcr_ref[DN_HEADS + h:DN_HEADS + h + 1, rows]
        g_last = gc[c - 1:c, :]
        decay = jnp.where(causal, jnp.exp(jnp.where(causal, gc - gr, 0.0)), 0.0)
        kb = k * beta
        both = _mm_nt(jnp.concatenate([kb, q], axis=0), k)
        a_ref[p] = jnp.where(strict, both[:c] * decay, 0.0)
        qk_ref[p] = (both[c:] * decay).astype(BF16)
        egc = jnp.exp(gc)
        rhs_ref[p] = jnp.concatenate([v * beta, kb * egc], axis=1).astype(BF16)
        wq_ref[p, c:, :] = (q * egc).astype(BF16)
        kdt_ref[p] = (k * jnp.exp(g_last - gc)).T.astype(BF16)
        decay_last.append(jnp.exp(g_last))

    shift = INV_BASE.bit_length() - 1
    blk = (row >> shift) == (col >> shift)
    for p in range(len(probs)):
        diag = jnp.where(blk, a_ref[p], 0.0)
        p_ref[p] = eye - diag
        d_ref[p] = _mm(diag, diag).astype(BF16)
    for it in range(shift - 1):
        for p in range(len(probs)):
            pw = d_ref[p]
            inv = p_ref[p]
            p_ref[p] = inv + _mm(inv, pw)
            if it < shift - 2:
                d_ref[p] = _mm(pw, pw).astype(BF16)
    s = INV_BASE
    while s < c:
        sh = s.bit_length() - 1
        off = ((row >> (sh + 1)) == (col >> (sh + 1))) & ((row >> sh) != (col >> sh))
        for p in range(len(probs)):
            d_ref[p] = _mm(p_ref[p], jnp.where(off, a_ref[p], 0.0)).astype(BF16)
        for p in range(len(probs)):
            inv = p_ref[p]
            p_ref[p] = inv - _mm(d_ref[p], inv)
        s *= 2
    for p in range(len(probs)):
        uw = _mm(p_ref[p], rhs_ref[p])
        u_ref[p] = uw[:, :dh]
        wq_ref[p, :c, :] = uw[:, dh:].astype(BF16)

    for ci in range(n_chunks):
        rows = slice(ci * c, (ci + 1) * c)
        ps = [ci * DN_HEADS + h for h in range(DN_HEADS)]
        s_old = [s_ref[h] for h in range(DN_HEADS)]
        ws = [_mm(wq_ref[p], s_old[h]) for h, p in enumerate(ps)]
        v_new = [u_ref[p] - ws[h][:c] for h, p in enumerate(ps)]
        outs = [ws[h][c:] + _mm(qk_ref[p], v_new[h]) for h, p in enumerate(ps)]
        for h, p in enumerate(ps):
            s_ref[h] = s_old[h] * decay_last[p] + _mm(kdt_ref[p], v_new[h])
        for h in range(DN_HEADS):
            o = _rmsnorm(outs[h], gain)
            zz = z_ref[rows, h * dh:(h + 1) * dh]
            o_ref[rows, h * dh:(h + 1) * dh] = (o * (zz * _sigmoid(zz))).astype(o_ref.dtype)


def _deltanet(proj, gbt, dn_conv, dn_norm, batch, seq, tb=512):
    n = proj.shape[0]
    nt = seq // tb
    c = DN_CHUNK
    n_prob = (tb // c) * DN_HEADS
    blk = lambda j: pl.BlockSpec((tb, D_DN), lambda b, t, j=j: (b * nt + t, j))
    return pl.pallas_call(
        functools.partial(_deltanet_kernel, tb=tb),
        grid=(batch, nt),
        in_specs=[
            blk(0), blk(1), blk(2), blk(3),
            pl.BlockSpec((tb, LANES), lambda b, t: (b * nt + t, COL_GB // LANES)),
            pl.BlockSpec((2 * DN_HEADS, tb), lambda b, t: (0, b * nt + t)),
            pl.BlockSpec((CONV_WIDTH, 3 * D_DN), lambda b, t: (0, 0)),
            pl.BlockSpec((1, DN_HEAD_DIM), lambda b, t: (0, 0)),
        ],
        out_specs=pl.BlockSpec((tb, D_DN), lambda b, t: (b * nt + t, 0)),
        out_shape=jax.ShapeDtypeStruct((n, D_DN), BF16),
        scratch_shapes=[
            pltpu.VMEM((DN_HEADS, DN_HEAD_DIM, DN_HEAD_DIM), F32),
            pltpu.VMEM((SUBLANES, 3 * D_DN), F32),
            pltpu.VMEM((tb + SUBLANES, 3 * D_DN), F32),
            pltpu.VMEM((tb, 3 * D_DN), F32),
            pltpu.VMEM((tb, LANES), F32),
            pltpu.VMEM((2 * DN_HEADS, tb), F32),
            pltpu.VMEM((n_prob, c, c), F32),
            pltpu.VMEM((n_prob, c, c), F32),
            pltpu.VMEM((n_prob, c, c), BF16),
            pltpu.VMEM((n_prob, c, c), BF16),
            pltpu.VMEM((n_prob, c, 2 * DN_HEAD_DIM), BF16),
            pltpu.VMEM((n_prob, c, DN_HEAD_DIM), F32),
            pltpu.VMEM((n_prob, 2 * c, DN_HEAD_DIM), BF16),
            pltpu.VMEM((n_prob, DN_HEAD_DIM, c), BF16),
        ],
        compiler_params=_cparams(2),
        name="deltanet",
    )(proj, proj, proj, proj, proj, gbt, dn_conv, dn_norm)


def _gelu_tanh(x):
    return 0.5 * x * (1.0 + jnp.tanh(0.7978845608028654 * (x + 0.044715 * (x * x * x))))


def _rglru_kernel(rx_ref, ry_ref, conv_ref, convb_ref, wa_ref, ba_ref, wx_ref, bx_ref, lam_ref,
                  o_ref, hc_ref, halo_ref, ext_ref, a_ref, b_ref, h_ref, *, tb):
    t = pl.program_id(1)
    bw = D_RG // RG_BLOCKS

    @pl.when(t == 0)
    def _():
        hc_ref[...] = jnp.zeros_like(hc_ref)
        halo_ref[...] = jnp.zeros_like(halo_ref)

    ext_ref[0:SUBLANES, :] = halo_ref[...]
    ext_ref[SUBLANES:, :] = rx_ref[...]
    halo_ref[...] = ext_ref[tb:tb + SUBLANES, :]
    xr = jnp.zeros((tb, D_RG), F32) + convb_ref[...]
    for j in range(CONV_WIDTH):
        off = SUBLANES - (CONV_WIDTH - 1) + j
        xr = xr + conv_ref[j:j + 1, :] * ext_ref[off:off + tb, :]

    lam = lam_ref[...]
    log_sig = -_softplus(-lam)
    rowi = lax.broadcasted_iota(I32, (tb, bw), 0)
    seq_start_row = jnp.where(t == 0, 0, -1)
    for nb in range(RG_BLOCKS):
        cs = slice(nb * bw, (nb + 1) * bw)
        xb = xr[:, cs]
        r = _sigmoid(_mm(xb, wa_ref[nb]) + ba_ref[:, cs])
        gi = _sigmoid(_mm(xb, wx_ref[nb]) + bx_ref[:, cs])
        log_a = RG_C * r * log_sig[:, cs]
        a = jnp.exp(log_a)
        mult = jnp.sqrt(jnp.tanh(-log_a) * (1.0 + a * a))
        mult = jnp.where(rowi == seq_start_row, 1.0, mult)
        bx = mult * (gi * xb)
        rpos = rowi & (SUBLANES - 1)
        d = 1
        while d < SUBLANES:
            m = rpos >= d
            bx = jnp.where(m, a * pltpu.roll(bx, d, 0) + bx, bx)
            a = jnp.where(m, a * pltpu.roll(a, d, 0), a)
            d *= 2
        a_ref[:, cs] = a
        b_ref[:, cs] = bx

    def group(i, h_prev):
        rows = pl.ds(pl.multiple_of(i * SUBLANES, SUBLANES), SUBLANES)
        ht = a_ref[rows, :] * h_prev + b_ref[rows, :]
        h_ref[rows, :] = ht
        return ht[SUBLANES - 1:SUBLANES, :]

    hc_ref[...] = lax.fori_loop(0, tb // SUBLANES, group, hc_ref[...], unroll=8)
    o_ref[...] = (h_ref[...] * _gelu_tanh(ry_ref[...])).astype(o_ref.dtype)


def _rglru(proj, rg_conv, rg_conv_b, w_a, b_a, w_x, b_x, lam, batch, seq, tb=512):
    n = proj.shape[0]
    nt = seq // tb
    bw = D_RG // RG_BLOCKS
    full = lambda shape: pl.BlockSpec(shape, lambda b, t: (0,) * len(shape))
    return pl.pallas_call(
        functools.partial(_rglru_kernel, tb=tb),
        grid=(batch, nt),
        in_specs=[
            pl.BlockSpec((tb, D_RG), lambda b, t: (b * nt + t, COL_RX // D_RG)),
            pl.BlockSpec((tb, D_RG), lambda b, t: (b * nt + t, COL_RY // D_RG)),
            full((CONV_WIDTH, D_RG)), full((1, D_RG)),
            full((RG_BLOCKS, bw, bw)), full((1, D_RG)),
            full((RG_BLOCKS, bw, bw)), full((1, D_RG)),
            full((1, D_RG)),
        ],
        out_specs=pl.BlockSpec((tb, D_RG), lambda b, t: (b * nt + t, 0)),
        out_shape=jax.ShapeDtypeStruct((n, D_RG), BF16),
        scratch_shapes=[
            pltpu.VMEM((1, D_RG), F32),
            pltpu.VMEM((SUBLANES, D_RG), F32),
            pltpu.VMEM((tb + SUBLANES, D_RG), F32),
            pltpu.VMEM((tb, D_RG), F32),
            pltpu.VMEM((tb, D_RG), F32),
            pltpu.VMEM((tb, D_RG), F32),
        ],
        compiler_params=_cparams(2),
        name="rglru",
    )(proj, proj, rg_conv, rg_conv_b, w_a, b_a, w_x, b_x, lam)


def _mem_kv_kernel(m_ref, g_ref, w_ref, o_ref):
    mn = _rmsnorm(m_ref[...], g_ref[...]).astype(BF16)
    o_ref[...] = jnp.dot(mn, w_ref[...], preferred_element_type=F32).astype(o_ref.dtype)


def _mem_kv(mem2, gain, w_ckv, tm=256):
    n, d = mem2.shape
    return pl.pallas_call(
        _mem_kv_kernel,
        grid=(n // tm,),
        in_specs=[
            pl.BlockSpec((tm, d), lambda i: (i, 0)),
            pl.BlockSpec((1, d), lambda i: (0, 0)),
            pl.BlockSpec((d, 2 * d), lambda i: (0, 0)),
        ],
        out_specs=pl.BlockSpec((tm, 2 * d), lambda i: (i, 0)),
        out_shape=jax.ShapeDtypeStruct((n, 2 * d), BF16),
        compiler_params=_cparams(1),
        name="mem_kv",
    )(mem2, gain, w_ckv)


ROUTE_E = 0
ROUTE_RANK = TOP_K
ROUTE_GATE = 2 * TOP_K


def _post_mix_kernel(dn_ref, rg_ref, x_ref, wo_ref, gx_ref, wq_ref, kv_ref, wco_ref, gm_ref,
                     wr_ref, br_ref, h2_ref, xn_ref, route_ref, cnt_ref, carry_ref):
    i = pl.program_id(0)
    tm, d = x_ref.shape
    hd = d // XA_HEADS

    @pl.when(i == 0)
    def _():
        carry_ref[...] = jnp.zeros_like(carry_ref)

    h1 = (x_ref[...] + jnp.dot(dn_ref[...], wo_ref[0:D_DN, :], preferred_element_type=F32)
          + jnp.dot(rg_ref[...], wo_ref[D_DN:, :], preferred_element_type=F32))

    hn = _rmsnorm(h1, gx_ref[...]).astype(BF16)
    q = jnp.dot(hn, wq_ref[...], preferred_element_type=F32)
    heads = []
    for hh in range(XA_HEADS):
        cs = slice(hh * hd, (hh + 1) * hd)
        s = _mm_nt(q[:, cs], kv_ref[:, cs]) * (hd ** -0.5)
        p = jnp.exp(s - jnp.max(s, axis=-1, keepdims=True))
        p = p / jnp.sum(p, axis=-1, keepdims=True)
        heads.append(_mm(p, kv_ref[:, d + hh * hd:d + (hh + 1) * hd]).astype(BF16))
    o = jnp.concatenate(heads, axis=1)
    h2 = h1 + jnp.dot(o, wco_ref[...], preferred_element_type=F32)
    h2_ref[...] = h2

    xn = _rmsnorm(h2, gm_ref[...])
    _store_row_tiles(xn_ref, xn)
    logits = jnp.dot(xn, wr_ref[...], preferred_element_type=F32,
                     precision=lax.Precision.HIGHEST) + br_ref[...]
    lane = lax.broadcasted_iota(I32, logits.shape, 1)
    lane_f = lane.astype(F32)
    neg = jnp.float32(-jnp.inf)
    lg = jnp.where(lane < N_EXPERTS, logits, neg)
    vals, idxs, hots = [], [], []
    for _ in range(TOP_K):
        m = jnp.max(lg, axis=-1, keepdims=True)
        idx = jnp.min(jnp.where(lg == m, lane_f, float(LANES)), axis=-1, keepdims=True)
        hot = lane_f == idx
        lg = jnp.where(hot, neg, lg)
        vals.append(m)
        idxs.append(idx)
        hots.append(hot)
    es = [jnp.exp(v - vals[0]) for v in vals]
    den = es[0] + es[1] + es[2] + es[3]
    gates = [e / den for e in es]

    chosen = jnp.zeros(logits.shape, F32)
    for hot in hots:
        chosen = chosen + jnp.where(hot, 1.0, 0.0)
    r2 = lax.broadcasted_iota(I32, (tm, tm), 0)
    c2 = lax.broadcasted_iota(I32, (tm, tm), 1)
    before = _mm(jnp.where(r2 > c2, 1.0, 0.0), chosen) + carry_ref[0:1, :]
    ranks = [jnp.sum(jnp.where(hot, before, 0.0), axis=-1, keepdims=True) for hot in hots]
    carry_ref[...] = carry_ref[...] + jnp.sum(chosen, axis=0, keepdims=True)
    cnt_ref[...] = carry_ref[...]

    rec = jnp.zeros(logits.shape, F32)
    for kk in range(TOP_K):
        rec = jnp.where(lane == ROUTE_E + kk, idxs[kk], rec)
        rec = jnp.where(lane == ROUTE_RANK + kk, ranks[kk], rec)
        rec = jnp.where(lane == ROUTE_GATE + kk, gates[kk], rec)
    route_ref[...] = rec


def _post_mix(dn, rg, x2, w_out, g_cross, w_cq, kv, w_co, g_moe, w_r, b_r, seq, mem_len, tm=256):
    n, d = x2.shape
    per_b = seq // tm
    full = lambda shape: pl.BlockSpec(shape, lambda i: (0,) * len(shape))
    return pl.pallas_call(
        _post_mix_kernel,
        grid=(n // tm,),
        in_specs=[
            pl.BlockSpec((tm, D_DN), lambda i: (i, 0)),
            pl.BlockSpec((tm, D_RG), lambda i: (i, 0)),
            pl.BlockSpec((tm, d), lambda i: (i, 0)),
            full((d, d)), full((1, d)), full((d, d)),
            pl.BlockSpec((mem_len, 2 * d), lambda i: (i // per_b, 0)),
            full((d, d)), full((1, d)),
            full((d, LANES)), full((1, LANES)),
        ],
        out_specs=[
            pl.BlockSpec((tm, d), lambda i: (i, 0)),
            pl.BlockSpec((tm * (d // LANES), LANES), lambda i: (i, 0)),
            pl.BlockSpec((tm, LANES), lambda i: (i, 0)),
            pl.BlockSpec((SUBLANES, LANES), lambda i: (0, 0)),
        ],
        out_shape=[
            jax.ShapeDtypeStruct((n, d), F32),
            jax.ShapeDtypeStruct((n * (d // LANES), LANES), F32),
            jax.ShapeDtypeStruct((n, LANES), F32),
            jax.ShapeDtypeStruct((SUBLANES, LANES), F32),
        ],
        scratch_shapes=[pltpu.VMEM((SUBLANES, LANES), F32)],
        compiler_params=_cparams(1),
        name="post_mix",
    )(dn, rg, x2, w_out, g_cross, w_cq, kv, w_co, g_moe, w_r, b_r)


def _dest_kernel(route_ref, start_ref, dest_ref):
    rec = route_ref[...]
    lane = lax.broadcasted_iota(I32, rec.shape, 1)
    lane_f = lane.astype(F32)
    start = start_ref[0:1, :]
    out = jnp.zeros(rec.shape, F32)
    for kk in range(TOP_K):
        e = rec[:, ROUTE_E + kk:ROUTE_E + kk + 1]
        base = jnp.sum(jnp.where(lane_f == e, start, 0.0), axis=-1, keepdims=True)
        out = jnp.where(lane == kk, base + rec[:, ROUTE_RANK + kk:ROUTE_RANK + kk + 1], out)
    dest_ref[...] = out.astype(I32)


def _dest(route, start_row, tm=2048):
    n = route.shape[0]
    return pl.pallas_call(
        _dest_kernel,
        grid=(n // tm,),
        in_specs=[
            pl.BlockSpec((tm, LANES), lambda i: (i, 0)),
            pl.BlockSpec((SUBLANES, LANES), lambda i: (0, 0)),
        ],
        out_specs=pl.BlockSpec((tm, LANES), lambda i: (i, 0)),
        out_shape=jax.ShapeDtypeStruct((n, LANES), I32),
        compiler_params=_cparams(1),
        name="dest",
    )(route, start_row)


def _row_tile(ref, r, rt):
    return ref.at[pl.ds(pl.multiple_of(r * rt, rt), rt)]


def _dispatch_kernel(meta_ref, dest_ref, xn_ref, xs_ref, zero_ref, sem, zsem, *, tm, rt, n_blocks):
    i = pl.program_id(0)
    bm = MOE_BM * rt

    def per_token(tk, carry):
        src = _row_tile(xn_ref, tk, rt)
        for kk in range(TOP_K):
            r = dest_ref[0, 0, tk * TOP_K + kk]
            pltpu.make_async_copy(src, _row_tile(xs_ref, r, rt), sem).start()
        return carry

    lax.fori_loop(0, tm, per_token, 0, unroll=4)

    @pl.when(i == 0)
    def _():
        zero_ref[...] = jnp.zeros_like(zero_ref)

        def pad_copy(r):
            return pltpu.make_async_copy(zero_ref.at[pl.ds(0, rt)], _row_tile(xs_ref, r, rt), zsem)

        def per_expert(e, total):
            r0 = meta_ref[e]
            cnt = meta_ref[N_EXPERTS + e]

            def one(r, carry):
                pad_copy(r0 + r).start()
                return carry

            lax.fori_loop(0, cnt, one, 0)
            return total + cnt

        total = lax.fori_loop(0, N_EXPERTS, per_expert, 0)

        def drain(r, carry):
            pad_copy(0).wait()
            return carry

        lax.fori_loop(0, total, drain, 0)

        def tail_copy(b):
            return pltpu.make_async_copy(
                zero_ref, xs_ref.at[pl.ds(pl.multiple_of(b * bm, bm), bm)], zsem)

        n_used = meta_ref[2 * N_EXPERTS]

        def tail(b, carry):
            tail_copy(b).start()
            return carry

        lax.fori_loop(n_used, n_blocks, tail, 0)

        def tail_drain(b, carry):
            tail_copy(0).wait()
            return carry

        lax.fori_loop(n_used, n_blocks, tail_drain, 0)

    for _ in range(TOP_K):
        pltpu.make_async_copy(xn_ref, xs_ref.at[pl.ds(0, tm * rt)], sem).wait()


def _dispatch(meta, dest3, xn_t, n_rows, d, tm=256):
    rt = d // LANES
    n = xn_t.shape[0] // rt
    n_blocks = n_rows // MOE_BM
    grid_spec = pltpu.PrefetchScalarGridSpec(
        num_scalar_prefetch=1,
        grid=(n // tm,),
        in_specs=[
            pl.BlockSpec((1, 1, tm * TOP_K), lambda i, meta: (i, 0, 0), memory_space=pltpu.SMEM),
            pl.BlockSpec((tm * rt, LANES), lambda i, meta: (i, 0)),
        ],
        out_specs=pl.BlockSpec(memory_space=pl.ANY),
        scratch_shapes=[
            pltpu.VMEM((MOE_BM * rt, LANES), F32),
            pltpu.SemaphoreType.DMA,
            pltpu.SemaphoreType.DMA,
        ],
    )
    return pl.pallas_call(
        functools.partial(_dispatch_kernel, tm=tm, rt=rt, n_blocks=n_blocks),
        grid_spec=grid_spec,
        out_shape=jax.ShapeDtypeStruct((n_rows * rt, LANES), F32),
        compiler_params=_cparams(1),
        name="dispatch",
    )(meta, dest3, xn_t)


def _experts_kernel(be_ref, nu_ref, x_ref, wg_ref, bg_ref, wu_ref, bu_ref, wd_ref, bd_ref, y_ref,
                    wgb_ref, wub_ref, wdb_ref):
    i = pl.program_id(0)
    used = i < nu_ref[0]
    changed = (i == 0) | (be_ref[i] != be_ref[jnp.maximum(i - 1, 0)])

    @pl.when(used & changed)
    def _():
        wgb_ref[...] = wg_ref[...].astype(BF16)
        wub_ref[...] = wu_ref[...].astype(BF16)
        wdb_ref[...] = wd_ref[...].astype(BF16)

    @pl.when(used)
    def _():
        d = wgb_ref.shape[0]
        x = _load_row_tiles(x_ref, MOE_BM, d).astype(BF16)
        gt = jnp.minimum(jnp.dot(x, wgb_ref[...], preferred_element_type=F32) + bg_ref[...],
                         SWIGLU_LIMIT)
        up = jnp.clip(jnp.dot(x, wub_ref[...], preferred_element_type=F32) + bu_ref[...],
                      -SWIGLU_LIMIT, SWIGLU_LIMIT)
        hid = (up + 1.0) * (gt * _sigmoid(SWIGLU_ALPHA * gt))
        _store_row_tiles(y_ref, jnp.dot(hid.astype(BF16), wdb_ref[...],
                                        preferred_element_type=F32) + bd_ref[...])

    @pl.when(jnp.logical_not(used))
    def _():
        y_ref[...] = jnp.zeros_like(y_ref)


def _experts(block_e, n_used, xs_t, w_gate, b_gate, w_up, b_up, w_down, b_down):
    d, d_ff = w_gate.shape[1:]
    rt = d // LANES
    n_blocks = xs_t.shape[0] // (MOE_BM * rt)
    wspec = lambda k, m: pl.BlockSpec((None, k, m), lambda i, be, nu: (be[i], 0, 0))
    grid_spec = pltpu.PrefetchScalarGridSpec(
        num_scalar_prefetch=2,
        grid=(n_blocks,),
        in_specs=[
            pl.BlockSpec((MOE_BM * rt, LANES), lambda i, be, nu: (jnp.minimum(i, nu[0] - 1), 0)),
            wspec(d, d_ff), wspec(1, d_ff), wspec(d, d_ff), wspec(1, d_ff),
            wspec(d_ff, d), wspec(1, d),
        ],
        out_specs=pl.BlockSpec((MOE_BM * rt, LANES), lambda i, be, nu: (i, 0)),
        scratch_shapes=[
            pltpu.VMEM((d, d_ff), BF16), pltpu.VMEM((d, d_ff), BF16), pltpu.VMEM((d_ff, d), BF16),
        ],
    )
    return pl.pallas_call(
        _experts_kernel,
        grid_spec=grid_spec,
        out_shape=jax.ShapeDtypeStruct(xs_t.shape, F32),
        compiler_params=_cparams(1),
        name="experts",
    )(block_e, n_used, xs_t, w_gate, b_gate, w_up, b_up, w_down, b_down)


def _combine_kernel(dcur_ref, dnext_ref, h2_ref, route_ref, gain_ref, ys_ref, o_ref,
                    buf_ref, sem, *, tm):
    i = pl.program_id(0)
    nsteps = pl.num_programs(0)
    slot = i & 1
    d = h2_ref.shape[1]
    rt = d // LANES

    def issue_all(dref, s):
        def per_token(tk, carry):
            for kk in range(TOP_K):
                r = dref[0, 0, tk * TOP_K + kk]
                pltpu.make_async_copy(_row_tile(ys_ref, r, rt),
                                      _row_tile(buf_ref.at[s, kk], tk, rt), sem.at[s]).start()
            return carry
        lax.fori_loop(0, tm, per_token, 0, unroll=4)

    @pl.when(i == 0)
    def _():
        issue_all(dcur_ref, 0)

    @pl.when(i + 1 < nsteps)
    def _():
        issue_all(dnext_ref, 1 - slot)

    for kk in range(TOP_K):
        pltpu.make_async_copy(ys_ref.at[pl.ds(0, tm * rt)], buf_ref.at[slot, kk],
                              sem.at[slot]).wait()

    rec = route_ref[...]
    acc = h2_ref[...]
    for kk in range(TOP_K):
        acc = acc + (rec[:, ROUTE_GATE + kk:ROUTE_GATE + kk + 1]
                     * _load_row_tiles(buf_ref.at[slot, kk], tm, d))
    o_ref[...] = _rmsnorm(acc, gain_ref[...])


def _combine(dest3, h2, route, gain, ys_t, tm=128):
    n, d = h2.shape
    nsteps = n // tm
    return pl.pallas_call(
        functools.partial(_combine_kernel, tm=tm),
        grid=(nsteps,),
        in_specs=[
            pl.BlockSpec((1, 1, tm * TOP_K), lambda i: (i, 0, 0), memory_space=pltpu.SMEM),
            pl.BlockSpec((1, 1, tm * TOP_K), lambda i: (jnp.minimum(i + 1, nsteps - 1), 0, 0),
                         memory_space=pltpu.SMEM),
            pl.BlockSpec((tm, d), lambda i: (i, 0)),
            pl.BlockSpec((tm, LANES), lambda i: (i, 0)),
            pl.BlockSpec((1, d), lambda i: (0, 0)),
            pl.BlockSpec(memory_space=pl.ANY),
        ],
        out_specs=pl.BlockSpec((tm, d), lambda i: (i, 0)),
        out_shape=jax.ShapeDtypeStruct((n, d), F32),
        scratch_shapes=[
            pltpu.VMEM((2, TOP_K, tm * (d // LANES), LANES), F32),
            pltpu.SemaphoreType.DMA((2,)),
        ],
        compiler_params=_cparams(1),
        name="combine",
    )(dest3, dest3, h2, route, gain, ys_t)


def _lane_row(vec, offset):
    row = jnp.zeros((LANES,), F32).at[offset:offset + vec.shape[0]].set(vec.astype(F32))
    return jnp.zeros((SUBLANES, LANES), F32).at[0].set(row)


def kernel(x, mem, norm_mix, w_in, dn_conv, dn_a_log, dn_dt_bias, dn_norm, rg_conv, rg_conv_b, rg_w_a, rg_b_a, rg_w_x, rg_b_x, rg_lambda, w_out, norm_cross, norm_mem, w_cq, w_ckv, w_co, norm_moe, w_router, b_router, w_gate, b_gate, w_up, b_up, w_down, b_down, norm_final):
    batch, seq, d = x.shape
    mem_len = mem.shape[1]
    n = batch * seq
    assert w_in.shape[0] == 1, "single-layer trunk"
    x2 = x.reshape(n, d)

    wi = w_in[0]
    n_gate = 2 * DN_HEADS
    w_cat = jnp.concatenate(
        [wi[:, :4 * D_DN], wi[:, 4 * D_DN + n_gate:],
         jnp.pad(wi[:, 4 * D_DN:4 * D_DN + n_gate], ((0, 0), (0, LANES - n_gate)))],
        axis=1).astype(BF16)
    wbat = wi[:, 4 * D_DN:4 * D_DN + n_gate].T.astype(BF16)
    prow = (jnp.zeros((SUBLANES, LANES), F32)
            .at[0, DN_HEADS:n_gate].set(dn_a_log[0]).at[1, DN_HEADS:n_gate].set(dn_dt_bias[0]))
    pcol = (jnp.zeros((n_gate, LANES), F32)
            .at[DN_HEADS:, 0].set(dn_a_log[0]).at[DN_HEADS:, 1].set(dn_dt_bias[0]))

    proj, gbt = _in_proj(x2, norm_mix, w_cat, wbat, prow, pcol)
    dn = _deltanet(proj, gbt, dn_conv[0], dn_norm, batch, seq)
    rg = _rglru(proj, rg_conv[0], rg_conv_b, rg_w_a[0].astype(BF16), rg_b_a[0].reshape(1, D_RG),
                rg_w_x[0].astype(BF16), rg_b_x[0].reshape(1, D_RG), rg_lambda, batch, seq)
    kv = _mem_kv(mem.reshape(batch * mem_len, d), norm_mem, w_ckv[0].astype(BF16))

    w_r = jnp.pad(w_router[0], ((0, 0), (0, LANES - N_EXPERTS)))
    b_r = jnp.pad(b_router, ((0, 0), (0, LANES - N_EXPERTS)))
    h2, xn, route, counts = _post_mix(
        dn, rg, x2, w_out[0].astype(BF16), norm_cross, w_cq[0].astype(BF16), kv,
        w_co[0].astype(BF16), norm_moe, w_r, b_r, seq, mem_len)

    n_blocks = n * TOP_K // MOE_BM + N_EXPERTS
    n_rows = n_blocks * MOE_BM
    cnt = counts[0, :N_EXPERTS].astype(I32)
    padded = (cnt + MOE_BM - 1) // MOE_BM * MOE_BM
    pad_end = jnp.cumsum(padded)
    pad_start = pad_end - padded
    n_used = (pad_end[-1:] // MOE_BM).astype(I32)
    block_e = jnp.minimum(
        jnp.sum(pad_end[None, :] <= (jnp.arange(n_blocks, dtype=I32) * MOE_BM)[:, None], axis=1),
        N_EXPERTS - 1).astype(I32)
    meta = jnp.concatenate([pad_start + cnt, padded - cnt, n_used]).astype(I32)

    dest = _dest(route, _lane_row(pad_start, 0))
    dest4 = dest[:, :TOP_K]
    xs = _dispatch(meta, dest4.reshape(n // 256, 1, 256 * TOP_K), xn, n_rows, d, tm=256)
    ys = _experts(block_e, n_used, xs, w_gate[0], b_gate[0][:, None, :], w_up[0],
                  b_up[0][:, None, :], w_down[0], b_down[0][:, None, :])
    out = _combine(dest4.reshape(n // 128, 1, 128 * TOP_K), h2, route,
                   norm_final.reshape(1, d), ys, tm=128)
    return out.reshape(batch, seq, d)
```

```python
import functools

import jax
import jax.numpy as jnp
from jax import lax
from jax.experimental import pallas as pl
from jax.experimental.pallas import tpu as pltpu

F32 = jnp.float32
BF16 = jnp.bfloat16
I32 = jnp.int32

EPS = 1e-6
LANES = 128
SUBLANES = 8
VMEM_LIMIT = 48 * 1024 * 1024

DN_HEADS = 4
DN_HEAD_DIM = 128
D_DN = DN_HEADS * DN_HEAD_DIM
D_RG = 512
RG_BLOCKS = 4
RG_C = 8.0
CONV_WIDTH = 4
XA_HEADS = 4
N_EXPERTS = 32
TOP_K = 4
SWIGLU_LIMIT = 7.0
SWIGLU_ALPHA = 1.702

DN_CHUNK = 128
INV_BASE = 16
MOE_BM = 256

COL_RX = 4 * D_DN
COL_RY = COL_RX + D_RG
COL_GB = COL_RY + D_RG
PROJ_W = COL_GB + LANES


def _cparams(n_axes=1):
    return pltpu.CompilerParams(
        dimension_semantics=("arbitrary",) * n_axes, vmem_limit_bytes=VMEM_LIMIT)


def _mm(a, b):
    return jnp.dot(a.astype(BF16), b.astype(BF16), preferred_element_type=F32)


def _mm_nt(a, b):
    return lax.dot_general(a.astype(BF16), b.astype(BF16), (((1,), (1,)), ((), ())),
                           preferred_element_type=F32)


def _rmsnorm(x, g):
    return x * lax.rsqrt(jnp.mean(x * x, axis=-1, keepdims=True) + EPS) * g


def _sigmoid(x):
    return 0.5 * jnp.tanh(0.5 * x) + 0.5


def _mm_split(a, b_hi, b_lo):
    a_hi = a.astype(BF16)
    a_lo = (a - a_hi.astype(F32)).astype(BF16)
    return (jnp.dot(a_hi, b_hi, preferred_element_type=F32)
            + jnp.dot(a_hi, b_lo, preferred_element_type=F32)
            + jnp.dot(a_lo, b_hi, preferred_element_type=F32))


def _softplus(x):
    return jnp.maximum(x, 0.0) + jnp.log1p(jnp.exp(-jnp.abs(x)))


def _load_row_tiles(ref, rows, d):
    rt = d // LANES
    return jnp.concatenate([ref[pl.ds(s, rows, stride=rt), :] for s in range(rt)], axis=1)


def _store_row_tiles(ref, val):
    rows, d = val.shape
    rt = d // LANES
    for s in range(rt):
        ref[pl.ds(s, rows, stride=rt), :] = val[:, s * LANES:(s + 1) * LANES]


def _in_proj_kernel(x_ref, g_ref, w_ref, wbat_ref, prow_ref, pcol_ref, proj_ref, gbt_ref):
    u = _rmsnorm(x_ref[...], g_ref[...]).astype(BF16)
    for c0 in range(0, COL_GB, 512):
        proj_ref[:, c0:c0 + 512] = jnp.dot(u, w_ref[:, c0:c0 + 512], preferred_element_type=F32)
    ba = jnp.dot(u, w_ref[:, COL_GB:PROJ_W], preferred_element_type=F32)
    lane = lax.broadcasted_iota(I32, ba.shape, 1)
    g = -jnp.exp(prow_ref[0:1, :]) * _softplus(ba + prow_ref[1:2, :])
    proj_ref[:, COL_GB:PROJ_W] = jnp.where(lane < DN_HEADS, _sigmoid(ba), g)
    bat = lax.dot_general(wbat_ref[...], u, (((1,), (1,)), ((), ())), preferred_element_type=F32)
    row = lax.broadcasted_iota(I32, bat.shape, 0)
    gt = -jnp.exp(pcol_ref[:, 0:1]) * _softplus(bat + pcol_ref[:, 1:2])
    gbt_ref[...] = jnp.where(row < DN_HEADS, _sigmoid(bat), gt)


def _in_proj(x2, gain, w_cat, wbat, prow, pcol, tm=512):
    n, d = x2.shape
    return pl.pallas_call(
        _in_proj_kernel,
        grid=(n // tm,),
        in_specs=[
            pl.BlockSpec((tm, d), lambda i: (i, 0)),
            pl.BlockSpec((1, d), lambda i: (0, 0)),
            pl.BlockSpec((d, PROJ_W), lambda i: (0, 0)),
            pl.BlockSpec((2 * DN_HEADS, d), lambda i: (0, 0)),
            pl.BlockSpec((SUBLANES, LANES), lambda i: (0, 0)),
            pl.BlockSpec((2 * DN_HEADS, LANES), lambda i: (0, 0)),
        ],
        out_specs=[
            pl.BlockSpec((tm, PROJ_W), lambda i: (i, 0)),
            pl.BlockSpec((2 * DN_HEADS, tm), lambda i: (0, i)),
        ],
        out_shape=[
            jax.ShapeDtypeStruct((n, PROJ_W), F32),
            jax.ShapeDtypeStruct((2 * DN_HEADS, n), F32),
        ],
        compiler_params=_cparams(1),
        name="in_proj",
    )(x2, gain, w_cat, wbat, prow, pcol)


def _deltanet_kernel(q_ref, k_ref, v_ref, z_ref, gb_ref, gbt_ref, conv_ref, norm_ref, o_ref,
                     s_ref, halo_ref, ext_ref, act_ref, gcc_ref, gcr_ref,
                     a_ref, p_ref, d_ref, qk_ref, rhs_ref, u_ref, wq_ref, kdt_ref, *, tb):
    t = pl.program_id(1)
    c = DN_CHUNK
    dh = DN_HEAD_DIM

    @pl.when(t == 0)
    def _():
        s_ref[...] = jnp.zeros_like(s_ref)
        halo_ref[...] = jnp.zeros_like(halo_ref)

    ext_ref[0:SUBLANES, :] = halo_ref[...]
    ext_ref[SUBLANES:, 0:D_DN] = q_ref[...]
    ext_ref[SUBLANES:, D_DN:2 * D_DN] = k_ref[...]
    ext_ref[SUBLANES:, 2 * D_DN:] = v_ref[...]
    halo_ref[...] = ext_ref[tb:tb + SUBLANES, :]
    for grp in range(3 * DN_HEADS):
        cs = slice(grp * dh, (grp + 1) * dh)
        y = jnp.zeros((tb, dh), F32)
        for j in range(CONV_WIDTH):
            off = SUBLANES - (CONV_WIDTH - 1) + j
            y = y + conv_ref[j:j + 1, cs] * ext_ref[off:off + tb, cs]
        y = y * _sigmoid(y)
        if grp < 2 * DN_HEADS:
            y = y * lax.rsqrt(jnp.sum(y * y, axis=-1, keepdims=True) + EPS)
        if grp < DN_HEADS:
            y = y * (dh ** -0.5)
        act_ref[:, cs] = y

    gcol = gb_ref[...]
    rpos = lax.broadcasted_iota(I32, gcol.shape, 0) & (c - 1)
    d = 1
    while d < c:
        gcol = gcol + jnp.where(rpos >= d, pltpu.roll(gcol, d, 0), 0.0)
        d *= 2
    gcc_ref[...] = gcol
    grow = gbt_ref[...]
    lpos = lax.broadcasted_iota(I32, grow.shape, 1) & (c - 1)
    d = 1
    while d < c:
        grow = grow + jnp.where(lpos >= d, pltpu.roll(grow, d, 1), 0.0)
        d *= 2
    gcr_ref[...] = grow

    row = lax.broadcasted_iota(I32, (c, c), 0)
    col = lax.broadcasted_iota(I32, (c, c), 1)
    causal = row >= col
    strict = row > col
    eye = jnp.where(row == col, 1.0, 0.0)
    gain = norm_ref[...]
    n_chunks = tb // c
    probs = [(ci, h) for ci in range(n_chunks) for h in range(DN_HEADS)]

    decay_last = []
    for p, (ci, h) in enumerate(probs):
        rows = slice(ci * c, (ci + 1) * c)
        q = act_ref[rows, h * dh:(h + 1) * dh]
        k = act_ref[rows, D_DN + h * dh:D_DN + (h + 1) * dh]
        v = act_ref[rows, 2 * D_DN + h * dh:2 * D_DN + (h + 1) * dh]
        beta = gb_ref[rows, h:h + 1]
        gc = gcc_ref[rows, DN_HEADS + h:DN_HEADS + h + 1]
        gr = gcr_ref[DN_HEADS + h:DN_HEADS + h + 1, rows]
        g_last = gc[c - 1:c, :]
        decay = jnp.where(causal, jnp.exp(jnp.where(causal, gc - gr, 0.0)), 0.0)
        kb = k * beta
        both = _mm_nt(jnp.concatenate([kb, q], axis=0), k)
        a_ref[p] = jnp.where(strict, both[:c] * decay, 0.0)
        qk_ref[p] = (both[c:] * decay).astype(BF16)
        egc = jnp.exp(gc)
        rhs_ref[p] = jnp.concatenate([v * beta, kb * egc], axis=1).astype(BF16)
        wq_ref[p, c:, :] = (q * egc).astype(BF16)
        kdt_ref[p] = (k * jnp.exp(g_last - gc)).T.astype(BF16)
        decay_last.append(jnp.exp(g_last))

    shift = INV_BASE.bit_length() - 1
    blk = (row >> shift) == (col >> shift)
    for p in range(len(probs)):
        diag = jnp.where(blk, a_ref[p], 0.0)
        p_ref[p] = eye - diag
        d_ref[p] = _mm(diag, diag).astype(BF16)
    for it in range(shift - 1):
        for p in range(len(probs)):
            pw = d_ref[p]
            inv = p_ref[p]
            p_ref[p] = inv + _mm(inv, pw)
            if it < shift - 2:
                d_ref[p] = _mm(pw, pw).astype(BF16)
    s = INV_BASE
    while s < c:
        sh = s.bit_length() - 1
        off = ((row >> (sh + 1)) == (col >> (sh + 1))) & ((row >> sh) != (col >> sh))
        for p in range(len(probs)):
            d_ref[p] = _mm(p_ref[p], jnp.where(off, a_ref[p], 0.0)).astype(BF16)
        for p in range(len(probs)):
            inv = p_ref[p]
            p_ref[p] = inv - _mm(d_ref[p], inv)
        s *= 2
    for p in range(len(probs)):
        uw = _mm(p_ref[p], rhs_ref[p])
        u_ref[p] = uw[:, :dh]
        wq_ref[p, :c, :] = uw[:, dh:].astype(BF16)

    for ci in range(n_chunks):
        rows = slice(ci * c, (ci + 1) * c)
        ps = [ci * DN_HEADS + h for h in range(DN_HEADS)]
        s_old = [s_ref[h] for h in range(DN_HEADS)]
        ws = [_mm(wq_ref[p], s_old[h]) for h, p in enumerate(ps)]
        v_new = [u_ref[p] - ws[h][:c] for h, p in enumerate(ps)]
        outs = [ws[h][c:] + _mm(qk_ref[p], v_new[h]) for h, p in enumerate(ps)]
        for h, p in enumerate(ps):
            s_ref[h] = s_old[h] * decay_last[p] + _mm(kdt_ref[p], v_new[h])
        for h in range(DN_HEADS):
            o = _rmsnorm(outs[h], gain)
            zz = z_ref[rows, h * dh:(h + 1) * dh]
            o_ref[rows, h * dh:(h + 1) * dh] = (o * (zz * _sigmoid(zz))).astype(o_ref.dtype)


def _deltanet(proj, gbt, dn_conv, dn_norm, batch, seq, tb=512):
    n = proj.shape[0]
    nt = seq // tb
    c = DN_CHUNK
    n_prob = (tb // c) * DN_HEADS
    blk = lambda j: pl.BlockSpec((tb, D_DN), lambda b, t, j=j: (b * nt + t, j))
    return pl.pallas_call(
        functools.partial(_deltanet_kernel, tb=tb),
        grid=(batch, nt),
        in_specs=[
            blk(0), blk(1), blk(2), blk(3),
            pl.BlockSpec((tb, LANES), lambda b, t: (b * nt + t, COL_GB // LANES)),
            pl.BlockSpec((2 * DN_HEADS, tb), lambda b, t: (0, b * nt + t)),
            pl.BlockSpec((CONV_WIDTH, 3 * D_DN), lambda b, t: (0, 0)),
            pl.BlockSpec((1, DN_HEAD_DIM), lambda b, t: (0, 0)),
        ],
        out_specs=pl.BlockSpec((tb, D_DN), lambda b, t: (b * nt + t, 0)),
        out_shape=jax.ShapeDtypeStruct((n, D_DN), BF16),
        scratch_shapes=[
            pltpu.VMEM((DN_HEADS, DN_HEAD_DIM, DN_HEAD_DIM), F32),
            pltpu.VMEM((SUBLANES, 3 * D_DN), F32),
            pltpu.VMEM((tb + SUBLANES, 3 * D_DN), F32),
            pltpu.VMEM((tb, 3 * D_DN), F32),
            pltpu.VMEM((tb, LANES), F32),
            pltpu.VMEM((2 * DN_HEADS, tb), F32),
            pltpu.VMEM((n_prob, c, c), F32),
            pltpu.VMEM((n_prob, c, c), F32),
            pltpu.VMEM((n_prob, c, c), BF16),
            pltpu.VMEM((n_prob, c, c), BF16),
            pltpu.VMEM((n_prob, c, 2 * DN_HEAD_DIM), BF16),
            pltpu.VMEM((n_prob, c, DN_HEAD_DIM), F32),
            pltpu.VMEM((n_prob, 2 * c, DN_HEAD_DIM), BF16),
            pltpu.VMEM((n_prob, DN_HEAD_DIM, c), BF16),
        ],
        compiler_params=_cparams(2),
        name="deltanet",
    )(proj, proj, proj, proj, proj, gbt, dn_conv, dn_norm)


def _gelu_tanh(x):
    return 0.5 * x * (1.0 + jnp.tanh(0.7978845608028654 * (x + 0.044715 * (x * x * x))))


def _rglru_kernel(rx_ref, ry_ref, conv_ref, convb_ref, wa_ref, ba_ref, wx_ref, bx_ref, lam_ref,
                  o_ref, hc_ref, halo_ref, ext_ref, a_ref, b_ref, h_ref, *, tb):
    t = pl.program_id(1)
    bw = D_RG // RG_BLOCKS

    @pl.when(t == 0)
    def _():
        hc_ref[...] = jnp.zeros_like(hc_ref)
        halo_ref[...] = jnp.zeros_like(halo_ref)

    ext_ref[0:SUBLANES, :] = halo_ref[...]
    ext_ref[SUBLANES:, :] = rx_ref[...]
    halo_ref[...] = ext_ref[tb:tb + SUBLANES, :]
    xr = jnp.zeros((tb, D_RG), F32) + convb_ref[...]
    for j in range(CONV_WIDTH):
        off = SUBLANES - (CONV_WIDTH - 1) + j
        xr = xr + conv_ref[j:j + 1, :] * ext_ref[off:off + tb, :]

    lam = lam_ref[...]
    log_sig = -_softplus(-lam)
    rowi = lax.broadcasted_iota(I32, (tb, bw), 0)
    seq_start_row = jnp.where(t == 0, 0, -1)
    for nb in range(RG_BLOCKS):
        cs = slice(nb * bw, (nb + 1) * bw)
        xb = xr[:, cs]
        r = _sigmoid(_mm(xb, wa_ref[nb]) + ba_ref[:, cs])
        gi = _sigmoid(_mm(xb, wx_ref[nb]) + bx_ref[:, cs])
        log_a = RG_C * r * log_sig[:, cs]
        a = jnp.exp(log_a)
        mult = jnp.sqrt(jnp.tanh(-log_a) * (1.0 + a * a))
        mult = jnp.where(rowi == seq_start_row, 1.0, mult)
        bx = mult * (gi * xb)
        rpos = rowi & (SUBLANES - 1)
        d = 1
        while d < SUBLANES:
            m = rpos >= d
            bx = jnp.where(m, a * pltpu.roll(bx, d, 0) + bx, bx)
            a = jnp.where(m, a * pltpu.roll(a, d, 0), a)
            d *= 2
        a_ref[:, cs] = a
        b_ref[:, cs] = bx

    def group(i, h_prev):
        rows = pl.ds(pl.multiple_of(i * SUBLANES, SUBLANES), SUBLANES)
        ht = a_ref[rows, :] * h_prev + b_ref[rows, :]
        h_ref[rows, :] = ht
        return ht[SUBLANES - 1:SUBLANES, :]

    hc_ref[...] = lax.fori_loop(0, tb // SUBLANES, group, hc_ref[...], unroll=8)
    o_ref[...] = (h_ref[...] * _gelu_tanh(ry_ref[...])).astype(o_ref.dtype)


def _rglru(proj, rg_conv, rg_conv_b, w_a, b_a, w_x, b_x, lam, batch, seq, tb=512):
    n = proj.shape[0]
    nt = seq // tb
    bw = D_RG // RG_BLOCKS
    full = lambda shape: pl.BlockSpec(shape, lambda b, t: (0,) * len(shape))
    return pl.pallas_call(
        functools.partial(_rglru_kernel, tb=tb),
        grid=(batch, nt),
        in_specs=[
            pl.BlockSpec((tb, D_RG), lambda b, t: (b * nt + t, COL_RX // D_RG)),
            pl.BlockSpec((tb, D_RG), lambda b, t: (b * nt + t, COL_RY // D_RG)),
            full((CONV_WIDTH, D_RG)), full((1, D_RG)),
            full((RG_BLOCKS, bw, bw)), full((1, D_RG)),
            full((RG_BLOCKS, bw, bw)), full((1, D_RG)),
            full((1, D_RG)),
        ],
        out_specs=pl.BlockSpec((tb, D_RG), lambda b, t: (b * nt + t, 0)),
        out_shape=jax.ShapeDtypeStruct((n, D_RG), BF16),
        scratch_shapes=[
            pltpu.VMEM((1, D_RG), F32),
            pltpu.VMEM((SUBLANES, D_RG), F32),
            pltpu.VMEM((tb + SUBLANES, D_RG), F32),
            pltpu.VMEM((tb, D_RG), F32),
            pltpu.VMEM((tb, D_RG), F32),
            pltpu.VMEM((tb, D_RG), F32),
        ],
        compiler_params=_cparams(2),
        name="rglru",
    )(proj, proj, rg_conv, rg_conv_b, w_a, b_a, w_x, b_x, lam)


def _mem_kv_kernel(m_ref, g_ref, w_ref, o_ref):
    mn = _rmsnorm(m_ref[...], g_ref[...]).astype(BF16)
    o_ref[...] = jnp.dot(mn, w_ref[...], preferred_element_type=F32).astype(o_ref.dtype)


def _mem_kv(mem2, gain, w_ckv, tm=256):
    n, d = mem2.shape
    return pl.pallas_call(
        _mem_kv_kernel,
        grid=(n // tm,),
        in_specs=[
            pl.BlockSpec((tm, d), lambda i: (i, 0)),
            pl.BlockSpec((1, d), lambda i: (0, 0)),
            pl.BlockSpec((d, 2 * d), lambda i: (0, 0)),
        ],
        out_specs=pl.BlockSpec((tm, 2 * d), lambda i: (i, 0)),
        out_shape=jax.ShapeDtypeStruct((n, 2 * d), BF16),
        compiler_params=_cparams(1),
        name="mem_kv",
    )(mem2, gain, w_ckv)


ROUTE_E = 0
ROUTE_RANK = TOP_K
ROUTE_GATE = 2 * TOP_K


def _post_mix_kernel(dn_ref, rg_ref, x_ref, wo_ref, gx_ref, wq_ref, kv_ref, wco_ref, gm_ref,
                     wrh_ref, wrl_ref, br_ref, h2_ref, xn_ref, route_ref, cnt_ref, carry_ref):
    i = pl.program_id(0)
    tm, d = x_ref.shape
    hd = d // XA_HEADS

    @pl.when(i == 0)
    def _():
        carry_ref[...] = jnp.zeros_like(carry_ref)

    h1 = (x_ref[...] + jnp.dot(dn_ref[...], wo_ref[0:D_DN, :], preferred_element_type=F32)
          + jnp.dot(rg_ref[...], wo_ref[D_DN:, :], preferred_element_type=F32))

    hn = _rmsnorm(h1, gx_ref[...]).astype(BF16)
    q = jnp.dot(hn, wq_ref[...], preferred_element_type=F32)
    heads = []
    for hh in range(XA_HEADS):
        cs = slice(hh * hd, (hh + 1) * hd)
        s = _mm_nt(q[:, cs], kv_ref[:, cs]) * (hd ** -0.5)
        p = jnp.exp(s - jnp.max(s, axis=-1, keepdims=True))
        p = p / jnp.sum(p, axis=-1, keepdims=True)
        heads.append(_mm(p, kv_ref[:, d + hh * hd:d + (hh + 1) * hd]).astype(BF16))
    o = jnp.concatenate(heads, axis=1)
    h2 = h1 + jnp.dot(o, wco_ref[...], preferred_element_type=F32)
    h2_ref[...] = h2

    xn = _rmsnorm(h2, gm_ref[...])
    _store_row_tiles(xn_ref, xn)
    logits = _mm_split(xn, wrh_ref[...], wrl_ref[...]) + br_ref[...]
    lane = lax.broadcasted_iota(I32, logits.shape, 1)
    lane_f = lane.astype(F32)
    neg = jnp.float32(-jnp.inf)
    lg = jnp.where(lane < N_EXPERTS, logits, neg)
    vals, idxs, hots = [], [], []
    for _ in range(TOP_K):
        m = jnp.max(lg, axis=-1, keepdims=True)
        idx = jnp.min(jnp.where(lg == m, lane_f, float(LANES)), axis=-1, keepdims=True)
        hot = lane_f == idx
        lg = jnp.where(hot, neg, lg)
        vals.append(m)
        idxs.append(idx)
        hots.append(hot)
    es = [jnp.exp(v - vals[0]) for v in vals]
    den = es[0] + es[1] + es[2] + es[3]
    gates = [e / den for e in es]

    chosen = jnp.zeros(logits.shape, F32)
    for hot in hots:
        chosen = chosen + jnp.where(hot, 1.0, 0.0)
    r2 = lax.broadcasted_iota(I32, (tm, tm), 0)
    c2 = lax.broadcasted_iota(I32, (tm, tm), 1)
    before = _mm(jnp.where(r2 > c2, 1.0, 0.0), chosen) + carry_ref[0:1, :]
    ranks = [jnp.sum(jnp.where(hot, before, 0.0), axis=-1, keepdims=True) for hot in hots]
    carry_ref[...] = carry_ref[...] + jnp.sum(chosen, axis=0, keepdims=True)
    cnt_ref[...] = carry_ref[...]

    rec = jnp.zeros(logits.shape, F32)
    for kk in range(TOP_K):
        rec = jnp.where(lane == ROUTE_E + kk, idxs[kk], rec)
        rec = jnp.where(lane == ROUTE_RANK + kk, ranks[kk], rec)
        rec = jnp.where(lane == ROUTE_GATE + kk, gates[kk], rec)
    route_ref[...] = rec


def _post_mix(dn, rg, x2, w_out, g_cross, w_cq, kv, w_co, g_moe, w_r_hi, w_r_lo, b_r, seq, mem_len,
              tm=512):
    n, d = x2.shape
    per_b = seq // tm
    full = lambda shape: pl.BlockSpec(shape, lambda i: (0,) * len(shape))
    return pl.pallas_call(
        _post_mix_kernel,
        grid=(n // tm,),
        in_specs=[
            pl.BlockSpec((tm, D_DN), lambda i: (i, 0)),
            pl.BlockSpec((tm, D_RG), lambda i: (i, 0)),
            pl.BlockSpec((tm, d), lambda i: (i, 0)),
            full((d, d)), full((1, d)), full((d, d)),
            pl.BlockSpec((mem_len, 2 * d), lambda i: (i // per_b, 0)),
            full((d, d)), full((1, d)),
            full((d, LANES)), full((d, LANES)), full((1, LANES)),
        ],
        out_specs=[
            pl.BlockSpec((tm, d), lambda i: (i, 0)),
            pl.BlockSpec((tm * (d // LANES), LANES), lambda i: (i, 0)),
            pl.BlockSpec((tm, LANES), lambda i: (i, 0)),
            pl.BlockSpec((SUBLANES, LANES), lambda i: (0, 0)),
        ],
        out_shape=[
            jax.ShapeDtypeStruct((n, d), F32),
            jax.ShapeDtypeStruct((n * (d // LANES), LANES), F32),
            jax.ShapeDtypeStruct((n, LANES), F32),
            jax.ShapeDtypeStruct((SUBLANES, LANES), F32),
        ],
        scratch_shapes=[pltpu.VMEM((SUBLANES, LANES), F32)],
        compiler_params=_cparams(1),
        name="post_mix",
    )(dn, rg, x2, w_out, g_cross, w_cq, kv, w_co, g_moe, w_r_hi, w_r_lo, b_r)


def _dest_kernel(route_ref, start_ref, dest_ref):
    rec = route_ref[...]
    lane = lax.broadcasted_iota(I32, rec.shape, 1)
    lane_f = lane.astype(F32)
    start = start_ref[0:1, :]
    out = jnp.zeros(rec.shape, F32)
    for kk in range(TOP_K):
        e = rec[:, ROUTE_E + kk:ROUTE_E + kk + 1]
        base = jnp.sum(jnp.where(lane_f == e, start, 0.0), axis=-1, keepdims=True)
        out = jnp.where(lane == kk, base + rec[:, ROUTE_RANK + kk:ROUTE_RANK + kk + 1], out)
    dest_ref[...] = out.astype(I32)


def _dest(route, start_row, tm=2048):
    n = route.shape[0]
    return pl.pallas_call(
        _dest_kernel,
        grid=(n // tm,),
        in_specs=[
            pl.BlockSpec((tm, LANES), lambda i: (i, 0)),
            pl.BlockSpec((SUBLANES, LANES), lambda i: (0, 0)),
        ],
        out_specs=pl.BlockSpec((tm, LANES), lambda i: (i, 0)),
        out_shape=jax.ShapeDtypeStruct((n, LANES), I32),
        compiler_params=_cparams(1),
        name="dest",
    )(route, start_row)


def _row_tile(ref, r, rt):
    return ref.at[pl.ds(pl.multiple_of(r * rt, rt), rt)]


def _dispatch_kernel(meta_ref, dest_ref, xn_ref, xs_ref, zero_ref, sem, zsem, *, tm, rt, n_blocks):
    i = pl.program_id(0)
    bm = MOE_BM * rt

    def per_token(tk, carry):
        src = _row_tile(xn_ref, tk, rt)
        for kk in range(TOP_K):
            r = dest_ref[0, 0, tk * TOP_K + kk]
            pltpu.make_async_copy(src, _row_tile(xs_ref, r, rt), sem).start(priority=kk % 2)
        return carry

    lax.fori_loop(0, tm, per_token, 0, unroll=4)

    @pl.when(i == 0)
    def _():
        zero_ref[...] = jnp.zeros_like(zero_ref)

        def pad_copy(r):
            return pltpu.make_async_copy(zero_ref.at[pl.ds(0, rt)], _row_tile(xs_ref, r, rt), zsem)

        def per_expert(e, total):
            r0 = meta_ref[e]
            cnt = meta_ref[N_EXPERTS + e]

            def one(r, carry):
                pad_copy(r0 + r).start()
                return carry

            lax.fori_loop(0, cnt, one, 0)
            return total + cnt

        total = lax.fori_loop(0, N_EXPERTS, per_expert, 0)

        def drain(r, carry):
            pad_copy(0).wait()
            return carry

        lax.fori_loop(0, total, drain, 0)

        def tail_copy(b):
            return pltpu.make_async_copy(
                zero_ref, xs_ref.at[pl.ds(pl.multiple_of(b * bm, bm), bm)], zsem)

        n_used = meta_ref[2 * N_EXPERTS]

        def tail(b, carry):
            tail_copy(b).start()
            return carry

        lax.fori_loop(n_used, n_blocks, tail, 0)

        def tail_drain(b, carry):
            tail_copy(0).wait()
            return carry

        lax.fori_loop(n_used, n_blocks, tail_drain, 0)

    for _ in range(TOP_K):
        pltpu.make_async_copy(xn_ref, xs_ref.at[pl.ds(0, tm * rt)], sem).wait()


def _dispatch(meta, dest3, xn_t, n_rows, d, tm=256):
    rt = d // LANES
    n = xn_t.shape[0] // rt
    n_blocks = n_rows // MOE_BM
    grid_spec = pltpu.PrefetchScalarGridSpec(
        num_scalar_prefetch=1,
        grid=(n // tm,),
        in_specs=[
            pl.BlockSpec((1, 1, tm * TOP_K), lambda i, meta: (i, 0, 0), memory_space=pltpu.SMEM),
            pl.BlockSpec((tm * rt, LANES), lambda i, meta: (i, 0)),
        ],
        out_specs=pl.BlockSpec(memory_space=pl.ANY),
        scratch_shapes=[
            pltpu.VMEM((MOE_BM * rt, LANES), F32),
            pltpu.SemaphoreType.DMA,
            pltpu.SemaphoreType.DMA,
        ],
    )
    return pl.pallas_call(
        functools.partial(_dispatch_kernel, tm=tm, rt=rt, n_blocks=n_blocks),
        grid_spec=grid_spec,
        out_shape=jax.ShapeDtypeStruct((n_rows * rt, LANES), F32),
        compiler_params=_cparams(1),
        name="dispatch",
    )(meta, dest3, xn_t)


def _experts_kernel(be_ref, nu_ref, nxt_ref, par_ref, x_ref, bg_ref, bu_ref, bd_ref,
                    wg_hbm, wu_hbm, wd_hbm, y_ref, wf_ref, wb_ref, sem):
    i = pl.program_id(0)
    used = i < nu_ref[0]
    e = be_ref[i]
    changed = (i == 0) | (e != be_ref[jnp.maximum(i - 1, 0)])

    def weight_copies(expert, slot):
        return [pltpu.make_async_copy(w.at[expert], wf_ref.at[slot, j], sem.at[slot])
                for j, w in enumerate((wg_hbm, wu_hbm, wd_hbm))]

    @pl.when(i == 0)
    def _():
        for cp in weight_copies(e, par_ref[e]):
            cp.start()

    @pl.when(used & changed)
    def _():
        slot = par_ref[e]
        for cp in weight_copies(e, slot):
            cp.wait()
        nxt = nxt_ref[e]

        @pl.when(nxt < N_EXPERTS)
        def _():
            for cp in weight_copies(nxt, 1 - slot):
                cp.start()

        for j in range(3):
            wb_ref[j] = wf_ref[slot, j].astype(BF16)

    @pl.when(used)
    def _():
        d = wb_ref.shape[1]
        x = _load_row_tiles(x_ref, MOE_BM, d).astype(BF16)
        gt = jnp.minimum(jnp.dot(x, wb_ref[0], preferred_element_type=F32) + bg_ref[...],
                         SWIGLU_LIMIT)
        up = jnp.clip(jnp.dot(x, wb_ref[1], preferred_element_type=F32) + bu_ref[...],
                      -SWIGLU_LIMIT, SWIGLU_LIMIT)
        hid = (up + 1.0) * (gt * _sigmoid(SWIGLU_ALPHA * gt))
        _store_row_tiles(y_ref, jnp.dot(hid.astype(BF16), wb_ref[2],
                                        preferred_element_type=F32) + bd_ref[...])

    @pl.when(jnp.logical_not(used))
    def _():
        y_ref[...] = jnp.zeros_like(y_ref)


def _experts(block_e, n_used, next_e, parity, xs_t, w_gate, b_gate, w_up, b_up, w_down, b_down):
    d, d_ff = w_gate.shape[1:]
    assert d == d_ff, "weight staging buffers assume square expert matrices"
    rt = d // LANES
    n_blocks = xs_t.shape[0] // (MOE_BM * rt)
    bspec = lambda m: pl.BlockSpec((None, 1, m), lambda i, be, nu, nx, pa: (be[i], 0, 0))
    hbm = pl.BlockSpec(memory_space=pl.ANY)
    grid_spec = pltpu.PrefetchScalarGridSpec(
        num_scalar_prefetch=4,
        grid=(n_blocks,),
        in_specs=[
            pl.BlockSpec((MOE_BM * rt, LANES),
                         lambda i, be, nu, nx, pa: (jnp.maximum(jnp.minimum(i, nu[0] - 1), 0), 0)),
            bspec(d_ff), bspec(d_ff), bspec(d),
            hbm, hbm, hbm,
        ],
        out_specs=pl.BlockSpec((MOE_BM * rt, LANES), lambda i, be, nu, nx, pa: (i, 0)),
        scratch_shapes=[
            pltpu.VMEM((2, 3, d, d_ff), F32),
            pltpu.VMEM((3, d, d_ff), BF16),
            pltpu.SemaphoreType.DMA((2,)),
        ],
    )
    return pl.pallas_call(
        _experts_kernel,
        grid_spec=grid_spec,
        out_shape=jax.ShapeDtypeStruct(xs_t.shape, F32),
        compiler_params=_cparams(1),
        name="experts",
    )(block_e, n_used, next_e, parity, xs_t, b_gate, b_up, b_down, w_gate, w_up, w_down)


def _combine_kernel(dcur_ref, dnext_ref, h2_ref, route_ref, gain_ref, ys_ref, o_ref,
                    buf_ref, sem, *, tm):
    i = pl.program_id(0)
    nsteps = pl.num_programs(0)
    slot = i & 1
    d = h2_ref.shape[1]
    rt = d // LANES

    def issue_all(dref, s):
        def per_token(tk, carry):
            for kk in range(TOP_K):
                r = dref[0, 0, tk * TOP_K + kk]
                pltpu.make_async_copy(_row_tile(ys_ref, r, rt),
                                      _row_tile(buf_ref.at[s, kk], tk, rt),
                                      sem.at[s]).start(priority=kk % 2)
            return carry
        lax.fori_loop(0, tm, per_token, 0, unroll=4)

    @pl.when(i == 0)
    def _():
        issue_all(dcur_ref, 0)

    @pl.when(i + 1 < nsteps)
    def _():
        issue_all(dnext_ref, 1 - slot)

    for kk in range(TOP_K):
        pltpu.make_async_copy(ys_ref.at[pl.ds(0, tm * rt)], buf_ref.at[slot, kk],
                              sem.at[slot]).wait()

    rec = route_ref[...]
    acc = h2_ref[...]
    for kk in range(TOP_K):
        acc = acc + (rec[:, ROUTE_GATE + kk:ROUTE_GATE + kk + 1]
                     * _load_row_tiles(buf_ref.at[slot, kk], tm, d))
    o_ref[...] = _rmsnorm(acc, gain_ref[...])


def _combine(dest3, h2, route, gain, ys_t, tm=128):
    n, d = h2.shape
    nsteps = n // tm
    return pl.pallas_call(
        functools.partial(_combine_kernel, tm=tm),
        grid=(nsteps,),
        in_specs=[
            pl.BlockSpec((1, 1, tm * TOP_K), lambda i: (i, 0, 0), memory_space=pltpu.SMEM),
            pl.BlockSpec((1, 1, tm * TOP_K), lambda i: (jnp.minimum(i + 1, nsteps - 1), 0, 0),
                         memory_space=pltpu.SMEM),
            pl.BlockSpec((tm, d), lambda i: (i, 0)),
            pl.BlockSpec((tm, LANES), lambda i: (i, 0)),
            pl.BlockSpec((1, d), lambda i: (0, 0)),
            pl.BlockSpec(memory_space=pl.ANY),
        ],
        out_specs=pl.BlockSpec((tm, d), lambda i: (i, 0)),
        out_shape=jax.ShapeDtypeStruct((n, d), F32),
        scratch_shapes=[
            pltpu.VMEM((2, TOP_K, tm * (d // LANES), LANES), F32),
            pltpu.SemaphoreType.DMA((2,)),
        ],
        compiler_params=_cparams(1),
        name="combine",
    )(dest3, dest3, h2, route, gain, ys_t)


def _lane_row(vec, offset):
    row = jnp.zeros((LANES,), F32).at[offset:offset + vec.shape[0]].set(vec.astype(F32))
    return jnp.zeros((SUBLANES, LANES), F32).at[0].set(row)


def kernel(x, mem, norm_mix, w_in, dn_conv, dn_a_log, dn_dt_bias, dn_norm, rg_conv, rg_conv_b, rg_w_a, rg_b_a, rg_w_x, rg_b_x, rg_lambda, w_out, norm_cross, norm_mem, w_cq, w_ckv, w_co, norm_moe, w_router, b_router, w_gate, b_gate, w_up, b_up, w_down, b_down, norm_final):
    batch, seq, d = x.shape
    mem_len = mem.shape[1]
    n = batch * seq
    assert w_in.shape[0] == 1, "single-layer trunk"
    x2 = x.reshape(n, d)

    wi = w_in[0]
    n_gate = 2 * DN_HEADS
    w_cat = jnp.concatenate(
        [wi[:, :4 * D_DN], wi[:, 4 * D_DN + n_gate:],
         jnp.pad(wi[:, 4 * D_DN:4 * D_DN + n_gate], ((0, 0), (0, LANES - n_gate)))],
        axis=1).astype(BF16)
    wbat = wi[:, 4 * D_DN:4 * D_DN + n_gate].T.astype(BF16)
    prow = (jnp.zeros((SUBLANES, LANES), F32)
            .at[0, DN_HEADS:n_gate].set(dn_a_log[0]).at[1, DN_HEADS:n_gate].set(dn_dt_bias[0]))
    pcol = (jnp.zeros((n_gate, LANES), F32)
            .at[DN_HEADS:, 0].set(dn_a_log[0]).at[DN_HEADS:, 1].set(dn_dt_bias[0]))

    proj, gbt = _in_proj(x2, norm_mix, w_cat, wbat, prow, pcol)
    dn = _deltanet(proj, gbt, dn_conv[0], dn_norm, batch, seq)
    rg = _rglru(proj, rg_conv[0], rg_conv_b, rg_w_a[0].astype(BF16), rg_b_a[0].reshape(1, D_RG),
                rg_w_x[0].astype(BF16), rg_b_x[0].reshape(1, D_RG), rg_lambda, batch, seq)
    kv = _mem_kv(mem.reshape(batch * mem_len, d), norm_mem, w_ckv[0].astype(BF16))

    w_r = jnp.pad(w_router[0], ((0, 0), (0, LANES - N_EXPERTS)))
    w_r_hi = w_r.astype(BF16)
    w_r_lo = (w_r - w_r_hi.astype(F32)).astype(BF16)
    b_r = jnp.pad(b_router, ((0, 0), (0, LANES - N_EXPERTS)))
    h2, xn, route, counts = _post_mix(
        dn, rg, x2, w_out[0].astype(BF16), norm_cross, w_cq[0].astype(BF16), kv,
        w_co[0].astype(BF16), norm_moe, w_r_hi, w_r_lo, b_r, seq, mem_len)

    n_blocks = n * TOP_K // MOE_BM + N_EXPERTS
    n_rows = n_blocks * MOE_BM
    cnt = counts[0, :N_EXPERTS].astype(I32)
    padded = (cnt + MOE_BM - 1) // MOE_BM * MOE_BM
    pad_end = jnp.cumsum(padded)
    pad_start = pad_end - padded
    n_used = (pad_end[-1:] // MOE_BM).astype(I32)
    block_e = jnp.minimum(
        jnp.sum(pad_end[None, :] <= (jnp.arange(n_blocks, dtype=I32) * MOE_BM)[:, None], axis=1),
        N_EXPERTS - 1).astype(I32)
    meta = jnp.concatenate([pad_start + cnt, padded - cnt, n_used]).astype(I32)

    dest = _dest(route, _lane_row(pad_start, 0))
    dest4 = dest[:, :TOP_K]
    xs = _dispatch(meta, dest4.reshape(n // 256, 1, 256 * TOP_K), xn, n_rows, d, tm=256)
    has = cnt > 0
    eid = jnp.where(has, jnp.arange(N_EXPERTS, dtype=I32), N_EXPERTS)
    after = lax.cummin(eid, axis=0, reverse=True)
    next_e = jnp.concatenate([after[1:], jnp.full((1,), N_EXPERTS, I32)]).astype(I32)
    parity = ((jnp.cumsum(has.astype(I32)) - 1) & 1).astype(I32)
    ys = _experts(block_e, n_used, next_e, parity, xs, w_gate[0], b_gate[0][:, None, :], w_up[0],
                  b_up[0][:, None, :], w_down[0], b_down[0][:, None, :])
    out = _combine(dest4.reshape(n // 128, 1, 128 * TOP_K), h2, route,
                   norm_final.reshape(1, d), ys, tm=128)
    return out.reshape(batch, seq, d)
```

```python
import functools

import jax
import jax.numpy as jnp
from jax import lax
from jax.experimental import pallas as pl
from jax.experimental.pallas import tpu as pltpu

F32 = jnp.float32
BF16 = jnp.bfloat16
I32 = jnp.int32

EPS = 1e-6
LANES = 128
SUBLANES = 8
VMEM_LIMIT = 48 * 1024 * 1024

DN_HEADS = 4
DN_HEAD_DIM = 128
D_DN = DN_HEADS * DN_HEAD_DIM
D_RG = 512
RG_BLOCKS = 4
RG_C = 8.0
CONV_WIDTH = 4
XA_HEADS = 4
N_EXPERTS = 32
TOP_K = 4
SWIGLU_LIMIT = 7.0
SWIGLU_ALPHA = 1.702

DN_CHUNK = 128
INV_BASE = 16
MOE_BM = 256

COL_RX = 4 * D_DN
COL_RY = COL_RX + D_RG
COL_GB = COL_RY + D_RG
PROJ_W = COL_GB + LANES


def _cparams(n_axes=1):
    return pltpu.CompilerParams(
        dimension_semantics=("arbitrary",) * n_axes, vmem_limit_bytes=VMEM_LIMIT)


def _mm(a, b):
    return jnp.dot(a.astype(BF16), b.astype(BF16), preferred_element_type=F32)


def _mm_nt(a, b):
    return lax.dot_general(a.astype(BF16), b.astype(BF16), (((1,), (1,)), ((), ())),
                           preferred_element_type=F32)


def _rmsnorm(x, g):
    return x * lax.rsqrt(jnp.mean(x * x, axis=-1, keepdims=True) + EPS) * g


def _sigmoid(x):
    return 0.5 * jnp.tanh(0.5 * x) + 0.5


def _mm_split(a, b_hi, b_lo):
    a_hi = a.astype(BF16)
    a_lo = (a - a_hi.astype(F32)).astype(BF16)
    return (jnp.dot(a_hi, b_hi, preferred_element_type=F32)
            + jnp.dot(a_hi, b_lo, preferred_element_type=F32)
            + jnp.dot(a_lo, b_hi, preferred_element_type=F32))


def _softplus(x):
    return jnp.maximum(x, 0.0) + jnp.log1p(jnp.exp(-jnp.abs(x)))


def _load_row_tiles(ref, rows, d):
    rt = d // LANES
    return jnp.concatenate([ref[pl.ds(s, rows, stride=rt), :] for s in range(rt)], axis=1)


def _store_row_tiles(ref, val):
    rows, d = val.shape
    rt = d // LANES
    for s in range(rt):
        ref[pl.ds(s, rows, stride=rt), :] = val[:, s * LANES:(s + 1) * LANES]


def _in_proj_kernel(x_ref, g_ref, w_ref, wbat_ref, prow_ref, pcol_ref, proj_ref, gbt_ref):
    u = _rmsnorm(x_ref[...], g_ref[...]).astype(BF16)
    for c0 in range(0, COL_GB, 512):
        proj_ref[:, c0:c0 + 512] = jnp.dot(u, w_ref[:, c0:c0 + 512], preferred_element_type=F32)
    ba = jnp.dot(u, w_ref[:, COL_GB:PROJ_W], preferred_element_type=F32)
    lane = lax.broadcasted_iota(I32, ba.shape, 1)
    g = -jnp.exp(prow_ref[0:1, :]) * _softplus(ba + prow_ref[1:2, :])
    proj_ref[:, COL_GB:PROJ_W] = jnp.where(lane < DN_HEADS, _sigmoid(ba), g)
    bat = lax.dot_general(wbat_ref[...], u, (((1,), (1,)), ((), ())), preferred_element_type=F32)
    row = lax.broadcasted_iota(I32, bat.shape, 0)
    gt = -jnp.exp(pcol_ref[:, 0:1]) * _softplus(bat + pcol_ref[:, 1:2])
    gbt_ref[...] = jnp.where(row < DN_HEADS, _sigmoid(bat), gt)


def _in_proj(x2, gain, w_cat, wbat, prow, pcol, tm=512):
    n, d = x2.shape
    return pl.pallas_call(
        _in_proj_kernel,
        grid=(n // tm,),
        in_specs=[
            pl.BlockSpec((tm, d), lambda i: (i, 0)),
            pl.BlockSpec((1, d), lambda i: (0, 0)),
            pl.BlockSpec((d, PROJ_W), lambda i: (0, 0)),
            pl.BlockSpec((2 * DN_HEADS, d), lambda i: (0, 0)),
            pl.BlockSpec((SUBLANES, LANES), lambda i: (0, 0)),
            pl.BlockSpec((2 * DN_HEADS, LANES), lambda i: (0, 0)),
        ],
        out_specs=[
            pl.BlockSpec((tm, PROJ_W), lambda i: (i, 0)),
            pl.BlockSpec((2 * DN_HEADS, tm), lambda i: (0, i)),
        ],
        out_shape=[
            jax.ShapeDtypeStruct((n, PROJ_W), F32),
            jax.ShapeDtypeStruct((2 * DN_HEADS, n), F32),
        ],
        compiler_params=_cparams(1),
        name="in_proj",
    )(x2, gain, w_cat, wbat, prow, pcol)


def _deltanet_kernel(q_ref, k_ref, v_ref, z_ref, gb_ref, gbt_ref, conv_ref, norm_ref, o_ref,
                     s_ref, halo_ref, ext_ref, act_ref, gcc_ref, gcr_ref,
                     a_ref, p_ref, d_ref, qk_ref, rhs_ref, u_ref, wq_ref, kdt_ref, *, tb):
    t = pl.program_id(1)
    c = DN_CHUNK
    dh = DN_HEAD_DIM

    @pl.when(t == 0)
    def _():
        s_ref[...] = jnp.zeros_like(s_ref)
        halo_ref[...] = jnp.zeros_like(halo_ref)

    ext_ref[0:SUBLANES, :] = halo_ref[...]
    ext_ref[SUBLANES:, 0:D_DN] = q_ref[...]
    ext_ref[SUBLANES:, D_DN:2 * D_DN] = k_ref[...]
    ext_ref[SUBLANES:, 2 * D_DN:] = v_ref[...]
    halo_ref[...] = ext_ref[tb:tb + SUBLANES, :]
    for grp in range(3 * DN_HEADS):
        cs = slice(grp * dh, (grp + 1) * dh)
        y = jnp.zeros((tb, dh), F32)
        for j in range(CONV_WIDTH):
            off = SUBLANES - (CONV_WIDTH - 1) + j
            y = y + conv_ref[j:j + 1, cs] * ext_ref[off:off + tb, cs]
        y = y * _sigmoid(y)
        if grp < 2 * DN_HEADS:
            y = y * lax.rsqrt(jnp.sum(y * y, axis=-1, keepdims=True) + EPS)
        if grp < DN_HEADS:
            y = y * (dh ** -0.5)
        act_ref[:, cs] = y

    gcol = gb_ref[...]
    rpos = lax.broadcasted_iota(I32, gcol.shape, 0) & (c - 1)
    d = 1
    while d < c:
        gcol = gcol + jnp.where(rpos >= d, pltpu.roll(gcol, d, 0), 0.0)
        d *= 2
    gcc_ref[...] = gcol
    grow = gbt_ref[...]
    lpos = lax.broadcasted_iota(I32, grow.shape, 1) & (c - 1)
    d = 1
    while d < c:
        grow = grow + jnp.where(lpos >= d, pltpu.roll(grow, d, 1), 0.0)
        d *= 2
    gcr_ref[...] = grow

    row = lax.broadcasted_iota(I32, (c, c), 0)
    col = lax.broadcasted_iota(I32, (c, c), 1)
    causal = row >= col
    strict = row > col
    eye = jnp.where(row == col, 1.0, 0.0)
    gain = norm_ref[...]
    n_chunks = tb // c
    probs = [(ci, h) for ci in range(n_chunks) for h in range(DN_HEADS)]

    decay_last = []
    for p, (ci, h) in enumerate(probs):
        rows = slice(ci * c, (ci + 1) * c)
        q = act_ref[rows, h * dh:(h + 1) * dh]
        k = act_ref[rows, D_DN + h * dh:D_DN + (h + 1) * dh]
        v = act_ref[rows, 2 * D_DN + h * dh:2 * D_DN + (h + 1) * dh]
        beta = gb_ref[rows, h:h + 1]
        gc = gcc_ref[rows, DN_HEADS + h:DN_HEADS + h + 1]
        gr = gcr_ref[DN_HEADS + h:DN_HEADS + h + 1, rows]
        g_last = gc[c - 1:c, :]
        decay = jnp.where(causal, jnp.exp(jnp.where(causal, gc - gr, 0.0)), 0.0)
        kb = k * beta
        both = _mm_nt(jnp.concatenate([kb, q], axis=0), k)
        a_ref[p] = jnp.where(strict, both[:c] * decay, 0.0)
        qk_ref[p] = (both[c:] * decay).astype(BF16)
        egc = jnp.exp(gc)
        rhs_ref[p] = jnp.concatenate([v * beta, kb * egc], axis=1).astype(BF16)
        wq_ref[p, c:, :] = (q * egc).astype(BF16)
        kdt_ref[p] = (k * jnp.exp(g_last - gc)).T.astype(BF16)
        decay_last.append(jnp.exp(g_last))

    shift = INV_BASE.bit_length() - 1
    blk = (row >> shift) == (col >> shift)
    for p in range(len(probs)):
        diag = jnp.where(blk, a_ref[p], 0.0)
        p_ref[p] = eye - diag
        d_ref[p] = _mm(diag, diag).astype(BF16)
    for it in range(shift - 1):
        for p in range(len(probs)):
            pw = d_ref[p]
            inv = p_ref[p]
            p_ref[p] = inv + _mm(inv, pw)
            if it < shift - 2:
                d_ref[p] = _mm(pw, pw).astype(BF16)
    s = INV_BASE
    while s < c:
        sh = s.bit_length() - 1
        off = ((row >> (sh + 1)) == (col >> (sh + 1))) & ((row >> sh) != (col >> sh))
        for p in range(len(probs)):
            d_ref[p] = _mm(p_ref[p], jnp.where(off, a_ref[p], 0.0)).astype(BF16)
        for p in range(len(probs)):
            inv = p_ref[p]
            p_ref[p] = inv - _mm(d_ref[p], inv)
        s *= 2
    for p in range(len(probs)):
        uw = _mm(p_ref[p], rhs_ref[p])
        u_ref[p] = uw[:, :dh]
        wq_ref[p, :c, :] = uw[:, dh:].astype(BF16)

    for ci in range(n_chunks):
        rows = slice(ci * c, (ci + 1) * c)
        ps = [ci * DN_HEADS + h for h in range(DN_HEADS)]
        s_old = [s_ref[h] for h in range(DN_HEADS)]
        ws = [_mm(wq_ref[p], s_old[h]) for h, p in enumerate(ps)]
        v_new = [u_ref[p] - ws[h][:c] for h, p in enumerate(ps)]
        outs = [ws[h][c:] + _mm(qk_ref[p], v_new[h]) for h, p in enumerate(ps)]
        for h, p in enumerate(ps):
            s_ref[h] = s_old[h] * decay_last[p] + _mm(kdt_ref[p], v_new[h])
        for h in range(DN_HEADS):
            o = _rmsnorm(outs[h], gain)
            zz = z_ref[rows, h * dh:(h + 1) * dh]
            o_ref[rows, h * dh:(h + 1) * dh] = (o * (zz * _sigmoid(zz))).astype(o_ref.dtype)


def _deltanet(proj, gbt, dn_conv, dn_norm, batch, seq, tb=512):
    n = proj.shape[0]
    nt = seq // tb
    c = DN_CHUNK
    n_prob = (tb // c) * DN_HEADS
    blk = lambda j: pl.BlockSpec((tb, D_DN), lambda b, t, j=j: (b * nt + t, j))
    return pl.pallas_call(
        functools.partial(_deltanet_kernel, tb=tb),
        grid=(batch, nt),
        in_specs=[
            blk(0), blk(1), blk(2), blk(3),
            pl.BlockSpec((tb, LANES), lambda b, t: (b * nt + t, COL_GB // LANES)),
            pl.BlockSpec((2 * DN_HEADS, tb), lambda b, t: (0, b * nt + t)),
            pl.BlockSpec((CONV_WIDTH, 3 * D_DN), lambda b, t: (0, 0)),
            pl.BlockSpec((1, DN_HEAD_DIM), lambda b, t: (0, 0)),
        ],
        out_specs=pl.BlockSpec((tb, D_DN), lambda b, t: (b * nt + t, 0)),
        out_shape=jax.ShapeDtypeStruct((n, D_DN), BF16),
        scratch_shapes=[
            pltpu.VMEM((DN_HEADS, DN_HEAD_DIM, DN_HEAD_DIM), F32),
            pltpu.VMEM((SUBLANES, 3 * D_DN), F32),
            pltpu.VMEM((tb + SUBLANES, 3 * D_DN), F32),
            pltpu.VMEM((tb, 3 * D_DN), F32),
            pltpu.VMEM((tb, LANES), F32),
            pltpu.VMEM((2 * DN_HEADS, tb), F32),
            pltpu.VMEM((n_prob, c, c), F32),
            pltpu.VMEM((n_prob, c, c), F32),
            pltpu.VMEM((n_prob, c, c), BF16),
            pltpu.VMEM((n_prob, c, c), BF16),
            pltpu.VMEM((n_prob, c, 2 * DN_HEAD_DIM), BF16),
            pltpu.VMEM((n_prob, c, DN_HEAD_DIM), F32),
            pltpu.VMEM((n_prob, 2 * c, DN_HEAD_DIM), BF16),
            pltpu.VMEM((n_prob, DN_HEAD_DIM, c), BF16),
        ],
        compiler_params=_cparams(2),
        name="deltanet",
    )(proj, proj, proj, proj, proj, gbt, dn_conv, dn_norm)


def _gelu_tanh(x):
    return 0.5 * x * (1.0 + jnp.tanh(0.7978845608028654 * (x + 0.044715 * (x * x * x))))


def _rglru_kernel(rx_ref, ry_ref, conv_ref, convb_ref, wa_ref, ba_ref, wx_ref, bx_ref, lam_ref,
                  o_ref, hc_ref, halo_ref, ext_ref, a_ref, b_ref, h_ref, *, tb):
    t = pl.program_id(1)
    bw = D_RG // RG_BLOCKS

    @pl.when(t == 0)
    def _():
        hc_ref[...] = jnp.zeros_like(hc_ref)
        halo_ref[...] = jnp.zeros_like(halo_ref)

    ext_ref[0:SUBLANES, :] = halo_ref[...]
    ext_ref[SUBLANES:, :] = rx_ref[...]
    halo_ref[...] = ext_ref[tb:tb + SUBLANES, :]
    xr = jnp.zeros((tb, D_RG), F32) + convb_ref[...]
    for j in range(CONV_WIDTH):
        off = SUBLANES - (CONV_WIDTH - 1) + j
        xr = xr + conv_ref[j:j + 1, :] * ext_ref[off:off + tb, :]

    lam = lam_ref[...]
    log_sig = -_softplus(-lam)
    rowi = lax.broadcasted_iota(I32, (tb, bw), 0)
    seq_start_row = jnp.where(t == 0, 0, -1)
    for nb in range(RG_BLOCKS):
        cs = slice(nb * bw, (nb + 1) * bw)
        xb = xr[:, cs]
        r = _sigmoid(_mm(xb, wa_ref[nb]) + ba_ref[:, cs])
        gi = _sigmoid(_mm(xb, wx_ref[nb]) + bx_ref[:, cs])
        log_a = RG_C * r * log_sig[:, cs]
        a = jnp.exp(log_a)
        mult = jnp.sqrt(jnp.tanh(-log_a) * (1.0 + a * a))
        mult = jnp.where(rowi == seq_start_row, 1.0, mult)
        bx = mult * (gi * xb)
        rpos = rowi & (SUBLANES - 1)
        d = 1
        while d < SUBLANES:
            m = rpos >= d
            bx = jnp.where(m, a * pltpu.roll(bx, d, 0) + bx, bx)
            a = jnp.where(m, a * pltpu.roll(a, d, 0), a)
            d *= 2
        a_ref[:, cs] = a
        b_ref[:, cs] = bx

    def group(i, h_prev):
        rows = pl.ds(pl.multiple_of(i * SUBLANES, SUBLANES), SUBLANES)
        ht = a_ref[rows, :] * h_prev + b_ref[rows, :]
        h_ref[rows, :] = ht
        return ht[SUBLANES - 1:SUBLANES, :]

    hc_ref[...] = lax.fori_loop(0, tb // SUBLANES, group, hc_ref[...], unroll=8)
    o_ref[...] = (h_ref[...] * _gelu_tanh(ry_ref[...])).astype(o_ref.dtype)


def _rglru(proj, rg_conv, rg_conv_b, w_a, b_a, w_x, b_x, lam, batch, seq, tb=512):
    n = proj.shape[0]
    nt = seq // tb
    bw = D_RG // RG_BLOCKS
    full = lambda shape: pl.BlockSpec(shape, lambda b, t: (0,) * len(shape))
    return pl.pallas_call(
        functools.partial(_rglru_kernel, tb=tb),
        grid=(batch, nt),
        in_specs=[
            pl.BlockSpec((tb, D_RG), lambda b, t: (b * nt + t, COL_RX // D_RG)),
            pl.BlockSpec((tb, D_RG), lambda b, t: (b * nt + t, COL_RY // D_RG)),
            full((CONV_WIDTH, D_RG)), full((1, D_RG)),
            full((RG_BLOCKS, bw, bw)), full((1, D_RG)),
            full((RG_BLOCKS, bw, bw)), full((1, D_RG)),
            full((1, D_RG)),
        ],
        out_specs=pl.BlockSpec((tb, D_RG), lambda b, t: (b * nt + t, 0)),
        out_shape=jax.ShapeDtypeStruct((n, D_RG), BF16),
        scratch_shapes=[
            pltpu.VMEM((1, D_RG), F32),
            pltpu.VMEM((SUBLANES, D_RG), F32),
            pltpu.VMEM((tb + SUBLANES, D_RG), F32),
            pltpu.VMEM((tb, D_RG), F32),
            pltpu.VMEM((tb, D_RG), F32),
            pltpu.VMEM((tb, D_RG), F32),
        ],
        compiler_params=_cparams(2),
        name="rglru",
    )(proj, proj, rg_conv, rg_conv_b, w_a, b_a, w_x, b_x, lam)


def _mem_kv_kernel(m_ref, g_ref, w_ref, o_ref):
    mn = _rmsnorm(m_ref[...], g_ref[...]).astype(BF16)
    for c0 in range(0, o_ref.shape[1], 512):
        o_ref[:, c0:c0 + 512] = jnp.dot(mn, w_ref[:, c0:c0 + 512].astype(BF16),
                                        preferred_element_type=F32).astype(o_ref.dtype)


def _mem_kv(mem2, gain, w_ckv):
    n, d = mem2.shape
    return pl.pallas_call(
        _mem_kv_kernel,
        grid=(1,),
        in_specs=[
            pl.BlockSpec((n, d), lambda i: (0, 0)),
            pl.BlockSpec((1, d), lambda i: (0, 0)),
            pl.BlockSpec((d, 2 * d), lambda i: (0, 0)),
        ],
        out_specs=pl.BlockSpec((n, 2 * d), lambda i: (0, 0)),
        out_shape=jax.ShapeDtypeStruct((n, 2 * d), BF16),
        compiler_params=_cparams(1),
        name="mem_kv",
    )(mem2, gain, w_ckv)


ROUTE_E = 0
ROUTE_RANK = TOP_K
ROUTE_GATE = 2 * TOP_K
ROUTE_ROWS = 16


def _post_mix_kernel(dn_ref, rg_ref, x_ref, wo_f32, gx_ref, wq_f32, kv_ref, wco_f32, gm_ref,
                     wrh_ref, wrl_ref, br_ref, h2_ref, xn_ref, route_ref, routet_ref, cnt_ref,
                     carry_ref, wo_ref, wq_ref, wco_ref):
    i = pl.program_id(0)
    tm, d = x_ref.shape
    hd = d // XA_HEADS

    @pl.when(i == 0)
    def _():
        carry_ref[...] = jnp.zeros_like(carry_ref)
        wo_ref[...] = wo_f32[...].astype(BF16)
        wq_ref[...] = wq_f32[...].astype(BF16)
        wco_ref[...] = wco_f32[...].astype(BF16)

    h1 = (x_ref[...] + jnp.dot(dn_ref[...], wo_ref[0:D_DN, :], preferred_element_type=F32)
          + jnp.dot(rg_ref[...], wo_ref[D_DN:, :], preferred_element_type=F32))

    hn = _rmsnorm(h1, gx_ref[...]).astype(BF16)
    q = jnp.dot(hn, wq_ref[...], preferred_element_type=F32)
    heads = []
    for hh in range(XA_HEADS):
        cs = slice(hh * hd, (hh + 1) * hd)
        s = _mm_nt(q[:, cs], kv_ref[:, cs]) * (hd ** -0.5)
        p = jnp.exp(s - jnp.max(s, axis=-1, keepdims=True))
        p = p / jnp.sum(p, axis=-1, keepdims=True)
        heads.append(_mm(p, kv_ref[:, d + hh * hd:d + (hh + 1) * hd]).astype(BF16))
    o = jnp.concatenate(heads, axis=1)
    h2 = h1 + jnp.dot(o, wco_ref[...], preferred_element_type=F32)
    h2_ref[...] = h2

    xn = _rmsnorm(h2, gm_ref[...])
    _store_row_tiles(xn_ref, xn)
    logits = _mm_split(xn, wrh_ref[...], wrl_ref[...]) + br_ref[...]
    lane = lax.broadcasted_iota(I32, logits.shape, 1)
    lane_f = lane.astype(F32)
    neg = jnp.float32(-jnp.inf)
    lg = jnp.where(lane < N_EXPERTS, logits, neg)
    vals, idxs, hots = [], [], []
    for _ in range(TOP_K):
        m = jnp.max(lg, axis=-1, keepdims=True)
        idx = jnp.min(jnp.where(lg == m, lane_f, float(LANES)), axis=-1, keepdims=True)
        hot = lane_f == idx
        lg = jnp.where(hot, neg, lg)
        vals.append(m)
        idxs.append(idx)
        hots.append(hot)
    es = [jnp.exp(v - vals[0]) for v in vals]
    den = es[0] + es[1] + es[2] + es[3]
    gates = [e / den for e in es]

    chosen = jnp.zeros(logits.shape, F32)
    for hot in hots:
        chosen = chosen + jnp.where(hot, 1.0, 0.0)
    r2 = lax.broadcasted_iota(I32, (tm, tm), 0)
    c2 = lax.broadcasted_iota(I32, (tm, tm), 1)
    before = _mm(jnp.where(r2 > c2, 1.0, 0.0), chosen) + carry_ref[0:1, :]
    ranks = [jnp.sum(jnp.where(hot, before, 0.0), axis=-1, keepdims=True) for hot in hots]
    carry_ref[...] = carry_ref[...] + jnp.sum(chosen, axis=0, keepdims=True)
    cnt_ref[...] = carry_ref[...]

    rec = jnp.zeros(logits.shape, F32)
    for kk in range(TOP_K):
        rec = jnp.where(lane == ROUTE_E + kk, idxs[kk], rec)
        rec = jnp.where(lane == ROUTE_RANK + kk, ranks[kk], rec)
        rec = jnp.where(lane == ROUTE_GATE + kk, gates[kk], rec)
    route_ref[...] = rec
    routet_ref[...] = rec.T[0:ROUTE_ROWS, :]


def _post_mix(dn, rg, x2, w_out, g_cross, w_cq, kv, w_co, g_moe, w_r_hi, w_r_lo, b_r, seq, mem_len,
              tm=512):
    n, d = x2.shape
    per_b = seq // tm
    full = lambda shape: pl.BlockSpec(shape, lambda i: (0,) * len(shape))
    once = lambda shape: pl.BlockSpec(shape, lambda i: (0,) * len(shape),
                                      pipeline_mode=pl.Buffered(1))
    return pl.pallas_call(
        _post_mix_kernel,
        grid=(n // tm,),
        in_specs=[
            pl.BlockSpec((tm, D_DN), lambda i: (i, 0)),
            pl.BlockSpec((tm, D_RG), lambda i: (i, 0)),
            pl.BlockSpec((tm, d), lambda i: (i, 0)),
            once((d, d)), full((1, d)), once((d, d)),
            pl.BlockSpec((mem_len, 2 * d), lambda i: (i // per_b, 0)),
            once((d, d)), full((1, d)),
            full((d, LANES)), full((d, LANES)), full((1, LANES)),
        ],
        out_specs=[
            pl.BlockSpec((tm, d), lambda i: (i, 0)),
            pl.BlockSpec((tm * (d // LANES), LANES), lambda i: (i, 0)),
            pl.BlockSpec((tm, LANES), lambda i: (i, 0)),
            pl.BlockSpec((ROUTE_ROWS, tm), lambda i: (0, i)),
            pl.BlockSpec((SUBLANES, LANES), lambda i: (0, 0)),
        ],
        out_shape=[
            jax.ShapeDtypeStruct((n, d), F32),
            jax.ShapeDtypeStruct((n * (d // LANES), LANES), F32),
            jax.ShapeDtypeStruct((n, LANES), F32),
            jax.ShapeDtypeStruct((ROUTE_ROWS, n), F32),
            jax.ShapeDtypeStruct((SUBLANES, LANES), F32),
        ],
        scratch_shapes=[pltpu.VMEM((SUBLANES, LANES), F32),
                        pltpu.VMEM((d, d), BF16), pltpu.VMEM((d, d), BF16), pltpu.VMEM((d, d), BF16)],
        compiler_params=_cparams(1),
        name="post_mix",
    )(dn, rg, x2, w_out, g_cross, w_cq, kv, w_co, g_moe, w_r_hi, w_r_lo, b_r)


def _dest_kernel(start_ref, routet_ref, dest_ref):
    e = routet_ref[ROUTE_E:ROUTE_E + SUBLANES, :]
    rank = routet_ref[ROUTE_RANK:ROUTE_RANK + SUBLANES, :]
    base = jnp.zeros(e.shape, F32)
    for j in range(N_EXPERTS):
        base = jnp.where(e == float(j), start_ref[j].astype(F32), base)
    row = lax.broadcasted_iota(I32, e.shape, 0)
    dest_ref[...] = jnp.where(row < TOP_K, (base + rank).astype(I32), 0)


def _dest(pad_start, route_t, tm=2048):
    n = route_t.shape[1]
    grid_spec = pltpu.PrefetchScalarGridSpec(
        num_scalar_prefetch=1,
        grid=(n // tm,),
        in_specs=[pl.BlockSpec((ROUTE_ROWS, tm), lambda i, st: (0, i))],
        out_specs=pl.BlockSpec((SUBLANES, tm), lambda i, st: (0, i)),
    )
    return pl.pallas_call(
        _dest_kernel,
        grid_spec=grid_spec,
        out_shape=jax.ShapeDtypeStruct((SUBLANES, n), I32),
        compiler_params=_cparams(1),
        name="dest",
    )(pad_start, route_t)


def _row_tile(ref, r, rt):
    return ref.at[pl.ds(pl.multiple_of(r * rt, rt), rt)]


def _dispatch_kernel(meta_ref, dest_ref, xn_ref, xs_ref, zero_ref, sem, zsem, *, tm, rt, n_blocks):
    i = pl.program_id(0)
    bm = MOE_BM * rt

    def per_token(tk, carry):
        src = _row_tile(xn_ref, tk, rt)
        for kk in range(TOP_K):
            r = dest_ref[kk, tk]
            pltpu.make_async_copy(src, _row_tile(xs_ref, r, rt), sem).start(priority=kk % 2)
        return carry

    lax.fori_loop(0, tm, per_token, 0, unroll=4)

    @pl.when(i == 0)
    def _():
        zero_ref[...] = jnp.zeros_like(zero_ref)

        def pad_copy(r):
            return pltpu.make_async_copy(zero_ref.at[pl.ds(0, rt)], _row_tile(xs_ref, r, rt), zsem)

        def per_expert(e, total):
            r0 = meta_ref[e]
            cnt = meta_ref[N_EXPERTS + e]

            def one(r, carry):
                pad_copy(r0 + r).start()
                return carry

            lax.fori_loop(0, cnt, one, 0)
            return total + cnt

        total = lax.fori_loop(0, N_EXPERTS, per_expert, 0)

        def drain(r, carry):
            pad_copy(0).wait()
            return carry

        lax.fori_loop(0, total, drain, 0)

        def tail_copy(b):
            return pltpu.make_async_copy(
                zero_ref, xs_ref.at[pl.ds(pl.multiple_of(b * bm, bm), bm)], zsem)

        n_used = meta_ref[2 * N_EXPERTS]

        def tail(b, carry):
            tail_copy(b).start()
            return carry

        lax.fori_loop(n_used, n_blocks, tail, 0)

        def tail_drain(b, carry):
            tail_copy(0).wait()
            return carry

        lax.fori_loop(n_used, n_blocks, tail_drain, 0)

    for _ in range(TOP_K):
        pltpu.make_async_copy(xn_ref, xs_ref.at[pl.ds(0, tm * rt)], sem).wait()


def _dispatch(meta, dest, xn_t, n_rows, d, tm=256):
    rt = d // LANES
    n = xn_t.shape[0] // rt
    n_blocks = n_rows // MOE_BM
    grid_spec = pltpu.PrefetchScalarGridSpec(
        num_scalar_prefetch=1,
        grid=(n // tm,),
        in_specs=[
            pl.BlockSpec((SUBLANES, tm), lambda i, meta: (0, i), memory_space=pltpu.SMEM),
            pl.BlockSpec((tm * rt, LANES), lambda i, meta: (i, 0)),
        ],
        out_specs=pl.BlockSpec(memory_space=pl.ANY),
        scratch_shapes=[
            pltpu.VMEM((MOE_BM * rt, LANES), F32),
            pltpu.SemaphoreType.DMA,
            pltpu.SemaphoreType.DMA,
        ],
    )
    return pl.pallas_call(
        functools.partial(_dispatch_kernel, tm=tm, rt=rt, n_blocks=n_blocks),
        grid_spec=grid_spec,
        out_shape=jax.ShapeDtypeStruct((n_rows * rt, LANES), F32),
        compiler_params=_cparams(1),
        name="dispatch",
    )(meta, dest, xn_t)


def _experts_kernel(be_ref, nu_ref, nxt_ref, par_ref, x_ref, bg_ref, bu_ref, bd_ref,
                    wg_hbm, wu_hbm, wd_hbm, y_ref, wf_ref, wb_ref, sem):
    i = pl.program_id(0)
    used = i < nu_ref[0]
    e = be_ref[i]
    changed = (i == 0) | (e != be_ref[jnp.maximum(i - 1, 0)])

    def weight_copies(expert, slot):
        return [pltpu.make_async_copy(w.at[expert], wf_ref.at[slot, j], sem.at[slot])
                for j, w in enumerate((wg_hbm, wu_hbm, wd_hbm))]

    @pl.when(i == 0)
    def _():
        for cp in weight_copies(e, par_ref[e]):
            cp.start()

    @pl.when(used & changed)
    def _():
        slot = par_ref[e]
        for cp in weight_copies(e, slot):
            cp.wait()
        nxt = nxt_ref[e]

        @pl.when(nxt < N_EXPERTS)
        def _():
            for cp in weight_copies(nxt, 1 - slot):
                cp.start()

        for j in range(3):
            wb_ref[j] = wf_ref[slot, j].astype(BF16)

    @pl.when(used)
    def _():
        d = wb_ref.shape[1]
        x = _load_row_tiles(x_ref, MOE_BM, d).astype(BF16)
        gt = jnp.minimum(jnp.dot(x, wb_ref[0], preferred_element_type=F32) + bg_ref[...],
                         SWIGLU_LIMIT)
        up = jnp.clip(jnp.dot(x, wb_ref[1], preferred_element_type=F32) + bu_ref[...],
                      -SWIGLU_LIMIT, SWIGLU_LIMIT)
        hid = (up + 1.0) * (gt * _sigmoid(SWIGLU_ALPHA * gt))
        _store_row_tiles(y_ref, jnp.dot(hid.astype(BF16), wb_ref[2],
                                        preferred_element_type=F32) + bd_ref[...])

    @pl.when(jnp.logical_not(used))
    def _():
        y_ref[...] = jnp.zeros_like(y_ref)


def _experts(block_e, n_used, next_e, parity, xs_t, w_gate, b_gate, w_up, b_up, w_down, b_down):
    d, d_ff = w_gate.shape[1:]
    assert d == d_ff, "weight staging buffers assume square expert matrices"
    rt = d // LANES
    n_blocks = xs_t.shape[0] // (MOE_BM * rt)
    bspec = lambda m: pl.BlockSpec((None, 1, m), lambda i, be, nu, nx, pa: (be[i], 0, 0))
    hbm = pl.BlockSpec(memory_space=pl.ANY)
    grid_spec = pltpu.PrefetchScalarGridSpec(
        num_scalar_prefetch=4,
        grid=(n_blocks,),
        in_specs=[
            pl.BlockSpec((MOE_BM * rt, LANES),
                         lambda i, be, nu, nx, pa: (jnp.maximum(jnp.minimum(i, nu[0] - 1), 0), 0)),
            bspec(d_ff), bspec(d_ff), bspec(d),
            hbm, hbm, hbm,
        ],
        out_specs=pl.BlockSpec((MOE_BM * rt, LANES), lambda i, be, nu, nx, pa: (i, 0)),
        scratch_shapes=[
            pltpu.VMEM((2, 3, d, d_ff), F32),
            pltpu.VMEM((3, d, d_ff), BF16),
            pltpu.SemaphoreType.DMA((2,)),
        ],
    )
    return pl.pallas_call(
        _experts_kernel,
        grid_spec=grid_spec,
        out_shape=jax.ShapeDtypeStruct(xs_t.shape, F32),
        compiler_params=_cparams(1),
        name="experts",
    )(block_e, n_used, next_e, parity, xs_t, b_gate, b_up, b_down, w_gate, w_up, w_down)


def _combine_kernel(dcur_ref, dnext_ref, h2_ref, route_ref, gain_ref, ys_ref, o_ref,
                    buf_ref, sem, *, tm):
    i = pl.program_id(0)
    nsteps = pl.num_programs(0)
    slot = i & 1
    d = h2_ref.shape[1]
    rt = d // LANES

    def issue_all(dref, s):
        def per_token(tk, carry):
            for kk in range(TOP_K):
                r = dref[kk, tk]
                pltpu.make_async_copy(_row_tile(ys_ref, r, rt),
                                      _row_tile(buf_ref.at[s, kk], tk, rt),
                                      sem.at[s]).start(priority=kk % 2)
            return carry
        lax.fori_loop(0, tm, per_token, 0, unroll=4)

    @pl.when(i == 0)
    def _():
        issue_all(dcur_ref, 0)

    @pl.when(i + 1 < nsteps)
    def _():
        issue_all(dnext_ref, 1 - slot)

    for kk in range(TOP_K):
        pltpu.make_async_copy(ys_ref.at[pl.ds(0, tm * rt)], buf_ref.at[slot, kk],
                              sem.at[slot]).wait()

    rec = route_ref[...]
    acc = h2_ref[...]
    for kk in range(TOP_K):
        acc = acc + (rec[:, ROUTE_GATE + kk:ROUTE_GATE + kk + 1]
                     * _load_row_tiles(buf_ref.at[slot, kk], tm, d))
    o_ref[...] = _rmsnorm(acc, gain_ref[...])


def _combine(dest, h2, route, gain, ys_t, tm=128):
    n, d = h2.shape
    nsteps = n // tm
    return pl.pallas_call(
        functools.partial(_combine_kernel, tm=tm),
        grid=(nsteps,),
        in_specs=[
            pl.BlockSpec((SUBLANES, tm), lambda i: (0, i), memory_space=pltpu.SMEM),
            pl.BlockSpec((SUBLANES, tm), lambda i: (0, jnp.minimum(i + 1, nsteps - 1)),
                         memory_space=pltpu.SMEM),
            pl.BlockSpec((tm, d), lambda i: (i, 0)),
            pl.BlockSpec((tm, LANES), lambda i: (i, 0)),
            pl.BlockSpec((1, d), lambda i: (0, 0)),
            pl.BlockSpec(memory_space=pl.ANY),
        ],
        out_specs=pl.BlockSpec((tm, d), lambda i: (i, 0)),
        out_shape=jax.ShapeDtypeStruct((n, d), F32),
        scratch_shapes=[
            pltpu.VMEM((2, TOP_K, tm * (d // LANES), LANES), F32),
            pltpu.SemaphoreType.DMA((2,)),
        ],
        compiler_params=_cparams(1),
        name="combine",
    )(dest, dest, h2, route, gain, ys_t)


def kernel(x, mem, norm_mix, w_in, dn_conv, dn_a_log, dn_dt_bias, dn_norm, rg_conv, rg_conv_b, rg_w_a, rg_b_a, rg_w_x, rg_b_x, rg_lambda, w_out, norm_cross, norm_mem, w_cq, w_ckv, w_co, norm_moe, w_router, b_router, w_gate, b_gate, w_up, b_up, w_down, b_down, norm_final):
    batch, seq, d = x.shape
    mem_len = mem.shape[1]
    n = batch * seq
    assert w_in.shape[0] == 1, "single-layer trunk"
    x2 = x.reshape(n, d)

    wi = w_in[0]
    n_gate = 2 * DN_HEADS
    w_cat = jnp.concatenate(
        [wi[:, :4 * D_DN], wi[:, 4 * D_DN + n_gate:],
         jnp.pad(wi[:, 4 * D_DN:4 * D_DN + n_gate], ((0, 0), (0, LANES - n_gate)))],
        axis=1).astype(BF16)
    wbat = wi[:, 4 * D_DN:4 * D_DN + n_gate].T.astype(BF16)
    prow = (jnp.zeros((SUBLANES, LANES), F32)
            .at[0, DN_HEADS:n_gate].set(dn_a_log[0]).at[1, DN_HEADS:n_gate].set(dn_dt_bias[0]))
    pcol = (jnp.zeros((n_gate, LANES), F32)
            .at[DN_HEADS:, 0].set(dn_a_log[0]).at[DN_HEADS:, 1].set(dn_dt_bias[0]))

    proj, gbt = _in_proj(x2, norm_mix, w_cat, wbat, prow, pcol)
    dn = _deltanet(proj, gbt, dn_conv[0], dn_norm, batch, seq)
    rg = _rglru(proj, rg_conv[0], rg_conv_b, rg_w_a[0], rg_b_a[0].reshape(1, D_RG),
                rg_w_x[0], rg_b_x[0].reshape(1, D_RG), rg_lambda, batch, seq)
    kv = _mem_kv(mem.reshape(batch * mem_len, d), norm_mem, w_ckv[0])

    w_r = jnp.pad(w_router[0], ((0, 0), (0, LANES - N_EXPERTS)))
    w_r_hi = w_r.astype(BF16)
    w_r_lo = (w_r - w_r_hi.astype(F32)).astype(BF16)
    b_r = jnp.pad(b_router, ((0, 0), (0, LANES - N_EXPERTS)))
    h2, xn, route, route_t, counts = _post_mix(
        dn, rg, x2, w_out[0], norm_cross, w_cq[0], kv, w_co[0], norm_moe, w_r_hi, w_r_lo, b_r,
        seq, mem_len)

    n_blocks = n * TOP_K // MOE_BM + N_EXPERTS
    n_rows = n_blocks * MOE_BM
    cnt = counts[0, :N_EXPERTS].astype(I32)
    padded = (cnt + MOE_BM - 1) // MOE_BM * MOE_BM
    pad_end = jnp.cumsum(padded)
    pad_start = pad_end - padded
    n_used = (pad_end[-1:] // MOE_BM).astype(I32)
    block_e = jnp.minimum(
        jnp.sum(pad_end[None, :] <= (jnp.arange(n_blocks, dtype=I32) * MOE_BM)[:, None], axis=1),
        N_EXPERTS - 1).astype(I32)
    meta = jnp.concatenate([pad_start + cnt, padded - cnt, n_used]).astype(I32)

    dest = _dest(pad_start.astype(I32), route_t)
    xs = _dispatch(meta, dest, xn, n_rows, d)
    has = cnt > 0
    eid = jnp.where(has, jnp.arange(N_EXPERTS, dtype=I32), N_EXPERTS)
    after = lax.cummin(eid, axis=0, reverse=True)
    next_e = jnp.concatenate([after[1:], jnp.full((1,), N_EXPERTS, I32)]).astype(I32)
    parity = ((jnp.cumsum(has.astype(I32)) - 1) & 1).astype(I32)
    ys = _experts(block_e, n_used, next_e, parity, xs, w_gate[0], b_gate[0][:, None, :], w_up[0],
                  b_up[0][:, None, :], w_down[0], b_down[0][:, None, :])
    out = _combine(dest, h2, route, norm_final.reshape(1, d), ys)
    return out.reshape(batch, seq, d)
```

```python
import functools

import jax
import jax.numpy as jnp
from jax import lax
from jax.experimental import pallas as pl
from jax.experimental.pallas import tpu as pltpu

F32 = jnp.float32
BF16 = jnp.bfloat16
I32 = jnp.int32

EPS = 1e-6
LANES = 128
SUBLANES = 8
VMEM_LIMIT = 48 * 1024 * 1024

DN_HEADS = 4
DN_HEAD_DIM = 128
D_DN = DN_HEADS * DN_HEAD_DIM
D_RG = 512
RG_BLOCKS = 4
RG_C = 8.0
CONV_WIDTH = 4
XA_HEADS = 4
N_EXPERTS = 32
TOP_K = 4
SWIGLU_LIMIT = 7.0
SWIGLU_ALPHA = 1.702

DN_CHUNK = 128
INV_BASE = 16
MOE_BM = 256

COL_RX = 4 * D_DN
COL_RY = COL_RX + D_RG
COL_GB = COL_RY + D_RG
PROJ_W = COL_GB + LANES


def _cparams(n_axes=1):
    return pltpu.CompilerParams(
        dimension_semantics=("arbitrary",) * n_axes, vmem_limit_bytes=VMEM_LIMIT)


def _mm(a, b):
    return jnp.dot(a.astype(BF16), b.astype(BF16), preferred_element_type=F32)


def _mm_nt(a, b):
    return lax.dot_general(a.astype(BF16), b.astype(BF16), (((1,), (1,)), ((), ())),
                           preferred_element_type=F32)


def _rmsnorm(x, g):
    return x * lax.rsqrt(jnp.mean(x * x, axis=-1, keepdims=True) + EPS) * g


def _sigmoid(x):
    return 0.5 * jnp.tanh(0.5 * x) + 0.5


def _mm_split(a, b_hi, b_lo):
    a_hi = a.astype(BF16)
    a_lo = (a - a_hi.astype(F32)).astype(BF16)
    return (jnp.dot(a_hi, b_hi, preferred_element_type=F32)
            + jnp.dot(a_hi, b_lo, preferred_element_type=F32)
            + jnp.dot(a_lo, b_hi, preferred_element_type=F32))


def _softplus(x):
    return jnp.maximum(x, 0.0) + jnp.log1p(jnp.exp(-jnp.abs(x)))


def _load_row_tiles(ref, rows, d):
    rt = d // LANES
    return jnp.concatenate([ref[pl.ds(s, rows, stride=rt), :] for s in range(rt)], axis=1)


def _store_row_tiles(ref, val):
    rows, d = val.shape
    rt = d // LANES
    for s in range(rt):
        ref[pl.ds(s, rows, stride=rt), :] = val[:, s * LANES:(s + 1) * LANES]


def _load_segmented(ref):
    seg = ref.shape[0] // SUBLANES
    return jnp.concatenate([ref[pl.ds(j, SUBLANES, stride=seg), :] for j in range(seg)], axis=0)


def _store_segmented(ref, val):
    seg = val.shape[0] // SUBLANES
    for j in range(seg):
        ref[pl.ds(j, SUBLANES, stride=seg), :] = val[j * SUBLANES:(j + 1) * SUBLANES]


def _causal_conv_segmented(xp, halo_ref, cs, w):
    tb = xp.shape[0]
    ng = w.shape[0] - 1
    prev = halo_ref[:, cs]
    last = xp[tb - ng * SUBLANES:, :]
    halo_ref[:, cs] = last
    sub = lax.broadcasted_iota(I32, (SUBLANES, xp.shape[1]), 0)
    groups = []
    for g in range(ng):
        rows = slice(g * SUBLANES, (g + 1) * SUBLANES)
        groups.append(jnp.where(sub == 0, pltpu.roll(prev[rows], 1, 0),
                                pltpu.roll(last[rows], 1, 0)))
    ext = jnp.concatenate(groups + [xp], axis=0)
    y = w[ng:ng + 1] * xp
    for k in range(1, ng + 1):
        y = y + w[ng - k:ng - k + 1] * ext[(ng - k) * SUBLANES:(ng - k) * SUBLANES + tb]
    return y


def _in_proj_kernel(x_ref, g_ref, w_ref, wbat_ref, prow_ref, pcol_ref, proj_ref, gbt_ref):
    u = _rmsnorm(x_ref[...], g_ref[...]).astype(BF16)
    for c0 in range(0, COL_GB, 512):
        proj_ref[:, c0:c0 + 512] = jnp.dot(u, w_ref[:, c0:c0 + 512], preferred_element_type=F32)
    ba = jnp.dot(u, w_ref[:, COL_GB:PROJ_W], preferred_element_type=F32)
    lane = lax.broadcasted_iota(I32, ba.shape, 1)
    g = -jnp.exp(prow_ref[0:1, :]) * _softplus(ba + prow_ref[1:2, :])
    proj_ref[:, COL_GB:PROJ_W] = jnp.where(lane < DN_HEADS, _sigmoid(ba), g)
    bat = lax.dot_general(wbat_ref[...], u, (((1,), (1,)), ((), ())), preferred_element_type=F32)
    row = lax.broadcasted_iota(I32, bat.shape, 0)
    gt = -jnp.exp(pcol_ref[:, 0:1]) * _softplus(bat + pcol_ref[:, 1:2])
    gbt_ref[...] = jnp.where(row < DN_HEADS, _sigmoid(bat), gt)


def _in_proj(x2, gain, w_cat, wbat, prow, pcol, tm=512):
    n, d = x2.shape
    return pl.pallas_call(
        _in_proj_kernel,
        grid=(n // tm,),
        in_specs=[
            pl.BlockSpec((tm, d), lambda i: (i, 0)),
            pl.BlockSpec((1, d), lambda i: (0, 0)),
            pl.BlockSpec((d, PROJ_W), lambda i: (0, 0)),
            pl.BlockSpec((2 * DN_HEADS, d), lambda i: (0, 0)),
            pl.BlockSpec((SUBLANES, LANES), lambda i: (0, 0)),
            pl.BlockSpec((2 * DN_HEADS, LANES), lambda i: (0, 0)),
        ],
        out_specs=[
            pl.BlockSpec((tm, PROJ_W), lambda i: (i, 0)),
            pl.BlockSpec((2 * DN_HEADS, tm), lambda i: (0, i)),
        ],
        out_shape=[
            jax.ShapeDtypeStruct((n, PROJ_W), F32),
            jax.ShapeDtypeStruct((2 * DN_HEADS, n), F32),
        ],
        compiler_params=_cparams(1),
        name="in_proj",
    )(x2, gain, w_cat, wbat, prow, pcol)


def _deltanet_kernel(*refs, tb):
    n_grp = 3 * DN_HEADS
    qkv_refs = refs[:n_grp]
    (z_ref, gb_ref, gbt_ref, conv_ref, norm_ref, o_ref,
     s_ref, halo_ref, act_ref, gcc_ref, gcr_ref,
     a_ref, p_ref, d_ref, qk_ref, rhs_ref, u_ref, wq_ref, kdt_ref) = refs[n_grp:]
    t = pl.program_id(1)
    c = DN_CHUNK
    dh = DN_HEAD_DIM

    @pl.when(t == 0)
    def _():
        s_ref[...] = jnp.zeros_like(s_ref)
        halo_ref[...] = jnp.zeros_like(halo_ref)

    sub = lax.broadcasted_iota(I32, (SUBLANES, dh), 0)
    for grp in range(n_grp):
        cs = slice(grp * dh, (grp + 1) * dh)
        x = qkv_refs[grp][...]
        prev = halo_ref[:, cs]
        halo_ref[:, cs] = x[tb - SUBLANES:, :]
        y = conv_ref[CONV_WIDTH - 1:CONV_WIDTH, cs] * x
        for k in range(1, CONV_WIDTH):
            xs = pltpu.roll(x, k, 0)
            head = jnp.where(sub < k, pltpu.roll(prev, k, 0), xs[:SUBLANES])
            xs = jnp.concatenate([head, xs[SUBLANES:]], axis=0)
            y = y + conv_ref[CONV_WIDTH - 1 - k:CONV_WIDTH - k, cs] * xs
        y = y * _sigmoid(y)
        if grp < 2 * DN_HEADS:
            y = y * lax.rsqrt(jnp.sum(y * y, axis=-1, keepdims=True) + EPS)
        if grp < DN_HEADS:
            y = y * (dh ** -0.5)
        act_ref[grp] = y

    gcol = gb_ref[...]
    rpos = lax.broadcasted_iota(I32, gcol.shape, 0) & (c - 1)
    d = 1
    while d < c:
        gcol = gcol + jnp.where(rpos >= d, pltpu.roll(gcol, d, 0), 0.0)
        d *= 2
    gcc_ref[...] = gcol
    grow = gbt_ref[...]
    lpos = lax.broadcasted_iota(I32, grow.shape, 1) & (c - 1)
    d = 1
    while d < c:
        grow = grow + jnp.where(lpos >= d, pltpu.roll(grow, d, 1), 0.0)
        d *= 2
    gcr_ref[...] = grow

    row = lax.broadcasted_iota(I32, (c, c), 0)
    col = lax.broadcasted_iota(I32, (c, c), 1)
    causal = row >= col
    strict = row > col
    eye = jnp.where(row == col, 1.0, 0.0)
    gain = norm_ref[...]
    n_chunks = tb // c
    probs = [(ci, h) for ci in range(n_chunks) for h in range(DN_HEADS)]

    decay_last = []
    for p, (ci, h) in enumerate(probs):
        rows = slice(ci * c, (ci + 1) * c)
        q = act_ref[h, rows, :]
        k = act_ref[DN_HEADS + h, rows, :]
        v = act_ref[2 * DN_HEADS + h, rows, :]
        beta = gb_ref[rows, h:h + 1]
        gc = gcc_ref[rows, DN_HEADS + h:DN_HEADS + h + 1]
        gr = gcr_ref[DN_HEADS + h:DN_HEADS + h + 1, rows]
        g_last = gc[c - 1:c, :]
        decay = jnp.where(causal, jnp.exp(jnp.where(causal, gc - gr, 0.0)), 0.0)
        kb = k * beta
        both = _mm_nt(jnp.concatenate([kb, q], axis=0), k)
        a_ref[p] = jnp.where(strict, both[:c] * decay, 0.0)
        qk_ref[p] = (both[c:] * decay).astype(BF16)
        egc = jnp.exp(gc)
        rhs_ref[p] = jnp.concatenate([v * beta, kb * egc], axis=1).astype(BF16)
        wq_ref[p, c:, :] = (q * egc).astype(BF16)
        kdt_ref[p] = (k * jnp.exp(g_last - gc)).T.astype(BF16)
        decay_last.append(jnp.exp(g_last))

    shift = INV_BASE.bit_length() - 1
    blk = (row >> shift) == (col >> shift)
    for p in range(len(probs)):
        diag = jnp.where(blk, a_ref[p], 0.0)
        p_ref[p] = eye - diag
        d_ref[p] = _mm(diag, diag).astype(BF16)
    for it in range(shift - 1):
        for p in range(len(probs)):
            pw = d_ref[p]
            inv = p_ref[p]
            p_ref[p] = inv + _mm(inv, pw)
            if it < shift - 2:
                d_ref[p] = _mm(pw, pw).astype(BF16)
    s = INV_BASE
    while s < c:
        sh = s.bit_length() - 1
        off = ((row >> (sh + 1)) == (col >> (sh + 1))) & ((row >> sh) != (col >> sh))
        for p in range(len(probs)):
            d_ref[p] = _mm(p_ref[p], jnp.where(off, a_ref[p], 0.0)).astype(BF16)
        for p in range(len(probs)):
            inv = p_ref[p]
            p_ref[p] = inv - _mm(d_ref[p], inv)
        s *= 2
    for p in range(len(probs)):
        uw = _mm(p_ref[p], rhs_ref[p])
        u_ref[p] = uw[:, :dh]
        wq_ref[p, :c, :] = uw[:, dh:].astype(BF16)

    for ci in range(n_chunks):
        rows = slice(ci * c, (ci + 1) * c)
        ps = [ci * DN_HEADS + h for h in range(DN_HEADS)]
        s_old = [s_ref[h] for h in range(DN_HEADS)]
        ws = [_mm(wq_ref[p], s_old[h]) for h, p in enumerate(ps)]
        v_new = [u_ref[p] - ws[h][:c] for h, p in enumerate(ps)]
        outs = [ws[h][c:] + _mm(qk_ref[p], v_new[h]) for h, p in enumerate(ps)]
        for h, p in enumerate(ps):
            s_ref[h] = s_old[h] * decay_last[p] + _mm(kdt_ref[p], v_new[h])
        for h in range(DN_HEADS):
            o = _rmsnorm(outs[h], gain)
            zz = z_ref[rows, h * dh:(h + 1) * dh]
            o_ref[rows, h * dh:(h + 1) * dh] = (o * (zz * _sigmoid(zz))).astype(o_ref.dtype)


def _deltanet(proj, gbt, dn_conv, dn_norm, batch, seq, tb=512):
    n = proj.shape[0]
    nt = seq // tb
    c = DN_CHUNK
    n_prob = (tb // c) * DN_HEADS
    n_grp = 3 * DN_HEADS
    grp = lambda j: pl.BlockSpec((tb, DN_HEAD_DIM), lambda b, t, j=j: (b * nt + t, j))
    return pl.pallas_call(
        functools.partial(_deltanet_kernel, tb=tb),
        grid=(batch, nt),
        in_specs=[grp(j) for j in range(n_grp)] + [
            pl.BlockSpec((tb, D_DN), lambda b, t: (b * nt + t, 3)),
            pl.BlockSpec((tb, LANES), lambda b, t: (b * nt + t, COL_GB // LANES)),
            pl.BlockSpec((2 * DN_HEADS, tb), lambda b, t: (0, b * nt + t)),
            pl.BlockSpec((CONV_WIDTH, 3 * D_DN), lambda b, t: (0, 0)),
            pl.BlockSpec((1, DN_HEAD_DIM), lambda b, t: (0, 0)),
        ],
        out_specs=pl.BlockSpec((tb, D_DN), lambda b, t: (b * nt + t, 0)),
        out_shape=jax.ShapeDtypeStruct((n, D_DN), BF16),
        scratch_shapes=[
            pltpu.VMEM((DN_HEADS, DN_HEAD_DIM, DN_HEAD_DIM), F32),
            pltpu.VMEM((SUBLANES, 3 * D_DN), F32),
            pltpu.VMEM((n_grp, tb, DN_HEAD_DIM), F32),
            pltpu.VMEM((tb, LANES), F32),
            pltpu.VMEM((2 * DN_HEADS, tb), F32),
            pltpu.VMEM((n_prob, c, c), F32),
            pltpu.VMEM((n_prob, c, c), F32),
            pltpu.VMEM((n_prob, c, c), BF16),
            pltpu.VMEM((n_prob, c, c), BF16),
            pltpu.VMEM((n_prob, c, 2 * DN_HEAD_DIM), BF16),
            pltpu.VMEM((n_prob, c, DN_HEAD_DIM), F32),
            pltpu.VMEM((n_prob, 2 * c, DN_HEAD_DIM), BF16),
            pltpu.VMEM((n_prob, DN_HEAD_DIM, c), BF16),
        ],
        compiler_params=_cparams(2),
        name="deltanet",
    )(*([proj] * (n_grp + 2)), gbt, dn_conv, dn_norm)


def _gelu_tanh(x):
    return 0.5 * x * (1.0 + jnp.tanh(0.7978845608028654 * (x + 0.044715 * (x * x * x))))


def _rglru_kernel(*refs, tb):
    rx_refs = refs[:RG_BLOCKS]
    (ry_ref, conv_ref, convb_ref, wa_ref, ba_ref, wx_ref, bx_ref, lam_ref,
     o_ref, hc_ref, halo_ref, a_ref, b_ref, h_ref) = refs[RG_BLOCKS:]
    t = pl.program_id(1)
    bw = D_RG // RG_BLOCKS
    seg = tb // SUBLANES

    @pl.when(t == 0)
    def _():
        hc_ref[...] = jnp.zeros_like(hc_ref)
        halo_ref[...] = jnp.zeros_like(halo_ref)

    log_sig = -_softplus(-lam_ref[...])
    rowi = lax.broadcasted_iota(I32, (tb, bw), 0)
    seq_start_row = jnp.where(t == 0, 0, -1)
    for nb in range(RG_BLOCKS):
        cs = slice(nb * bw, (nb + 1) * bw)
        xb = (_causal_conv_segmented(_load_segmented(rx_refs[nb]), halo_ref, cs,
                                     conv_ref[:, cs]) + convb_ref[:, cs])
        r = _sigmoid(_mm(xb, wa_ref[nb]) + ba_ref[:, cs])
        gi = _sigmoid(_mm(xb, wx_ref[nb]) + bx_ref[:, cs])
        log_a = RG_C * r * log_sig[:, cs]
        a = jnp.exp(log_a)
        y = jnp.tanh(-log_a) * (1.0 + a * a)
        mult = jnp.where(y > 0.0, y * lax.rsqrt(y), 0.0)
        mult = jnp.where(rowi == seq_start_row, 1.0, mult)
        a_ref[:, cs] = a
        b_ref[:, cs] = mult * (gi * xb)

    def scan(j, carry):
        ac, bc = carry
        rows = pl.ds(pl.multiple_of(j * SUBLANES, SUBLANES), SUBLANES)
        aj = a_ref[rows, :]
        bc = aj * bc + b_ref[rows, :]
        ac = aj * ac
        a_ref[rows, :] = ac
        b_ref[rows, :] = bc
        return ac, bc

    ac, bc = lax.fori_loop(1, seg, scan, (a_ref[0:SUBLANES, :], b_ref[0:SUBLANES, :]), unroll=7)

    h = hc_ref[...]
    h_in = []
    for s in range(SUBLANES):
        h_in.append(h)
        h = ac[s:s + 1, :] * h + bc[s:s + 1, :]
    hc_ref[...] = h
    h_in = jnp.concatenate(h_in, axis=0)

    for j in range(seg):
        rows = slice(j * SUBLANES, (j + 1) * SUBLANES)
        hj = a_ref[rows, :] * h_in + b_ref[rows, :]
        for nb in range(RG_BLOCKS):
            h_ref[nb, rows, :] = hj[:, nb * bw:(nb + 1) * bw]
    g = seg // SUBLANES
    for nb in range(RG_BLOCKS):
        cs = slice(nb * bw, (nb + 1) * bw)
        h = jnp.concatenate(
            [h_ref[nb, pl.ds((i % g) * SUBLANES * SUBLANES + i // g, SUBLANES, stride=SUBLANES), :]
             for i in range(seg)], axis=0)
        o_ref[:, cs] = (h * _gelu_tanh(ry_ref[:, cs])).astype(o_ref.dtype)


def _rglru(proj, rg_conv, rg_conv_b, w_a, b_a, w_x, b_x, lam, batch, seq, tb=512):
    n = proj.shape[0]
    nt = seq // tb
    bw = D_RG // RG_BLOCKS
    full = lambda shape: pl.BlockSpec(shape, lambda b, t: (0,) * len(shape))
    return pl.pallas_call(
        functools.partial(_rglru_kernel, tb=tb),
        grid=(batch, nt),
        in_specs=[pl.BlockSpec((tb, bw), lambda b, t, j=j: (b * nt + t, COL_RX // bw + j))
                  for j in range(RG_BLOCKS)] + [
            pl.BlockSpec((tb, D_RG), lambda b, t: (b * nt + t, COL_RY // D_RG)),
            full((CONV_WIDTH, D_RG)), full((1, D_RG)),
            full((RG_BLOCKS, bw, bw)), full((1, D_RG)),
            full((RG_BLOCKS, bw, bw)), full((1, D_RG)),
            full((1, D_RG)),
        ],
        out_specs=pl.BlockSpec((tb, D_RG), lambda b, t: (b * nt + t, 0)),
        out_shape=jax.ShapeDtypeStruct((n, D_RG), BF16),
        scratch_shapes=[
            pltpu.VMEM((1, D_RG), F32),
            pltpu.VMEM(((CONV_WIDTH - 1) * SUBLANES, D_RG), F32),
            pltpu.VMEM((tb, D_RG), F32),
            pltpu.VMEM((tb, D_RG), F32),
            pltpu.VMEM((RG_BLOCKS, tb, bw), F32),
        ],
        compiler_params=_cparams(2),
        name="rglru",
    )(*([proj] * (RG_BLOCKS + 1)), rg_conv, rg_conv_b, w_a, b_a, w_x, b_x, lam)


def _mem_kv_kernel(m_ref, g_ref, w_ref, o_ref):
    mn = _rmsnorm(m_ref[...], g_ref[...]).astype(BF16)
    for c0 in range(0, o_ref.shape[1], 512):
        o_ref[:, c0:c0 + 512] = jnp.dot(mn, w_ref[:, c0:c0 + 512].astype(BF16),
                                        preferred_element_type=F32).astype(o_ref.dtype)


def _mem_kv(mem2, gain, w_ckv):
    n, d = mem2.shape
    return pl.pallas_call(
        _mem_kv_kernel,
        grid=(1,),
        in_specs=[
            pl.BlockSpec((n, d), lambda i: (0, 0)),
            pl.BlockSpec((1, d), lambda i: (0, 0)),
            pl.BlockSpec((d, 2 * d), lambda i: (0, 0)),
        ],
        out_specs=pl.BlockSpec((n, 2 * d), lambda i: (0, 0)),
        out_shape=jax.ShapeDtypeStruct((n, 2 * d), BF16),
        compiler_params=_cparams(1),
        name="mem_kv",
    )(mem2, gain, w_ckv)


ROUTE_E = 0
ROUTE_RANK = TOP_K
ROUTE_GATE = 2 * TOP_K
ROUTE_ROWS = 16


def _post_mix_kernel(dn_ref, rg_ref, x_ref, wo_f32, gx_ref, wq_f32, kv_ref, wco_f32, gm_ref,
                     wrh_ref, wrl_ref, br_ref, h2_ref, xn_ref, route_ref, routet_ref, cnt_ref,
                     carry_ref, wo_ref, wq_ref, wco_ref):
    i = pl.program_id(0)
    tm, d = x_ref.shape
    hd = d // XA_HEADS

    @pl.when(i == 0)
    def _():
        carry_ref[...] = jnp.zeros_like(carry_ref)
        wo_ref[...] = wo_f32[...].astype(BF16)
        wq_ref[...] = wq_f32[...].astype(BF16)
        wco_ref[...] = wco_f32[...].astype(BF16)

    h1 = (x_ref[...] + jnp.dot(dn_ref[...], wo_ref[0:D_DN, :], preferred_element_type=F32)
          + jnp.dot(rg_ref[...], wo_ref[D_DN:, :], preferred_element_type=F32))

    hn = _rmsnorm(h1, gx_ref[...]).astype(BF16)
    q = jnp.dot(hn, wq_ref[...], preferred_element_type=F32)
    heads = []
    for hh in range(XA_HEADS):
        cs = slice(hh * hd, (hh + 1) * hd)
        s = _mm_nt(q[:, cs], kv_ref[:, cs]) * (hd ** -0.5)
        p = jnp.exp(s - jnp.max(s, axis=-1, keepdims=True))
        p = p / jnp.sum(p, axis=-1, keepdims=True)
        heads.append(_mm(p, kv_ref[:, d + hh * hd:d + (hh + 1) * hd]).astype(BF16))
    o = jnp.concatenate(heads, axis=1)
    h2 = h1 + jnp.dot(o, wco_ref[...], preferred_element_type=F32)
    h2_ref[...] = h2

    xn = _rmsnorm(h2, gm_ref[...])
    _store_row_tiles(xn_ref, xn)
    logits = _mm_split(xn, wrh_ref[...], wrl_ref[...]) + br_ref[...]
    lane = lax.broadcasted_iota(I32, logits.shape, 1)
    lane_f = lane.astype(F32)
    neg = jnp.float32(-jnp.inf)
    lg = jnp.where(lane < N_EXPERTS, logits, neg)
    vals, idxs, hots = [], [], []
    for _ in range(TOP_K):
        m = jnp.max(lg, axis=-1, keepdims=True)
        idx = jnp.min(jnp.where(lg == m, lane_f, float(LANES)), axis=-1, keepdims=True)
        hot = lane_f == idx
        lg = jnp.where(hot, neg, lg)
        vals.append(m)
        idxs.append(idx)
        hots.append(hot)
    es = [jnp.exp(v - vals[0]) for v in vals]
    den = es[0] + es[1] + es[2] + es[3]
    gates = [e / den for e in es]

    chosen = jnp.zeros(logits.shape, F32)
    for hot in hots:
        chosen = chosen + jnp.where(hot, 1.0, 0.0)
    r2 = lax.broadcasted_iota(I32, (tm, tm), 0)
    c2 = lax.broadcasted_iota(I32, (tm, tm), 1)
    before = _mm(jnp.where(r2 > c2, 1.0, 0.0), chosen) + carry_ref[0:1, :]
    ranks = [jnp.sum(jnp.where(hot, before, 0.0), axis=-1, keepdims=True) for hot in hots]
    carry_ref[...] = carry_ref[...] + jnp.sum(chosen, axis=0, keepdims=True)
    cnt_ref[...] = carry_ref[...]

    rec = jnp.zeros(logits.shape, F32)
    for kk in range(TOP_K):
        rec = jnp.where(lane == ROUTE_E + kk, idxs[kk], rec)
        rec = jnp.where(lane == ROUTE_RANK + kk, ranks[kk], rec)
        rec = jnp.where(lane == ROUTE_GATE + kk, gates[kk], rec)
    route_ref[...] = rec
    routet_ref[...] = rec.T[0:ROUTE_ROWS, :]


def _post_mix(dn, rg, x2, w_out, g_cross, w_cq, kv, w_co, g_moe, w_r_hi, w_r_lo, b_r, seq, mem_len,
              tm=512):
    n, d = x2.shape
    per_b = seq // tm
    full = lambda shape: pl.BlockSpec(shape, lambda i: (0,) * len(shape))
    once = lambda shape: pl.BlockSpec(shape, lambda i: (0,) * len(shape),
                                      pipeline_mode=pl.Buffered(1))
    return pl.pallas_call(
        _post_mix_kernel,
        grid=(n // tm,),
        in_specs=[
            pl.BlockSpec((tm, D_DN), lambda i: (i, 0)),
            pl.BlockSpec((tm, D_RG), lambda i: (i, 0)),
            pl.BlockSpec((tm, d), lambda i: (i, 0)),
            once((d, d)), full((1, d)), once((d, d)),
            pl.BlockSpec((mem_len, 2 * d), lambda i: (i // per_b, 0)),
            once((d, d)), full((1, d)),
            full((d, LANES)), full((d, LANES)), full((1, LANES)),
        ],
        out_specs=[
            pl.BlockSpec((tm, d), lambda i: (i, 0)),
            pl.BlockSpec((tm * (d // LANES), LANES), lambda i: (i, 0)),
            pl.BlockSpec((tm, LANES), lambda i: (i, 0)),
            pl.BlockSpec((ROUTE_ROWS, tm), lambda i: (0, i)),
            pl.BlockSpec((SUBLANES, LANES), lambda i: (0, 0)),
        ],
        out_shape=[
            jax.ShapeDtypeStruct((n, d), F32),
            jax.ShapeDtypeStruct((n * (d // LANES), LANES), F32),
            jax.ShapeDtypeStruct((n, LANES), F32),
            jax.ShapeDtypeStruct((ROUTE_ROWS, n), F32),
            jax.ShapeDtypeStruct((SUBLANES, LANES), F32),
        ],
        scratch_shapes=[pltpu.VMEM((SUBLANES, LANES), F32),
                        pltpu.VMEM((d, d), BF16), pltpu.VMEM((d, d), BF16), pltpu.VMEM((d, d), BF16)],
        compiler_params=_cparams(1),
        name="post_mix",
    )(dn, rg, x2, w_out, g_cross, w_cq, kv, w_co, g_moe, w_r_hi, w_r_lo, b_r)


def _dest_kernel(start_ref, routet_ref, dest_ref):
    e = routet_ref[ROUTE_E:ROUTE_E + SUBLANES, :]
    rank = routet_ref[ROUTE_RANK:ROUTE_RANK + SUBLANES, :]
    base = jnp.zeros(e.shape, F32)
    for j in range(N_EXPERTS):
        base = jnp.where(e == float(j), start_ref[j].astype(F32), base)
    row = lax.broadcasted_iota(I32, e.shape, 0)
    dest_ref[...] = jnp.where(row < TOP_K, (base + rank).astype(I32), 0)


def _dest(pad_start, route_t, tm=2048):
    n = route_t.shape[1]
    grid_spec = pltpu.PrefetchScalarGridSpec(
        num_scalar_prefetch=1,
        grid=(n // tm,),
        in_specs=[pl.BlockSpec((ROUTE_ROWS, tm), lambda i, st: (0, i))],
        out_specs=pl.BlockSpec((SUBLANES, tm), lambda i, st: (0, i)),
    )
    return pl.pallas_call(
        _dest_kernel,
        grid_spec=grid_spec,
        out_shape=jax.ShapeDtypeStruct((SUBLANES, n), I32),
        compiler_params=_cparams(1),
        name="dest",
    )(pad_start, route_t)


def _row_tile(ref, r, rt):
    return ref.at[pl.ds(pl.multiple_of(r * rt, rt), rt)]


def _dispatch_kernel(meta_ref, dest_ref, xn_ref, xs_ref, zero_ref, sem, zsem, *, tm, rt, n_blocks):
    i = pl.program_id(0)
    bm = MOE_BM * rt

    def per_token(tk, carry):
        src = _row_tile(xn_ref, tk, rt)
        for kk in range(TOP_K):
            r = dest_ref[kk, tk]
            pltpu.make_async_copy(src, _row_tile(xs_ref, r, rt), sem).start(priority=kk % 2)
        return carry

    lax.fori_loop(0, tm, per_token, 0, unroll=4)

    @pl.when(i == 0)
    def _():
        zero_ref[...] = jnp.zeros_like(zero_ref)

        def pad_copy(r):
            return pltpu.make_async_copy(zero_ref.at[pl.ds(0, rt)], _row_tile(xs_ref, r, rt), zsem)

        def per_expert(e, total):
            r0 = meta_ref[e]
            cnt = meta_ref[N_EXPERTS + e]

            def one(r, carry):
                pad_copy(r0 + r).start()
                return carry

            lax.fori_loop(0, cnt, one, 0)
            return total + cnt

        total = lax.fori_loop(0, N_EXPERTS, per_expert, 0)

        def drain(r, carry):
            pad_copy(0).wait()
            return carry

        lax.fori_loop(0, total, drain, 0)

        def tail_copy(b):
            return pltpu.make_async_copy(
                zero_ref, xs_ref.at[pl.ds(pl.multiple_of(b * bm, bm), bm)], zsem)

        n_used = meta_ref[2 * N_EXPERTS]

        def tail(b, carry):
            tail_copy(b).start()
            return carry

        lax.fori_loop(n_used, n_blocks, tail, 0)

        def tail_drain(b, carry):
            tail_copy(0).wait()
            return carry

        lax.fori_loop(n_used, n_blocks, tail_drain, 0)

    for _ in range(TOP_K):
        pltpu.make_async_copy(xn_ref, xs_ref.at[pl.ds(0, tm * rt)], sem).wait()


def _dispatch(meta, dest, xn_t, n_rows, d, tm=256):
    rt = d // LANES
    n = xn_t.shape[0] // rt
    n_blocks = n_rows // MOE_BM
    grid_spec = pltpu.PrefetchScalarGridSpec(
        num_scalar_prefetch=1,
        grid=(n // tm,),
        in_specs=[
            pl.BlockSpec((SUBLANES, tm), lambda i, meta: (0, i), memory_space=pltpu.SMEM),
            pl.BlockSpec((tm * rt, LANES), lambda i, meta: (i, 0)),
        ],
        out_specs=pl.BlockSpec(memory_space=pl.ANY),
        scratch_shapes=[
            pltpu.VMEM((MOE_BM * rt, LANES), F32),
            pltpu.SemaphoreType.DMA,
            pltpu.SemaphoreType.DMA,
        ],
    )
    return pl.pallas_call(
        functools.partial(_dispatch_kernel, tm=tm, rt=rt, n_blocks=n_blocks),
        grid_spec=grid_spec,
        out_shape=jax.ShapeDtypeStruct((n_rows * rt, LANES), F32),
        compiler_params=_cparams(1),
        name="dispatch",
    )(meta, dest, xn_t)


def _experts_kernel(be_ref, nu_ref, nxt_ref, par_ref, x_ref, bg_ref, bu_ref, bd_ref,
                    wg_hbm, wu_hbm, wd_hbm, y_ref, wf_ref, wb_ref, sem):
    i = pl.program_id(0)
    used = i < nu_ref[0]
    e = be_ref[i]
    changed = (i == 0) | (e != be_ref[jnp.maximum(i - 1, 0)])

    def weight_copies(expert, slot):
        return [pltpu.make_async_copy(w.at[expert], wf_ref.at[slot, j], sem.at[slot])
                for j, w in enumerate((wg_hbm, wu_hbm, wd_hbm))]

    @pl.when(i == 0)
    def _():
        for cp in weight_copies(e, par_ref[e]):
            cp.start()

    @pl.when(used & changed)
    def _():
        slot = par_ref[e]
        for cp in weight_copies(e, slot):
            cp.wait()
        nxt = nxt_ref[e]

        @pl.when(nxt < N_EXPERTS)
        def _():
            for cp in weight_copies(nxt, 1 - slot):
                cp.start()

        for j in range(3):
            wb_ref[j] = wf_ref[slot, j].astype(BF16)

    @pl.when(used)
    def _():
        d = wb_ref.shape[1]
        x = _load_row_tiles(x_ref, MOE_BM, d).astype(BF16)
        gt = jnp.minimum(jnp.dot(x, wb_ref[0], preferred_element_type=F32) + bg_ref[...],
                         SWIGLU_LIMIT)
        up = jnp.clip(jnp.dot(x, wb_ref[1], preferred_element_type=F32) + bu_ref[...],
                      -SWIGLU_LIMIT, SWIGLU_LIMIT)
        hid = (up + 1.0) * (gt * _sigmoid(SWIGLU_ALPHA * gt))
        _store_row_tiles(y_ref, jnp.dot(hid.astype(BF16), wb_ref[2],
                                        preferred_element_type=F32) + bd_ref[...])

    @pl.when(jnp.logical_not(used))
    def _():
        y_ref[...] = jnp.zeros_like(y_ref)


def _experts(block_e, n_used, next_e, parity, xs_t, w_gate, b_gate, w_up, b_up, w_down, b_down):
    d, d_ff = w_gate.shape[1:]
    assert d == d_ff, "weight staging buffers assume square expert matrices"
    rt = d // LANES
    n_blocks = xs_t.shape[0] // (MOE_BM * rt)
    bspec = lambda m: pl.BlockSpec((None, 1, m), lambda i, be, nu, nx, pa: (be[i], 0, 0))
    hbm = pl.BlockSpec(memory_space=pl.ANY)
    grid_spec = pltpu.PrefetchScalarGridSpec(
        num_scalar_prefetch=4,
        grid=(n_blocks,),
        in_specs=[
            pl.BlockSpec((MOE_BM * rt, LANES),
                         lambda i, be, nu, nx, pa: (jnp.maximum(jnp.minimum(i, nu[0] - 1), 0), 0)),
            bspec(d_ff), bspec(d_ff), bspec(d),
            hbm, hbm, hbm,
        ],
        out_specs=pl.BlockSpec((MOE_BM * rt, LANES), lambda i, be, nu, nx, pa: (i, 0)),
        scratch_shapes=[
            pltpu.VMEM((2, 3, d, d_ff), F32),
            pltpu.VMEM((3, d, d_ff), BF16),
            pltpu.SemaphoreType.DMA((2,)),
        ],
    )
    return pl.pallas_call(
        _experts_kernel,
        grid_spec=grid_spec,
        out_shape=jax.ShapeDtypeStruct(xs_t.shape, F32),
        compiler_params=_cparams(1),
        name="experts",
    )(block_e, n_used, next_e, parity, xs_t, b_gate, b_up, b_down, w_gate, w_up, w_down)


def _combine_kernel(dcur_ref, dnext_ref, h2_ref, route_ref, gain_ref, ys_ref, o_ref,
                    buf_ref, sem, *, tm):
    i = pl.program_id(0)
    nsteps = pl.num_programs(0)
    slot = i & 1
    d = h2_ref.shape[1]
    rt = d // LANES

    def issue_all(dref, s):
        def per_token(tk, carry):
            for kk in range(TOP_K):
                r = dref[kk, tk]
                pltpu.make_async_copy(_row_tile(ys_ref, r, rt),
                                      _row_tile(buf_ref.at[s, kk], tk, rt),
                                      sem.at[s]).start(priority=kk % 2)
            return carry
        lax.fori_loop(0, tm, per_token, 0, unroll=4)

    @pl.when(i == 0)
    def _():
        issue_all(dcur_ref, 0)

    @pl.when(i + 1 < nsteps)
    def _():
        issue_all(dnext_ref, 1 - slot)

    for kk in range(TOP_K):
        pltpu.make_async_copy(ys_ref.at[pl.ds(0, tm * rt)], buf_ref.at[slot, kk],
                              sem.at[slot]).wait()

    rec = route_ref[...]
    acc = h2_ref[...]
    for kk in range(TOP_K):
        acc = acc + (rec[:, ROUTE_GATE + kk:ROUTE_GATE + kk + 1]
                     * _load_row_tiles(buf_ref.at[slot, kk], tm, d))
    o_ref[...] = _rmsnorm(acc, gain_ref[...])


def _combine(dest, h2, route, gain, ys_t, tm=128):
    n, d = h2.shape
    nsteps = n // tm
    return pl.pallas_call(
        functools.partial(_combine_kernel, tm=tm),
        grid=(nsteps,),
        in_specs=[
            pl.BlockSpec((SUBLANES, tm), lambda i: (0, i), memory_space=pltpu.SMEM),
            pl.BlockSpec((SUBLANES, tm), lambda i: (0, jnp.minimum(i + 1, nsteps - 1)),
                         memory_space=pltpu.SMEM),
            pl.BlockSpec((tm, d), lambda i: (i, 0)),
            pl.BlockSpec((tm, LANES), lambda i: (i, 0)),
            pl.BlockSpec((1, d), lambda i: (0, 0)),
            pl.BlockSpec(memory_space=pl.ANY),
        ],
        out_specs=pl.BlockSpec((tm, d), lambda i: (i, 0)),
        out_shape=jax.ShapeDtypeStruct((n, d), F32),
        scratch_shapes=[
            pltpu.VMEM((2, TOP_K, tm * (d // LANES), LANES), F32),
            pltpu.SemaphoreType.DMA((2,)),
        ],
        compiler_params=_cparams(1),
        name="combine",
    )(dest, dest, h2, route, gain, ys_t)


def kernel(x, mem, norm_mix, w_in, dn_conv, dn_a_log, dn_dt_bias, dn_norm, rg_conv, rg_conv_b, rg_w_a, rg_b_a, rg_w_x, rg_b_x, rg_lambda, w_out, norm_cross, norm_mem, w_cq, w_ckv, w_co, norm_moe, w_router, b_router, w_gate, b_gate, w_up, b_up, w_down, b_down, norm_final):
    batch, seq, d = x.shape
    mem_len = mem.shape[1]
    n = batch * seq
    assert w_in.shape[0] == 1, "single-layer trunk"
    x2 = x.reshape(n, d)

    wi = w_in[0]
    n_gate = 2 * DN_HEADS
    w_cat = jnp.concatenate(
        [wi[:, :4 * D_DN], wi[:, 4 * D_DN + n_gate:],
         jnp.pad(wi[:, 4 * D_DN:4 * D_DN + n_gate], ((0, 0), (0, LANES - n_gate)))],
        axis=1).astype(BF16)
    wbat = wi[:, 4 * D_DN:4 * D_DN + n_gate].T.astype(BF16)
    prow = (jnp.zeros((SUBLANES, LANES), F32)
            .at[0, DN_HEADS:n_gate].set(dn_a_log[0]).at[1, DN_HEADS:n_gate].set(dn_dt_bias[0]))
    pcol = (jnp.zeros((n_gate, LANES), F32)
            .at[DN_HEADS:, 0].set(dn_a_log[0]).at[DN_HEADS:, 1].set(dn_dt_bias[0]))

    proj, gbt = _in_proj(x2, norm_mix, w_cat, wbat, prow, pcol)
    dn = _deltanet(proj, gbt, dn_conv[0], dn_norm, batch, seq)
    rg = _rglru(proj, rg_conv[0], rg_conv_b, rg_w_a[0], rg_b_a[0].reshape(1, D_RG),
                rg_w_x[0], rg_b_x[0].reshape(1, D_RG), rg_lambda, batch, seq)
    kv = _mem_kv(mem.reshape(batch * mem_len, d), norm_mem, w_ckv[0])

    w_r = jnp.pad(w_router[0], ((0, 0), (0, LANES - N_EXPERTS)))
    w_r_hi = w_r.astype(BF16)
    w_r_lo = (w_r - w_r_hi.astype(F32)).astype(BF16)
    b_r = jnp.pad(b_router, ((0, 0), (0, LANES - N_EXPERTS)))
    h2, xn, route, route_t, counts = _post_mix(
        dn, rg, x2, w_out[0], norm_cross, w_cq[0], kv, w_co[0], norm_moe, w_r_hi, w_r_lo, b_r,
        seq, mem_len)

    n_blocks = n * TOP_K // MOE_BM + N_EXPERTS
    n_rows = n_blocks * MOE_BM
    cnt = counts[0, :N_EXPERTS].astype(I32)
    padded = (cnt + MOE_BM - 1) // MOE_BM * MOE_BM
    pad_end = jnp.cumsum(padded)
    pad_start = pad_end - padded
    n_used = (pad_end[-1:] // MOE_BM).astype(I32)
    block_e = jnp.minimum(
        jnp.sum(pad_end[None, :] <= (jnp.arange(n_blocks, dtype=I32) * MOE_BM)[:, None], axis=1),
        N_EXPERTS - 1).astype(I32)
    meta = jnp.concatenate([pad_start + cnt, padded - cnt, n_used]).astype(I32)

    dest = _dest(pad_start.astype(I32), route_t)
    xs = _dispatch(meta, dest, xn, n_rows, d)
    has = cnt > 0
    eid = jnp.where(has, jnp.arange(N_EXPERTS, dtype=I32), N_EXPERTS)
    after = lax.cummin(eid, axis=0, reverse=True)
    next_e = jnp.concatenate([after[1:], jnp.full((1,), N_EXPERTS, I32)]).astype(I32)
    parity = ((jnp.cumsum(has.astype(I32)) - 1) & 1).astype(I32)
    ys = _experts(block_e, n_used, next_e, parity, xs, w_gate[0], b_gate[0][:, None, :], w_up[0],
                  b_up[0][:, None, :], w_down[0], b_down[0][:, None, :])
    out = _combine(dest, h2, route, norm_final.reshape(1, d), ys)
    return out.reshape(batch, seq, d)
```

```python
import functools

import jax
import jax.numpy as jnp
from jax import lax
from jax.experimental import pallas as pl
from jax.experimental.pallas import tpu as pltpu

F32 = jnp.float32
BF16 = jnp.bfloat16
I32 = jnp.int32
U32 = jnp.uint32

EPS = 1e-6
LANES = 128
SUBLANES = 8
VMEM_LIMIT = 48 * 1024 * 1024

DN_HEADS = 4
DN_HEAD_DIM = 128
D_DN = DN_HEADS * DN_HEAD_DIM
D_RG = 512
RG_BLOCKS = 4
RG_C = 8.0
CONV_WIDTH = 4
XA_HEADS = 4
N_EXPERTS = 32
TOP_K = 4
SWIGLU_LIMIT = 7.0
SWIGLU_ALPHA = 1.702

DN_CHUNK = 128
INV_BASE = 16
MOE_BM = 256

COL_RX = 4 * D_DN
COL_RY = COL_RX + D_RG
COL_GB = COL_RY + D_RG
PROJ_W = COL_GB + LANES


def _cparams(n_axes=1):
    return pltpu.CompilerParams(
        dimension_semantics=("arbitrary",) * n_axes, vmem_limit_bytes=VMEM_LIMIT)


def _mm(a, b):
    return jnp.dot(a.astype(BF16), b.astype(BF16), preferred_element_type=F32)


def _mm_nt(a, b):
    return lax.dot_general(a.astype(BF16), b.astype(BF16), (((1,), (1,)), ((), ())),
                           preferred_element_type=F32)


def _rmsnorm(x, g):
    return x * lax.rsqrt(jnp.mean(x * x, axis=-1, keepdims=True) + EPS) * g


def _sigmoid(x):
    return 0.5 * jnp.tanh(0.5 * x) + 0.5


def _mm_split(a, b_hi, b_lo):
    a_hi = a.astype(BF16)
    a_lo = (a - a_hi.astype(F32)).astype(BF16)
    return (jnp.dot(a_hi, b_hi, preferred_element_type=F32)
            + jnp.dot(a_hi, b_lo, preferred_element_type=F32)
            + jnp.dot(a_lo, b_hi, preferred_element_type=F32))


def _softplus(x):
    return jnp.maximum(x, 0.0) + jnp.log1p(jnp.exp(-jnp.abs(x)))


def _load_row_tiles(ref, rows):
    rt = ref.shape[0] // rows
    return jnp.concatenate([ref[pl.ds(s, rows, stride=rt), :] for s in range(rt)], axis=1)


def _store_row_tiles(ref, val):
    rows, w = val.shape
    rt = w // LANES
    for s in range(rt):
        ref[pl.ds(s, rows, stride=rt), :] = val[:, s * LANES:(s + 1) * LANES]


def _pack_bf16_pairs(x):
    half = x.shape[1] // 2
    hi = pltpu.bitcast(x[:, :half].astype(BF16).astype(F32), U32)
    lo = pltpu.bitcast(x[:, half:].astype(BF16).astype(F32), U32)
    return hi | (lo >> 16)


def _unpack_bf16_pairs(p):
    hi = pltpu.bitcast(p & jnp.uint32(0xFFFF0000), F32)
    lo = pltpu.bitcast(p << 16, F32)
    return jnp.concatenate([hi, lo], axis=1).astype(BF16)


def _load_segmented(ref):
    seg = ref.shape[0] // SUBLANES
    return jnp.concatenate([ref[pl.ds(j, SUBLANES, stride=seg), :] for j in range(seg)], axis=0)


def _causal_conv_segmented(xp, halo_ref, cs, w):
    tb = xp.shape[0]
    ng = w.shape[0] - 1
    prev = halo_ref[:, cs]
    last = xp[tb - ng * SUBLANES:, :]
    halo_ref[:, cs] = last
    sub = lax.broadcasted_iota(I32, (SUBLANES, xp.shape[1]), 0)
    groups = []
    for g in range(ng):
        rows = slice(g * SUBLANES, (g + 1) * SUBLANES)
        groups.append(jnp.where(sub == 0, pltpu.roll(prev[rows], 1, 0),
                                pltpu.roll(last[rows], 1, 0)))
    ext = jnp.concatenate(groups + [xp], axis=0)
    y = w[ng:ng + 1] * xp
    for k in range(1, ng + 1):
        y = y + w[ng - k:ng - k + 1] * ext[(ng - k) * SUBLANES:(ng - k) * SUBLANES + tb]
    return y


def _in_proj_kernel(x_ref, g_ref, w_ref, wbat_ref, prow_ref, pcol_ref, proj_ref, gbt_ref):
    u = _rmsnorm(x_ref[...], g_ref[...]).astype(BF16)
    for c0 in range(0, COL_GB, 512):
        proj_ref[:, c0:c0 + 512] = jnp.dot(u, w_ref[:, c0:c0 + 512], preferred_element_type=F32)
    ba = jnp.dot(u, w_ref[:, COL_GB:PROJ_W], preferred_element_type=F32)
    lane = lax.broadcasted_iota(I32, ba.shape, 1)
    g = -jnp.exp(prow_ref[0:1, :]) * _softplus(ba + prow_ref[1:2, :])
    proj_ref[:, COL_GB:PROJ_W] = jnp.where(lane < DN_HEADS, _sigmoid(ba), g)
    bat = lax.dot_general(wbat_ref[...], u, (((1,), (1,)), ((), ())), preferred_element_type=F32)
    row = lax.broadcasted_iota(I32, bat.shape, 0)
    gt = -jnp.exp(pcol_ref[:, 0:1]) * _softplus(bat + pcol_ref[:, 1:2])
    gbt_ref[...] = jnp.where(row < DN_HEADS, _sigmoid(bat), gt)


def _in_proj(x2, gain, w_cat, wbat, prow, pcol, tm=512):
    n, d = x2.shape
    return pl.pallas_call(
        _in_proj_kernel,
        grid=(n // tm,),
        in_specs=[
            pl.BlockSpec((tm, d), lambda i: (i, 0)),
            pl.BlockSpec((1, d), lambda i: (0, 0)),
            pl.BlockSpec((d, PROJ_W), lambda i: (0, 0)),
            pl.BlockSpec((2 * DN_HEADS, d), lambda i: (0, 0)),
            pl.BlockSpec((SUBLANES, LANES), lambda i: (0, 0)),
            pl.BlockSpec((2 * DN_HEADS, LANES), lambda i: (0, 0)),
        ],
        out_specs=[
            pl.BlockSpec((tm, PROJ_W), lambda i: (i, 0)),
            pl.BlockSpec((2 * DN_HEADS, tm), lambda i: (0, i)),
        ],
        out_shape=[
            jax.ShapeDtypeStruct((n, PROJ_W), F32),
            jax.ShapeDtypeStruct((2 * DN_HEADS, n), F32),
        ],
        compiler_params=_cparams(1),
        name="in_proj",
    )(x2, gain, w_cat, wbat, prow, pcol)


def _deltanet_kernel(*refs, tb):
    n_grp = 3 * DN_HEADS
    qkv_refs = refs[:n_grp]
    (z_ref, gb_ref, gbt_ref, conv_ref, norm_ref, o_ref,
     s_ref, halo_ref, act_ref, gcc_ref, gcr_ref,
     a_ref, p_ref, d_ref, qk_ref, rhs_ref, u_ref, wq_ref, kdt_ref) = refs[n_grp:]
    t = pl.program_id(1)
    c = DN_CHUNK
    dh = DN_HEAD_DIM

    @pl.when(t == 0)
    def _():
        s_ref[...] = jnp.zeros_like(s_ref)
        halo_ref[...] = jnp.zeros_like(halo_ref)

    sub = lax.broadcasted_iota(I32, (SUBLANES, dh), 0)
    for grp in range(n_grp):
        cs = slice(grp * dh, (grp + 1) * dh)
        x = qkv_refs[grp][...]
        prev = halo_ref[:, cs]
        halo_ref[:, cs] = x[tb - SUBLANES:, :]
        y = conv_ref[CONV_WIDTH - 1:CONV_WIDTH, cs] * x
        for k in range(1, CONV_WIDTH):
            xs = pltpu.roll(x, k, 0)
            head = jnp.where(sub < k, pltpu.roll(prev, k, 0), xs[:SUBLANES])
            xs = jnp.concatenate([head, xs[SUBLANES:]], axis=0)
            y = y + conv_ref[CONV_WIDTH - 1 - k:CONV_WIDTH - k, cs] * xs
        y = y * _sigmoid(y)
        if grp < 2 * DN_HEADS:
            y = y * lax.rsqrt(jnp.sum(y * y, axis=-1, keepdims=True) + EPS)
        if grp < DN_HEADS:
            y = y * (dh ** -0.5)
        act_ref[grp] = y

    gcol = gb_ref[...]
    rpos = lax.broadcasted_iota(I32, gcol.shape, 0) & (c - 1)
    d = 1
    while d < c:
        gcol = gcol + jnp.where(rpos >= d, pltpu.roll(gcol, d, 0), 0.0)
        d *= 2
    gcc_ref[...] = gcol
    grow = gbt_ref[...]
    lpos = lax.broadcasted_iota(I32, grow.shape, 1) & (c - 1)
    d = 1
    while d < c:
        grow = grow + jnp.where(lpos >= d, pltpu.roll(grow, d, 1), 0.0)
        d *= 2
    gcr_ref[...] = grow

    row = lax.broadcasted_iota(I32, (c, c), 0)
    col = lax.broadcasted_iota(I32, (c, c), 1)
    causal = row >= col
    strict = row > col
    eye = jnp.where(row == col, 1.0, 0.0)
    gain = norm_ref[...]
    n_chunks = tb // c
    probs = [(ci, h) for ci in range(n_chunks) for h in range(DN_HEADS)]

    decay_last = []
    for p, (ci, h) in enumerate(probs):
        rows = slice(ci * c, (ci + 1) * c)
        q = act_ref[h, rows, :]
        k = act_ref[DN_HEADS + h, rows, :]
        v = act_ref[2 * DN_HEADS + h, rows, :]
        beta = gb_ref[rows, h:h + 1]
        gc = gcc_ref[rows, DN_HEADS + h:DN_HEADS + h + 1]
        gr = gcr_ref[DN_HEADS + h:DN_HEADS + h + 1, rows]
        g_last = gc[c - 1:c, :]
        decay = jnp.where(causal, jnp.exp(jnp.where(causal, gc - gr, 0.0)), 0.0)
        kb = k * beta
        both = _mm_nt(jnp.concatenate([kb, q], axis=0), k)
        a_ref[p] = jnp.where(strict, both[:c] * decay, 0.0)
        qk_ref[p] = (both[c:] * decay).astype(BF16)
        egc = jnp.exp(gc)
        rhs_ref[p] = jnp.concatenate([v * beta, kb * egc], axis=1).astype(BF16)
        wq_ref[p, c:, :] = (q * egc).astype(BF16)
        kdt_ref[p] = (k * jnp.exp(g_last - gc)).T.astype(BF16)
        decay_last.append(jnp.exp(g_last))

    shift = INV_BASE.bit_length() - 1
    blk = (row >> shift) == (col >> shift)
    for p in range(len(probs)):
        diag = jnp.where(blk, a_ref[p], 0.0)
        p_ref[p] = eye - diag
        d_ref[p] = _mm(diag, diag).astype(BF16)
    for it in range(shift - 1):
        for p in range(len(probs)):
            pw = d_ref[p]
            inv = p_ref[p]
            p_ref[p] = inv + _mm(inv, pw)
            if it < shift - 2:
                d_ref[p] = _mm(pw, pw).astype(BF16)
    s = INV_BASE
    while s < c:
        sh = s.bit_length() - 1
        off = ((row >> (sh + 1)) == (col >> (sh + 1))) & ((row >> sh) != (col >> sh))
        for p in range(len(probs)):
            d_ref[p] = _mm(p_ref[p], jnp.where(off, a_ref[p], 0.0)).astype(BF16)
        for p in range(len(probs)):
            inv = p_ref[p]
            p_ref[p] = inv - _mm(d_ref[p], inv)
        s *= 2
    for p in range(len(probs)):
        uw = _mm(p_ref[p], rhs_ref[p])
        u_ref[p] = uw[:, :dh]
        wq_ref[p, :c, :] = uw[:, dh:].astype(BF16)

    for ci in range(n_chunks):
        rows = slice(ci * c, (ci + 1) * c)
        ps = [ci * DN_HEADS + h for h in range(DN_HEADS)]
        s_old = [s_ref[h] for h in range(DN_HEADS)]
        ws = [_mm(wq_ref[p], s_old[h]) for h, p in enumerate(ps)]
        v_new = [u_ref[p] - ws[h][:c] for h, p in enumerate(ps)]
        outs = [ws[h][c:] + _mm(qk_ref[p], v_new[h]) for h, p in enumerate(ps)]
        for h, p in enumerate(ps):
            s_ref[h] = s_old[h] * decay_last[p] + _mm(kdt_ref[p], v_new[h])
        for h in range(DN_HEADS):
            o = _rmsnorm(outs[h], gain)
            zz = z_ref[rows, h * dh:(h + 1) * dh]
            o_ref[rows, h * dh:(h + 1) * dh] = (o * (zz * _sigmoid(zz))).astype(o_ref.dtype)


def _deltanet(proj, gbt, dn_conv, dn_norm, batch, seq, tb=512):
    n = proj.shape[0]
    nt = seq // tb
    c = DN_CHUNK
    n_prob = (tb // c) * DN_HEADS
    n_grp = 3 * DN_HEADS
    grp = lambda j: pl.BlockSpec((tb, DN_HEAD_DIM), lambda b, t, j=j: (b * nt + t, j))
    return pl.pallas_call(
        functools.partial(_deltanet_kernel, tb=tb),
        grid=(batch, nt),
        in_specs=[grp(j) for j in range(n_grp)] + [
            pl.BlockSpec((tb, D_DN), lambda b, t: (b * nt + t, 3)),
            pl.BlockSpec((tb, LANES), lambda b, t: (b * nt + t, COL_GB // LANES)),
            pl.BlockSpec((2 * DN_HEADS, tb), lambda b, t: (0, b * nt + t)),
            pl.BlockSpec((CONV_WIDTH, 3 * D_DN), lambda b, t: (0, 0)),
            pl.BlockSpec((1, DN_HEAD_DIM), lambda b, t: (0, 0)),
        ],
        out_specs=pl.BlockSpec((tb, D_DN), lambda b, t: (b * nt + t, 0)),
        out_shape=jax.ShapeDtypeStruct((n, D_DN), BF16),
        scratch_shapes=[
            pltpu.VMEM((DN_HEADS, DN_HEAD_DIM, DN_HEAD_DIM), F32),
            pltpu.VMEM((SUBLANES, 3 * D_DN), F32),
            pltpu.VMEM((n_grp, tb, DN_HEAD_DIM), F32),
            pltpu.VMEM((tb, LANES), F32),
            pltpu.VMEM((2 * DN_HEADS, tb), F32),
            pltpu.VMEM((n_prob, c, c), F32),
            pltpu.VMEM((n_prob, c, c), F32),
            pltpu.VMEM((n_prob, c, c), BF16),
            pltpu.VMEM((n_prob, c, c), BF16),
            pltpu.VMEM((n_prob, c, 2 * DN_HEAD_DIM), BF16),
            pltpu.VMEM((n_prob, c, DN_HEAD_DIM), F32),
            pltpu.VMEM((n_prob, 2 * c, DN_HEAD_DIM), BF16),
            pltpu.VMEM((n_prob, DN_HEAD_DIM, c), BF16),
        ],
        compiler_params=_cparams(2),
        name="deltanet",
    )(*([proj] * (n_grp + 2)), gbt, dn_conv, dn_norm)


def _gelu_tanh(x):
    return 0.5 * x * (1.0 + jnp.tanh(0.7978845608028654 * (x + 0.044715 * (x * x * x))))


def _rglru_kernel(*refs, tb):
    rx_refs = refs[:RG_BLOCKS]
    (ry_ref, conv_ref, convb_ref, wa_ref, ba_ref, wx_ref, bx_ref, lam_ref,
     o_ref, hc_ref, halo_ref, a_ref, b_ref, h_ref) = refs[RG_BLOCKS:]
    t = pl.program_id(1)
    bw = D_RG // RG_BLOCKS
    seg = tb // SUBLANES

    @pl.when(t == 0)
    def _():
        hc_ref[...] = jnp.zeros_like(hc_ref)
        halo_ref[...] = jnp.zeros_like(halo_ref)

    log_sig = -_softplus(-lam_ref[...])
    rowi = lax.broadcasted_iota(I32, (tb, bw), 0)
    seq_start_row = jnp.where(t == 0, 0, -1)
    for nb in range(RG_BLOCKS):
        cs = slice(nb * bw, (nb + 1) * bw)
        xb = (_causal_conv_segmented(_load_segmented(rx_refs[nb]), halo_ref, cs,
                                     conv_ref[:, cs]) + convb_ref[:, cs])
        r = _sigmoid(_mm(xb, wa_ref[nb]) + ba_ref[:, cs])
        gi = _sigmoid(_mm(xb, wx_ref[nb]) + bx_ref[:, cs])
        log_a = RG_C * r * log_sig[:, cs]
        a = jnp.exp(log_a)
        y = jnp.tanh(-log_a) * (1.0 + a * a)
        mult = jnp.where(y > 0.0, y * lax.rsqrt(y), 0.0)
        mult = jnp.where(rowi == seq_start_row, 1.0, mult)
        a_ref[:, cs] = a
        b_ref[:, cs] = mult * (gi * xb)

    def scan(j, carry):
        ac, bc = carry
        rows = pl.ds(pl.multiple_of(j * SUBLANES, SUBLANES), SUBLANES)
        aj = a_ref[rows, :]
        bc = aj * bc + b_ref[rows, :]
        ac = aj * ac
        a_ref[rows, :] = ac
        b_ref[rows, :] = bc
        return ac, bc

    ac, bc = lax.fori_loop(1, seg, scan, (a_ref[0:SUBLANES, :], b_ref[0:SUBLANES, :]), unroll=7)

    h = hc_ref[...]
    h_in = []
    for s in range(SUBLANES):
        h_in.append(h)
        h = ac[s:s + 1, :] * h + bc[s:s + 1, :]
    hc_ref[...] = h
    h_in = jnp.concatenate(h_in, axis=0)

    for j in range(seg):
        rows = slice(j * SUBLANES, (j + 1) * SUBLANES)
        hj = a_ref[rows, :] * h_in + b_ref[rows, :]
        for nb in range(RG_BLOCKS):
            h_ref[nb, rows, :] = hj[:, nb * bw:(nb + 1) * bw]
    g = seg // SUBLANES
    for nb in range(RG_BLOCKS):
        cs = slice(nb * bw, (nb + 1) * bw)
        h = jnp.concatenate(
            [h_ref[nb, pl.ds((i % g) * SUBLANES * SUBLANES + i // g, SUBLANES, stride=SUBLANES), :]
             for i in range(seg)], axis=0)
        o_ref[:, cs] = (h * _gelu_tanh(ry_ref[:, cs])).astype(o_ref.dtype)


def _rglru(proj, rg_conv, rg_conv_b, w_a, b_a, w_x, b_x, lam, batch, seq, tb=512):
    n = proj.shape[0]
    nt = seq // tb
    bw = D_RG // RG_BLOCKS
    full = lambda shape: pl.BlockSpec(shape, lambda b, t: (0,) * len(shape))
    return pl.pallas_call(
        functools.partial(_rglru_kernel, tb=tb),
        grid=(batch, nt),
        in_specs=[pl.BlockSpec((tb, bw), lambda b, t, j=j: (b * nt + t, COL_RX // bw + j))
                  for j in range(RG_BLOCKS)] + [
            pl.BlockSpec((tb, D_RG), lambda b, t: (b * nt + t, COL_RY // D_RG)),
            full((CONV_WIDTH, D_RG)), full((1, D_RG)),
            full((RG_BLOCKS, bw, bw)), full((1, D_RG)),
            full((RG_BLOCKS, bw, bw)), full((1, D_RG)),
            full((1, D_RG)),
        ],
        out_specs=pl.BlockSpec((tb, D_RG), lambda b, t: (b * nt + t, 0)),
        out_shape=jax.ShapeDtypeStruct((n, D_RG), BF16),
        scratch_shapes=[
            pltpu.VMEM((1, D_RG), F32),
            pltpu.VMEM(((CONV_WIDTH - 1) * SUBLANES, D_RG), F32),
            pltpu.VMEM((tb, D_RG), F32),
            pltpu.VMEM((tb, D_RG), F32),
            pltpu.VMEM((RG_BLOCKS, tb, bw), F32),
        ],
        compiler_params=_cparams(2),
        name="rglru",
    )(*([proj] * (RG_BLOCKS + 1)), rg_conv, rg_conv_b, w_a, b_a, w_x, b_x, lam)


def _mem_kv_kernel(m_ref, g_ref, w_ref, o_ref):
    mn = _rmsnorm(m_ref[...], g_ref[...]).astype(BF16)
    for c0 in range(0, o_ref.shape[1], 512):
        o_ref[:, c0:c0 + 512] = jnp.dot(mn, w_ref[:, c0:c0 + 512].astype(BF16),
                                        preferred_element_type=F32).astype(o_ref.dtype)


def _mem_kv(mem2, gain, w_ckv):
    n, d = mem2.shape
    return pl.pallas_call(
        _mem_kv_kernel,
        grid=(1,),
        in_specs=[
            pl.BlockSpec((n, d), lambda i: (0, 0)),
            pl.BlockSpec((1, d), lambda i: (0, 0)),
            pl.BlockSpec((d, 2 * d), lambda i: (0, 0)),
        ],
        out_specs=pl.BlockSpec((n, 2 * d), lambda i: (0, 0)),
        out_shape=jax.ShapeDtypeStruct((n, 2 * d), BF16),
        compiler_params=_cparams(1),
        name="mem_kv",
    )(mem2, gain, w_ckv)


ROUTE_E = 0
ROUTE_RANK = TOP_K
ROUTE_GATE = 2 * TOP_K
ROUTE_ROWS = 16


def _post_mix_kernel(dn_ref, rg_ref, x_ref, wo_f32, gx_ref, wq_f32, kv_ref, wco_f32, gm_ref,
                     wrh_ref, wrl_ref, br_ref, h2_ref, xn_ref, route_ref, routet_ref, cnt_ref,
                     carry_ref, wo_ref, wq_ref, wco_ref):
    i = pl.program_id(0)
    tm, d = x_ref.shape
    hd = d // XA_HEADS

    @pl.when(i == 0)
    def _():
        carry_ref[...] = jnp.zeros_like(carry_ref)
        wo_ref[...] = wo_f32[...].astype(BF16)
        wq_ref[...] = wq_f32[...].astype(BF16)
        wco_ref[...] = wco_f32[...].astype(BF16)

    h1 = (x_ref[...] + jnp.dot(dn_ref[...], wo_ref[0:D_DN, :], preferred_element_type=F32)
          + jnp.dot(rg_ref[...], wo_ref[D_DN:, :], preferred_element_type=F32))

    hn = _rmsnorm(h1, gx_ref[...]).astype(BF16)
    q = jnp.dot(hn, wq_ref[...], preferred_element_type=F32)
    heads = []
    for hh in range(XA_HEADS):
        cs = slice(hh * hd, (hh + 1) * hd)
        s = _mm_nt(q[:, cs], kv_ref[:, cs]) * (hd ** -0.5)
        p = jnp.exp(s - jnp.max(s, axis=-1, keepdims=True))
        p = p / jnp.sum(p, axis=-1, keepdims=True)
        heads.append(_mm(p, kv_ref[:, d + hh * hd:d + (hh + 1) * hd]).astype(BF16))
    o = jnp.concatenate(heads, axis=1)
    h2 = h1 + jnp.dot(o, wco_ref[...], preferred_element_type=F32)
    h2_ref[...] = h2

    xn = _rmsnorm(h2, gm_ref[...])
    _store_row_tiles(xn_ref, _pack_bf16_pairs(xn))
    logits = _mm_split(xn, wrh_ref[...], wrl_ref[...]) + br_ref[...]
    lane = lax.broadcasted_iota(I32, logits.shape, 1)
    lane_f = lane.astype(F32)
    neg = jnp.float32(-jnp.inf)
    lg = jnp.where(lane < N_EXPERTS, logits, neg)
    vals, idxs, hots = [], [], []
    for _ in range(TOP_K):
        m = jnp.max(lg, axis=-1, keepdims=True)
        idx = jnp.min(jnp.where(lg == m, lane_f, float(LANES)), axis=-1, keepdims=True)
        hot = lane_f == idx
        lg = jnp.where(hot, neg, lg)
        vals.append(m)
        idxs.append(idx)
        hots.append(hot)
    es = [jnp.exp(v - vals[0]) for v in vals]
    den = es[0] + es[1] + es[2] + es[3]
    gates = [e / den for e in es]

    chosen = jnp.zeros(logits.shape, F32)
    for hot in hots:
        chosen = chosen + jnp.where(hot, 1.0, 0.0)
    r2 = lax.broadcasted_iota(I32, (tm, tm), 0)
    c2 = lax.broadcasted_iota(I32, (tm, tm), 1)
    before = _mm(jnp.where(r2 > c2, 1.0, 0.0), chosen) + carry_ref[0:1, :]
    ranks = [jnp.sum(jnp.where(hot, before, 0.0), axis=-1, keepdims=True) for hot in hots]
    carry_ref[...] = carry_ref[...] + jnp.sum(chosen, axis=0, keepdims=True)
    cnt_ref[...] = carry_ref[...]

    rec = jnp.zeros(logits.shape, F32)
    for kk in range(TOP_K):
        rec = jnp.where(lane == ROUTE_E + kk, idxs[kk], rec)
        rec = jnp.where(lane == ROUTE_RANK + kk, ranks[kk], rec)
        rec = jnp.where(lane == ROUTE_GATE + kk, gates[kk], rec)
    route_ref[...] = rec
    routet_ref[...] = rec.T[0:ROUTE_ROWS, :]


def _post_mix(dn, rg, x2, w_out, g_cross, w_cq, kv, w_co, g_moe, w_r_hi, w_r_lo, b_r, seq, mem_len,
              tm=512):
    n, d = x2.shape
    per_b = seq // tm
    rt_x = d // 2 // LANES
    full = lambda shape: pl.BlockSpec(shape, lambda i: (0,) * len(shape))
    once = lambda shape: pl.BlockSpec(shape, lambda i: (0,) * len(shape),
                                      pipeline_mode=pl.Buffered(1))
    return pl.pallas_call(
        _post_mix_kernel,
        grid=(n // tm,),
        in_specs=[
            pl.BlockSpec((tm, D_DN), lambda i: (i, 0)),
            pl.BlockSpec((tm, D_RG), lambda i: (i, 0)),
            pl.BlockSpec((tm, d), lambda i: (i, 0)),
            once((d, d)), full((1, d)), once((d, d)),
            pl.BlockSpec((mem_len, 2 * d), lambda i: (i // per_b, 0)),
            once((d, d)), full((1, d)),
            full((d, LANES)), full((d, LANES)), full((1, LANES)),
        ],
        out_specs=[
            pl.BlockSpec((tm, d), lambda i: (i, 0)),
            pl.BlockSpec((tm * rt_x, LANES), lambda i: (i, 0)),
            pl.BlockSpec((tm, LANES), lambda i: (i, 0)),
            pl.BlockSpec((ROUTE_ROWS, tm), lambda i: (0, i)),
            pl.BlockSpec((SUBLANES, LANES), lambda i: (0, 0)),
        ],
        out_shape=[
            jax.ShapeDtypeStruct((n, d), F32),
            jax.ShapeDtypeStruct((n * rt_x, LANES), U32),
            jax.ShapeDtypeStruct((n, LANES), F32),
            jax.ShapeDtypeStruct((ROUTE_ROWS, n), F32),
            jax.ShapeDtypeStruct((SUBLANES, LANES), F32),
        ],
        scratch_shapes=[pltpu.VMEM((SUBLANES, LANES), F32),
                        pltpu.VMEM((d, d), BF16), pltpu.VMEM((d, d), BF16), pltpu.VMEM((d, d), BF16)],
        compiler_params=_cparams(1),
        name="post_mix",
    )(dn, rg, x2, w_out, g_cross, w_cq, kv, w_co, g_moe, w_r_hi, w_r_lo, b_r)


def _dest_kernel(start_ref, routet_ref, dest_ref):
    e = routet_ref[ROUTE_E:ROUTE_E + SUBLANES, :]
    rank = routet_ref[ROUTE_RANK:ROUTE_RANK + SUBLANES, :]
    base = jnp.zeros(e.shape, F32)
    for j in range(N_EXPERTS):
        base = jnp.where(e == float(j), start_ref[j].astype(F32), base)
    row = lax.broadcasted_iota(I32, e.shape, 0)
    dest_ref[...] = jnp.where(row < TOP_K, (base + rank).astype(I32), 0)


def _dest(pad_start, route_t, tm=2048):
    n = route_t.shape[1]
    grid_spec = pltpu.PrefetchScalarGridSpec(
        num_scalar_prefetch=1,
        grid=(n // tm,),
        in_specs=[pl.BlockSpec((ROUTE_ROWS, tm), lambda i, st: (0, i))],
        out_specs=pl.BlockSpec((SUBLANES, tm), lambda i, st: (0, i)),
    )
    return pl.pallas_call(
        _dest_kernel,
        grid_spec=grid_spec,
        out_shape=jax.ShapeDtypeStruct((SUBLANES, n), I32),
        compiler_params=_cparams(1),
        name="dest",
    )(pad_start, route_t)


def _row_tile(ref, r, rt):
    return ref.at[pl.ds(pl.multiple_of(r * rt, rt), rt)]


def _dispatch_kernel(meta_ref, dest_ref, xn_ref, xs_ref, zero_ref, sem, zsem, *, tm, rt, n_blocks):
    i = pl.program_id(0)
    bm = MOE_BM * rt

    def per_token(tk, carry):
        src = _row_tile(xn_ref, tk, rt)
        for kk in range(TOP_K):
            r = dest_ref[kk, tk]
            pltpu.make_async_copy(src, _row_tile(xs_ref, r, rt), sem).start(priority=kk % 2)
        return carry

    lax.fori_loop(0, tm, per_token, 0, unroll=4)

    @pl.when(i == 0)
    def _():
        zero_ref[...] = jnp.zeros_like(zero_ref)

        def pad_copy(r):
            return pltpu.make_async_copy(zero_ref.at[pl.ds(0, rt)], _row_tile(xs_ref, r, rt), zsem)

        def per_expert(e, total):
            r0 = meta_ref[e]
            cnt = meta_ref[N_EXPERTS + e]

            def one(r, carry):
                pad_copy(r0 + r).start()
                return carry

            lax.fori_loop(0, cnt, one, 0)
            return total + cnt

        total = lax.fori_loop(0, N_EXPERTS, per_expert, 0)

        def drain(r, carry):
            pad_copy(0).wait()
            return carry

        lax.fori_loop(0, total, drain, 0)

        def tail_copy(b):
            return pltpu.make_async_copy(
                zero_ref, xs_ref.at[pl.ds(pl.multiple_of(b * bm, bm), bm)], zsem)

        n_used = meta_ref[2 * N_EXPERTS]

        def tail(b, carry):
            tail_copy(b).start()
            return carry

        lax.fori_loop(n_used, n_blocks, tail, 0)

        def tail_drain(b, carry):
            tail_copy(0).wait()
            return carry

        lax.fori_loop(n_used, n_blocks, tail_drain, 0)

    for _ in range(TOP_K):
        pltpu.make_async_copy(xn_ref, xs_ref.at[pl.ds(0, tm * rt)], sem).wait()


def _dispatch(meta, dest, xn_t, n_tok, n_rows, tm=256):
    rt = xn_t.shape[0] // n_tok
    n_blocks = n_rows // MOE_BM
    grid_spec = pltpu.PrefetchScalarGridSpec(
        num_scalar_prefetch=1,
        grid=(n_tok // tm,),
        in_specs=[
            pl.BlockSpec((SUBLANES, tm), lambda i, meta: (0, i), memory_space=pltpu.SMEM),
            pl.BlockSpec((tm * rt, LANES), lambda i, meta: (i, 0)),
        ],
        out_specs=pl.BlockSpec(memory_space=pl.ANY),
        scratch_shapes=[
            pltpu.VMEM((MOE_BM * rt, LANES), xn_t.dtype),
            pltpu.SemaphoreType.DMA,
            pltpu.SemaphoreType.DMA,
        ],
    )
    return pl.pallas_call(
        functools.partial(_dispatch_kernel, tm=tm, rt=rt, n_blocks=n_blocks),
        grid_spec=grid_spec,
        out_shape=jax.ShapeDtypeStruct((n_rows * rt, LANES), xn_t.dtype),
        compiler_params=_cparams(1),
        name="dispatch",
    )(meta, dest, xn_t)


def _experts_kernel(be_ref, nu_ref, nxt_ref, par_ref, x_ref, bg_ref, bu_ref, bd_ref,
                    wg_hbm, wu_hbm, wd_hbm, y_ref, wf_ref, wb_ref, sem):
    i = pl.program_id(0)
    used = i < nu_ref[0]
    e = be_ref[i]
    changed = (i == 0) | (e != be_ref[jnp.maximum(i - 1, 0)])

    def weight_copies(expert, slot):
        return [pltpu.make_async_copy(w.at[expert], wf_ref.at[slot, j], sem.at[slot])
                for j, w in enumerate((wg_hbm, wu_hbm, wd_hbm))]

    @pl.when(i == 0)
    def _():
        for cp in weight_copies(e, par_ref[e]):
            cp.start()

    @pl.when(used & changed)
    def _():
        slot = par_ref[e]
        for cp in weight_copies(e, slot):
            cp.wait()
        nxt = nxt_ref[e]

        @pl.when(nxt < N_EXPERTS)
        def _():
            for cp in weight_copies(nxt, 1 - slot):
                cp.start()

        for j in range(3):
            wb_ref[j] = wf_ref[slot, j].astype(BF16)

    @pl.when(used)
    def _():
        x = _unpack_bf16_pairs(_load_row_tiles(x_ref, MOE_BM))
        gt = jnp.minimum(jnp.dot(x, wb_ref[0], preferred_element_type=F32) + bg_ref[...],
                         SWIGLU_LIMIT)
        up = jnp.clip(jnp.dot(x, wb_ref[1], preferred_element_type=F32) + bu_ref[...],
                      -SWIGLU_LIMIT, SWIGLU_LIMIT)
        hid = (up + 1.0) * (gt * _sigmoid(SWIGLU_ALPHA * gt))
        _store_row_tiles(y_ref, jnp.dot(hid.astype(BF16), wb_ref[2],
                                        preferred_element_type=F32) + bd_ref[...])

    @pl.when(jnp.logical_not(used))
    def _():
        y_ref[...] = jnp.zeros_like(y_ref)


def _experts(block_e, n_used, next_e, parity, xs_t, w_gate, b_gate, w_up, b_up, w_down, b_down):
    d, d_ff = w_gate.shape[1:]
    assert d == d_ff, "weight staging buffers assume square expert matrices"
    rt = d // LANES
    rt_x = d // 2 // LANES
    n_blocks = xs_t.shape[0] // (MOE_BM * rt_x)
    bspec = lambda m: pl.BlockSpec((None, 1, m), lambda i, be, nu, nx, pa: (be[i], 0, 0))
    hbm = pl.BlockSpec(memory_space=pl.ANY)
    grid_spec = pltpu.PrefetchScalarGridSpec(
        num_scalar_prefetch=4,
        grid=(n_blocks,),
        in_specs=[
            pl.BlockSpec((MOE_BM * rt_x, LANES),
                         lambda i, be, nu, nx, pa: (jnp.maximum(jnp.minimum(i, nu[0] - 1), 0), 0)),
            bspec(d_ff), bspec(d_ff), bspec(d),
            hbm, hbm, hbm,
        ],
        out_specs=pl.BlockSpec((MOE_BM * rt, LANES), lambda i, be, nu, nx, pa: (i, 0)),
        scratch_shapes=[
            pltpu.VMEM((2, 3, d, d_ff), F32),
            pltpu.VMEM((3, d, d_ff), BF16),
            pltpu.SemaphoreType.DMA((2,)),
        ],
    )
    return pl.pallas_call(
        _experts_kernel,
        grid_spec=grid_spec,
        out_shape=jax.ShapeDtypeStruct((n_blocks * MOE_BM * rt, LANES), F32),
        compiler_params=_cparams(1),
        name="experts",
    )(block_e, n_used, next_e, parity, xs_t, b_gate, b_up, b_down, w_gate, w_up, w_down)


def _combine_kernel(dcur_ref, dnext_ref, h2_ref, route_ref, gain_ref, ys_ref, o_ref,
                    buf_ref, sem, *, tm):
    i = pl.program_id(0)
    nsteps = pl.num_programs(0)
    slot = i & 1
    d = h2_ref.shape[1]
    rt = d // LANES

    def issue_all(dref, s):
        def per_token(tk, carry):
            for kk in range(TOP_K):
                r = dref[kk, tk]
                pltpu.make_async_copy(_row_tile(ys_ref, r, rt),
                                      _row_tile(buf_ref.at[s, kk], tk, rt),
                                      sem.at[s]).start(priority=kk % 2)
            return carry
        lax.fori_loop(0, tm, per_token, 0, unroll=4)

    @pl.when(i == 0)
    def _():
        issue_all(dcur_ref, 0)

    @pl.when(i + 1 < nsteps)
    def _():
        issue_all(dnext_ref, 1 - slot)

    for kk in range(TOP_K):
        pltpu.make_async_copy(ys_ref.at[pl.ds(0, tm * rt)], buf_ref.at[slot, kk],
                              sem.at[slot]).wait()

    rec = route_ref[...]
    acc = h2_ref[...]
    for kk in range(TOP_K):
        acc = acc + (rec[:, ROUTE_GATE + kk:ROUTE_GATE + kk + 1]
                     * _load_row_tiles(buf_ref.at[slot, kk], tm))
    o_ref[...] = _rmsnorm(acc, gain_ref[...])


def _combine(dest, h2, route, gain, ys_t, tm=128):
    n, d = h2.shape
    nsteps = n // tm
    return pl.pallas_call(
        functools.partial(_combine_kernel, tm=tm),
        grid=(nsteps,),
        in_specs=[
            pl.BlockSpec((SUBLANES, tm), lambda i: (0, i), memory_space=pltpu.SMEM),
            pl.BlockSpec((SUBLANES, tm), lambda i: (0, jnp.minimum(i + 1, nsteps - 1)),
                         memory_space=pltpu.SMEM),
            pl.BlockSpec((tm, d), lambda i: (i, 0)),
            pl.BlockSpec((tm, LANES), lambda i: (i, 0)),
            pl.BlockSpec((1, d), lambda i: (0, 0)),
            pl.BlockSpec(memory_space=pl.ANY),
        ],
        out_specs=pl.BlockSpec((tm, d), lambda i: (i, 0)),
        out_shape=jax.ShapeDtypeStruct((n, d), F32),
        scratch_shapes=[
            pltpu.VMEM((2, TOP_K, tm * (d // LANES), LANES), F32),
            pltpu.SemaphoreType.DMA((2,)),
        ],
        compiler_params=_cparams(1),
        name="combine",
    )(dest, dest, h2, route, gain, ys_t)


def kernel(x, mem, norm_mix, w_in, dn_conv, dn_a_log, dn_dt_bias, dn_norm, rg_conv, rg_conv_b, rg_w_a, rg_b_a, rg_w_x, rg_b_x, rg_lambda, w_out, norm_cross, norm_mem, w_cq, w_ckv, w_co, norm_moe, w_router, b_router, w_gate, b_gate, w_up, b_up, w_down, b_down, norm_final):
    batch, seq, d = x.shape
    mem_len = mem.shape[1]
    n = batch * seq
    assert w_in.shape[0] == 1, "single-layer trunk"
    x2 = x.reshape(n, d)

    wi = w_in[0]
    n_gate = 2 * DN_HEADS
    w_cat = jnp.concatenate(
        [wi[:, :4 * D_DN], wi[:, 4 * D_DN + n_gate:],
         jnp.pad(wi[:, 4 * D_DN:4 * D_DN + n_gate], ((0, 0), (0, LANES - n_gate)))],
        axis=1).astype(BF16)
    wbat = wi[:, 4 * D_DN:4 * D_DN + n_gate].T.astype(BF16)
    prow = (jnp.zeros((SUBLANES, LANES), F32)
            .at[0, DN_HEADS:n_gate].set(dn_a_log[0]).at[1, DN_HEADS:n_gate].set(dn_dt_bias[0]))
    pcol = (jnp.zeros((n_gate, LANES), F32)
            .at[DN_HEADS:, 0].set(dn_a_log[0]).at[DN_HEADS:, 1].set(dn_dt_bias[0]))

    proj, gbt = _in_proj(x2, norm_mix, w_cat, wbat, prow, pcol)
    dn = _deltanet(proj, gbt, dn_conv[0], dn_norm, batch, seq)
    rg = _rglru(proj, rg_conv[0], rg_conv_b, rg_w_a[0], rg_b_a[0].reshape(1, D_RG),
                rg_w_x[0], rg_b_x[0].reshape(1, D_RG), rg_lambda, batch, seq)
    kv = _mem_kv(mem.reshape(batch * mem_len, d), norm_mem, w_ckv[0])

    w_r = jnp.pad(w_router[0], ((0, 0), (0, LANES - N_EXPERTS)))
    w_r_hi = w_r.astype(BF16)
    w_r_lo = (w_r - w_r_hi.astype(F32)).astype(BF16)
    b_r = jnp.pad(b_router, ((0, 0), (0, LANES - N_EXPERTS)))
    h2, xn, route, route_t, counts = _post_mix(
        dn, rg, x2, w_out[0], norm_cross, w_cq[0], kv, w_co[0], norm_moe, w_r_hi, w_r_lo, b_r,
        seq, mem_len)

    n_blocks = n * TOP_K // MOE_BM + N_EXPERTS
    n_rows = n_blocks * MOE_BM
    cnt = counts[0, :N_EXPERTS].astype(I32)
    padded = (cnt + MOE_BM - 1) // MOE_BM * MOE_BM
    pad_end = jnp.cumsum(padded)
    pad_start = pad_end - padded
    n_used = (pad_end[-1:] // MOE_BM).astype(I32)
    block_e = jnp.minimum(
        jnp.sum(pad_end[None, :] <= (jnp.arange(n_blocks, dtype=I32) * MOE_BM)[:, None], axis=1),
        N_EXPERTS - 1).astype(I32)
    meta = jnp.concatenate([pad_start + cnt, padded - cnt, n_used]).astype(I32)

    dest = _dest(pad_start.astype(I32), route_t)
    xs = _dispatch(meta, dest, xn, n, n_rows)
    has = cnt > 0
    eid = jnp.where(has, jnp.arange(N_EXPERTS, dtype=I32), N_EXPERTS)
    after = lax.cummin(eid, axis=0, reverse=True)
    next_e = jnp.concatenate([after[1:], jnp.full((1,), N_EXPERTS, I32)]).astype(I32)
    parity = ((jnp.cumsum(has.astype(I32)) - 1) & 1).astype(I32)
    ys = _experts(block_e, n_used, next_e, parity, xs, w_gate[0], b_gate[0][:, None, :], w_up[0],
                  b_up[0][:, None, :], w_down[0], b_down[0][:, None, :])
    out = _combine(dest, h2, route, norm_final.reshape(1, d), ys)
    return out.reshape(batch, seq, d)
```

```python
import functools

import jax
import jax.numpy as jnp
from jax import lax
from jax.experimental import pallas as pl
from jax.experimental.pallas import tpu as pltpu

F32 = jnp.float32
BF16 = jnp.bfloat16
I32 = jnp.int32
U32 = jnp.uint32

EPS = 1e-6
LANES = 128
SUBLANES = 8
VMEM_LIMIT = 48 * 1024 * 1024

DN_HEADS = 4
DN_HEAD_DIM = 128
D_DN = DN_HEADS * DN_HEAD_DIM
D_RG = 512
RG_BLOCKS = 4
RG_C = 8.0
CONV_WIDTH = 4
XA_HEADS = 4
N_EXPERTS = 32
TOP_K = 4
SWIGLU_LIMIT = 7.0
SWIGLU_ALPHA = 1.702

DN_CHUNK = 128
INV_BASE = 16
MOE_BM = 256

COL_RX = 4 * D_DN
COL_RY = COL_RX + D_RG
COL_GB = COL_RY + D_RG
PROJ_W = COL_GB + LANES


def _cparams(n_axes=1):
    return pltpu.CompilerParams(
        dimension_semantics=("arbitrary",) * n_axes, vmem_limit_bytes=VMEM_LIMIT)


def _mm(a, b):
    return jnp.dot(a.astype(BF16), b.astype(BF16), preferred_element_type=F32)


def _mm_nt(a, b):
    return lax.dot_general(a.astype(BF16), b.astype(BF16), (((1,), (1,)), ((), ())),
                           preferred_element_type=F32)


def _rmsnorm(x, g):
    return x * lax.rsqrt(jnp.mean(x * x, axis=-1, keepdims=True) + EPS) * g


def _sigmoid(x):
    return 0.5 * jnp.tanh(0.5 * x) + 0.5


def _mm_split(a, b_hi, b_lo):
    a_hi = a.astype(BF16)
    a_lo = (a - a_hi.astype(F32)).astype(BF16)
    return (jnp.dot(a_hi, b_hi, preferred_element_type=F32)
            + jnp.dot(a_hi, b_lo, preferred_element_type=F32)
            + jnp.dot(a_lo, b_hi, preferred_element_type=F32))


def _softplus(x):
    return jnp.maximum(x, 0.0) + jnp.log1p(jnp.exp(-jnp.abs(x)))


def _load_row_tiles(ref, rows):
    rt = ref.shape[0] // rows
    return jnp.concatenate([ref[pl.ds(s, rows, stride=rt), :] for s in range(rt)], axis=1)


def _store_row_tiles(ref, val):
    rows, w = val.shape
    rt = w // LANES
    for s in range(rt):
        ref[pl.ds(s, rows, stride=rt), :] = val[:, s * LANES:(s + 1) * LANES]


def _pack_bf16_pairs(x):
    half = x.shape[1] // 2
    hi = pltpu.bitcast(x[:, :half].astype(BF16).astype(F32), U32)
    lo = pltpu.bitcast(x[:, half:].astype(BF16).astype(F32), U32)
    return hi | (lo >> 16)


def _unpack_bf16_pairs(p):
    hi = pltpu.bitcast(p & jnp.uint32(0xFFFF0000), F32)
    lo = pltpu.bitcast(p << 16, F32)
    return jnp.concatenate([hi, lo], axis=1).astype(BF16)


def _load_segmented(ref):
    seg = ref.shape[0] // SUBLANES
    return jnp.concatenate([ref[pl.ds(j, SUBLANES, stride=seg), :] for j in range(seg)], axis=0)


def _causal_conv_segmented(xp, halo_ref, cs, w):
    tb = xp.shape[0]
    ng = w.shape[0] - 1
    prev = halo_ref[:, cs]
    last = xp[tb - ng * SUBLANES:, :]
    halo_ref[:, cs] = last
    sub = lax.broadcasted_iota(I32, (SUBLANES, xp.shape[1]), 0)
    groups = []
    for g in range(ng):
        rows = slice(g * SUBLANES, (g + 1) * SUBLANES)
        groups.append(jnp.where(sub == 0, pltpu.roll(prev[rows], 1, 0),
                                pltpu.roll(last[rows], 1, 0)))
    ext = jnp.concatenate(groups + [xp], axis=0)
    y = w[ng:ng + 1] * xp
    for k in range(1, ng + 1):
        y = y + w[ng - k:ng - k + 1] * ext[(ng - k) * SUBLANES:(ng - k) * SUBLANES + tb]
    return y


def _in_proj_kernel(x_ref, g_ref, w_ref, wbat_ref, prow_ref, pcol_ref, proj_ref, gbt_ref):
    u = _rmsnorm(x_ref[...], g_ref[...]).astype(BF16)
    for c0 in range(0, COL_GB, 512):
        proj_ref[:, c0:c0 + 512] = jnp.dot(u, w_ref[:, c0:c0 + 512], preferred_element_type=F32)
    ba = jnp.dot(u, w_ref[:, COL_GB:PROJ_W], preferred_element_type=F32)
    lane = lax.broadcasted_iota(I32, ba.shape, 1)
    g = -jnp.exp(prow_ref[0:1, :]) * _softplus(ba + prow_ref[1:2, :])
    proj_ref[:, COL_GB:PROJ_W] = jnp.where(lane < DN_HEADS, _sigmoid(ba), g)
    bat = lax.dot_general(wbat_ref[...], u, (((1,), (1,)), ((), ())), preferred_element_type=F32)
    row = lax.broadcasted_iota(I32, bat.shape, 0)
    gt = -jnp.exp(pcol_ref[:, 0:1]) * _softplus(bat + pcol_ref[:, 1:2])
    gbt_ref[...] = jnp.where(row < DN_HEADS, _sigmoid(bat), gt)


def _in_proj(x2, gain, w_cat, wbat, prow, pcol, tm=512):
    n, d = x2.shape
    return pl.pallas_call(
        _in_proj_kernel,
        grid=(n // tm,),
        in_specs=[
            pl.BlockSpec((tm, d), lambda i: (i, 0)),
            pl.BlockSpec((1, d), lambda i: (0, 0)),
            pl.BlockSpec((d, PROJ_W), lambda i: (0, 0)),
            pl.BlockSpec((2 * DN_HEADS, d), lambda i: (0, 0)),
            pl.BlockSpec((SUBLANES, LANES), lambda i: (0, 0)),
            pl.BlockSpec((2 * DN_HEADS, LANES), lambda i: (0, 0)),
        ],
        out_specs=[
            pl.BlockSpec((tm, PROJ_W), lambda i: (i, 0)),
            pl.BlockSpec((2 * DN_HEADS, tm), lambda i: (0, i)),
        ],
        out_shape=[
            jax.ShapeDtypeStruct((n, PROJ_W), F32),
            jax.ShapeDtypeStruct((2 * DN_HEADS, n), F32),
        ],
        compiler_params=_cparams(1),
        name="in_proj",
    )(x2, gain, w_cat, wbat, prow, pcol)


def _deltanet_kernel(*refs, tb):
    n_grp = 3 * DN_HEADS
    qkv_refs = refs[:n_grp]
    (z_ref, gb_ref, gbt_ref, conv_ref, norm_ref, o_ref,
     s_ref, halo_ref, act_ref, gcc_ref, gcr_ref,
     a_ref, p_ref, d_ref, qk_ref, rhs_ref, u_ref, wq_ref, kdt_ref) = refs[n_grp:]
    t = pl.program_id(1)
    c = DN_CHUNK
    dh = DN_HEAD_DIM

    @pl.when(t == 0)
    def _():
        s_ref[...] = jnp.zeros_like(s_ref)
        halo_ref[...] = jnp.zeros_like(halo_ref)

    sub = lax.broadcasted_iota(I32, (SUBLANES, dh), 0)
    for grp in range(n_grp):
        cs = slice(grp * dh, (grp + 1) * dh)
        x = qkv_refs[grp][...]
        prev = halo_ref[:, cs]
        halo_ref[:, cs] = x[tb - SUBLANES:, :]
        y = conv_ref[CONV_WIDTH - 1:CONV_WIDTH, cs] * x
        for k in range(1, CONV_WIDTH):
            xs = pltpu.roll(x, k, 0)
            head = jnp.where(sub < k, pltpu.roll(prev, k, 0), xs[:SUBLANES])
            xs = jnp.concatenate([head, xs[SUBLANES:]], axis=0)
            y = y + conv_ref[CONV_WIDTH - 1 - k:CONV_WIDTH - k, cs] * xs
        y = y * _sigmoid(y)
        if grp < 2 * DN_HEADS:
            y = y * lax.rsqrt(jnp.sum(y * y, axis=-1, keepdims=True) + EPS)
        if grp < DN_HEADS:
            y = y * (dh ** -0.5)
        act_ref[grp] = y

    gcol = gb_ref[...]
    rpos = lax.broadcasted_iota(I32, gcol.shape, 0) & (c - 1)
    d = 1
    while d < c:
        gcol = gcol + jnp.where(rpos >= d, pltpu.roll(gcol, d, 0), 0.0)
        d *= 2
    gcc_ref[...] = gcol
    grow = gbt_ref[...]
    lpos = lax.broadcasted_iota(I32, grow.shape, 1) & (c - 1)
    d = 1
    while d < c:
        grow = grow + jnp.where(lpos >= d, pltpu.roll(grow, d, 1), 0.0)
        d *= 2
    gcr_ref[...] = grow

    row = lax.broadcasted_iota(I32, (c, c), 0)
    col = lax.broadcasted_iota(I32, (c, c), 1)
    causal = row >= col
    strict = row > col
    eye = jnp.where(row == col, 1.0, 0.0)
    gain = norm_ref[...]
    n_chunks = tb // c
    probs = [(ci, h) for ci in range(n_chunks) for h in range(DN_HEADS)]

    decay_last = []
    for p, (ci, h) in enumerate(probs):
        rows = slice(ci * c, (ci + 1) * c)
        q = act_ref[h, rows, :]
        k = act_ref[DN_HEADS + h, rows, :]
        v = act_ref[2 * DN_HEADS + h, rows, :]
        beta = gb_ref[rows, h:h + 1]
        gc = gcc_ref[rows, DN_HEADS + h:DN_HEADS + h + 1]
        gr = gcr_ref[DN_HEADS + h:DN_HEADS + h + 1, rows]
        g_last = gc[c - 1:c, :]
        decay = jnp.where(causal, jnp.exp(jnp.where(causal, gc - gr, 0.0)), 0.0)
        kb = k * beta
        both = _mm_nt(jnp.concatenate([kb, q], axis=0), k)
        a_ref[p] = jnp.where(strict, both[:c] * decay, 0.0)
        qk_ref[p] = (both[c:] * decay).astype(BF16)
        egc = jnp.exp(gc)
        rhs_ref[p] = jnp.concatenate([v * beta, kb * egc], axis=1).astype(BF16)
        wq_ref[p, c:, :] = (q * egc).astype(BF16)
        kdt_ref[p] = (k * jnp.exp(g_last - gc)).T.astype(BF16)
        decay_last.append(jnp.exp(g_last))

    shift = INV_BASE.bit_length() - 1
    blk = (row >> shift) == (col >> shift)
    for p in range(len(probs)):
        diag = jnp.where(blk, a_ref[p], 0.0)
        p_ref[p] = eye - diag
        d_ref[p] = _mm(diag, diag).astype(BF16)
    for it in range(shift - 1):
        for p in range(len(probs)):
            pw = d_ref[p]
            inv = p_ref[p]
            p_ref[p] = inv + _mm(inv, pw)
            if it < shift - 2:
                d_ref[p] = _mm(pw, pw).astype(BF16)
    s = INV_BASE
    while s < c:
        sh = s.bit_length() - 1
        off = ((row >> (sh + 1)) == (col >> (sh + 1))) & ((row >> sh) != (col >> sh))
        for p in range(len(probs)):
            d_ref[p] = _mm(p_ref[p], jnp.where(off, a_ref[p], 0.0)).astype(BF16)
        for p in range(len(probs)):
            inv = p_ref[p]
            p_ref[p] = inv - _mm(d_ref[p], inv)
        s *= 2
    for p in range(len(probs)):
        uw = _mm(p_ref[p], rhs_ref[p])
        u_ref[p] = uw[:, :dh]
        wq_ref[p, :c, :] = uw[:, dh:].astype(BF16)

    for ci in range(n_chunks):
        rows = slice(ci * c, (ci + 1) * c)
        ps = [ci * DN_HEADS + h for h in range(DN_HEADS)]
        s_old = [s_ref[h] for h in range(DN_HEADS)]
        ws = [_mm(wq_ref[p], s_old[h]) for h, p in enumerate(ps)]
        v_new = [u_ref[p] - ws[h][:c] for h, p in enumerate(ps)]
        outs = [ws[h][c:] + _mm(qk_ref[p], v_new[h]) for h, p in enumerate(ps)]
        for h, p in enumerate(ps):
            s_ref[h] = s_old[h] * decay_last[p] + _mm(kdt_ref[p], v_new[h])
        for h in range(DN_HEADS):
            o = _rmsnorm(outs[h], gain)
            zz = z_ref[rows, h * dh:(h + 1) * dh]
            o_ref[rows, h * dh:(h + 1) * dh] = (o * (zz * _sigmoid(zz))).astype(o_ref.dtype)


def _deltanet(proj, gbt, dn_conv, dn_norm, batch, seq, tb=512):
    n = proj.shape[0]
    nt = seq // tb
    c = DN_CHUNK
    n_prob = (tb // c) * DN_HEADS
    n_grp = 3 * DN_HEADS
    grp = lambda j: pl.BlockSpec((tb, DN_HEAD_DIM), lambda b, t, j=j: (b * nt + t, j))
    return pl.pallas_call(
        functools.partial(_deltanet_kernel, tb=tb),
        grid=(batch, nt),
        in_specs=[grp(j) for j in range(n_grp)] + [
            pl.BlockSpec((tb, D_DN), lambda b, t: (b * nt + t, 3)),
            pl.BlockSpec((tb, LANES), lambda b, t: (b * nt + t, COL_GB // LANES)),
            pl.BlockSpec((2 * DN_HEADS, tb), lambda b, t: (0, b * nt + t)),
            pl.BlockSpec((CONV_WIDTH, 3 * D_DN), lambda b, t: (0, 0)),
            pl.BlockSpec((1, DN_HEAD_DIM), lambda b, t: (0, 0)),
        ],
        out_specs=pl.BlockSpec((tb, D_DN), lambda b, t: (b * nt + t, 0)),
        out_shape=jax.ShapeDtypeStruct((n, D_DN), BF16),
        scratch_shapes=[
            pltpu.VMEM((DN_HEADS, DN_HEAD_DIM, DN_HEAD_DIM), F32),
            pltpu.VMEM((SUBLANES, 3 * D_DN), F32),
            pltpu.VMEM((n_grp, tb, DN_HEAD_DIM), F32),
            pltpu.VMEM((tb, LANES), F32),
            pltpu.VMEM((2 * DN_HEADS, tb), F32),
            pltpu.VMEM((n_prob, c, c), F32),
            pltpu.VMEM((n_prob, c, c), F32),
            pltpu.VMEM((n_prob, c, c), BF16),
            pltpu.VMEM((n_prob, c, c), BF16),
            pltpu.VMEM((n_prob, c, 2 * DN_HEAD_DIM), BF16),
            pltpu.VMEM((n_prob, c, DN_HEAD_DIM), F32),
            pltpu.VMEM((n_prob, 2 * c, DN_HEAD_DIM), BF16),
            pltpu.VMEM((n_prob, DN_HEAD_DIM, c), BF16),
        ],
        compiler_params=_cparams(2),
        name="deltanet",
    )(*([proj] * (n_grp + 2)), gbt, dn_conv, dn_norm)


def _gelu_tanh(x):
    return 0.5 * x * (1.0 + jnp.tanh(0.7978845608028654 * (x + 0.044715 * (x * x * x))))


def _rglru_kernel(*refs, tb):
    rx_refs = refs[:RG_BLOCKS]
    (ry_ref, conv_ref, convb_ref, wa_ref, ba_ref, wx_ref, bx_ref, lam_ref,
     o_ref, hc_ref, halo_ref, a_ref, b_ref, h_ref) = refs[RG_BLOCKS:]
    t = pl.program_id(1)
    bw = D_RG // RG_BLOCKS
    seg = tb // SUBLANES

    @pl.when(t == 0)
    def _():
        hc_ref[...] = jnp.zeros_like(hc_ref)
        halo_ref[...] = jnp.zeros_like(halo_ref)

    log_sig = -_softplus(-lam_ref[...])
    rowi = lax.broadcasted_iota(I32, (tb, bw), 0)
    seq_start_row = jnp.where(t == 0, 0, -1)
    for nb in range(RG_BLOCKS):
        cs = slice(nb * bw, (nb + 1) * bw)
        xb = (_causal_conv_segmented(_load_segmented(rx_refs[nb]), halo_ref, cs,
                                     conv_ref[:, cs]) + convb_ref[:, cs])
        r = _sigmoid(_mm(xb, wa_ref[nb]) + ba_ref[:, cs])
        gi = _sigmoid(_mm(xb, wx_ref[nb]) + bx_ref[:, cs])
        log_a = RG_C * r * log_sig[:, cs]
        a = jnp.exp(log_a)
        y = jnp.tanh(-log_a) * (1.0 + a * a)
        mult = jnp.where(y > 0.0, y * lax.rsqrt(y), 0.0)
        mult = jnp.where(rowi == seq_start_row, 1.0, mult)
        a_ref[:, cs] = a
        b_ref[:, cs] = mult * (gi * xb)

    def scan(j, carry):
        ac, bc = carry
        rows = pl.ds(pl.multiple_of(j * SUBLANES, SUBLANES), SUBLANES)
        aj = a_ref[rows, :]
        bc = aj * bc + b_ref[rows, :]
        ac = aj * ac
        a_ref[rows, :] = ac
        b_ref[rows, :] = bc
        return ac, bc

    ac, bc = lax.fori_loop(1, seg, scan, (a_ref[0:SUBLANES, :], b_ref[0:SUBLANES, :]), unroll=7)

    h = hc_ref[...]
    h_in = []
    for s in range(SUBLANES):
        h_in.append(h)
        h = ac[s:s + 1, :] * h + bc[s:s + 1, :]
    hc_ref[...] = h
    h_in = jnp.concatenate(h_in, axis=0)

    for j in range(seg):
        rows = slice(j * SUBLANES, (j + 1) * SUBLANES)
        hj = a_ref[rows, :] * h_in + b_ref[rows, :]
        for nb in range(RG_BLOCKS):
            h_ref[nb, rows, :] = hj[:, nb * bw:(nb + 1) * bw]
    g = seg // SUBLANES
    for nb in range(RG_BLOCKS):
        cs = slice(nb * bw, (nb + 1) * bw)
        h = jnp.concatenate(
            [h_ref[nb, pl.ds((i % g) * SUBLANES * SUBLANES + i // g, SUBLANES, stride=SUBLANES), :]
             for i in range(seg)], axis=0)
        o_ref[:, cs] = (h * _gelu_tanh(ry_ref[:, cs])).astype(o_ref.dtype)


def _rglru(proj, rg_conv, rg_conv_b, w_a, b_a, w_x, b_x, lam, batch, seq, tb=512):
    n = proj.shape[0]
    nt = seq // tb
    bw = D_RG // RG_BLOCKS
    full = lambda shape: pl.BlockSpec(shape, lambda b, t: (0,) * len(shape))
    return pl.pallas_call(
        functools.partial(_rglru_kernel, tb=tb),
        grid=(batch, nt),
        in_specs=[pl.BlockSpec((tb, bw), lambda b, t, j=j: (b * nt + t, COL_RX // bw + j))
                  for j in range(RG_BLOCKS)] + [
            pl.BlockSpec((tb, D_RG), lambda b, t: (b * nt + t, COL_RY // D_RG)),
            full((CONV_WIDTH, D_RG)), full((1, D_RG)),
            full((RG_BLOCKS, bw, bw)), full((1, D_RG)),
            full((RG_BLOCKS, bw, bw)), full((1, D_RG)),
            full((1, D_RG)),
        ],
        out_specs=pl.BlockSpec((tb, D_RG), lambda b, t: (b * nt + t, 0)),
        out_shape=jax.ShapeDtypeStruct((n, D_RG), BF16),
        scratch_shapes=[
            pltpu.VMEM((1, D_RG), F32),
            pltpu.VMEM(((CONV_WIDTH - 1) * SUBLANES, D_RG), F32),
            pltpu.VMEM((tb, D_RG), F32),
            pltpu.VMEM((tb, D_RG), F32),
            pltpu.VMEM((RG_BLOCKS, tb, bw), F32),
        ],
        compiler_params=_cparams(2),
        name="rglru",
    )(*([proj] * (RG_BLOCKS + 1)), rg_conv, rg_conv_b, w_a, b_a, w_x, b_x, lam)


def _mem_kv_kernel(m_ref, g_ref, w_ref, o_ref):
    mn = _rmsnorm(m_ref[...], g_ref[...]).astype(BF16)
    for c0 in range(0, o_ref.shape[1], 512):
        o_ref[:, c0:c0 + 512] = jnp.dot(mn, w_ref[:, c0:c0 + 512].astype(BF16),
                                        preferred_element_type=F32).astype(o_ref.dtype)


def _mem_kv(mem2, gain, w_ckv):
    n, d = mem2.shape
    return pl.pallas_call(
        _mem_kv_kernel,
        grid=(1,),
        in_specs=[
            pl.BlockSpec((n, d), lambda i: (0, 0)),
            pl.BlockSpec((1, d), lambda i: (0, 0)),
            pl.BlockSpec((d, 2 * d), lambda i: (0, 0)),
        ],
        out_specs=pl.BlockSpec((n, 2 * d), lambda i: (0, 0)),
        out_shape=jax.ShapeDtypeStruct((n, 2 * d), BF16),
        compiler_params=_cparams(1),
        name="mem_kv",
    )(mem2, gain, w_ckv)


ROUTE_E = 0
ROUTE_RANK = TOP_K
ROUTE_GATE = 2 * TOP_K
ROUTE_ROWS = 16


def _post_mix_kernel(dn_ref, rg_ref, x_ref, wo_f32, gx_ref, wq_f32, kv_ref, wco_f32, gm_ref,
                     wrh_ref, wrl_ref, br_ref, h2_ref, xn_ref, route_ref, routet_ref, cnt_ref,
                     carry_ref, wo_ref, wq_ref, wco_ref):
    i = pl.program_id(0)
    tm, d = x_ref.shape
    hd = d // XA_HEADS

    @pl.when(i == 0)
    def _():
        carry_ref[...] = jnp.zeros_like(carry_ref)
        wo_ref[...] = wo_f32[...].astype(BF16)
        wq_ref[...] = wq_f32[...].astype(BF16)
        wco_ref[...] = wco_f32[...].astype(BF16)

    h1 = (x_ref[...] + jnp.dot(dn_ref[...], wo_ref[0:D_DN, :], preferred_element_type=F32)
          + jnp.dot(rg_ref[...], wo_ref[D_DN:, :], preferred_element_type=F32))

    hn = _rmsnorm(h1, gx_ref[...]).astype(BF16)
    q = jnp.dot(hn, wq_ref[...], preferred_element_type=F32)
    heads = []
    for hh in range(XA_HEADS):
        cs = slice(hh * hd, (hh + 1) * hd)
        s = _mm_nt(q[:, cs], kv_ref[:, cs]) * (hd ** -0.5)
        p = jnp.exp(s - jnp.max(s, axis=-1, keepdims=True))
        p = p / jnp.sum(p, axis=-1, keepdims=True)
        heads.append(_mm(p, kv_ref[:, d + hh * hd:d + (hh + 1) * hd]).astype(BF16))
    o = jnp.concatenate(heads, axis=1)
    h2 = h1 + jnp.dot(o, wco_ref[...], preferred_element_type=F32)
    h2_ref[...] = h2

    xn = _rmsnorm(h2, gm_ref[...])
    _store_row_tiles(xn_ref, _pack_bf16_pairs(xn))
    logits = _mm_split(xn, wrh_ref[...], wrl_ref[...]) + br_ref[...]
    lg = logits.T[0:N_EXPERTS, :]
    eidx = lax.broadcasted_iota(I32, lg.shape, 0).astype(F32)
    neg = jnp.float32(-jnp.inf)
    vals, idxs, hots = [], [], []
    for _ in range(TOP_K):
        m = jnp.max(lg, axis=0, keepdims=True)
        idx = jnp.min(jnp.where(lg == m, eidx, float(N_EXPERTS)), axis=0, keepdims=True)
        hot = eidx == idx
        lg = jnp.where(hot, neg, lg)
        vals.append(m)
        idxs.append(idx)
        hots.append(hot)
    es = [jnp.exp(v - vals[0]) for v in vals]
    den = es[0] + es[1] + es[2] + es[3]
    gates = [e / den for e in es]

    chosen = jnp.zeros(lg.shape, F32)
    for hot in hots:
        chosen = chosen + jnp.where(hot, 1.0, 0.0)
    r2 = lax.broadcasted_iota(I32, (tm, tm), 0)
    c2 = lax.broadcasted_iota(I32, (tm, tm), 1)
    before = _mm(chosen, jnp.where(r2 < c2, 1.0, 0.0)) + carry_ref[:, 0:1]
    ranks = [jnp.sum(jnp.where(hot, before, 0.0), axis=0, keepdims=True) for hot in hots]
    carry_ref[...] = carry_ref[...] + jnp.sum(chosen, axis=1, keepdims=True)
    cnt_ref[...] = carry_ref[...]

    row = lax.broadcasted_iota(I32, (ROUTE_ROWS, tm), 0)
    rect = jnp.zeros((ROUTE_ROWS, tm), F32)
    for kk in range(TOP_K):
        rect = jnp.where(row == ROUTE_E + kk, idxs[kk], rect)
        rect = jnp.where(row == ROUTE_RANK + kk, ranks[kk], rect)
        rect = jnp.where(row == ROUTE_GATE + kk, gates[kk], rect)
    routet_ref[...] = rect
    route_ref[...] = jnp.concatenate(
        [rect, jnp.zeros((LANES - ROUTE_ROWS, tm), F32)], axis=0).T


def _post_mix(dn, rg, x2, w_out, g_cross, w_cq, kv, w_co, g_moe, w_r_hi, w_r_lo, b_r, seq, mem_len,
              tm=512):
    n, d = x2.shape
    per_b = seq // tm
    rt_x = d // 2 // LANES
    full = lambda shape: pl.BlockSpec(shape, lambda i: (0,) * len(shape))
    once = lambda shape: pl.BlockSpec(shape, lambda i: (0,) * len(shape),
                                      pipeline_mode=pl.Buffered(1))
    return pl.pallas_call(
        _post_mix_kernel,
        grid=(n // tm,),
        in_specs=[
            pl.BlockSpec((tm, D_DN), lambda i: (i, 0)),
            pl.BlockSpec((tm, D_RG), lambda i: (i, 0)),
            pl.BlockSpec((tm, d), lambda i: (i, 0)),
            once((d, d)), full((1, d)), once((d, d)),
            pl.BlockSpec((mem_len, 2 * d), lambda i: (i // per_b, 0)),
            once((d, d)), full((1, d)),
            full((d, LANES)), full((d, LANES)), full((1, LANES)),
        ],
        out_specs=[
            pl.BlockSpec((tm, d), lambda i: (i, 0)),
            pl.BlockSpec((tm * rt_x, LANES), lambda i: (i, 0)),
            pl.BlockSpec((tm, LANES), lambda i: (i, 0)),
            pl.BlockSpec((ROUTE_ROWS, tm), lambda i: (0, i)),
            pl.BlockSpec((N_EXPERTS, LANES), lambda i: (0, 0)),
        ],
        out_shape=[
            jax.ShapeDtypeStruct((n, d), F32),
            jax.ShapeDtypeStruct((n * rt_x, LANES), U32),
            jax.ShapeDtypeStruct((n, LANES), F32),
            jax.ShapeDtypeStruct((ROUTE_ROWS, n), F32),
            jax.ShapeDtypeStruct((N_EXPERTS, LANES), F32),
        ],
        scratch_shapes=[pltpu.VMEM((N_EXPERTS, LANES), F32),
                        pltpu.VMEM((d, d), BF16), pltpu.VMEM((d, d), BF16), pltpu.VMEM((d, d), BF16)],
        compiler_params=_cparams(1),
        name="post_mix",
    )(dn, rg, x2, w_out, g_cross, w_cq, kv, w_co, g_moe, w_r_hi, w_r_lo, b_r)


def _dest_kernel(start_ref, routet_ref, dest_ref):
    e = routet_ref[ROUTE_E:ROUTE_E + SUBLANES, :]
    rank = routet_ref[ROUTE_RANK:ROUTE_RANK + SUBLANES, :]
    base = jnp.zeros(e.shape, F32)
    for j in range(N_EXPERTS):
        base = jnp.where(e == float(j), start_ref[j].astype(F32), base)
    row = lax.broadcasted_iota(I32, e.shape, 0)
    dest_ref[...] = jnp.where(row < TOP_K, (base + rank).astype(I32), 0)


def _dest(pad_start, route_t, tm=2048):
    n = route_t.shape[1]
    grid_spec = pltpu.PrefetchScalarGridSpec(
        num_scalar_prefetch=1,
        grid=(n // tm,),
        in_specs=[pl.BlockSpec((ROUTE_ROWS, tm), lambda i, st: (0, i))],
        out_specs=pl.BlockSpec((SUBLANES, tm), lambda i, st: (0, i)),
    )
    return pl.pallas_call(
        _dest_kernel,
        grid_spec=grid_spec,
        out_shape=jax.ShapeDtypeStruct((SUBLANES, n), I32),
        compiler_params=_cparams(1),
        name="dest",
    )(pad_start, route_t)


def _row_tile(ref, r, rt):
    return ref.at[pl.ds(pl.multiple_of(r * rt, rt), rt)]


def _dispatch_kernel(meta_ref, dest_ref, xn_ref, xs_ref, zero_ref, ring_ref, sem, zsem,
                     *, tm, rt, n_blocks):
    i = pl.program_id(0)
    bm = MOE_BM * rt
    ring = ring_ref.at[i & 1]
    ring[...] = xn_ref[...]

    def per_token(tk, carry):
        src = _row_tile(ring, tk, rt)
        for kk in range(TOP_K):
            r = dest_ref[kk, tk]
            pltpu.make_async_copy(src, _row_tile(xs_ref, r, rt),
                                  sem.at[i & 1]).start(priority=kk % 2)
        return carry

    lax.fori_loop(0, tm, per_token, 0, unroll=4)

    @pl.when(i == 0)
    def _():
        zero_ref[...] = jnp.zeros_like(zero_ref)

        def pad_copy(r):
            return pltpu.make_async_copy(zero_ref.at[pl.ds(0, rt)], _row_tile(xs_ref, r, rt), zsem)

        def per_expert(e, total):
            r0 = meta_ref[e]
            cnt = meta_ref[N_EXPERTS + e]

            def one(r, carry):
                pad_copy(r0 + r).start()
                return carry

            lax.fori_loop(0, cnt, one, 0)
            return total + cnt

        total = lax.fori_loop(0, N_EXPERTS, per_expert, 0)

        def drain(r, carry):
            pad_copy(0).wait()
            return carry

        lax.fori_loop(0, total, drain, 0)

        def tail_copy(b):
            return pltpu.make_async_copy(
                zero_ref, xs_ref.at[pl.ds(pl.multiple_of(b * bm, bm), bm)], zsem)

        n_used = meta_ref[2 * N_EXPERTS]

        def tail(b, carry):
            tail_copy(b).start()
            return carry

        lax.fori_loop(n_used, n_blocks, tail, 0)

        def tail_drain(b, carry):
            tail_copy(0).wait()
            return carry

        lax.fori_loop(n_used, n_blocks, tail_drain, 0)

    def wait_step(parity):
        for _ in range(TOP_K):
            pltpu.make_async_copy(ring_ref.at[parity], xs_ref.at[pl.ds(0, tm * rt)],
                                  sem.at[parity]).wait()

    @pl.when(i > 0)
    def _():
        wait_step(1 - (i & 1))

    @pl.when(i == pl.num_programs(0) - 1)
    def _():
        wait_step(i & 1)


def _dispatch(meta, dest, xn_t, n_tok, n_rows, tm=256):
    rt = xn_t.shape[0] // n_tok
    n_blocks = n_rows // MOE_BM
    grid_spec = pltpu.PrefetchScalarGridSpec(
        num_scalar_prefetch=1,
        grid=(n_tok // tm,),
        in_specs=[
            pl.BlockSpec((SUBLANES, tm), lambda i, meta: (0, i), memory_space=pltpu.SMEM),
            pl.BlockSpec((tm * rt, LANES), lambda i, meta: (i, 0)),
        ],
        out_specs=pl.BlockSpec(memory_space=pl.ANY),
        scratch_shapes=[
            pltpu.VMEM((MOE_BM * rt, LANES), xn_t.dtype),
            pltpu.VMEM((2, tm * rt, LANES), xn_t.dtype),
            pltpu.SemaphoreType.DMA((2,)),
            pltpu.SemaphoreType.DMA,
        ],
    )
    return pl.pallas_call(
        functools.partial(_dispatch_kernel, tm=tm, rt=rt, n_blocks=n_blocks),
        grid_spec=grid_spec,
        out_shape=jax.ShapeDtypeStruct((n_rows * rt, LANES), xn_t.dtype),
        compiler_params=_cparams(1),
        name="dispatch",
    )(meta, dest, xn_t)


def _experts_kernel(be_ref, nu_ref, nxt_ref, par_ref, x_ref, bg_ref, bu_ref, bd_ref,
                    wg_hbm, wu_hbm, wd_hbm, y_ref, wf_ref, wb_ref, sem):
    i = pl.program_id(0)
    used = i < nu_ref[0]
    e = be_ref[i]
    changed = (i == 0) | (e != be_ref[jnp.maximum(i - 1, 0)])

    def weight_copies(expert, slot):
        return [pltpu.make_async_copy(w.at[expert], wf_ref.at[slot, j], sem.at[slot])
                for j, w in enumerate((wg_hbm, wu_hbm, wd_hbm))]

    @pl.when(i == 0)
    def _():
        for cp in weight_copies(e, par_ref[e]):
            cp.start()

    @pl.when(used & changed)
    def _():
        slot = par_ref[e]
        for cp in weight_copies(e, slot):
            cp.wait()
        nxt = nxt_ref[e]

        @pl.when(nxt < N_EXPERTS)
        def _():
            for cp in weight_copies(nxt, 1 - slot):
                cp.start()

        for j in range(3):
            wb_ref[j] = wf_ref[slot, j].astype(BF16)

    @pl.when(used)
    def _():
        x = _unpack_bf16_pairs(_load_row_tiles(x_ref, MOE_BM))
        gt = jnp.minimum(jnp.dot(x, wb_ref[0], preferred_element_type=F32) + bg_ref[...],
                         SWIGLU_LIMIT)
        up = jnp.clip(jnp.dot(x, wb_ref[1], preferred_element_type=F32) + bu_ref[...],
                      -SWIGLU_LIMIT, SWIGLU_LIMIT)
        hid = (up + 1.0) * (gt * _sigmoid(SWIGLU_ALPHA * gt))
        _store_row_tiles(y_ref, jnp.dot(hid.astype(BF16), wb_ref[2],
                                        preferred_element_type=F32) + bd_ref[...])

    @pl.when(jnp.logical_not(used))
    def _():
        y_ref[...] = jnp.zeros_like(y_ref)


def _experts(block_e, n_used, next_e, parity, xs_t, w_gate, b_gate, w_up, b_up, w_down, b_down):
    d, d_ff = w_gate.shape[1:]
    assert d == d_ff, "weight staging buffers assume square expert matrices"
    rt = d // LANES
    rt_x = d // 2 // LANES
    n_blocks = xs_t.shape[0] // (MOE_BM * rt_x)
    bspec = lambda m: pl.BlockSpec((None, 1, m), lambda i, be, nu, nx, pa: (be[i], 0, 0))
    hbm = pl.BlockSpec(memory_space=pl.ANY)
    grid_spec = pltpu.PrefetchScalarGridSpec(
        num_scalar_prefetch=4,
        grid=(n_blocks,),
        in_specs=[
            pl.BlockSpec((MOE_BM * rt_x, LANES),
                         lambda i, be, nu, nx, pa: (jnp.maximum(jnp.minimum(i, nu[0] - 1), 0), 0)),
            bspec(d_ff), bspec(d_ff), bspec(d),
            hbm, hbm, hbm,
        ],
        out_specs=pl.BlockSpec((MOE_BM * rt, LANES), lambda i, be, nu, nx, pa: (i, 0)),
        scratch_shapes=[
            pltpu.VMEM((2, 3, d, d_ff), F32),
            pltpu.VMEM((3, d, d_ff), BF16),
            pltpu.SemaphoreType.DMA((2,)),
        ],
    )
    return pl.pallas_call(
        _experts_kernel,
        grid_spec=grid_spec,
        out_shape=jax.ShapeDtypeStruct((n_blocks * MOE_BM * rt, LANES), F32),
        compiler_params=_cparams(1),
        name="experts",
    )(block_e, n_used, next_e, parity, xs_t, b_gate, b_up, b_down, w_gate, w_up, w_down)


def _combine_kernel(dcur_ref, dnext_ref, h2_ref, route_ref, gain_ref, ys_ref, o_ref,
                    buf_ref, sem, *, tm):
    i = pl.program_id(0)
    nsteps = pl.num_programs(0)
    slot = i & 1
    d = h2_ref.shape[1]
    rt = d // LANES

    def issue_all(dref, s):
        def per_token(tk, carry):
            for kk in range(TOP_K):
                r = dref[kk, tk]
                pltpu.make_async_copy(_row_tile(ys_ref, r, rt),
                                      _row_tile(buf_ref.at[s, kk], tk, rt),
                                      sem.at[s]).start(priority=kk % 2)
            return carry
        lax.fori_loop(0, tm, per_token, 0, unroll=4)

    @pl.when(i == 0)
    def _():
        issue_all(dcur_ref, 0)

    @pl.when(i + 1 < nsteps)
    def _():
        issue_all(dnext_ref, 1 - slot)

    for kk in range(TOP_K):
        pltpu.make_async_copy(ys_ref.at[pl.ds(0, tm * rt)], buf_ref.at[slot, kk],
                              sem.at[slot]).wait()

    rec = route_ref[...]
    acc = h2_ref[...]
    for kk in range(TOP_K):
        acc = acc + (rec[:, ROUTE_GATE + kk:ROUTE_GATE + kk + 1]
                     * _load_row_tiles(buf_ref.at[slot, kk], tm))
    o_ref[...] = _rmsnorm(acc, gain_ref[...])


def _combine(dest, h2, route, gain, ys_t, tm=128):
    n, d = h2.shape
    nsteps = n // tm
    return pl.pallas_call(
        functools.partial(_combine_kernel, tm=tm),
        grid=(nsteps,),
        in_specs=[
            pl.BlockSpec((SUBLANES, tm), lambda i: (0, i), memory_space=pltpu.SMEM),
            pl.BlockSpec((SUBLANES, tm), lambda i: (0, jnp.minimum(i + 1, nsteps - 1)),
                         memory_space=pltpu.SMEM),
            pl.BlockSpec((tm, d), lambda i: (i, 0)),
            pl.BlockSpec((tm, LANES), lambda i: (i, 0)),
            pl.BlockSpec((1, d), lambda i: (0, 0)),
            pl.BlockSpec(memory_space=pl.ANY),
        ],
        out_specs=pl.BlockSpec((tm, d), lambda i: (i, 0)),
        out_shape=jax.ShapeDtypeStruct((n, d), F32),
        scratch_shapes=[
            pltpu.VMEM((2, TOP_K, tm * (d // LANES), LANES), F32),
            pltpu.SemaphoreType.DMA((2,)),
        ],
        compiler_params=_cparams(1),
        name="combine",
    )(dest, dest, h2, route, gain, ys_t)


def kernel(x, mem, norm_mix, w_in, dn_conv, dn_a_log, dn_dt_bias, dn_norm, rg_conv, rg_conv_b, rg_w_a, rg_b_a, rg_w_x, rg_b_x, rg_lambda, w_out, norm_cross, norm_mem, w_cq, w_ckv, w_co, norm_moe, w_router, b_router, w_gate, b_gate, w_up, b_up, w_down, b_down, norm_final):
    batch, seq, d = x.shape
    mem_len = mem.shape[1]
    n = batch * seq
    assert w_in.shape[0] == 1, "single-layer trunk"
    x2 = x.reshape(n, d)

    wi = w_in[0]
    n_gate = 2 * DN_HEADS
    w_cat = jnp.concatenate(
        [wi[:, :4 * D_DN], wi[:, 4 * D_DN + n_gate:],
         jnp.pad(wi[:, 4 * D_DN:4 * D_DN + n_gate], ((0, 0), (0, LANES - n_gate)))],
        axis=1).astype(BF16)
    wbat = wi[:, 4 * D_DN:4 * D_DN + n_gate].T.astype(BF16)
    prow = (jnp.zeros((SUBLANES, LANES), F32)
            .at[0, DN_HEADS:n_gate].set(dn_a_log[0]).at[1, DN_HEADS:n_gate].set(dn_dt_bias[0]))
    pcol = (jnp.zeros((n_gate, LANES), F32)
            .at[DN_HEADS:, 0].set(dn_a_log[0]).at[DN_HEADS:, 1].set(dn_dt_bias[0]))

    proj, gbt = _in_proj(x2, norm_mix, w_cat, wbat, prow, pcol)
    dn = _deltanet(proj, gbt, dn_conv[0], dn_norm, batch, seq)
    rg = _rglru(proj, rg_conv[0], rg_conv_b, rg_w_a[0], rg_b_a[0].reshape(1, D_RG),
                rg_w_x[0], rg_b_x[0].reshape(1, D_RG), rg_lambda, batch, seq)
    kv = _mem_kv(mem.reshape(batch * mem_len, d), norm_mem, w_ckv[0])

    w_r = jnp.pad(w_router[0], ((0, 0), (0, LANES - N_EXPERTS)))
    w_r_hi = w_r.astype(BF16)
    w_r_lo = (w_r - w_r_hi.astype(F32)).astype(BF16)
    b_r = jnp.pad(b_router, ((0, 0), (0, LANES - N_EXPERTS)))
    h2, xn, route, route_t, counts = _post_mix(
        dn, rg, x2, w_out[0], norm_cross, w_cq[0], kv, w_co[0], norm_moe, w_r_hi, w_r_lo, b_r,
        seq, mem_len)

    n_blocks = n * TOP_K // MOE_BM + N_EXPERTS
    n_rows = n_blocks * MOE_BM
    cnt = counts[:, 0].astype(I32)
    padded = (cnt + MOE_BM - 1) // MOE_BM * MOE_BM
    pad_end = jnp.cumsum(padded)
    pad_start = pad_end - padded
    n_used = (pad_end[-1:] // MOE_BM).astype(I32)
    block_e = jnp.minimum(
        jnp.sum(pad_end[None, :] <= (jnp.arange(n_blocks, dtype=I32) * MOE_BM)[:, None], axis=1),
        N_EXPERTS - 1).astype(I32)
    meta = jnp.concatenate([pad_start + cnt, padded - cnt, n_used]).astype(I32)

    dest = _dest(pad_start.astype(I32), route_t)
    xs = _dispatch(meta, dest, xn, n, n_rows)
    has = cnt > 0
    eid = jnp.where(has, jnp.arange(N_EXPERTS, dtype=I32), N_EXPERTS)
    after = lax.cummin(eid, axis=0, reverse=True)
    next_e = jnp.concatenate([after[1:], jnp.full((1,), N_EXPERTS, I32)]).astype(I32)
    parity = ((jnp.cumsum(has.astype(I32)) - 1) & 1).astype(I32)
    ys = _experts(block_e, n_used, next_e, parity, xs, w_gate[0], b_gate[0][:, None, :], w_up[0],
                  b_up[0][:, None, :], w_down[0], b_down[0][:, None, :])
    out = _combine(dest, h2, route, norm_final.reshape(1, d), ys)
    return out.reshape(batch, seq, d)
```

```python
import functools

import jax
import jax.numpy as jnp
from jax import lax
from jax.experimental import pallas as pl
from jax.experimental.pallas import tpu as pltpu

F32 = jnp.float32
BF16 = jnp.bfloat16
I32 = jnp.int32
U32 = jnp.uint32

EPS = 1e-6
LANES = 128
SUBLANES = 8
VMEM_LIMIT = 48 * 1024 * 1024

DN_HEADS = 4
DN_HEAD_DIM = 128
D_DN = DN_HEADS * DN_HEAD_DIM
D_RG = 512
RG_BLOCKS = 4
RG_C = 8.0
CONV_WIDTH = 4
XA_HEADS = 4
N_EXPERTS = 32
TOP_K = 4
SWIGLU_LIMIT = 7.0
SWIGLU_ALPHA = 1.702

DN_CHUNK = 128
INV_BASE = 16
MOE_BM = 256
ROW_CHUNK = 64

COL_RX = 4 * D_DN
COL_RY = COL_RX + D_RG
COL_GB = COL_RY + D_RG
PROJ_W = COL_GB + LANES


def _cparams(n_axes=1):
    return pltpu.CompilerParams(
        dimension_semantics=("arbitrary",) * n_axes, vmem_limit_bytes=VMEM_LIMIT)


def _mm(a, b):
    return jnp.dot(a.astype(BF16), b.astype(BF16), preferred_element_type=F32)


def _mm_nt(a, b):
    return lax.dot_general(a.astype(BF16), b.astype(BF16), (((1,), (1,)), ((), ())),
                           preferred_element_type=F32)


def _rmsnorm(x, g):
    return x * lax.rsqrt(jnp.mean(x * x, axis=-1, keepdims=True) + EPS) * g


def _sigmoid(x):
    return 0.5 * jnp.tanh(0.5 * x) + 0.5


def _mm_split(a, b_hi, b_lo):
    a_hi = a.astype(BF16)
    a_lo = (a - a_hi.astype(F32)).astype(BF16)
    return (jnp.dot(a_hi, b_hi, preferred_element_type=F32)
            + jnp.dot(a_hi, b_lo, preferred_element_type=F32)
            + jnp.dot(a_lo, b_hi, preferred_element_type=F32))


def _softplus(x):
    return jnp.maximum(x, 0.0) + jnp.log1p(jnp.exp(-jnp.abs(x)))


def _load_row_tiles(ref, rows):
    rt = ref.shape[0] // rows
    return jnp.concatenate([ref[pl.ds(s, rows, stride=rt), :] for s in range(rt)], axis=1)


def _store_row_tiles(ref, val):
    rows, w = val.shape
    rt = w // LANES
    for s in range(rt):
        ref[pl.ds(s, rows, stride=rt), :] = val[:, s * LANES:(s + 1) * LANES]


def _pack_bf16_pairs(x):
    half = x.shape[1] // 2
    hi = pltpu.bitcast(x[:, :half].astype(BF16).astype(F32), U32)
    lo = pltpu.bitcast(x[:, half:].astype(BF16).astype(F32), U32)
    return hi | (lo >> 16)


def _unpack_bf16_pairs(p):
    hi = pltpu.bitcast(p & jnp.uint32(0xFFFF0000), F32)
    lo = pltpu.bitcast(p << 16, F32)
    return jnp.concatenate([hi, lo], axis=1).astype(BF16)


def _load_segmented(ref):
    seg = ref.shape[0] // SUBLANES
    return jnp.concatenate([ref[pl.ds(j, SUBLANES, stride=seg), :] for j in range(seg)], axis=0)


def _causal_conv_segmented(xp, halo_ref, cs, w):
    tb = xp.shape[0]
    ng = w.shape[0] - 1
    prev = halo_ref[:, cs]
    last = xp[tb - ng * SUBLANES:, :]
    halo_ref[:, cs] = last
    sub = lax.broadcasted_iota(I32, (SUBLANES, xp.shape[1]), 0)
    groups = []
    for g in range(ng):
        rows = slice(g * SUBLANES, (g + 1) * SUBLANES)
        groups.append(jnp.where(sub == 0, pltpu.roll(prev[rows], 1, 0),
                                pltpu.roll(last[rows], 1, 0)))
    ext = jnp.concatenate(groups + [xp], axis=0)
    y = w[ng:ng + 1] * xp
    for k in range(1, ng + 1):
        y = y + w[ng - k:ng - k + 1] * ext[(ng - k) * SUBLANES:(ng - k) * SUBLANES + tb]
    return y


def _gelu_tanh(x):
    return 0.5 * x * (1.0 + jnp.tanh(0.7978845608028654 * (x + 0.044715 * (x * x * x))))


def _in_proj_kernel(x_ref, g_ref, w_ref, wbat_ref, prow_ref, pcol_ref, conv_ref, proj_ref, gbt_ref,
                    halo_ref, *, steps_per_seq):
    i = pl.program_id(0)
    tm = x_ref.shape[0]
    dh = DN_HEAD_DIM

    @pl.when(i % steps_per_seq == 0)
    def _():
        halo_ref[...] = jnp.zeros_like(halo_ref)

    u = _rmsnorm(x_ref[...], g_ref[...]).astype(BF16)
    sub = lax.broadcasted_iota(I32, (SUBLANES, dh), 0)
    chunk = lambda c: jnp.dot(u, w_ref[:, c:c + 512], preferred_element_type=F32)
    p_next = chunk(0)
    for c0 in range(0, COL_GB, 512):
        p = p_next
        if c0 + 512 < COL_GB:
            p_next = chunk(c0 + 512)
        if c0 < 3 * D_DN:
            for j in range(512 // dh):
                grp = c0 // dh + j
                cs = slice(grp * dh, (grp + 1) * dh)
                prev = halo_ref[:, cs]
                for r0 in range(0, tm, ROW_CHUNK):
                    x = p[r0:r0 + ROW_CHUNK, j * dh:(j + 1) * dh]
                    y = conv_ref[CONV_WIDTH - 1:CONV_WIDTH, cs] * x
                    for k in range(1, CONV_WIDTH):
                        xs = pltpu.roll(x, k, 0)
                        head = jnp.where(sub < k, pltpu.roll(prev, k, 0), xs[:SUBLANES])
                        xs = jnp.concatenate([head, xs[SUBLANES:]], axis=0)
                        y = y + conv_ref[CONV_WIDTH - 1 - k:CONV_WIDTH - k, cs] * xs
                    y = y * _sigmoid(y)
                    if grp < 2 * DN_HEADS:
                        scale = dh ** -0.5 if grp < DN_HEADS else 1.0
                        y = y * (lax.rsqrt(jnp.sum(y * y, axis=-1, keepdims=True) + EPS) * scale)
                    proj_ref[r0:r0 + ROW_CHUNK, cs] = y
                    prev = x[ROW_CHUNK - SUBLANES:, :]
                halo_ref[:, cs] = prev
        elif c0 == 3 * D_DN or c0 == COL_RY:
            act = (lambda v: v * _sigmoid(v)) if c0 == 3 * D_DN else _gelu_tanh
            for r0 in range(0, tm, ROW_CHUNK // 2):
                rows = slice(r0, r0 + ROW_CHUNK // 2)
                proj_ref[rows, c0:c0 + 512] = act(p[rows, :])
        else:
            proj_ref[:, c0:c0 + 512] = p
    ba = jnp.dot(u, w_ref[:, COL_GB:PROJ_W], preferred_element_type=F32)
    lane = lax.broadcasted_iota(I32, ba.shape, 1)
    g = -jnp.exp(prow_ref[0:1, :]) * _softplus(ba + prow_ref[1:2, :])
    proj_ref[:, COL_GB:PROJ_W] = jnp.where(lane < DN_HEADS, _sigmoid(ba), g)
    bat = lax.dot_general(wbat_ref[...], u, (((1,), (1,)), ((), ())), preferred_element_type=F32)
    row = lax.broadcasted_iota(I32, bat.shape, 0)
    gt = -jnp.exp(pcol_ref[:, 0:1]) * _softplus(bat + pcol_ref[:, 1:2])
    gbt_ref[...] = jnp.where(row < DN_HEADS, _sigmoid(bat), gt)


def _in_proj(x2, gain, w_cat, wbat, prow, pcol, dn_conv, seq, tm=512):
    n, d = x2.shape
    return pl.pallas_call(
        functools.partial(_in_proj_kernel, steps_per_seq=seq // tm),
        grid=(n // tm,),
        in_specs=[
            pl.BlockSpec((tm, d), lambda i: (i, 0)),
            pl.BlockSpec((1, d), lambda i: (0, 0)),
            pl.BlockSpec((d, PROJ_W), lambda i: (0, 0)),
            pl.BlockSpec((2 * DN_HEADS, d), lambda i: (0, 0)),
            pl.BlockSpec((SUBLANES, LANES), lambda i: (0, 0)),
            pl.BlockSpec((2 * DN_HEADS, LANES), lambda i: (0, 0)),
            pl.BlockSpec((CONV_WIDTH, 3 * D_DN), lambda i: (0, 0)),
        ],
        out_specs=[
            pl.BlockSpec((tm, PROJ_W), lambda i: (i, 0)),
            pl.BlockSpec((2 * DN_HEADS, tm), lambda i: (0, i)),
        ],
        out_shape=[
            jax.ShapeDtypeStruct((n, PROJ_W), F32),
            jax.ShapeDtypeStruct((2 * DN_HEADS, n), F32),
        ],
        scratch_shapes=[pltpu.VMEM((SUBLANES, 3 * D_DN), F32)],
        compiler_params=_cparams(1),
        name="in_proj",
    )(x2, gain, w_cat, wbat, prow, pcol, dn_conv)


def _deltanet_kernel(*refs, tb):
    n_grp = 3 * DN_HEADS
    act_refs = refs[:n_grp]
    (z_ref, gb_ref, gbt_ref, norm_ref, o_ref,
     s_ref, gcc_ref, gcr_ref,
     a_ref, p_ref, d_ref, qk_ref, rhs_ref, u_ref, wq_ref, kdt_ref) = refs[n_grp:]
    t = pl.program_id(1)
    c = DN_CHUNK
    dh = DN_HEAD_DIM

    @pl.when(t == 0)
    def _():
        s_ref[...] = jnp.zeros_like(s_ref)

    gcol = gb_ref[...]
    rpos = lax.broadcasted_iota(I32, gcol.shape, 0) & (c - 1)
    d = 1
    while d < c:
        gcol = gcol + jnp.where(rpos >= d, pltpu.roll(gcol, d, 0), 0.0)
        d *= 2
    gcc_ref[...] = gcol
    grow = gbt_ref[...]
    lpos = lax.broadcasted_iota(I32, grow.shape, 1) & (c - 1)
    d = 1
    while d < c:
        grow = grow + jnp.where(lpos >= d, pltpu.roll(grow, d, 1), 0.0)
        d *= 2
    gcr_ref[...] = grow

    row = lax.broadcasted_iota(I32, (c, c), 0)
    col = lax.broadcasted_iota(I32, (c, c), 1)
    causal = row >= col
    strict = row > col
    eye = jnp.where(row == col, 1.0, 0.0)
    gain = norm_ref[...]
    n_chunks = tb // c
    probs = [(ci, h) for ci in range(n_chunks) for h in range(DN_HEADS)]

    decay_last = []
    for p, (ci, h) in enumerate(probs):
        rows = slice(ci * c, (ci + 1) * c)
        q = act_refs[h][rows, :]
        k = act_refs[DN_HEADS + h][rows, :]
        v = act_refs[2 * DN_HEADS + h][rows, :]
        beta = gb_ref[rows, h:h + 1]
        gc = gcc_ref[rows, DN_HEADS + h:DN_HEADS + h + 1]
        gr = gcr_ref[DN_HEADS + h:DN_HEADS + h + 1, rows]
        g_last = gc[c - 1:c, :]
        decay = jnp.where(causal, jnp.exp(jnp.where(causal, gc - gr, 0.0)), 0.0)
        kb = k * beta
        both = _mm_nt(jnp.concatenate([kb, q], axis=0), k)
        a_ref[p] = jnp.where(strict, both[:c] * decay, 0.0)
        qk_ref[p] = (both[c:] * decay).astype(BF16)
        egc = jnp.exp(gc)
        rhs_ref[p] = jnp.concatenate([v * beta, kb * egc], axis=1).astype(BF16)
        wq_ref[p, c:, :] = (q * egc).astype(BF16)
        kdt_ref[p] = (k * jnp.exp(g_last - gc)).T.astype(BF16)
        decay_last.append(jnp.exp(g_last))

    shift = INV_BASE.bit_length() - 1
    blk = (row >> shift) == (col >> shift)
    for p in range(len(probs)):
        diag = jnp.where(blk, a_ref[p], 0.0)
        p_ref[p] = eye - diag
        d_ref[p] = _mm(diag, diag).astype(BF16)
    for it in range(shift - 1):
        for p in range(len(probs)):
            pw = d_ref[p]
            inv = p_ref[p]
            p_ref[p] = inv + _mm(inv, pw)
            if it < shift - 2:
                d_ref[p] = _mm(pw, pw).astype(BF16)
    s = INV_BASE
    while s < c:
        sh = s.bit_length() - 1
        off = ((row >> (sh + 1)) == (col >> (sh + 1))) & ((row >> sh) != (col >> sh))
        for p in range(len(probs)):
            d_ref[p] = _mm(p_ref[p], jnp.where(off, a_ref[p], 0.0)).astype(BF16)
        for p in range(len(probs)):
            inv = p_ref[p]
            p_ref[p] = inv - _mm(d_ref[p], inv)
        s *= 2
    for p in range(len(probs)):
        uw = _mm(p_ref[p], rhs_ref[p])
        u_ref[p] = uw[:, :dh]
        wq_ref[p, :c, :] = uw[:, dh:].astype(BF16)

    for ci in range(n_chunks):
        rows = slice(ci * c, (ci + 1) * c)
        ps = [ci * DN_HEADS + h for h in range(DN_HEADS)]
        s_old = [s_ref[h] for h in range(DN_HEADS)]
        ws = [_mm(wq_ref[p], s_old[h]) for h, p in enumerate(ps)]
        v_new = [u_ref[p] - ws[h][:c] for h, p in enumerate(ps)]
        outs = [ws[h][c:] + _mm(qk_ref[p], v_new[h]) for h, p in enumerate(ps)]
        for h, p in enumerate(ps):
            s_ref[h] = s_old[h] * decay_last[p] + _mm(kdt_ref[p], v_new[h])
        for h in range(DN_HEADS):
            o = _rmsnorm(outs[h], gain)
            o_ref[rows, h * dh:(h + 1) * dh] = (
                o * z_ref[rows, h * dh:(h + 1) * dh]).astype(o_ref.dtype)


def _deltanet(proj, gbt, dn_norm, batch, seq, tb=512):
    n = proj.shape[0]
    nt = seq // tb
    c = DN_CHUNK
    n_prob = (tb // c) * DN_HEADS
    n_grp = 3 * DN_HEADS
    grp = lambda j: pl.BlockSpec((tb, DN_HEAD_DIM), lambda b, t, j=j: (b * nt + t, j))
    return pl.pallas_call(
        functools.partial(_deltanet_kernel, tb=tb),
        grid=(batch, nt),
        in_specs=[grp(j) for j in range(n_grp)] + [
            pl.BlockSpec((tb, D_DN), lambda b, t: (b * nt + t, 3)),
            pl.BlockSpec((tb, LANES), lambda b, t: (b * nt + t, COL_GB // LANES)),
            pl.BlockSpec((2 * DN_HEADS, tb), lambda b, t: (0, b * nt + t)),
            pl.BlockSpec((1, DN_HEAD_DIM), lambda b, t: (0, 0)),
        ],
        out_specs=pl.BlockSpec((tb, D_DN), lambda b, t: (b * nt + t, 0)),
        out_shape=jax.ShapeDtypeStruct((n, D_DN), BF16),
        scratch_shapes=[
            pltpu.VMEM((DN_HEADS, DN_HEAD_DIM, DN_HEAD_DIM), F32),
            pltpu.VMEM((tb, LANES), F32),
            pltpu.VMEM((2 * DN_HEADS, tb), F32),
            pltpu.VMEM((n_prob, c, c), F32),
            pltpu.VMEM((n_prob, c, c), F32),
            pltpu.VMEM((n_prob, c, c), BF16),
            pltpu.VMEM((n_prob, c, c), BF16),
            pltpu.VMEM((n_prob, c, 2 * DN_HEAD_DIM), BF16),
            pltpu.VMEM((n_prob, c, DN_HEAD_DIM), F32),
            pltpu.VMEM((n_prob, 2 * c, DN_HEAD_DIM), BF16),
            pltpu.VMEM((n_prob, DN_HEAD_DIM, c), BF16),
        ],
        compiler_params=_cparams(2),
        name="deltanet",
    )(*([proj] * (n_grp + 2)), gbt, dn_norm)


def _rglru_kernel(*refs, tb):
    rx_refs = refs[:RG_BLOCKS]
    (ry_ref, conv_ref, convb_ref, wa_ref, ba_ref, wx_ref, bx_ref, lam_ref,
     o_ref, hc_ref, halo_ref, a_ref, b_ref, h_ref) = refs[RG_BLOCKS:]
    t = pl.program_id(1)
    bw = D_RG // RG_BLOCKS
    seg = tb // SUBLANES

    @pl.when(t == 0)
    def _():
        hc_ref[...] = jnp.zeros_like(hc_ref)
        halo_ref[...] = jnp.zeros_like(halo_ref)

    log_sig = -_softplus(-lam_ref[...])
    rowi = lax.broadcasted_iota(I32, (tb, bw), 0)
    seq_start_row = jnp.where(t == 0, 0, -1)
    for nb in range(RG_BLOCKS):
        cs = slice(nb * bw, (nb + 1) * bw)
        xb = (_causal_conv_segmented(_load_segmented(rx_refs[nb]), halo_ref, cs,
                                     conv_ref[:, cs]) + convb_ref[:, cs])
        r = _sigmoid(_mm(xb, wa_ref[nb]) + ba_ref[:, cs])
        gi = _sigmoid(_mm(xb, wx_ref[nb]) + bx_ref[:, cs])
        log_a = RG_C * r * log_sig[:, cs]
        a = jnp.exp(log_a)
        y = jnp.tanh(-log_a) * (1.0 + a * a)
        mult = jnp.where(y > 0.0, y * lax.rsqrt(y), 0.0)
        mult = jnp.where(rowi == seq_start_row, 1.0, mult)
        a_ref[:, cs] = a
        b_ref[:, cs] = mult * (gi * xb)

    def scan(j, carry):
        ac, bc = carry
        rows = pl.ds(pl.multiple_of(j * SUBLANES, SUBLANES), SUBLANES)
        aj = a_ref[rows, :]
        bc = aj * bc + b_ref[rows, :]
        ac = aj * ac
        a_ref[rows, :] = ac
        b_ref[rows, :] = bc
        return ac, bc

    ac, bc = lax.fori_loop(1, seg, scan, (a_ref[0:SUBLANES, :], b_ref[0:SUBLANES, :]), unroll=7)

    h = hc_ref[...]
    h_in = []
    for s in range(SUBLANES):
        h_in.append(h)
        h = ac[s:s + 1, :] * h + bc[s:s + 1, :]
    hc_ref[...] = h
    h_in = jnp.concatenate(h_in, axis=0)

    for j in range(seg):
        rows = slice(j * SUBLANES, (j + 1) * SUBLANES)
        hj = a_ref[rows, :] * h_in + b_ref[rows, :]
        for nb in range(RG_BLOCKS):
            h_ref[nb, rows, :] = hj[:, nb * bw:(nb + 1) * bw]
    g = seg // SUBLANES
    for nb in range(RG_BLOCKS):
        cs = slice(nb * bw, (nb + 1) * bw)
        h = jnp.concatenate(
            [h_ref[nb, pl.ds((i % g) * SUBLANES * SUBLANES + i // g, SUBLANES, stride=SUBLANES), :]
             for i in range(seg)], axis=0)
        o_ref[:, cs] = (h * ry_ref[:, cs]).astype(o_ref.dtype)


def _rglru(proj, rg_conv, rg_conv_b, w_a, b_a, w_x, b_x, lam, batch, seq, tb=512):
    n = proj.shape[0]
    nt = seq // tb
    bw = D_RG // RG_BLOCKS
    full = lambda shape: pl.BlockSpec(shape, lambda b, t: (0,) * len(shape))
    return pl.pallas_call(
        functools.partial(_rglru_kernel, tb=tb),
        grid=(batch, nt),
        in_specs=[pl.BlockSpec((tb, bw), lambda b, t, j=j: (b * nt + t, COL_RX // bw + j))
                  for j in range(RG_BLOCKS)] + [
            pl.BlockSpec((tb, D_RG), lambda b, t: (b * nt + t, COL_RY // D_RG)),
            full((CONV_WIDTH, D_RG)), full((1, D_RG)),
            full((RG_BLOCKS, bw, bw)), full((1, D_RG)),
            full((RG_BLOCKS, bw, bw)), full((1, D_RG)),
            full((1, D_RG)),
        ],
        out_specs=pl.BlockSpec((tb, D_RG), lambda b, t: (b * nt + t, 0)),
        out_shape=jax.ShapeDtypeStruct((n, D_RG), BF16),
        scratch_shapes=[
            pltpu.VMEM((1, D_RG), F32),
            pltpu.VMEM(((CONV_WIDTH - 1) * SUBLANES, D_RG), F32),
            pltpu.VMEM((tb, D_RG), F32),
            pltpu.VMEM((tb, D_RG), F32),
            pltpu.VMEM((RG_BLOCKS, tb, bw), F32),
        ],
        compiler_params=_cparams(2),
        name="rglru",
    )(*([proj] * (RG_BLOCKS + 1)), rg_conv, rg_conv_b, w_a, b_a, w_x, b_x, lam)


def _mem_kv_kernel(m_ref, g_ref, w_ref, o_ref):
    mn = _rmsnorm(m_ref[...], g_ref[...]).astype(BF16)
    for c0 in range(0, o_ref.shape[1], 512):
        o_ref[:, c0:c0 + 512] = jnp.dot(mn, w_ref[:, c0:c0 + 512].astype(BF16),
                                        preferred_element_type=F32).astype(o_ref.dtype)


def _mem_kv(mem2, gain, w_ckv):
    n, d = mem2.shape
    return pl.pallas_call(
        _mem_kv_kernel,
        grid=(1,),
        in_specs=[
            pl.BlockSpec((n, d), lambda i: (0, 0)),
            pl.BlockSpec((1, d), lambda i: (0, 0)),
            pl.BlockSpec((d, 2 * d), lambda i: (0, 0)),
        ],
        out_specs=pl.BlockSpec((n, 2 * d), lambda i: (0, 0)),
        out_shape=jax.ShapeDtypeStruct((n, 2 * d), BF16),
        compiler_params=_cparams(1),
        name="mem_kv",
    )(mem2, gain, w_ckv)


ROUTE_E = 0
ROUTE_RANK = TOP_K
ROUTE_GATE = 2 * TOP_K
ROUTE_ROWS = 16


def _post_mix_kernel(dn_ref, rg_ref, x_ref, wo_f32, gx_ref, wq_f32, kv_ref, wco_f32, gm_ref,
                     wrh_ref, wrl_ref, br_ref, h2_ref, xn_ref, route_ref, routet_ref, cnt_ref,
                     carry_ref, wo_ref, wq_ref, wco_ref):
    i = pl.program_id(0)
    tm, d = x_ref.shape
    hd = d // XA_HEADS

    @pl.when(i == 0)
    def _():
        carry_ref[...] = jnp.zeros_like(carry_ref)
        wo_ref[...] = wo_f32[...].astype(BF16)
        wq_ref[...] = wq_f32[...].astype(BF16)
        wco_ref[...] = wco_f32[...].astype(BF16)

    h1 = (x_ref[...] + jnp.dot(dn_ref[...], wo_ref[0:D_DN, :], preferred_element_type=F32)
          + jnp.dot(rg_ref[...], wo_ref[D_DN:, :], preferred_element_type=F32))

    hn = _rmsnorm(h1, gx_ref[...]).astype(BF16)
    q = jnp.dot(hn, wq_ref[...], preferred_element_type=F32)
    heads = []
    for hh in range(XA_HEADS):
        cs = slice(hh * hd, (hh + 1) * hd)
        s = _mm_nt(q[:, cs], kv_ref[:, cs]) * (hd ** -0.5)
        p = jnp.exp(s - jnp.max(s, axis=-1, keepdims=True))
        p = p / jnp.sum(p, axis=-1, keepdims=True)
        heads.append(_mm(p, kv_ref[:, d + hh * hd:d + (hh + 1) * hd]).astype(BF16))
    o = jnp.concatenate(heads, axis=1)
    h2 = h1 + jnp.dot(o, wco_ref[...], preferred_element_type=F32)
    h2_ref[...] = h2

    xn = _rmsnorm(h2, gm_ref[...])
    _store_row_tiles(xn_ref, _pack_bf16_pairs(xn))
    logits = _mm_split(xn, wrh_ref[...], wrl_ref[...]) + br_ref[...]
    lg = logits.T[0:N_EXPERTS, :]
    eidx = lax.broadcasted_iota(I32, lg.shape, 0).astype(F32)
    neg = jnp.float32(-jnp.inf)
    vals, idxs, hots = [], [], []
    for _ in range(TOP_K):
        m = jnp.max(lg, axis=0, keepdims=True)
        idx = jnp.min(jnp.where(lg == m, eidx, float(N_EXPERTS)), axis=0, keepdims=True)
        hot = eidx == idx
        lg = jnp.where(hot, neg, lg)
        vals.append(m)
        idxs.append(idx)
        hots.append(hot)
    es = [jnp.exp(v - vals[0]) for v in vals]
    den = es[0] + es[1] + es[2] + es[3]
    gates = [e / den for e in es]

    chosen = jnp.zeros(lg.shape, F32)
    for hot in hots:
        chosen = chosen + jnp.where(hot, 1.0, 0.0)
    r2 = lax.broadcasted_iota(I32, (tm, tm), 0)
    c2 = lax.broadcasted_iota(I32, (tm, tm), 1)
    before = _mm(chosen, jnp.where(r2 < c2, 1.0, 0.0)) + carry_ref[:, 0:1]
    ranks = [jnp.sum(jnp.where(hot, before, 0.0), axis=0, keepdims=True) for hot in hots]
    carry_ref[...] = carry_ref[...] + jnp.sum(chosen, axis=1, keepdims=True)
    cnt_ref[...] = carry_ref[...]

    row = lax.broadcasted_iota(I32, (ROUTE_ROWS, tm), 0)
    rect = jnp.zeros((ROUTE_ROWS, tm), F32)
    for kk in range(TOP_K):
        rect = jnp.where(row == ROUTE_E + kk, idxs[kk], rect)
        rect = jnp.where(row == ROUTE_RANK + kk, ranks[kk], rect)
        rect = jnp.where(row == ROUTE_GATE + kk, gates[kk], rect)
    routet_ref[...] = rect
    route_ref[...] = jnp.concatenate(
        [rect, jnp.zeros((LANES - ROUTE_ROWS, tm), F32)], axis=0).T


def _post_mix(dn, rg, x2, w_out, g_cross, w_cq, kv, w_co, g_moe, w_r_hi, w_r_lo, b_r, seq, mem_len,
              tm=512):
    n, d = x2.shape
    per_b = seq // tm
    rt_x = d // 2 // LANES
    full = lambda shape: pl.BlockSpec(shape, lambda i: (0,) * len(shape))
    once = lambda shape: pl.BlockSpec(shape, lambda i: (0,) * len(shape),
                                      pipeline_mode=pl.Buffered(1))
    return pl.pallas_call(
        _post_mix_kernel,
        grid=(n // tm,),
        in_specs=[
            pl.BlockSpec((tm, D_DN), lambda i: (i, 0)),
            pl.BlockSpec((tm, D_RG), lambda i: (i, 0)),
            pl.BlockSpec((tm, d), lambda i: (i, 0)),
            once((d, d)), full((1, d)), once((d, d)),
            pl.BlockSpec((mem_len, 2 * d), lambda i: (i // per_b, 0)),
            once((d, d)), full((1, d)),
            full((d, LANES)), full((d, LANES)), full((1, LANES)),
        ],
        out_specs=[
            pl.BlockSpec((tm, d), lambda i: (i, 0)),
            pl.BlockSpec((tm * rt_x, LANES), lambda i: (i, 0)),
            pl.BlockSpec((tm, LANES), lambda i: (i, 0)),
            pl.BlockSpec((ROUTE_ROWS, tm), lambda i: (0, i)),
            pl.BlockSpec((N_EXPERTS, LANES), lambda i: (0, 0)),
        ],
        out_shape=[
            jax.ShapeDtypeStruct((n, d), F32),
            jax.ShapeDtypeStruct((n * rt_x, LANES), U32),
            jax.ShapeDtypeStruct((n, LANES), F32),
            jax.ShapeDtypeStruct((ROUTE_ROWS, n), F32),
            jax.ShapeDtypeStruct((N_EXPERTS, LANES), F32),
        ],
        scratch_shapes=[pltpu.VMEM((N_EXPERTS, LANES), F32),
                        pltpu.VMEM((d, d), BF16), pltpu.VMEM((d, d), BF16), pltpu.VMEM((d, d), BF16)],
        compiler_params=_cparams(1),
        name="post_mix",
    )(dn, rg, x2, w_out, g_cross, w_cq, kv, w_co, g_moe, w_r_hi, w_r_lo, b_r)


def _dest_kernel(start_ref, routet_ref, dest_ref):
    e = routet_ref[ROUTE_E:ROUTE_E + SUBLANES, :]
    rank = routet_ref[ROUTE_RANK:ROUTE_RANK + SUBLANES, :]
    base = jnp.zeros(e.shape, F32)
    for j in range(N_EXPERTS):
        base = jnp.where(e == float(j), start_ref[j].astype(F32), base)
    row = lax.broadcasted_iota(I32, e.shape, 0)
    dest_ref[...] = jnp.where(row < TOP_K, (base + rank).astype(I32), 0)


def _dest(pad_start, route_t, tm=2048):
    n = route_t.shape[1]
    grid_spec = pltpu.PrefetchScalarGridSpec(
        num_scalar_prefetch=1,
        grid=(n // tm,),
        in_specs=[pl.BlockSpec((ROUTE_ROWS, tm), lambda i, st: (0, i))],
        out_specs=pl.BlockSpec((SUBLANES, tm), lambda i, st: (0, i)),
    )
    return pl.pallas_call(
        _dest_kernel,
        grid_spec=grid_spec,
        out_shape=jax.ShapeDtypeStruct((SUBLANES, n), I32),
        compiler_params=_cparams(1),
        name="dest",
    )(pad_start, route_t)


def _row_tile(ref, r, rt):
    return ref.at[pl.ds(pl.multiple_of(r * rt, rt), rt)]


def _dispatch_kernel(meta_ref, dest_ref, xn_ref, xs_ref, zero_ref, ring_ref, sem, zsem,
                     *, tm, rt, n_blocks):
    i = pl.program_id(0)
    bm = MOE_BM * rt
    ring = ring_ref.at[i & 1]
    ring[...] = xn_ref[...]

    def per_token(tk, carry):
        src = _row_tile(ring, tk, rt)
        for kk in range(TOP_K):
            r = dest_ref[kk, tk]
            pltpu.make_async_copy(src, _row_tile(xs_ref, r, rt),
                                  sem.at[i & 1]).start(priority=kk % 2)
        return carry

    lax.fori_loop(0, tm, per_token, 0, unroll=4)

    @pl.when(i == 0)
    def _():
        zero_ref[...] = jnp.zeros_like(zero_ref)

        def pad_copy(r):
            return pltpu.make_async_copy(zero_ref.at[pl.ds(0, rt)], _row_tile(xs_ref, r, rt), zsem)

        def per_expert(e, total):
            r0 = meta_ref[e]
            cnt = meta_ref[N_EXPERTS + e]

            def one(r, carry):
                pad_copy(r0 + r).start()
                return carry

            lax.fori_loop(0, cnt, one, 0)
            return total + cnt

        total = lax.fori_loop(0, N_EXPERTS, per_expert, 0)

        def drain(r, carry):
            pad_copy(0).wait()
            return carry

        lax.fori_loop(0, total, drain, 0)

        def tail_copy(b):
            return pltpu.make_async_copy(
                zero_ref, xs_ref.at[pl.ds(pl.multiple_of(b * bm, bm), bm)], zsem)

        n_used = meta_ref[2 * N_EXPERTS]

        def tail(b, carry):
            tail_copy(b).start()
            return carry

        lax.fori_loop(n_used, n_blocks, tail, 0)

        def tail_drain(b, carry):
            tail_copy(0).wait()
            return carry

        lax.fori_loop(n_used, n_blocks, tail_drain, 0)

    def wait_step(parity):
        for _ in range(TOP_K):
            pltpu.make_async_copy(ring_ref.at[parity], xs_ref.at[pl.ds(0, tm * rt)],
                                  sem.at[parity]).wait()

    @pl.when(i > 0)
    def _():
        wait_step(1 - (i & 1))

    @pl.when(i == pl.num_programs(0) - 1)
    def _():
        wait_step(i & 1)


def _dispatch(meta, dest, xn_t, n_tok, n_rows, tm=256):
    rt = xn_t.shape[0] // n_tok
    n_blocks = n_rows // MOE_BM
    grid_spec = pltpu.PrefetchScalarGridSpec(
        num_scalar_prefetch=1,
        grid=(n_tok // tm,),
        in_specs=[
            pl.BlockSpec((SUBLANES, tm), lambda i, meta: (0, i), memory_space=pltpu.SMEM),
            pl.BlockSpec((tm * rt, LANES), lambda i, meta: (i, 0)),
        ],
        out_specs=pl.BlockSpec(memory_space=pl.ANY),
        scratch_shapes=[
            pltpu.VMEM((MOE_BM * rt, LANES), xn_t.dtype),
            pltpu.VMEM((2, tm * rt, LANES), xn_t.dtype),
            pltpu.SemaphoreType.DMA((2,)),
            pltpu.SemaphoreType.DMA,
        ],
    )
    return pl.pallas_call(
        functools.partial(_dispatch_kernel, tm=tm, rt=rt, n_blocks=n_blocks),
        grid_spec=grid_spec,
        out_shape=jax.ShapeDtypeStruct((n_rows * rt, LANES), xn_t.dtype),
        compiler_params=_cparams(1),
        name="dispatch",
    )(meta, dest, xn_t)


def _experts_kernel(be_ref, nu_ref, nxt_ref, par_ref, x_ref, bg_ref, bu_ref, bd_ref,
                    wg_hbm, wu_hbm, wd_hbm, y_ref, wf_ref, wb_ref, sem):
    i = pl.program_id(0)
    used = i < nu_ref[0]
    e = be_ref[i]
    changed = (i == 0) | (e != be_ref[jnp.maximum(i - 1, 0)])

    def weight_copies(expert, slot):
        return [pltpu.make_async_copy(w.at[expert], wf_ref.at[slot, j], sem.at[slot])
                for j, w in enumerate((wg_hbm, wu_hbm, wd_hbm))]

    @pl.when(i == 0)
    def _():
        for cp in weight_copies(e, par_ref[e]):
            cp.start()

    @pl.when(used & changed)
    def _():
        slot = par_ref[e]
        for cp in weight_copies(e, slot):
            cp.wait()
        nxt = nxt_ref[e]

        @pl.when(nxt < N_EXPERTS)
        def _():
            for cp in weight_copies(nxt, 1 - slot):
                cp.start()

        for j in range(3):
            wb_ref[j] = wf_ref[slot, j].astype(BF16)

    @pl.when(used)
    def _():
        x = _unpack_bf16_pairs(_load_row_tiles(x_ref, MOE_BM))
        gt = jnp.minimum(jnp.dot(x, wb_ref[0], preferred_element_type=F32) + bg_ref[...],
                         SWIGLU_LIMIT)
        up = jnp.clip(jnp.dot(x, wb_ref[1], preferred_element_type=F32) + bu_ref[...],
                      -SWIGLU_LIMIT, SWIGLU_LIMIT)
        hid = (up + 1.0) * (gt * _sigmoid(SWIGLU_ALPHA * gt))
        _store_row_tiles(y_ref, jnp.dot(hid.astype(BF16), wb_ref[2],
                                        preferred_element_type=F32) + bd_ref[...])

    @pl.when(jnp.logical_not(used))
    def _():
        y_ref[...] = jnp.zeros_like(y_ref)


def _experts(block_e, n_used, next_e, parity, xs_t, w_gate, b_gate, w_up, b_up, w_down, b_down):
    d, d_ff = w_gate.shape[1:]
    assert d == d_ff, "weight staging buffers assume square expert matrices"
    rt = d // LANES
    rt_x = d // 2 // LANES
    n_blocks = xs_t.shape[0] // (MOE_BM * rt_x)
    bspec = lambda m: pl.BlockSpec((None, 1, m), lambda i, be, nu, nx, pa: (be[i], 0, 0))
    hbm = pl.BlockSpec(memory_space=pl.ANY)
    grid_spec = pltpu.PrefetchScalarGridSpec(
        num_scalar_prefetch=4,
        grid=(n_blocks,),
        in_specs=[
            pl.BlockSpec((MOE_BM * rt_x, LANES),
                         lambda i, be, nu, nx, pa: (jnp.maximum(jnp.minimum(i, nu[0] - 1), 0), 0)),
            bspec(d_ff), bspec(d_ff), bspec(d),
            hbm, hbm, hbm,
        ],
        out_specs=pl.BlockSpec((MOE_BM * rt, LANES), lambda i, be, nu, nx, pa: (i, 0)),
        scratch_shapes=[
            pltpu.VMEM((2, 3, d, d_ff), F32),
            pltpu.VMEM((3, d, d_ff), BF16),
            pltpu.SemaphoreType.DMA((2,)),
        ],
    )
    return pl.pallas_call(
        _experts_kernel,
        grid_spec=grid_spec,
        out_shape=jax.ShapeDtypeStruct((n_blocks * MOE_BM * rt, LANES), F32),
        compiler_params=_cparams(1),
        name="experts",
    )(block_e, n_used, next_e, parity, xs_t, b_gate, b_up, b_down, w_gate, w_up, w_down)


def _combine_kernel(dcur_ref, dnext_ref, h2_ref, route_ref, gain_ref, ys_ref, o_ref,
                    buf_ref, sem, *, tm):
    i = pl.program_id(0)
    nsteps = pl.num_programs(0)
    slot = i & 1
    d = h2_ref.shape[1]
    rt = d // LANES

    def issue_all(dref, s):
        def per_token(tk, carry):
            for kk in range(TOP_K):
                r = dref[kk, tk]
                pltpu.make_async_copy(_row_tile(ys_ref, r, rt),
                                      _row_tile(buf_ref.at[s, kk], tk, rt),
                                      sem.at[s]).start(priority=kk % 2)
            return carry
        lax.fori_loop(0, tm, per_token, 0, unroll=4)

    @pl.when(i == 0)
    def _():
        issue_all(dcur_ref, 0)

    @pl.when(i + 1 < nsteps)
    def _():
        issue_all(dnext_ref, 1 - slot)

    for kk in range(TOP_K):
        pltpu.make_async_copy(ys_ref.at[pl.ds(0, tm * rt)], buf_ref.at[slot, kk],
                              sem.at[slot]).wait()

    rec = route_ref[...]
    acc = h2_ref[...]
    for kk in range(TOP_K):
        acc = acc + (rec[:, ROUTE_GATE + kk:ROUTE_GATE + kk + 1]
                     * _load_row_tiles(buf_ref.at[slot, kk], tm))
    o_ref[...] = _rmsnorm(acc, gain_ref[...])


def _combine(dest, h2, route, gain, ys_t, tm=128):
    n, d = h2.shape
    nsteps = n // tm
    return pl.pallas_call(
        functools.partial(_combine_kernel, tm=tm),
        grid=(nsteps,),
        in_specs=[
            pl.BlockSpec((SUBLANES, tm), lambda i: (0, i), memory_space=pltpu.SMEM),
            pl.BlockSpec((SUBLANES, tm), lambda i: (0, jnp.minimum(i + 1, nsteps - 1)),
                         memory_space=pltpu.SMEM),
            pl.BlockSpec((tm, d), lambda i: (i, 0)),
            pl.BlockSpec((tm, LANES), lambda i: (i, 0)),
            pl.BlockSpec((1, d), lambda i: (0, 0)),
            pl.BlockSpec(memory_space=pl.ANY),
        ],
        out_specs=pl.BlockSpec((tm, d), lambda i: (i, 0)),
        out_shape=jax.ShapeDtypeStruct((n, d), F32),
        scratch_shapes=[
            pltpu.VMEM((2, TOP_K, tm * (d // LANES), LANES), F32),
            pltpu.SemaphoreType.DMA((2,)),
        ],
        compiler_params=_cparams(1),
        name="combine",
    )(dest, dest, h2, route, gain, ys_t)


def kernel(x, mem, norm_mix, w_in, dn_conv, dn_a_log, dn_dt_bias, dn_norm, rg_conv, rg_conv_b, rg_w_a, rg_b_a, rg_w_x, rg_b_x, rg_lambda, w_out, norm_cross, norm_mem, w_cq, w_ckv, w_co, norm_moe, w_router, b_router, w_gate, b_gate, w_up, b_up, w_down, b_down, norm_final):
    batch, seq, d = x.shape
    mem_len = mem.shape[1]
    n = batch * seq
    assert w_in.shape[0] == 1, "single-layer trunk"
    x2 = x.reshape(n, d)

    wi = w_in[0]
    n_gate = 2 * DN_HEADS
    w_cat = jnp.concatenate(
        [wi[:, :4 * D_DN], wi[:, 4 * D_DN + n_gate:],
         jnp.pad(wi[:, 4 * D_DN:4 * D_DN + n_gate], ((0, 0), (0, LANES - n_gate)))],
        axis=1).astype(BF16)
    wbat = wi[:, 4 * D_DN:4 * D_DN + n_gate].T.astype(BF16)
    prow = (jnp.zeros((SUBLANES, LANES), F32)
            .at[0, DN_HEADS:n_gate].set(dn_a_log[0]).at[1, DN_HEADS:n_gate].set(dn_dt_bias[0]))
    pcol = (jnp.zeros((n_gate, LANES), F32)
            .at[DN_HEADS:, 0].set(dn_a_log[0]).at[DN_HEADS:, 1].set(dn_dt_bias[0]))

    proj, gbt = _in_proj(x2, norm_mix, w_cat, wbat, prow, pcol, dn_conv[0], seq)
    dn = _deltanet(proj, gbt, dn_norm, batch, seq)
    rg = _rglru(proj, rg_conv[0], rg_conv_b, rg_w_a[0], rg_b_a[0].reshape(1, D_RG),
                rg_w_x[0], rg_b_x[0].reshape(1, D_RG), rg_lambda, batch, seq)
    kv = _mem_kv(mem.reshape(batch * mem_len, d), norm_mem, w_ckv[0])

    w_r = jnp.pad(w_router[0], ((0, 0), (0, LANES - N_EXPERTS)))
    w_r_hi = w_r.astype(BF16)
    w_r_lo = (w_r - w_r_hi.astype(F32)).astype(BF16)
    b_r = jnp.pad(b_router, ((0, 0), (0, LANES - N_EXPERTS)))
    h2, xn, route, route_t, counts = _post_mix(
        dn, rg, x2, w_out[0], norm_cross, w_cq[0], kv, w_co[0], norm_moe, w_r_hi, w_r_lo, b_r,
        seq, mem_len)

    n_blocks = n * TOP_K // MOE_BM + N_EXPERTS
    n_rows = n_blocks * MOE_BM
    cnt = counts[:, 0].astype(I32)
    padded = (cnt + MOE_BM - 1) // MOE_BM * MOE_BM
    pad_end = jnp.cumsum(padded)
    pad_start = pad_end - padded
    n_used = (pad_end[-1:] // MOE_BM).astype(I32)
    block_e = jnp.minimum(
        jnp.sum(pad_end[None, :] <= (jnp.arange(n_blocks, dtype=I32) * MOE_BM)[:, None], axis=1),
        N_EXPERTS - 1).astype(I32)
    meta = jnp.concatenate([pad_start + cnt, padded - cnt, n_used]).astype(I32)

    dest = _dest(pad_start.astype(I32), route_t)
    xs = _dispatch(meta, dest, xn, n, n_rows)
    has = cnt > 0
    eid = jnp.where(has, jnp.arange(N_EXPERTS, dtype=I32), N_EXPERTS)
    after = lax.cummin(eid, axis=0, reverse=True)
    next_e = jnp.concatenate([after[1:], jnp.full((1,), N_EXPERTS, I32)]).astype(I32)
    parity = ((jnp.cumsum(has.astype(I32)) - 1) & 1).astype(I32)
    ys = _experts(block_e, n_used, next_e, parity, xs, w_gate[0], b_gate[0][:, None, :], w_up[0],
                  b_up[0][:, None, :], w_down[0], b_down[0][:, None, :])
    out = _combine(dest, h2, route, norm_final.reshape(1, d), ys)
    return out.reshape(batch, seq, d)
```

```python
import functools

import jax
import jax.numpy as jnp
from jax import lax
from jax.experimental import pallas as pl
from jax.experimental.pallas import tpu as pltpu

F32 = jnp.float32
BF16 = jnp.bfloat16
I32 = jnp.int32
U32 = jnp.uint32

EPS = 1e-6
LANES = 128
SUBLANES = 8
VMEM_LIMIT = 48 * 1024 * 1024

DN_HEADS = 4
DN_HEAD_DIM = 128
D_DN = DN_HEADS * DN_HEAD_DIM
D_RG = 512
RG_BLOCKS = 4
RG_C = 8.0
CONV_WIDTH = 4
XA_HEADS = 4
N_EXPERTS = 32
TOP_K = 4
SWIGLU_LIMIT = 7.0
SWIGLU_ALPHA = 1.702

DN_CHUNK = 128
INV_BASE = 16
MOE_BM = 256

COL_RX = 4 * D_DN
COL_RY = COL_RX + D_RG
COL_GB = COL_RY + D_RG
PROJ_W = COL_GB + LANES


def _cparams(n_axes=1):
    return pltpu.CompilerParams(
        dimension_semantics=("arbitrary",) * n_axes, vmem_limit_bytes=VMEM_LIMIT)


def _mm(a, b):
    return jnp.dot(a.astype(BF16), b.astype(BF16), preferred_element_type=F32)


def _mm_nt(a, b):
    return lax.dot_general(a.astype(BF16), b.astype(BF16), (((1,), (1,)), ((), ())),
                           preferred_element_type=F32)


def _rmsnorm(x, g):
    return x * lax.rsqrt(jnp.mean(x * x, axis=-1, keepdims=True) + EPS) * g


def _sigmoid(x):
    return 0.5 * jnp.tanh(0.5 * x) + 0.5


def _mm_split(a, b_hi, b_lo):
    a_hi = a.astype(BF16)
    a_lo = (a - a_hi.astype(F32)).astype(BF16)
    return (jnp.dot(a_hi, b_hi, preferred_element_type=F32)
            + jnp.dot(a_hi, b_lo, preferred_element_type=F32)
            + jnp.dot(a_lo, b_hi, preferred_element_type=F32))


def _softplus(x):
    return jnp.maximum(x, 0.0) + jnp.log1p(jnp.exp(-jnp.abs(x)))


def _load_row_tiles(ref, rows):
    rt = ref.shape[0] // rows
    return jnp.concatenate([ref[pl.ds(s, rows, stride=rt), :] for s in range(rt)], axis=1)


def _store_row_tiles(ref, val):
    rows, w = val.shape
    rt = w // LANES
    for s in range(rt):
        ref[pl.ds(s, rows, stride=rt), :] = val[:, s * LANES:(s + 1) * LANES]


def _pack_bf16_pairs(x):
    half = x.shape[1] // 2
    hi = pltpu.bitcast(x[:, :half].astype(BF16).astype(F32), U32)
    lo = pltpu.bitcast(x[:, half:].astype(BF16).astype(F32), U32)
    return hi | (lo >> 16)


def _unpack_pairs_f32(p):
    hi = pltpu.bitcast(p & jnp.uint32(0xFFFF0000), F32)
    lo = pltpu.bitcast(p << 16, F32)
    return jnp.concatenate([hi, lo], axis=1)


def _load_segmented(ref):
    seg = ref.shape[0] // SUBLANES
    return jnp.concatenate([ref[pl.ds(j, SUBLANES, stride=seg), :] for j in range(seg)], axis=0)


def _causal_conv_segmented(xp, halo_ref, cs, w):
    tb = xp.shape[0]
    ng = w.shape[0] - 1
    prev = halo_ref[:, cs]
    last = xp[tb - ng * SUBLANES:, :]
    halo_ref[:, cs] = last
    sub = lax.broadcasted_iota(I32, (SUBLANES, xp.shape[1]), 0)
    groups = []
    for g in range(ng):
        rows = slice(g * SUBLANES, (g + 1) * SUBLANES)
        groups.append(jnp.where(sub == 0, pltpu.roll(prev[rows], 1, 0),
                                pltpu.roll(last[rows], 1, 0)))
    ext = jnp.concatenate(groups + [xp], axis=0)
    y = w[ng:ng + 1] * xp
    for k in range(1, ng + 1):
        y = y + w[ng - k:ng - k + 1] * ext[(ng - k) * SUBLANES:(ng - k) * SUBLANES + tb]
    return y


def _in_proj_kernel(x_ref, g_ref, w_ref, wbat_ref, prow_ref, pcol_ref, proj_ref, gbt_ref):
    u = _rmsnorm(x_ref[...], g_ref[...]).astype(BF16)
    for c0 in range(0, COL_GB, 512):
        proj_ref[:, c0:c0 + 512] = jnp.dot(u, w_ref[:, c0:c0 + 512], preferred_element_type=F32)
    ba = jnp.dot(u, w_ref[:, COL_GB:PROJ_W], preferred_element_type=F32)
    lane = lax.broadcasted_iota(I32, ba.shape, 1)
    g = -jnp.exp(prow_ref[0:1, :]) * _softplus(ba + prow_ref[1:2, :])
    proj_ref[:, COL_GB:PROJ_W] = jnp.where(lane < DN_HEADS, _sigmoid(ba), g)
    bat = lax.dot_general(wbat_ref[...], u, (((1,), (1,)), ((), ())), preferred_element_type=F32)
    row = lax.broadcasted_iota(I32, bat.shape, 0)
    gt = -jnp.exp(pcol_ref[:, 0:1]) * _softplus(bat + pcol_ref[:, 1:2])
    gbt_ref[...] = jnp.where(row < DN_HEADS, _sigmoid(bat), gt)


def _in_proj(x2, gain, w_cat, wbat, prow, pcol, tm=512):
    n, d = x2.shape
    return pl.pallas_call(
        _in_proj_kernel,
        grid=(n // tm,),
        in_specs=[
            pl.BlockSpec((tm, d), lambda i: (i, 0)),
            pl.BlockSpec((1, d), lambda i: (0, 0)),
            pl.BlockSpec((d, PROJ_W), lambda i: (0, 0)),
            pl.BlockSpec((2 * DN_HEADS, d), lambda i: (0, 0)),
            pl.BlockSpec((SUBLANES, LANES), lambda i: (0, 0)),
            pl.BlockSpec((2 * DN_HEADS, LANES), lambda i: (0, 0)),
        ],
        out_specs=[
            pl.BlockSpec((tm, PROJ_W), lambda i: (i, 0)),
            pl.BlockSpec((2 * DN_HEADS, tm), lambda i: (0, i)),
        ],
        out_shape=[
            jax.ShapeDtypeStruct((n, PROJ_W), F32),
            jax.ShapeDtypeStruct((2 * DN_HEADS, n), F32),
        ],
        compiler_params=_cparams(1),
        name="in_proj",
    )(x2, gain, w_cat, wbat, prow, pcol)


def _deltanet_kernel(*refs, tb):
    n_grp = 3 * DN_HEADS
    qkv_refs = refs[:n_grp]
    (z_ref, gb_ref, gbt_ref, conv_ref, norm_ref, o_ref,
     s_ref, halo_ref, act_ref, gcc_ref, gcr_ref,
     a_ref, p_ref, d_ref, qk_ref, rhs_ref, u_ref, wq_ref, kdt_ref) = refs[n_grp:]
    t = pl.program_id(1)
    c = DN_CHUNK
    dh = DN_HEAD_DIM

    @pl.when(t == 0)
    def _():
        s_ref[...] = jnp.zeros_like(s_ref)
        halo_ref[...] = jnp.zeros_like(halo_ref)

    sub = lax.broadcasted_iota(I32, (SUBLANES, dh), 0)
    for grp in range(n_grp):
        cs = slice(grp * dh, (grp + 1) * dh)
        x = qkv_refs[grp][...]
        prev = halo_ref[:, cs]
        halo_ref[:, cs] = x[tb - SUBLANES:, :]
        y = conv_ref[CONV_WIDTH - 1:CONV_WIDTH, cs] * x
        for k in range(1, CONV_WIDTH):
            xs = pltpu.roll(x, k, 0)
            head = jnp.where(sub < k, pltpu.roll(prev, k, 0), xs[:SUBLANES])
            xs = jnp.concatenate([head, xs[SUBLANES:]], axis=0)
            y = y + conv_ref[CONV_WIDTH - 1 - k:CONV_WIDTH - k, cs] * xs
        y = y * _sigmoid(y)
        if grp < 2 * DN_HEADS:
            y = y * lax.rsqrt(jnp.sum(y * y, axis=-1, keepdims=True) + EPS)
        if grp < DN_HEADS:
            y = y * (dh ** -0.5)
        act_ref[grp] = y

    gcol = gb_ref[...]
    rpos = lax.broadcasted_iota(I32, gcol.shape, 0) & (c - 1)
    d = 1
    while d < c:
        gcol = gcol + jnp.where(rpos >= d, pltpu.roll(gcol, d, 0), 0.0)
        d *= 2
    gcc_ref[...] = gcol
    grow = gbt_ref[...]
    lpos = lax.broadcasted_iota(I32, grow.shape, 1) & (c - 1)
    d = 1
    while d < c:
        grow = grow + jnp.where(lpos >= d, pltpu.roll(grow, d, 1), 0.0)
        d *= 2
    gcr_ref[...] = grow

    row = lax.broadcasted_iota(I32, (c, c), 0)
    col = lax.broadcasted_iota(I32, (c, c), 1)
    causal = row >= col
    strict = row > col
    eye = jnp.where(row == col, 1.0, 0.0)
    gain = norm_ref[...]
    n_chunks = tb // c
    probs = [(ci, h) for ci in range(n_chunks) for h in range(DN_HEADS)]

    decay_last = []
    for p, (ci, h) in enumerate(probs):
        rows = slice(ci * c, (ci + 1) * c)
        q = act_ref[h, rows, :]
        k = act_ref[DN_HEADS + h, rows, :]
        v = act_ref[2 * DN_HEADS + h, rows, :]
        beta = gb_ref[rows, h:h + 1]
        gc = gcc_ref[rows, DN_HEADS + h:DN_HEADS + h + 1]
        gr = gcr_ref[DN_HEADS + h:DN_HEADS + h + 1, rows]
        g_last = gc[c - 1:c, :]
        decay = jnp.where(causal, jnp.exp(jnp.where(causal, gc - gr, 0.0)), 0.0)
        kb = k * beta
        both = _mm_nt(jnp.concatenate([kb, q], axis=0), k)
        a_ref[p] = jnp.where(strict, both[:c] * decay, 0.0)
        qk_ref[p] = (both[c:] * decay).astype(BF16)
        egc = jnp.exp(gc)
        rhs_ref[p] = jnp.concatenate([v * beta, kb * egc], axis=1).astype(BF16)
        wq_ref[p, c:, :] = (q * egc).astype(BF16)
        kdt_ref[p] = (k * jnp.exp(g_last - gc)).T.astype(BF16)
        decay_last.append(jnp.exp(g_last))

    shift = INV_BASE.bit_length() - 1
    blk = (row >> shift) == (col >> shift)
    for p in range(len(probs)):
        diag = jnp.where(blk, a_ref[p], 0.0)
        p_ref[p] = eye - diag
        d_ref[p] = _mm(diag, diag).astype(BF16)
    for it in range(shift - 1):
        for p in range(len(probs)):
            pw = d_ref[p]
            inv = p_ref[p]
            p_ref[p] = inv + _mm(inv, pw)
            if it < shift - 2:
                d_ref[p] = _mm(pw, pw).astype(BF16)
    s = INV_BASE
    while s < c:
        sh = s.bit_length() - 1
        off = ((row >> (sh + 1)) == (col >> (sh + 1))) & ((row >> sh) != (col >> sh))
        for p in range(len(probs)):
            d_ref[p] = _mm(p_ref[p], jnp.where(off, a_ref[p], 0.0)).astype(BF16)
        for p in range(len(probs)):
            inv = p_ref[p]
            p_ref[p] = inv - _mm(d_ref[p], inv)
        s *= 2
    for p in range(len(probs)):
        uw = _mm(p_ref[p], rhs_ref[p])
        u_ref[p] = uw[:, :dh]
        wq_ref[p, :c, :] = uw[:, dh:].astype(BF16)

    for ci in range(n_chunks):
        rows = slice(ci * c, (ci + 1) * c)
        ps = [ci * DN_HEADS + h for h in range(DN_HEADS)]
        s_old = [s_ref[h] for h in range(DN_HEADS)]
        ws = [_mm(wq_ref[p], s_old[h]) for h, p in enumerate(ps)]
        v_new = [u_ref[p] - ws[h][:c] for h, p in enumerate(ps)]
        outs = [ws[h][c:] + _mm(qk_ref[p], v_new[h]) for h, p in enumerate(ps)]
        for h, p in enumerate(ps):
            s_ref[h] = s_old[h] * decay_last[p] + _mm(kdt_ref[p], v_new[h])
        for h in range(DN_HEADS):
            o = _rmsnorm(outs[h], gain)
            zz = z_ref[rows, h * dh:(h + 1) * dh]
            o_ref[rows, h * dh:(h + 1) * dh] = (o * (zz * _sigmoid(zz))).astype(o_ref.dtype)


def _deltanet(proj, gbt, dn_conv, dn_norm, batch, seq, tb=512):
    n = proj.shape[0]
    nt = seq // tb
    c = DN_CHUNK
    n_prob = (tb // c) * DN_HEADS
    n_grp = 3 * DN_HEADS
    grp = lambda j: pl.BlockSpec((tb, DN_HEAD_DIM), lambda b, t, j=j: (b * nt + t, j))
    return pl.pallas_call(
        functools.partial(_deltanet_kernel, tb=tb),
        grid=(batch, nt),
        in_specs=[grp(j) for j in range(n_grp)] + [
            pl.BlockSpec((tb, D_DN), lambda b, t: (b * nt + t, 3)),
            pl.BlockSpec((tb, LANES), lambda b, t: (b * nt + t, COL_GB // LANES)),
            pl.BlockSpec((2 * DN_HEADS, tb), lambda b, t: (0, b * nt + t)),
            pl.BlockSpec((CONV_WIDTH, 3 * D_DN), lambda b, t: (0, 0)),
            pl.BlockSpec((1, DN_HEAD_DIM), lambda b, t: (0, 0)),
        ],
        out_specs=pl.BlockSpec((tb, D_DN), lambda b, t: (b * nt + t, 0)),
        out_shape=jax.ShapeDtypeStruct((n, D_DN), BF16),
        scratch_shapes=[
            pltpu.VMEM((DN_HEADS, DN_HEAD_DIM, DN_HEAD_DIM), F32),
            pltpu.VMEM((SUBLANES, 3 * D_DN), F32),
            pltpu.VMEM((n_grp, tb, DN_HEAD_DIM), F32),
            pltpu.VMEM((tb, LANES), F32),
            pltpu.VMEM((2 * DN_HEADS, tb), F32),
            pltpu.VMEM((n_prob, c, c), F32),
            pltpu.VMEM((n_prob, c, c), F32),
            pltpu.VMEM((n_prob, c, c), BF16),
            pltpu.VMEM((n_prob, c, c), BF16),
            pltpu.VMEM((n_prob, c, 2 * DN_HEAD_DIM), BF16),
            pltpu.VMEM((n_prob, c, DN_HEAD_DIM), F32),
            pltpu.VMEM((n_prob, 2 * c, DN_HEAD_DIM), BF16),
            pltpu.VMEM((n_prob, DN_HEAD_DIM, c), BF16),
        ],
        compiler_params=_cparams(2),
        name="deltanet",
    )(*([proj] * (n_grp + 2)), gbt, dn_conv, dn_norm)


def _gelu_tanh(x):
    return 0.5 * x * (1.0 + jnp.tanh(0.7978845608028654 * (x + 0.044715 * (x * x * x))))


def _rglru_kernel(*refs, tb):
    rx_refs = refs[:RG_BLOCKS]
    (ry_ref, conv_ref, convb_ref, wa_ref, ba_ref, wx_ref, bx_ref, lam_ref,
     o_ref, hc_ref, halo_ref, a_ref, b_ref, h_ref) = refs[RG_BLOCKS:]
    t = pl.program_id(1)
    bw = D_RG // RG_BLOCKS
    seg = tb // SUBLANES

    @pl.when(t == 0)
    def _():
        hc_ref[...] = jnp.zeros_like(hc_ref)
        halo_ref[...] = jnp.zeros_like(halo_ref)

    log_sig = -_softplus(-lam_ref[...])
    rowi = lax.broadcasted_iota(I32, (tb, bw), 0)
    seq_start_row = jnp.where(t == 0, 0, -1)
    for nb in range(RG_BLOCKS):
        cs = slice(nb * bw, (nb + 1) * bw)
        xb = (_causal_conv_segmented(_load_segmented(rx_refs[nb]), halo_ref, cs,
                                     conv_ref[:, cs]) + convb_ref[:, cs])
        r = _sigmoid(_mm(xb, wa_ref[nb]) + ba_ref[:, cs])
        gi = _sigmoid(_mm(xb, wx_ref[nb]) + bx_ref[:, cs])
        log_a = RG_C * r * log_sig[:, cs]
        a = jnp.exp(log_a)
        y = jnp.tanh(-log_a) * (1.0 + a * a)
        mult = jnp.where(y > 0.0, y * lax.rsqrt(y), 0.0)
        mult = jnp.where(rowi == seq_start_row, 1.0, mult)
        a_ref[:, cs] = a
        b_ref[:, cs] = mult * (gi * xb)

    def scan(j, carry):
        ac, bc = carry
        rows = pl.ds(pl.multiple_of(j * SUBLANES, SUBLANES), SUBLANES)
        aj = a_ref[rows, :]
        bc = aj * bc + b_ref[rows, :]
        ac = aj * ac
        a_ref[rows, :] = ac
        b_ref[rows, :] = bc
        return ac, bc

    ac, bc = lax.fori_loop(1, seg, scan, (a_ref[0:SUBLANES, :], b_ref[0:SUBLANES, :]), unroll=7)

    h = hc_ref[...]
    h_in = []
    for s in range(SUBLANES):
        h_in.append(h)
        h = ac[s:s + 1, :] * h + bc[s:s + 1, :]
    hc_ref[...] = h
    h_in = jnp.concatenate(h_in, axis=0)

    for j in range(seg):
        rows = slice(j * SUBLANES, (j + 1) * SUBLANES)
        hj = a_ref[rows, :] * h_in + b_ref[rows, :]
        for nb in range(RG_BLOCKS):
            h_ref[nb, rows, :] = hj[:, nb * bw:(nb + 1) * bw]
    g = seg // SUBLANES
    for nb in range(RG_BLOCKS):
        cs = slice(nb * bw, (nb + 1) * bw)
        h = jnp.concatenate(
            [h_ref[nb, pl.ds((i % g) * SUBLANES * SUBLANES + i // g, SUBLANES, stride=SUBLANES), :]
             for i in range(seg)], axis=0)
        o_ref[:, cs] = (h * _gelu_tanh(ry_ref[:, cs])).astype(o_ref.dtype)


def _rglru(proj, rg_conv, rg_conv_b, w_a, b_a, w_x, b_x, lam, batch, seq, tb=512):
    n = proj.shape[0]
    nt = seq // tb
    bw = D_RG // RG_BLOCKS
    full = lambda shape: pl.BlockSpec(shape, lambda b, t: (0,) * len(shape))
    return pl.pallas_call(
        functools.partial(_rglru_kernel, tb=tb),
        grid=(batch, nt),
        in_specs=[pl.BlockSpec((tb, bw), lambda b, t, j=j: (b * nt + t, COL_RX // bw + j))
                  for j in range(RG_BLOCKS)] + [
            pl.BlockSpec((tb, D_RG), lambda b, t: (b * nt + t, COL_RY // D_RG)),
            full((CONV_WIDTH, D_RG)), full((1, D_RG)),
            full((RG_BLOCKS, bw, bw)), full((1, D_RG)),
            full((RG_BLOCKS, bw, bw)), full((1, D_RG)),
            full((1, D_RG)),
        ],
        out_specs=pl.BlockSpec((tb, D_RG), lambda b, t: (b * nt + t, 0)),
        out_shape=jax.ShapeDtypeStruct((n, D_RG), BF16),
        scratch_shapes=[
            pltpu.VMEM((1, D_RG), F32),
            pltpu.VMEM(((CONV_WIDTH - 1) * SUBLANES, D_RG), F32),
            pltpu.VMEM((tb, D_RG), F32),
            pltpu.VMEM((tb, D_RG), F32),
            pltpu.VMEM((RG_BLOCKS, tb, bw), F32),
        ],
        compiler_params=_cparams(2),
        name="rglru",
    )(*([proj] * (RG_BLOCKS + 1)), rg_conv, rg_conv_b, w_a, b_a, w_x, b_x, lam)


def _mem_kv_kernel(m_ref, g_ref, w_ref, o_ref):
    mn = _rmsnorm(m_ref[...], g_ref[...]).astype(BF16)
    for c0 in range(0, o_ref.shape[1], 512):
        o_ref[:, c0:c0 + 512] = jnp.dot(mn, w_ref[:, c0:c0 + 512].astype(BF16),
                                        preferred_element_type=F32).astype(o_ref.dtype)


def _mem_kv(mem2, gain, w_ckv):
    n, d = mem2.shape
    return pl.pallas_call(
        _mem_kv_kernel,
        grid=(1,),
        in_specs=[
            pl.BlockSpec((n, d), lambda i: (0, 0)),
            pl.BlockSpec((1, d), lambda i: (0, 0)),
            pl.BlockSpec((d, 2 * d), lambda i: (0, 0)),
        ],
        out_specs=pl.BlockSpec((n, 2 * d), lambda i: (0, 0)),
        out_shape=jax.ShapeDtypeStruct((n, 2 * d), BF16),
        compiler_params=_cparams(1),
        name="mem_kv",
    )(mem2, gain, w_ckv)


ROUTE_E = 0
ROUTE_RANK = TOP_K
ROUTE_GATE = 2 * TOP_K
ROUTE_ROWS = 16


def _post_mix_kernel(dn_ref, rg_ref, x_ref, wo_f32, gx_ref, wq_f32, kv_ref, wco_f32, gm_ref,
                     wrh_ref, wrl_ref, br_ref, h2_ref, xn_ref, route_ref, routet_ref, cnt_ref,
                     carry_ref, wo_ref, wq_ref, wco_ref):
    i = pl.program_id(0)
    tm, d = x_ref.shape
    hd = d // XA_HEADS

    @pl.when(i == 0)
    def _():
        carry_ref[...] = jnp.zeros_like(carry_ref)
        wo_ref[...] = wo_f32[...].astype(BF16)
        wq_ref[...] = wq_f32[...].astype(BF16)
        wco_ref[...] = wco_f32[...].astype(BF16)

    h1 = (x_ref[...] + jnp.dot(dn_ref[...], wo_ref[0:D_DN, :], preferred_element_type=F32)
          + jnp.dot(rg_ref[...], wo_ref[D_DN:, :], preferred_element_type=F32))

    hn = _rmsnorm(h1, gx_ref[...]).astype(BF16)
    q = jnp.dot(hn, wq_ref[...], preferred_element_type=F32)
    heads = []
    for hh in range(XA_HEADS):
        cs = slice(hh * hd, (hh + 1) * hd)
        s = _mm_nt(q[:, cs], kv_ref[:, cs]) * (hd ** -0.5)
        p = jnp.exp(s - jnp.max(s, axis=-1, keepdims=True))
        p = p / jnp.sum(p, axis=-1, keepdims=True)
        heads.append(_mm(p, kv_ref[:, d + hh * hd:d + (hh + 1) * hd]).astype(BF16))
    o = jnp.concatenate(heads, axis=1)
    h2 = h1 + jnp.dot(o, wco_ref[...], preferred_element_type=F32)
    h2_ref[...] = h2

    xn = _rmsnorm(h2, gm_ref[...])
    _store_row_tiles(xn_ref, _pack_bf16_pairs(xn))
    logits = _mm_split(xn, wrh_ref[...], wrl_ref[...]) + br_ref[...]
    lg = logits.T[0:N_EXPERTS, :]
    eidx = lax.broadcasted_iota(I32, lg.shape, 0).astype(F32)
    neg = jnp.float32(-jnp.inf)
    vals, idxs, hots = [], [], []
    for _ in range(TOP_K):
        m = jnp.max(lg, axis=0, keepdims=True)
        idx = jnp.min(jnp.where(lg == m, eidx, float(N_EXPERTS)), axis=0, keepdims=True)
        hot = eidx == idx
        lg = jnp.where(hot, neg, lg)
        vals.append(m)
        idxs.append(idx)
        hots.append(hot)
    es = [jnp.exp(v - vals[0]) for v in vals]
    den = es[0] + es[1] + es[2] + es[3]
    gates = [e / den for e in es]

    chosen = jnp.zeros(lg.shape, F32)
    for hot in hots:
        chosen = chosen + jnp.where(hot, 1.0, 0.0)
    r2 = lax.broadcasted_iota(I32, (tm, tm), 0)
    c2 = lax.broadcasted_iota(I32, (tm, tm), 1)
    before = _mm(chosen, jnp.where(r2 < c2, 1.0, 0.0)) + carry_ref[:, 0:1]
    ranks = [jnp.sum(jnp.where(hot, before, 0.0), axis=0, keepdims=True) for hot in hots]
    carry_ref[...] = carry_ref[...] + jnp.sum(chosen, axis=1, keepdims=True)
    cnt_ref[...] = carry_ref[...]

    row = lax.broadcasted_iota(I32, (ROUTE_ROWS, tm), 0)
    rect = jnp.zeros((ROUTE_ROWS, tm), F32)
    for kk in range(TOP_K):
        rect = jnp.where(row == ROUTE_E + kk, idxs[kk], rect)
        rect = jnp.where(row == ROUTE_RANK + kk, ranks[kk], rect)
        rect = jnp.where(row == ROUTE_GATE + kk, gates[kk], rect)
    routet_ref[...] = rect
    route_ref[...] = jnp.concatenate(
        [rect, jnp.zeros((LANES - ROUTE_ROWS, tm), F32)], axis=0).T


def _post_mix(dn, rg, x2, w_out, g_cross, w_cq, kv, w_co, g_moe, w_r_hi, w_r_lo, b_r, seq, mem_len,
              tm=512):
    n, d = x2.shape
    per_b = seq // tm
    rt_x = d // 2 // LANES
    full = lambda shape: pl.BlockSpec(shape, lambda i: (0,) * len(shape))
    once = lambda shape: pl.BlockSpec(shape, lambda i: (0,) * len(shape),
                                      pipeline_mode=pl.Buffered(1))
    return pl.pallas_call(
        _post_mix_kernel,
        grid=(n // tm,),
        in_specs=[
            pl.BlockSpec((tm, D_DN), lambda i: (i, 0)),
            pl.BlockSpec((tm, D_RG), lambda i: (i, 0)),
            pl.BlockSpec((tm, d), lambda i: (i, 0)),
            once((d, d)), full((1, d)), once((d, d)),
            pl.BlockSpec((mem_len, 2 * d), lambda i: (i // per_b, 0)),
            once((d, d)), full((1, d)),
            full((d, LANES)), full((d, LANES)), full((1, LANES)),
        ],
        out_specs=[
            pl.BlockSpec((tm, d), lambda i: (i, 0)),
            pl.BlockSpec((tm * rt_x, LANES), lambda i: (i, 0)),
            pl.BlockSpec((tm, LANES), lambda i: (i, 0)),
            pl.BlockSpec((ROUTE_ROWS, tm), lambda i: (0, i)),
            pl.BlockSpec((N_EXPERTS, LANES), lambda i: (0, 0)),
        ],
        out_shape=[
            jax.ShapeDtypeStruct((n, d), F32),
            jax.ShapeDtypeStruct((n * rt_x, LANES), U32),
            jax.ShapeDtypeStruct((n, LANES), F32),
            jax.ShapeDtypeStruct((ROUTE_ROWS, n), F32),
            jax.ShapeDtypeStruct((N_EXPERTS, LANES), F32),
        ],
        scratch_shapes=[pltpu.VMEM((N_EXPERTS, LANES), F32),
                        pltpu.VMEM((d, d), BF16), pltpu.VMEM((d, d), BF16), pltpu.VMEM((d, d), BF16)],
        compiler_params=_cparams(1),
        name="post_mix",
    )(dn, rg, x2, w_out, g_cross, w_cq, kv, w_co, g_moe, w_r_hi, w_r_lo, b_r)


def _dest_kernel(start_ref, routet_ref, dest_ref):
    e = routet_ref[ROUTE_E:ROUTE_E + SUBLANES, :]
    rank = routet_ref[ROUTE_RANK:ROUTE_RANK + SUBLANES, :]
    base = jnp.zeros(e.shape, F32)
    for j in range(N_EXPERTS):
        base = jnp.where(e == float(j), start_ref[j].astype(F32), base)
    row = lax.broadcasted_iota(I32, e.shape, 0)
    dest_ref[...] = jnp.where(row < TOP_K, (base + rank).astype(I32), 0)


def _dest(pad_start, route_t, tm=2048):
    n = route_t.shape[1]
    grid_spec = pltpu.PrefetchScalarGridSpec(
        num_scalar_prefetch=1,
        grid=(n // tm,),
        in_specs=[pl.BlockSpec((ROUTE_ROWS, tm), lambda i, st: (0, i))],
        out_specs=pl.BlockSpec((SUBLANES, tm), lambda i, st: (0, i)),
    )
    return pl.pallas_call(
        _dest_kernel,
        grid_spec=grid_spec,
        out_shape=jax.ShapeDtypeStruct((SUBLANES, n), I32),
        compiler_params=_cparams(1),
        name="dest",
    )(pad_start, route_t)


def _row_tile(ref, r, rt):
    return ref.at[pl.ds(pl.multiple_of(r * rt, rt), rt)]


def _dispatch_kernel(meta_ref, dest_ref, xn_ref, xs_ref, zero_ref, ring_ref, sem, zsem,
                     *, tm, rt, n_blocks):
    i = pl.program_id(0)
    bm = MOE_BM * rt
    ring = ring_ref.at[i & 1]
    ring[...] = xn_ref[...]

    def per_token(tk, carry):
        src = _row_tile(ring, tk, rt)
        for kk in range(TOP_K):
            r = dest_ref[kk, tk]
            pltpu.make_async_copy(src, _row_tile(xs_ref, r, rt),
                                  sem.at[i & 1]).start(priority=kk % 2)
        return carry

    lax.fori_loop(0, tm, per_token, 0, unroll=4)

    @pl.when(i == 0)
    def _():
        zero_ref[...] = jnp.zeros_like(zero_ref)

        def pad_copy(r):
            return pltpu.make_async_copy(zero_ref.at[pl.ds(0, rt)], _row_tile(xs_ref, r, rt), zsem)

        def per_expert(e, total):
            r0 = meta_ref[e]
            cnt = meta_ref[N_EXPERTS + e]

            def one(r, carry):
                pad_copy(r0 + r).start()
                return carry

            lax.fori_loop(0, cnt, one, 0)
            return total + cnt

        total = lax.fori_loop(0, N_EXPERTS, per_expert, 0)

        def drain(r, carry):
            pad_copy(0).wait()
            return carry

        lax.fori_loop(0, total, drain, 0)

        def tail_copy(b):
            return pltpu.make_async_copy(
                zero_ref, xs_ref.at[pl.ds(pl.multiple_of(b * bm, bm), bm)], zsem)

        n_used = meta_ref[2 * N_EXPERTS]

        def tail(b, carry):
            tail_copy(b).start()
            return carry

        lax.fori_loop(n_used, n_blocks, tail, 0)

        def tail_drain(b, carry):
            tail_copy(0).wait()
            return carry

        lax.fori_loop(n_used, n_blocks, tail_drain, 0)

    def wait_step(parity):
        for _ in range(TOP_K):
            pltpu.make_async_copy(ring_ref.at[parity], xs_ref.at[pl.ds(0, tm * rt)],
                                  sem.at[parity]).wait()

    @pl.when(i > 0)
    def _():
        wait_step(1 - (i & 1))

    @pl.when(i == pl.num_programs(0) - 1)
    def _():
        wait_step(i & 1)


def _dispatch(meta, dest, xn_t, n_tok, n_rows, tm=256):
    rt = xn_t.shape[0] // n_tok
    n_blocks = n_rows // MOE_BM
    grid_spec = pltpu.PrefetchScalarGridSpec(
        num_scalar_prefetch=1,
        grid=(n_tok // tm,),
        in_specs=[
            pl.BlockSpec((SUBLANES, tm), lambda i, meta: (0, i), memory_space=pltpu.SMEM),
            pl.BlockSpec((tm * rt, LANES), lambda i, meta: (i, 0)),
        ],
        out_specs=pl.BlockSpec(memory_space=pl.ANY),
        scratch_shapes=[
            pltpu.VMEM((MOE_BM * rt, LANES), xn_t.dtype),
            pltpu.VMEM((2, tm * rt, LANES), xn_t.dtype),
            pltpu.SemaphoreType.DMA((2,)),
            pltpu.SemaphoreType.DMA,
        ],
    )
    return pl.pallas_call(
        functools.partial(_dispatch_kernel, tm=tm, rt=rt, n_blocks=n_blocks),
        grid_spec=grid_spec,
        out_shape=jax.ShapeDtypeStruct((n_rows * rt, LANES), xn_t.dtype),
        compiler_params=_cparams(1),
        name="dispatch",
    )(meta, dest, xn_t)


def _experts_kernel(be_ref, nu_ref, nxt_ref, par_ref, x_ref, bg_ref, bu_ref, bd_ref,
                    wg_hbm, wu_hbm, wd_hbm, y_ref, wf_ref, wb_ref, sem):
    i = pl.program_id(0)
    used = i < nu_ref[0]
    e = be_ref[i]
    changed = (i == 0) | (e != be_ref[jnp.maximum(i - 1, 0)])

    def weight_copies(expert, slot):
        return [pltpu.make_async_copy(w.at[expert], wf_ref.at[slot, j], sem.at[slot])
                for j, w in enumerate((wg_hbm, wu_hbm, wd_hbm))]

    @pl.when(i == 0)
    def _():
        for cp in weight_copies(e, par_ref[e]):
            cp.start()

    @pl.when(used & changed)
    def _():
        slot = par_ref[e]
        for cp in weight_copies(e, slot):
            cp.wait()
        nxt = nxt_ref[e]

        @pl.when(nxt < N_EXPERTS)
        def _():
            for cp in weight_copies(nxt, 1 - slot):
                cp.start()

        for j in range(3):
            wb_ref[j] = wf_ref[slot, j].astype(BF16)

    @pl.when(used)
    def _():
        x = _unpack_pairs_f32(_load_row_tiles(x_ref, MOE_BM)).astype(BF16)
        gt = jnp.minimum(jnp.dot(x, wb_ref[0], preferred_element_type=F32) + bg_ref[...],
                         SWIGLU_LIMIT)
        up = jnp.clip(jnp.dot(x, wb_ref[1], preferred_element_type=F32) + bu_ref[...],
                      -SWIGLU_LIMIT, SWIGLU_LIMIT)
        hid = (up + 1.0) * (gt * _sigmoid(SWIGLU_ALPHA * gt))
        y = jnp.dot(hid.astype(BF16), wb_ref[2], preferred_element_type=F32) + bd_ref[...]
        _store_row_tiles(y_ref, _pack_bf16_pairs(y))

    @pl.when(jnp.logical_not(used))
    def _():
        y_ref[...] = jnp.zeros_like(y_ref)


def _experts(block_e, n_used, next_e, parity, xs_t, w_gate, b_gate, w_up, b_up, w_down, b_down):
    d, d_ff = w_gate.shape[1:]
    assert d == d_ff, "weight staging buffers assume square expert matrices"
    rt = d // 2 // LANES
    n_blocks = xs_t.shape[0] // (MOE_BM * rt)
    bspec = lambda m: pl.BlockSpec((None, 1, m), lambda i, be, nu, nx, pa: (be[i], 0, 0))
    hbm = pl.BlockSpec(memory_space=pl.ANY)
    grid_spec = pltpu.PrefetchScalarGridSpec(
        num_scalar_prefetch=4,
        grid=(n_blocks,),
        in_specs=[
            pl.BlockSpec((MOE_BM * rt, LANES),
                         lambda i, be, nu, nx, pa: (jnp.maximum(jnp.minimum(i, nu[0] - 1), 0), 0)),
            bspec(d_ff), bspec(d_ff), bspec(d),
            hbm, hbm, hbm,
        ],
        out_specs=pl.BlockSpec((MOE_BM * rt, LANES), lambda i, be, nu, nx, pa: (i, 0)),
        scratch_shapes=[
            pltpu.VMEM((2, 3, d, d_ff), F32),
            pltpu.VMEM((3, d, d_ff), BF16),
            pltpu.SemaphoreType.DMA((2,)),
        ],
    )
    return pl.pallas_call(
        _experts_kernel,
        grid_spec=grid_spec,
        out_shape=jax.ShapeDtypeStruct(xs_t.shape, U32),
        compiler_params=_cparams(1),
        name="experts",
    )(block_e, n_used, next_e, parity, xs_t, b_gate, b_up, b_down, w_gate, w_up, w_down)


def _combine_kernel(dcur_ref, dnext_ref, h2_ref, route_ref, gain_ref, ys_ref, o_ref,
                    buf_ref, sem, *, tm, rt):
    i = pl.program_id(0)
    nsteps = pl.num_programs(0)
    slot = i & 1

    def issue_all(dref, s):
        def per_token(tk, carry):
            for kk in range(TOP_K):
                r = dref[kk, tk]
                pltpu.make_async_copy(_row_tile(ys_ref, r, rt),
                                      _row_tile(buf_ref.at[s, kk], tk, rt),
                                      sem.at[s]).start(priority=kk % 2)
            return carry
        lax.fori_loop(0, tm, per_token, 0, unroll=4)

    @pl.when(i == 0)
    def _():
        issue_all(dcur_ref, 0)

    @pl.when(i + 1 < nsteps)
    def _():
        issue_all(dnext_ref, 1 - slot)

    for kk in range(TOP_K):
        pltpu.make_async_copy(ys_ref.at[pl.ds(0, tm * rt)], buf_ref.at[slot, kk],
                              sem.at[slot]).wait()

    rec = route_ref[...]
    acc = h2_ref[...]
    for kk in range(TOP_K):
        acc = acc + (rec[:, ROUTE_GATE + kk:ROUTE_GATE + kk + 1]
                     * _unpack_pairs_f32(_load_row_tiles(buf_ref.at[slot, kk], tm)))
    o_ref[...] = _rmsnorm(acc, gain_ref[...])


def _combine(dest, h2, route, gain, ys_t, tm=128):
    n, d = h2.shape
    nsteps = n // tm
    rt = d // 2 // LANES
    return pl.pallas_call(
        functools.partial(_combine_kernel, tm=tm, rt=rt),
        grid=(nsteps,),
        in_specs=[
            pl.BlockSpec((SUBLANES, tm), lambda i: (0, i), memory_space=pltpu.SMEM),
            pl.BlockSpec((SUBLANES, tm), lambda i: (0, jnp.minimum(i + 1, nsteps - 1)),
                         memory_space=pltpu.SMEM),
            pl.BlockSpec((tm, d), lambda i: (i, 0)),
            pl.BlockSpec((tm, LANES), lambda i: (i, 0)),
            pl.BlockSpec((1, d), lambda i: (0, 0)),
            pl.BlockSpec(memory_space=pl.ANY),
        ],
        out_specs=pl.BlockSpec((tm, d), lambda i: (i, 0)),
        out_shape=jax.ShapeDtypeStruct((n, d), F32),
        scratch_shapes=[
            pltpu.VMEM((2, TOP_K, tm * rt, LANES), U32),
            pltpu.SemaphoreType.DMA((2,)),
        ],
        compiler_params=_cparams(1),
        name="combine",
    )(dest, dest, h2, route, gain, ys_t)


def kernel(x, mem, norm_mix, w_in, dn_conv, dn_a_log, dn_dt_bias, dn_norm, rg_conv, rg_conv_b, rg_w_a, rg_b_a, rg_w_x, rg_b_x, rg_lambda, w_out, norm_cross, norm_mem, w_cq, w_ckv, w_co, norm_moe, w_router, b_router, w_gate, b_gate, w_up, b_up, w_down, b_down, norm_final):
    batch, seq, d = x.shape
    mem_len = mem.shape[1]
    n = batch * seq
    assert w_in.shape[0] == 1, "single-layer trunk"
    x2 = x.reshape(n, d)

    wi = w_in[0]
    n_gate = 2 * DN_HEADS
    w_cat = jnp.concatenate(
        [wi[:, :4 * D_DN], wi[:, 4 * D_DN + n_gate:],
         jnp.pad(wi[:, 4 * D_DN:4 * D_DN + n_gate], ((0, 0), (0, LANES - n_gate)))],
        axis=1).astype(BF16)
    wbat = wi[:, 4 * D_DN:4 * D_DN + n_gate].T.astype(BF16)
    prow = (jnp.zeros((SUBLANES, LANES), F32)
            .at[0, DN_HEADS:n_gate].set(dn_a_log[0]).at[1, DN_HEADS:n_gate].set(dn_dt_bias[0]))
    pcol = (jnp.zeros((n_gate, LANES), F32)
            .at[DN_HEADS:, 0].set(dn_a_log[0]).at[DN_HEADS:, 1].set(dn_dt_bias[0]))

    proj, gbt = _in_proj(x2, norm_mix, w_cat, wbat, prow, pcol)
    dn = _deltanet(proj, gbt, dn_conv[0], dn_norm, batch, seq)
    rg = _rglru(proj, rg_conv[0], rg_conv_b, rg_w_a[0], rg_b_a[0].reshape(1, D_RG),
                rg_w_x[0], rg_b_x[0].reshape(1, D_RG), rg_lambda, batch, seq)
    kv = _mem_kv(mem.reshape(batch * mem_len, d), norm_mem, w_ckv[0])

    w_r = jnp.pad(w_router[0], ((0, 0), (0, LANES - N_EXPERTS)))
    w_r_hi = w_r.astype(BF16)
    w_r_lo = (w_r - w_r_hi.astype(F32)).astype(BF16)
    b_r = jnp.pad(b_router, ((0, 0), (0, LANES - N_EXPERTS)))
    h2, xn, route, route_t, counts = _post_mix(
        dn, rg, x2, w_out[0], norm_cross, w_cq[0], kv, w_co[0], norm_moe, w_r_hi, w_r_lo, b_r,
        seq, mem_len)

    n_blocks = n * TOP_K // MOE_BM + N_EXPERTS
    n_rows = n_blocks * MOE_BM
    cnt = counts[:, 0].astype(I32)
    padded = (cnt + MOE_BM - 1) // MOE_BM * MOE_BM
    pad_end = jnp.cumsum(padded)
    pad_start = pad_end - padded
    n_used = (pad_end[-1:] // MOE_BM).astype(I32)
    block_e = jnp.minimum(
        jnp.sum(pad_end[None, :] <= (jnp.arange(n_blocks, dtype=I32) * MOE_BM)[:, None], axis=1),
        N_EXPERTS - 1).astype(I32)
    meta = jnp.concatenate([pad_start + cnt, padded - cnt, n_used]).astype(I32)

    dest = _dest(pad_start.astype(I32), route_t)
    xs = _dispatch(meta, dest, xn, n, n_rows)
    has = cnt > 0
    eid = jnp.where(has, jnp.arange(N_EXPERTS, dtype=I32), N_EXPERTS)
    after = lax.cummin(eid, axis=0, reverse=True)
    next_e = jnp.concatenate([after[1:], jnp.full((1,), N_EXPERTS, I32)]).astype(I32)
    parity = ((jnp.cumsum(has.astype(I32)) - 1) & 1).astype(I32)
    ys = _experts(block_e, n_used, next_e, parity, xs, w_gate[0], b_gate[0][:, None, :], w_up[0],
                  b_up[0][:, None, :], w_down[0], b_down[0][:, None, :])
    out = _combine(dest, h2, route, norm_final.reshape(1, d), ys)
    return out.reshape(batch, seq, d)
```

```python
import functools

import jax
import jax.numpy as jnp
from jax import lax
from jax.experimental import pallas as pl
from jax.experimental.pallas import tpu as pltpu

F32 = jnp.float32
BF16 = jnp.bfloat16
I32 = jnp.int32
U32 = jnp.uint32

EPS = 1e-6
LANES = 128
SUBLANES = 8
VMEM_LIMIT = 48 * 1024 * 1024

DN_HEADS = 4
DN_HEAD_DIM = 128
D_DN = DN_HEADS * DN_HEAD_DIM
D_RG = 512
RG_BLOCKS = 4
RG_C = 8.0
CONV_WIDTH = 4
XA_HEADS = 4
N_EXPERTS = 32
TOP_K = 4
SWIGLU_LIMIT = 7.0
SWIGLU_ALPHA = 1.702

DN_CHUNK = 128
INV_BASE = 16
MOE_BM = 256

COL_RX = 4 * D_DN
COL_RY = COL_RX + D_RG
COL_GB = COL_RY + D_RG
PROJ_W = COL_GB + LANES


def _cparams(n_axes=1):
    return pltpu.CompilerParams(
        dimension_semantics=("arbitrary",) * n_axes, vmem_limit_bytes=VMEM_LIMIT)


def _mm(a, b):
    return jnp.dot(a.astype(BF16), b.astype(BF16), preferred_element_type=F32)


def _mm_nt(a, b):
    return lax.dot_general(a.astype(BF16), b.astype(BF16), (((1,), (1,)), ((), ())),
                           preferred_element_type=F32)


def _rmsnorm(x, g):
    return x * lax.rsqrt(jnp.mean(x * x, axis=-1, keepdims=True) + EPS) * g


def _sigmoid(x):
    return 0.5 * jnp.tanh(0.5 * x) + 0.5


def _mm_split(a, b_hi, b_lo):
    a_hi = a.astype(BF16)
    a_lo = (a - a_hi.astype(F32)).astype(BF16)
    return (jnp.dot(a_hi, b_hi, preferred_element_type=F32)
            + jnp.dot(a_hi, b_lo, preferred_element_type=F32)
            + jnp.dot(a_lo, b_hi, preferred_element_type=F32))


def _softplus(x):
    return jnp.maximum(x, 0.0) + jnp.log1p(jnp.exp(-jnp.abs(x)))


def _load_row_tiles(ref, rows):
    rt = ref.shape[0] // rows
    return jnp.concatenate([ref[pl.ds(s, rows, stride=rt), :] for s in range(rt)], axis=1)


def _store_row_tiles(ref, val):
    rows, w = val.shape
    rt = w // LANES
    for s in range(rt):
        ref[pl.ds(s, rows, stride=rt), :] = val[:, s * LANES:(s + 1) * LANES]


def _pack_bf16_pairs(x):
    half = x.shape[1] // 2
    hi = pltpu.bitcast(x[:, :half].astype(BF16).astype(F32), U32)
    lo = pltpu.bitcast(x[:, half:].astype(BF16).astype(F32), U32)
    return hi | (lo >> 16)


def _unpack_pairs_f32(p):
    hi = pltpu.bitcast(p & jnp.uint32(0xFFFF0000), F32)
    lo = pltpu.bitcast(p << 16, F32)
    return jnp.concatenate([hi, lo], axis=1)


def _load_segmented(ref):
    seg = ref.shape[0] // SUBLANES
    return jnp.concatenate([ref[pl.ds(j, SUBLANES, stride=seg), :] for j in range(seg)], axis=0)


def _causal_conv_segmented(xp, halo_ref, cs, w):
    tb = xp.shape[0]
    ng = w.shape[0] - 1
    prev = halo_ref[:, cs]
    last = xp[tb - ng * SUBLANES:, :]
    halo_ref[:, cs] = last
    sub = lax.broadcasted_iota(I32, (SUBLANES, xp.shape[1]), 0)
    groups = []
    for g in range(ng):
        rows = slice(g * SUBLANES, (g + 1) * SUBLANES)
        groups.append(jnp.where(sub == 0, pltpu.roll(prev[rows], 1, 0),
                                pltpu.roll(last[rows], 1, 0)))
    ext = jnp.concatenate(groups + [xp], axis=0)
    y = w[ng:ng + 1] * xp
    for k in range(1, ng + 1):
        y = y + w[ng - k:ng - k + 1] * ext[(ng - k) * SUBLANES:(ng - k) * SUBLANES + tb]
    return y


def _in_proj_kernel(x_ref, g_ref, w_ref, prow_ref, pcol_ref, proj_ref, gbt_ref):
    u = _rmsnorm(x_ref[...], g_ref[...]).astype(BF16)
    for c0 in range(0, COL_GB, 512):
        proj_ref[:, c0:c0 + 512] = jnp.dot(u, w_ref[:, c0:c0 + 512], preferred_element_type=F32)
    ba = jnp.dot(u, w_ref[:, COL_GB:PROJ_W], preferred_element_type=F32)
    lane = lax.broadcasted_iota(I32, ba.shape, 1)
    g = -jnp.exp(prow_ref[0:1, :]) * _softplus(ba + prow_ref[1:2, :])
    proj_ref[:, COL_GB:PROJ_W] = jnp.where(lane < DN_HEADS, _sigmoid(ba), g)
    bat = lax.dot_general(w_ref[:, COL_GB:PROJ_W], u, (((0,), (1,)), ((), ())),
                          preferred_element_type=F32)[0:2 * DN_HEADS, :]
    row = lax.broadcasted_iota(I32, bat.shape, 0)
    gt = -jnp.exp(pcol_ref[:, 0:1]) * _softplus(bat + pcol_ref[:, 1:2])
    gbt_ref[...] = jnp.where(row < DN_HEADS, _sigmoid(bat), gt)


def _in_proj(x2, gain, w_cat, prow, pcol, tm=512):
    n, d = x2.shape
    return pl.pallas_call(
        _in_proj_kernel,
        grid=(n // tm,),
        in_specs=[
            pl.BlockSpec((tm, d), lambda i: (i, 0)),
            pl.BlockSpec((1, d), lambda i: (0, 0)),
            pl.BlockSpec((d, PROJ_W), lambda i: (0, 0)),
            pl.BlockSpec((SUBLANES, LANES), lambda i: (0, 0)),
            pl.BlockSpec((2 * DN_HEADS, LANES), lambda i: (0, 0)),
        ],
        out_specs=[
            pl.BlockSpec((tm, PROJ_W), lambda i: (i, 0)),
            pl.BlockSpec((2 * DN_HEADS, tm), lambda i: (0, i)),
        ],
        out_shape=[
            jax.ShapeDtypeStruct((n, PROJ_W), F32),
            jax.ShapeDtypeStruct((2 * DN_HEADS, n), F32),
        ],
        compiler_params=_cparams(1),
        name="in_proj",
    )(x2, gain, w_cat, prow, pcol)


def _deltanet_kernel(*refs, tb):
    n_grp = 3 * DN_HEADS
    qkv_refs = refs[:n_grp]
    (z_ref, gb_ref, gbt_ref, conv_ref, norm_ref, o_ref,
     s_ref, halo_ref, act_ref, gcc_ref, gcr_ref,
     a_ref, p_ref, d_ref, qk_ref, rhs_ref, u_ref, wq_ref, kdt_ref) = refs[n_grp:]
    t = pl.program_id(1)
    c = DN_CHUNK
    dh = DN_HEAD_DIM

    @pl.when(t == 0)
    def _():
        s_ref[...] = jnp.zeros_like(s_ref)
        halo_ref[...] = jnp.zeros_like(halo_ref)

    sub = lax.broadcasted_iota(I32, (SUBLANES, dh), 0)
    for grp in range(n_grp):
        cs = slice(grp * dh, (grp + 1) * dh)
        x = qkv_refs[grp][...]
        prev = halo_ref[:, cs]
        halo_ref[:, cs] = x[tb - SUBLANES:, :]
        y = conv_ref[CONV_WIDTH - 1:CONV_WIDTH, cs] * x
        for k in range(1, CONV_WIDTH):
            xs = pltpu.roll(x, k, 0)
            head = jnp.where(sub < k, pltpu.roll(prev, k, 0), xs[:SUBLANES])
            xs = jnp.concatenate([head, xs[SUBLANES:]], axis=0)
            y = y + conv_ref[CONV_WIDTH - 1 - k:CONV_WIDTH - k, cs] * xs
        y = y * _sigmoid(y)
        if grp < 2 * DN_HEADS:
            y = y * lax.rsqrt(jnp.sum(y * y, axis=-1, keepdims=True) + EPS)
        if grp < DN_HEADS:
            y = y * (dh ** -0.5)
        act_ref[grp] = y

    gcol = gb_ref[...]
    rpos = lax.broadcasted_iota(I32, gcol.shape, 0) & (c - 1)
    d = 1
    while d < c:
        gcol = gcol + jnp.where(rpos >= d, pltpu.roll(gcol, d, 0), 0.0)
        d *= 2
    gcc_ref[...] = gcol
    grow = gbt_ref[...]
    lpos = lax.broadcasted_iota(I32, grow.shape, 1) & (c - 1)
    d = 1
    while d < c:
        grow = grow + jnp.where(lpos >= d, pltpu.roll(grow, d, 1), 0.0)
        d *= 2
    gcr_ref[...] = grow

    row = lax.broadcasted_iota(I32, (c, c), 0)
    col = lax.broadcasted_iota(I32, (c, c), 1)
    causal = row >= col
    strict = row > col
    eye = jnp.where(row == col, 1.0, 0.0)
    gain = norm_ref[...]
    n_chunks = tb // c
    probs = [(ci, h) for ci in range(n_chunks) for h in range(DN_HEADS)]

    decay_last = []
    for p, (ci, h) in enumerate(probs):
        rows = slice(ci * c, (ci + 1) * c)
        q = act_ref[h, rows, :]
        k = act_ref[DN_HEADS + h, rows, :]
        v = act_ref[2 * DN_HEADS + h, rows, :]
        beta = gb_ref[rows, h:h + 1]
        gc = gcc_ref[rows, DN_HEADS + h:DN_HEADS + h + 1]
        gr = gcr_ref[DN_HEADS + h:DN_HEADS + h + 1, rows]
        g_last = gc[c - 1:c, :]
        decay = jnp.where(causal, jnp.exp(jnp.where(causal, gc - gr, 0.0)), 0.0)
        kb = k * beta
        both = _mm_nt(jnp.concatenate([kb, q], axis=0), k)
        a_ref[p] = jnp.where(strict, both[:c] * decay, 0.0)
        qk_ref[p] = (both[c:] * decay).astype(BF16)
        egc = jnp.exp(gc)
        rhs_ref[p] = jnp.concatenate([v * beta, kb * egc], axis=1).astype(BF16)
        wq_ref[p, c:, :] = (q * egc).astype(BF16)
        kdt_ref[p] = (k * jnp.exp(g_last - gc)).T.astype(BF16)
        decay_last.append(jnp.exp(g_last))

    shift = INV_BASE.bit_length() - 1
    blk = (row >> shift) == (col >> shift)
    for p in range(len(probs)):
        diag = jnp.where(blk, a_ref[p], 0.0)
        p_ref[p] = eye - diag
        d_ref[p] = _mm(diag, diag).astype(BF16)
    for it in range(shift - 1):
        for p in range(len(probs)):
            pw = d_ref[p]
            inv = p_ref[p]
            p_ref[p] = inv + _mm(inv, pw)
            if it < shift - 2:
                d_ref[p] = _mm(pw, pw).astype(BF16)
    s = INV_BASE
    while s < c:
        sh = s.bit_length() - 1
        off = ((row >> (sh + 1)) == (col >> (sh + 1))) & ((row >> sh) != (col >> sh))
        for p in range(len(probs)):
            d_ref[p] = _mm(p_ref[p], jnp.where(off, a_ref[p], 0.0)).astype(BF16)
        for p in range(len(probs)):
            inv = p_ref[p]
            p_ref[p] = inv - _mm(d_ref[p], inv)
        s *= 2
    for p in range(len(probs)):
        uw = _mm(p_ref[p], rhs_ref[p])
        u_ref[p] = uw[:, :dh]
        wq_ref[p, :c, :] = uw[:, dh:].astype(BF16)

    for ci in range(n_chunks):
        rows = slice(ci * c, (ci + 1) * c)
        ps = [ci * DN_HEADS + h for h in range(DN_HEADS)]
        s_old = [s_ref[h] for h in range(DN_HEADS)]
        ws = [_mm(wq_ref[p], s_old[h]) for h, p in enumerate(ps)]
        v_new = [u_ref[p] - ws[h][:c] for h, p in enumerate(ps)]
        outs = [ws[h][c:] + _mm(qk_ref[p], v_new[h]) for h, p in enumerate(ps)]
        for h, p in enumerate(ps):
            s_ref[h] = s_old[h] * decay_last[p] + _mm(kdt_ref[p], v_new[h])
        for h in range(DN_HEADS):
            o = _rmsnorm(outs[h], gain)
            zz = z_ref[rows, h * dh:(h + 1) * dh]
            o_ref[rows, h * dh:(h + 1) * dh] = (o * (zz * _sigmoid(zz))).astype(o_ref.dtype)


def _deltanet(proj, gbt, dn_conv, dn_norm, batch, seq, tb=512):
    n = proj.shape[0]
    nt = seq // tb
    c = DN_CHUNK
    n_prob = (tb // c) * DN_HEADS
    n_grp = 3 * DN_HEADS
    grp = lambda j: pl.BlockSpec((tb, DN_HEAD_DIM), lambda b, t, j=j: (b * nt + t, j))
    return pl.pallas_call(
        functools.partial(_deltanet_kernel, tb=tb),
        grid=(batch, nt),
        in_specs=[grp(j) for j in range(n_grp)] + [
            pl.BlockSpec((tb, D_DN), lambda b, t: (b * nt + t, 3)),
            pl.BlockSpec((tb, LANES), lambda b, t: (b * nt + t, COL_GB // LANES)),
            pl.BlockSpec((2 * DN_HEADS, tb), lambda b, t: (0, b * nt + t)),
            pl.BlockSpec((CONV_WIDTH, 3 * D_DN), lambda b, t: (0, 0)),
            pl.BlockSpec((1, DN_HEAD_DIM), lambda b, t: (0, 0)),
        ],
        out_specs=pl.BlockSpec((tb, D_DN), lambda b, t: (b * nt + t, 0)),
        out_shape=jax.ShapeDtypeStruct((n, D_DN), BF16),
        scratch_shapes=[
            pltpu.VMEM((DN_HEADS, DN_HEAD_DIM, DN_HEAD_DIM), F32),
            pltpu.VMEM((SUBLANES, 3 * D_DN), F32),
            pltpu.VMEM((n_grp, tb, DN_HEAD_DIM), F32),
            pltpu.VMEM((tb, LANES), F32),
            pltpu.VMEM((2 * DN_HEADS, tb), F32),
            pltpu.VMEM((n_prob, c, c), F32),
            pltpu.VMEM((n_prob, c, c), F32),
            pltpu.VMEM((n_prob, c, c), BF16),
            pltpu.VMEM((n_prob, c, c), BF16),
            pltpu.VMEM((n_prob, c, 2 * DN_HEAD_DIM), BF16),
            pltpu.VMEM((n_prob, c, DN_HEAD_DIM), F32),
            pltpu.VMEM((n_prob, 2 * c, DN_HEAD_DIM), BF16),
            pltpu.VMEM((n_prob, DN_HEAD_DIM, c), BF16),
        ],
        compiler_params=_cparams(2),
        name="deltanet",
    )(*([proj] * (n_grp + 2)), gbt, dn_conv, dn_norm)


def _gelu_tanh(x):
    return 0.5 * x * (1.0 + jnp.tanh(0.7978845608028654 * (x + 0.044715 * (x * x * x))))


def _rglru_kernel(*refs, tb):
    rx_refs = refs[:RG_BLOCKS]
    (ry_ref, conv_ref, convb_ref, wa_ref, ba_ref, wx_ref, bx_ref, lam_ref,
     o_ref, hc_ref, halo_ref, a_ref, b_ref, h_ref) = refs[RG_BLOCKS:]
    t = pl.program_id(1)
    bw = D_RG // RG_BLOCKS
    seg = tb // SUBLANES

    @pl.when(t == 0)
    def _():
        hc_ref[...] = jnp.zeros_like(hc_ref)
        halo_ref[...] = jnp.zeros_like(halo_ref)

    log_sig = -_softplus(-lam_ref[...])
    rowi = lax.broadcasted_iota(I32, (tb, bw), 0)
    seq_start_row = jnp.where(t == 0, 0, -1)
    for nb in range(RG_BLOCKS):
        cs = slice(nb * bw, (nb + 1) * bw)
        xb = (_causal_conv_segmented(_load_segmented(rx_refs[nb]), halo_ref, cs,
                                     conv_ref[:, cs]) + convb_ref[:, cs])
        r = _sigmoid(_mm(xb, wa_ref[nb]) + ba_ref[:, cs])
        gi = _sigmoid(_mm(xb, wx_ref[nb]) + bx_ref[:, cs])
        log_a = RG_C * r * log_sig[:, cs]
        a = jnp.exp(log_a)
        y = jnp.tanh(-log_a) * (1.0 + a * a)
        mult = jnp.where(y > 0.0, y * lax.rsqrt(y), 0.0)
        mult = jnp.where(rowi == seq_start_row, 1.0, mult)
        a_ref[:, cs] = a
        b_ref[:, cs] = mult * (gi * xb)

    def scan(j, carry):
        ac, bc = carry
        rows = pl.ds(pl.multiple_of(j * SUBLANES, SUBLANES), SUBLANES)
        aj = a_ref[rows, :]
        bc = aj * bc + b_ref[rows, :]
        ac = aj * ac
        a_ref[rows, :] = ac
        b_ref[rows, :] = bc
        return ac, bc

    ac, bc = lax.fori_loop(1, seg, scan, (a_ref[0:SUBLANES, :], b_ref[0:SUBLANES, :]), unroll=7)

    h = hc_ref[...]
    h_in = []
    for s in range(SUBLANES):
        h_in.append(h)
        h = ac[s:s + 1, :] * h + bc[s:s + 1, :]
    hc_ref[...] = h
    h_in = jnp.concatenate(h_in, axis=0)

    for j in range(seg):
        rows = slice(j * SUBLANES, (j + 1) * SUBLANES)
        hj = a_ref[rows, :] * h_in + b_ref[rows, :]
        for nb in range(RG_BLOCKS):
            h_ref[nb, rows, :] = hj[:, nb * bw:(nb + 1) * bw]
    g = seg // SUBLANES
    for nb in range(RG_BLOCKS):
        cs = slice(nb * bw, (nb + 1) * bw)
        h = jnp.concatenate(
            [h_ref[nb, pl.ds((i % g) * SUBLANES * SUBLANES + i // g, SUBLANES, stride=SUBLANES), :]
             for i in range(seg)], axis=0)
        o_ref[:, cs] = (h * _gelu_tanh(ry_ref[:, cs])).astype(o_ref.dtype)


def _rglru(proj, rg_conv, rg_conv_b, w_a, b_a, w_x, b_x, lam, batch, seq, tb=512):
    n = proj.shape[0]
    nt = seq // tb
    bw = D_RG // RG_BLOCKS
    full = lambda shape: pl.BlockSpec(shape, lambda b, t: (0,) * len(shape))
    return pl.pallas_call(
        functools.partial(_rglru_kernel, tb=tb),
        grid=(batch, nt),
        in_specs=[pl.BlockSpec((tb, bw), lambda b, t, j=j: (b * nt + t, COL_RX // bw + j))
                  for j in range(RG_BLOCKS)] + [
            pl.BlockSpec((tb, D_RG), lambda b, t: (b * nt + t, COL_RY // D_RG)),
            full((CONV_WIDTH, D_RG)), full((1, D_RG)),
            full((RG_BLOCKS, bw, bw)), full((1, D_RG)),
            full((RG_BLOCKS, bw, bw)), full((1, D_RG)),
            full((1, D_RG)),
        ],
        out_specs=pl.BlockSpec((tb, D_RG), lambda b, t: (b * nt + t, 0)),
        out_shape=jax.ShapeDtypeStruct((n, D_RG), BF16),
        scratch_shapes=[
            pltpu.VMEM((1, D_RG), F32),
            pltpu.VMEM(((CONV_WIDTH - 1) * SUBLANES, D_RG), F32),
            pltpu.VMEM((tb, D_RG), F32),
            pltpu.VMEM((tb, D_RG), F32),
            pltpu.VMEM((RG_BLOCKS, tb, bw), F32),
        ],
        compiler_params=_cparams(2),
        name="rglru",
    )(*([proj] * (RG_BLOCKS + 1)), rg_conv, rg_conv_b, w_a, b_a, w_x, b_x, lam)


def _mem_kv_kernel(m_ref, g_ref, w_ref, o_ref):
    mn = _rmsnorm(m_ref[...], g_ref[...]).astype(BF16)
    for c0 in range(0, o_ref.shape[1], 512):
        o_ref[:, c0:c0 + 512] = jnp.dot(mn, w_ref[:, c0:c0 + 512].astype(BF16),
                                        preferred_element_type=F32).astype(o_ref.dtype)


def _mem_kv(mem2, gain, w_ckv):
    n, d = mem2.shape
    return pl.pallas_call(
        _mem_kv_kernel,
        grid=(1,),
        in_specs=[
            pl.BlockSpec((n, d), lambda i: (0, 0)),
            pl.BlockSpec((1, d), lambda i: (0, 0)),
            pl.BlockSpec((d, 2 * d), lambda i: (0, 0)),
        ],
        out_specs=pl.BlockSpec((n, 2 * d), lambda i: (0, 0)),
        out_shape=jax.ShapeDtypeStruct((n, 2 * d), BF16),
        compiler_params=_cparams(1),
        name="mem_kv",
    )(mem2, gain, w_ckv)


ROUTE_E = 0
ROUTE_RANK = TOP_K
ROUTE_GATE = 2 * TOP_K
ROUTE_ROWS = 16


def _post_mix_kernel(dn_ref, rg_ref, x_ref, wo_f32, gx_ref, wq_f32, kv_ref, wco_f32, gm_ref,
                     wrh_ref, wrl_ref, br_ref, h2_ref, xn_ref, route_ref, routet_ref, cnt_ref,
                     carry_ref, wo_ref, wq_ref, wco_ref):
    i = pl.program_id(0)
    tm, d = x_ref.shape
    hd = d // XA_HEADS

    @pl.when(i == 0)
    def _():
        carry_ref[...] = jnp.zeros_like(carry_ref)
        wo_ref[...] = wo_f32[...].astype(BF16)
        wq_ref[...] = wq_f32[...].astype(BF16)
        wco_ref[...] = wco_f32[...].astype(BF16)

    h1 = (x_ref[...] + jnp.dot(dn_ref[...], wo_ref[0:D_DN, :], preferred_element_type=F32)
          + jnp.dot(rg_ref[...], wo_ref[D_DN:, :], preferred_element_type=F32))

    hn = _rmsnorm(h1, gx_ref[...]).astype(BF16)
    q = jnp.dot(hn, wq_ref[...], preferred_element_type=F32)
    heads = []
    for hh in range(XA_HEADS):
        cs = slice(hh * hd, (hh + 1) * hd)
        s = _mm_nt(q[:, cs], kv_ref[:, cs]) * (hd ** -0.5)
        p = jnp.exp(s - jnp.max(s, axis=-1, keepdims=True))
        p = p / jnp.sum(p, axis=-1, keepdims=True)
        heads.append(_mm(p, kv_ref[:, d + hh * hd:d + (hh + 1) * hd]).astype(BF16))
    o = jnp.concatenate(heads, axis=1)
    h2 = h1 + jnp.dot(o, wco_ref[...], preferred_element_type=F32)
    h2_ref[...] = h2

    xn = _rmsnorm(h2, gm_ref[...])
    _store_row_tiles(xn_ref, _pack_bf16_pairs(xn))
    logits = _mm_split(xn, wrh_ref[...], wrl_ref[...]) + br_ref[...]
    lg = logits.T[0:N_EXPERTS, :]
    eidx = lax.broadcasted_iota(I32, lg.shape, 0).astype(F32)
    neg = jnp.float32(-jnp.inf)
    vals, idxs, hots = [], [], []
    for _ in range(TOP_K):
        m = jnp.max(lg, axis=0, keepdims=True)
        idx = jnp.min(jnp.where(lg == m, eidx, float(N_EXPERTS)), axis=0, keepdims=True)
        hot = eidx == idx
        lg = jnp.where(hot, neg, lg)
        vals.append(m)
        idxs.append(idx)
        hots.append(hot)
    es = [jnp.exp(v - vals[0]) for v in vals]
    den = es[0] + es[1] + es[2] + es[3]
    gates = [e / den for e in es]

    chosen = jnp.zeros(lg.shape, F32)
    for hot in hots:
        chosen = chosen + jnp.where(hot, 1.0, 0.0)
    r2 = lax.broadcasted_iota(I32, (tm, tm), 0)
    c2 = lax.broadcasted_iota(I32, (tm, tm), 1)
    before = _mm(chosen, jnp.where(r2 < c2, 1.0, 0.0)) + carry_ref[:, 0:1]
    ranks = [jnp.sum(jnp.where(hot, before, 0.0), axis=0, keepdims=True) for hot in hots]
    carry_ref[...] = carry_ref[...] + jnp.sum(chosen, axis=1, keepdims=True)
    cnt_ref[...] = carry_ref[...]

    row = lax.broadcasted_iota(I32, (ROUTE_ROWS, tm), 0)
    rect = jnp.zeros((ROUTE_ROWS, tm), F32)
    for kk in range(TOP_K):
        rect = jnp.where(row == ROUTE_E + kk, idxs[kk], rect)
        rect = jnp.where(row == ROUTE_RANK + kk, ranks[kk], rect)
        rect = jnp.where(row == ROUTE_GATE + kk, gates[kk], rect)
    routet_ref[...] = rect
    route_ref[...] = jnp.concatenate(
        [rect, jnp.zeros((LANES - ROUTE_ROWS, tm), F32)], axis=0).T


def _post_mix(dn, rg, x2, w_out, g_cross, w_cq, kv, w_co, g_moe, w_r_hi, w_r_lo, b_r, seq, mem_len,
              tm=512):
    n, d = x2.shape
    per_b = seq // tm
    rt_x = d // 2 // LANES
    full = lambda shape: pl.BlockSpec(shape, lambda i: (0,) * len(shape))
    once = lambda shape: pl.BlockSpec(shape, lambda i: (0,) * len(shape),
                                      pipeline_mode=pl.Buffered(1))
    return pl.pallas_call(
        _post_mix_kernel,
        grid=(n // tm,),
        in_specs=[
            pl.BlockSpec((tm, D_DN), lambda i: (i, 0)),
            pl.BlockSpec((tm, D_RG), lambda i: (i, 0)),
            pl.BlockSpec((tm, d), lambda i: (i, 0)),
            once((d, d)), full((1, d)), once((d, d)),
            pl.BlockSpec((mem_len, 2 * d), lambda i: (i // per_b, 0)),
            once((d, d)), full((1, d)),
            full((d, LANES)), full((d, LANES)), full((1, LANES)),
        ],
        out_specs=[
            pl.BlockSpec((tm, d), lambda i: (i, 0)),
            pl.BlockSpec((tm * rt_x, LANES), lambda i: (i, 0)),
            pl.BlockSpec((tm, LANES), lambda i: (i, 0)),
            pl.BlockSpec((ROUTE_ROWS, tm), lambda i: (0, i)),
            pl.BlockSpec((N_EXPERTS, LANES), lambda i: (0, 0)),
        ],
        out_shape=[
            jax.ShapeDtypeStruct((n, d), F32),
            jax.ShapeDtypeStruct((n * rt_x, LANES), U32),
            jax.ShapeDtypeStruct((n, LANES), F32),
            jax.ShapeDtypeStruct((ROUTE_ROWS, n), F32),
            jax.ShapeDtypeStruct((N_EXPERTS, LANES), F32),
        ],
        scratch_shapes=[pltpu.VMEM((N_EXPERTS, LANES), F32),
                        pltpu.VMEM((d, d), BF16), pltpu.VMEM((d, d), BF16), pltpu.VMEM((d, d), BF16)],
        compiler_params=_cparams(1),
        name="post_mix",
    )(dn, rg, x2, w_out, g_cross, w_cq, kv, w_co, g_moe, w_r_hi, w_r_lo, b_r)


def _dest_kernel(start_ref, routet_ref, dest_ref):
    e = routet_ref[ROUTE_E:ROUTE_E + SUBLANES, :]
    rank = routet_ref[ROUTE_RANK:ROUTE_RANK + SUBLANES, :]
    base = jnp.zeros(e.shape, F32)
    for j in range(N_EXPERTS):
        base = jnp.where(e == float(j), start_ref[j].astype(F32), base)
    row = lax.broadcasted_iota(I32, e.shape, 0)
    dest_ref[...] = jnp.where(row < TOP_K, (base + rank).astype(I32), 0)


def _dest(pad_start, route_t, tm=2048):
    n = route_t.shape[1]
    grid_spec = pltpu.PrefetchScalarGridSpec(
        num_scalar_prefetch=1,
        grid=(n // tm,),
        in_specs=[pl.BlockSpec((ROUTE_ROWS, tm), lambda i, st: (0, i))],
        out_specs=pl.BlockSpec((SUBLANES, tm), lambda i, st: (0, i)),
    )
    return pl.pallas_call(
        _dest_kernel,
        grid_spec=grid_spec,
        out_shape=jax.ShapeDtypeStruct((SUBLANES, n), I32),
        compiler_params=_cparams(1),
        name="dest",
    )(pad_start, route_t)


def _row_tile(ref, r, rt):
    return ref.at[pl.ds(pl.multiple_of(r * rt, rt), rt)]


def _dispatch_kernel(meta_ref, dest_ref, xn_ref, init_hbm, xs_ref, rowmap_ref, zero_ref, ring_ref,
                     sem, zsem, *, tm, rt, n_blocks):
    i = pl.program_id(0)
    bm = MOE_BM * rt
    ring = ring_ref.at[i & 1]
    ring[...] = xn_ref[...]

    @pl.when(i == 0)
    def _():
        cp = pltpu.make_async_copy(init_hbm, rowmap_ref, zsem)
        cp.start()
        cp.wait()

    def per_token(tk, carry):
        src = _row_tile(ring, tk, rt)
        for kk in range(TOP_K):
            r = dest_ref[kk, tk]
            rowmap_ref[r] = (i * tm + tk) * TOP_K + kk
            pltpu.make_async_copy(src, _row_tile(xs_ref, r, rt),
                                  sem.at[i & 1]).start(priority=kk % 2)
        return carry

    lax.fori_loop(0, tm, per_token, 0, unroll=4)

    @pl.when(i == 0)
    def _():
        zero_ref[...] = jnp.zeros_like(zero_ref)

        def pad_copy(r):
            return pltpu.make_async_copy(zero_ref.at[pl.ds(0, rt)], _row_tile(xs_ref, r, rt), zsem)

        def per_expert(e, total):
            r0 = meta_ref[e]
            cnt = meta_ref[N_EXPERTS + e]

            def one(r, carry):
                pad_copy(r0 + r).start()
                return carry

            lax.fori_loop(0, cnt, one, 0)
            return total + cnt

        total = lax.fori_loop(0, N_EXPERTS, per_expert, 0)

        def drain(r, carry):
            pad_copy(0).wait()
            return carry

        lax.fori_loop(0, total, drain, 0)

        def tail_copy(b):
            return pltpu.make_async_copy(
                zero_ref, xs_ref.at[pl.ds(pl.multiple_of(b * bm, bm), bm)], zsem)

        n_used = meta_ref[2 * N_EXPERTS]

        def tail(b, carry):
            tail_copy(b).start()
            return carry

        lax.fori_loop(n_used, n_blocks, tail, 0)

        def tail_drain(b, carry):
            tail_copy(0).wait()
            return carry

        lax.fori_loop(n_used, n_blocks, tail_drain, 0)

    def wait_step(parity):
        for _ in range(TOP_K):
            pltpu.make_async_copy(ring_ref.at[parity], xs_ref.at[pl.ds(0, tm * rt)],
                                  sem.at[parity]).wait()

    @pl.when(i > 0)
    def _():
        wait_step(1 - (i & 1))

    @pl.when(i == pl.num_programs(0) - 1)
    def _():
        wait_step(i & 1)


def _dispatch(meta, dest, xn_t, n_tok, n_rows, tm=256):
    rt = xn_t.shape[0] // n_tok
    n_blocks = n_rows // MOE_BM
    grid_spec = pltpu.PrefetchScalarGridSpec(
        num_scalar_prefetch=1,
        grid=(n_tok // tm,),
        in_specs=[
            pl.BlockSpec((SUBLANES, tm), lambda i, meta: (0, i), memory_space=pltpu.SMEM),
            pl.BlockSpec((tm * rt, LANES), lambda i, meta: (i, 0)),
            pl.BlockSpec(memory_space=pl.ANY),
        ],
        out_specs=[pl.BlockSpec(memory_space=pl.ANY), pl.BlockSpec(memory_space=pltpu.SMEM)],
        scratch_shapes=[
            pltpu.VMEM((MOE_BM * rt, LANES), xn_t.dtype),
            pltpu.VMEM((2, tm * rt, LANES), xn_t.dtype),
            pltpu.SemaphoreType.DMA((2,)),
            pltpu.SemaphoreType.DMA,
        ],
    )
    spare = n_tok * TOP_K + jnp.arange(n_rows, dtype=I32) % MOE_BM
    return pl.pallas_call(
        functools.partial(_dispatch_kernel, tm=tm, rt=rt, n_blocks=n_blocks),
        grid_spec=grid_spec,
        out_shape=[jax.ShapeDtypeStruct((n_rows * rt, LANES), xn_t.dtype),
                   jax.ShapeDtypeStruct((n_rows,), I32)],
        compiler_params=_cparams(1),
        name="dispatch",
    )(meta, dest, xn_t, spare)


def _experts_kernel(be_ref, nu_ref, nxt_ref, par_ref, x_ref, rm_ref, bg_ref, bu_ref, bd_ref,
                    wg_hbm, wu_hbm, wd_hbm, ys_hbm, wf_ref, wb_ref, stage_ref, sem, ssem,
                    *, n_assign, rt):
    i = pl.program_id(0)
    used = i < nu_ref[0]
    e = be_ref[i]
    changed = (i == 0) | (e != be_ref[jnp.maximum(i - 1, 0)])

    def weight_copies(expert, slot):
        return [pltpu.make_async_copy(w.at[expert], wf_ref.at[slot, j], sem.at[slot])
                for j, w in enumerate((wg_hbm, wu_hbm, wd_hbm))]

    @pl.when(i == 0)
    def _():
        for cp in weight_copies(e, par_ref[e]):
            cp.start()

    @pl.when(used & changed)
    def _():
        slot = par_ref[e]
        for cp in weight_copies(e, slot):
            cp.wait()
        nxt = nxt_ref[e]

        @pl.when(nxt < N_EXPERTS)
        def _():
            for cp in weight_copies(nxt, 1 - slot):
                cp.start()

        for j in range(3):
            wb_ref[j] = wf_ref[slot, j].astype(BF16)

    prev_used = (i > 0) & (i - 1 < nu_ref[0])
    stage_prev = stage_ref.at[(i + 1) & 1]
    stage_cur = stage_ref.at[i & 1]

    def scatter_prev():
        for r in range(MOE_BM):
            pltpu.make_async_copy(_row_tile(stage_prev, r, rt),
                                  _row_tile(ys_hbm, rm_ref[0, 0, r], rt),
                                  ssem).start(priority=r % 2)

    def wait_scatter():
        pltpu.make_async_copy(stage_prev, ys_hbm.at[pl.ds(0, MOE_BM * rt)], ssem).wait()

    def compute():
        x = _unpack_pairs_f32(_load_row_tiles(x_ref, MOE_BM)).astype(BF16)
        gt = jnp.minimum(jnp.dot(x, wb_ref[0], preferred_element_type=F32) + bg_ref[...],
                         SWIGLU_LIMIT)
        up = jnp.clip(jnp.dot(x, wb_ref[1], preferred_element_type=F32) + bu_ref[...],
                      -SWIGLU_LIMIT, SWIGLU_LIMIT)
        hid = (up + 1.0) * (gt * _sigmoid(SWIGLU_ALPHA * gt))
        y = jnp.dot(hid.astype(BF16), wb_ref[2], preferred_element_type=F32) + bd_ref[...]
        _store_row_tiles(stage_cur, _pack_bf16_pairs(y))

    @pl.when(i == 0)
    def _():
        stage_ref[1] = jnp.zeros(stage_ref.shape[1:], stage_ref.dtype)
        cp = pltpu.make_async_copy(stage_ref.at[1],
                                   ys_hbm.at[pl.ds(n_assign * rt, MOE_BM * rt)], ssem)
        cp.start()
        cp.wait()
        compute()

    @pl.when(used & prev_used)
    def _():
        scatter_prev()
        compute()
        wait_scatter()

    @pl.when(prev_used & jnp.logical_not(used))
    def _():
        scatter_prev()
        wait_scatter()


def _experts(block_e, n_used, next_e, parity, xs_t, rowmap, w_gate, b_gate, w_up, b_up, w_down,
             b_down, n_assign):
    d, d_ff = w_gate.shape[1:]
    assert d == d_ff, "weight staging buffers assume square expert matrices"
    rt = d // 2 // LANES
    n_blocks = xs_t.shape[0] // (MOE_BM * rt)
    bspec = lambda m: pl.BlockSpec((None, 1, m), lambda i, be, nu, nx, pa: (be[i], 0, 0))
    hbm = pl.BlockSpec(memory_space=pl.ANY)
    grid_spec = pltpu.PrefetchScalarGridSpec(
        num_scalar_prefetch=4,
        grid=(n_blocks + 1,),
        in_specs=[
            pl.BlockSpec((MOE_BM * rt, LANES),
                         lambda i, be, nu, nx, pa: (jnp.maximum(jnp.minimum(i, nu[0] - 1), 0), 0)),
            pl.BlockSpec((1, 1, MOE_BM), lambda i, be, nu, nx, pa: (jnp.maximum(i - 1, 0), 0, 0),
                         memory_space=pltpu.SMEM),
            bspec(d_ff), bspec(d_ff), bspec(d),
            hbm, hbm, hbm,
        ],
        out_specs=hbm,
        scratch_shapes=[
            pltpu.VMEM((2, 3, d, d_ff), F32),
            pltpu.VMEM((3, d, d_ff), BF16),
            pltpu.VMEM((2, MOE_BM * rt, LANES), U32),
            pltpu.SemaphoreType.DMA((2,)),
            pltpu.SemaphoreType.DMA,
        ],
    )
    return pl.pallas_call(
        functools.partial(_experts_kernel, n_assign=n_assign, rt=rt),
        grid_spec=grid_spec,
        out_shape=jax.ShapeDtypeStruct(((n_assign + MOE_BM) * rt, LANES), U32),
        compiler_params=_cparams(1),
        name="experts",
    )(block_e, n_used, next_e, parity, xs_t, rowmap.reshape(n_blocks, 1, MOE_BM),
      b_gate, b_up, b_down, w_gate, w_up, w_down)


def _combine_kernel(h2_ref, route_ref, gain_ref, ys_ref, o_ref, *, rt):
    tm = h2_ref.shape[0]
    rec = route_ref[...]
    acc = h2_ref[...]
    for kk in range(TOP_K):
        packed = jnp.concatenate([ys_ref[pl.ds(kk * rt + s, tm, stride=TOP_K * rt), :]
                                  for s in range(rt)], axis=1)
        acc = acc + rec[:, ROUTE_GATE + kk:ROUTE_GATE + kk + 1] * _unpack_pairs_f32(packed)
    o_ref[...] = _rmsnorm(acc, gain_ref[...])


def _combine(h2, route, gain, ys_t, tm=256):
    n, d = h2.shape
    rt = d // 2 // LANES
    return pl.pallas_call(
        functools.partial(_combine_kernel, rt=rt),
        grid=(n // tm,),
        in_specs=[
            pl.BlockSpec((tm, d), lambda i: (i, 0)),
            pl.BlockSpec((tm, LANES), lambda i: (i, 0)),
            pl.BlockSpec((1, d), lambda i: (0, 0)),
            pl.BlockSpec((tm * TOP_K * rt, LANES), lambda i: (i, 0)),
        ],
        out_specs=pl.BlockSpec((tm, d), lambda i: (i, 0)),
        out_shape=jax.ShapeDtypeStruct((n, d), F32),
        compiler_params=_cparams(1),
        name="combine",
    )(h2, route, gain, ys_t)


def kernel(x, mem, norm_mix, w_in, dn_conv, dn_a_log, dn_dt_bias, dn_norm, rg_conv, rg_conv_b, rg_w_a, rg_b_a, rg_w_x, rg_b_x, rg_lambda, w_out, norm_cross, norm_mem, w_cq, w_ckv, w_co, norm_moe, w_router, b_router, w_gate, b_gate, w_up, b_up, w_down, b_down, norm_final):
    batch, seq, d = x.shape
    mem_len = mem.shape[1]
    n = batch * seq
    assert w_in.shape[0] == 1, "single-layer trunk"
    x2 = x.reshape(n, d)

    wi = w_in[0]
    n_gate = 2 * DN_HEADS
    w_cat = jnp.concatenate(
        [wi[:, :4 * D_DN], wi[:, 4 * D_DN + n_gate:],
         jnp.pad(wi[:, 4 * D_DN:4 * D_DN + n_gate], ((0, 0), (0, LANES - n_gate)))],
        axis=1).astype(BF16)
    prow =(jnp.zeros((SUBLANES, LANES), F32)
            .at[0, DN_HEADS:n_gate].set(dn_a_log[0]).at[1, DN_HEADS:n_gate].set(dn_dt_bias[0]))
    pcol = (jnp.zeros((n_gate, LANES), F32)
            .at[DN_HEADS:, 0].set(dn_a_log[0]).at[DN_HEADS:, 1].set(dn_dt_bias[0]))

    proj, gbt = _in_proj(x2, norm_mix, w_cat, prow, pcol)
    dn = _deltanet(proj, gbt, dn_conv[0], dn_norm, batch, seq)
    rg = _rglru(proj, rg_conv[0], rg_conv_b, rg_w_a[0], rg_b_a[0].reshape(1, D_RG),
                rg_w_x[0], rg_b_x[0].reshape(1, D_RG), rg_lambda, batch, seq)
    kv = _mem_kv(mem.reshape(batch * mem_len, d), norm_mem, w_ckv[0])

    w_r = jnp.pad(w_router[0], ((0, 0), (0, LANES - N_EXPERTS)))
    w_r_hi = w_r.astype(BF16)
    w_r_lo = (w_r - w_r_hi.astype(F32)).astype(BF16)
    b_r = jnp.pad(b_router, ((0, 0), (0, LANES - N_EXPERTS)))
    h2, xn, route, route_t, counts = _post_mix(
        dn, rg, x2, w_out[0], norm_cross, w_cq[0], kv, w_co[0], norm_moe, w_r_hi, w_r_lo, b_r,
        seq, mem_len)

    n_blocks = n * TOP_K // MOE_BM + N_EXPERTS
    n_rows = n_blocks * MOE_BM
    cnt = counts[:, 0].astype(I32)
    padded = (cnt + MOE_BM - 1) // MOE_BM * MOE_BM
    pad_end = jnp.cumsum(padded)
    pad_start = pad_end - padded
    n_used = (pad_end[-1:] // MOE_BM).astype(I32)
    block_e = jnp.minimum(
        jnp.sum(pad_end[None, :] <= (jnp.arange(n_blocks, dtype=I32) * MOE_BM)[:, None], axis=1),
        N_EXPERTS - 1).astype(I32)
    meta = jnp.concatenate([pad_start + cnt, padded - cnt, n_used]).astype(I32)

    dest = _dest(pad_start.astype(I32), route_t)
    xs, rowmap = _dispatch(meta, dest, xn, n, n_rows)
    has = cnt > 0
    eid = jnp.where(has, jnp.arange(N_EXPERTS, dtype=I32), N_EXPERTS)
    after = lax.cummin(eid, axis=0, reverse=True)
    next_e = jnp.concatenate([after[1:], jnp.full((1,), N_EXPERTS, I32)]).astype(I32)
    parity = ((jnp.cumsum(has.astype(I32)) - 1) & 1).astype(I32)
    block_e1 = jnp.concatenate([block_e, block_e[-1:]])
    ys = _experts(block_e1, n_used, next_e, parity, xs, rowmap, w_gate[0], b_gate[0][:, None, :],
                  w_up[0], b_up[0][:, None, :], w_down[0], b_down[0][:, None, :], n * TOP_K)
    out = _combine(h2, route, norm_final.reshape(1, d), ys)
    return out.reshape(batch, seq, d)
```

```python
import functools

import jax
import jax.numpy as jnp
from jax import lax
from jax.experimental import pallas as pl
from jax.experimental.pallas import tpu as pltpu

F32 = jnp.float32
BF16 = jnp.bfloat16
I32 = jnp.int32
U32 = jnp.uint32

EPS = 1e-6
LANES = 128
SUBLANES = 8
VMEM_LIMIT = 48 * 1024 * 1024

DN_HEADS = 4
DN_HEAD_DIM = 128
D_DN = DN_HEADS * DN_HEAD_DIM
D_RG = 512
RG_BLOCKS = 4
RG_C = 8.0
CONV_WIDTH = 4
XA_HEADS = 4
N_EXPERTS = 32
TOP_K = 4
SWIGLU_LIMIT = 7.0
SWIGLU_ALPHA = 1.702

DN_CHUNK = 128
INV_BASE = 16
MOE_BM = 256

COL_RX = 4 * D_DN
COL_RY = COL_RX + D_RG
COL_GB = COL_RY + D_RG
PROJ_W = COL_GB + LANES


def _cparams(n_axes=1):
    return pltpu.CompilerParams(
        dimension_semantics=("arbitrary",) * n_axes, vmem_limit_bytes=VMEM_LIMIT)


def _mm(a, b):
    return jnp.dot(a.astype(BF16), b.astype(BF16), preferred_element_type=F32)


def _mm_nt(a, b):
    return lax.dot_general(a.astype(BF16), b.astype(BF16), (((1,), (1,)), ((), ())),
                           preferred_element_type=F32)


def _rmsnorm(x, g):
    return x * lax.rsqrt(jnp.mean(x * x, axis=-1, keepdims=True) + EPS) * g


def _sigmoid(x):
    return 0.5 * jnp.tanh(0.5 * x) + 0.5


def _mm_split(a, b_hi, b_lo):
    a_hi = a.astype(BF16)
    a_lo = (a - a_hi.astype(F32)).astype(BF16)
    return (jnp.dot(a_hi, b_hi, preferred_element_type=F32)
            + jnp.dot(a_hi, b_lo, preferred_element_type=F32)
            + jnp.dot(a_lo, b_hi, preferred_element_type=F32))


def _softplus(x):
    return jnp.maximum(x, 0.0) + jnp.log1p(jnp.exp(-jnp.abs(x)))


def _load_row_tiles(ref, rows):
    rt = ref.shape[0] // rows
    return jnp.concatenate([ref[pl.ds(s, rows, stride=rt), :] for s in range(rt)], axis=1)


def _store_row_tiles(ref, val):
    rows, w = val.shape
    rt = w // LANES
    for s in range(rt):
        ref[pl.ds(s, rows, stride=rt), :] = val[:, s * LANES:(s + 1) * LANES]


def _pack_bf16_pairs(x):
    half = x.shape[1] // 2
    hi = pltpu.bitcast(x[:, :half].astype(BF16).astype(F32), U32)
    lo = pltpu.bitcast(x[:, half:].astype(BF16).astype(F32), U32)
    return hi | (lo >> 16)


def _unpack_pairs_f32(p):
    hi = pltpu.bitcast(p & jnp.uint32(0xFFFF0000), F32)
    lo = pltpu.bitcast(p << 16, F32)
    return jnp.concatenate([hi, lo], axis=1)


def _load_segmented(ref):
    seg = ref.shape[0] // SUBLANES
    return jnp.concatenate([ref[pl.ds(j, SUBLANES, stride=seg), :] for j in range(seg)], axis=0)


def _causal_conv_segmented(xp, halo_ref, cs, w):
    tb = xp.shape[0]
    ng = w.shape[0] - 1
    prev = halo_ref[:, cs]
    last = xp[tb - ng * SUBLANES:, :]
    halo_ref[:, cs] = last
    sub = lax.broadcasted_iota(I32, (SUBLANES, xp.shape[1]), 0)
    groups = []
    for g in range(ng):
        rows = slice(g * SUBLANES, (g + 1) * SUBLANES)
        groups.append(jnp.where(sub == 0, pltpu.roll(prev[rows], 1, 0),
                                pltpu.roll(last[rows], 1, 0)))
    ext = jnp.concatenate(groups + [xp], axis=0)
    y = w[ng:ng + 1] * xp
    for k in range(1, ng + 1):
        y = y + w[ng - k:ng - k + 1] * ext[(ng - k) * SUBLANES:(ng - k) * SUBLANES + tb]
    return y


def _in_proj_kernel(x_ref, g_ref, w_ref, prow_ref, pcol_ref, proj_ref, gbt_ref):
    u = _rmsnorm(x_ref[...], g_ref[...]).astype(BF16)
    for c0 in range(0, COL_GB, 512):
        proj_ref[:, c0:c0 + 512] = jnp.dot(u, w_ref[:, c0:c0 + 512], preferred_element_type=F32)
    ba = jnp.dot(u, w_ref[:, COL_GB:PROJ_W], preferred_element_type=F32)
    lane = lax.broadcasted_iota(I32, ba.shape, 1)
    g = -jnp.exp(prow_ref[0:1, :]) * _softplus(ba + prow_ref[1:2, :])
    proj_ref[:, COL_GB:PROJ_W] = jnp.where(lane < DN_HEADS, _sigmoid(ba), g)
    bat = lax.dot_general(w_ref[:, COL_GB:PROJ_W], u, (((0,), (1,)), ((), ())),
                          preferred_element_type=F32)[0:2 * DN_HEADS, :]
    row = lax.broadcasted_iota(I32, bat.shape, 0)
    gt = -jnp.exp(pcol_ref[:, 0:1]) * _softplus(bat + pcol_ref[:, 1:2])
    gbt_ref[...] = jnp.where(row < DN_HEADS, _sigmoid(bat), gt)


def _in_proj(x2, gain, w_cat, prow, pcol, tm=512):
    n, d = x2.shape
    return pl.pallas_call(
        _in_proj_kernel,
        grid=(n // tm,),
        in_specs=[
            pl.BlockSpec((tm, d), lambda i: (i, 0)),
            pl.BlockSpec((1, d), lambda i: (0, 0)),
            pl.BlockSpec((d, PROJ_W), lambda i: (0, 0)),
            pl.BlockSpec((SUBLANES, LANES), lambda i: (0, 0)),
            pl.BlockSpec((2 * DN_HEADS, LANES), lambda i: (0, 0)),
        ],
        out_specs=[
            pl.BlockSpec((tm, PROJ_W), lambda i: (i, 0)),
            pl.BlockSpec((2 * DN_HEADS, tm), lambda i: (0, i)),
        ],
        out_shape=[
            jax.ShapeDtypeStruct((n, PROJ_W), F32),
            jax.ShapeDtypeStruct((2 * DN_HEADS, n), F32),
        ],
        compiler_params=_cparams(1),
        name="in_proj",
    )(x2, gain, w_cat, prow, pcol)


def _deltanet_kernel(*refs, tb):
    n_grp = 3 * DN_HEADS
    qkv_refs = refs[:n_grp]
    (z_ref, gb_ref, gbt_ref, conv_ref, norm_ref, o_ref,
     s_ref, halo_ref, act_ref, gcc_ref, gcr_ref,
     a_ref, p_ref, d_ref, qk_ref, rhs_ref, u_ref, wq_ref, kdt_ref) = refs[n_grp:]
    t = pl.program_id(1)
    c = DN_CHUNK
    dh = DN_HEAD_DIM

    @pl.when(t == 0)
    def _():
        s_ref[...] = jnp.zeros_like(s_ref)
        halo_ref[...] = jnp.zeros_like(halo_ref)

    sub = lax.broadcasted_iota(I32, (SUBLANES, dh), 0)
    for grp in range(n_grp):
        cs = slice(grp * dh, (grp + 1) * dh)
        x = qkv_refs[grp][...]
        prev = halo_ref[:, cs]
        halo_ref[:, cs] = x[tb - SUBLANES:, :]
        y = conv_ref[CONV_WIDTH - 1:CONV_WIDTH, cs] * x
        for k in range(1, CONV_WIDTH):
            xs = pltpu.roll(x, k, 0)
            head = jnp.where(sub < k, pltpu.roll(prev, k, 0), xs[:SUBLANES])
            xs = jnp.concatenate([head, xs[SUBLANES:]], axis=0)
            y = y + conv_ref[CONV_WIDTH - 1 - k:CONV_WIDTH - k, cs] * xs
        y = y * _sigmoid(y)
        if grp < 2 * DN_HEADS:
            y = y * lax.rsqrt(jnp.sum(y * y, axis=-1, keepdims=True) + EPS)
        if grp < DN_HEADS:
            y = y * (dh ** -0.5)
        act_ref[grp] = y

    gcol = gb_ref[...]
    rpos = lax.broadcasted_iota(I32, gcol.shape, 0) & (c - 1)
    d = 1
    while d < c:
        gcol = gcol + jnp.where(rpos >= d, pltpu.roll(gcol, d, 0), 0.0)
        d *= 2
    gcc_ref[...] = gcol
    grow = gbt_ref[...]
    lpos = lax.broadcasted_iota(I32, grow.shape, 1) & (c - 1)
    d = 1
    while d < c:
        grow = grow + jnp.where(lpos >= d, pltpu.roll(grow, d, 1), 0.0)
        d *= 2
    gcr_ref[...] = grow

    row = lax.broadcasted_iota(I32, (c, c), 0)
    col = lax.broadcasted_iota(I32, (c, c), 1)
    causal = row >= col
    strict = row > col
    eye = jnp.where(row == col, 1.0, 0.0)
    gain = norm_ref[...]
    n_chunks = tb // c
    probs = [(ci, h) for ci in range(n_chunks) for h in range(DN_HEADS)]

    decay_last = []
    for p, (ci, h) in enumerate(probs):
        rows = slice(ci * c, (ci + 1) * c)
        q = act_ref[h, rows, :]
        k = act_ref[DN_HEADS + h, rows, :]
        v = act_ref[2 * DN_HEADS + h, rows, :]
        beta = gb_ref[rows, h:h + 1]
        gc = gcc_ref[rows, DN_HEADS + h:DN_HEADS + h + 1]
        gr = gcr_ref[DN_HEADS + h:DN_HEADS + h + 1, rows]
        g_last = gc[c - 1:c, :]
        decay = jnp.where(causal, jnp.exp(jnp.where(causal, gc - gr, 0.0)), 0.0)
        kb = k * beta
        both = _mm_nt(jnp.concatenate([kb, q], axis=0), k)
        a_ref[p] = jnp.where(strict, both[:c] * decay, 0.0)
        qk_ref[p] = (both[c:] * decay).astype(BF16)
        egc = jnp.exp(gc)
        rhs_ref[p] = jnp.concatenate([v * beta, kb * egc], axis=1).astype(BF16)
        wq_ref[p, c:, :] = (q * egc).astype(BF16)
        kdt_ref[p] = (k * jnp.exp(g_last - gc)).T.astype(BF16)
        decay_last.append(jnp.exp(g_last))

    shift = INV_BASE.bit_length() - 1
    blk = (row >> shift) == (col >> shift)
    for p in range(len(probs)):
        diag = jnp.where(blk, a_ref[p], 0.0)
        p_ref[p] = eye - diag
        d_ref[p] = _mm(diag, diag).astype(BF16)
    for it in range(shift - 1):
        for p in range(len(probs)):
            pw = d_ref[p]
            inv = p_ref[p]
            p_ref[p] = inv + _mm(inv, pw)
            if it < shift - 2:
                d_ref[p] = _mm(pw, pw).astype(BF16)
    s = INV_BASE
    while s < c:
        sh = s.bit_length() - 1
        off = ((row >> (sh + 1)) == (col >> (sh + 1))) & ((row >> sh) != (col >> sh))
        for p in range(len(probs)):
            d_ref[p] = _mm(p_ref[p], jnp.where(off, a_ref[p], 0.0)).astype(BF16)
        for p in range(len(probs)):
            inv = p_ref[p]
            p_ref[p] = inv - _mm(d_ref[p], inv)
        s *= 2
    for p in range(len(probs)):
        uw = _mm(p_ref[p], rhs_ref[p])
        u_ref[p] = uw[:, :dh]
        wq_ref[p, :c, :] = uw[:, dh:].astype(BF16)

    for ci in range(n_chunks):
        rows = slice(ci * c, (ci + 1) * c)
        ps = [ci * DN_HEADS + h for h in range(DN_HEADS)]
        s_old = [s_ref[h] for h in range(DN_HEADS)]
        ws = [_mm(wq_ref[p], s_old[h]) for h, p in enumerate(ps)]
        v_new = [u_ref[p] - ws[h][:c] for h, p in enumerate(ps)]
        outs = [ws[h][c:] + _mm(qk_ref[p], v_new[h]) for h, p in enumerate(ps)]
        for h, p in enumerate(ps):
            s_ref[h] = s_old[h] * decay_last[p] + _mm(kdt_ref[p], v_new[h])
        for h in range(DN_HEADS):
            o = _rmsnorm(outs[h], gain)
            zz = z_ref[rows, h * dh:(h + 1) * dh]
            o_ref[rows, h * dh:(h + 1) * dh] = (o * (zz * _sigmoid(zz))).astype(o_ref.dtype)


def _deltanet(proj, gbt, dn_conv, dn_norm, batch, seq, tb=512):
    n = proj.shape[0]
    nt = seq // tb
    c = DN_CHUNK
    n_prob = (tb // c) * DN_HEADS
    n_grp = 3 * DN_HEADS
    grp = lambda j: pl.BlockSpec((tb, DN_HEAD_DIM), lambda b, t, j=j: (b * nt + t, j))
    return pl.pallas_call(
        functools.partial(_deltanet_kernel, tb=tb),
        grid=(batch, nt),
        in_specs=[grp(j) for j in range(n_grp)] + [
            pl.BlockSpec((tb, D_DN), lambda b, t: (b * nt + t, 3)),
            pl.BlockSpec((tb, LANES), lambda b, t: (b * nt + t, COL_GB // LANES)),
            pl.BlockSpec((2 * DN_HEADS, tb), lambda b, t: (0, b * nt + t)),
            pl.BlockSpec((CONV_WIDTH, 3 * D_DN), lambda b, t: (0, 0)),
            pl.BlockSpec((1, DN_HEAD_DIM), lambda b, t: (0, 0)),
        ],
        out_specs=pl.BlockSpec((tb, D_DN), lambda b, t: (b * nt + t, 0)),
        out_shape=jax.ShapeDtypeStruct((n, D_DN), BF16),
        scratch_shapes=[
            pltpu.VMEM((DN_HEADS, DN_HEAD_DIM, DN_HEAD_DIM), F32),
            pltpu.VMEM((SUBLANES, 3 * D_DN), F32),
            pltpu.VMEM((n_grp, tb, DN_HEAD_DIM), F32),
            pltpu.VMEM((tb, LANES), F32),
            pltpu.VMEM((2 * DN_HEADS, tb), F32),
            pltpu.VMEM((n_prob, c, c), F32),
            pltpu.VMEM((n_prob, c, c), F32),
            pltpu.VMEM((n_prob, c, c), BF16),
            pltpu.VMEM((n_prob, c, c), BF16),
            pltpu.VMEM((n_prob, c, 2 * DN_HEAD_DIM), BF16),
            pltpu.VMEM((n_prob, c, DN_HEAD_DIM), F32),
            pltpu.VMEM((n_prob, 2 * c, DN_HEAD_DIM), BF16),
            pltpu.VMEM((n_prob, DN_HEAD_DIM, c), BF16),
        ],
        compiler_params=_cparams(2),
        name="deltanet",
    )(*([proj] * (n_grp + 2)), gbt, dn_conv, dn_norm)


def _gelu_tanh(x):
    return 0.5 * x * (1.0 + jnp.tanh(0.7978845608028654 * (x + 0.044715 * (x * x * x))))


def _rglru_kernel(*refs, tb):
    rx_refs = refs[:RG_BLOCKS]
    (ry_ref, conv_ref, convb_ref, wa_ref, ba_ref, wx_ref, bx_ref, lam_ref,
     o_ref, hc_ref, halo_ref, a_ref, b_ref, h_ref) = refs[RG_BLOCKS:]
    t = pl.program_id(1)
    bw = D_RG // RG_BLOCKS
    seg = tb // SUBLANES

    @pl.when(t == 0)
    def _():
        hc_ref[...] = jnp.zeros_like(hc_ref)
        halo_ref[...] = jnp.zeros_like(halo_ref)

    log_sig = -_softplus(-lam_ref[...])
    rowi = lax.broadcasted_iota(I32, (tb, bw), 0)
    seq_start_row = jnp.where(t == 0, 0, -1)
    for nb in range(RG_BLOCKS):
        cs = slice(nb * bw, (nb + 1) * bw)
        xb = (_causal_conv_segmented(_load_segmented(rx_refs[nb]), halo_ref, cs,
                                     conv_ref[:, cs]) + convb_ref[:, cs])
        r = _sigmoid(_mm(xb, wa_ref[nb]) + ba_ref[:, cs])
        gi = _sigmoid(_mm(xb, wx_ref[nb]) + bx_ref[:, cs])
        log_a = RG_C * r * log_sig[:, cs]
        a = jnp.exp(log_a)
        y = jnp.tanh(-log_a) * (1.0 + a * a)
        mult = jnp.where(y > 0.0, y * lax.rsqrt(y), 0.0)
        mult = jnp.where(rowi == seq_start_row, 1.0, mult)
        a_ref[:, cs] = a
        b_ref[:, cs] = mult * (gi * xb)

    def scan(j, carry):
        ac, bc = carry
        rows = pl.ds(pl.multiple_of(j * SUBLANES, SUBLANES), SUBLANES)
        aj = a_ref[rows, :]
        bc = aj * bc + b_ref[rows, :]
        ac = aj * ac
        a_ref[rows, :] = ac
        b_ref[rows, :] = bc
        return ac, bc

    ac, bc = lax.fori_loop(1, seg, scan, (a_ref[0:SUBLANES, :], b_ref[0:SUBLANES, :]), unroll=7)

    h = hc_ref[...]
    h_in = []
    for s in range(SUBLANES):
        h_in.append(h)
        h = ac[s:s + 1, :] * h + bc[s:s + 1, :]
    hc_ref[...] = h
    h_in = jnp.concatenate(h_in, axis=0)

    for j in range(seg):
        rows = slice(j * SUBLANES, (j + 1) * SUBLANES)
        hj = a_ref[rows, :] * h_in + b_ref[rows, :]
        for nb in range(RG_BLOCKS):
            h_ref[nb, rows, :] = hj[:, nb * bw:(nb + 1) * bw]
    g = seg // SUBLANES
    for nb in range(RG_BLOCKS):
        cs = slice(nb * bw, (nb + 1) * bw)
        h = jnp.concatenate(
            [h_ref[nb, pl.ds((i % g) * SUBLANES * SUBLANES + i // g, SUBLANES, stride=SUBLANES), :]
             for i in range(seg)], axis=0)
        o_ref[:, cs] = (h * _gelu_tanh(ry_ref[:, cs])).astype(o_ref.dtype)


def _rglru(proj, rg_conv, rg_conv_b, w_a, b_a, w_x, b_x, lam, batch, seq, tb=512):
    n = proj.shape[0]
    nt = seq // tb
    bw = D_RG // RG_BLOCKS
    full = lambda shape: pl.BlockSpec(shape, lambda b, t: (0,) * len(shape))
    return pl.pallas_call(
        functools.partial(_rglru_kernel, tb=tb),
        grid=(batch, nt),
        in_specs=[pl.BlockSpec((tb, bw), lambda b, t, j=j: (b * nt + t, COL_RX // bw + j))
                  for j in range(RG_BLOCKS)] + [
            pl.BlockSpec((tb, D_RG), lambda b, t: (b * nt + t, COL_RY // D_RG)),
            full((CONV_WIDTH, D_RG)), full((1, D_RG)),
            full((RG_BLOCKS, bw, bw)), full((1, D_RG)),
            full((RG_BLOCKS, bw, bw)), full((1, D_RG)),
            full((1, D_RG)),
        ],
        out_specs=pl.BlockSpec((tb, D_RG), lambda b, t: (b * nt + t, 0)),
        out_shape=jax.ShapeDtypeStruct((n, D_RG), BF16),
        scratch_shapes=[
            pltpu.VMEM((1, D_RG), F32),
            pltpu.VMEM(((CONV_WIDTH - 1) * SUBLANES, D_RG), F32),
            pltpu.VMEM((tb, D_RG), F32),
            pltpu.VMEM((tb, D_RG), F32),
            pltpu.VMEM((RG_BLOCKS, tb, bw), F32),
        ],
        compiler_params=_cparams(2),
        name="rglru",
    )(*([proj] * (RG_BLOCKS + 1)), rg_conv, rg_conv_b, w_a, b_a, w_x, b_x, lam)


def _mem_kv_kernel(m_ref, g_ref, w_ref, o_ref):
    mn = _rmsnorm(m_ref[...], g_ref[...]).astype(BF16)
    for c0 in range(0, o_ref.shape[1], 512):
        o_ref[:, c0:c0 + 512] = jnp.dot(mn, w_ref[:, c0:c0 + 512].astype(BF16),
                                        preferred_element_type=F32).astype(o_ref.dtype)


def _mem_kv(mem2, gain, w_ckv):
    n, d = mem2.shape
    return pl.pallas_call(
        _mem_kv_kernel,
        grid=(1,),
        in_specs=[
            pl.BlockSpec((n, d), lambda i: (0, 0)),
            pl.BlockSpec((1, d), lambda i: (0, 0)),
            pl.BlockSpec((d, 2 * d), lambda i: (0, 0)),
        ],
        out_specs=pl.BlockSpec((n, 2 * d), lambda i: (0, 0)),
        out_shape=jax.ShapeDtypeStruct((n, 2 * d), BF16),
        compiler_params=_cparams(1),
        name="mem_kv",
    )(mem2, gain, w_ckv)


ROUTE_E = 0
ROUTE_RANK = TOP_K
ROUTE_GATE = 2 * TOP_K
ROUTE_ROWS = 16


def _post_mix_kernel(dn_ref, rg_ref, x_ref, wo_f32, gx_ref, wq_f32, kv_ref, wco_f32, gm_ref,
                     wrh_ref, wrl_ref, br_ref, h2_ref, xn_ref, route_ref, routet_ref, cnt_ref,
                     carry_ref, wo_ref, wq_ref, wco_ref):
    i = pl.program_id(0)
    tm, d = x_ref.shape
    hd = d // XA_HEADS

    @pl.when(i == 0)
    def _():
        carry_ref[...] = jnp.zeros_like(carry_ref)
        wo_ref[...] = wo_f32[...].astype(BF16)
        wq_ref[...] = wq_f32[...].astype(BF16)
        wco_ref[...] = wco_f32[...].astype(BF16)

    h1 = (x_ref[...] + jnp.dot(dn_ref[...], wo_ref[0:D_DN, :], preferred_element_type=F32)
          + jnp.dot(rg_ref[...], wo_ref[D_DN:, :], preferred_element_type=F32))

    hn = _rmsnorm(h1, gx_ref[...]).astype(BF16)
    q = jnp.dot(hn, wq_ref[...], preferred_element_type=F32)
    heads = []
    for hh in range(XA_HEADS):
        cs = slice(hh * hd, (hh + 1) * hd)
        s = _mm_nt(q[:, cs], kv_ref[:, cs]) * (hd ** -0.5)
        p = jnp.exp(s - jnp.max(s, axis=-1, keepdims=True))
        p = p / jnp.sum(p, axis=-1, keepdims=True)
        heads.append(_mm(p, kv_ref[:, d + hh * hd:d + (hh + 1) * hd]).astype(BF16))
    o = jnp.concatenate(heads, axis=1)
    h2 = h1 + jnp.dot(o, wco_ref[...], preferred_element_type=F32)
    h2_ref[...] = h2

    xn = _rmsnorm(h2, gm_ref[...])
    _store_row_tiles(xn_ref, _pack_bf16_pairs(xn))
    logits = _mm_split(xn, wrh_ref[...], wrl_ref[...]) + br_ref[...]
    lg = logits.T[0:N_EXPERTS, :]
    eidx = lax.broadcasted_iota(I32, lg.shape, 0).astype(F32)
    neg = jnp.float32(-jnp.inf)
    vals, idxs, hots = [], [], []
    for _ in range(TOP_K):
        m = jnp.max(lg, axis=0, keepdims=True)
        idx = jnp.min(jnp.where(lg == m, eidx, float(N_EXPERTS)), axis=0, keepdims=True)
        hot = eidx == idx
        lg = jnp.where(hot, neg, lg)
        vals.append(m)
        idxs.append(idx)
        hots.append(hot)
    es = [jnp.exp(v - vals[0]) for v in vals]
    den = es[0] + es[1] + es[2] + es[3]
    gates = [e / den for e in es]

    chosen = jnp.zeros(lg.shape, F32)
    for hot in hots:
        chosen = chosen + jnp.where(hot, 1.0, 0.0)
    r2 = lax.broadcasted_iota(I32, (tm, tm), 0)
    c2 = lax.broadcasted_iota(I32, (tm, tm), 1)
    before = _mm(chosen, jnp.where(r2 < c2, 1.0, 0.0)) + carry_ref[:, 0:1]
    ranks = [jnp.sum(jnp.where(hot, before, 0.0), axis=0, keepdims=True) for hot in hots]
    carry_ref[...] = carry_ref[...] + jnp.sum(chosen, axis=1, keepdims=True)
    cnt_ref[...] = carry_ref[...]

    row = lax.broadcasted_iota(I32, (ROUTE_ROWS, tm), 0)
    rect = jnp.zeros((ROUTE_ROWS, tm), F32)
    for kk in range(TOP_K):
        rect = jnp.where(row == ROUTE_E + kk, idxs[kk], rect)
        rect = jnp.where(row == ROUTE_RANK + kk, ranks[kk], rect)
        rect = jnp.where(row == ROUTE_GATE + kk, gates[kk], rect)
    routet_ref[...] = rect
    route_ref[...] = jnp.concatenate(
        [rect, jnp.zeros((LANES - ROUTE_ROWS, tm), F32)], axis=0).T


def _post_mix(dn, rg, x2, w_out, g_cross, w_cq, kv, w_co, g_moe, w_r_hi, w_r_lo, b_r, seq, mem_len,
              tm=512):
    n, d = x2.shape
    per_b = seq // tm
    rt_x = d // 2 // LANES
    full = lambda shape: pl.BlockSpec(shape, lambda i: (0,) * len(shape))
    once = lambda shape: pl.BlockSpec(shape, lambda i: (0,) * len(shape),
                                      pipeline_mode=pl.Buffered(1))
    return pl.pallas_call(
        _post_mix_kernel,
        grid=(n // tm,),
        in_specs=[
            pl.BlockSpec((tm, D_DN), lambda i: (i, 0)),
            pl.BlockSpec((tm, D_RG), lambda i: (i, 0)),
            pl.BlockSpec((tm, d), lambda i: (i, 0)),
            once((d, d)), full((1, d)), once((d, d)),
            pl.BlockSpec((mem_len, 2 * d), lambda i: (i // per_b, 0)),
            once((d, d)), full((1, d)),
            full((d, LANES)), full((d, LANES)), full((1, LANES)),
        ],
        out_specs=[
            pl.BlockSpec((tm, d), lambda i: (i, 0)),
            pl.BlockSpec((tm * rt_x, LANES), lambda i: (i, 0)),
            pl.BlockSpec((tm, LANES), lambda i: (i, 0)),
            pl.BlockSpec((ROUTE_ROWS, tm), lambda i: (0, i)),
            pl.BlockSpec((N_EXPERTS, LANES), lambda i: (0, 0)),
        ],
        out_shape=[
            jax.ShapeDtypeStruct((n, d), F32),
            jax.ShapeDtypeStruct((n * rt_x, LANES), U32),
            jax.ShapeDtypeStruct((n, LANES), F32),
            jax.ShapeDtypeStruct((ROUTE_ROWS, n), F32),
            jax.ShapeDtypeStruct((N_EXPERTS, LANES), F32),
        ],
        scratch_shapes=[pltpu.VMEM((N_EXPERTS, LANES), F32),
                        pltpu.VMEM((d, d), BF16), pltpu.VMEM((d, d), BF16), pltpu.VMEM((d, d), BF16)],
        compiler_params=_cparams(1),
        name="post_mix",
    )(dn, rg, x2, w_out, g_cross, w_cq, kv, w_co, g_moe, w_r_hi, w_r_lo, b_r)


def _dest_kernel(start_ref, routet_ref, dest_ref):
    e = routet_ref[ROUTE_E:ROUTE_E + SUBLANES, :]
    rank = routet_ref[ROUTE_RANK:ROUTE_RANK + SUBLANES, :]
    base = jnp.zeros(e.shape, F32)
    for j in range(N_EXPERTS):
        base = jnp.where(e == float(j), start_ref[j].astype(F32), base)
    row = lax.broadcasted_iota(I32, e.shape, 0)
    dest_ref[...] = jnp.where(row < TOP_K, (base + rank).astype(I32), 0)


def _dest(pad_start, route_t, tm=2048):
    n = route_t.shape[1]
    grid_spec = pltpu.PrefetchScalarGridSpec(
        num_scalar_prefetch=1,
        grid=(n // tm,),
        in_specs=[pl.BlockSpec((ROUTE_ROWS, tm), lambda i, st: (0, i))],
        out_specs=pl.BlockSpec((SUBLANES, tm), lambda i, st: (0, i)),
    )
    return pl.pallas_call(
        _dest_kernel,
        grid_spec=grid_spec,
        out_shape=jax.ShapeDtypeStruct((SUBLANES, n), I32),
        compiler_params=_cparams(1),
        name="dest",
    )(pad_start, route_t)


def _row_tile(ref, r, rt):
    return ref.at[pl.ds(pl.multiple_of(r * rt, rt), rt)]


def _dispatch_kernel(meta_ref, dest_ref, xn_ref, xs_ref, zero_ref, ring_ref, sem, zsem,
                     *, tm, rt, n_blocks):
    i = pl.program_id(0)
    bm = MOE_BM * rt
    ring = ring_ref.at[i & 1]
    ring[...] = xn_ref[...]

    def per_token(tk, carry):
        src = _row_tile(ring, tk, rt)
        for kk in range(TOP_K):
            r = dest_ref[kk, tk]
            pltpu.make_async_copy(src, _row_tile(xs_ref, r, rt),
                                  sem.at[i & 1]).start(priority=kk % 2)
        return carry

    lax.fori_loop(0, tm, per_token, 0, unroll=4)

    @pl.when(i == 0)
    def _():
        zero_ref[...] = jnp.zeros_like(zero_ref)

        def pad_copy(r):
            return pltpu.make_async_copy(zero_ref.at[pl.ds(0, rt)], _row_tile(xs_ref, r, rt), zsem)

        def per_expert(e, total):
            r0 = meta_ref[e]
            cnt = meta_ref[N_EXPERTS + e]

            def one(r, carry):
                pad_copy(r0 + r).start()
                return carry

            lax.fori_loop(0, cnt, one, 0)
            return total + cnt

        total = lax.fori_loop(0, N_EXPERTS, per_expert, 0)

        def drain(r, carry):
            pad_copy(0).wait()
            return carry

        lax.fori_loop(0, total, drain, 0)

        def tail_copy(b):
            return pltpu.make_async_copy(
                zero_ref, xs_ref.at[pl.ds(pl.multiple_of(b * bm, bm), bm)], zsem)

        n_used = meta_ref[2 * N_EXPERTS]

        def tail(b, carry):
            tail_copy(b).start()
            return carry

        lax.fori_loop(n_used, n_blocks, tail, 0)

        def tail_drain(b, carry):
            tail_copy(0).wait()
            return carry

        lax.fori_loop(n_used, n_blocks, tail_drain, 0)

    def wait_step(parity):
        for _ in range(TOP_K):
            pltpu.make_async_copy(ring_ref.at[parity], xs_ref.at[pl.ds(0, tm * rt)],
                                  sem.at[parity]).wait()

    @pl.when(i > 0)
    def _():
        wait_step(1 - (i & 1))

    @pl.when(i == pl.num_programs(0) - 1)
    def _():
        wait_step(i & 1)


def _dispatch(meta, dest, xn_t, n_tok, n_rows, tm=256):
    rt = xn_t.shape[0] // n_tok
    n_blocks = n_rows // MOE_BM
    grid_spec = pltpu.PrefetchScalarGridSpec(
        num_scalar_prefetch=1,
        grid=(n_tok // tm,),
        in_specs=[
            pl.BlockSpec((SUBLANES, tm), lambda i, meta: (0, i), memory_space=pltpu.SMEM),
            pl.BlockSpec((tm * rt, LANES), lambda i, meta: (i, 0)),
        ],
        out_specs=pl.BlockSpec(memory_space=pl.ANY),
        scratch_shapes=[
            pltpu.VMEM((MOE_BM * rt, LANES), xn_t.dtype),
            pltpu.VMEM((2, tm * rt, LANES), xn_t.dtype),
            pltpu.SemaphoreType.DMA((2,)),
            pltpu.SemaphoreType.DMA,
        ],
    )
    return pl.pallas_call(
        functools.partial(_dispatch_kernel, tm=tm, rt=rt, n_blocks=n_blocks),
        grid_spec=grid_spec,
        out_shape=jax.ShapeDtypeStruct((n_rows * rt, LANES), xn_t.dtype),
        compiler_params=_cparams(1),
        name="dispatch",
    )(meta, dest, xn_t)


def _experts_kernel(be_ref, nu_ref, nxt_ref, par_ref, x_ref, bg_ref, bu_ref, bd_ref,
                    wg_hbm, wu_hbm, wd_hbm, y_ref, wf_ref, wb_ref, sem):
    i = pl.program_id(0)
    used = i < nu_ref[0]
    e = be_ref[i]
    changed = (i == 0) | (e != be_ref[jnp.maximum(i - 1, 0)])

    def weight_copies(expert, slot):
        return [pltpu.make_async_copy(w.at[0, expert], wf_ref.at[slot, j], sem.at[slot])
                for j, w in enumerate((wg_hbm, wu_hbm, wd_hbm))]

    @pl.when(i == 0)
    def _():
        for cp in weight_copies(e, par_ref[e]):
            cp.start()

    @pl.when(used & changed)
    def _():
        slot = par_ref[e]
        for cp in weight_copies(e, slot):
            cp.wait()
        nxt = nxt_ref[e]

        @pl.when(nxt < N_EXPERTS)
        def _():
            for cp in weight_copies(nxt, 1 - slot):
                cp.start()

        for j in range(3):
            wb_ref[j] = wf_ref[slot, j].astype(BF16)

    @pl.when(used)
    def _():
        x = _unpack_pairs_f32(_load_row_tiles(x_ref, MOE_BM)).astype(BF16)
        gt = jnp.minimum(jnp.dot(x, wb_ref[0], preferred_element_type=F32) + bg_ref[...],
                         SWIGLU_LIMIT)
        up = jnp.clip(jnp.dot(x, wb_ref[1], preferred_element_type=F32) + bu_ref[...],
                      -SWIGLU_LIMIT, SWIGLU_LIMIT)
        hid = (up + 1.0) * (gt * _sigmoid(SWIGLU_ALPHA * gt))
        y = jnp.dot(hid.astype(BF16), wb_ref[2], preferred_element_type=F32) + bd_ref[...]
        _store_row_tiles(y_ref, _pack_bf16_pairs(y))

    @pl.when(jnp.logical_not(used))
    def _():
        y_ref[...] = jnp.zeros_like(y_ref)


def _experts(block_e, n_used, next_e, parity, xs_t, w_gate, b_gate, w_up, b_up, w_down, b_down):
    d, d_ff = w_gate.shape[2:]
    assert d == d_ff, "weight staging buffers assume square expert matrices"
    rt = d // 2 // LANES
    n_blocks = xs_t.shape[0] // (MOE_BM * rt)
    bspec = lambda m: pl.BlockSpec((None, 1, m), lambda i, be, nu, nx, pa: (be[i], 0, 0))
    hbm = pl.BlockSpec(memory_space=pl.ANY)
    grid_spec = pltpu.PrefetchScalarGridSpec(
        num_scalar_prefetch=4,
        grid=(n_blocks,),
        in_specs=[
            pl.BlockSpec((MOE_BM * rt, LANES),
                         lambda i, be, nu, nx, pa: (jnp.maximum(jnp.minimum(i, nu[0] - 1), 0), 0)),
            bspec(d_ff), bspec(d_ff), bspec(d),
            hbm, hbm, hbm,
        ],
        out_specs=pl.BlockSpec((MOE_BM * rt, LANES), lambda i, be, nu, nx, pa: (i, 0)),
        scratch_shapes=[
            pltpu.VMEM((2, 3, d, d_ff), F32),
            pltpu.VMEM((3, d, d_ff), BF16),
            pltpu.SemaphoreType.DMA((2,)),
        ],
    )
    return pl.pallas_call(
        _experts_kernel,
        grid_spec=grid_spec,
        out_shape=jax.ShapeDtypeStruct(xs_t.shape, U32),
        compiler_params=_cparams(1),
        name="experts",
    )(block_e, n_used, next_e, parity, xs_t, b_gate, b_up, b_down, w_gate, w_up, w_down)


def _combine_kernel(dcur_ref, dnext_ref, h2_ref, route_ref, gain_ref, ys_ref, o_ref,
                    buf_ref, sem, *, tm, rt):
    i = pl.program_id(0)
    nsteps = pl.num_programs(0)
    slot = i & 1

    def issue_all(dref, s):
        def per_token(tk, carry):
            for kk in range(TOP_K):
                r = dref[kk, tk]
                pltpu.make_async_copy(_row_tile(ys_ref, r, rt),
                                      _row_tile(buf_ref.at[s, kk], tk, rt),
                                      sem.at[s]).start(priority=kk % 2)
            return carry
        lax.fori_loop(0, tm, per_token, 0, unroll=4)

    @pl.when(i == 0)
    def _():
        issue_all(dcur_ref, 0)

    @pl.when(i + 1 < nsteps)
    def _():
        issue_all(dnext_ref, 1 - slot)

    for kk in range(TOP_K):
        pltpu.make_async_copy(ys_ref.at[pl.ds(0, tm * rt)], buf_ref.at[slot, kk],
                              sem.at[slot]).wait()

    rec = route_ref[...]
    acc = h2_ref[...]
    for kk in range(TOP_K):
        acc = acc + (rec[:, ROUTE_GATE + kk:ROUTE_GATE + kk + 1]
                     * _unpack_pairs_f32(_load_row_tiles(buf_ref.at[slot, kk], tm)))
    o_ref[...] = _rmsnorm(acc, gain_ref[...])


def _combine(dest, h2, route, gain, ys_t, tm=128):
    n, d = h2.shape
    nsteps = n // tm
    rt = d // 2 // LANES
    return pl.pallas_call(
        functools.partial(_combine_kernel, tm=tm, rt=rt),
        grid=(nsteps,),
        in_specs=[
            pl.BlockSpec((SUBLANES, tm), lambda i: (0, i), memory_space=pltpu.SMEM),
            pl.BlockSpec((SUBLANES, tm), lambda i: (0, jnp.minimum(i + 1, nsteps - 1)),
                         memory_space=pltpu.SMEM),
            pl.BlockSpec((tm, d), lambda i: (i, 0)),
            pl.BlockSpec((tm, LANES), lambda i: (i, 0)),
            pl.BlockSpec((1, d), lambda i: (0, 0)),
            pl.BlockSpec(memory_space=pl.ANY),
        ],
        out_specs=pl.BlockSpec((tm, d), lambda i: (i, 0)),
        out_shape=jax.ShapeDtypeStruct((n, d), F32),
        scratch_shapes=[
            pltpu.VMEM((2, TOP_K, tm * rt, LANES), U32),
            pltpu.SemaphoreType.DMA((2,)),
        ],
        compiler_params=_cparams(1),
        name="combine",
    )(dest, dest, h2, route, gain, ys_t)


def kernel(x, mem, norm_mix, w_in, dn_conv, dn_a_log, dn_dt_bias, dn_norm, rg_conv, rg_conv_b, rg_w_a, rg_b_a, rg_w_x, rg_b_x, rg_lambda, w_out, norm_cross, norm_mem, w_cq, w_ckv, w_co, norm_moe, w_router, b_router, w_gate, b_gate, w_up, b_up, w_down, b_down, norm_final):
    batch, seq, d = x.shape
    mem_len = mem.shape[1]
    n = batch * seq
    assert w_in.shape[0] == 1, "single-layer trunk"
    x2 = x.reshape(n, d)

    wi = w_in[0]
    n_gate = 2 * DN_HEADS
    w_cat = jnp.concatenate(
        [wi[:, :4 * D_DN], wi[:, 4 * D_DN + n_gate:],
         jnp.pad(wi[:, 4 * D_DN:4 * D_DN + n_gate], ((0, 0), (0, LANES - n_gate)))],
        axis=1).astype(BF16)
    prow = (jnp.zeros((SUBLANES, LANES), F32)
            .at[0, DN_HEADS:n_gate].set(dn_a_log[0]).at[1, DN_HEADS:n_gate].set(dn_dt_bias[0]))
    pcol = (jnp.zeros((n_gate, LANES), F32)
            .at[DN_HEADS:, 0].set(dn_a_log[0]).at[DN_HEADS:, 1].set(dn_dt_bias[0]))

    proj, gbt = _in_proj(x2, norm_mix, w_cat, prow, pcol)
    dn = _deltanet(proj, gbt, dn_conv[0], dn_norm, batch, seq)
    rg = _rglru(proj, rg_conv[0], rg_conv_b, rg_w_a[0], rg_b_a[0].reshape(1, D_RG),
                rg_w_x[0], rg_b_x[0].reshape(1, D_RG), rg_lambda, batch, seq)
    kv = _mem_kv(mem.reshape(batch * mem_len, d), norm_mem, w_ckv[0])

    w_r = jnp.pad(w_router[0], ((0, 0), (0, LANES - N_EXPERTS)))
    w_r_hi = w_r.astype(BF16)
    w_r_lo = (w_r - w_r_hi.astype(F32)).astype(BF16)
    b_r = jnp.pad(b_router, ((0, 0), (0, LANES - N_EXPERTS)))
    h2, xn, route, route_t, counts = _post_mix(
        dn, rg, x2, w_out[0], norm_cross, w_cq[0], kv, w_co[0], norm_moe, w_r_hi, w_r_lo, b_r,
        seq, mem_len)

    n_blocks = n * TOP_K // MOE_BM + N_EXPERTS
    n_rows = n_blocks * MOE_BM
    cnt = counts[:, 0].astype(I32)
    padded = (cnt + MOE_BM - 1) // MOE_BM * MOE_BM
    pad_end = jnp.cumsum(padded)
    pad_start = pad_end - padded
    n_used = (pad_end[-1:] // MOE_BM).astype(I32)
    block_e = jnp.minimum(
        jnp.sum(pad_end[None, :] <= (jnp.arange(n_blocks, dtype=I32) * MOE_BM)[:, None], axis=1),
        N_EXPERTS - 1).astype(I32)
    meta = jnp.concatenate([pad_start + cnt, padded - cnt, n_used]).astype(I32)

    dest = _dest(pad_start.astype(I32), route_t)
    xs = _dispatch(meta, dest, xn, n, n_rows)
    has = cnt > 0
    eid = jnp.where(has, jnp.arange(N_EXPERTS, dtype=I32), N_EXPERTS)
    after = lax.cummin(eid, axis=0, reverse=True)
    next_e = jnp.concatenate([after[1:], jnp.full((1,), N_EXPERTS, I32)]).astype(I32)
    parity = ((jnp.cumsum(has.astype(I32)) - 1) & 1).astype(I32)
    ys = _experts(block_e, n_used, next_e, parity, xs, w_gate, b_gate[0][:, None, :], w_up,
                  b_up[0][:, None, :], w_down, b_down[0][:, None, :])
    out = _combine(dest, h2, route, norm_final.reshape(1, d), ys)
    return out.reshape(batch, seq, d)
```

```python
import functools

import jax
import jax.numpy as jnp
from jax import lax
from jax.experimental import pallas as pl
from jax.experimental.pallas import tpu as pltpu

F32 = jnp.float32
BF16 = jnp.bfloat16
I32 = jnp.int32
U32 = jnp.uint32

EPS = 1e-6
LANES = 128
SUBLANES = 8
VMEM_LIMIT = 48 * 1024 * 1024

DN_HEADS = 4
DN_HEAD_DIM = 128
D_DN = DN_HEADS * DN_HEAD_DIM
D_RG = 512
RG_BLOCKS = 4
RG_C = 8.0
CONV_WIDTH = 4
XA_HEADS = 4
N_EXPERTS = 32
TOP_K = 4
SWIGLU_LIMIT = 7.0
SWIGLU_ALPHA = 1.702

DN_CHUNK = 128
INV_BASE = 16
MOE_BM = 512

COL_RX = 4 * D_DN
COL_RY = COL_RX + D_RG
COL_GB = COL_RY + D_RG
PROJ_W = COL_GB + LANES


def _cparams(n_axes=1):
    return pltpu.CompilerParams(
        dimension_semantics=("arbitrary",) * n_axes, vmem_limit_bytes=VMEM_LIMIT)


def _mm(a, b):
    return jnp.dot(a.astype(BF16), b.astype(BF16), preferred_element_type=F32)


def _mm_nt(a, b):
    return lax.dot_general(a.astype(BF16), b.astype(BF16), (((1,), (1,)), ((), ())),
                           preferred_element_type=F32)


def _rmsnorm(x, g):
    return x * lax.rsqrt(jnp.mean(x * x, axis=-1, keepdims=True) + EPS) * g


def _sigmoid(x):
    return 0.5 * jnp.tanh(0.5 * x) + 0.5


def _mm_split(a, b_hi, b_lo):
    a_hi = a.astype(BF16)
    a_lo = (a - a_hi.astype(F32)).astype(BF16)
    return (jnp.dot(a_hi, b_hi, preferred_element_type=F32)
            + jnp.dot(a_hi, b_lo, preferred_element_type=F32)
            + jnp.dot(a_lo, b_hi, preferred_element_type=F32))


def _softplus(x):
    return jnp.maximum(x, 0.0) + jnp.log1p(jnp.exp(-jnp.abs(x)))


def _load_row_tiles(ref, rows):
    rt = ref.shape[0] // rows
    return jnp.concatenate([ref[pl.ds(s, rows, stride=rt), :] for s in range(rt)], axis=1)


def _store_row_tiles(ref, val):
    rows, w = val.shape
    rt = w // LANES
    for s in range(rt):
        ref[pl.ds(s, rows, stride=rt), :] = val[:, s * LANES:(s + 1) * LANES]


def _pack_bf16_pairs(x):
    half = x.shape[1] // 2
    hi = pltpu.bitcast(x[:, :half].astype(BF16).astype(F32), U32)
    lo = pltpu.bitcast(x[:, half:].astype(BF16).astype(F32), U32)
    return hi | (lo >> 16)


def _unpack_pairs_f32(p):
    hi = pltpu.bitcast(p & jnp.uint32(0xFFFF0000), F32)
    lo = pltpu.bitcast(p << 16, F32)
    return jnp.concatenate([hi, lo], axis=1)


def _load_segmented(ref):
    seg = ref.shape[0] // SUBLANES
    return jnp.concatenate([ref[pl.ds(j, SUBLANES, stride=seg), :] for j in range(seg)], axis=0)


def _causal_conv_segmented(xp, halo_ref, cs, w):
    tb = xp.shape[0]
    ng = w.shape[0] - 1
    prev = halo_ref[:, cs]
    last = xp[tb - ng * SUBLANES:, :]
    halo_ref[:, cs] = last
    sub = lax.broadcasted_iota(I32, (SUBLANES, xp.shape[1]), 0)
    groups = []
    for g in range(ng):
        rows = slice(g * SUBLANES, (g + 1) * SUBLANES)
        groups.append(jnp.where(sub == 0, pltpu.roll(prev[rows], 1, 0),
                                pltpu.roll(last[rows], 1, 0)))
    ext = jnp.concatenate(groups + [xp], axis=0)
    y = w[ng:ng + 1] * xp
    for k in range(1, ng + 1):
        y = y + w[ng - k:ng - k + 1] * ext[(ng - k) * SUBLANES:(ng - k) * SUBLANES + tb]
    return y


def _in_proj_kernel(x_ref, g_ref, w_ref, prow_ref, pcol_ref, proj_ref, gbt_ref):
    u = _rmsnorm(x_ref[...], g_ref[...]).astype(BF16)
    for c0 in range(0, COL_GB, 512):
        proj_ref[:, c0:c0 + 512] = jnp.dot(u, w_ref[:, c0:c0 + 512], preferred_element_type=F32)
    ba = jnp.dot(u, w_ref[:, COL_GB:PROJ_W], preferred_element_type=F32)
    lane = lax.broadcasted_iota(I32, ba.shape, 1)
    g = -jnp.exp(prow_ref[0:1, :]) * _softplus(ba + prow_ref[1:2, :])
    proj_ref[:, COL_GB:PROJ_W] = jnp.where(lane < DN_HEADS, _sigmoid(ba), g)
    bat = lax.dot_general(w_ref[:, COL_GB:PROJ_W], u, (((0,), (1,)), ((), ())),
                          preferred_element_type=F32)[0:2 * DN_HEADS, :]
    row = lax.broadcasted_iota(I32, bat.shape, 0)
    gt = -jnp.exp(pcol_ref[:, 0:1]) * _softplus(bat + pcol_ref[:, 1:2])
    gbt_ref[...] = jnp.where(row < DN_HEADS, _sigmoid(bat), gt)


def _in_proj(x2, gain, w_cat, prow, pcol, tm=512):
    n, d = x2.shape
    return pl.pallas_call(
        _in_proj_kernel,
        grid=(n // tm,),
        in_specs=[
            pl.BlockSpec((tm, d), lambda i: (i, 0)),
            pl.BlockSpec((1, d), lambda i: (0, 0)),
            pl.BlockSpec((d, PROJ_W), lambda i: (0, 0)),
            pl.BlockSpec((SUBLANES, LANES), lambda i: (0, 0)),
            pl.BlockSpec((2 * DN_HEADS, LANES), lambda i: (0, 0)),
        ],
        out_specs=[
            pl.BlockSpec((tm, PROJ_W), lambda i: (i, 0)),
            pl.BlockSpec((2 * DN_HEADS, tm), lambda i: (0, i)),
        ],
        out_shape=[
            jax.ShapeDtypeStruct((n, PROJ_W), F32),
            jax.ShapeDtypeStruct((2 * DN_HEADS, n), F32),
        ],
        compiler_params=_cparams(1),
        name="in_proj",
    )(x2, gain, w_cat, prow, pcol)


def _deltanet_kernel(*refs, tb):
    n_grp = 3 * DN_HEADS
    qkv_refs = refs[:n_grp]
    (z_ref, gb_ref, gbt_ref, conv_ref, norm_ref, o_ref,
     s_ref, halo_ref, act_ref, gcc_ref, gcr_ref,
     a_ref, p_ref, d_ref, qk_ref, rhs_ref, u_ref, wq_ref, kdt_ref) = refs[n_grp:]
    t = pl.program_id(1)
    c = DN_CHUNK
    dh = DN_HEAD_DIM

    @pl.when(t == 0)
    def _():
        s_ref[...] = jnp.zeros_like(s_ref)
        halo_ref[...] = jnp.zeros_like(halo_ref)

    sub = lax.broadcasted_iota(I32, (SUBLANES, dh), 0)
    for grp in range(n_grp):
        cs = slice(grp * dh, (grp + 1) * dh)
        x = qkv_refs[grp][...]
        prev = halo_ref[:, cs]
        halo_ref[:, cs] = x[tb - SUBLANES:, :]
        y = conv_ref[CONV_WIDTH - 1:CONV_WIDTH, cs] * x
        for k in range(1, CONV_WIDTH):
            xs = pltpu.roll(x, k, 0)
            head = jnp.where(sub < k, pltpu.roll(prev, k, 0), xs[:SUBLANES])
            xs = jnp.concatenate([head, xs[SUBLANES:]], axis=0)
            y = y + conv_ref[CONV_WIDTH - 1 - k:CONV_WIDTH - k, cs] * xs
        y = y * _sigmoid(y)
        if grp < 2 * DN_HEADS:
            y = y * lax.rsqrt(jnp.sum(y * y, axis=-1, keepdims=True) + EPS)
        if grp < DN_HEADS:
            y = y * (dh ** -0.5)
        act_ref[grp] = y

    gcol = gb_ref[...]
    rpos = lax.broadcasted_iota(I32, gcol.shape, 0) & (c - 1)
    d = 1
    while d < c:
        gcol = gcol + jnp.where(rpos >= d, pltpu.roll(gcol, d, 0), 0.0)
        d *= 2
    gcc_ref[...] = gcol
    grow = gbt_ref[...]
    lpos = lax.broadcasted_iota(I32, grow.shape, 1) & (c - 1)
    d = 1
    while d < c:
        grow = grow + jnp.where(lpos >= d, pltpu.roll(grow, d, 1), 0.0)
        d *= 2
    gcr_ref[...] = grow

    row = lax.broadcasted_iota(I32, (c, c), 0)
    col = lax.broadcasted_iota(I32, (c, c), 1)
    causal = row >= col
    strict = row > col
    eye = jnp.where(row == col, 1.0, 0.0)
    gain = norm_ref[...]
    n_chunks = tb // c
    probs = [(ci, h) for ci in range(n_chunks) for h in range(DN_HEADS)]

    decay_last = []
    for p, (ci, h) in enumerate(probs):
        rows = slice(ci * c, (ci + 1) * c)
        q = act_ref[h, rows, :]
        k = act_ref[DN_HEADS + h, rows, :]
        v = act_ref[2 * DN_HEADS + h, rows, :]
        beta = gb_ref[rows, h:h + 1]
        gc = gcc_ref[rows, DN_HEADS + h:DN_HEADS + h + 1]
        gr = gcr_ref[DN_HEADS + h:DN_HEADS + h + 1, rows]
        g_last = gc[c - 1:c, :]
        decay = jnp.where(causal, jnp.exp(jnp.where(causal, gc - gr, 0.0)), 0.0)
        kb = k * beta
        both = _mm_nt(jnp.concatenate([kb, q], axis=0), k)
        a_ref[p] = jnp.where(strict, both[:c] * decay, 0.0)
        qk_ref[p] = (both[c:] * decay).astype(BF16)
        egc = jnp.exp(gc)
        rhs_ref[p] = jnp.concatenate([v * beta, kb * egc], axis=1).astype(BF16)
        wq_ref[p, c:, :] = (q * egc).astype(BF16)
        kdt_ref[p] = (k * jnp.exp(g_last - gc)).T.astype(BF16)
        decay_last.append(jnp.exp(g_last))

    shift = INV_BASE.bit_length() - 1
    blk = (row >> shift) == (col >> shift)
    for p in range(len(probs)):
        diag = jnp.where(blk, a_ref[p], 0.0)
        p_ref[p] = eye - diag
        d_ref[p] = _mm(diag, diag).astype(BF16)
    for it in range(shift - 1):
        for p in range(len(probs)):
            pw = d_ref[p]
            inv = p_ref[p]
            p_ref[p] = inv + _mm(inv, pw)
            if it < shift - 2:
                d_ref[p] = _mm(pw, pw).astype(BF16)
    s = INV_BASE
    while s < c:
        sh = s.bit_length() - 1
        off = ((row >> (sh + 1)) == (col >> (sh + 1))) & ((row >> sh) != (col >> sh))
        for p in range(len(probs)):
            d_ref[p] = _mm(p_ref[p], jnp.where(off, a_ref[p], 0.0)).astype(BF16)
        for p in range(len(probs)):
            inv = p_ref[p]
            p_ref[p] = inv - _mm(d_ref[p], inv)
        s *= 2
    for p in range(len(probs)):
        uw = _mm(p_ref[p], rhs_ref[p])
        u_ref[p] = uw[:, :dh]
        wq_ref[p, :c, :] = uw[:, dh:].astype(BF16)

    for ci in range(n_chunks):
        rows = slice(ci * c, (ci + 1) * c)
        ps = [ci * DN_HEADS + h for h in range(DN_HEADS)]
        s_old = [s_ref[h] for h in range(DN_HEADS)]
        ws = [_mm(wq_ref[p], s_old[h]) for h, p in enumerate(ps)]
        v_new = [u_ref[p] - ws[h][:c] for h, p in enumerate(ps)]
        outs = [ws[h][c:] + _mm(qk_ref[p], v_new[h]) for h, p in enumerate(ps)]
        for h, p in enumerate(ps):
            s_ref[h] = s_old[h] * decay_last[p] + _mm(kdt_ref[p], v_new[h])
        for h in range(DN_HEADS):
            o = _rmsnorm(outs[h], gain)
            zz = z_ref[rows, h * dh:(h + 1) * dh]
            o_ref[rows, h * dh:(h + 1) * dh] = (o * (zz * _sigmoid(zz))).astype(o_ref.dtype)


def _deltanet(proj, gbt, dn_conv, dn_norm, batch, seq, tb=512):
    n = proj.shape[0]
    nt = seq // tb
    c = DN_CHUNK
    n_prob = (tb // c) * DN_HEADS
    n_grp = 3 * DN_HEADS
    grp = lambda j: pl.BlockSpec((tb, DN_HEAD_DIM), lambda b, t, j=j: (b * nt + t, j))
    return pl.pallas_call(
        functools.partial(_deltanet_kernel, tb=tb),
        grid=(batch, nt),
        in_specs=[grp(j) for j in range(n_grp)] + [
            pl.BlockSpec((tb, D_DN), lambda b, t: (b * nt + t, 3)),
            pl.BlockSpec((tb, LANES), lambda b, t: (b * nt + t, COL_GB // LANES)),
            pl.BlockSpec((2 * DN_HEADS, tb), lambda b, t: (0, b * nt + t)),
            pl.BlockSpec((CONV_WIDTH, 3 * D_DN), lambda b, t: (0, 0)),
            pl.BlockSpec((1, DN_HEAD_DIM), lambda b, t: (0, 0)),
        ],
        out_specs=pl.BlockSpec((tb, D_DN), lambda b, t: (b * nt + t, 0)),
        out_shape=jax.ShapeDtypeStruct((n, D_DN), BF16),
        scratch_shapes=[
            pltpu.VMEM((DN_HEADS, DN_HEAD_DIM, DN_HEAD_DIM), F32),
            pltpu.VMEM((SUBLANES, 3 * D_DN), F32),
            pltpu.VMEM((n_grp, tb, DN_HEAD_DIM), F32),
            pltpu.VMEM((tb, LANES), F32),
            pltpu.VMEM((2 * DN_HEADS, tb), F32),
            pltpu.VMEM((n_prob, c, c), F32),
            pltpu.VMEM((n_prob, c, c), F32),
            pltpu.VMEM((n_prob, c, c), BF16),
            pltpu.VMEM((n_prob, c, c), BF16),
            pltpu.VMEM((n_prob, c, 2 * DN_HEAD_DIM), BF16),
            pltpu.VMEM((n_prob, c, DN_HEAD_DIM), F32),
            pltpu.VMEM((n_prob, 2 * c, DN_HEAD_DIM), BF16),
            pltpu.VMEM((n_prob, DN_HEAD_DIM, c), BF16),
        ],
        compiler_params=_cparams(2),
        name="deltanet",
    )(*([proj] * (n_grp + 2)), gbt, dn_conv, dn_norm)


def _gelu_tanh(x):
    return 0.5 * x * (1.0 + jnp.tanh(0.7978845608028654 * (x + 0.044715 * (x * x * x))))


def _rglru_kernel(*refs, tb):
    rx_refs = refs[:RG_BLOCKS]
    (ry_ref, conv_ref, convb_ref, wa_ref, ba_ref, wx_ref, bx_ref, lam_ref,
     o_ref, hc_ref, halo_ref, a_ref, b_ref, h_ref) = refs[RG_BLOCKS:]
    t = pl.program_id(1)
    bw = D_RG // RG_BLOCKS
    seg = tb // SUBLANES

    @pl.when(t == 0)
    def _():
        hc_ref[...] = jnp.zeros_like(hc_ref)
        halo_ref[...] = jnp.zeros_like(halo_ref)

    log_sig = -_softplus(-lam_ref[...])
    rowi = lax.broadcasted_iota(I32, (tb, bw), 0)
    seq_start_row = jnp.where(t == 0, 0, -1)
    for nb in range(RG_BLOCKS):
        cs = slice(nb * bw, (nb + 1) * bw)
        xb = (_causal_conv_segmented(_load_segmented(rx_refs[nb]), halo_ref, cs,
                                     conv_ref[:, cs]) + convb_ref[:, cs])
        r = _sigmoid(_mm(xb, wa_ref[nb]) + ba_ref[:, cs])
        gi = _sigmoid(_mm(xb, wx_ref[nb]) + bx_ref[:, cs])
        log_a = RG_C * r * log_sig[:, cs]
        a = jnp.exp(log_a)
        y = jnp.tanh(-log_a) * (1.0 + a * a)
        mult = jnp.where(y > 0.0, y * lax.rsqrt(y), 0.0)
        mult = jnp.where(rowi == seq_start_row, 1.0, mult)
        a_ref[:, cs] = a
        b_ref[:, cs] = mult * (gi * xb)

    def scan(j, carry):
        ac, bc = carry
        rows = pl.ds(pl.multiple_of(j * SUBLANES, SUBLANES), SUBLANES)
        aj = a_ref[rows, :]
        bc = aj * bc + b_ref[rows, :]
        ac = aj * ac
        a_ref[rows, :] = ac
        b_ref[rows, :] = bc
        return ac, bc

    ac, bc = lax.fori_loop(1, seg, scan, (a_ref[0:SUBLANES, :], b_ref[0:SUBLANES, :]), unroll=7)

    h = hc_ref[...]
    h_in = []
    for s in range(SUBLANES):
        h_in.append(h)
        h = ac[s:s + 1, :] * h + bc[s:s + 1, :]
    hc_ref[...] = h
    h_in = jnp.concatenate(h_in, axis=0)

    for j in range(seg):
        rows = slice(j * SUBLANES, (j + 1) * SUBLANES)
        hj = a_ref[rows, :] * h_in + b_ref[rows, :]
        for nb in range(RG_BLOCKS):
            h_ref[nb, rows, :] = hj[:, nb * bw:(nb + 1) * bw]
    g = seg // SUBLANES
    for nb in range(RG_BLOCKS):
        cs = slice(nb * bw, (nb + 1) * bw)
        h = jnp.concatenate(
            [h_ref[nb, pl.ds((i % g) * SUBLANES * SUBLANES + i // g, SUBLANES, stride=SUBLANES), :]
             for i in range(seg)], axis=0)
        o_ref[:, cs] = (h * _gelu_tanh(ry_ref[:, cs])).astype(o_ref.dtype)


def _rglru(proj, rg_conv, rg_conv_b, w_a, b_a, w_x, b_x, lam, batch, seq, tb=512):
    n = proj.shape[0]
    nt = seq // tb
    bw = D_RG // RG_BLOCKS
    full = lambda shape: pl.BlockSpec(shape, lambda b, t: (0,) * len(shape))
    return pl.pallas_call(
        functools.partial(_rglru_kernel, tb=tb),
        grid=(batch, nt),
        in_specs=[pl.BlockSpec((tb, bw), lambda b, t, j=j: (b * nt + t, COL_RX // bw + j))
                  for j in range(RG_BLOCKS)] + [
            pl.BlockSpec((tb, D_RG), lambda b, t: (b * nt + t, COL_RY // D_RG)),
            full((CONV_WIDTH, D_RG)), full((1, D_RG)),
            full((RG_BLOCKS, bw, bw)), full((1, D_RG)),
            full((RG_BLOCKS, bw, bw)), full((1, D_RG)),
            full((1, D_RG)),
        ],
        out_specs=pl.BlockSpec((tb, D_RG), lambda b, t: (b * nt + t, 0)),
        out_shape=jax.ShapeDtypeStruct((n, D_RG), BF16),
        scratch_shapes=[
            pltpu.VMEM((1, D_RG), F32),
            pltpu.VMEM(((CONV_WIDTH - 1) * SUBLANES, D_RG), F32),
            pltpu.VMEM((tb, D_RG), F32),
            pltpu.VMEM((tb, D_RG), F32),
            pltpu.VMEM((RG_BLOCKS, tb, bw), F32),
        ],
        compiler_params=_cparams(2),
        name="rglru",
    )(*([proj] * (RG_BLOCKS + 1)), rg_conv, rg_conv_b, w_a, b_a, w_x, b_x, lam)


def _mem_kv_kernel(m_ref, g_ref, w_ref, o_ref):
    mn = _rmsnorm(m_ref[...], g_ref[...]).astype(BF16)
    for c0 in range(0, o_ref.shape[1], 512):
        o_ref[:, c0:c0 + 512] = jnp.dot(mn, w_ref[:, c0:c0 + 512].astype(BF16),
                                        preferred_element_type=F32).astype(o_ref.dtype)


def _mem_kv(mem2, gain, w_ckv):
    n, d = mem2.shape
    return pl.pallas_call(
        _mem_kv_kernel,
        grid=(1,),
        in_specs=[
            pl.BlockSpec((n, d), lambda i: (0, 0)),
            pl.BlockSpec((1, d), lambda i: (0, 0)),
            pl.BlockSpec((d, 2 * d), lambda i: (0, 0)),
        ],
        out_specs=pl.BlockSpec((n, 2 * d), lambda i: (0, 0)),
        out_shape=jax.ShapeDtypeStruct((n, 2 * d), BF16),
        compiler_params=_cparams(1),
        name="mem_kv",
    )(mem2, gain, w_ckv)


ROUTE_E = 0
ROUTE_RANK = TOP_K
ROUTE_GATE = 2 * TOP_K
ROUTE_ROWS = 16


def _post_mix_kernel(dn_ref, rg_ref, x_ref, wo_f32, gx_ref, wq_f32, kv_ref, wco_f32, gm_ref,
                     wrh_ref, wrl_ref, br_ref, h2_ref, xn_ref, route_ref, routet_ref, cnt_ref,
                     carry_ref, wo_ref, wq_ref, wco_ref):
    i = pl.program_id(0)
    tm, d = x_ref.shape
    hd = d // XA_HEADS

    @pl.when(i == 0)
    def _():
        carry_ref[...] = jnp.zeros_like(carry_ref)
        wo_ref[...] = wo_f32[...].astype(BF16)
        wq_ref[...] = wq_f32[...].astype(BF16)
        wco_ref[...] = wco_f32[...].astype(BF16)

    h1 = (x_ref[...] + jnp.dot(dn_ref[...], wo_ref[0:D_DN, :], preferred_element_type=F32)
          + jnp.dot(rg_ref[...], wo_ref[D_DN:, :], preferred_element_type=F32))

    hn = _rmsnorm(h1, gx_ref[...]).astype(BF16)
    q = jnp.dot(hn, wq_ref[...], preferred_element_type=F32)
    heads = []
    for hh in range(XA_HEADS):
        cs = slice(hh * hd, (hh + 1) * hd)
        s = _mm_nt(q[:, cs], kv_ref[:, cs]) * (hd ** -0.5)
        p = jnp.exp(s - jnp.max(s, axis=-1, keepdims=True))
        p = p / jnp.sum(p, axis=-1, keepdims=True)
        heads.append(_mm(p, kv_ref[:, d + hh * hd:d + (hh + 1) * hd]).astype(BF16))
    o = jnp.concatenate(heads, axis=1)
    h2 = h1 + jnp.dot(o, wco_ref[...], preferred_element_type=F32)
    h2_ref[...] = h2

    xn = _rmsnorm(h2, gm_ref[...])
    _store_row_tiles(xn_ref, _pack_bf16_pairs(xn))
    logits = _mm_split(xn, wrh_ref[...], wrl_ref[...]) + br_ref[...]
    lg = logits.T[0:N_EXPERTS, :]
    eidx = lax.broadcasted_iota(I32, lg.shape, 0).astype(F32)
    neg = jnp.float32(-jnp.inf)
    vals, idxs, hots = [], [], []
    for _ in range(TOP_K):
        m = jnp.max(lg, axis=0, keepdims=True)
        idx = jnp.min(jnp.where(lg == m, eidx, float(N_EXPERTS)), axis=0, keepdims=True)
        hot = eidx == idx
        lg = jnp.where(hot, neg, lg)
        vals.append(m)
        idxs.append(idx)
        hots.append(hot)
    es = [jnp.exp(v - vals[0]) for v in vals]
    den = es[0] + es[1] + es[2] + es[3]
    gates = [e / den for e in es]

    chosen = jnp.zeros(lg.shape, F32)
    for hot in hots:
        chosen = chosen + jnp.where(hot, 1.0, 0.0)
    r2 = lax.broadcasted_iota(I32, (tm, tm), 0)
    c2 = lax.broadcasted_iota(I32, (tm, tm), 1)
    before = _mm(chosen, jnp.where(r2 < c2, 1.0, 0.0)) + carry_ref[:, 0:1]
    ranks = [jnp.sum(jnp.where(hot, before, 0.0), axis=0, keepdims=True) for hot in hots]
    carry_ref[...] = carry_ref[...] + jnp.sum(chosen, axis=1, keepdims=True)
    cnt_ref[...] = carry_ref[...]

    row = lax.broadcasted_iota(I32, (ROUTE_ROWS, tm), 0)
    rect = jnp.zeros((ROUTE_ROWS, tm), F32)
    for kk in range(TOP_K):
        rect = jnp.where(row == ROUTE_E + kk, idxs[kk], rect)
        rect = jnp.where(row == ROUTE_RANK + kk, ranks[kk], rect)
        rect = jnp.where(row == ROUTE_GATE + kk, gates[kk], rect)
    routet_ref[...] = rect
    route_ref[...] = jnp.concatenate(
        [rect, jnp.zeros((LANES - ROUTE_ROWS, tm), F32)], axis=0).T


def _post_mix(dn, rg, x2, w_out, g_cross, w_cq, kv, w_co, g_moe, w_r_hi, w_r_lo, b_r, seq, mem_len,
              tm=512):
    n, d = x2.shape
    per_b = seq // tm
    rt_x = d // 2 // LANES
    full = lambda shape: pl.BlockSpec(shape, lambda i: (0,) * len(shape))
    once = lambda shape: pl.BlockSpec(shape, lambda i: (0,) * len(shape),
                                      pipeline_mode=pl.Buffered(1))
    return pl.pallas_call(
        _post_mix_kernel,
        grid=(n // tm,),
        in_specs=[
            pl.BlockSpec((tm, D_DN), lambda i: (i, 0)),
            pl.BlockSpec((tm, D_RG), lambda i: (i, 0)),
            pl.BlockSpec((tm, d), lambda i: (i, 0)),
            once((d, d)), full((1, d)), once((d, d)),
            pl.BlockSpec((mem_len, 2 * d), lambda i: (i // per_b, 0)),
            once((d, d)), full((1, d)),
            full((d, LANES)), full((d, LANES)), full((1, LANES)),
        ],
        out_specs=[
            pl.BlockSpec((tm, d), lambda i: (i, 0)),
            pl.BlockSpec((tm * rt_x, LANES), lambda i: (i, 0)),
            pl.BlockSpec((tm, LANES), lambda i: (i, 0)),
            pl.BlockSpec((ROUTE_ROWS, tm), lambda i: (0, i)),
            pl.BlockSpec((N_EXPERTS, LANES), lambda i: (0, 0)),
        ],
        out_shape=[
            jax.ShapeDtypeStruct((n, d), F32),
            jax.ShapeDtypeStruct((n * rt_x, LANES), U32),
            jax.ShapeDtypeStruct((n, LANES), F32),
            jax.ShapeDtypeStruct((ROUTE_ROWS, n), F32),
            jax.ShapeDtypeStruct((N_EXPERTS, LANES), F32),
        ],
        scratch_shapes=[pltpu.VMEM((N_EXPERTS, LANES), F32),
                        pltpu.VMEM((d, d), BF16), pltpu.VMEM((d, d), BF16), pltpu.VMEM((d, d), BF16)],
        compiler_params=_cparams(1),
        name="post_mix",
    )(dn, rg, x2, w_out, g_cross, w_cq, kv, w_co, g_moe, w_r_hi, w_r_lo, b_r)


def _dest_kernel(start_ref, routet_ref, dest_ref):
    e = routet_ref[ROUTE_E:ROUTE_E + SUBLANES, :]
    rank = routet_ref[ROUTE_RANK:ROUTE_RANK + SUBLANES, :]
    base = jnp.zeros(e.shape, F32)
    for j in range(N_EXPERTS):
        base = jnp.where(e == float(j), start_ref[j].astype(F32), base)
    row = lax.broadcasted_iota(I32, e.shape, 0)
    dest_ref[...] = jnp.where(row < TOP_K, (base + rank).astype(I32), 0)


def _dest(pad_start, route_t, tm=2048):
    n = route_t.shape[1]
    grid_spec = pltpu.PrefetchScalarGridSpec(
        num_scalar_prefetch=1,
        grid=(n // tm,),
        in_specs=[pl.BlockSpec((ROUTE_ROWS, tm), lambda i, st: (0, i))],
        out_specs=pl.BlockSpec((SUBLANES, tm), lambda i, st: (0, i)),
    )
    return pl.pallas_call(
        _dest_kernel,
        grid_spec=grid_spec,
        out_shape=jax.ShapeDtypeStruct((SUBLANES, n), I32),
        compiler_params=_cparams(1),
        name="dest",
    )(pad_start, route_t)


def _row_tile(ref, r, rt):
    return ref.at[pl.ds(pl.multiple_of(r * rt, rt), rt)]


def _dispatch_kernel(meta_ref, dest_ref, xn_ref, xs_ref, zero_ref, ring_ref, sem, zsem,
                     *, tm, rt, n_blocks):
    i = pl.program_id(0)
    bm = MOE_BM * rt
    ring = ring_ref.at[i & 1]
    ring[...] = xn_ref[...]

    def per_token(tk, carry):
        src = _row_tile(ring, tk, rt)
        for kk in range(TOP_K):
            r = dest_ref[kk, tk]
            pltpu.make_async_copy(src, _row_tile(xs_ref, r, rt),
                                  sem.at[i & 1]).start(priority=kk % 2)
        return carry

    lax.fori_loop(0, tm, per_token, 0, unroll=4)

    @pl.when(i == 0)
    def _():
        zero_ref[...] = jnp.zeros_like(zero_ref)

        def pad_copy(r):
            return pltpu.make_async_copy(zero_ref.at[pl.ds(0, rt)], _row_tile(xs_ref, r, rt), zsem)

        def per_expert(e, total):
            r0 = meta_ref[e]
            cnt = meta_ref[N_EXPERTS + e]

            def one(r, carry):
                pad_copy(r0 + r).start()
                return carry

            lax.fori_loop(0, cnt, one, 0)
            return total + cnt

        total = lax.fori_loop(0, N_EXPERTS, per_expert, 0)

        def drain(r, carry):
            pad_copy(0).wait()
            return carry

        lax.fori_loop(0, total, drain, 0)

        def tail_copy(b):
            return pltpu.make_async_copy(
                zero_ref, xs_ref.at[pl.ds(pl.multiple_of(b * bm, bm), bm)], zsem)

        n_used = meta_ref[2 * N_EXPERTS]

        def tail(b, carry):
            tail_copy(b).start()
            return carry

        lax.fori_loop(n_used, n_blocks, tail, 0)

        def tail_drain(b, carry):
            tail_copy(0).wait()
            return carry

        lax.fori_loop(n_used, n_blocks, tail_drain, 0)

    def wait_step(parity):
        for _ in range(TOP_K):
            pltpu.make_async_copy(ring_ref.at[parity], xs_ref.at[pl.ds(0, tm * rt)],
                                  sem.at[parity]).wait()

    @pl.when(i > 0)
    def _():
        wait_step(1 - (i & 1))

    @pl.when(i == pl.num_programs(0) - 1)
    def _():
        wait_step(i & 1)


def _dispatch(meta, dest, xn_t, n_tok, n_rows, tm=256):
    rt = xn_t.shape[0] // n_tok
    n_blocks = n_rows // MOE_BM
    grid_spec = pltpu.PrefetchScalarGridSpec(
        num_scalar_prefetch=1,
        grid=(n_tok // tm,),
        in_specs=[
            pl.BlockSpec((SUBLANES, tm), lambda i, meta: (0, i), memory_space=pltpu.SMEM),
            pl.BlockSpec((tm * rt, LANES), lambda i, meta: (i, 0)),
        ],
        out_specs=pl.BlockSpec(memory_space=pl.ANY),
        scratch_shapes=[
            pltpu.VMEM((MOE_BM * rt, LANES), xn_t.dtype),
            pltpu.VMEM((2, tm * rt, LANES), xn_t.dtype),
            pltpu.SemaphoreType.DMA((2,)),
            pltpu.SemaphoreType.DMA,
        ],
    )
    return pl.pallas_call(
        functools.partial(_dispatch_kernel, tm=tm, rt=rt, n_blocks=n_blocks),
        grid_spec=grid_spec,
        out_shape=jax.ShapeDtypeStruct((n_rows * rt, LANES), xn_t.dtype),
        compiler_params=_cparams(1),
        name="dispatch",
    )(meta, dest, xn_t)


def _experts_kernel(be_ref, nu_ref, nxt_ref, par_ref, x_ref, bg_ref, bu_ref, bd_ref,
                    wg_hbm, wu_hbm, wd_hbm, y_ref, wf_ref, wb_ref, sem):
    i = pl.program_id(0)
    used = i < nu_ref[0]
    e = be_ref[i]
    changed = (i == 0) | (e != be_ref[jnp.maximum(i - 1, 0)])

    def weight_copies(expert, slot):
        return [pltpu.make_async_copy(w.at[0, expert], wf_ref.at[slot, j], sem.at[slot])
                for j, w in enumerate((wg_hbm, wu_hbm, wd_hbm))]

    @pl.when(i == 0)
    def _():
        for cp in weight_copies(e, par_ref[e]):
            cp.start()

    @pl.when(used & changed)
    def _():
        slot = par_ref[e]
        for cp in weight_copies(e, slot):
            cp.wait()
        nxt = nxt_ref[e]

        @pl.when(nxt < N_EXPERTS)
        def _():
            for cp in weight_copies(nxt, 1 - slot):
                cp.start()

        for j in range(3):
            wb_ref[j] = wf_ref[slot, j].astype(BF16)

    @pl.when(used)
    def _():
        x = _unpack_pairs_f32(_load_row_tiles(x_ref, MOE_BM)).astype(BF16)
        gt = jnp.minimum(jnp.dot(x, wb_ref[0], preferred_element_type=F32) + bg_ref[...],
                         SWIGLU_LIMIT)
        up = jnp.clip(jnp.dot(x, wb_ref[1], preferred_element_type=F32) + bu_ref[...],
                      -SWIGLU_LIMIT, SWIGLU_LIMIT)
        hid = (up + 1.0) * (gt * _sigmoid(SWIGLU_ALPHA * gt))
        y = jnp.dot(hid.astype(BF16), wb_ref[2], preferred_element_type=F32) + bd_ref[...]
        _store_row_tiles(y_ref, _pack_bf16_pairs(y))

    @pl.when(jnp.logical_not(used))
    def _():
        y_ref[...] = jnp.zeros_like(y_ref)


def _experts(block_e, n_used, next_e, parity, xs_t, w_gate, b_gate, w_up, b_up, w_down, b_down):
    d, d_ff = w_gate.shape[2:]
    assert d == d_ff, "weight staging buffers assume square expert matrices"
    rt = d // 2 // LANES
    n_blocks = xs_t.shape[0] // (MOE_BM * rt)
    bspec = lambda m: pl.BlockSpec((None, 1, m), lambda i, be, nu, nx, pa: (be[i], 0, 0))
    hbm = pl.BlockSpec(memory_space=pl.ANY)
    grid_spec = pltpu.PrefetchScalarGridSpec(
        num_scalar_prefetch=4,
        grid=(n_blocks,),
        in_specs=[
            pl.BlockSpec((MOE_BM * rt, LANES),
                         lambda i, be, nu, nx, pa: (jnp.maximum(jnp.minimum(i, nu[0] - 1), 0), 0)),
            bspec(d_ff), bspec(d_ff), bspec(d),
            hbm, hbm, hbm,
        ],
        out_specs=pl.BlockSpec((MOE_BM * rt, LANES), lambda i, be, nu, nx, pa: (i, 0)),
        scratch_shapes=[
            pltpu.VMEM((2, 3, d, d_ff), F32),
            pltpu.VMEM((3, d, d_ff), BF16),
            pltpu.SemaphoreType.DMA((2,)),
        ],
    )
    return pl.pallas_call(
        _experts_kernel,
        grid_spec=grid_spec,
        out_shape=jax.ShapeDtypeStruct(xs_t.shape, U32),
        compiler_params=_cparams(1),
        name="experts",
    )(block_e, n_used, next_e, parity, xs_t, b_gate, b_up, b_down, w_gate, w_up, w_down)


def _combine_kernel(dcur_ref, dnext_ref, h2_ref, route_ref, gain_ref, ys_ref, o_ref,
                    buf_ref, sem, *, tm, rt):
    i = pl.program_id(0)
    nsteps = pl.num_programs(0)
    slot = i & 1

    def issue_all(dref, s):
        def per_token(tk, carry):
            for kk in range(TOP_K):
                r = dref[kk, tk]
                pltpu.make_async_copy(_row_tile(ys_ref, r, rt),
                                      _row_tile(buf_ref.at[s, kk], tk, rt),
                                      sem.at[s]).start(priority=kk % 2)
            return carry
        lax.fori_loop(0, tm, per_token, 0, unroll=4)

    @pl.when(i == 0)
    def _():
        issue_all(dcur_ref, 0)

    @pl.when(i + 1 < nsteps)
    def _():
        issue_all(dnext_ref, 1 - slot)

    for kk in range(TOP_K):
        pltpu.make_async_copy(ys_ref.at[pl.ds(0, tm * rt)], buf_ref.at[slot, kk],
                              sem.at[slot]).wait()

    rec = route_ref[...]
    acc = h2_ref[...]
    for kk in range(TOP_K):
        acc = acc + (rec[:, ROUTE_GATE + kk:ROUTE_GATE + kk + 1]
                     * _unpack_pairs_f32(_load_row_tiles(buf_ref.at[slot, kk], tm)))
    o_ref[...] = _rmsnorm(acc, gain_ref[...])


def _combine(dest, h2, route, gain, ys_t, tm=256):
    n, d = h2.shape
    nsteps = n // tm
    rt = d // 2 // LANES
    return pl.pallas_call(
        functools.partial(_combine_kernel, tm=tm, rt=rt),
        grid=(nsteps,),
        in_specs=[
            pl.BlockSpec((SUBLANES, tm), lambda i: (0, i), memory_space=pltpu.SMEM),
            pl.BlockSpec((SUBLANES, tm), lambda i: (0, jnp.minimum(i + 1, nsteps - 1)),
                         memory_space=pltpu.SMEM),
            pl.BlockSpec((tm, d), lambda i: (i, 0)),
            pl.BlockSpec((tm, LANES), lambda i: (i, 0)),
            pl.BlockSpec((1, d), lambda i: (0, 0)),
            pl.BlockSpec(memory_space=pl.ANY),
        ],
        out_specs=pl.BlockSpec((tm, d), lambda i: (i, 0)),
        out_shape=jax.ShapeDtypeStruct((n, d), F32),
        scratch_shapes=[
            pltpu.VMEM((2, TOP_K, tm * rt, LANES), U32),
            pltpu.SemaphoreType.DMA((2,)),
        ],
        compiler_params=_cparams(1),
        name="combine",
    )(dest, dest, h2, route, gain, ys_t)


def kernel(x, mem, norm_mix, w_in, dn_conv, dn_a_log, dn_dt_bias, dn_norm, rg_conv, rg_conv_b, rg_w_a, rg_b_a, rg_w_x, rg_b_x, rg_lambda, w_out, norm_cross, norm_mem, w_cq, w_ckv, w_co, norm_moe, w_router, b_router, w_gate, b_gate, w_up, b_up, w_down, b_down, norm_final):
    batch, seq, d = x.shape
    mem_len = mem.shape[1]
    n = batch * seq
    assert w_in.shape[0] == 1, "single-layer trunk"
    x2 = x.reshape(n, d)

    wi = w_in[0]
    n_gate = 2 * DN_HEADS
    w_cat = jnp.concatenate(
        [wi[:, :4 * D_DN], wi[:, 4 * D_DN + n_gate:],
         jnp.pad(wi[:, 4 * D_DN:4 * D_DN + n_gate], ((0, 0), (0, LANES - n_gate)))],
        axis=1).astype(BF16)
    prow = (jnp.zeros((SUBLANES, LANES), F32)
            .at[0, DN_HEADS:n_gate].set(dn_a_log[0]).at[1, DN_HEADS:n_gate].set(dn_dt_bias[0]))
    pcol = (jnp.zeros((n_gate, LANES), F32)
            .at[DN_HEADS:, 0].set(dn_a_log[0]).at[DN_HEADS:, 1].set(dn_dt_bias[0]))

    proj, gbt = _in_proj(x2, norm_mix, w_cat, prow, pcol)
    dn = _deltanet(proj, gbt, dn_conv[0], dn_norm, batch, seq)
    rg = _rglru(proj, rg_conv[0], rg_conv_b, rg_w_a[0], rg_b_a[0].reshape(1, D_RG),
                rg_w_x[0], rg_b_x[0].reshape(1, D_RG), rg_lambda, batch, seq)
    kv = _mem_kv(mem.reshape(batch * mem_len, d), norm_mem, w_ckv[0])

    w_r = jnp.pad(w_router[0], ((0, 0), (0, LANES - N_EXPERTS)))
    w_r_hi = w_r.astype(BF16)
    w_r_lo = (w_r - w_r_hi.astype(F32)).astype(BF16)
    b_r = jnp.pad(b_router, ((0, 0), (0, LANES - N_EXPERTS)))
    h2, xn, route, route_t, counts = _post_mix(
        dn, rg, x2, w_out[0], norm_cross, w_cq[0], kv, w_co[0], norm_moe, w_r_hi, w_r_lo, b_r,
        seq, mem_len)

    n_blocks = n * TOP_K // MOE_BM + N_EXPERTS
    n_rows = n_blocks * MOE_BM
    cnt = counts[:, 0].astype(I32)
    padded = (cnt + MOE_BM - 1) // MOE_BM * MOE_BM
    pad_end = jnp.cumsum(padded)
    pad_start = pad_end - padded
    n_used = (pad_end[-1:] // MOE_BM).astype(I32)
    block_e = jnp.minimum(
        jnp.sum(pad_end[None, :] <= (jnp.arange(n_blocks, dtype=I32) * MOE_BM)[:, None], axis=1),
        N_EXPERTS - 1).astype(I32)
    meta = jnp.concatenate([pad_start + cnt, padded - cnt, n_used]).astype(I32)

    dest = _dest(pad_start.astype(I32), route_t)
    xs = _dispatch(meta, dest, xn, n, n_rows)
    has = cnt > 0
    eid = jnp.where(has, jnp.arange(N_EXPERTS, dtype=I32), N_EXPERTS)
    after = lax.cummin(eid, axis=0, reverse=True)
    next_e = jnp.concatenate([after[1:], jnp.full((1,), N_EXPERTS, I32)]).astype(I32)
    parity = ((jnp.cumsum(has.astype(I32)) - 1) & 1).astype(I32)
    ys = _experts(block_e, n_used, next_e, parity, xs, w_gate, b_gate[0][:, None, :], w_up,
                  b_up[0][:, None, :], w_down, b_down[0][:, None, :])
    out = _combine(dest, h2, route, norm_final.reshape(1, d), ys)
    return out.reshape(batch, seq, d)
```

```python
import functools

import jax
import jax.numpy as jnp
from jax import lax
from jax.experimental import pallas as pl
from jax.experimental.pallas import tpu as pltpu

F32 = jnp.float32
BF16 = jnp.bfloat16
I32 = jnp.int32
U32 = jnp.uint32

EPS = 1e-6
LANES = 128
SUBLANES = 8
VMEM_LIMIT = 48 * 1024 * 1024

DN_HEADS = 4
DN_HEAD_DIM = 128
D_DN = DN_HEADS * DN_HEAD_DIM
D_RG = 512
RG_BLOCKS = 4
RG_C = 8.0
CONV_WIDTH = 4
XA_HEADS = 4
N_EXPERTS = 32
TOP_K = 4
SWIGLU_LIMIT = 7.0
SWIGLU_ALPHA = 1.702

DN_CHUNK = 128
INV_BASE = 16
MOE_BM = 512
PAD_UNROLL = 8

COL_RX = 4 * D_DN
COL_RY = COL_RX + D_RG
COL_GB = COL_RY + D_RG
PROJ_W = COL_GB + LANES


def _cparams(n_axes=1):
    return pltpu.CompilerParams(
        dimension_semantics=("arbitrary",) * n_axes, vmem_limit_bytes=VMEM_LIMIT)


def _mm(a, b):
    return jnp.dot(a.astype(BF16), b.astype(BF16), preferred_element_type=F32)


def _mm_nt(a, b):
    return lax.dot_general(a.astype(BF16), b.astype(BF16), (((1,), (1,)), ((), ())),
                           preferred_element_type=F32)


def _rmsnorm(x, g):
    return x * lax.rsqrt(jnp.mean(x * x, axis=-1, keepdims=True) + EPS) * g


def _sigmoid(x):
    return 0.5 * jnp.tanh(0.5 * x) + 0.5


def _mm_split(a, b_hi, b_lo):
    a_hi = a.astype(BF16)
    a_lo = (a - a_hi.astype(F32)).astype(BF16)
    return (jnp.dot(a_hi, b_hi, preferred_element_type=F32)
            + jnp.dot(a_hi, b_lo, preferred_element_type=F32)
            + jnp.dot(a_lo, b_hi, preferred_element_type=F32))


def _softplus(x):
    return jnp.maximum(x, 0.0) + jnp.log1p(jnp.exp(-jnp.abs(x)))


def _load_row_tiles(ref, rows):
    rt = ref.shape[0] // rows
    return jnp.concatenate([ref[pl.ds(s, rows, stride=rt), :] for s in range(rt)], axis=1)


def _store_row_tiles(ref, val):
    rows, w = val.shape
    rt = w // LANES
    for s in range(rt):
        ref[pl.ds(s, rows, stride=rt), :] = val[:, s * LANES:(s + 1) * LANES]


def _pack_bf16_pairs(x):
    half = x.shape[1] // 2
    hi = pltpu.bitcast(x[:, :half].astype(BF16).astype(F32), U32)
    lo = pltpu.bitcast(x[:, half:].astype(BF16).astype(F32), U32)
    return hi | (lo >> 16)


def _unpack_pairs_f32(p):
    hi = pltpu.bitcast(p & jnp.uint32(0xFFFF0000), F32)
    lo = pltpu.bitcast(p << 16, F32)
    return jnp.concatenate([hi, lo], axis=1)


def _load_segmented(ref):
    seg = ref.shape[0] // SUBLANES
    return jnp.concatenate([ref[pl.ds(j, SUBLANES, stride=seg), :] for j in range(seg)], axis=0)


def _causal_conv_segmented(xp, halo_ref, cs, w):
    tb = xp.shape[0]
    ng = w.shape[0] - 1
    prev = halo_ref[:, cs]
    last = xp[tb - ng * SUBLANES:, :]
    halo_ref[:, cs] = last
    sub = lax.broadcasted_iota(I32, (SUBLANES, xp.shape[1]), 0)
    groups = []
    for g in range(ng):
        rows = slice(g * SUBLANES, (g + 1) * SUBLANES)
        groups.append(jnp.where(sub == 0, pltpu.roll(prev[rows], 1, 0),
                                pltpu.roll(last[rows], 1, 0)))
    ext = jnp.concatenate(groups + [xp], axis=0)
    y = w[ng:ng + 1] * xp
    for k in range(1, ng + 1):
        y = y + w[ng - k:ng - k + 1] * ext[(ng - k) * SUBLANES:(ng - k) * SUBLANES + tb]
    return y


def _in_proj_kernel(x_ref, g_ref, w_ref, prow_ref, pcol_ref, proj_ref, gbt_ref):
    u = _rmsnorm(x_ref[...], g_ref[...]).astype(BF16)
    for c0 in range(0, COL_GB, 512):
        proj_ref[:, c0:c0 + 512] = jnp.dot(u, w_ref[:, c0:c0 + 512], preferred_element_type=F32)
    ba = jnp.dot(u, w_ref[:, COL_GB:PROJ_W], preferred_element_type=F32)
    lane = lax.broadcasted_iota(I32, ba.shape, 1)
    g = -jnp.exp(prow_ref[0:1, :]) * _softplus(ba + prow_ref[1:2, :])
    proj_ref[:, COL_GB:PROJ_W] = jnp.where(lane < DN_HEADS, _sigmoid(ba), g)
    bat = lax.dot_general(w_ref[:, COL_GB:PROJ_W], u, (((0,), (1,)), ((), ())),
                          preferred_element_type=F32)[0:2 * DN_HEADS, :]
    row = lax.broadcasted_iota(I32, bat.shape, 0)
    gt = -jnp.exp(pcol_ref[:, 0:1]) * _softplus(bat + pcol_ref[:, 1:2])
    gbt_ref[...] = jnp.where(row < DN_HEADS, _sigmoid(bat), gt)


def _in_proj(x2, gain, w_cat, prow, pcol, tm=512):
    n, d = x2.shape
    return pl.pallas_call(
        _in_proj_kernel,
        grid=(n // tm,),
        in_specs=[
            pl.BlockSpec((tm, d), lambda i: (i, 0)),
            pl.BlockSpec((1, d), lambda i: (0, 0)),
            pl.BlockSpec((d, PROJ_W), lambda i: (0, 0)),
            pl.BlockSpec((SUBLANES, LANES), lambda i: (0, 0)),
            pl.BlockSpec((2 * DN_HEADS, LANES), lambda i: (0, 0)),
        ],
        out_specs=[
            pl.BlockSpec((tm, PROJ_W), lambda i: (i, 0)),
            pl.BlockSpec((2 * DN_HEADS, tm), lambda i: (0, i)),
        ],
        out_shape=[
            jax.ShapeDtypeStruct((n, PROJ_W), F32),
            jax.ShapeDtypeStruct((2 * DN_HEADS, n), F32),
        ],
        compiler_params=_cparams(1),
        name="in_proj",
    )(x2, gain, w_cat, prow, pcol)


def _deltanet_kernel(*refs, tb):
    n_grp = 3 * DN_HEADS
    qkv_refs = refs[:n_grp]
    (z_ref, gb_ref, gbt_ref, conv_ref, norm_ref, o_ref,
     s_ref, halo_ref, act_ref, gcc_ref, gcr_ref,
     a_ref, p_ref, d_ref, qk_ref, rhs_ref, u_ref, wq_ref, kdt_ref) = refs[n_grp:]
    t = pl.program_id(1)
    c = DN_CHUNK
    dh = DN_HEAD_DIM

    @pl.when(t == 0)
    def _():
        s_ref[...] = jnp.zeros_like(s_ref)
        halo_ref[...] = jnp.zeros_like(halo_ref)

    sub = lax.broadcasted_iota(I32, (SUBLANES, dh), 0)
    for grp in range(n_grp):
        cs = slice(grp * dh, (grp + 1) * dh)
        x = qkv_refs[grp][...]
        prev = halo_ref[:, cs]
        halo_ref[:, cs] = x[tb - SUBLANES:, :]
        y = conv_ref[CONV_WIDTH - 1:CONV_WIDTH, cs] * x
        for k in range(1, CONV_WIDTH):
            xs = pltpu.roll(x, k, 0)
            head = jnp.where(sub < k, pltpu.roll(prev, k, 0), xs[:SUBLANES])
            xs = jnp.concatenate([head, xs[SUBLANES:]], axis=0)
            y = y + conv_ref[CONV_WIDTH - 1 - k:CONV_WIDTH - k, cs] * xs
        y = y * _sigmoid(y)
        if grp < 2 * DN_HEADS:
            y = y * lax.rsqrt(jnp.sum(y * y, axis=-1, keepdims=True) + EPS)
        if grp < DN_HEADS:
            y = y * (dh ** -0.5)
        act_ref[grp] = y

    gcol = gb_ref[...]
    rpos = lax.broadcasted_iota(I32, gcol.shape, 0) & (c - 1)
    d = 1
    while d < c:
        gcol = gcol + jnp.where(rpos >= d, pltpu.roll(gcol, d, 0), 0.0)
        d *= 2
    gcc_ref[...] = gcol
    grow = gbt_ref[...]
    lpos = lax.broadcasted_iota(I32, grow.shape, 1) & (c - 1)
    d = 1
    while d < c:
        grow = grow + jnp.where(lpos >= d, pltpu.roll(grow, d, 1), 0.0)
        d *= 2
    gcr_ref[...] = grow

    row = lax.broadcasted_iota(I32, (c, c), 0)
    col = lax.broadcasted_iota(I32, (c, c), 1)
    causal = row >= col
    strict = row > col
    eye = jnp.where(row == col, 1.0, 0.0)
    gain = norm_ref[...]
    n_chunks = tb // c
    probs = [(ci, h) for ci in range(n_chunks) for h in range(DN_HEADS)]

    decay_last = []
    for p, (ci, h) in enumerate(probs):
        rows = slice(ci * c, (ci + 1) * c)
        q = act_ref[h, rows, :]
        k = act_ref[DN_HEADS + h, rows, :]
        v = act_ref[2 * DN_HEADS + h, rows, :]
        beta = gb_ref[rows, h:h + 1]
        gc = gcc_ref[rows, DN_HEADS + h:DN_HEADS + h + 1]
        gr = gcr_ref[DN_HEADS + h:DN_HEADS + h + 1, rows]
        g_last = gc[c - 1:c, :]
        decay = jnp.where(causal, jnp.exp(jnp.where(causal, gc - gr, 0.0)), 0.0)
        kb = k * beta
        both = _mm_nt(jnp.concatenate([kb, q], axis=0), k)
        a_ref[p] = jnp.where(strict, both[:c] * decay, 0.0)
        qk_ref[p] = (both[c:] * decay).astype(BF16)
        egc = jnp.exp(gc)
        rhs_ref[p] = jnp.concatenate([v * beta, kb * egc], axis=1).astype(BF16)
        wq_ref[p, c:, :] = (q * egc).astype(BF16)
        kdt_ref[p] = (k * jnp.exp(g_last - gc)).T.astype(BF16)
        decay_last.append(jnp.exp(g_last))

    shift = INV_BASE.bit_length() - 1
    blk = (row >> shift) == (col >> shift)
    for p in range(len(probs)):
        diag = jnp.where(blk, a_ref[p], 0.0)
        p_ref[p] = eye - diag
        d_ref[p] = _mm(diag, diag).astype(BF16)
    for it in range(shift - 1):
        for p in range(len(probs)):
            pw = d_ref[p]
            inv = p_ref[p]
            p_ref[p] = inv + _mm(inv, pw)
            if it < shift - 2:
                d_ref[p] = _mm(pw, pw).astype(BF16)
    s = INV_BASE
    while s < c:
        sh = s.bit_length() - 1
        off = ((row >> (sh + 1)) == (col >> (sh + 1))) & ((row >> sh) != (col >> sh))
        for p in range(len(probs)):
            d_ref[p] = _mm(p_ref[p], jnp.where(off, a_ref[p], 0.0)).astype(BF16)
        for p in range(len(probs)):
            inv = p_ref[p]
            p_ref[p] = inv - _mm(d_ref[p], inv)
        s *= 2
    for p in range(len(probs)):
        uw = _mm(p_ref[p], rhs_ref[p])
        u_ref[p] = uw[:, :dh]
        wq_ref[p, :c, :] = uw[:, dh:].astype(BF16)

    for ci in range(n_chunks):
        rows = slice(ci * c, (ci + 1) * c)
        ps = [ci * DN_HEADS + h for h in range(DN_HEADS)]
        s_old = [s_ref[h] for h in range(DN_HEADS)]
        ws = [_mm(wq_ref[p], s_old[h]) for h, p in enumerate(ps)]
        v_new = [u_ref[p] - ws[h][:c] for h, p in enumerate(ps)]
        outs = [ws[h][c:] + _mm(qk_ref[p], v_new[h]) for h, p in enumerate(ps)]
        for h, p in enumerate(ps):
            s_ref[h] = s_old[h] * decay_last[p] + _mm(kdt_ref[p], v_new[h])
        for h in range(DN_HEADS):
            o = _rmsnorm(outs[h], gain)
            zz = z_ref[rows, h * dh:(h + 1) * dh]
            o_ref[rows, h * dh:(h + 1) * dh] = (o * (zz * _sigmoid(zz))).astype(o_ref.dtype)


def _deltanet(proj, gbt, dn_conv, dn_norm, batch, seq, tb=512):
    n = proj.shape[0]
    nt = seq // tb
    c = DN_CHUNK
    n_prob = (tb // c) * DN_HEADS
    n_grp = 3 * DN_HEADS
    grp = lambda j: pl.BlockSpec((tb, DN_HEAD_DIM), lambda b, t, j=j: (b * nt + t, j))
    return pl.pallas_call(
        functools.partial(_deltanet_kernel, tb=tb),
        grid=(batch, nt),
        in_specs=[grp(j) for j in range(n_grp)] + [
            pl.BlockSpec((tb, D_DN), lambda b, t: (b * nt + t, 3)),
            pl.BlockSpec((tb, LANES), lambda b, t: (b * nt + t, COL_GB // LANES)),
            pl.BlockSpec((2 * DN_HEADS, tb), lambda b, t: (0, b * nt + t)),
            pl.BlockSpec((CONV_WIDTH, 3 * D_DN), lambda b, t: (0, 0)),
            pl.BlockSpec((1, DN_HEAD_DIM), lambda b, t: (0, 0)),
        ],
        out_specs=pl.BlockSpec((tb, D_DN), lambda b, t: (b * nt + t, 0)),
        out_shape=jax.ShapeDtypeStruct((n, D_DN), BF16),
        scratch_shapes=[
            pltpu.VMEM((DN_HEADS, DN_HEAD_DIM, DN_HEAD_DIM), F32),
            pltpu.VMEM((SUBLANES, 3 * D_DN), F32),
            pltpu.VMEM((n_grp, tb, DN_HEAD_DIM), F32),
            pltpu.VMEM((tb, LANES), F32),
            pltpu.VMEM((2 * DN_HEADS, tb), F32),
            pltpu.VMEM((n_prob, c, c), F32),
            pltpu.VMEM((n_prob, c, c), F32),
            pltpu.VMEM((n_prob, c, c), BF16),
            pltpu.VMEM((n_prob, c, c), BF16),
            pltpu.VMEM((n_prob, c, 2 * DN_HEAD_DIM), BF16),
            pltpu.VMEM((n_prob, c, DN_HEAD_DIM), F32),
            pltpu.VMEM((n_prob, 2 * c, DN_HEAD_DIM), BF16),
            pltpu.VMEM((n_prob, DN_HEAD_DIM, c), BF16),
        ],
        compiler_params=_cparams(2),
        name="deltanet",
    )(*([proj] * (n_grp + 2)), gbt, dn_conv, dn_norm)


def _gelu_tanh(x):
    return 0.5 * x * (1.0 + jnp.tanh(0.7978845608028654 * (x + 0.044715 * (x * x * x))))


def _rglru_kernel(*refs, tb):
    rx_refs = refs[:RG_BLOCKS]
    (ry_ref, conv_ref, convb_ref, wa_ref, ba_ref, wx_ref, bx_ref, lam_ref,
     o_ref, hc_ref, halo_ref, a_ref, b_ref, h_ref) = refs[RG_BLOCKS:]
    t = pl.program_id(1)
    bw = D_RG // RG_BLOCKS
    seg = tb // SUBLANES

    @pl.when(t == 0)
    def _():
        hc_ref[...] = jnp.zeros_like(hc_ref)
        halo_ref[...] = jnp.zeros_like(halo_ref)

    log_sig = -_softplus(-lam_ref[...])
    rowi = lax.broadcasted_iota(I32, (tb, bw), 0)
    seq_start_row = jnp.where(t == 0, 0, -1)
    for nb in range(RG_BLOCKS):
        cs = slice(nb * bw, (nb + 1) * bw)
        xb = (_causal_conv_segmented(_load_segmented(rx_refs[nb]), halo_ref, cs,
                                     conv_ref[:, cs]) + convb_ref[:, cs])
        r = _sigmoid(_mm(xb, wa_ref[nb]) + ba_ref[:, cs])
        gi = _sigmoid(_mm(xb, wx_ref[nb]) + bx_ref[:, cs])
        log_a = RG_C * r * log_sig[:, cs]
        a = jnp.exp(log_a)
        y = jnp.tanh(-log_a) * (1.0 + a * a)
        mult = jnp.where(y > 0.0, y * lax.rsqrt(y), 0.0)
        mult = jnp.where(rowi == seq_start_row, 1.0, mult)
        a_ref[:, cs] = a
        b_ref[:, cs] = mult * (gi * xb)

    def scan(j, carry):
        ac, bc = carry
        rows = pl.ds(pl.multiple_of(j * SUBLANES, SUBLANES), SUBLANES)
        aj = a_ref[rows, :]
        bc = aj * bc + b_ref[rows, :]
        ac = aj * ac
        a_ref[rows, :] = ac
        b_ref[rows, :] = bc
        return ac, bc

    ac, bc = lax.fori_loop(1, seg, scan, (a_ref[0:SUBLANES, :], b_ref[0:SUBLANES, :]), unroll=7)

    h = hc_ref[...]
    h_in = []
    for s in range(SUBLANES):
        h_in.append(h)
        h = ac[s:s + 1, :] * h + bc[s:s + 1, :]
    hc_ref[...] = h
    h_in = jnp.concatenate(h_in, axis=0)

    for j in range(seg):
        rows = slice(j * SUBLANES, (j + 1) * SUBLANES)
        hj = a_ref[rows, :] * h_in + b_ref[rows, :]
        for nb in range(RG_BLOCKS):
            h_ref[nb, rows, :] = hj[:, nb * bw:(nb + 1) * bw]
    g = seg // SUBLANES
    for nb in range(RG_BLOCKS):
        cs = slice(nb * bw, (nb + 1) * bw)
        h = jnp.concatenate(
            [h_ref[nb, pl.ds((i % g) * SUBLANES * SUBLANES + i // g, SUBLANES, stride=SUBLANES), :]
             for i in range(seg)], axis=0)
        o_ref[:, cs] = (h * _gelu_tanh(ry_ref[:, cs])).astype(o_ref.dtype)


def _rglru(proj, rg_conv, rg_conv_b, w_a, b_a, w_x, b_x, lam, batch, seq, tb=512):
    n = proj.shape[0]
    nt = seq // tb
    bw = D_RG // RG_BLOCKS
    full = lambda shape: pl.BlockSpec(shape, lambda b, t: (0,) * len(shape))
    return pl.pallas_call(
        functools.partial(_rglru_kernel, tb=tb),
        grid=(batch, nt),
        in_specs=[pl.BlockSpec((tb, bw), lambda b, t, j=j: (b * nt + t, COL_RX // bw + j))
                  for j in range(RG_BLOCKS)] + [
            pl.BlockSpec((tb, D_RG), lambda b, t: (b * nt + t, COL_RY // D_RG)),
            full((CONV_WIDTH, D_RG)), full((1, D_RG)),
            full((RG_BLOCKS, bw, bw)), full((1, D_RG)),
            full((RG_BLOCKS, bw, bw)), full((1, D_RG)),
            full((1, D_RG)),
        ],
        out_specs=pl.BlockSpec((tb, D_RG), lambda b, t: (b * nt + t, 0)),
        out_shape=jax.ShapeDtypeStruct((n, D_RG), BF16),
        scratch_shapes=[
            pltpu.VMEM((1, D_RG), F32),
            pltpu.VMEM(((CONV_WIDTH - 1) * SUBLANES, D_RG), F32),
            pltpu.VMEM((tb, D_RG), F32),
            pltpu.VMEM((tb, D_RG), F32),
            pltpu.VMEM((RG_BLOCKS, tb, bw), F32),
        ],
        compiler_params=_cparams(2),
        name="rglru",
    )(*([proj] * (RG_BLOCKS + 1)), rg_conv, rg_conv_b, w_a, b_a, w_x, b_x, lam)


def _mem_kv_kernel(m_ref, g_ref, w_ref, o_ref):
    mn = _rmsnorm(m_ref[...], g_ref[...]).astype(BF16)
    for c0 in range(0, o_ref.shape[1], 512):
        o_ref[:, c0:c0 + 512] = jnp.dot(mn, w_ref[:, c0:c0 + 512].astype(BF16),
                                        preferred_element_type=F32).astype(o_ref.dtype)


def _mem_kv(mem2, gain, w_ckv):
    n, d = mem2.shape
    return pl.pallas_call(
        _mem_kv_kernel,
        grid=(1,),
        in_specs=[
            pl.BlockSpec((n, d), lambda i: (0, 0)),
            pl.BlockSpec((1, d), lambda i: (0, 0)),
            pl.BlockSpec((d, 2 * d), lambda i: (0, 0)),
        ],
        out_specs=pl.BlockSpec((n, 2 * d), lambda i: (0, 0)),
        out_shape=jax.ShapeDtypeStruct((n, 2 * d), BF16),
        compiler_params=_cparams(1),
        name="mem_kv",
    )(mem2, gain, w_ckv)


ROUTE_E = 0
ROUTE_RANK = TOP_K
ROUTE_GATE = 2 * TOP_K
ROUTE_ROWS = 16


def _post_mix_kernel(dn_ref, rg_ref, x_ref, wo_f32, gx_ref, wq_f32, kv_ref, wco_f32, gm_ref,
                     wrh_ref, wrl_ref, br_ref, h2_ref, xn_ref, route_ref, routet_ref, cnt_ref,
                     carry_ref, wo_ref, wq_ref, wco_ref):
    i = pl.program_id(0)
    tm, d = x_ref.shape
    hd = d // XA_HEADS

    @pl.when(i == 0)
    def _():
        carry_ref[...] = jnp.zeros_like(carry_ref)
        wo_ref[...] = wo_f32[...].astype(BF16)
        wq_ref[...] = wq_f32[...].astype(BF16)
        wco_ref[...] = wco_f32[...].astype(BF16)

    h1 = (x_ref[...] + jnp.dot(dn_ref[...], wo_ref[0:D_DN, :], preferred_element_type=F32)
          + jnp.dot(rg_ref[...], wo_ref[D_DN:, :], preferred_element_type=F32))

    hn = _rmsnorm(h1, gx_ref[...]).astype(BF16)
    q = jnp.dot(hn, wq_ref[...], preferred_element_type=F32)
    heads = []
    for hh in range(XA_HEADS):
        cs = slice(hh * hd, (hh + 1) * hd)
        s = _mm_nt(q[:, cs], kv_ref[:, cs]) * (hd ** -0.5)
        p = jnp.exp(s - jnp.max(s, axis=-1, keepdims=True))
        p = p / jnp.sum(p, axis=-1, keepdims=True)
        heads.append(_mm(p, kv_ref[:, d + hh * hd:d + (hh + 1) * hd]).astype(BF16))
    o = jnp.concatenate(heads, axis=1)
    h2 = h1 + jnp.dot(o, wco_ref[...], preferred_element_type=F32)
    h2_ref[...] = h2

    xn = _rmsnorm(h2, gm_ref[...])
    _store_row_tiles(xn_ref, _pack_bf16_pairs(xn))
    logits = _mm_split(xn, wrh_ref[...], wrl_ref[...]) + br_ref[...]
    lg = logits.T[0:N_EXPERTS, :]
    eidx = lax.broadcasted_iota(I32, lg.shape, 0).astype(F32)
    neg = jnp.float32(-jnp.inf)
    vals, idxs, hots = [], [], []
    for _ in range(TOP_K):
        m = jnp.max(lg, axis=0, keepdims=True)
        idx = jnp.min(jnp.where(lg == m, eidx, float(N_EXPERTS)), axis=0, keepdims=True)
        hot = eidx == idx
        lg = jnp.where(hot, neg, lg)
        vals.append(m)
        idxs.append(idx)
        hots.append(hot)
    es = [jnp.exp(v - vals[0]) for v in vals]
    den = es[0] + es[1] + es[2] + es[3]
    gates = [e / den for e in es]

    chosen = jnp.zeros(lg.shape, F32)
    for hot in hots:
        chosen = chosen + jnp.where(hot, 1.0, 0.0)
    r2 = lax.broadcasted_iota(I32, (tm, tm), 0)
    c2 = lax.broadcasted_iota(I32, (tm, tm), 1)
    before = _mm(chosen, jnp.where(r2 < c2, 1.0, 0.0)) + carry_ref[:, 0:1]
    ranks = [jnp.sum(jnp.where(hot, before, 0.0), axis=0, keepdims=True) for hot in hots]
    carry_ref[...] = carry_ref[...] + jnp.sum(chosen, axis=1, keepdims=True)
    cnt_ref[...] = carry_ref[...]

    row = lax.broadcasted_iota(I32, (ROUTE_ROWS, tm), 0)
    rect = jnp.zeros((ROUTE_ROWS, tm), F32)
    for kk in range(TOP_K):
        rect = jnp.where(row == ROUTE_E + kk, idxs[kk], rect)
        rect = jnp.where(row == ROUTE_RANK + kk, ranks[kk], rect)
        rect = jnp.where(row == ROUTE_GATE + kk, gates[kk], rect)
    routet_ref[...] = rect
    route_ref[...] = jnp.concatenate(
        [rect, jnp.zeros((LANES - ROUTE_ROWS, tm), F32)], axis=0).T


def _post_mix(dn, rg, x2, w_out, g_cross, w_cq, kv, w_co, g_moe, w_r_hi, w_r_lo, b_r, seq, mem_len,
              tm=512):
    n, d = x2.shape
    per_b = seq // tm
    rt_x = d // 2 // LANES
    full = lambda shape: pl.BlockSpec(shape, lambda i: (0,) * len(shape))
    once = lambda shape: pl.BlockSpec(shape, lambda i: (0,) * len(shape),
                                      pipeline_mode=pl.Buffered(1))
    return pl.pallas_call(
        _post_mix_kernel,
        grid=(n // tm,),
        in_specs=[
            pl.BlockSpec((tm, D_DN), lambda i: (i, 0)),
            pl.BlockSpec((tm, D_RG), lambda i: (i, 0)),
            pl.BlockSpec((tm, d), lambda i: (i, 0)),
            once((d, d)), full((1, d)), once((d, d)),
            pl.BlockSpec((mem_len, 2 * d), lambda i: (i // per_b, 0)),
            once((d, d)), full((1, d)),
            full((d, LANES)), full((d, LANES)), full((1, LANES)),
        ],
        out_specs=[
            pl.BlockSpec((tm, d), lambda i: (i, 0)),
            pl.BlockSpec((tm * rt_x, LANES), lambda i: (i, 0)),
            pl.BlockSpec((tm, LANES), lambda i: (i, 0)),
            pl.BlockSpec((ROUTE_ROWS, tm), lambda i: (0, i)),
            pl.BlockSpec((N_EXPERTS, LANES), lambda i: (0, 0)),
        ],
        out_shape=[
            jax.ShapeDtypeStruct((n, d), F32),
            jax.ShapeDtypeStruct((n * rt_x, LANES), U32),
            jax.ShapeDtypeStruct((n, LANES), F32),
            jax.ShapeDtypeStruct((ROUTE_ROWS, n), F32),
            jax.ShapeDtypeStruct((N_EXPERTS, LANES), F32),
        ],
        scratch_shapes=[pltpu.VMEM((N_EXPERTS, LANES), F32),
                        pltpu.VMEM((d, d), BF16), pltpu.VMEM((d, d), BF16), pltpu.VMEM((d, d), BF16)],
        compiler_params=_cparams(1),
        name="post_mix",
    )(dn, rg, x2, w_out, g_cross, w_cq, kv, w_co, g_moe, w_r_hi, w_r_lo, b_r)


def _dest_kernel(start_ref, routet_ref, dest_ref):
    e = routet_ref[ROUTE_E:ROUTE_E + SUBLANES, :]
    rank = routet_ref[ROUTE_RANK:ROUTE_RANK + SUBLANES, :]
    base = jnp.zeros(e.shape, F32)
    for j in range(N_EXPERTS):
        base = jnp.where(e == float(j), start_ref[j].astype(F32), base)
    row = lax.broadcasted_iota(I32, e.shape, 0)
    dest_ref[...] = jnp.where(row < TOP_K, (base + rank).astype(I32), 0)


def _dest(pad_start, route_t, tm=2048):
    n = route_t.shape[1]
    grid_spec = pltpu.PrefetchScalarGridSpec(
        num_scalar_prefetch=1,
        grid=(n // tm,),
        in_specs=[pl.BlockSpec((ROUTE_ROWS, tm), lambda i, st: (0, i))],
        out_specs=pl.BlockSpec((SUBLANES, tm), lambda i, st: (0, i)),
    )
    return pl.pallas_call(
        _dest_kernel,
        grid_spec=grid_spec,
        out_shape=jax.ShapeDtypeStruct((SUBLANES, n), I32),
        compiler_params=_cparams(1),
        name="dest",
    )(pad_start, route_t)


def _row_tile(ref, r, rt):
    return ref.at[pl.ds(pl.multiple_of(r * rt, rt), rt)]


def _dispatch_kernel(meta_ref, dest_ref, xn_ref, xs_ref, zero_ref, ring_ref, sem, zsem,
                     *, tm, rt, n_blocks):
    i = pl.program_id(0)
    bm = MOE_BM * rt
    ring = ring_ref.at[i & 1]
    ring[...] = xn_ref[...]

    def per_token(tk, carry):
        src = _row_tile(ring, tk, rt)
        for kk in range(TOP_K):
            r = dest_ref[kk, tk]
            pltpu.make_async_copy(src, _row_tile(xs_ref, r, rt),
                                  sem.at[i & 1]).start(priority=kk % 2)
        return carry

    lax.fori_loop(0, tm, per_token, 0, unroll=4)

    @pl.when(i == 0)
    def _():
        zero_ref[...] = jnp.zeros_like(zero_ref)

        def pad_copy(r):
            return pltpu.make_async_copy(zero_ref.at[pl.ds(0, rt)], _row_tile(xs_ref, r, rt), zsem)

        def per_expert(e, total):
            r0 = meta_ref[e]
            cnt = meta_ref[N_EXPERTS + e]

            def group(j, carry):
                for u in range(PAD_UNROLL):
                    pad_copy(r0 + j * PAD_UNROLL + u).start(priority=u % 2)
                return carry

            def one(r, carry):
                pad_copy(r0 + r).start()
                return carry

            groups = cnt // PAD_UNROLL
            lax.fori_loop(0, groups, group, 0)
            lax.fori_loop(groups * PAD_UNROLL, cnt, one, 0)
            return total + cnt

        total = lax.fori_loop(0, N_EXPERTS, per_expert, 0)

        def drain_block(j, carry):
            pltpu.make_async_copy(zero_ref, xs_ref.at[pl.ds(0, bm)], zsem).wait()
            return carry

        def drain_group(j, carry):
            pltpu.make_async_copy(zero_ref.at[pl.ds(0, PAD_UNROLL * rt)],
                                  xs_ref.at[pl.ds(0, PAD_UNROLL * rt)], zsem).wait()
            return carry

        def drain_row(j, carry):
            pad_copy(0).wait()
            return carry

        lax.fori_loop(0, total // MOE_BM, drain_block, 0)
        rest = total % MOE_BM
        lax.fori_loop(0, rest // PAD_UNROLL, drain_group, 0)
        lax.fori_loop(0, rest % PAD_UNROLL, drain_row, 0)

        def tail_copy(b):
            return pltpu.make_async_copy(
                zero_ref, xs_ref.at[pl.ds(pl.multiple_of(b * bm, bm), bm)], zsem)

        n_used = meta_ref[2 * N_EXPERTS]

        def tail(b, carry):
            tail_copy(b).start()
            return carry

        lax.fori_loop(n_used, n_blocks, tail, 0)

        def tail_drain(b, carry):
            tail_copy(0).wait()
            return carry

        lax.fori_loop(n_used, n_blocks, tail_drain, 0)

    def wait_step(parity):
        for _ in range(TOP_K):
            pltpu.make_async_copy(ring_ref.at[parity], xs_ref.at[pl.ds(0, tm * rt)],
                                  sem.at[parity]).wait()

    @pl.when(i > 0)
    def _():
        wait_step(1 - (i & 1))

    @pl.when(i == pl.num_programs(0) - 1)
    def _():
        wait_step(i & 1)


def _dispatch(meta, dest, xn_t, n_tok, n_rows, tm=256):
    rt = xn_t.shape[0] // n_tok
    n_blocks = n_rows // MOE_BM
    grid_spec = pltpu.PrefetchScalarGridSpec(
        num_scalar_prefetch=1,
        grid=(n_tok // tm,),
        in_specs=[
            pl.BlockSpec((SUBLANES, tm), lambda i, meta: (0, i), memory_space=pltpu.SMEM),
            pl.BlockSpec((tm * rt, LANES), lambda i, meta: (i, 0)),
        ],
        out_specs=pl.BlockSpec(memory_space=pl.ANY),
        scratch_shapes=[
            pltpu.VMEM((MOE_BM * rt, LANES), xn_t.dtype),
            pltpu.VMEM((2, tm * rt, LANES), xn_t.dtype),
            pltpu.SemaphoreType.DMA((2,)),
            pltpu.SemaphoreType.DMA,
        ],
    )
    return pl.pallas_call(
        functools.partial(_dispatch_kernel, tm=tm, rt=rt, n_blocks=n_blocks),
        grid_spec=grid_spec,
        out_shape=jax.ShapeDtypeStruct((n_rows * rt, LANES), xn_t.dtype),
        compiler_params=_cparams(1),
        name="dispatch",
    )(meta, dest, xn_t)


def _experts_kernel(be_ref, nu_ref, nxt_ref, par_ref, x_ref, bg_ref, bu_ref, bd_ref,
                    wg_hbm, wu_hbm, wd_hbm, y_ref, wf_ref, wb_ref, sem):
    i = pl.program_id(0)
    used = i < nu_ref[0]
    e = be_ref[i]
    changed = (i == 0) | (e != be_ref[jnp.maximum(i - 1, 0)])

    def weight_copies(expert, slot):
        return [pltpu.make_async_copy(w.at[0, expert], wf_ref.at[slot, j], sem.at[slot])
                for j, w in enumerate((wg_hbm, wu_hbm, wd_hbm))]

    @pl.when(i == 0)
    def _():
        for cp in weight_copies(e, par_ref[e]):
            cp.start()

    @pl.when(used & changed)
    def _():
        slot = par_ref[e]
        for cp in weight_copies(e, slot):
            cp.wait()
        nxt = nxt_ref[e]

        @pl.when(nxt < N_EXPERTS)
        def _():
            for cp in weight_copies(nxt, 1 - slot):
                cp.start()

        for j in range(3):
            wb_ref[j] = wf_ref[slot, j].astype(BF16)

    @pl.when(used)
    def _():
        x = _unpack_pairs_f32(_load_row_tiles(x_ref, MOE_BM)).astype(BF16)
        gt = jnp.minimum(jnp.dot(x, wb_ref[0], preferred_element_type=F32) + bg_ref[...],
                         SWIGLU_LIMIT)
        up = jnp.clip(jnp.dot(x, wb_ref[1], preferred_element_type=F32) + bu_ref[...],
                      -SWIGLU_LIMIT, SWIGLU_LIMIT)
        hid = (up + 1.0) * (gt * _sigmoid(SWIGLU_ALPHA * gt))
        y = jnp.dot(hid.astype(BF16), wb_ref[2], preferred_element_type=F32) + bd_ref[...]
        _store_row_tiles(y_ref, _pack_bf16_pairs(y))

    @pl.when(jnp.logical_not(used))
    def _():
        y_ref[...] = jnp.zeros_like(y_ref)


def _experts(block_e, n_used, next_e, parity, xs_t, w_gate, b_gate, w_up, b_up, w_down, b_down):
    d, d_ff = w_gate.shape[2:]
    assert d == d_ff, "weight staging buffers assume square expert matrices"
    rt = d // 2 // LANES
    n_blocks = xs_t.shape[0] // (MOE_BM * rt)
    bspec = lambda m: pl.BlockSpec((None, 1, m), lambda i, be, nu, nx, pa: (be[i], 0, 0))
    hbm = pl.BlockSpec(memory_space=pl.ANY)
    grid_spec = pltpu.PrefetchScalarGridSpec(
        num_scalar_prefetch=4,
        grid=(n_blocks,),
        in_specs=[
            pl.BlockSpec((MOE_BM * rt, LANES),
                         lambda i, be, nu, nx, pa: (jnp.maximum(jnp.minimum(i, nu[0] - 1), 0), 0)),
            bspec(d_ff), bspec(d_ff), bspec(d),
            hbm, hbm, hbm,
        ],
        out_specs=pl.BlockSpec((MOE_BM * rt, LANES), lambda i, be, nu, nx, pa: (i, 0)),
        scratch_shapes=[
            pltpu.VMEM((2, 3, d, d_ff), F32),
            pltpu.VMEM((3, d, d_ff), BF16),
            pltpu.SemaphoreType.DMA((2,)),
        ],
    )
    return pl.pallas_call(
        _experts_kernel,
        grid_spec=grid_spec,
        out_shape=jax.ShapeDtypeStruct(xs_t.shape, U32),
        compiler_params=_cparams(1),
        name="experts",
    )(block_e, n_used, next_e, parity, xs_t, b_gate, b_up, b_down, w_gate, w_up, w_down)


def _combine_kernel(dcur_ref, dnext_ref, h2_ref, route_ref, gain_ref, ys_ref, o_ref,
                    buf_ref, sem, *, tm, rt):
    i = pl.program_id(0)
    nsteps = pl.num_programs(0)
    slot = i & 1

    def issue_all(dref, s):
        def per_token(tk, carry):
            for kk in range(TOP_K):
                r = dref[kk, tk]
                pltpu.make_async_copy(_row_tile(ys_ref, r, rt),
                                      _row_tile(buf_ref.at[s, kk], tk, rt),
                                      sem.at[s]).start(priority=kk % 2)
            return carry
        lax.fori_loop(0, tm, per_token, 0, unroll=4)

    @pl.when(i == 0)
    def _():
        issue_all(dcur_ref, 0)

    @pl.when(i + 1 < nsteps)
    def _():
        issue_all(dnext_ref, 1 - slot)

    for kk in range(TOP_K):
        pltpu.make_async_copy(ys_ref.at[pl.ds(0, tm * rt)], buf_ref.at[slot, kk],
                              sem.at[slot]).wait()

    rec = route_ref[...]
    acc = h2_ref[...]
    for kk in range(TOP_K):
        acc = acc + (rec[:, ROUTE_GATE + kk:ROUTE_GATE + kk + 1]
                     * _unpack_pairs_f32(_load_row_tiles(buf_ref.at[slot, kk], tm)))
    o_ref[...] = _rmsnorm(acc, gain_ref[...])


def _combine(dest, h2, route, gain, ys_t, tm=256):
    n, d = h2.shape
    nsteps = n // tm
    rt = d // 2 // LANES
    return pl.pallas_call(
        functools.partial(_combine_kernel, tm=tm, rt=rt),
        grid=(nsteps,),
        in_specs=[
            pl.BlockSpec((SUBLANES, tm), lambda i: (0, i), memory_space=pltpu.SMEM),
            pl.BlockSpec((SUBLANES, tm), lambda i: (0, jnp.minimum(i + 1, nsteps - 1)),
                         memory_space=pltpu.SMEM),
            pl.BlockSpec((tm, d), lambda i: (i, 0)),
            pl.BlockSpec((tm, LANES), lambda i: (i, 0)),
            pl.BlockSpec((1, d), lambda i: (0, 0)),
            pl.BlockSpec(memory_space=pl.ANY),
        ],
        out_specs=pl.BlockSpec((tm, d), lambda i: (i, 0)),
        out_shape=jax.ShapeDtypeStruct((n, d), F32),
        scratch_shapes=[
            pltpu.VMEM((2, TOP_K, tm * rt, LANES), U32),
            pltpu.SemaphoreType.DMA((2,)),
        ],
        compiler_params=_cparams(1),
        name="combine",
    )(dest, dest, h2, route, gain, ys_t)


def kernel(x, mem, norm_mix, w_in, dn_conv, dn_a_log, dn_dt_bias, dn_norm, rg_conv, rg_conv_b, rg_w_a, rg_b_a, rg_w_x, rg_b_x, rg_lambda, w_out, norm_cross, norm_mem, w_cq, w_ckv, w_co, norm_moe, w_router, b_router, w_gate, b_gate, w_up, b_up, w_down, b_down, norm_final):
    batch, seq, d = x.shape
    mem_len = mem.shape[1]
    n = batch * seq
    assert w_in.shape[0] == 1, "single-layer trunk"
    x2 = x.reshape(n, d)

    wi = w_in[0]
    n_gate = 2 * DN_HEADS
    w_cat = jnp.concatenate(
        [wi[:, :4 * D_DN], wi[:, 4 * D_DN + n_gate:],
         jnp.pad(wi[:, 4 * D_DN:4 * D_DN + n_gate], ((0, 0), (0, LANES - n_gate)))],
        axis=1).astype(BF16)
    prow = (jnp.zeros((SUBLANES, LANES), F32)
            .at[0, DN_HEADS:n_gate].set(dn_a_log[0]).at[1, DN_HEADS:n_gate].set(dn_dt_bias[0]))
    pcol = (jnp.zeros((n_gate, LANES), F32)
            .at[DN_HEADS:, 0].set(dn_a_log[0]).at[DN_HEADS:, 1].set(dn_dt_bias[0]))

    proj, gbt = _in_proj(x2, norm_mix, w_cat, prow, pcol)
    dn = _deltanet(proj, gbt, dn_conv[0], dn_norm, batch, seq)
    rg = _rglru(proj, rg_conv[0], rg_conv_b, rg_w_a[0], rg_b_a[0].reshape(1, D_RG),
                rg_w_x[0], rg_b_x[0].reshape(1, D_RG), rg_lambda, batch, seq)
    kv = _mem_kv(mem.reshape(batch * mem_len, d), norm_mem, w_ckv[0])

    w_r = jnp.pad(w_router[0], ((0, 0), (0, LANES - N_EXPERTS)))
    w_r_hi = w_r.astype(BF16)
    w_r_lo = (w_r - w_r_hi.astype(F32)).astype(BF16)
    b_r = jnp.pad(b_router, ((0, 0), (0, LANES - N_EXPERTS)))
    h2, xn, route, route_t, counts = _post_mix(
        dn, rg, x2, w_out[0], norm_cross, w_cq[0], kv, w_co[0], norm_moe, w_r_hi, w_r_lo, b_r,
        seq, mem_len)

    n_blocks = n * TOP_K // MOE_BM + N_EXPERTS
    n_rows = n_blocks * MOE_BM
    cnt = counts[:, 0].astype(I32)
    padded = (cnt + MOE_BM - 1) // MOE_BM * MOE_BM
    pad_end = jnp.cumsum(padded)
    pad_start = pad_end - padded
    n_used = (pad_end[-1:] // MOE_BM).astype(I32)
    block_e = jnp.minimum(
        jnp.sum(pad_end[None, :] <= (jnp.arange(n_blocks, dtype=I32) * MOE_BM)[:, None], axis=1),
        N_EXPERTS - 1).astype(I32)
    meta = jnp.concatenate([pad_start + cnt, padded - cnt, n_used]).astype(I32)

    dest = _dest(pad_start.astype(I32), route_t)
    xs = _dispatch(meta, dest, xn, n, n_rows)
    has = cnt > 0
    eid = jnp.where(has, jnp.arange(N_EXPERTS, dtype=I32), N_EXPERTS)
    after = lax.cummin(eid, axis=0, reverse=True)
    next_e = jnp.concatenate([after[1:], jnp.full((1,), N_EXPERTS, I32)]).astype(I32)
    parity = ((jnp.cumsum(has.astype(I32)) - 1) & 1).astype(I32)
    ys = _experts(block_e, n_used, next_e, parity, xs, w_gate, b_gate[0][:, None, :], w_up,
                  b_up[0][:, None, :], w_down, b_down[0][:, None, :])
    out = _combine(dest, h2, route, norm_final.reshape(1, d), ys)
    return out.reshape(batch, seq, d)
```

```python
import functools

import jax
import jax.numpy as jnp
from jax import lax
from jax.experimental import pallas as pl
from jax.experimental.pallas import tpu as pltpu

F32 = jnp.float32
BF16 = jnp.bfloat16
I32 = jnp.int32
U32 = jnp.uint32

EPS = 1e-6
LANES = 128
SUBLANES = 8
VMEM_LIMIT = 48 * 1024 * 1024

DN_HEADS = 4
DN_HEAD_DIM = 128
D_DN = DN_HEADS * DN_HEAD_DIM
D_RG = 512
RG_BLOCKS = 4
RG_C = 8.0
CONV_WIDTH = 4
XA_HEADS = 4
N_EXPERTS = 32
TOP_K = 4
SWIGLU_LIMIT = 7.0
SWIGLU_ALPHA = 1.702

DN_CHUNK = 128
INV_BASE = 16
MOE_BM = 512
PAD_UNROLL = 8

COL_RX = 4 * D_DN
COL_RY = COL_RX + D_RG
COL_GB = COL_RY + D_RG
PROJ_W = COL_GB + LANES


def _cparams(n_axes=1):
    return pltpu.CompilerParams(
        dimension_semantics=("arbitrary",) * n_axes, vmem_limit_bytes=VMEM_LIMIT)


def _mm(a, b):
    return jnp.dot(a.astype(BF16), b.astype(BF16), preferred_element_type=F32)


def _mm_nt(a, b):
    return lax.dot_general(a.astype(BF16), b.astype(BF16), (((1,), (1,)), ((), ())),
                           preferred_element_type=F32)


def _rmsnorm(x, g):
    return x * lax.rsqrt(jnp.mean(x * x, axis=-1, keepdims=True) + EPS) * g


def _sigmoid(x):
    return 0.5 * jnp.tanh(0.5 * x) + 0.5


def _mm_split(a, b_hi, b_lo):
    a_hi = a.astype(BF16)
    a_lo = (a - a_hi.astype(F32)).astype(BF16)
    return (jnp.dot(a_hi, b_hi, preferred_element_type=F32)
            + jnp.dot(a_hi, b_lo, preferred_element_type=F32)
            + jnp.dot(a_lo, b_hi, preferred_element_type=F32))


def _softplus(x):
    return jnp.maximum(x, 0.0) + jnp.log1p(jnp.exp(-jnp.abs(x)))


def _load_row_tiles(ref, rows):
    rt = ref.shape[0] // rows
    return jnp.concatenate([ref[pl.ds(s, rows, stride=rt), :] for s in range(rt)], axis=1)


def _store_row_tiles(ref, val):
    rows, w = val.shape
    rt = w // LANES
    for s in range(rt):
        ref[pl.ds(s, rows, stride=rt), :] = val[:, s * LANES:(s + 1) * LANES]


def _pack_bf16_pairs(x):
    half = x.shape[1] // 2
    hi = pltpu.bitcast(x[:, :half].astype(BF16).astype(F32), U32)
    lo = pltpu.bitcast(x[:, half:].astype(BF16).astype(F32), U32)
    return hi | (lo >> 16)


def _unpack_pairs_f32(p):
    hi = pltpu.bitcast(p & jnp.uint32(0xFFFF0000), F32)
    lo = pltpu.bitcast(p << 16, F32)
    return jnp.concatenate([hi, lo], axis=1)


def _load_segmented(ref):
    seg = ref.shape[0] // SUBLANES
    return jnp.concatenate([ref[pl.ds(j, SUBLANES, stride=seg), :] for j in range(seg)], axis=0)


def _causal_conv_segmented(xp, halo_ref, cs, w):
    tb = xp.shape[0]
    ng = w.shape[0] - 1
    prev = halo_ref[:, cs]
    last = xp[tb - ng * SUBLANES:, :]
    halo_ref[:, cs] = last
    sub = lax.broadcasted_iota(I32, (SUBLANES, xp.shape[1]), 0)
    groups = []
    for g in range(ng):
        rows = slice(g * SUBLANES, (g + 1) * SUBLANES)
        groups.append(jnp.where(sub == 0, pltpu.roll(prev[rows], 1, 0),
                                pltpu.roll(last[rows], 1, 0)))
    ext = jnp.concatenate(groups + [xp], axis=0)
    y = w[ng:ng + 1] * xp
    for k in range(1, ng + 1):
        y = y + w[ng - k:ng - k + 1] * ext[(ng - k) * SUBLANES:(ng - k) * SUBLANES + tb]
    return y


def _in_proj_kernel(x_ref, g_ref, w_ref, prow_ref, pcol_ref, proj_ref, gbt_ref):
    u = _rmsnorm(x_ref[...], g_ref[...]).astype(BF16)
    for c0 in range(0, COL_GB, 512):
        proj_ref[:, c0:c0 + 512] = jnp.dot(u, w_ref[:, c0:c0 + 512], preferred_element_type=F32)
    ba = jnp.dot(u, w_ref[:, COL_GB:PROJ_W], preferred_element_type=F32)
    lane = lax.broadcasted_iota(I32, ba.shape, 1)
    g = -jnp.exp(prow_ref[0:1, :]) * _softplus(ba + prow_ref[1:2, :])
    proj_ref[:, COL_GB:PROJ_W] = jnp.where(lane < DN_HEADS, _sigmoid(ba), g)
    bat = lax.dot_general(w_ref[:, COL_GB:PROJ_W], u, (((0,), (1,)), ((), ())),
                          preferred_element_type=F32)[0:2 * DN_HEADS, :]
    row = lax.broadcasted_iota(I32, bat.shape, 0)
    gt = -jnp.exp(pcol_ref[:, 0:1]) * _softplus(bat + pcol_ref[:, 1:2])
    gbt_ref[...] = jnp.where(row < DN_HEADS, _sigmoid(bat), gt)


def _in_proj(x2, gain, w_cat, prow, pcol, tm=512):
    n, d = x2.shape
    return pl.pallas_call(
        _in_proj_kernel,
        grid=(n // tm,),
        in_specs=[
            pl.BlockSpec((tm, d), lambda i: (i, 0)),
            pl.BlockSpec((1, d), lambda i: (0, 0)),
            pl.BlockSpec((d, PROJ_W), lambda i: (0, 0)),
            pl.BlockSpec((SUBLANES, LANES), lambda i: (0, 0)),
            pl.BlockSpec((2 * DN_HEADS, LANES), lambda i: (0, 0)),
        ],
        out_specs=[
            pl.BlockSpec((tm, PROJ_W), lambda i: (i, 0)),
            pl.BlockSpec((2 * DN_HEADS, tm), lambda i: (0, i)),
        ],
        out_shape=[
            jax.ShapeDtypeStruct((n, PROJ_W), F32),
            jax.ShapeDtypeStruct((2 * DN_HEADS, n), F32),
        ],
        compiler_params=_cparams(1),
        name="in_proj",
    )(x2, gain, w_cat, prow, pcol)


def _deltanet_kernel(*refs, tb):
    n_grp = 3 * DN_HEADS
    qkv_refs = refs[:n_grp]
    (z_ref, gb_ref, gbt_ref, conv_ref, norm_ref, o_ref,
     s_ref, halo_ref, act_ref, gcc_ref, gcr_ref,
     a_ref, p_ref, d_ref, qk_ref, rhs_ref, u_ref, wq_ref, kdt_ref) = refs[n_grp:]
    t = pl.program_id(1)
    c = DN_CHUNK
    dh = DN_HEAD_DIM

    @pl.when(t == 0)
    def _():
        s_ref[...] = jnp.zeros_like(s_ref)
        halo_ref[...] = jnp.zeros_like(halo_ref)

    sub = lax.broadcasted_iota(I32, (SUBLANES, dh), 0)
    for grp in range(n_grp):
        cs = slice(grp * dh, (grp + 1) * dh)
        x = qkv_refs[grp][...]
        prev = halo_ref[:, cs]
        halo_ref[:, cs] = x[tb - SUBLANES:, :]
        y = conv_ref[CONV_WIDTH - 1:CONV_WIDTH, cs] * x
        for k in range(1, CONV_WIDTH):
            xs = pltpu.roll(x, k, 0)
            head = jnp.where(sub < k, pltpu.roll(prev, k, 0), xs[:SUBLANES])
            xs = jnp.concatenate([head, xs[SUBLANES:]], axis=0)
            y = y + conv_ref[CONV_WIDTH - 1 - k:CONV_WIDTH - k, cs] * xs
        y = y * _sigmoid(y)
        if grp < 2 * DN_HEADS:
            y = y * lax.rsqrt(jnp.sum(y * y, axis=-1, keepdims=True) + EPS)
        if grp < DN_HEADS:
            y = y * (dh ** -0.5)
        act_ref[grp] = y

    gcol = gb_ref[...]
    rpos = lax.broadcasted_iota(I32, gcol.shape, 0) & (c - 1)
    d = 1
    while d < c:
        gcol = gcol + jnp.where(rpos >= d, pltpu.roll(gcol, d, 0), 0.0)
        d *= 2
    gcc_ref[...] = gcol
    grow = gbt_ref[...]
    lpos = lax.broadcasted_iota(I32, grow.shape, 1) & (c - 1)
    d = 1
    while d < c:
        grow = grow + jnp.where(lpos >= d, pltpu.roll(grow, d, 1), 0.0)
        d *= 2
    gcr_ref[...] = grow

    row = lax.broadcasted_iota(I32, (c, c), 0)
    col = lax.broadcasted_iota(I32, (c, c), 1)
    causal = row >= col
    strict = row > col
    eye = jnp.where(row == col, 1.0, 0.0)
    gain = norm_ref[...]
    n_chunks = tb // c
    probs = [(ci, h) for ci in range(n_chunks) for h in range(DN_HEADS)]

    decay_last = []
    for p, (ci, h) in enumerate(probs):
        rows = slice(ci * c, (ci + 1) * c)
        q = act_ref[h, rows, :]
        k = act_ref[DN_HEADS + h, rows, :]
        v = act_ref[2 * DN_HEADS + h, rows, :]
        beta = gb_ref[rows, h:h + 1]
        gc = gcc_ref[rows, DN_HEADS + h:DN_HEADS + h + 1]
        gr = gcr_ref[DN_HEADS + h:DN_HEADS + h + 1, rows]
        g_last = gc[c - 1:c, :]
        decay = jnp.where(causal, jnp.exp(jnp.where(causal, gc - gr, 0.0)), 0.0)
        kb = k * beta
        both = _mm_nt(jnp.concatenate([kb, q], axis=0), k)
        a_ref[p] = jnp.where(strict, both[:c] * decay, 0.0)
        qk_ref[p] = (both[c:] * decay).astype(BF16)
        egc = jnp.exp(gc)
        rhs_ref[p] = jnp.concatenate([v * beta, kb * egc], axis=1).astype(BF16)
        wq_ref[p, c:, :] = (q * egc).astype(BF16)
        kdt_ref[p] = (k * jnp.exp(g_last - gc)).T.astype(BF16)
        decay_last.append(jnp.exp(g_last))

    shift = INV_BASE.bit_length() - 1
    blk = (row >> shift) == (col >> shift)
    for p in range(len(probs)):
        diag = jnp.where(blk, a_ref[p], 0.0)
        p_ref[p] = eye - diag
        d_ref[p] = _mm(diag, diag).astype(BF16)
    for it in range(shift - 1):
        for p in range(len(probs)):
            pw = d_ref[p]
            inv = p_ref[p]
            p_ref[p] = inv + _mm(inv, pw)
            if it < shift - 2:
                d_ref[p] = _mm(pw, pw).astype(BF16)
    s = INV_BASE
    while s < c:
        sh = s.bit_length() - 1
        off = ((row >> (sh + 1)) == (col >> (sh + 1))) & ((row >> sh) != (col >> sh))
        for p in range(len(probs)):
            d_ref[p] = _mm(p_ref[p], jnp.where(off, a_ref[p], 0.0)).astype(BF16)
        for p in range(len(probs)):
            inv = p_ref[p]
            p_ref[p] = inv - _mm(d_ref[p], inv)
        s *= 2
    for p in range(len(probs)):
        uw = _mm(p_ref[p], rhs_ref[p])
        u_ref[p] = uw[:, :dh]
        wq_ref[p, :c, :] = uw[:, dh:].astype(BF16)

    for ci in range(n_chunks):
        rows = slice(ci * c, (ci + 1) * c)
        ps = [ci * DN_HEADS + h for h in range(DN_HEADS)]
        s_old = [s_ref[h] for h in range(DN_HEADS)]
        ws = [_mm(wq_ref[p], s_old[h]) for h, p in enumerate(ps)]
        v_new = [u_ref[p] - ws[h][:c] for h, p in enumerate(ps)]
        outs = [ws[h][c:] + _mm(qk_ref[p], v_new[h]) for h, p in enumerate(ps)]
        for h, p in enumerate(ps):
            s_ref[h] = s_old[h] * decay_last[p] + _mm(kdt_ref[p], v_new[h])
        for h in range(DN_HEADS):
            o = _rmsnorm(outs[h], gain)
            zz = z_ref[rows, h * dh:(h + 1) * dh]
            o_ref[rows, h * dh:(h + 1) * dh] = (o * (zz * _sigmoid(zz))).astype(o_ref.dtype)


def _deltanet(proj, gbt, dn_conv, dn_norm, batch, seq, tb=512):
    n = proj.shape[0]
    nt = seq // tb
    c = DN_CHUNK
    n_prob = (tb // c) * DN_HEADS
    n_grp = 3 * DN_HEADS
    grp = lambda j: pl.BlockSpec((tb, DN_HEAD_DIM), lambda b, t, j=j: (b * nt + t, j))
    return pl.pallas_call(
        functools.partial(_deltanet_kernel, tb=tb),
        grid=(batch, nt),
        in_specs=[grp(j) for j in range(n_grp)] + [
            pl.BlockSpec((tb, D_DN), lambda b, t: (b * nt + t, 3)),
            pl.BlockSpec((tb, LANES), lambda b, t: (b * nt + t, COL_GB // LANES)),
            pl.BlockSpec((2 * DN_HEADS, tb), lambda b, t: (0, b * nt + t)),
            pl.BlockSpec((CONV_WIDTH, 3 * D_DN), lambda b, t: (0, 0)),
            pl.BlockSpec((1, DN_HEAD_DIM), lambda b, t: (0, 0)),
        ],
        out_specs=pl.BlockSpec((tb, D_DN), lambda b, t: (b * nt + t, 0)),
        out_shape=jax.ShapeDtypeStruct((n, D_DN), BF16),
        scratch_shapes=[
            pltpu.VMEM((DN_HEADS, DN_HEAD_DIM, DN_HEAD_DIM), F32),
            pltpu.VMEM((SUBLANES, 3 * D_DN), F32),
            pltpu.VMEM((n_grp, tb, DN_HEAD_DIM), F32),
            pltpu.VMEM((tb, LANES), F32),
            pltpu.VMEM((2 * DN_HEADS, tb), F32),
            pltpu.VMEM((n_prob, c, c), F32),
            pltpu.VMEM((n_prob, c, c), F32),
            pltpu.VMEM((n_prob, c, c), BF16),
            pltpu.VMEM((n_prob, c, c), BF16),
            pltpu.VMEM((n_prob, c, 2 * DN_HEAD_DIM), BF16),
            pltpu.VMEM((n_prob, c, DN_HEAD_DIM), F32),
            pltpu.VMEM((n_prob, 2 * c, DN_HEAD_DIM), BF16),
            pltpu.VMEM((n_prob, DN_HEAD_DIM, c), BF16),
        ],
        compiler_params=_cparams(2),
        name="deltanet",
    )(*([proj] * (n_grp + 2)), gbt, dn_conv, dn_norm)


def _gelu_tanh(x):
    return 0.5 * x * (1.0 + jnp.tanh(0.7978845608028654 * (x + 0.044715 * (x * x * x))))


def _rglru_kernel(*refs, tb):
    rx_refs = refs[:RG_BLOCKS]
    (ry_ref, conv_ref, convb_ref, wa_ref, ba_ref, wx_ref, bx_ref, lam_ref,
     o_ref, hc_ref, halo_ref, a_ref, b_ref, h_ref) = refs[RG_BLOCKS:]
    t = pl.program_id(1)
    bw = D_RG // RG_BLOCKS
    seg = tb // SUBLANES

    @pl.when(t == 0)
    def _():
        hc_ref[...] = jnp.zeros_like(hc_ref)
        halo_ref[...] = jnp.zeros_like(halo_ref)

    log_sig = -_softplus(-lam_ref[...])
    rowi = lax.broadcasted_iota(I32, (tb, bw), 0)
    seq_start_row = jnp.where(t == 0, 0, -1)
    for nb in range(RG_BLOCKS):
        cs = slice(nb * bw, (nb + 1) * bw)
        xb = (_causal_conv_segmented(_load_segmented(rx_refs[nb]), halo_ref, cs,
                                     conv_ref[:, cs]) + convb_ref[:, cs])
        r = _sigmoid(_mm(xb, wa_ref[nb]) + ba_ref[:, cs])
        gi = _sigmoid(_mm(xb, wx_ref[nb]) + bx_ref[:, cs])
        log_a = RG_C * r * log_sig[:, cs]
        a = jnp.exp(log_a)
        y = jnp.tanh(-log_a) * (1.0 + a * a)
        mult = jnp.where(y > 0.0, y * lax.rsqrt(y), 0.0)
        mult = jnp.where(rowi == seq_start_row, 1.0, mult)
        a_ref[:, cs] = a
        b_ref[:, cs] = mult * (gi * xb)

    def scan(j, carry):
        ac, bc = carry
        rows = pl.ds(pl.multiple_of(j * SUBLANES, SUBLANES), SUBLANES)
        aj = a_ref[rows, :]
        bc = aj * bc + b_ref[rows, :]
        ac = aj * ac
        a_ref[rows, :] = ac
        b_ref[rows, :] = bc
        return ac, bc

    ac, bc = lax.fori_loop(1, seg, scan, (a_ref[0:SUBLANES, :], b_ref[0:SUBLANES, :]), unroll=7)

    h = hc_ref[...]
    h_in = []
    for s in range(SUBLANES):
        h_in.append(h)
        h = ac[s:s + 1, :] * h + bc[s:s + 1, :]
    hc_ref[...] = h
    h_in = jnp.concatenate(h_in, axis=0)

    for j in range(seg):
        rows = slice(j * SUBLANES, (j + 1) * SUBLANES)
        hj = a_ref[rows, :] * h_in + b_ref[rows, :]
        for nb in range(RG_BLOCKS):
            h_ref[nb, rows, :] = hj[:, nb * bw:(nb + 1) * bw]
    g = seg // SUBLANES
    for nb in range(RG_BLOCKS):
        cs = slice(nb * bw, (nb + 1) * bw)
        h = jnp.concatenate(
            [h_ref[nb, pl.ds((i % g) * SUBLANES * SUBLANES + i // g, SUBLANES, stride=SUBLANES), :]
             for i in range(seg)], axis=0)
        o_ref[:, cs] = (h * _gelu_tanh(ry_ref[:, cs])).astype(o_ref.dtype)


def _rglru(proj, rg_conv, rg_conv_b, w_a, b_a, w_x, b_x, lam, batch, seq, tb=512):
    n = proj.shape[0]
    nt = seq // tb
    bw = D_RG // RG_BLOCKS
    full = lambda shape: pl.BlockSpec(shape, lambda b, t: (0,) * len(shape))
    return pl.pallas_call(
        functools.partial(_rglru_kernel, tb=tb),
        grid=(batch, nt),
        in_specs=[pl.BlockSpec((tb, bw), lambda b, t, j=j: (b * nt + t, COL_RX // bw + j))
                  for j in range(RG_BLOCKS)] + [
            pl.BlockSpec((tb, D_RG), lambda b, t: (b * nt + t, COL_RY // D_RG)),
            full((CONV_WIDTH, D_RG)), full((1, D_RG)),
            full((RG_BLOCKS, bw, bw)), full((1, D_RG)),
            full((RG_BLOCKS, bw, bw)), full((1, D_RG)),
            full((1, D_RG)),
        ],
        out_specs=pl.BlockSpec((tb, D_RG), lambda b, t: (b * nt + t, 0)),
        out_shape=jax.ShapeDtypeStruct((n, D_RG), BF16),
        scratch_shapes=[
            pltpu.VMEM((1, D_RG), F32),
            pltpu.VMEM(((CONV_WIDTH - 1) * SUBLANES, D_RG), F32),
            pltpu.VMEM((tb, D_RG), F32),
            pltpu.VMEM((tb, D_RG), F32),
            pltpu.VMEM((RG_BLOCKS, tb, bw), F32),
        ],
        compiler_params=_cparams(2),
        name="rglru",
    )(*([proj] * (RG_BLOCKS + 1)), rg_conv, rg_conv_b, w_a, b_a, w_x, b_x, lam)


def _mem_kv_kernel(m_ref, g_ref, w_ref, o_ref):
    mn = _rmsnorm(m_ref[...], g_ref[...]).astype(BF16)
    for c0 in range(0, o_ref.shape[1], 512):
        o_ref[:, c0:c0 + 512] = jnp.dot(mn, w_ref[:, c0:c0 + 512].astype(BF16),
                                        preferred_element_type=F32).astype(o_ref.dtype)


def _mem_kv(mem2, gain, w_ckv):
    n, d = mem2.shape
    return pl.pallas_call(
        _mem_kv_kernel,
        grid=(1,),
        in_specs=[
            pl.BlockSpec((n, d), lambda i: (0, 0)),
            pl.BlockSpec((1, d), lambda i: (0, 0)),
            pl.BlockSpec((d, 2 * d), lambda i: (0, 0)),
        ],
        out_specs=pl.BlockSpec((n, 2 * d), lambda i: (0, 0)),
        out_shape=jax.ShapeDtypeStruct((n, 2 * d), BF16),
        compiler_params=_cparams(1),
        name="mem_kv",
    )(mem2, gain, w_ckv)


ROUTE_E = 0
ROUTE_RANK = TOP_K
ROUTE_GATE = 2 * TOP_K
ROUTE_ROWS = 16


def _post_mix_kernel(dn_ref, rg_ref, x_ref, wo_f32, gx_ref, wq_f32, kv_ref, wco_f32, gm_ref,
                     wrh_ref, wrl_ref, br_ref, h2_ref, xn_ref, route_ref, routet_ref, cnt_ref,
                     carry_ref, wo_ref, wq_ref, wco_ref):
    i = pl.program_id(0)
    tm, d = x_ref.shape
    hd = d // XA_HEADS

    @pl.when(i == 0)
    def _():
        carry_ref[...] = jnp.zeros_like(carry_ref)
        wo_ref[...] = wo_f32[...].astype(BF16)
        wq_ref[...] = wq_f32[...].astype(BF16)
        wco_ref[...] = wco_f32[...].astype(BF16)

    h1 = (x_ref[...] + jnp.dot(dn_ref[...], wo_ref[0:D_DN, :], preferred_element_type=F32)
          + jnp.dot(rg_ref[...], wo_ref[D_DN:, :], preferred_element_type=F32))

    hn = _rmsnorm(h1, gx_ref[...]).astype(BF16)
    q = jnp.dot(hn, wq_ref[...], preferred_element_type=F32)
    heads = []
    for hh in range(XA_HEADS):
        cs = slice(hh * hd, (hh + 1) * hd)
        s = _mm_nt(q[:, cs], kv_ref[:, cs]) * (hd ** -0.5)
        p = jnp.exp(s - jnp.max(s, axis=-1, keepdims=True))
        p = p / jnp.sum(p, axis=-1, keepdims=True)
        heads.append(_mm(p, kv_ref[:, d + hh * hd:d + (hh + 1) * hd]).astype(BF16))
    o = jnp.concatenate(heads, axis=1)
    h2 = h1 + jnp.dot(o, wco_ref[...], preferred_element_type=F32)
    h2_ref[...] = h2

    xn = _rmsnorm(h2, gm_ref[...])
    _store_row_tiles(xn_ref, _pack_bf16_pairs(xn))
    logits = _mm_split(xn, wrh_ref[...], wrl_ref[...]) + br_ref[...]
    lg = logits.T[0:N_EXPERTS, :]
    eidx = lax.broadcasted_iota(I32, lg.shape, 0).astype(F32)
    neg = jnp.float32(-jnp.inf)
    vals, idxs, hots = [], [], []
    for _ in range(TOP_K):
        m = jnp.max(lg, axis=0, keepdims=True)
        idx = jnp.min(jnp.where(lg == m, eidx, float(N_EXPERTS)), axis=0, keepdims=True)
        hot = eidx == idx
        lg = jnp.where(hot, neg, lg)
        vals.append(m)
        idxs.append(idx)
        hots.append(hot)
    es = [jnp.exp(v - vals[0]) for v in vals]
    den = es[0] + es[1] + es[2] + es[3]
    gates = [e / den for e in es]

    chosen = jnp.zeros(lg.shape, F32)
    for hot in hots:
        chosen = chosen + jnp.where(hot, 1.0, 0.0)
    r2 = lax.broadcasted_iota(I32, (tm, tm), 0)
    c2 = lax.broadcasted_iota(I32, (tm, tm), 1)
    before = _mm(chosen, jnp.where(r2 < c2, 1.0, 0.0)) + carry_ref[:, 0:1]
    ranks = [jnp.sum(jnp.where(hot, before, 0.0), axis=0, keepdims=True) for hot in hots]
    carry_ref[...] = carry_ref[...] + jnp.sum(chosen, axis=1, keepdims=True)
    cnt_ref[...] = carry_ref[...]

    row = lax.broadcasted_iota(I32, (ROUTE_ROWS, tm), 0)
    rect = jnp.zeros((ROUTE_ROWS, tm), F32)
    for kk in range(TOP_K):
        rect = jnp.where(row == ROUTE_E + kk, idxs[kk], rect)
        rect = jnp.where(row == ROUTE_RANK + kk, ranks[kk], rect)
        rect = jnp.where(row == ROUTE_GATE + kk, gates[kk], rect)
    routet_ref[...] = rect
    route_ref[...] = jnp.concatenate(
        [rect, jnp.zeros((LANES - ROUTE_ROWS, tm), F32)], axis=0).T


def _post_mix(dn, rg, x2, w_out, g_cross, w_cq, kv, w_co, g_moe, w_r_hi, w_r_lo, b_r, seq, mem_len,
              tm=512):
    n, d = x2.shape
    per_b = seq // tm
    rt_x = d // 2 // LANES
    full = lambda shape: pl.BlockSpec(shape, lambda i: (0,) * len(shape))
    once = lambda shape: pl.BlockSpec(shape, lambda i: (0,) * len(shape),
                                      pipeline_mode=pl.Buffered(1))
    return pl.pallas_call(
        _post_mix_kernel,
        grid=(n // tm,),
        in_specs=[
            pl.BlockSpec((tm, D_DN), lambda i: (i, 0)),
            pl.BlockSpec((tm, D_RG), lambda i: (i, 0)),
            pl.BlockSpec((tm, d), lambda i: (i, 0)),
            once((d, d)), full((1, d)), once((d, d)),
            pl.BlockSpec((mem_len, 2 * d), lambda i: (i // per_b, 0)),
            once((d, d)), full((1, d)),
            full((d, LANES)), full((d, LANES)), full((1, LANES)),
        ],
        out_specs=[
            pl.BlockSpec((tm, d), lambda i: (i, 0)),
            pl.BlockSpec((tm * rt_x, LANES), lambda i: (i, 0)),
            pl.BlockSpec((tm, LANES), lambda i: (i, 0)),
            pl.BlockSpec((ROUTE_ROWS, tm), lambda i: (0, i)),
            pl.BlockSpec((N_EXPERTS, LANES), lambda i: (0, 0)),
        ],
        out_shape=[
            jax.ShapeDtypeStruct((n, d), F32),
            jax.ShapeDtypeStruct((n * rt_x, LANES), U32),
            jax.ShapeDtypeStruct((n, LANES), F32),
            jax.ShapeDtypeStruct((ROUTE_ROWS, n), F32),
            jax.ShapeDtypeStruct((N_EXPERTS, LANES), F32),
        ],
        scratch_shapes=[pltpu.VMEM((N_EXPERTS, LANES), F32),
                        pltpu.VMEM((d, d), BF16), pltpu.VMEM((d, d), BF16), pltpu.VMEM((d, d), BF16)],
        compiler_params=_cparams(1),
        name="post_mix",
    )(dn, rg, x2, w_out, g_cross, w_cq, kv, w_co, g_moe, w_r_hi, w_r_lo, b_r)


def _dest_kernel(start_ref, routet_ref, dest_ref):
    e = routet_ref[ROUTE_E:ROUTE_E + SUBLANES, :]
    rank = routet_ref[ROUTE_RANK:ROUTE_RANK + SUBLANES, :]
    base = jnp.zeros(e.shape, F32)
    for j in range(N_EXPERTS):
        base = jnp.where(e == float(j), start_ref[j].astype(F32), base)
    row = lax.broadcasted_iota(I32, e.shape, 0)
    dest_ref[...] = jnp.where(row < TOP_K, (base + rank).astype(I32), 0)


def _dest(pad_start, route_t, tm=2048):
    n = route_t.shape[1]
    grid_spec = pltpu.PrefetchScalarGridSpec(
        num_scalar_prefetch=1,
        grid=(n // tm,),
        in_specs=[pl.BlockSpec((ROUTE_ROWS, tm), lambda i, st: (0, i))],
        out_specs=pl.BlockSpec((SUBLANES, tm), lambda i, st: (0, i)),
    )
    return pl.pallas_call(
        _dest_kernel,
        grid_spec=grid_spec,
        out_shape=jax.ShapeDtypeStruct((SUBLANES, n), I32),
        compiler_params=_cparams(1),
        name="dest",
    )(pad_start, route_t)


def _row_tile(ref, r, rt):
    return ref.at[pl.ds(pl.multiple_of(r * rt, rt), rt)]


def _dispatch_kernel(meta_ref, dest_ref, xn_ref, xs_ref, zero_ref, ring_ref, sem, zsem,
                     *, tm, rt, n_blocks):
    i = pl.program_id(0)
    bm = MOE_BM * rt
    ring = ring_ref.at[i & 1]
    ring[...] = xn_ref[...]

    def per_token(tk, carry):
        src = _row_tile(ring, tk, rt)
        for kk in range(TOP_K):
            r = dest_ref[kk, tk]
            pltpu.make_async_copy(src, _row_tile(xs_ref, r, rt),
                                  sem.at[i & 1]).start(priority=kk % 2)
        return carry

    lax.fori_loop(0, tm, per_token, 0, unroll=4)

    @pl.when(i == 0)
    def _():
        zero_ref[...] = jnp.zeros_like(zero_ref)

        def pad_copy(r):
            return pltpu.make_async_copy(zero_ref.at[pl.ds(0, rt)], _row_tile(xs_ref, r, rt), zsem)

        def per_expert(e, total):
            r0 = meta_ref[e]
            cnt = meta_ref[N_EXPERTS + e]

            def group(j, carry):
                for u in range(PAD_UNROLL):
                    pad_copy(r0 + j * PAD_UNROLL + u).start(priority=u % 2)
                return carry

            def one(r, carry):
                pad_copy(r0 + r).start()
                return carry

            groups = cnt // PAD_UNROLL
            lax.fori_loop(0, groups, group, 0)
            lax.fori_loop(groups * PAD_UNROLL, cnt, one, 0)
            return total + cnt

        total = lax.fori_loop(0, N_EXPERTS, per_expert, 0)

        def drain_block(j, carry):
            pltpu.make_async_copy(zero_ref, xs_ref.at[pl.ds(0, bm)], zsem).wait()
            return carry

        def drain_group(j, carry):
            pltpu.make_async_copy(zero_ref.at[pl.ds(0, PAD_UNROLL * rt)],
                                  xs_ref.at[pl.ds(0, PAD_UNROLL * rt)], zsem).wait()
            return carry

        def drain_row(j, carry):
            pad_copy(0).wait()
            return carry

        lax.fori_loop(0, total // MOE_BM, drain_block, 0)
        rest = total % MOE_BM
        lax.fori_loop(0, rest // PAD_UNROLL, drain_group, 0)
        lax.fori_loop(0, rest % PAD_UNROLL, drain_row, 0)

        def tail_copy(b):
            return pltpu.make_async_copy(
                zero_ref, xs_ref.at[pl.ds(pl.multiple_of(b * bm, bm), bm)], zsem)

        n_used = meta_ref[2 * N_EXPERTS]

        def tail(b, carry):
            tail_copy(b).start()
            return carry

        lax.fori_loop(n_used, n_blocks, tail, 0)

        def tail_drain(b, carry):
            tail_copy(0).wait()
            return carry

        lax.fori_loop(n_used, n_blocks, tail_drain, 0)

    def wait_step(parity):
        for _ in range(TOP_K):
            pltpu.make_async_copy(ring_ref.at[parity], xs_ref.at[pl.ds(0, tm * rt)],
                                  sem.at[parity]).wait()

    @pl.when(i > 0)
    def _():
        wait_step(1 - (i & 1))

    @pl.when(i == pl.num_programs(0) - 1)
    def _():
        wait_step(i & 1)


def _dispatch(meta, dest, xn_t, n_tok, n_rows, tm=256):
    rt = xn_t.shape[0] // n_tok
    n_blocks = n_rows // MOE_BM
    grid_spec = pltpu.PrefetchScalarGridSpec(
        num_scalar_prefetch=1,
        grid=(n_tok // tm,),
        in_specs=[
            pl.BlockSpec((SUBLANES, tm), lambda i, meta: (0, i), memory_space=pltpu.SMEM),
            pl.BlockSpec((tm * rt, LANES), lambda i, meta: (i, 0)),
        ],
        out_specs=pl.BlockSpec(memory_space=pl.ANY),
        scratch_shapes=[
            pltpu.VMEM((MOE_BM * rt, LANES), xn_t.dtype),
            pltpu.VMEM((2, tm * rt, LANES), xn_t.dtype),
            pltpu.SemaphoreType.DMA((2,)),
            pltpu.SemaphoreType.DMA,
        ],
    )
    return pl.pallas_call(
        functools.partial(_dispatch_kernel, tm=tm, rt=rt, n_blocks=n_blocks),
        grid_spec=grid_spec,
        out_shape=jax.ShapeDtypeStruct((n_rows * rt, LANES), xn_t.dtype),
        compiler_params=_cparams(1),
        name="dispatch",
    )(meta, dest, xn_t)


def _experts_kernel(be_ref, nu_ref, nxt_ref, par_ref, x_ref, bg_ref, bu_ref, bd_ref,
                    wg_hbm, wu_hbm, wd_hbm, y_ref, wf_ref, wb_ref, sem):
    i = pl.program_id(0)
    used = i < nu_ref[0]
    e = be_ref[i]
    changed = (i == 0) | (e != be_ref[jnp.maximum(i - 1, 0)])

    def weight_copies(expert, slot):
        return [pltpu.make_async_copy(w.at[0, expert], wf_ref.at[slot, j], sem.at[slot])
                for j, w in enumerate((wg_hbm, wu_hbm, wd_hbm))]

    @pl.when(i == 0)
    def _():
        for cp in weight_copies(e, par_ref[e]):
            cp.start()

    @pl.when(used & changed)
    def _():
        slot = par_ref[e]
        for cp in weight_copies(e, slot):
            cp.wait()
        nxt = nxt_ref[e]

        @pl.when(nxt < N_EXPERTS)
        def _():
            for cp in weight_copies(nxt, 1 - slot):
                cp.start()

        for j in range(3):
            wb_ref[j] = wf_ref[slot, j].astype(BF16)

    @pl.when(used)
    def _():
        x = _unpack_pairs_f32(_load_row_tiles(x_ref, MOE_BM)).astype(BF16)
        gt = jnp.minimum(jnp.dot(x, wb_ref[0], preferred_element_type=F32) + bg_ref[...],
                         SWIGLU_LIMIT)
        up = jnp.clip(jnp.dot(x, wb_ref[1], preferred_element_type=F32) + bu_ref[...],
                      -SWIGLU_LIMIT, SWIGLU_LIMIT)
        hid = (up + 1.0) * (gt * _sigmoid(SWIGLU_ALPHA * gt))
        y = jnp.dot(hid.astype(BF16), wb_ref[2], preferred_element_type=F32) + bd_ref[...]
        _store_row_tiles(y_ref, _pack_bf16_pairs(y))

    @pl.when(jnp.logical_not(used))
    def _():
        y_ref[...] = jnp.zeros_like(y_ref)


def _experts(block_e, n_used, next_e, parity, xs_t, w_gate, b_gate, w_up, b_up, w_down, b_down):
    d, d_ff = w_gate.shape[2:]
    assert d == d_ff, "weight staging buffers assume square expert matrices"
    rt = d // 2 // LANES
    n_blocks = xs_t.shape[0] // (MOE_BM * rt)
    bspec = lambda m: pl.BlockSpec((None, 1, m), lambda i, be, nu, nx, pa: (be[i], 0, 0))
    hbm = pl.BlockSpec(memory_space=pl.ANY)
    grid_spec = pltpu.PrefetchScalarGridSpec(
        num_scalar_prefetch=4,
        grid=(n_blocks,),
        in_specs=[
            pl.BlockSpec((MOE_BM * rt, LANES),
                         lambda i, be, nu, nx, pa: (jnp.maximum(jnp.minimum(i, nu[0] - 1), 0), 0)),
            bspec(d_ff), bspec(d_ff), bspec(d),
            hbm, hbm, hbm,
        ],
        out_specs=pl.BlockSpec((MOE_BM * rt, LANES), lambda i, be, nu, nx, pa: (i, 0)),
        scratch_shapes=[
            pltpu.VMEM((2, 3, d, d_ff), F32),
            pltpu.VMEM((3, d, d_ff), BF16),
            pltpu.SemaphoreType.DMA((2,)),
        ],
    )
    return pl.pallas_call(
        _experts_kernel,
        grid_spec=grid_spec,
        out_shape=jax.ShapeDtypeStruct(xs_t.shape, U32),
        compiler_params=_cparams(1),
        name="experts",
    )(block_e, n_used, next_e, parity, xs_t, b_gate, b_up, b_down, w_gate, w_up, w_down)


def _combine_kernel(dcur_ref, dnext_ref, h2_ref, route_ref, gain_ref, ys_ref, o_ref,
                    buf_ref, sem, *, tm, rt):
    i = pl.program_id(0)
    nsteps = pl.num_programs(0)
    slot = i & 1

    def row_copy(dref, s, tk, kk):
        return pltpu.make_async_copy(_row_tile(ys_ref, dref[kk, tk], rt),
                                     _row_tile(buf_ref.at[s, kk], tk, rt), sem.at[s])

    def wait_slot(s):
        for kk in range(TOP_K):
            pltpu.make_async_copy(ys_ref.at[pl.ds(0, tm * rt)], buf_ref.at[s, kk],
                                  sem.at[s]).wait()

    @pl.when(i == 0)
    def _():
        def per_token(tk, carry):
            for kk in range(TOP_K):
                row_copy(dcur_ref, 0, tk, kk).start(priority=kk % 2)
            return carry
        lax.fori_loop(0, tm, per_token, 0, unroll=4)

    wait_slot(slot)
    for tk in range(tm):
        for kk in range(TOP_K):
            row_copy(dnext_ref, 1 - slot, tk, kk).start(priority=kk % 2)
    rec = route_ref[...]
    acc = h2_ref[...]
    for kk in range(TOP_K):
        acc = acc + (rec[:, ROUTE_GATE + kk:ROUTE_GATE + kk + 1]
                     * _unpack_pairs_f32(_load_row_tiles(buf_ref.at[slot, kk], tm)))
    o_ref[...] = _rmsnorm(acc, gain_ref[...])

    @pl.when(i == nsteps - 1)
    def _():
        wait_slot(1 - slot)


def _combine(dest, h2, route, gain, ys_t, tm=128):
    n, d = h2.shape
    nsteps = n // tm
    rt = d // 2 // LANES
    return pl.pallas_call(
        functools.partial(_combine_kernel, tm=tm, rt=rt),
        grid=(nsteps,),
        in_specs=[
            pl.BlockSpec((SUBLANES, tm), lambda i: (0, i), memory_space=pltpu.SMEM),
            pl.BlockSpec((SUBLANES, tm), lambda i: (0, jnp.minimum(i + 1, nsteps - 1)),
                         memory_space=pltpu.SMEM),
            pl.BlockSpec((tm, d), lambda i: (i, 0)),
            pl.BlockSpec((tm, LANES), lambda i: (i, 0)),
            pl.BlockSpec((1, d), lambda i: (0, 0)),
            pl.BlockSpec(memory_space=pl.ANY),
        ],
        out_specs=pl.BlockSpec((tm, d), lambda i: (i, 0)),
        out_shape=jax.ShapeDtypeStruct((n, d), F32),
        scratch_shapes=[
            pltpu.VMEM((2, TOP_K, tm * rt, LANES), U32),
            pltpu.SemaphoreType.DMA((2,)),
        ],
        compiler_params=_cparams(1),
        name="combine",
    )(dest, dest, h2, route, gain, ys_t)


def kernel(x, mem, norm_mix, w_in, dn_conv, dn_a_log, dn_dt_bias, dn_norm, rg_conv, rg_conv_b, rg_w_a, rg_b_a, rg_w_x, rg_b_x, rg_lambda, w_out, norm_cross, norm_mem, w_cq, w_ckv, w_co, norm_moe, w_router, b_router, w_gate, b_gate, w_up, b_up, w_down, b_down, norm_final):
    batch, seq, d = x.shape
    mem_len = mem.shape[1]
    n = batch * seq
    assert w_in.shape[0] == 1, "single-layer trunk"
    x2 = x.reshape(n, d)

    wi = w_in[0]
    n_gate = 2 * DN_HEADS
    w_cat = jnp.concatenate(
        [wi[:, :4 * D_DN], wi[:, 4 * D_DN + n_gate:],
         jnp.pad(wi[:, 4 * D_DN:4 * D_DN + n_gate], ((0, 0), (0, LANES - n_gate)))],
        axis=1).astype(BF16)
    prow = (jnp.zeros((SUBLANES, LANES), F32)
            .at[0, DN_HEADS:n_gate].set(dn_a_log[0]).at[1, DN_HEADS:n_gate].set(dn_dt_bias[0]))
    pcol = (jnp.zeros((n_gate, LANES), F32)
            .at[DN_HEADS:, 0].set(dn_a_log[0]).at[DN_HEADS:, 1].set(dn_dt_bias[0]))

    proj, gbt = _in_proj(x2, norm_mix, w_cat, prow, pcol)
    dn = _deltanet(proj, gbt, dn_conv[0], dn_norm, batch, seq)
    rg = _rglru(proj, rg_conv[0], rg_conv_b, rg_w_a[0], rg_b_a[0].reshape(1, D_RG),
                rg_w_x[0], rg_b_x[0].reshape(1, D_RG), rg_lambda, batch, seq)
    kv = _mem_kv(mem.reshape(batch * mem_len, d), norm_mem, w_ckv[0])

    w_r = jnp.pad(w_router[0], ((0, 0), (0, LANES - N_EXPERTS)))
    w_r_hi = w_r.astype(BF16)
    w_r_lo = (w_r - w_r_hi.astype(F32)).astype(BF16)
    b_r = jnp.pad(b_router, ((0, 0), (0, LANES - N_EXPERTS)))
    h2, xn, route, route_t, counts = _post_mix(
        dn, rg, x2, w_out[0], norm_cross, w_cq[0], kv, w_co[0], norm_moe, w_r_hi, w_r_lo, b_r,
        seq, mem_len)

    n_blocks = n * TOP_K // MOE_BM + N_EXPERTS
    n_rows = n_blocks * MOE_BM
    cnt = counts[:, 0].astype(I32)
    padded = (cnt + MOE_BM - 1) // MOE_BM * MOE_BM
    pad_end = jnp.cumsum(padded)
    pad_start = pad_end - padded
    n_used = (pad_end[-1:] // MOE_BM).astype(I32)
    block_e = jnp.minimum(
        jnp.sum(pad_end[None, :] <= (jnp.arange(n_blocks, dtype=I32) * MOE_BM)[:, None], axis=1),
        N_EXPERTS - 1).astype(I32)
    meta = jnp.concatenate([pad_start + cnt, padded - cnt, n_used]).astype(I32)

    dest = _dest(pad_start.astype(I32), route_t)
    xs = _dispatch(meta, dest, xn, n, n_rows)
    has = cnt > 0
    eid = jnp.where(has, jnp.arange(N_EXPERTS, dtype=I32), N_EXPERTS)
    after = lax.cummin(eid, axis=0, reverse=True)
    next_e = jnp.concatenate([after[1:], jnp.full((1,), N_EXPERTS, I32)]).astype(I32)
    parity = ((jnp.cumsum(has.astype(I32)) - 1) & 1).astype(I32)
    ys = _experts(block_e, n_used, next_e, parity, xs, w_gate, b_gate[0][:, None, :], w_up,
                  b_up[0][:, None, :], w_down, b_down[0][:, None, :])
    out = _combine(dest, h2, route, norm_final.reshape(1, d), ys)
    return out.reshape(batch, seq, d)
```

```python
import functools

import jax
import jax.numpy as jnp
from jax import lax
from jax.experimental import pallas as pl
from jax.experimental.pallas import tpu as pltpu

F32 = jnp.float32
BF16 = jnp.bfloat16
I32 = jnp.int32
U32 = jnp.uint32

EPS = 1e-6
LANES = 128
SUBLANES = 8
VMEM_LIMIT = 48 * 1024 * 1024

DN_HEADS = 4
DN_HEAD_DIM = 128
D_DN = DN_HEADS * DN_HEAD_DIM
D_RG = 512
RG_BLOCKS = 4
RG_C = 8.0
CONV_WIDTH = 4
XA_HEADS = 4
N_EXPERTS = 32
TOP_K = 4
SWIGLU_LIMIT = 7.0
SWIGLU_ALPHA = 1.702

DN_CHUNK = 128
INV_BASE = 16
MOE_BM = 512
PAD_UNROLL = 8

COL_RX = 4 * D_DN
COL_RY = COL_RX + D_RG
COL_GB = COL_RY + D_RG
PROJ_W = COL_GB + LANES


def _cparams(n_axes=1):
    return pltpu.CompilerParams(
        dimension_semantics=("arbitrary",) * n_axes, vmem_limit_bytes=VMEM_LIMIT)


def _mm(a, b):
    return jnp.dot(a.astype(BF16), b.astype(BF16), preferred_element_type=F32)


def _mm_nt(a, b):
    return lax.dot_general(a.astype(BF16), b.astype(BF16), (((1,), (1,)), ((), ())),
                           preferred_element_type=F32)


def _rmsnorm(x, g):
    return x * lax.rsqrt(jnp.mean(x * x, axis=-1, keepdims=True) + EPS) * g


def _sigmoid(x):
    return 0.5 * jnp.tanh(0.5 * x) + 0.5


def _mm_split(a, b_hi, b_lo):
    a_hi = a.astype(BF16)
    a_lo = (a - a_hi.astype(F32)).astype(BF16)
    return (jnp.dot(a_hi, b_hi, preferred_element_type=F32)
            + jnp.dot(a_hi, b_lo, preferred_element_type=F32)
            + jnp.dot(a_lo, b_hi, preferred_element_type=F32))


def _softplus(x):
    return jnp.maximum(x, 0.0) + jnp.log1p(jnp.exp(-jnp.abs(x)))


def _load_row_tiles(ref, rows):
    rt = ref.shape[0] // rows
    return jnp.concatenate([ref[pl.ds(s, rows, stride=rt), :] for s in range(rt)], axis=1)


def _store_row_tiles(ref, val):
    rows, w = val.shape
    rt = w // LANES
    for s in range(rt):
        ref[pl.ds(s, rows, stride=rt), :] = val[:, s * LANES:(s + 1) * LANES]


def _pack_bf16_pairs(x):
    half = x.shape[1] // 2
    hi = pltpu.bitcast(x[:, :half].astype(BF16).astype(F32), U32)
    lo = pltpu.bitcast(x[:, half:].astype(BF16).astype(F32), U32)
    return hi | (lo >> 16)


def _unpack_pairs_f32(p):
    hi = pltpu.bitcast(p & jnp.uint32(0xFFFF0000), F32)
    lo = pltpu.bitcast(p << 16, F32)
    return jnp.concatenate([hi, lo], axis=1)


def _load_segmented(ref):
    seg = ref.shape[0] // SUBLANES
    return jnp.concatenate([ref[pl.ds(j, SUBLANES, stride=seg), :] for j in range(seg)], axis=0)


def _causal_conv_segmented(xp, halo_ref, cs, w):
    tb = xp.shape[0]
    ng = w.shape[0] - 1
    prev = halo_ref[:, cs]
    last = xp[tb - ng * SUBLANES:, :]
    halo_ref[:, cs] = last
    sub = lax.broadcasted_iota(I32, (SUBLANES, xp.shape[1]), 0)
    groups = []
    for g in range(ng):
        rows = slice(g * SUBLANES, (g + 1) * SUBLANES)
        groups.append(jnp.where(sub == 0, pltpu.roll(prev[rows], 1, 0),
                                pltpu.roll(last[rows], 1, 0)))
    ext = jnp.concatenate(groups + [xp], axis=0)
    y = w[ng:ng + 1] * xp
    for k in range(1, ng + 1):
        y = y + w[ng - k:ng - k + 1] * ext[(ng - k) * SUBLANES:(ng - k) * SUBLANES + tb]
    return y


def _in_proj_kernel(x_ref, g_ref, w_ref, prow_ref, pcol_ref, proj_ref, gbt_ref):
    u = _rmsnorm(x_ref[...], g_ref[...]).astype(BF16)
    for c0 in range(0, COL_GB, 512):
        proj_ref[:, c0:c0 + 512] = jnp.dot(u, w_ref[:, c0:c0 + 512], preferred_element_type=F32)
    ba = jnp.dot(u, w_ref[:, COL_GB:PROJ_W], preferred_element_type=F32)
    lane = lax.broadcasted_iota(I32, ba.shape, 1)
    g = -jnp.exp(prow_ref[0:1, :]) * _softplus(ba + prow_ref[1:2, :])
    proj_ref[:, COL_GB:PROJ_W] = jnp.where(lane < DN_HEADS, _sigmoid(ba), g)
    bat = lax.dot_general(w_ref[:, COL_GB:PROJ_W], u, (((0,), (1,)), ((), ())),
                          preferred_element_type=F32)[0:2 * DN_HEADS, :]
    row = lax.broadcasted_iota(I32, bat.shape, 0)
    gt = -jnp.exp(pcol_ref[:, 0:1]) * _softplus(bat + pcol_ref[:, 1:2])
    gbt_ref[...] = jnp.where(row < DN_HEADS, _sigmoid(bat), gt)


def _in_proj(x2, gain, w_cat, prow, pcol, tm=512):
    n, d = x2.shape
    return pl.pallas_call(
        _in_proj_kernel,
        grid=(n // tm,),
        in_specs=[
            pl.BlockSpec((tm, d), lambda i: (i, 0)),
            pl.BlockSpec((1, d), lambda i: (0, 0)),
            pl.BlockSpec((d, PROJ_W), lambda i: (0, 0)),
            pl.BlockSpec((SUBLANES, LANES), lambda i: (0, 0)),
            pl.BlockSpec((2 * DN_HEADS, LANES), lambda i: (0, 0)),
        ],
        out_specs=[
            pl.BlockSpec((tm, PROJ_W), lambda i: (i, 0)),
            pl.BlockSpec((2 * DN_HEADS, tm), lambda i: (0, i)),
        ],
        out_shape=[
            jax.ShapeDtypeStruct((n, PROJ_W), F32),
            jax.ShapeDtypeStruct((2 * DN_HEADS, n), F32),
        ],
        compiler_params=_cparams(1),
        name="in_proj",
    )(x2, gain, w_cat, prow, pcol)


def _deltanet_kernel(*refs, tb):
    n_grp = 3 * DN_HEADS
    qkv_refs = refs[:n_grp]
    (z_ref, gb_ref, gbt_ref, conv_ref, norm_ref, o_ref,
     s_ref, halo_ref, act_ref, gcc_ref, gcr_ref,
     a_ref, p_ref, d_ref, qk_ref, rhs_ref, u_ref, wq_ref, kdt_ref) = refs[n_grp:]
    t = pl.program_id(1)
    c = DN_CHUNK
    dh = DN_HEAD_DIM

    @pl.when(t == 0)
    def _():
        s_ref[...] = jnp.zeros_like(s_ref)
        halo_ref[...] = jnp.zeros_like(halo_ref)

    sub = lax.broadcasted_iota(I32, (SUBLANES, dh), 0)
    for grp in range(n_grp):
        cs = slice(grp * dh, (grp + 1) * dh)
        x = qkv_refs[grp][...]
        prev = halo_ref[:, cs]
        halo_ref[:, cs] = x[tb - SUBLANES:, :]
        y = conv_ref[CONV_WIDTH - 1:CONV_WIDTH, cs] * x
        for k in range(1, CONV_WIDTH):
            xs = pltpu.roll(x, k, 0)
            head = jnp.where(sub < k, pltpu.roll(prev, k, 0), xs[:SUBLANES])
            xs = jnp.concatenate([head, xs[SUBLANES:]], axis=0)
            y = y + conv_ref[CONV_WIDTH - 1 - k:CONV_WIDTH - k, cs] * xs
        y = y * _sigmoid(y)
        if grp < 2 * DN_HEADS:
            y = y * lax.rsqrt(jnp.sum(y * y, axis=-1, keepdims=True) + EPS)
        if grp < DN_HEADS:
            y = y * (dh ** -0.5)
        act_ref[grp] = y

    gcol = gb_ref[...]
    rpos = lax.broadcasted_iota(I32, gcol.shape, 0) & (c - 1)
    d = 1
    while d < c:
        gcol = gcol + jnp.where(rpos >= d, pltpu.roll(gcol, d, 0), 0.0)
        d *= 2
    gcc_ref[...] = gcol
    grow = gbt_ref[...]
    lpos = lax.broadcasted_iota(I32, grow.shape, 1) & (c - 1)
    d = 1
    while d < c:
        grow = grow + jnp.where(lpos >= d, pltpu.roll(grow, d, 1), 0.0)
        d *= 2
    gcr_ref[...] = grow

    row = lax.broadcasted_iota(I32, (c, c), 0)
    col = lax.broadcasted_iota(I32, (c, c), 1)
    causal = row >= col
    strict = row > col
    eye = jnp.where(row == col, 1.0, 0.0)
    gain = norm_ref[...]
    n_chunks = tb // c
    probs = [(ci, h) for ci in range(n_chunks) for h in range(DN_HEADS)]

    decay_last = []
    for p, (ci, h) in enumerate(probs):
        rows = slice(ci * c, (ci + 1) * c)
        q = act_ref[h, rows, :]
        k = act_ref[DN_HEADS + h, rows, :]
        v = act_ref[2 * DN_HEADS + h, rows, :]
        beta = gb_ref[rows, h:h + 1]
        gc = gcc_ref[rows, DN_HEADS + h:DN_HEADS + h + 1]
        gr = gcr_ref[DN_HEADS + h:DN_HEADS + h + 1, rows]
        g_last = gc[c - 1:c, :]
        decay = jnp.where(causal, jnp.exp(jnp.where(causal, gc - gr, 0.0)), 0.0)
        kb = k * beta
        both = _mm_nt(jnp.concatenate([kb, q], axis=0), k)
        a_ref[p] = jnp.where(strict, both[:c] * decay, 0.0)
        qk_ref[p] = (both[c:] * decay).astype(BF16)
        egc = jnp.exp(gc)
        rhs_ref[p] = jnp.concatenate([v * beta, kb * egc], axis=1).astype(BF16)
        wq_ref[p, c:, :] = (q * egc).astype(BF16)
        kdt_ref[p] = (k * jnp.exp(g_last - gc)).T.astype(BF16)
        decay_last.append(jnp.exp(g_last))

    shift = INV_BASE.bit_length() - 1
    blk = (row >> shift) == (col >> shift)
    for p in range(len(probs)):
        diag = jnp.where(blk, a_ref[p], 0.0)
        p_ref[p] = eye - diag
        d_ref[p] = _mm(diag, diag).astype(BF16)
    for it in range(shift - 1):
        for p in range(len(probs)):
            pw = d_ref[p]
            inv = p_ref[p]
            p_ref[p] = inv + _mm(inv, pw)
            if it < shift - 2:
                d_ref[p] = _mm(pw, pw).astype(BF16)
    s = INV_BASE
    while s < c:
        sh = s.bit_length() - 1
        off = ((row >> (sh + 1)) == (col >> (sh + 1))) & ((row >> sh) != (col >> sh))
        for p in range(len(probs)):
            d_ref[p] = _mm(p_ref[p], jnp.where(off, a_ref[p], 0.0)).astype(BF16)
        for p in range(len(probs)):
            inv = p_ref[p]
            p_ref[p] = inv - _mm(d_ref[p], inv)
        s *= 2
    for p in range(len(probs)):
        uw = _mm(p_ref[p], rhs_ref[p])
        u_ref[p] = uw[:, :dh]
        wq_ref[p, :c, :] = uw[:, dh:].astype(BF16)

    for ci in range(n_chunks):
        rows = slice(ci * c, (ci + 1) * c)
        ps = [ci * DN_HEADS + h for h in range(DN_HEADS)]
        s_old = [s_ref[h] for h in range(DN_HEADS)]
        ws = [_mm(wq_ref[p], s_old[h]) for h, p in enumerate(ps)]
        v_new = [u_ref[p] - ws[h][:c] for h, p in enumerate(ps)]
        outs = [ws[h][c:] + _mm(qk_ref[p], v_new[h]) for h, p in enumerate(ps)]
        for h, p in enumerate(ps):
            s_ref[h] = s_old[h] * decay_last[p] + _mm(kdt_ref[p], v_new[h])
        for h in range(DN_HEADS):
            o = _rmsnorm(outs[h], gain)
            zz = z_ref[rows, h * dh:(h + 1) * dh]
            o_ref[rows, h * dh:(h + 1) * dh] = (o * (zz * _sigmoid(zz))).astype(o_ref.dtype)


def _deltanet(proj, gbt, dn_conv, dn_norm, batch, seq, tb=512):
    n = proj.shape[0]
    nt = seq // tb
    c = DN_CHUNK
    n_prob = (tb // c) * DN_HEADS
    n_grp = 3 * DN_HEADS
    grp = lambda j: pl.BlockSpec((tb, DN_HEAD_DIM), lambda b, t, j=j: (b * nt + t, j))
    return pl.pallas_call(
        functools.partial(_deltanet_kernel, tb=tb),
        grid=(batch, nt),
        in_specs=[grp(j) for j in range(n_grp)] + [
            pl.BlockSpec((tb, D_DN), lambda b, t: (b * nt + t, 3)),
            pl.BlockSpec((tb, LANES), lambda b, t: (b * nt + t, COL_GB // LANES)),
            pl.BlockSpec((2 * DN_HEADS, tb), lambda b, t: (0, b * nt + t)),
            pl.BlockSpec((CONV_WIDTH, 3 * D_DN), lambda b, t: (0, 0)),
            pl.BlockSpec((1, DN_HEAD_DIM), lambda b, t: (0, 0)),
        ],
        out_specs=pl.BlockSpec((tb, D_DN), lambda b, t: (b * nt + t, 0)),
        out_shape=jax.ShapeDtypeStruct((n, D_DN), BF16),
        scratch_shapes=[
            pltpu.VMEM((DN_HEADS, DN_HEAD_DIM, DN_HEAD_DIM), F32),
            pltpu.VMEM((SUBLANES, 3 * D_DN), F32),
            pltpu.VMEM((n_grp, tb, DN_HEAD_DIM), F32),
            pltpu.VMEM((tb, LANES), F32),
            pltpu.VMEM((2 * DN_HEADS, tb), F32),
            pltpu.VMEM((n_prob, c, c), F32),
            pltpu.VMEM((n_prob, c, c), F32),
            pltpu.VMEM((n_prob, c, c), BF16),
            pltpu.VMEM((n_prob, c, c), BF16),
            pltpu.VMEM((n_prob, c, 2 * DN_HEAD_DIM), BF16),
            pltpu.VMEM((n_prob, c, DN_HEAD_DIM), F32),
            pltpu.VMEM((n_prob, 2 * c, DN_HEAD_DIM), BF16),
            pltpu.VMEM((n_prob, DN_HEAD_DIM, c), BF16),
        ],
        compiler_params=_cparams(2),
        name="deltanet",
    )(*([proj] * (n_grp + 2)), gbt, dn_conv, dn_norm)


def _gelu_tanh(x):
    return 0.5 * x * (1.0 + jnp.tanh(0.7978845608028654 * (x + 0.044715 * (x * x * x))))


def _rglru_kernel(*refs, tb):
    rx_refs = refs[:RG_BLOCKS]
    (ry_ref, conv_ref, convb_ref, wa_ref, ba_ref, wx_ref, bx_ref, lam_ref,
     o_ref, hc_ref, halo_ref, a_ref, b_ref, h_ref) = refs[RG_BLOCKS:]
    t = pl.program_id(1)
    bw = D_RG // RG_BLOCKS
    seg = tb // SUBLANES

    @pl.when(t == 0)
    def _():
        hc_ref[...] = jnp.zeros_like(hc_ref)
        halo_ref[...] = jnp.zeros_like(halo_ref)

    log_sig = -_softplus(-lam_ref[...])
    rowi = lax.broadcasted_iota(I32, (tb, bw), 0)
    seq_start_row = jnp.where(t == 0, 0, -1)
    for nb in range(RG_BLOCKS):
        cs = slice(nb * bw, (nb + 1) * bw)
        xb = (_causal_conv_segmented(_load_segmented(rx_refs[nb]), halo_ref, cs,
                                     conv_ref[:, cs]) + convb_ref[:, cs])
        r = _sigmoid(_mm(xb, wa_ref[nb]) + ba_ref[:, cs])
        gi = _sigmoid(_mm(xb, wx_ref[nb]) + bx_ref[:, cs])
        log_a = RG_C * r * log_sig[:, cs]
        a = jnp.exp(log_a)
        y = jnp.tanh(-log_a) * (1.0 + a * a)
        mult = jnp.where(y > 0.0, y * lax.rsqrt(y), 0.0)
        mult = jnp.where(rowi == seq_start_row, 1.0, mult)
        a_ref[:, cs] = a
        b_ref[:, cs] = mult * (gi * xb)

    def scan(j, carry):
        ac, bc = carry
        rows = pl.ds(pl.multiple_of(j * SUBLANES, SUBLANES), SUBLANES)
        aj = a_ref[rows, :]
        bc = aj * bc + b_ref[rows, :]
        ac = aj * ac
        a_ref[rows, :] = ac
        b_ref[rows, :] = bc
        return ac, bc

    ac, bc = lax.fori_loop(1, seg, scan, (a_ref[0:SUBLANES, :], b_ref[0:SUBLANES, :]), unroll=7)

    h = hc_ref[...]
    h_in = []
    for s in range(SUBLANES):
        h_in.append(h)
        h = ac[s:s + 1, :] * h + bc[s:s + 1, :]
    hc_ref[...] = h
    h_in = jnp.concatenate(h_in, axis=0)

    for j in range(seg):
        rows = slice(j * SUBLANES, (j + 1) * SUBLANES)
        hj = a_ref[rows, :] * h_in + b_ref[rows, :]
        for nb in range(RG_BLOCKS):
            h_ref[nb, rows, :] = hj[:, nb * bw:(nb + 1) * bw]
    g = seg // SUBLANES
    for nb in range(RG_BLOCKS):
        cs = slice(nb * bw, (nb + 1) * bw)
        h = jnp.concatenate(
            [h_ref[nb, pl.ds((i % g) * SUBLANES * SUBLANES + i // g, SUBLANES, stride=SUBLANES), :]
             for i in range(seg)], axis=0)
        o_ref[:, cs] = (h * _gelu_tanh(ry_ref[:, cs])).astype(o_ref.dtype)


def _rglru(proj, rg_conv, rg_conv_b, w_a, b_a, w_x, b_x, lam, batch, seq, tb=512):
    n = proj.shape[0]
    nt = seq // tb
    bw = D_RG // RG_BLOCKS
    full = lambda shape: pl.BlockSpec(shape, lambda b, t: (0,) * len(shape))
    return pl.pallas_call(
        functools.partial(_rglru_kernel, tb=tb),
        grid=(batch, nt),
        in_specs=[pl.BlockSpec((tb, bw), lambda b, t, j=j: (b * nt + t, COL_RX // bw + j))
                  for j in range(RG_BLOCKS)] + [
            pl.BlockSpec((tb, D_RG), lambda b, t: (b * nt + t, COL_RY // D_RG)),
            full((CONV_WIDTH, D_RG)), full((1, D_RG)),
            full((RG_BLOCKS, bw, bw)), full((1, D_RG)),
            full((RG_BLOCKS, bw, bw)), full((1, D_RG)),
            full((1, D_RG)),
        ],
        out_specs=pl.BlockSpec((tb, D_RG), lambda b, t: (b * nt + t, 0)),
        out_shape=jax.ShapeDtypeStruct((n, D_RG), BF16),
        scratch_shapes=[
            pltpu.VMEM((1, D_RG), F32),
            pltpu.VMEM(((CONV_WIDTH - 1) * SUBLANES, D_RG), F32),
            pltpu.VMEM((tb, D_RG), F32),
            pltpu.VMEM((tb, D_RG), F32),
            pltpu.VMEM((RG_BLOCKS, tb, bw), F32),
        ],
        compiler_params=_cparams(2),
        name="rglru",
    )(*([proj] * (RG_BLOCKS + 1)), rg_conv, rg_conv_b, w_a, b_a, w_x, b_x, lam)


def _mem_kv_kernel(m_ref, g_ref, w_ref, o_ref):
    mn = _rmsnorm(m_ref[...], g_ref[...]).astype(BF16)
    for c0 in range(0, o_ref.shape[1], 512):
        o_ref[:, c0:c0 + 512] = jnp.dot(mn, w_ref[:, c0:c0 + 512].astype(BF16),
                                        preferred_element_type=F32).astype(o_ref.dtype)


def _mem_kv(mem2, gain, w_ckv):
    n, d = mem2.shape
    return pl.pallas_call(
        _mem_kv_kernel,
        grid=(1,),
        in_specs=[
            pl.BlockSpec((n, d), lambda i: (0, 0)),
            pl.BlockSpec((1, d), lambda i: (0, 0)),
            pl.BlockSpec((d, 2 * d), lambda i: (0, 0)),
        ],
        out_specs=pl.BlockSpec((n, 2 * d), lambda i: (0, 0)),
        out_shape=jax.ShapeDtypeStruct((n, 2 * d), BF16),
        compiler_params=_cparams(1),
        name="mem_kv",
    )(mem2, gain, w_ckv)


ROUTE_E = 0
ROUTE_RANK = TOP_K
ROUTE_GATE = 2 * TOP_K
ROUTE_ROWS = 16


def _post_mix_kernel(dn_ref, rg_ref, x_ref, wo_f32, gx_ref, wq_f32, kv_ref, wco_f32, gm_ref,
                     wrh_ref, wrl_ref, br_ref, h2_ref, xn_ref, route_ref, routet_ref, cnt_ref,
                     carry_ref, wo_ref, wq_ref, wco_ref):
    i = pl.program_id(0)
    tm, d = x_ref.shape
    hd = d // XA_HEADS

    @pl.when(i == 0)
    def _():
        carry_ref[...] = jnp.zeros_like(carry_ref)
        wo_ref[...] = wo_f32[...].astype(BF16)
        wq_ref[...] = wq_f32[...].astype(BF16)
        wco_ref[...] = wco_f32[...].astype(BF16)

    h1 = (x_ref[...] + jnp.dot(dn_ref[...], wo_ref[0:D_DN, :], preferred_element_type=F32)
          + jnp.dot(rg_ref[...], wo_ref[D_DN:, :], preferred_element_type=F32))

    hn = _rmsnorm(h1, gx_ref[...]).astype(BF16)
    q = jnp.dot(hn, wq_ref[...], preferred_element_type=F32)
    heads = []
    for hh in range(XA_HEADS):
        cs = slice(hh * hd, (hh + 1) * hd)
        s = _mm_nt(q[:, cs], kv_ref[:, cs]) * (hd ** -0.5)
        p = jnp.exp(s - jnp.max(s, axis=-1, keepdims=True))
        p = p / jnp.sum(p, axis=-1, keepdims=True)
        heads.append(_mm(p, kv_ref[:, d + hh * hd:d + (hh + 1) * hd]).astype(BF16))
    o = jnp.concatenate(heads, axis=1)
    h2 = h1 + jnp.dot(o, wco_ref[...], preferred_element_type=F32)
    h2_ref[...] = h2

    xn = _rmsnorm(h2, gm_ref[...])
    _store_row_tiles(xn_ref, _pack_bf16_pairs(xn))
    logits = _mm_split(xn, wrh_ref[...], wrl_ref[...]) + br_ref[...]
    lg = logits.T[0:N_EXPERTS, :]
    eidx = lax.broadcasted_iota(I32, lg.shape, 0).astype(F32)
    neg = jnp.float32(-jnp.inf)
    vals, idxs, hots = [], [], []
    for _ in range(TOP_K):
        m = jnp.max(lg, axis=0, keepdims=True)
        idx = jnp.min(jnp.where(lg == m, eidx, float(N_EXPERTS)), axis=0, keepdims=True)
        hot = eidx == idx
        lg = jnp.where(hot, neg, lg)
        vals.append(m)
        idxs.append(idx)
        hots.append(hot)
    es = [jnp.exp(v - vals[0]) for v in vals]
    den = es[0] + es[1] + es[2] + es[3]
    gates = [e / den for e in es]

    chosen = jnp.zeros(lg.shape, F32)
    for hot in hots:
        chosen = chosen + jnp.where(hot, 1.0, 0.0)
    r2 = lax.broadcasted_iota(I32, (tm, tm), 0)
    c2 = lax.broadcasted_iota(I32, (tm, tm), 1)
    before = _mm(chosen, jnp.where(r2 < c2, 1.0, 0.0)) + carry_ref[:, 0:1]
    ranks = [jnp.sum(jnp.where(hot, before, 0.0), axis=0, keepdims=True) for hot in hots]
    carry_ref[...] = carry_ref[...] + jnp.sum(chosen, axis=1, keepdims=True)
    cnt_ref[...] = carry_ref[...]

    row = lax.broadcasted_iota(I32, (ROUTE_ROWS, tm), 0)
    rect = jnp.zeros((ROUTE_ROWS, tm), F32)
    for kk in range(TOP_K):
        rect = jnp.where(row == ROUTE_E + kk, idxs[kk], rect)
        rect = jnp.where(row == ROUTE_RANK + kk, ranks[kk], rect)
        rect = jnp.where(row == ROUTE_GATE + kk, gates[kk], rect)
    routet_ref[...] = rect
    route_ref[...] = jnp.concatenate(
        [rect, jnp.zeros((LANES - ROUTE_ROWS, tm), F32)], axis=0).T


def _post_mix(dn, rg, x2, w_out, g_cross, w_cq, kv, w_co, g_moe, w_r_hi, w_r_lo, b_r, seq, mem_len,
              tm=512):
    n, d = x2.shape
    per_b = seq // tm
    rt_x = d // 2 // LANES
    full = lambda shape: pl.BlockSpec(shape, lambda i: (0,) * len(shape))
    once = lambda shape: pl.BlockSpec(shape, lambda i: (0,) * len(shape),
                                      pipeline_mode=pl.Buffered(1))
    return pl.pallas_call(
        _post_mix_kernel,
        grid=(n // tm,),
        in_specs=[
            pl.BlockSpec((tm, D_DN), lambda i: (i, 0)),
            pl.BlockSpec((tm, D_RG), lambda i: (i, 0)),
            pl.BlockSpec((tm, d), lambda i: (i, 0)),
            once((d, d)), full((1, d)), once((d, d)),
            pl.BlockSpec((mem_len, 2 * d), lambda i: (i // per_b, 0)),
            once((d, d)), full((1, d)),
            full((d, LANES)), full((d, LANES)), full((1, LANES)),
        ],
        out_specs=[
            pl.BlockSpec((tm, d), lambda i: (i, 0)),
            pl.BlockSpec((tm * rt_x, LANES), lambda i: (i, 0)),
            pl.BlockSpec((tm, LANES), lambda i: (i, 0)),
            pl.BlockSpec((ROUTE_ROWS, tm), lambda i: (0, i)),
            pl.BlockSpec((N_EXPERTS, LANES), lambda i: (0, 0)),
        ],
        out_shape=[
            jax.ShapeDtypeStruct((n, d), F32),
            jax.ShapeDtypeStruct((n * rt_x, LANES), U32),
            jax.ShapeDtypeStruct((n, LANES), F32),
            jax.ShapeDtypeStruct((ROUTE_ROWS, n), F32),
            jax.ShapeDtypeStruct((N_EXPERTS, LANES), F32),
        ],
        scratch_shapes=[pltpu.VMEM((N_EXPERTS, LANES), F32),
                        pltpu.VMEM((d, d), BF16), pltpu.VMEM((d, d), BF16), pltpu.VMEM((d, d), BF16)],
        compiler_params=_cparams(1),
        name="post_mix",
    )(dn, rg, x2, w_out, g_cross, w_cq, kv, w_co, g_moe, w_r_hi, w_r_lo, b_r)


def _dest_kernel(start_ref, routet_ref, dest_ref):
    e = routet_ref[ROUTE_E:ROUTE_E + SUBLANES, :]
    rank = routet_ref[ROUTE_RANK:ROUTE_RANK + SUBLANES, :]
    base = jnp.zeros(e.shape, F32)
    for j in range(N_EXPERTS):
        base = jnp.where(e == float(j), start_ref[j].astype(F32), base)
    row = lax.broadcasted_iota(I32, e.shape, 0)
    dest_ref[...] = jnp.where(row < TOP_K, (base + rank).astype(I32), 0)


def _dest(pad_start, route_t, tm=2048):
    n = route_t.shape[1]
    grid_spec = pltpu.PrefetchScalarGridSpec(
        num_scalar_prefetch=1,
        grid=(n // tm,),
        in_specs=[pl.BlockSpec((ROUTE_ROWS, tm), lambda i, st: (0, i))],
        out_specs=pl.BlockSpec((SUBLANES, tm), lambda i, st: (0, i)),
    )
    return pl.pallas_call(
        _dest_kernel,
        grid_spec=grid_spec,
        out_shape=jax.ShapeDtypeStruct((SUBLANES, n), I32),
        compiler_params=_cparams(1),
        name="dest",
    )(pad_start, route_t)


def _row_tile(ref, r, rt):
    return ref.at[pl.ds(pl.multiple_of(r * rt, rt), rt)]


def _dispatch_kernel(meta_ref, dest_ref, xn_ref, xs_ref, zero_ref, ring_ref, sem, zsem,
                     *, tm, rt, n_blocks):
    i = pl.program_id(0)
    bm = MOE_BM * rt
    ring = ring_ref.at[i & 1]
    ring[...] = xn_ref[...]

    def per_token(tk, carry):
        src = _row_tile(ring, tk, rt)
        for kk in range(TOP_K):
            r = dest_ref[kk, tk]
            pltpu.make_async_copy(src, _row_tile(xs_ref, r, rt),
                                  sem.at[i & 1]).start(priority=kk % 2)
        return carry

    lax.fori_loop(0, tm, per_token, 0, unroll=4)

    @pl.when(i == 0)
    def _():
        zero_ref[...] = jnp.zeros_like(zero_ref)

        def pad_copy(r):
            return pltpu.make_async_copy(zero_ref.at[pl.ds(0, rt)], _row_tile(xs_ref, r, rt), zsem)

        def per_expert(e, total):
            r0 = meta_ref[e]
            cnt = meta_ref[N_EXPERTS + e]

            def group(j, carry):
                for u in range(PAD_UNROLL):
                    pad_copy(r0 + j * PAD_UNROLL + u).start(priority=u % 2)
                return carry

            def one(r, carry):
                pad_copy(r0 + r).start()
                return carry

            groups = cnt // PAD_UNROLL
            lax.fori_loop(0, groups, group, 0)
            lax.fori_loop(groups * PAD_UNROLL, cnt, one, 0)
            return total + cnt

        total = lax.fori_loop(0, N_EXPERTS, per_expert, 0)

        def drain_block(j, carry):
            pltpu.make_async_copy(zero_ref, xs_ref.at[pl.ds(0, bm)], zsem).wait()
            return carry

        def drain_group(j, carry):
            pltpu.make_async_copy(zero_ref.at[pl.ds(0, PAD_UNROLL * rt)],
                                  xs_ref.at[pl.ds(0, PAD_UNROLL * rt)], zsem).wait()
            return carry

        def drain_row(j, carry):
            pad_copy(0).wait()
            return carry

        lax.fori_loop(0, total // MOE_BM, drain_block, 0)
        rest = total % MOE_BM
        lax.fori_loop(0, rest // PAD_UNROLL, drain_group, 0)
        lax.fori_loop(0, rest % PAD_UNROLL, drain_row, 0)

        def tail_copy(b):
            return pltpu.make_async_copy(
                zero_ref, xs_ref.at[pl.ds(pl.multiple_of(b * bm, bm), bm)], zsem)

        n_used = meta_ref[2 * N_EXPERTS]

        def tail(b, carry):
            tail_copy(b).start()
            return carry

        lax.fori_loop(n_used, n_blocks, tail, 0)

        def tail_drain(b, carry):
            tail_copy(0).wait()
            return carry

        lax.fori_loop(n_used, n_blocks, tail_drain, 0)

    def wait_step(parity):
        for _ in range(TOP_K):
            pltpu.make_async_copy(ring_ref.at[parity], xs_ref.at[pl.ds(0, tm * rt)],
                                  sem.at[parity]).wait()

    @pl.when(i > 0)
    def _():
        wait_step(1 - (i & 1))

    @pl.when(i == pl.num_programs(0) - 1)
    def _():
        wait_step(i & 1)


def _dispatch(meta, dest, xn_t, n_tok, n_rows, tm=512):
    rt = xn_t.shape[0] // n_tok
    n_blocks = n_rows // MOE_BM
    grid_spec = pltpu.PrefetchScalarGridSpec(
        num_scalar_prefetch=1,
        grid=(n_tok // tm,),
        in_specs=[
            pl.BlockSpec((SUBLANES, tm), lambda i, meta: (0, i), memory_space=pltpu.SMEM),
            pl.BlockSpec((tm * rt, LANES), lambda i, meta: (i, 0)),
        ],
        out_specs=pl.BlockSpec(memory_space=pl.ANY),
        scratch_shapes=[
            pltpu.VMEM((MOE_BM * rt, LANES), xn_t.dtype),
            pltpu.VMEM((2, tm * rt, LANES), xn_t.dtype),
            pltpu.SemaphoreType.DMA((2,)),
            pltpu.SemaphoreType.DMA,
        ],
    )
    return pl.pallas_call(
        functools.partial(_dispatch_kernel, tm=tm, rt=rt, n_blocks=n_blocks),
        grid_spec=grid_spec,
        out_shape=jax.ShapeDtypeStruct((n_rows * rt, LANES), xn_t.dtype),
        compiler_params=_cparams(1),
        name="dispatch",
    )(meta, dest, xn_t)


def _experts_kernel(be_ref, nu_ref, nxt_ref, par_ref, x_ref, bg_ref, bu_ref, bd_ref,
                    wg_hbm, wu_hbm, wd_hbm, y_ref, wf_ref, wb_ref, sem):
    i = pl.program_id(0)
    used = i < nu_ref[0]
    e = be_ref[i]
    changed = (i == 0) | (e != be_ref[jnp.maximum(i - 1, 0)])

    def weight_copies(expert, slot):
        return [pltpu.make_async_copy(w.at[0, expert], wf_ref.at[slot, j], sem.at[slot])
                for j, w in enumerate((wg_hbm, wu_hbm, wd_hbm))]

    @pl.when(i == 0)
    def _():
        for cp in weight_copies(e, par_ref[e]):
            cp.start()

    @pl.when(used & changed)
    def _():
        slot = par_ref[e]
        for cp in weight_copies(e, slot):
            cp.wait()
        nxt = nxt_ref[e]

        @pl.when(nxt < N_EXPERTS)
        def _():
            for cp in weight_copies(nxt, 1 - slot):
                cp.start()

        for j in range(3):
            wb_ref[j] = wf_ref[slot, j].astype(BF16)

    @pl.when(used)
    def _():
        x = _unpack_pairs_f32(_load_row_tiles(x_ref, MOE_BM)).astype(BF16)
        gt = jnp.minimum(jnp.dot(x, wb_ref[0], preferred_element_type=F32) + bg_ref[...],
                         SWIGLU_LIMIT)
        up = jnp.clip(jnp.dot(x, wb_ref[1], preferred_element_type=F32) + bu_ref[...],
                      -SWIGLU_LIMIT, SWIGLU_LIMIT)
        hid = (up + 1.0) * (gt * _sigmoid(SWIGLU_ALPHA * gt))
        y = jnp.dot(hid.astype(BF16), wb_ref[2], preferred_element_type=F32) + bd_ref[...]
        _store_row_tiles(y_ref, _pack_bf16_pairs(y))

    @pl.when(jnp.logical_not(used))
    def _():
        y_ref[...] = jnp.zeros_like(y_ref)


def _experts(block_e, n_used, next_e, parity, xs_t, w_gate, b_gate, w_up, b_up, w_down, b_down):
    d, d_ff = w_gate.shape[2:]
    assert d == d_ff, "weight staging buffers assume square expert matrices"
    rt = d // 2 // LANES
    n_blocks = xs_t.shape[0] // (MOE_BM * rt)
    bspec = lambda m: pl.BlockSpec((None, 1, m), lambda i, be, nu, nx, pa: (be[i], 0, 0))
    hbm = pl.BlockSpec(memory_space=pl.ANY)
    grid_spec = pltpu.PrefetchScalarGridSpec(
        num_scalar_prefetch=4,
        grid=(n_blocks,),
        in_specs=[
            pl.BlockSpec((MOE_BM * rt, LANES),
                         lambda i, be, nu, nx, pa: (jnp.maximum(jnp.minimum(i, nu[0] - 1), 0), 0)),
            bspec(d_ff), bspec(d_ff), bspec(d),
            hbm, hbm, hbm,
        ],
        out_specs=pl.BlockSpec((MOE_BM * rt, LANES), lambda i, be, nu, nx, pa: (i, 0)),
        scratch_shapes=[
            pltpu.VMEM((2, 3, d, d_ff), F32),
            pltpu.VMEM((3, d, d_ff), BF16),
            pltpu.SemaphoreType.DMA((2,)),
        ],
    )
    return pl.pallas_call(
        _experts_kernel,
        grid_spec=grid_spec,
        out_shape=jax.ShapeDtypeStruct(xs_t.shape, U32),
        compiler_params=_cparams(1),
        name="experts",
    )(block_e, n_used, next_e, parity, xs_t, b_gate, b_up, b_down, w_gate, w_up, w_down)


def _combine_kernel(dcur_ref, dnext_ref, h2_ref, route_ref, gain_ref, ys_ref, o_ref,
                    buf_ref, sem, *, tm, rt):
    i = pl.program_id(0)
    nsteps = pl.num_programs(0)
    slot = i & 1

    def issue_all(dref, s):
        def per_token(tk, carry):
            for kk in range(TOP_K):
                r = dref[kk, tk]
                pltpu.make_async_copy(_row_tile(ys_ref, r, rt),
                                      _row_tile(buf_ref.at[s, kk], tk, rt),
                                      sem.at[s]).start(priority=kk % 2)
            return carry
        lax.fori_loop(0, tm, per_token, 0, unroll=4)

    @pl.when(i == 0)
    def _():
        issue_all(dcur_ref, 0)

    @pl.when(i + 1 < nsteps)
    def _():
        issue_all(dnext_ref, 1 - slot)

    for kk in range(TOP_K):
        pltpu.make_async_copy(ys_ref.at[pl.ds(0, tm * rt)], buf_ref.at[slot, kk],
                              sem.at[slot]).wait()

    rec = route_ref[...]
    acc = h2_ref[...]
    for kk in range(TOP_K):
        acc = acc + (rec[:, ROUTE_GATE + kk:ROUTE_GATE + kk + 1]
                     * _unpack_pairs_f32(_load_row_tiles(buf_ref.at[slot, kk], tm)))
    o_ref[...] = _rmsnorm(acc, gain_ref[...])


def _combine(dest, h2, route, gain, ys_t, tm=512):
    n, d = h2.shape
    nsteps = n // tm
    rt = d // 2 // LANES
    return pl.pallas_call(
        functools.partial(_combine_kernel, tm=tm, rt=rt),
        grid=(nsteps,),
        in_specs=[
            pl.BlockSpec((SUBLANES, tm), lambda i: (0, i), memory_space=pltpu.SMEM),
            pl.BlockSpec((SUBLANES, tm), lambda i: (0, jnp.minimum(i + 1, nsteps - 1)),
                         memory_space=pltpu.SMEM),
            pl.BlockSpec((tm, d), lambda i: (i, 0)),
            pl.BlockSpec((tm, LANES), lambda i: (i, 0)),
            pl.BlockSpec((1, d), lambda i: (0, 0)),
            pl.BlockSpec(memory_space=pl.ANY),
        ],
        out_specs=pl.BlockSpec((tm, d), lambda i: (i, 0)),
        out_shape=jax.ShapeDtypeStruct((n, d), F32),
        scratch_shapes=[
            pltpu.VMEM((2, TOP_K, tm * rt, LANES), U32),
            pltpu.SemaphoreType.DMA((2,)),
        ],
        compiler_params=_cparams(1),
        name="combine",
    )(dest, dest, h2, route, gain, ys_t)


def kernel(x, mem, norm_mix, w_in, dn_conv, dn_a_log, dn_dt_bias, dn_norm, rg_conv, rg_conv_b, rg_w_a, rg_b_a, rg_w_x, rg_b_x, rg_lambda, w_out, norm_cross, norm_mem, w_cq, w_ckv, w_co, norm_moe, w_router, b_router, w_gate, b_gate, w_up, b_up, w_down, b_down, norm_final):
    batch, seq, d = x.shape
    mem_len = mem.shape[1]
    n = batch * seq
    assert w_in.shape[0] == 1, "single-layer trunk"
    x2 = x.reshape(n, d)

    wi = w_in[0]
    n_gate = 2 * DN_HEADS
    w_cat = jnp.concatenate(
        [wi[:, :4 * D_DN], wi[:, 4 * D_DN + n_gate:],
         jnp.pad(wi[:, 4 * D_DN:4 * D_DN + n_gate], ((0, 0), (0, LANES - n_gate)))],
        axis=1).astype(BF16)
    prow = (jnp.zeros((SUBLANES, LANES), F32)
            .at[0, DN_HEADS:n_gate].set(dn_a_log[0]).at[1, DN_HEADS:n_gate].set(dn_dt_bias[0]))
    pcol = (jnp.zeros((n_gate, LANES), F32)
            .at[DN_HEADS:, 0].set(dn_a_log[0]).at[DN_HEADS:, 1].set(dn_dt_bias[0]))

    proj, gbt = _in_proj(x2, norm_mix, w_cat, prow, pcol)
    dn = _deltanet(proj, gbt, dn_conv[0], dn_norm, batch, seq)
    rg = _rglru(proj, rg_conv[0], rg_conv_b, rg_w_a[0], rg_b_a[0].reshape(1, D_RG),
                rg_w_x[0], rg_b_x[0].reshape(1, D_RG), rg_lambda, batch, seq)
    kv = _mem_kv(mem.reshape(batch * mem_len, d), norm_mem, w_ckv[0])

    w_r = jnp.pad(w_router[0], ((0, 0), (0, LANES - N_EXPERTS)))
    w_r_hi = w_r.astype(BF16)
    w_r_lo = (w_r - w_r_hi.astype(F32)).astype(BF16)
    b_r = jnp.pad(b_router, ((0, 0), (0, LANES - N_EXPERTS)))
    h2, xn, route, route_t, counts = _post_mix(
        dn, rg, x2, w_out[0], norm_cross, w_cq[0], kv, w_co[0], norm_moe, w_r_hi, w_r_lo, b_r,
        seq, mem_len)

    n_blocks = n * TOP_K // MOE_BM + N_EXPERTS
    n_rows = n_blocks * MOE_BM
    cnt = counts[:, 0].astype(I32)
    padded = (cnt + MOE_BM - 1) // MOE_BM * MOE_BM
    pad_end = jnp.cumsum(padded)
    pad_start = pad_end - padded
    n_used = (pad_end[-1:] // MOE_BM).astype(I32)
    block_e = jnp.minimum(
        jnp.sum(pad_end[None, :] <= (jnp.arange(n_blocks, dtype=I32) * MOE_BM)[:, None], axis=1),
        N_EXPERTS - 1).astype(I32)
    meta = jnp.concatenate([pad_start + cnt, padded - cnt, n_used]).astype(I32)

    dest = _dest(pad_start.astype(I32), route_t)
    xs = _dispatch(meta, dest, xn, n, n_rows)
    has = cnt > 0
    eid = jnp.where(has, jnp.arange(N_EXPERTS, dtype=I32), N_EXPERTS)
    after = lax.cummin(eid, axis=0, reverse=True)
    next_e = jnp.concatenate([after[1:], jnp.full((1,), N_EXPERTS, I32)]).astype(I32)
    parity = ((jnp.cumsum(has.astype(I32)) - 1) & 1).astype(I32)
    ys = _experts(block_e, n_used, next_e, parity, xs, w_gate, b_gate[0][:, None, :], w_up,
                  b_up[0][:, None, :], w_down, b_down[0][:, None, :])
    out = _combine(dest, h2, route, norm_final.reshape(1, d), ys)
    return out.reshape(batch, seq, d)
```

```python
import functools

import jax
import jax.numpy as jnp
from jax import lax
from jax.experimental import pallas as pl
from jax.experimental.pallas import tpu as pltpu

F32 = jnp.float32
BF16 = jnp.bfloat16
I32 = jnp.int32
U32 = jnp.uint32

EPS = 1e-6
LANES = 128
SUBLANES = 8
VMEM_LIMIT = 48 * 1024 * 1024

DN_HEADS = 4
DN_HEAD_DIM = 128
D_DN = DN_HEADS * DN_HEAD_DIM
D_RG = 512
RG_BLOCKS = 4
RG_C = 8.0
CONV_WIDTH = 4
XA_HEADS = 4
N_EXPERTS = 32
TOP_K = 4
SWIGLU_LIMIT = 7.0
SWIGLU_ALPHA = 1.702

DN_CHUNK = 128
INV_BASE = 16
MOE_BM = 512
PAD_UNROLL = 8

COL_RX = 4 * D_DN
COL_RY = COL_RX + D_RG
COL_GB = COL_RY + D_RG
PROJ_W = COL_GB + LANES


def _cparams(n_axes=1):
    return pltpu.CompilerParams(
        dimension_semantics=("arbitrary",) * n_axes, vmem_limit_bytes=VMEM_LIMIT)


def _mm(a, b):
    return jnp.dot(a.astype(BF16), b.astype(BF16), preferred_element_type=F32)


def _mm_nt(a, b):
    return lax.dot_general(a.astype(BF16), b.astype(BF16), (((1,), (1,)), ((), ())),
                           preferred_element_type=F32)


def _rmsnorm(x, g):
    return x * lax.rsqrt(jnp.mean(x * x, axis=-1, keepdims=True) + EPS) * g


def _sigmoid(x):
    return 0.5 * jnp.tanh(0.5 * x) + 0.5


def _mm_split(a, b_hi, b_lo):
    a_hi = a.astype(BF16)
    a_lo = (a - a_hi.astype(F32)).astype(BF16)
    return (jnp.dot(a_hi, b_hi, preferred_element_type=F32)
            + jnp.dot(a_hi, b_lo, preferred_element_type=F32)
            + jnp.dot(a_lo, b_hi, preferred_element_type=F32))


def _softplus(x):
    return jnp.maximum(x, 0.0) + jnp.log1p(jnp.exp(-jnp.abs(x)))


def _load_row_tiles(ref, rows):
    rt = ref.shape[0] // rows
    return jnp.concatenate([ref[pl.ds(s, rows, stride=rt), :] for s in range(rt)], axis=1)


def _store_row_tiles(ref, val):
    rows, w = val.shape
    rt = w // LANES
    for s in range(rt):
        ref[pl.ds(s, rows, stride=rt), :] = val[:, s * LANES:(s + 1) * LANES]


def _pack_bf16_pairs(x):
    half = x.shape[1] // 2
    hi = pltpu.bitcast(x[:, :half].astype(BF16).astype(F32), U32)
    lo = pltpu.bitcast(x[:, half:].astype(BF16).astype(F32), U32)
    return hi | (lo >> 16)


def _unpack_pairs_f32(p):
    hi = pltpu.bitcast(p & jnp.uint32(0xFFFF0000), F32)
    lo = pltpu.bitcast(p << 16, F32)
    return jnp.concatenate([hi, lo], axis=1)


def _load_segmented(ref):
    seg = ref.shape[0] // SUBLANES
    return jnp.concatenate([ref[pl.ds(j, SUBLANES, stride=seg), :] for j in range(seg)], axis=0)


def _causal_conv_segmented(xp, halo_ref, cs, w):
    tb = xp.shape[0]
    ng = w.shape[0] - 1
    prev = halo_ref[:, cs]
    last = xp[tb - ng * SUBLANES:, :]
    halo_ref[:, cs] = last
    sub = lax.broadcasted_iota(I32, (SUBLANES, xp.shape[1]), 0)
    groups = []
    for g in range(ng):
        rows = slice(g * SUBLANES, (g + 1) * SUBLANES)
        groups.append(jnp.where(sub == 0, pltpu.roll(prev[rows], 1, 0),
                                pltpu.roll(last[rows], 1, 0)))
    ext = jnp.concatenate(groups + [xp], axis=0)
    y = w[ng:ng + 1] * xp
    for k in range(1, ng + 1):
        y = y + w[ng - k:ng - k + 1] * ext[(ng - k) * SUBLANES:(ng - k) * SUBLANES + tb]
    return y


def _in_proj_kernel(x_ref, g_ref, w_ref, prow_ref, pcol_ref, proj_ref, gbt_ref):
    u = _rmsnorm(x_ref[...], g_ref[...]).astype(BF16)
    for c0 in range(0, COL_GB, 512):
        proj_ref[:, c0:c0 + 512] = jnp.dot(u, w_ref[:, c0:c0 + 512], preferred_element_type=F32)
    ba = jnp.dot(u, w_ref[:, COL_GB:PROJ_W], preferred_element_type=F32)
    lane = lax.broadcasted_iota(I32, ba.shape, 1)
    g = -jnp.exp(prow_ref[0:1, :]) * _softplus(ba + prow_ref[1:2, :])
    proj_ref[:, COL_GB:PROJ_W] = jnp.where(lane < DN_HEADS, _sigmoid(ba), g)
    bat = lax.dot_general(w_ref[:, COL_GB:PROJ_W], u, (((0,), (1,)), ((), ())),
                          preferred_element_type=F32)[0:2 * DN_HEADS, :]
    row = lax.broadcasted_iota(I32, bat.shape, 0)
    gt = -jnp.exp(pcol_ref[:, 0:1]) * _softplus(bat + pcol_ref[:, 1:2])
    gbt_ref[...] = jnp.where(row < DN_HEADS, _sigmoid(bat), gt)


def _in_proj(x2, gain, w_cat, prow, pcol, tm=512):
    n, d = x2.shape
    return pl.pallas_call(
        _in_proj_kernel,
        grid=(n // tm,),
        in_specs=[
            pl.BlockSpec((tm, d), lambda i: (i, 0)),
            pl.BlockSpec((1, d), lambda i: (0, 0)),
            pl.BlockSpec((d, PROJ_W), lambda i: (0, 0)),
            pl.BlockSpec((SUBLANES, LANES), lambda i: (0, 0)),
            pl.BlockSpec((2 * DN_HEADS, LANES), lambda i: (0, 0)),
        ],
        out_specs=[
            pl.BlockSpec((tm, PROJ_W), lambda i: (i, 0)),
            pl.BlockSpec((2 * DN_HEADS, tm), lambda i: (0, i)),
        ],
        out_shape=[
            jax.ShapeDtypeStruct((n, PROJ_W), F32),
            jax.ShapeDtypeStruct((2 * DN_HEADS, n), F32),
        ],
        compiler_params=_cparams(1),
        name="in_proj",
    )(x2, gain, w_cat, prow, pcol)


def _deltanet_kernel(*refs, tb):
    n_grp = 3 * DN_HEADS
    qkv_refs = refs[:n_grp]
    (z_ref, gb_ref, gbt_ref, conv_ref, norm_ref, o_ref,
     s_ref, halo_ref, act_ref, gcc_ref, gcr_ref,
     a_ref, p_ref, d_ref, qk_ref, rhs_ref, u_ref, wq_ref, kdt_ref) = refs[n_grp:]
    t = pl.program_id(1)
    c = DN_CHUNK
    dh = DN_HEAD_DIM

    @pl.when(t == 0)
    def _():
        s_ref[...] = jnp.zeros_like(s_ref)
        halo_ref[...] = jnp.zeros_like(halo_ref)

    sub = lax.broadcasted_iota(I32, (SUBLANES, dh), 0)
    for grp in range(n_grp):
        cs = slice(grp * dh, (grp + 1) * dh)
        x = qkv_refs[grp][...]
        prev = halo_ref[:, cs]
        halo_ref[:, cs] = x[tb - SUBLANES:, :]
        y = conv_ref[CONV_WIDTH - 1:CONV_WIDTH, cs] * x
        for k in range(1, CONV_WIDTH):
            xs = pltpu.roll(x, k, 0)
            head = jnp.where(sub < k, pltpu.roll(prev, k, 0), xs[:SUBLANES])
            xs = jnp.concatenate([head, xs[SUBLANES:]], axis=0)
            y = y + conv_ref[CONV_WIDTH - 1 - k:CONV_WIDTH - k, cs] * xs
        y = y * _sigmoid(y)
        if grp < 2 * DN_HEADS:
            y = y * lax.rsqrt(jnp.sum(y * y, axis=-1, keepdims=True) + EPS)
        if grp < DN_HEADS:
            y = y * (dh ** -0.5)
        act_ref[grp] = y

    gcol = gb_ref[...]
    rpos = lax.broadcasted_iota(I32, gcol.shape, 0) & (c - 1)
    d = 1
    while d < c:
        gcol = gcol + jnp.where(rpos >= d, pltpu.roll(gcol, d, 0), 0.0)
        d *= 2
    gcc_ref[...] = gcol
    grow = gbt_ref[...]
    lpos = lax.broadcasted_iota(I32, grow.shape, 1) & (c - 1)
    d = 1
    while d < c:
        grow = grow + jnp.where(lpos >= d, pltpu.roll(grow, d, 1), 0.0)
        d *= 2
    gcr_ref[...] = grow

    row = lax.broadcasted_iota(I32, (c, c), 0)
    col = lax.broadcasted_iota(I32, (c, c), 1)
    causal = row >= col
    strict = row > col
    eye = jnp.where(row == col, 1.0, 0.0)
    gain = norm_ref[...]
    n_chunks = tb // c
    probs = [(ci, h) for ci in range(n_chunks) for h in range(DN_HEADS)]

    decay_last = []
    for p, (ci, h) in enumerate(probs):
        rows = slice(ci * c, (ci + 1) * c)
        q = act_ref[h, rows, :]
        k = act_ref[DN_HEADS + h, rows, :]
        v = act_ref[2 * DN_HEADS + h, rows, :]
        beta = gb_ref[rows, h:h + 1]
        gc = gcc_ref[rows, DN_HEADS + h:DN_HEADS + h + 1]
        gr = gcr_ref[DN_HEADS + h:DN_HEADS + h + 1, rows]
        g_last = gc[c - 1:c, :]
        decay = jnp.where(causal, jnp.exp(jnp.where(causal, gc - gr, 0.0)), 0.0)
        kb = k * beta
        both = _mm_nt(jnp.concatenate([kb, q], axis=0), k)
        a_ref[p] = jnp.where(strict, both[:c] * decay, 0.0)
        qk_ref[p] = (both[c:] * decay).astype(BF16)
        egc = jnp.exp(gc)
        rhs_ref[p] = jnp.concatenate([v * beta, kb * egc], axis=1).astype(BF16)
        wq_ref[p, c:, :] = (q * egc).astype(BF16)
        kdt_ref[p] = (k * jnp.exp(g_last - gc)).T.astype(BF16)
        decay_last.append(jnp.exp(g_last))

    shift = INV_BASE.bit_length() - 1
    blk = (row >> shift) == (col >> shift)
    for p in range(len(probs)):
        diag = jnp.where(blk, a_ref[p], 0.0)
        p_ref[p] = eye - diag
        d_ref[p] = _mm(diag, diag).astype(BF16)
    for it in range(shift - 1):
        for p in range(len(probs)):
            pw = d_ref[p]
            inv = p_ref[p]
            p_ref[p] = inv + _mm(inv, pw)
            if it < shift - 2:
                d_ref[p] = _mm(pw, pw).astype(BF16)
    s = INV_BASE
    while s < c:
        sh = s.bit_length() - 1
        off = ((row >> (sh + 1)) == (col >> (sh + 1))) & ((row >> sh) != (col >> sh))
        for p in range(len(probs)):
            d_ref[p] = _mm(p_ref[p], jnp.where(off, a_ref[p], 0.0)).astype(BF16)
        for p in range(len(probs)):
            inv = p_ref[p]
            p_ref[p] = inv - _mm(d_ref[p], inv)
        s *= 2
    for p in range(len(probs)):
        uw = _mm(p_ref[p], rhs_ref[p])
        u_ref[p] = uw[:, :dh]
        wq_ref[p, :c, :] = uw[:, dh:].astype(BF16)

    for ci in range(n_chunks):
        rows = slice(ci * c, (ci + 1) * c)
        ps = [ci * DN_HEADS + h for h in range(DN_HEADS)]
        s_old = [s_ref[h] for h in range(DN_HEADS)]
        ws = [_mm(wq_ref[p], s_old[h]) for h, p in enumerate(ps)]
        v_new = [u_ref[p] - ws[h][:c] for h, p in enumerate(ps)]
        outs = [ws[h][c:] + _mm(qk_ref[p], v_new[h]) for h, p in enumerate(ps)]
        for h, p in enumerate(ps):
            s_ref[h] = s_old[h] * decay_last[p] + _mm(kdt_ref[p], v_new[h])
        for h in range(DN_HEADS):
            o = _rmsnorm(outs[h], gain)
            zz = z_ref[rows, h * dh:(h + 1) * dh]
            o_ref[rows, h * dh:(h + 1) * dh] = (o * (zz * _sigmoid(zz))).astype(o_ref.dtype)


def _deltanet(proj, gbt, dn_conv, dn_norm, batch, seq, tb=512):
    n = proj.shape[0]
    nt = seq // tb
    c = DN_CHUNK
    n_prob = (tb // c) * DN_HEADS
    n_grp = 3 * DN_HEADS
    grp = lambda j: pl.BlockSpec((tb, DN_HEAD_DIM), lambda b, t, j=j: (b * nt + t, j))
    return pl.pallas_call(
        functools.partial(_deltanet_kernel, tb=tb),
        grid=(batch, nt),
        in_specs=[grp(j) for j in range(n_grp)] + [
            pl.BlockSpec((tb, D_DN), lambda b, t: (b * nt + t, 3)),
            pl.BlockSpec((tb, LANES), lambda b, t: (b * nt + t, COL_GB // LANES)),
            pl.BlockSpec((2 * DN_HEADS, tb), lambda b, t: (0, b * nt + t)),
            pl.BlockSpec((CONV_WIDTH, 3 * D_DN), lambda b, t: (0, 0)),
            pl.BlockSpec((1, DN_HEAD_DIM), lambda b, t: (0, 0)),
        ],
        out_specs=pl.BlockSpec((tb, D_DN), lambda b, t: (b * nt + t, 0)),
        out_shape=jax.ShapeDtypeStruct((n, D_DN), BF16),
        scratch_shapes=[
            pltpu.VMEM((DN_HEADS, DN_HEAD_DIM, DN_HEAD_DIM), F32),
            pltpu.VMEM((SUBLANES, 3 * D_DN), F32),
            pltpu.VMEM((n_grp, tb, DN_HEAD_DIM), F32),
            pltpu.VMEM((tb, LANES), F32),
            pltpu.VMEM((2 * DN_HEADS, tb), F32),
            pltpu.VMEM((n_prob, c, c), F32),
            pltpu.VMEM((n_prob, c, c), F32),
            pltpu.VMEM((n_prob, c, c), BF16),
            pltpu.VMEM((n_prob, c, c), BF16),
            pltpu.VMEM((n_prob, c, 2 * DN_HEAD_DIM), BF16),
            pltpu.VMEM((n_prob, c, DN_HEAD_DIM), F32),
            pltpu.VMEM((n_prob, 2 * c, DN_HEAD_DIM), BF16),
            pltpu.VMEM((n_prob, DN_HEAD_DIM, c), BF16),
        ],
        compiler_params=_cparams(2),
        name="deltanet",
    )(*([proj] * (n_grp + 2)), gbt, dn_conv, dn_norm)


def _gelu_tanh(x):
    return 0.5 * x * (1.0 + jnp.tanh(0.7978845608028654 * (x + 0.044715 * (x * x * x))))


def _rglru_kernel(*refs, tb):
    rx_refs = refs[:RG_BLOCKS]
    (ry_ref, conv_ref, convb_ref, wa_ref, ba_ref, wx_ref, bx_ref, lam_ref,
     o_ref, hc_ref, halo_ref, a_ref, b_ref, h_ref) = refs[RG_BLOCKS:]
    t = pl.program_id(1)
    bw = D_RG // RG_BLOCKS
    seg = tb // SUBLANES

    @pl.when(t == 0)
    def _():
        hc_ref[...] = jnp.zeros_like(hc_ref)
        halo_ref[...] = jnp.zeros_like(halo_ref)

    log_sig = -_softplus(-lam_ref[...])
    rowi = lax.broadcasted_iota(I32, (tb, bw), 0)
    seq_start_row = jnp.where(t == 0, 0, -1)
    for nb in range(RG_BLOCKS):
        cs = slice(nb * bw, (nb + 1) * bw)
        xb = (_causal_conv_segmented(_load_segmented(rx_refs[nb]), halo_ref, cs,
                                     conv_ref[:, cs]) + convb_ref[:, cs])
        r = _sigmoid(_mm(xb, wa_ref[nb]) + ba_ref[:, cs])
        gi = _sigmoid(_mm(xb, wx_ref[nb]) + bx_ref[:, cs])
        log_a = RG_C * r * log_sig[:, cs]
        a = jnp.exp(log_a)
        y = jnp.tanh(-log_a) * (1.0 + a * a)
        mult = jnp.where(y > 0.0, y * lax.rsqrt(y), 0.0)
        mult = jnp.where(rowi == seq_start_row, 1.0, mult)
        a_ref[:, cs] = a
        b_ref[:, cs] = mult * (gi * xb)

    def scan(j, carry):
        ac, bc = carry
        rows = pl.ds(pl.multiple_of(j * SUBLANES, SUBLANES), SUBLANES)
        aj = a_ref[rows, :]
        bc = aj * bc + b_ref[rows, :]
        ac = aj * ac
        a_ref[rows, :] = ac
        b_ref[rows, :] = bc
        return ac, bc

    ac, bc = lax.fori_loop(1, seg, scan, (a_ref[0:SUBLANES, :], b_ref[0:SUBLANES, :]), unroll=7)

    h = hc_ref[...]
    h_in = []
    for s in range(SUBLANES):
        h_in.append(h)
        h = ac[s:s + 1, :] * h + bc[s:s + 1, :]
    hc_ref[...] = h
    h_in = jnp.concatenate(h_in, axis=0)

    for j in range(seg):
        rows = slice(j * SUBLANES, (j + 1) * SUBLANES)
        hj = a_ref[rows, :] * h_in + b_ref[rows, :]
        for nb in range(RG_BLOCKS):
            h_ref[nb, rows, :] = hj[:, nb * bw:(nb + 1) * bw]
    g = seg // SUBLANES
    for nb in range(RG_BLOCKS):
        cs = slice(nb * bw, (nb + 1) * bw)
        h = jnp.concatenate(
            [h_ref[nb, pl.ds((i % g) * SUBLANES * SUBLANES + i // g, SUBLANES, stride=SUBLANES), :]
             for i in range(seg)], axis=0)
        o_ref[:, cs] = (h * _gelu_tanh(ry_ref[:, cs])).astype(o_ref.dtype)


def _rglru(proj, rg_conv, rg_conv_b, w_a, b_a, w_x, b_x, lam, batch, seq, tb=512):
    n = proj.shape[0]
    nt = seq // tb
    bw = D_RG // RG_BLOCKS
    full = lambda shape: pl.BlockSpec(shape, lambda b, t: (0,) * len(shape))
    return pl.pallas_call(
        functools.partial(_rglru_kernel, tb=tb),
        grid=(batch, nt),
        in_specs=[pl.BlockSpec((tb, bw), lambda b, t, j=j: (b * nt + t, COL_RX // bw + j))
                  for j in range(RG_BLOCKS)] + [
            pl.BlockSpec((tb, D_RG), lambda b, t: (b * nt + t, COL_RY // D_RG)),
            full((CONV_WIDTH, D_RG)), full((1, D_RG)),
            full((RG_BLOCKS, bw, bw)), full((1, D_RG)),
            full((RG_BLOCKS, bw, bw)), full((1, D_RG)),
            full((1, D_RG)),
        ],
        out_specs=pl.BlockSpec((tb, D_RG), lambda b, t: (b * nt + t, 0)),
        out_shape=jax.ShapeDtypeStruct((n, D_RG), BF16),
        scratch_shapes=[
            pltpu.VMEM((1, D_RG), F32),
            pltpu.VMEM(((CONV_WIDTH - 1) * SUBLANES, D_RG), F32),
            pltpu.VMEM((tb, D_RG), F32),
            pltpu.VMEM((tb, D_RG), F32),
            pltpu.VMEM((RG_BLOCKS, tb, bw), F32),
        ],
        compiler_params=_cparams(2),
        name="rglru",
    )(*([proj] * (RG_BLOCKS + 1)), rg_conv, rg_conv_b, w_a, b_a, w_x, b_x, lam)


def _mem_kv_kernel(m_ref, g_ref, w_ref, o_ref):
    mn = _rmsnorm(m_ref[...], g_ref[...]).astype(BF16)
    for c0 in range(0, o_ref.shape[1], 512):
        o_ref[:, c0:c0 + 512] = jnp.dot(mn, w_ref[:, c0:c0 + 512].astype(BF16),
                                        preferred_element_type=F32).astype(o_ref.dtype)


def _mem_kv(mem2, gain, w_ckv):
    n, d = mem2.shape
    return pl.pallas_call(
        _mem_kv_kernel,
        grid=(1,),
        in_specs=[
            pl.BlockSpec((n, d), lambda i: (0, 0)),
            pl.BlockSpec((1, d), lambda i: (0, 0)),
            pl.BlockSpec((d, 2 * d), lambda i: (0, 0)),
        ],
        out_specs=pl.BlockSpec((n, 2 * d), lambda i: (0, 0)),
        out_shape=jax.ShapeDtypeStruct((n, 2 * d), BF16),
        compiler_params=_cparams(1),
        name="mem_kv",
    )(mem2, gain, w_ckv)


ROUTE_E = 0
ROUTE_RANK = TOP_K
ROUTE_GATE = 2 * TOP_K
ROUTE_ROWS = 16


def _post_mix_kernel(dn_ref, rg_ref, x_ref, wo_f32, gx_ref, wq_f32, kv_ref, wco_f32, gm_ref,
                     wrh_ref, wrl_ref, br_ref, h2_ref, xn_ref, route_ref, routet_ref, cnt_ref,
                     carry_ref, wo_ref, wq_ref, wco_ref):
    i = pl.program_id(0)
    tm, d = x_ref.shape
    hd = d // XA_HEADS

    @pl.when(i == 0)
    def _():
        carry_ref[...] = jnp.zeros_like(carry_ref)
        wo_ref[...] = wo_f32[...].astype(BF16)
        wq_ref[...] = wq_f32[...].astype(BF16)
        wco_ref[...] = wco_f32[...].astype(BF16)

    h1 = (x_ref[...] + jnp.dot(dn_ref[...], wo_ref[0:D_DN, :], preferred_element_type=F32)
          + jnp.dot(rg_ref[...], wo_ref[D_DN:, :], preferred_element_type=F32))

    hn = _rmsnorm(h1, gx_ref[...]).astype(BF16)
    q = jnp.dot(hn, wq_ref[...], preferred_element_type=F32)
    heads = []
    for hh in range(XA_HEADS):
        cs = slice(hh * hd, (hh + 1) * hd)
        s = _mm_nt(q[:, cs], kv_ref[:, cs]) * (hd ** -0.5)
        p = jnp.exp(s - jnp.max(s, axis=-1, keepdims=True))
        p = p / jnp.sum(p, axis=-1, keepdims=True)
        heads.append(_mm(p, kv_ref[:, d + hh * hd:d + (hh + 1) * hd]).astype(BF16))
    o = jnp.concatenate(heads, axis=1)
    h2 = h1 + jnp.dot(o, wco_ref[...], preferred_element_type=F32)
    h2_ref[...] = h2

    xn = _rmsnorm(h2, gm_ref[...])
    _store_row_tiles(xn_ref, _pack_bf16_pairs(xn))
    logits = _mm_split(xn, wrh_ref[...], wrl_ref[...]) + br_ref[...]
    lg = logits.T[0:N_EXPERTS, :]
    eidx = lax.broadcasted_iota(I32, lg.shape, 0).astype(F32)
    neg = jnp.float32(-jnp.inf)
    vals, idxs, hots = [], [], []
    for _ in range(TOP_K):
        m = jnp.max(lg, axis=0, keepdims=True)
        idx = jnp.min(jnp.where(lg == m, eidx, float(N_EXPERTS)), axis=0, keepdims=True)
        hot = eidx == idx
        lg = jnp.where(hot, neg, lg)
        vals.append(m)
        idxs.append(idx)
        hots.append(hot)
    es = [jnp.exp(v - vals[0]) for v in vals]
    den = es[0] + es[1] + es[2] + es[3]
    gates = [e / den for e in es]

    chosen = jnp.zeros(lg.shape, F32)
    for hot in hots:
        chosen = chosen + jnp.where(hot, 1.0, 0.0)
    r2 = lax.broadcasted_iota(I32, (tm, tm), 0)
    c2 = lax.broadcasted_iota(I32, (tm, tm), 1)
    before = _mm(chosen, jnp.where(r2 < c2, 1.0, 0.0)) + carry_ref[:, 0:1]
    ranks = [jnp.sum(jnp.where(hot, before, 0.0), axis=0, keepdims=True) for hot in hots]
    carry_ref[...] = carry_ref[...] + jnp.sum(chosen, axis=1, keepdims=True)
    cnt_ref[...] = carry_ref[...]

    row = lax.broadcasted_iota(I32, (ROUTE_ROWS, tm), 0)
    rect = jnp.zeros((ROUTE_ROWS, tm), F32)
    for kk in range(TOP_K):
        rect = jnp.where(row == ROUTE_E + kk, idxs[kk], rect)
        rect = jnp.where(row == ROUTE_RANK + kk, ranks[kk], rect)
        rect = jnp.where(row == ROUTE_GATE + kk, gates[kk], rect)
    routet_ref[...] = rect
    route_ref[...] = jnp.concatenate(
        [rect, jnp.zeros((LANES - ROUTE_ROWS, tm), F32)], axis=0).T


def _post_mix(dn, rg, x2, w_out, g_cross, w_cq, kv, w_co, g_moe, w_r_hi, w_r_lo, b_r, seq, mem_len,
              tm=512):
    n, d = x2.shape
    per_b = seq // tm
    rt_x = d // 2 // LANES
    full = lambda shape: pl.BlockSpec(shape, lambda i: (0,) * len(shape))
    once = lambda shape: pl.BlockSpec(shape, lambda i: (0,) * len(shape),
                                      pipeline_mode=pl.Buffered(1))
    return pl.pallas_call(
        _post_mix_kernel,
        grid=(n // tm,),
        in_specs=[
            pl.BlockSpec((tm, D_DN), lambda i: (i, 0)),
            pl.BlockSpec((tm, D_RG), lambda i: (i, 0)),
            pl.BlockSpec((tm, d), lambda i: (i, 0)),
            once((d, d)), full((1, d)), once((d, d)),
            pl.BlockSpec((mem_len, 2 * d), lambda i: (i // per_b, 0)),
            once((d, d)), full((1, d)),
            full((d, LANES)), full((d, LANES)), full((1, LANES)),
        ],
        out_specs=[
            pl.BlockSpec((tm, d), lambda i: (i, 0)),
            pl.BlockSpec((tm * rt_x, LANES), lambda i: (i, 0)),
            pl.BlockSpec((tm, LANES), lambda i: (i, 0)),
            pl.BlockSpec((ROUTE_ROWS, tm), lambda i: (0, i)),
            pl.BlockSpec((N_EXPERTS, LANES), lambda i: (0, 0)),
        ],
        out_shape=[
            jax.ShapeDtypeStruct((n, d), F32),
            jax.ShapeDtypeStruct((n * rt_x, LANES), U32),
            jax.ShapeDtypeStruct((n, LANES), F32),
            jax.ShapeDtypeStruct((ROUTE_ROWS, n), F32),
            jax.ShapeDtypeStruct((N_EXPERTS, LANES), F32),
        ],
        scratch_shapes=[pltpu.VMEM((N_EXPERTS, LANES), F32),
                        pltpu.VMEM((d, d), BF16), pltpu.VMEM((d, d), BF16), pltpu.VMEM((d, d), BF16)],
        compiler_params=_cparams(1),
        name="post_mix",
    )(dn, rg, x2, w_out, g_cross, w_cq, kv, w_co, g_moe, w_r_hi, w_r_lo, b_r)


def _dest_kernel(start_ref, routet_ref, dest_ref):
    e = routet_ref[ROUTE_E:ROUTE_E + SUBLANES, :]
    rank = routet_ref[ROUTE_RANK:ROUTE_RANK + SUBLANES, :]
    base = jnp.zeros(e.shape, F32)
    for j in range(N_EXPERTS):
        base = jnp.where(e == float(j), start_ref[j].astype(F32), base)
    row = lax.broadcasted_iota(I32, e.shape, 0)
    dest_ref[...] = jnp.where(row < TOP_K, (base + rank).astype(I32), 0)


def _dest(pad_start, route_t, tm=2048):
    n = route_t.shape[1]
    grid_spec = pltpu.PrefetchScalarGridSpec(
        num_scalar_prefetch=1,
        grid=(n // tm,),
        in_specs=[pl.BlockSpec((ROUTE_ROWS, tm), lambda i, st: (0, i))],
        out_specs=pl.BlockSpec((SUBLANES, tm), lambda i, st: (0, i)),
    )
    return pl.pallas_call(
        _dest_kernel,
        grid_spec=grid_spec,
        out_shape=jax.ShapeDtypeStruct((SUBLANES, n), I32),
        compiler_params=_cparams(1),
        name="dest",
    )(pad_start, route_t)


def _row_tile(ref, r, rt):
    return ref.at[pl.ds(pl.multiple_of(r * rt, rt), rt)]


def _dispatch_kernel(meta_ref, dest_ref, xn_ref, xs_ref, zero_ref, ring_ref, sem, zsem,
                     *, tm, rt, n_blocks):
    i = pl.program_id(0)
    bm = MOE_BM * rt
    ring = ring_ref.at[i & 1]
    ring[...] = xn_ref[...]

    def per_token(tk, carry):
        src = _row_tile(ring, tk, rt)
        for kk in range(TOP_K):
            r = dest_ref[kk, tk]
            pltpu.make_async_copy(src, _row_tile(xs_ref, r, rt),
                                  sem.at[i & 1]).start(priority=kk % 2)
        return carry

    lax.fori_loop(0, tm, per_token, 0, unroll=4)

    @pl.when(i == 0)
    def _():
        zero_ref[...] = jnp.zeros_like(zero_ref)

        def pad_copy(r):
            return pltpu.make_async_copy(zero_ref.at[pl.ds(0, rt)], _row_tile(xs_ref, r, rt), zsem)

        def per_expert(e, total):
            r0 = meta_ref[e]
            cnt = meta_ref[N_EXPERTS + e]

            def group(j, carry):
                for u in range(PAD_UNROLL):
                    pad_copy(r0 + j * PAD_UNROLL + u).start(priority=u % 2)
                return carry

            def one(r, carry):
                pad_copy(r0 + r).start()
                return carry

            groups = cnt // PAD_UNROLL
            lax.fori_loop(0, groups, group, 0)
            lax.fori_loop(groups * PAD_UNROLL, cnt, one, 0)
            return total + cnt

        total = lax.fori_loop(0, N_EXPERTS, per_expert, 0)

        def drain_block(j, carry):
            pltpu.make_async_copy(zero_ref, xs_ref.at[pl.ds(0, bm)], zsem).wait()
            return carry

        def drain_group(j, carry):
            pltpu.make_async_copy(zero_ref.at[pl.ds(0, PAD_UNROLL * rt)],
                                  xs_ref.at[pl.ds(0, PAD_UNROLL * rt)], zsem).wait()
            return carry

        def drain_row(j, carry):
            pad_copy(0).wait()
            return carry

        lax.fori_loop(0, total // MOE_BM, drain_block, 0)
        rest = total % MOE_BM
        lax.fori_loop(0, rest // PAD_UNROLL, drain_group, 0)
        lax.fori_loop(0, rest % PAD_UNROLL, drain_row, 0)

        def tail_copy(b):
            return pltpu.make_async_copy(
                zero_ref, xs_ref.at[pl.ds(pl.multiple_of(b * bm, bm), bm)], zsem)

        n_used = meta_ref[2 * N_EXPERTS]

        def tail(b, carry):
            tail_copy(b).start()
            return carry

        lax.fori_loop(n_used, n_blocks, tail, 0)

        def tail_drain(b, carry):
            tail_copy(0).wait()
            return carry

        lax.fori_loop(n_used, n_blocks, tail_drain, 0)

    def wait_step(parity):
        for _ in range(TOP_K):
            pltpu.make_async_copy(ring_ref.at[parity], xs_ref.at[pl.ds(0, tm * rt)],
                                  sem.at[parity]).wait()

    @pl.when(i > 0)
    def _():
        wait_step(1 - (i & 1))

    @pl.when(i == pl.num_programs(0) - 1)
    def _():
        wait_step(i & 1)


def _dispatch(meta, dest, xn_t, n_tok, n_rows, tm=256):
    rt = xn_t.shape[0] // n_tok
    n_blocks = n_rows // MOE_BM
    grid_spec = pltpu.PrefetchScalarGridSpec(
        num_scalar_prefetch=1,
        grid=(n_tok // tm,),
        in_specs=[
            pl.BlockSpec((SUBLANES, tm), lambda i, meta: (0, i), memory_space=pltpu.SMEM),
            pl.BlockSpec((tm * rt, LANES), lambda i, meta: (i, 0)),
        ],
        out_specs=pl.BlockSpec(memory_space=pl.ANY),
        scratch_shapes=[
            pltpu.VMEM((MOE_BM * rt, LANES), xn_t.dtype),
            pltpu.VMEM((2, tm * rt, LANES), xn_t.dtype),
            pltpu.SemaphoreType.DMA((2,)),
            pltpu.SemaphoreType.DMA,
        ],
    )
    return pl.pallas_call(
        functools.partial(_dispatch_kernel, tm=tm, rt=rt, n_blocks=n_blocks),
        grid_spec=grid_spec,
        out_shape=jax.ShapeDtypeStruct((n_rows * rt, LANES), xn_t.dtype),
        compiler_params=_cparams(1),
        name="dispatch",
    )(meta, dest, xn_t)


def _experts_kernel(be_ref, nu_ref, nxt_ref, par_ref, nv_ref, x_ref, bg_ref, bu_ref, bd_ref,
                    wg_hbm, wu_hbm, wd_hbm, y_ref, wf_ref, wb_ref, sem):
    i = pl.program_id(0)
    used = i < nu_ref[0]
    e = be_ref[i]
    changed = (i == 0) | (e != be_ref[jnp.maximum(i - 1, 0)])

    def weight_copies(expert, slot):
        return [pltpu.make_async_copy(w.at[0, expert], wf_ref.at[slot, j], sem.at[slot])
                for j, w in enumerate((wg_hbm, wu_hbm, wd_hbm))]

    @pl.when(i == 0)
    def _():
        for cp in weight_copies(e, par_ref[e]):
            cp.start()

    @pl.when(used & changed)
    def _():
        slot = par_ref[e]
        for cp in weight_copies(e, slot):
            cp.wait()
        nxt = nxt_ref[e]

        @pl.when(nxt < N_EXPERTS)
        def _():
            for cp in weight_copies(nxt, 1 - slot):
                cp.start()

        for j in range(3):
            wb_ref[j] = wf_ref[slot, j].astype(BF16)

    rt = x_ref.shape[0] // MOE_BM
    half = MOE_BM // 2

    def mlp(rows):
        x = _unpack_pairs_f32(_load_row_tiles(x_ref.at[pl.ds(0, rows * rt)], rows)).astype(BF16)
        gt = jnp.minimum(jnp.dot(x, wb_ref[0], preferred_element_type=F32) + bg_ref[...],
                         SWIGLU_LIMIT)
        up = jnp.clip(jnp.dot(x, wb_ref[1], preferred_element_type=F32) + bu_ref[...],
                      -SWIGLU_LIMIT, SWIGLU_LIMIT)
        hid = (up + 1.0) * (gt * _sigmoid(SWIGLU_ALPHA * gt))
        y = jnp.dot(hid.astype(BF16), wb_ref[2], preferred_element_type=F32) + bd_ref[...]
        _store_row_tiles(y_ref.at[pl.ds(0, rows * rt)], _pack_bf16_pairs(y))

    n_valid = nv_ref[i]

    @pl.when(used & (n_valid > half))
    def _():
        mlp(MOE_BM)

    @pl.when(used & (n_valid <= half))
    def _():
        mlp(half)
        y_ref[pl.ds(half * rt, half * rt), :] = jnp.zeros((half * rt, LANES), y_ref.dtype)

    @pl.when(jnp.logical_not(used))
    def _():
        y_ref[...] = jnp.zeros_like(y_ref)


def _experts(block_e, n_used, next_e, parity, block_valid, xs_t, w_gate, b_gate, w_up, b_up,
             w_down, b_down):
    d, d_ff = w_gate.shape[2:]
    assert d == d_ff, "weight staging buffers assume square expert matrices"
    rt = d // 2 // LANES
    n_blocks = xs_t.shape[0] // (MOE_BM * rt)
    bspec = lambda m: pl.BlockSpec((None, 1, m), lambda i, be, nu, nx, pa, nv: (be[i], 0, 0))
    hbm = pl.BlockSpec(memory_space=pl.ANY)
    grid_spec = pltpu.PrefetchScalarGridSpec(
        num_scalar_prefetch=5,
        grid=(n_blocks,),
        in_specs=[
            pl.BlockSpec((MOE_BM * rt, LANES), lambda i, be, nu, nx, pa, nv:
                         (jnp.maximum(jnp.minimum(i, nu[0] - 1), 0), 0)),
            bspec(d_ff), bspec(d_ff), bspec(d),
            hbm, hbm, hbm,
        ],
        out_specs=pl.BlockSpec((MOE_BM * rt, LANES), lambda i, be, nu, nx, pa, nv: (i, 0)),
        scratch_shapes=[
            pltpu.VMEM((2, 3, d, d_ff), F32),
            pltpu.VMEM((3, d, d_ff), BF16),
            pltpu.SemaphoreType.DMA((2,)),
        ],
    )
    return pl.pallas_call(
        _experts_kernel,
        grid_spec=grid_spec,
        out_shape=jax.ShapeDtypeStruct(xs_t.shape, U32),
        compiler_params=_cparams(1),
        name="experts",
    )(block_e, n_used, next_e, parity, block_valid, xs_t, b_gate, b_up, b_down,
      w_gate, w_up, w_down)


def _combine_kernel(dcur_ref, dnext_ref, h2_ref, route_ref, gain_ref, ys_ref, o_ref,
                    buf_ref, sem, *, tm, rt):
    i = pl.program_id(0)
    nsteps = pl.num_programs(0)
    slot = i & 1

    def issue_all(dref, s):
        def per_token(tk, carry):
            for kk in range(TOP_K):
                r = dref[kk, tk]
                pltpu.make_async_copy(_row_tile(ys_ref, r, rt),
                                      _row_tile(buf_ref.at[s, kk], tk, rt),
                                      sem.at[s]).start(priority=kk % 2)
            return carry
        lax.fori_loop(0, tm, per_token, 0, unroll=4)

    @pl.when(i == 0)
    def _():
        issue_all(dcur_ref, 0)

    @pl.when(i + 1 < nsteps)
    def _():
        issue_all(dnext_ref, 1 - slot)

    for kk in range(TOP_K):
        pltpu.make_async_copy(ys_ref.at[pl.ds(0, tm * rt)], buf_ref.at[slot, kk],
                              sem.at[slot]).wait()

    rec = route_ref[...]
    acc = h2_ref[...]
    for kk in range(TOP_K):
        acc = acc + (rec[:, ROUTE_GATE + kk:ROUTE_GATE + kk + 1]
                     * _unpack_pairs_f32(_load_row_tiles(buf_ref.at[slot, kk], tm)))
    o_ref[...] = _rmsnorm(acc, gain_ref[...])


def _combine(dest, h2, route, gain, ys_t, tm=256):
    n, d = h2.shape
    nsteps = n // tm
    rt = d // 2 // LANES
    return pl.pallas_call(
        functools.partial(_combine_kernel, tm=tm, rt=rt),
        grid=(nsteps,),
        in_specs=[
            pl.BlockSpec((SUBLANES, tm), lambda i: (0, i), memory_space=pltpu.SMEM),
            pl.BlockSpec((SUBLANES, tm), lambda i: (0, jnp.minimum(i + 1, nsteps - 1)),
                         memory_space=pltpu.SMEM),
            pl.BlockSpec((tm, d), lambda i: (i, 0)),
            pl.BlockSpec((tm, LANES), lambda i: (i, 0)),
            pl.BlockSpec((1, d), lambda i: (0, 0)),
            pl.BlockSpec(memory_space=pl.ANY),
        ],
        out_specs=pl.BlockSpec((tm, d), lambda i: (i, 0)),
        out_shape=jax.ShapeDtypeStruct((n, d), F32),
        scratch_shapes=[
            pltpu.VMEM((2, TOP_K, tm * rt, LANES), U32),
            pltpu.SemaphoreType.DMA((2,)),
        ],
        compiler_params=_cparams(1),
        name="combine",
    )(dest, dest, h2, route, gain, ys_t)


def kernel(x, mem, norm_mix, w_in, dn_conv, dn_a_log, dn_dt_bias, dn_norm, rg_conv, rg_conv_b, rg_w_a, rg_b_a, rg_w_x, rg_b_x, rg_lambda, w_out, norm_cross, norm_mem, w_cq, w_ckv, w_co, norm_moe, w_router, b_router, w_gate, b_gate, w_up, b_up, w_down, b_down, norm_final):
    batch, seq, d = x.shape
    mem_len = mem.shape[1]
    n = batch * seq
    assert w_in.shape[0] == 1, "single-layer trunk"
    x2 = x.reshape(n, d)

    wi = w_in[0]
    n_gate = 2 * DN_HEADS
    w_cat = jnp.concatenate(
        [wi[:, :4 * D_DN], wi[:, 4 * D_DN + n_gate:],
         jnp.pad(wi[:, 4 * D_DN:4 * D_DN + n_gate], ((0, 0), (0, LANES - n_gate)))],
        axis=1).astype(BF16)
    prow = (jnp.zeros((SUBLANES, LANES), F32)
            .at[0, DN_HEADS:n_gate].set(dn_a_log[0]).at[1, DN_HEADS:n_gate].set(dn_dt_bias[0]))
    pcol = (jnp.zeros((n_gate, LANES), F32)
            .at[DN_HEADS:, 0].set(dn_a_log[0]).at[DN_HEADS:, 1].set(dn_dt_bias[0]))

    proj, gbt = _in_proj(x2, norm_mix, w_cat, prow, pcol)
    dn = _deltanet(proj, gbt, dn_conv[0], dn_norm, batch, seq)
    rg = _rglru(proj, rg_conv[0], rg_conv_b, rg_w_a[0], rg_b_a[0].reshape(1, D_RG),
                rg_w_x[0], rg_b_x[0].reshape(1, D_RG), rg_lambda, batch, seq)
    kv = _mem_kv(mem.reshape(batch * mem_len, d), norm_mem, w_ckv[0])

    w_r = jnp.pad(w_router[0], ((0, 0), (0, LANES - N_EXPERTS)))
    w_r_hi = w_r.astype(BF16)
    w_r_lo = (w_r - w_r_hi.astype(F32)).astype(BF16)
    b_r = jnp.pad(b_router, ((0, 0), (0, LANES - N_EXPERTS)))
    h2, xn, route, route_t, counts = _post_mix(
        dn, rg, x2, w_out[0], norm_cross, w_cq[0], kv, w_co[0], norm_moe, w_r_hi, w_r_lo, b_r,
        seq, mem_len)

    n_blocks = n * TOP_K // MOE_BM + N_EXPERTS
    n_rows = n_blocks * MOE_BM
    cnt = counts[:, 0].astype(I32)
    padded = (cnt + MOE_BM - 1) // MOE_BM * MOE_BM
    pad_end = jnp.cumsum(padded)
    pad_start = pad_end - padded
    n_used = (pad_end[-1:] // MOE_BM).astype(I32)
    block_e = jnp.minimum(
        jnp.sum(pad_end[None, :] <= (jnp.arange(n_blocks, dtype=I32) * MOE_BM)[:, None], axis=1),
        N_EXPERTS - 1).astype(I32)
    meta = jnp.concatenate([pad_start + cnt, padded - cnt, n_used]).astype(I32)

    dest = _dest(pad_start.astype(I32), route_t)
    xs = _dispatch(meta, dest, xn, n, n_rows)
    has = cnt > 0
    eid = jnp.where(has, jnp.arange(N_EXPERTS, dtype=I32), N_EXPERTS)
    after = lax.cummin(eid, axis=0, reverse=True)
    next_e = jnp.concatenate([after[1:], jnp.full((1,), N_EXPERTS, I32)]).astype(I32)
    parity = ((jnp.cumsum(has.astype(I32)) - 1) & 1).astype(I32)
    block_first = jnp.arange(n_blocks, dtype=I32) * MOE_BM - pad_start[block_e]
    block_valid = jnp.clip(cnt[block_e] - block_first, 0, MOE_BM).astype(I32)
    ys = _experts(block_e, n_used, next_e, parity, block_valid, xs, w_gate, b_gate[0][:, None, :],
                  w_up, b_up[0][:, None, :], w_down, b_down[0][:, None, :])
    out = _combine(dest, h2, route, norm_final.reshape(1, d), ys)
    return out.reshape(batch, seq, d)
```

```python
import functools

import jax
import jax.numpy as jnp
from jax import lax
from jax.experimental import pallas as pl
from jax.experimental.pallas import tpu as pltpu

F32 = jnp.float32
BF16 = jnp.bfloat16
I32 = jnp.int32
U32 = jnp.uint32

EPS = 1e-6
LANES = 128
SUBLANES = 8
VMEM_LIMIT = 48 * 1024 * 1024

DN_HEADS = 4
DN_HEAD_DIM = 128
D_DN = DN_HEADS * DN_HEAD_DIM
D_RG = 512
RG_BLOCKS = 4
RG_C = 8.0
CONV_WIDTH = 4
XA_HEADS = 4
N_EXPERTS = 32
TOP_K = 4
SWIGLU_LIMIT = 7.0
SWIGLU_ALPHA = 1.702

DN_CHUNK = 128
INV_BASE = 16
MOE_BM = 512
PAD_UNROLL = 8

COL_RX = 4 * D_DN
COL_RY = COL_RX + D_RG
COL_GB = COL_RY + D_RG
PROJ_W = COL_GB + LANES


def _cparams(n_axes=1):
    return pltpu.CompilerParams(
        dimension_semantics=("arbitrary",) * n_axes, vmem_limit_bytes=VMEM_LIMIT)


def _mm(a, b):
    return jnp.dot(a.astype(BF16), b.astype(BF16), preferred_element_type=F32)


def _mm_nt(a, b):
    return lax.dot_general(a.astype(BF16), b.astype(BF16), (((1,), (1,)), ((), ())),
                           preferred_element_type=F32)


def _rmsnorm(x, g):
    return x * lax.rsqrt(jnp.mean(x * x, axis=-1, keepdims=True) + EPS) * g


def _sigmoid(x):
    return 0.5 * jnp.tanh(0.5 * x) + 0.5


def _mm_split(a, b_hi, b_lo):
    a_hi = a.astype(BF16)
    a_lo = (a - a_hi.astype(F32)).astype(BF16)
    return (jnp.dot(a_hi, b_hi, preferred_element_type=F32)
            + jnp.dot(a_hi, b_lo, preferred_element_type=F32)
            + jnp.dot(a_lo, b_hi, preferred_element_type=F32))


def _softplus(x):
    return jnp.maximum(x, 0.0) + jnp.log1p(jnp.exp(-jnp.abs(x)))


def _load_row_tiles(ref, rows):
    rt = ref.shape[0] // rows
    return jnp.concatenate([ref[pl.ds(s, rows, stride=rt), :] for s in range(rt)], axis=1)


def _store_row_tiles(ref, val):
    rows, w = val.shape
    rt = w // LANES
    for s in range(rt):
        ref[pl.ds(s, rows, stride=rt), :] = val[:, s * LANES:(s + 1) * LANES]


def _pack_bf16_pairs(x):
    half = x.shape[1] // 2
    hi = pltpu.bitcast(x[:, :half].astype(BF16).astype(F32), U32)
    lo = pltpu.bitcast(x[:, half:].astype(BF16).astype(F32), U32)
    return hi | (lo >> 16)


def _unpack_pairs_f32(p):
    hi = pltpu.bitcast(p & jnp.uint32(0xFFFF0000), F32)
    lo = pltpu.bitcast(p << 16, F32)
    return jnp.concatenate([hi, lo], axis=1)


def _load_segmented(ref):
    seg = ref.shape[0] // SUBLANES
    return jnp.concatenate([ref[pl.ds(j, SUBLANES, stride=seg), :] for j in range(seg)], axis=0)


def _causal_conv_segmented(xp, halo_ref, cs, w):
    tb = xp.shape[0]
    ng = w.shape[0] - 1
    prev = halo_ref[:, cs]
    last = xp[tb - ng * SUBLANES:, :]
    halo_ref[:, cs] = last
    sub = lax.broadcasted_iota(I32, (SUBLANES, xp.shape[1]), 0)
    groups = []
    for g in range(ng):
        rows = slice(g * SUBLANES, (g + 1) * SUBLANES)
        groups.append(jnp.where(sub == 0, pltpu.roll(prev[rows], 1, 0),
                                pltpu.roll(last[rows], 1, 0)))
    ext = jnp.concatenate(groups + [xp], axis=0)
    y = w[ng:ng + 1] * xp
    for k in range(1, ng + 1):
        y = y + w[ng - k:ng - k + 1] * ext[(ng - k) * SUBLANES:(ng - k) * SUBLANES + tb]
    return y


def _in_proj_kernel(x_ref, g_ref, w_ref, prow_ref, pcol_ref, proj_ref, gbt_ref):
    u = _rmsnorm(x_ref[...], g_ref[...]).astype(BF16)
    for c0 in range(0, COL_GB, 512):
        proj_ref[:, c0:c0 + 512] = jnp.dot(u, w_ref[:, c0:c0 + 512], preferred_element_type=F32)
    ba = jnp.dot(u, w_ref[:, COL_GB:PROJ_W], preferred_element_type=F32)
    lane = lax.broadcasted_iota(I32, ba.shape, 1)
    g = -jnp.exp(prow_ref[0:1, :]) * _softplus(ba + prow_ref[1:2, :])
    proj_ref[:, COL_GB:PROJ_W] = jnp.where(lane < DN_HEADS, _sigmoid(ba), g)
    bat = lax.dot_general(w_ref[:, COL_GB:PROJ_W], u, (((0,), (1,)), ((), ())),
                          preferred_element_type=F32)[0:2 * DN_HEADS, :]
    row = lax.broadcasted_iota(I32, bat.shape, 0)
    gt = -jnp.exp(pcol_ref[:, 0:1]) * _softplus(bat + pcol_ref[:, 1:2])
    gbt_ref[...] = jnp.where(row < DN_HEADS, _sigmoid(bat), gt)


def _in_proj(x2, gain, w_cat, prow, pcol, tm=512):
    n, d = x2.shape
    return pl.pallas_call(
        _in_proj_kernel,
        grid=(n // tm,),
        in_specs=[
            pl.BlockSpec((tm, d), lambda i: (i, 0)),
            pl.BlockSpec((1, d), lambda i: (0, 0)),
            pl.BlockSpec((d, PROJ_W), lambda i: (0, 0)),
            pl.BlockSpec((SUBLANES, LANES), lambda i: (0, 0)),
            pl.BlockSpec((2 * DN_HEADS, LANES), lambda i: (0, 0)),
        ],
        out_specs=[
            pl.BlockSpec((tm, PROJ_W), lambda i: (i, 0)),
            pl.BlockSpec((2 * DN_HEADS, tm), lambda i: (0, i)),
        ],
        out_shape=[
            jax.ShapeDtypeStruct((n, PROJ_W), F32),
            jax.ShapeDtypeStruct((2 * DN_HEADS, n), F32),
        ],
        compiler_params=_cparams(1),
        name="in_proj",
    )(x2, gain, w_cat, prow, pcol)


def _deltanet_kernel(*refs, tb):
    n_grp = 3 * DN_HEADS
    qkv_refs = refs[:n_grp]
    (z_ref, gb_ref, gbt_ref, conv_ref, norm_ref, o_ref,
     s_ref, halo_ref, act_ref, gcc_ref, gcr_ref,
     a_ref, p_ref, d_ref, qk_ref, rhs_ref, u_ref, wq_ref, kdt_ref) = refs[n_grp:]
    t = pl.program_id(1)
    c = DN_CHUNK
    dh = DN_HEAD_DIM

    @pl.when(t == 0)
    def _():
        s_ref[...] = jnp.zeros_like(s_ref)
        halo_ref[...] = jnp.zeros_like(halo_ref)

    sub = lax.broadcasted_iota(I32, (SUBLANES, dh), 0)
    for grp in range(n_grp):
        cs = slice(grp * dh, (grp + 1) * dh)
        x = qkv_refs[grp][...]
        prev = halo_ref[:, cs]
        halo_ref[:, cs] = x[tb - SUBLANES:, :]
        y = conv_ref[CONV_WIDTH - 1:CONV_WIDTH, cs] * x
        for k in range(1, CONV_WIDTH):
            xs = pltpu.roll(x, k, 0)
            head = jnp.where(sub < k, pltpu.roll(prev, k, 0), xs[:SUBLANES])
            xs = jnp.concatenate([head, xs[SUBLANES:]], axis=0)
            y = y + conv_ref[CONV_WIDTH - 1 - k:CONV_WIDTH - k, cs] * xs
        y = y * _sigmoid(y)
        if grp < 2 * DN_HEADS:
            y = y * lax.rsqrt(jnp.sum(y * y, axis=-1, keepdims=True) + EPS)
        if grp < DN_HEADS:
            y = y * (dh ** -0.5)
        act_ref[grp] = y

    gcol = gb_ref[...]
    rpos = lax.broadcasted_iota(I32, gcol.shape, 0) & (c - 1)
    d = 1
    while d < c:
        gcol = gcol + jnp.where(rpos >= d, pltpu.roll(gcol, d, 0), 0.0)
        d *= 2
    gcc_ref[...] = gcol
    grow = gbt_ref[...]
    lpos = lax.broadcasted_iota(I32, grow.shape, 1) & (c - 1)
    d = 1
    while d < c:
        grow = grow + jnp.where(lpos >= d, pltpu.roll(grow, d, 1), 0.0)
        d *= 2
    gcr_ref[...] = grow

    row = lax.broadcasted_iota(I32, (c, c), 0)
    col = lax.broadcasted_iota(I32, (c, c), 1)
    causal = row >= col
    strict = row > col
    eye = jnp.where(row == col, 1.0, 0.0)
    gain = norm_ref[...]
    n_chunks = tb // c
    probs = [(ci, h) for ci in range(n_chunks) for h in range(DN_HEADS)]

    decay_last = []
    for p, (ci, h) in enumerate(probs):
        rows = slice(ci * c, (ci + 1) * c)
        q = act_ref[h, rows, :]
        k = act_ref[DN_HEADS + h, rows, :]
        v = act_ref[2 * DN_HEADS + h, rows, :]
        beta = gb_ref[rows, h:h + 1]
        gc = gcc_ref[rows, DN_HEADS + h:DN_HEADS + h + 1]
        gr = gcr_ref[DN_HEADS + h:DN_HEADS + h + 1, rows]
        g_last = gc[c - 1:c, :]
        decay = jnp.where(causal, jnp.exp(jnp.where(causal, gc - gr, 0.0)), 0.0)
        kb = k * beta
        both = _mm_nt(jnp.concatenate([kb, q], axis=0), k)
        a_ref[p] = jnp.where(strict, both[:c] * decay, 0.0)
        qk_ref[p] = (both[c:] * decay).astype(BF16)
        egc = jnp.exp(gc)
        rhs_ref[p] = jnp.concatenate([v * beta, kb * egc], axis=1).astype(BF16)
        wq_ref[p, c:, :] = (q * egc).astype(BF16)
        kdt_ref[p] = (k * jnp.exp(g_last - gc)).T.astype(BF16)
        decay_last.append(jnp.exp(g_last))

    shift = INV_BASE.bit_length() - 1
    blk = (row >> shift) == (col >> shift)
    for p in range(len(probs)):
        diag = jnp.where(blk, a_ref[p], 0.0)
        p_ref[p] = eye - diag
        d_ref[p] = _mm(diag, diag).astype(BF16)
    for it in range(shift - 1):
        for p in range(len(probs)):
            pw = d_ref[p]
            inv = p_ref[p]
            p_ref[p] = inv + _mm(inv, pw)
            if it < shift - 2:
                d_ref[p] = _mm(pw, pw).astype(BF16)
    s = INV_BASE
    while s < c:
        sh = s.bit_length() - 1
        off = ((row >> (sh + 1)) == (col >> (sh + 1))) & ((row >> sh) != (col >> sh))
        for p in range(len(probs)):
            d_ref[p] = _mm(p_ref[p], jnp.where(off, a_ref[p], 0.0)).astype(BF16)
        for p in range(len(probs)):
            inv = p_ref[p]
            p_ref[p] = inv - _mm(d_ref[p], inv)
        s *= 2
    for p in range(len(probs)):
        uw = _mm(p_ref[p], rhs_ref[p])
        u_ref[p] = uw[:, :dh]
        wq_ref[p, :c, :] = uw[:, dh:].astype(BF16)

    for ci in range(n_chunks):
        rows = slice(ci * c, (ci + 1) * c)
        ps = [ci * DN_HEADS + h for h in range(DN_HEADS)]
        s_old = [s_ref[h] for h in range(DN_HEADS)]
        ws = [_mm(wq_ref[p], s_old[h]) for h, p in enumerate(ps)]
        v_new = [u_ref[p] - ws[h][:c] for h, p in enumerate(ps)]
        outs = [ws[h][c:] + _mm(qk_ref[p], v_new[h]) for h, p in enumerate(ps)]
        for h, p in enumerate(ps):
            s_ref[h] = s_old[h] * decay_last[p] + _mm(kdt_ref[p], v_new[h])
        for h in range(DN_HEADS):
            o = _rmsnorm(outs[h], gain)
            zz = z_ref[rows, h * dh:(h + 1) * dh]
            o_ref[rows, h * dh:(h + 1) * dh] = (o * (zz * _sigmoid(zz))).astype(o_ref.dtype)


def _deltanet(proj, gbt, dn_conv, dn_norm, batch, seq, tb=512):
    n = proj.shape[0]
    nt = seq // tb
    c = DN_CHUNK
    n_prob = (tb // c) * DN_HEADS
    n_grp = 3 * DN_HEADS
    grp = lambda j: pl.BlockSpec((tb, DN_HEAD_DIM), lambda b, t, j=j: (b * nt + t, j))
    return pl.pallas_call(
        functools.partial(_deltanet_kernel, tb=tb),
        grid=(batch, nt),
        in_specs=[grp(j) for j in range(n_grp)] + [
            pl.BlockSpec((tb, D_DN), lambda b, t: (b * nt + t, 3)),
            pl.BlockSpec((tb, LANES), lambda b, t: (b * nt + t, COL_GB // LANES)),
            pl.BlockSpec((2 * DN_HEADS, tb), lambda b, t: (0, b * nt + t)),
            pl.BlockSpec((CONV_WIDTH, 3 * D_DN), lambda b, t: (0, 0)),
            pl.BlockSpec((1, DN_HEAD_DIM), lambda b, t: (0, 0)),
        ],
        out_specs=pl.BlockSpec((tb, D_DN), lambda b, t: (b * nt + t, 0)),
        out_shape=jax.ShapeDtypeStruct((n, D_DN), BF16),
        scratch_shapes=[
            pltpu.VMEM((DN_HEADS, DN_HEAD_DIM, DN_HEAD_DIM), F32),
            pltpu.VMEM((SUBLANES, 3 * D_DN), F32),
            pltpu.VMEM((n_grp, tb, DN_HEAD_DIM), F32),
            pltpu.VMEM((tb, LANES), F32),
            pltpu.VMEM((2 * DN_HEADS, tb), F32),
            pltpu.VMEM((n_prob, c, c), F32),
            pltpu.VMEM((n_prob, c, c), F32),
            pltpu.VMEM((n_prob, c, c), BF16),
            pltpu.VMEM((n_prob, c, c), BF16),
            pltpu.VMEM((n_prob, c, 2 * DN_HEAD_DIM), BF16),
            pltpu.VMEM((n_prob, c, DN_HEAD_DIM), F32),
            pltpu.VMEM((n_prob, 2 * c, DN_HEAD_DIM), BF16),
            pltpu.VMEM((n_prob, DN_HEAD_DIM, c), BF16),
        ],
        compiler_params=_cparams(2),
        name="deltanet",
    )(*([proj] * (n_grp + 2)), gbt, dn_conv, dn_norm)


def _gelu_tanh(x):
    return 0.5 * x * (1.0 + jnp.tanh(0.7978845608028654 * (x + 0.044715 * (x * x * x))))


def _rglru_kernel(*refs, tb):
    rx_refs = refs[:RG_BLOCKS]
    (ry_ref, conv_ref, convb_ref, wa_ref, ba_ref, wx_ref, bx_ref, lam_ref,
     o_ref, hc_ref, halo_ref, a_ref, b_ref, h_ref) = refs[RG_BLOCKS:]
    t = pl.program_id(1)
    bw = D_RG // RG_BLOCKS
    seg = tb // SUBLANES

    @pl.when(t == 0)
    def _():
        hc_ref[...] = jnp.zeros_like(hc_ref)
        halo_ref[...] = jnp.zeros_like(halo_ref)

    log_sig = -_softplus(-lam_ref[...])
    rowi = lax.broadcasted_iota(I32, (tb, bw), 0)
    seq_start_row = jnp.where(t == 0, 0, -1)
    for nb in range(RG_BLOCKS):
        cs = slice(nb * bw, (nb + 1) * bw)
        xb = (_causal_conv_segmented(_load_segmented(rx_refs[nb]), halo_ref, cs,
                                     conv_ref[:, cs]) + convb_ref[:, cs])
        r = _sigmoid(_mm(xb, wa_ref[nb]) + ba_ref[:, cs])
        gi = _sigmoid(_mm(xb, wx_ref[nb]) + bx_ref[:, cs])
        log_a = RG_C * r * log_sig[:, cs]
        a = jnp.exp(log_a)
        y = jnp.tanh(-log_a) * (1.0 + a * a)
        mult = jnp.where(y > 0.0, y * lax.rsqrt(y), 0.0)
        mult = jnp.where(rowi == seq_start_row, 1.0, mult)
        a_ref[:, cs] = a
        b_ref[:, cs] = mult * (gi * xb)

    def scan(j, carry):
        ac, bc = carry
        rows = pl.ds(pl.multiple_of(j * SUBLANES, SUBLANES), SUBLANES)
        aj = a_ref[rows, :]
        bc = aj * bc + b_ref[rows, :]
        ac = aj * ac
        a_ref[rows, :] = ac
        b_ref[rows, :] = bc
        return ac, bc

    ac, bc = lax.fori_loop(1, seg, scan, (a_ref[0:SUBLANES, :], b_ref[0:SUBLANES, :]), unroll=7)

    h = hc_ref[...]
    h_in = []
    for s in range(SUBLANES):
        h_in.append(h)
        h = ac[s:s + 1, :] * h + bc[s:s + 1, :]
    hc_ref[...] = h
    h_in = jnp.concatenate(h_in, axis=0)

    for j in range(seg):
        rows = slice(j * SUBLANES, (j + 1) * SUBLANES)
        hj = a_ref[rows, :] * h_in + b_ref[rows, :]
        for nb in range(RG_BLOCKS):
            h_ref[nb, rows, :] = hj[:, nb * bw:(nb + 1) * bw]
    g = seg // SUBLANES
    for nb in range(RG_BLOCKS):
        cs = slice(nb * bw, (nb + 1) * bw)
        h = jnp.concatenate(
            [h_ref[nb, pl.ds((i % g) * SUBLANES * SUBLANES + i // g, SUBLANES, stride=SUBLANES), :]
             for i in range(seg)], axis=0)
        o_ref[:, cs] = (h * _gelu_tanh(ry_ref[:, cs])).astype(o_ref.dtype)


def _rglru(proj, rg_conv, rg_conv_b, w_a, b_a, w_x, b_x, lam, batch, seq, tb=512):
    n = proj.shape[0]
    nt = seq // tb
    bw = D_RG // RG_BLOCKS
    full = lambda shape: pl.BlockSpec(shape, lambda b, t: (0,) * len(shape))
    return pl.pallas_call(
        functools.partial(_rglru_kernel, tb=tb),
        grid=(batch, nt),
        in_specs=[pl.BlockSpec((tb, bw), lambda b, t, j=j: (b * nt + t, COL_RX // bw + j))
                  for j in range(RG_BLOCKS)] + [
            pl.BlockSpec((tb, D_RG), lambda b, t: (b * nt + t, COL_RY // D_RG)),
            full((CONV_WIDTH, D_RG)), full((1, D_RG)),
            full((RG_BLOCKS, bw, bw)), full((1, D_RG)),
            full((RG_BLOCKS, bw, bw)), full((1, D_RG)),
            full((1, D_RG)),
        ],
        out_specs=pl.BlockSpec((tb, D_RG), lambda b, t: (b * nt + t, 0)),
        out_shape=jax.ShapeDtypeStruct((n, D_RG), BF16),
        scratch_shapes=[
            pltpu.VMEM((1, D_RG), F32),
            pltpu.VMEM(((CONV_WIDTH - 1) * SUBLANES, D_RG), F32),
            pltpu.VMEM((tb, D_RG), F32),
            pltpu.VMEM((tb, D_RG), F32),
            pltpu.VMEM((RG_BLOCKS, tb, bw), F32),
        ],
        compiler_params=_cparams(2),
        name="rglru",
    )(*([proj] * (RG_BLOCKS + 1)), rg_conv, rg_conv_b, w_a, b_a, w_x, b_x, lam)


def _mem_kv_kernel(m_ref, g_ref, w_ref, o_ref):
    mn = _rmsnorm(m_ref[...], g_ref[...]).astype(BF16)
    for c0 in range(0, o_ref.shape[1], 512):
        o_ref[:, c0:c0 + 512] = jnp.dot(mn, w_ref[:, c0:c0 + 512].astype(BF16),
                                        preferred_element_type=F32).astype(o_ref.dtype)


def _mem_kv(mem2, gain, w_ckv):
    n, d = mem2.shape
    return pl.pallas_call(
        _mem_kv_kernel,
        grid=(1,),
        in_specs=[
            pl.BlockSpec((n, d), lambda i: (0, 0)),
            pl.BlockSpec((1, d), lambda i: (0, 0)),
            pl.BlockSpec((d, 2 * d), lambda i: (0, 0)),
        ],
        out_specs=pl.BlockSpec((n, 2 * d), lambda i: (0, 0)),
        out_shape=jax.ShapeDtypeStruct((n, 2 * d), BF16),
        compiler_params=_cparams(1),
        name="mem_kv",
    )(mem2, gain, w_ckv)


ROUTE_E = 0
ROUTE_RANK = TOP_K
ROUTE_GATE = 2 * TOP_K
ROUTE_ROWS = 16


def _post_mix_kernel(dn_ref, rg_ref, x_ref, wo_f32, gx_ref, wq_f32, kv_ref, wco_f32, gm_ref,
                     wrh_ref, wrl_ref, br_ref, h2_ref, xn_ref, route_ref, routet_ref, cnt_ref,
                     carry_ref, wo_ref, wq_ref, wco_ref):
    i = pl.program_id(0)
    tm, d = x_ref.shape
    hd = d // XA_HEADS

    @pl.when(i == 0)
    def _():
        carry_ref[...] = jnp.zeros_like(carry_ref)
        wo_ref[...] = wo_f32[...].astype(BF16)
        wq_ref[...] = wq_f32[...].astype(BF16)
        wco_ref[...] = wco_f32[...].astype(BF16)

    h1 = (x_ref[...] + jnp.dot(dn_ref[...], wo_ref[0:D_DN, :], preferred_element_type=F32)
          + jnp.dot(rg_ref[...], wo_ref[D_DN:, :], preferred_element_type=F32))

    hn = _rmsnorm(h1, gx_ref[...]).astype(BF16)
    q = jnp.dot(hn, wq_ref[...], preferred_element_type=F32)
    heads = []
    for hh in range(XA_HEADS):
        cs = slice(hh * hd, (hh + 1) * hd)
        s = _mm_nt(q[:, cs], kv_ref[:, cs]) * (hd ** -0.5)
        p = jnp.exp(s - jnp.max(s, axis=-1, keepdims=True))
        p = p / jnp.sum(p, axis=-1, keepdims=True)
        heads.append(_mm(p, kv_ref[:, d + hh * hd:d + (hh + 1) * hd]).astype(BF16))
    o = jnp.concatenate(heads, axis=1)
    h2 = h1 + jnp.dot(o, wco_ref[...], preferred_element_type=F32)
    h2_ref[...] = h2

    xn = _rmsnorm(h2, gm_ref[...])
    _store_row_tiles(xn_ref, _pack_bf16_pairs(xn))
    logits = _mm_split(xn, wrh_ref[...], wrl_ref[...]) + br_ref[...]
    lg = logits.T[0:N_EXPERTS, :]
    eidx = lax.broadcasted_iota(I32, lg.shape, 0).astype(F32)
    neg = jnp.float32(-jnp.inf)
    vals, idxs, hots = [], [], []
    for _ in range(TOP_K):
        m = jnp.max(lg, axis=0, keepdims=True)
        idx = jnp.min(jnp.where(lg == m, eidx, float(N_EXPERTS)), axis=0, keepdims=True)
        hot = eidx == idx
        lg = jnp.where(hot, neg, lg)
        vals.append(m)
        idxs.append(idx)
        hots.append(hot)
    es = [jnp.exp(v - vals[0]) for v in vals]
    den = es[0] + es[1] + es[2] + es[3]
    gates = [e / den for e in es]

    chosen = jnp.zeros(lg.shape, F32)
    for hot in hots:
        chosen = chosen + jnp.where(hot, 1.0, 0.0)
    r2 = lax.broadcasted_iota(I32, (tm, tm), 0)
    c2 = lax.broadcasted_iota(I32, (tm, tm), 1)
    before = _mm(chosen, jnp.where(r2 < c2, 1.0, 0.0)) + carry_ref[:, 0:1]
    ranks = [jnp.sum(jnp.where(hot, before, 0.0), axis=0, keepdims=True) for hot in hots]
    carry_ref[...] = carry_ref[...] + jnp.sum(chosen, axis=1, keepdims=True)
    cnt_ref[...] = carry_ref[...]

    row = lax.broadcasted_iota(I32, (ROUTE_ROWS, tm), 0)
    rect = jnp.zeros((ROUTE_ROWS, tm), F32)
    for kk in range(TOP_K):
        rect = jnp.where(row == ROUTE_E + kk, idxs[kk], rect)
        rect = jnp.where(row == ROUTE_RANK + kk, ranks[kk], rect)
        rect = jnp.where(row == ROUTE_GATE + kk, gates[kk], rect)
    routet_ref[...] = rect
    route_ref[...] = jnp.concatenate(
        [rect, jnp.zeros((LANES - ROUTE_ROWS, tm), F32)], axis=0).T


def _post_mix(dn, rg, x2, w_out, g_cross, w_cq, kv, w_co, g_moe, w_r_hi, w_r_lo, b_r, seq, mem_len,
              tm=512):
    n, d = x2.shape
    per_b = seq // tm
    rt_x = d // 2 // LANES
    full = lambda shape: pl.BlockSpec(shape, lambda i: (0,) * len(shape))
    once = lambda shape: pl.BlockSpec(shape, lambda i: (0,) * len(shape),
                                      pipeline_mode=pl.Buffered(1))
    return pl.pallas_call(
        _post_mix_kernel,
        grid=(n // tm,),
        in_specs=[
            pl.BlockSpec((tm, D_DN), lambda i: (i, 0)),
            pl.BlockSpec((tm, D_RG), lambda i: (i, 0)),
            pl.BlockSpec((tm, d), lambda i: (i, 0)),
            once((d, d)), full((1, d)), once((d, d)),
            pl.BlockSpec((mem_len, 2 * d), lambda i: (i // per_b, 0)),
            once((d, d)), full((1, d)),
            full((d, LANES)), full((d, LANES)), full((1, LANES)),
        ],
        out_specs=[
            pl.BlockSpec((tm, d), lambda i: (i, 0)),
            pl.BlockSpec((tm * rt_x, LANES), lambda i: (i, 0)),
            pl.BlockSpec((tm, LANES), lambda i: (i, 0)),
            pl.BlockSpec((ROUTE_ROWS, tm), lambda i: (0, i)),
            pl.BlockSpec((N_EXPERTS, LANES), lambda i: (0, 0)),
        ],
        out_shape=[
            jax.ShapeDtypeStruct((n, d), F32),
            jax.ShapeDtypeStruct((n * rt_x, LANES), U32),
            jax.ShapeDtypeStruct((n, LANES), F32),
            jax.ShapeDtypeStruct((ROUTE_ROWS, n), F32),
            jax.ShapeDtypeStruct((N_EXPERTS, LANES), F32),
        ],
        scratch_shapes=[pltpu.VMEM((N_EXPERTS, LANES), F32),
                        pltpu.VMEM((d, d), BF16), pltpu.VMEM((d, d), BF16), pltpu.VMEM((d, d), BF16)],
        compiler_params=_cparams(1),
        name="post_mix",
    )(dn, rg, x2, w_out, g_cross, w_cq, kv, w_co, g_moe, w_r_hi, w_r_lo, b_r)


def _dest_kernel(start_ref, routet_ref, dest_ref):
    e = routet_ref[ROUTE_E:ROUTE_E + SUBLANES, :]
    rank = routet_ref[ROUTE_RANK:ROUTE_RANK + SUBLANES, :]
    base = jnp.zeros(e.shape, F32)
    for j in range(N_EXPERTS):
        base = jnp.where(e == float(j), start_ref[j].astype(F32), base)
    row = lax.broadcasted_iota(I32, e.shape, 0)
    dest_ref[...] = jnp.where(row < TOP_K, (base + rank).astype(I32), 0)


def _dest(pad_start, route_t, tm=2048):
    n = route_t.shape[1]
    grid_spec = pltpu.PrefetchScalarGridSpec(
        num_scalar_prefetch=1,
        grid=(n // tm,),
        in_specs=[pl.BlockSpec((ROUTE_ROWS, tm), lambda i, st: (0, i))],
        out_specs=pl.BlockSpec((SUBLANES, tm), lambda i, st: (0, i)),
    )
    return pl.pallas_call(
        _dest_kernel,
        grid_spec=grid_spec,
        out_shape=jax.ShapeDtypeStruct((SUBLANES, n), I32),
        compiler_params=_cparams(1),
        name="dest",
    )(pad_start, route_t)


def _row_tile(ref, r, rt):
    return ref.at[pl.ds(pl.multiple_of(r * rt, rt), rt)]


def _dispatch_kernel(meta_ref, dest_ref, xn_ref, xs_ref, zero_ref, ring_ref, sem, zsem,
                     *, tm, rt, n_blocks):
    i = pl.program_id(0)
    bm = MOE_BM * rt
    ring = ring_ref.at[i & 1]
    ring[...] = xn_ref[...]

    def per_token(tk, carry):
        src = _row_tile(ring, tk, rt)
        for kk in range(TOP_K):
            r = dest_ref[kk, tk]
            pltpu.make_async_copy(src, _row_tile(xs_ref, r, rt),
                                  sem.at[i & 1]).start(priority=kk % 2)
        return carry

    lax.fori_loop(0, tm, per_token, 0, unroll=4)

    @pl.when(i == 0)
    def _():
        zero_ref[...] = jnp.zeros_like(zero_ref)

        def pad_copy(r):
            return pltpu.make_async_copy(zero_ref.at[pl.ds(0, rt)], _row_tile(xs_ref, r, rt), zsem)

        def per_expert(e, total):
            r0 = meta_ref[e]
            cnt = meta_ref[N_EXPERTS + e]

            def group(j, carry):
                for u in range(PAD_UNROLL):
                    pad_copy(r0 + j * PAD_UNROLL + u).start(priority=u % 2)
                return carry

            def one(r, carry):
                pad_copy(r0 + r).start()
                return carry

            groups = cnt // PAD_UNROLL
            lax.fori_loop(0, groups, group, 0)
            lax.fori_loop(groups * PAD_UNROLL, cnt, one, 0)
            return total + cnt

        total = lax.fori_loop(0, N_EXPERTS, per_expert, 0)

        def drain_block(j, carry):
            pltpu.make_async_copy(zero_ref, xs_ref.at[pl.ds(0, bm)], zsem).wait()
            return carry

        def drain_group(j, carry):
            pltpu.make_async_copy(zero_ref.at[pl.ds(0, PAD_UNROLL * rt)],
                                  xs_ref.at[pl.ds(0, PAD_UNROLL * rt)], zsem).wait()
            return carry

        def drain_row(j, carry):
            pad_copy(0).wait()
            return carry

        lax.fori_loop(0, total // MOE_BM, drain_block, 0)
        rest = total % MOE_BM
        lax.fori_loop(0, rest // PAD_UNROLL, drain_group, 0)
        lax.fori_loop(0, rest % PAD_UNROLL, drain_row, 0)

        def tail_copy(b):
            return pltpu.make_async_copy(
                zero_ref, xs_ref.at[pl.ds(pl.multiple_of(b * bm, bm), bm)], zsem)

        n_used = meta_ref[2 * N_EXPERTS]

        def tail(b, carry):
            tail_copy(b).start()
            return carry

        lax.fori_loop(n_used, n_blocks, tail, 0)

        def tail_drain(b, carry):
            tail_copy(0).wait()
            return carry

        lax.fori_loop(n_used, n_blocks, tail_drain, 0)

    def wait_step(parity):
        for _ in range(TOP_K):
            pltpu.make_async_copy(ring_ref.at[parity], xs_ref.at[pl.ds(0, tm * rt)],
                                  sem.at[parity]).wait()

    @pl.when(i > 0)
    def _():
        wait_step(1 - (i & 1))

    @pl.when(i == pl.num_programs(0) - 1)
    def _():
        wait_step(i & 1)


def _dispatch(meta, dest, xn_t, n_tok, n_rows, tm=256):
    rt = xn_t.shape[0] // n_tok
    n_blocks = n_rows // MOE_BM
    grid_spec = pltpu.PrefetchScalarGridSpec(
        num_scalar_prefetch=1,
        grid=(n_tok // tm,),
        in_specs=[
            pl.BlockSpec((SUBLANES, tm), lambda i, meta: (0, i), memory_space=pltpu.SMEM),
            pl.BlockSpec((tm * rt, LANES), lambda i, meta: (i, 0)),
        ],
        out_specs=pl.BlockSpec(memory_space=pl.ANY),
        scratch_shapes=[
            pltpu.VMEM((MOE_BM * rt, LANES), xn_t.dtype),
            pltpu.VMEM((2, tm * rt, LANES), xn_t.dtype),
            pltpu.SemaphoreType.DMA((2,)),
            pltpu.SemaphoreType.DMA,
        ],
    )
    return pl.pallas_call(
        functools.partial(_dispatch_kernel, tm=tm, rt=rt, n_blocks=n_blocks),
        grid_spec=grid_spec,
        out_shape=jax.ShapeDtypeStruct((n_rows * rt, LANES), xn_t.dtype),
        compiler_params=_cparams(1),
        name="dispatch",
    )(meta, dest, xn_t)


def _experts_kernel(be_ref, nu_ref, nxt_ref, par_ref, nv_ref, x_ref, bg_ref, bu_ref, bd_ref,
                    wg_hbm, wu_hbm, wd_hbm, y_ref, wf_ref, wb_ref, sem):
    i = pl.program_id(0)
    used = i < nu_ref[0]
    e = be_ref[i]
    changed = (i == 0) | (e != be_ref[jnp.maximum(i - 1, 0)])

    def weight_copies(expert, slot):
        return [pltpu.make_async_copy(w.at[0, expert], wf_ref.at[slot, j], sem.at[slot])
                for j, w in enumerate((wg_hbm, wu_hbm, wd_hbm))]

    @pl.when(i == 0)
    def _():
        for cp in weight_copies(e, par_ref[e]):
            cp.start()

    @pl.when(used & changed)
    def _():
        slot = par_ref[e]
        for cp in weight_copies(e, slot):
            cp.wait()
        nxt = nxt_ref[e]

        @pl.when(nxt < N_EXPERTS)
        def _():
            for cp in weight_copies(nxt, 1 - slot):
                cp.start()

        for j in range(3):
            wb_ref[j] = wf_ref[slot, j].astype(BF16)

    rt = x_ref.shape[0] // MOE_BM
    half = MOE_BM // 2

    def mlp(rows):
        x = _unpack_pairs_f32(_load_row_tiles(x_ref.at[pl.ds(0, rows * rt)], rows)).astype(BF16)
        gt = jnp.minimum(jnp.dot(x, wb_ref[0], preferred_element_type=F32) + bg_ref[...],
                         SWIGLU_LIMIT)
        up = jnp.clip(jnp.dot(x, wb_ref[1], preferred_element_type=F32) + bu_ref[...],
                      -SWIGLU_LIMIT, SWIGLU_LIMIT)
        hid = (up + 1.0) * (gt * _sigmoid(SWIGLU_ALPHA * gt))
        y = jnp.dot(hid.astype(BF16), wb_ref[2], preferred_element_type=F32) + bd_ref[...]
        _store_row_tiles(y_ref.at[pl.ds(0, rows * rt)], _pack_bf16_pairs(y))

    n_valid = nv_ref[i]

    @pl.when(used & (n_valid > half))
    def _():
        mlp(MOE_BM)

    @pl.when(used & (n_valid <= half))
    def _():
        mlp(half)
        y_ref[pl.ds(half * rt, half * rt), :] = jnp.zeros((half * rt, LANES), y_ref.dtype)

    @pl.when(jnp.logical_not(used))
    def _():
        y_ref[...] = jnp.zeros_like(y_ref)


def _experts(block_e, n_used, next_e, parity, block_valid, xs_t, w_gate, b_gate, w_up, b_up,
             w_down, b_down):
    d, d_ff = w_gate.shape[2:]
    assert d == d_ff, "weight staging buffers assume square expert matrices"
    rt = d // 2 // LANES
    n_blocks = xs_t.shape[0] // (MOE_BM * rt)
    bspec = lambda m: pl.BlockSpec((None, 1, m), lambda i, be, nu, nx, pa, nv: (be[i], 0, 0))
    hbm = pl.BlockSpec(memory_space=pl.ANY)
    grid_spec = pltpu.PrefetchScalarGridSpec(
        num_scalar_prefetch=5,
        grid=(n_blocks,),
        in_specs=[
            pl.BlockSpec((MOE_BM * rt, LANES), lambda i, be, nu, nx, pa, nv:
                         (jnp.maximum(jnp.minimum(i, nu[0] - 1), 0), 0)),
            bspec(d_ff), bspec(d_ff), bspec(d),
            hbm, hbm, hbm,
        ],
        out_specs=pl.BlockSpec((MOE_BM * rt, LANES), lambda i, be, nu, nx, pa, nv: (i, 0)),
        scratch_shapes=[
            pltpu.VMEM((2, 3, d, d_ff), F32),
            pltpu.VMEM((3, d, d_ff), BF16),
            pltpu.SemaphoreType.DMA((2,)),
        ],
    )
    return pl.pallas_call(
        _experts_kernel,
        grid_spec=grid_spec,
        out_shape=jax.ShapeDtypeStruct(xs_t.shape, U32),
        compiler_params=_cparams(1),
        name="experts",
    )(block_e, n_used, next_e, parity, block_valid, xs_t, b_gate, b_up, b_down,
      w_gate, w_up, w_down)


def _combine_kernel(dcur_ref, dnext_ref, h2_ref, route_ref, gain_ref, ys_ref, o_ref,
                    buf_ref, sem, *, tm, rt):
    i = pl.program_id(0)
    nsteps = pl.num_programs(0)
    slot = i & 1

    def issue_all(dref, s):
        def per_token(tk, carry):
            for kk in range(TOP_K):
                r = dref[kk, tk]
                pltpu.make_async_copy(_row_tile(ys_ref, r, rt),
                                      _row_tile(buf_ref.at[s, kk], tk, rt),
                                      sem.at[s]).start(priority=kk % 2)
            return carry
        lax.fori_loop(0, tm, per_token, 0, unroll=4)

    @pl.when(i == 0)
    def _():
        issue_all(dcur_ref, 0)

    @pl.when(i + 1 < nsteps)
    def _():
        issue_all(dnext_ref, 1 - slot)

    for kk in range(TOP_K):
        pltpu.make_async_copy(ys_ref.at[pl.ds(0, tm * rt)], buf_ref.at[slot, kk],
                              sem.at[slot]).wait()

    rec = route_ref[...]
    acc = h2_ref[...]
    for kk in range(TOP_K):
        acc = acc + (rec[:, ROUTE_GATE + kk:ROUTE_GATE + kk + 1]
                     * _unpack_pairs_f32(_load_row_tiles(buf_ref.at[slot, kk], tm)))
    o_ref[...] = _rmsnorm(acc, gain_ref[...])


def _combine(dest, h2, route, gain, ys_t, tm=256):
    n, d = h2.shape
    nsteps = n // tm
    rt = d // 2 // LANES
    return pl.pallas_call(
        functools.partial(_combine_kernel, tm=tm, rt=rt),
        grid=(nsteps,),
        in_specs=[
            pl.BlockSpec((SUBLANES, tm), lambda i: (0, i), memory_space=pltpu.SMEM),
            pl.BlockSpec((SUBLANES, tm), lambda i: (0, jnp.minimum(i + 1, nsteps - 1)),
                         memory_space=pltpu.SMEM),
            pl.BlockSpec((tm, d), lambda i: (i, 0)),
            pl.BlockSpec((tm, LANES), lambda i: (i, 0)),
            pl.BlockSpec((1, d), lambda i: (0, 0)),
            pl.BlockSpec(memory_space=pl.ANY),
        ],
        out_specs=pl.BlockSpec((tm, d), lambda i: (i, 0)),
        out_shape=jax.ShapeDtypeStruct((n, d), F32),
        scratch_shapes=[
            pltpu.VMEM((2, TOP_K, tm * rt, LANES), U32),
            pltpu.SemaphoreType.DMA((2,)),
        ],
        compiler_params=_cparams(1),
        name="combine",
    )(dest, dest, h2, route, gain, ys_t)


def kernel(x, mem, norm_mix, w_in, dn_conv, dn_a_log, dn_dt_bias, dn_norm, rg_conv, rg_conv_b, rg_w_a, rg_b_a, rg_w_x, rg_b_x, rg_lambda, w_out, norm_cross, norm_mem, w_cq, w_ckv, w_co, norm_moe, w_router, b_router, w_gate, b_gate, w_up, b_up, w_down, b_down, norm_final):
    batch, seq, d = x.shape
    mem_len = mem.shape[1]
    n = batch * seq
    assert w_in.shape[0] == 1, "single-layer trunk"
    x2 = x.reshape(n, d)

    wi = w_in[0]
    n_gate = 2 * DN_HEADS
    w_cat = jnp.concatenate(
        [wi[:, :4 * D_DN], wi[:, 4 * D_DN + n_gate:],
         jnp.pad(wi[:, 4 * D_DN:4 * D_DN + n_gate], ((0, 0), (0, LANES - n_gate)))],
        axis=1).astype(BF16)
    prow = (jnp.zeros((SUBLANES, LANES), F32)
            .at[0, DN_HEADS:n_gate].set(dn_a_log[0]).at[1, DN_HEADS:n_gate].set(dn_dt_bias[0]))
    pcol = (jnp.zeros((n_gate, LANES), F32)
            .at[DN_HEADS:, 0].set(dn_a_log[0]).at[DN_HEADS:, 1].set(dn_dt_bias[0]))

    proj, gbt = _in_proj(x2, norm_mix, w_cat, prow, pcol)
    dn = _deltanet(proj, gbt, dn_conv[0], dn_norm, batch, seq)
    rg = _rglru(proj, rg_conv[0], rg_conv_b, rg_w_a[0], rg_b_a[0].reshape(1, D_RG),
                rg_w_x[0], rg_b_x[0].reshape(1, D_RG), rg_lambda, batch, seq)
    kv = _mem_kv(mem.reshape(batch * mem_len, d), norm_mem, w_ckv[0])

    w_r = jnp.pad(w_router[0], ((0, 0), (0, LANES - N_EXPERTS)))
    w_r_hi = w_r.astype(BF16)
    w_r_lo = (w_r - w_r_hi.astype(F32)).astype(BF16)
    b_r = jnp.pad(b_router, ((0, 0), (0, LANES - N_EXPERTS)))
    h2, xn, route, route_t, counts = _post_mix(
        dn, rg, x2, w_out[0], norm_cross, w_cq[0], kv, w_co[0], norm_moe, w_r_hi, w_r_lo, b_r,
        seq, mem_len)

    n_blocks = n * TOP_K // MOE_BM + N_EXPERTS
    n_rows = n_blocks * MOE_BM
    cnt = counts[:, 0].astype(I32)
    padded = (cnt + MOE_BM - 1) // MOE_BM * MOE_BM
    pad_end = jnp.cumsum(padded)
    pad_start = pad_end - padded
    n_used = (pad_end[-1:] // MOE_BM).astype(I32)
    block_e = jnp.minimum(
        jnp.sum(pad_end[None, :] <= (jnp.arange(n_blocks, dtype=I32) * MOE_BM)[:, None], axis=1),
        N_EXPERTS - 1).astype(I32)
    meta = jnp.concatenate([pad_start + cnt, padded - cnt, n_used]).astype(I32)

    dest = _dest(pad_start.astype(I32), route_t)
    xs = _dispatch(meta, dest, xn, n, n_rows)
    has = cnt > 0
    eid = jnp.where(has, jnp.arange(N_EXPERTS, dtype=I32), N_EXPERTS)
    after = lax.cummin(eid, axis=0, reverse=True)
    next_e = jnp.concatenate([after[1:], jnp.full((1,), N_EXPERTS, I32)]).astype(I32)
    parity = ((jnp.cumsum(has.astype(I32)) - 1) & 1).astype(I32)
    pos = (jnp.arange(n_blocks, dtype=I32) * MOE_BM)[:, None]
    own = (pad_start[None, :] <= pos) & (pos < pad_end[None, :])
    left = jnp.sum(jnp.where(own, pad_start[None, :] + cnt[None, :], 0), axis=1) - pos[:, 0]
    block_valid = jnp.clip(left, 0, MOE_BM).astype(I32)
    ys = _experts(block_e, n_used, next_e, parity, block_valid, xs, w_gate, b_gate[0][:, None, :],
                  w_up, b_up[0][:, None, :], w_down, b_down[0][:, None, :])
    out = _combine(dest, h2, route, norm_final.reshape(1, d), ys)
    return out.reshape(batch, seq, d)
```

```python
import functools

import jax
import jax.numpy as jnp
from jax import lax
from jax.experimental import pallas as pl
from jax.experimental.pallas import tpu as pltpu

F32 = jnp.float32
BF16 = jnp.bfloat16
I32 = jnp.int32
U32 = jnp.uint32

EPS = 1e-6
LANES = 128
SUBLANES = 8
VMEM_LIMIT = 48 * 1024 * 1024

DN_HEADS = 4
DN_HEAD_DIM = 128
D_DN = DN_HEADS * DN_HEAD_DIM
D_RG = 512
RG_BLOCKS = 4
RG_C = 8.0
CONV_WIDTH = 4
XA_HEADS = 4
N_EXPERTS = 32
TOP_K = 4
SWIGLU_LIMIT = 7.0
SWIGLU_ALPHA = 1.702

DN_CHUNK = 128
INV_BASE = 16
MOE_BM = 512
PAD_UNROLL = 8

COL_RX = 4 * D_DN
COL_RY = COL_RX + D_RG
COL_GB = COL_RY + D_RG
PROJ_W = COL_GB + LANES


def _cparams(n_axes=1):
    return pltpu.CompilerParams(
        dimension_semantics=("arbitrary",) * n_axes, vmem_limit_bytes=VMEM_LIMIT)


def _mm(a, b):
    return jnp.dot(a.astype(BF16), b.astype(BF16), preferred_element_type=F32)


def _mm_nt(a, b):
    return lax.dot_general(a.astype(BF16), b.astype(BF16), (((1,), (1,)), ((), ())),
                           preferred_element_type=F32)


def _rmsnorm(x, g):
    return x * lax.rsqrt(jnp.mean(x * x, axis=-1, keepdims=True) + EPS) * g


def _sigmoid(x):
    return 0.5 * jnp.tanh(0.5 * x) + 0.5


def _mm_split(a, b_hi, b_lo):
    a_hi = a.astype(BF16)
    a_lo = (a - a_hi.astype(F32)).astype(BF16)
    return (jnp.dot(a_hi, b_hi, preferred_element_type=F32)
            + jnp.dot(a_hi, b_lo, preferred_element_type=F32)
            + jnp.dot(a_lo, b_hi, preferred_element_type=F32))


def _softplus(x):
    return jnp.maximum(x, 0.0) + jnp.log1p(jnp.exp(-jnp.abs(x)))


def _load_row_tiles(ref, rows):
    rt = ref.shape[0] // rows
    return jnp.concatenate([ref[pl.ds(s, rows, stride=rt), :] for s in range(rt)], axis=1)


def _store_row_tiles(ref, val):
    rows, w = val.shape
    rt = w // LANES
    for s in range(rt):
        ref[pl.ds(s, rows, stride=rt), :] = val[:, s * LANES:(s + 1) * LANES]


def _pack_bf16_pairs(x):
    half = x.shape[1] // 2
    hi = pltpu.bitcast(x[:, :half].astype(BF16).astype(F32), U32)
    lo = pltpu.bitcast(x[:, half:].astype(BF16).astype(F32), U32)
    return hi | (lo >> 16)


def _unpack_pairs_f32(p):
    hi = pltpu.bitcast(p & jnp.uint32(0xFFFF0000), F32)
    lo = pltpu.bitcast(p << 16, F32)
    return jnp.concatenate([hi, lo], axis=1)


def _load_segmented(ref):
    seg = ref.shape[0] // SUBLANES
    return jnp.concatenate([ref[pl.ds(j, SUBLANES, stride=seg), :] for j in range(seg)], axis=0)


def _causal_conv_segmented(xp, halo_ref, cs, w):
    tb = xp.shape[0]
    ng = w.shape[0] - 1
    prev = halo_ref[:, cs]
    last = xp[tb - ng * SUBLANES:, :]
    halo_ref[:, cs] = last
    sub = lax.broadcasted_iota(I32, (SUBLANES, xp.shape[1]), 0)
    groups = []
    for g in range(ng):
        rows = slice(g * SUBLANES, (g + 1) * SUBLANES)
        groups.append(jnp.where(sub == 0, pltpu.roll(prev[rows], 1, 0),
                                pltpu.roll(last[rows], 1, 0)))
    ext = jnp.concatenate(groups + [xp], axis=0)
    y = w[ng:ng + 1] * xp
    for k in range(1, ng + 1):
        y = y + w[ng - k:ng - k + 1] * ext[(ng - k) * SUBLANES:(ng - k) * SUBLANES + tb]
    return y


def _in_proj_kernel(x_ref, g_ref, w_ref, prow_ref, pcol_ref, proj_ref, gbt_ref):
    u = _rmsnorm(x_ref[...], g_ref[...]).astype(BF16)
    for c0 in range(0, COL_GB, 512):
        proj_ref[:, c0:c0 + 512] = jnp.dot(u, w_ref[:, c0:c0 + 512], preferred_element_type=F32)
    ba = jnp.dot(u, w_ref[:, COL_GB:PROJ_W], preferred_element_type=F32)
    lane = lax.broadcasted_iota(I32, ba.shape, 1)
    g = -jnp.exp(prow_ref[0:1, :]) * _softplus(ba + prow_ref[1:2, :])
    proj_ref[:, COL_GB:PROJ_W] = jnp.where(lane < DN_HEADS, _sigmoid(ba), g)
    bat = lax.dot_general(w_ref[:, COL_GB:PROJ_W], u, (((0,), (1,)), ((), ())),
                          preferred_element_type=F32)[0:2 * DN_HEADS, :]
    row = lax.broadcasted_iota(I32, bat.shape, 0)
    gt = -jnp.exp(pcol_ref[:, 0:1]) * _softplus(bat + pcol_ref[:, 1:2])
    gbt_ref[...] = jnp.where(row < DN_HEADS, _sigmoid(bat), gt)


def _in_proj(x2, gain, w_cat, prow, pcol, tm=512):
    n, d = x2.shape
    return pl.pallas_call(
        _in_proj_kernel,
        grid=(n // tm,),
        in_specs=[
            pl.BlockSpec((tm, d), lambda i: (i, 0)),
            pl.BlockSpec((1, d), lambda i: (0, 0)),
            pl.BlockSpec((d, PROJ_W), lambda i: (0, 0)),
            pl.BlockSpec((SUBLANES, LANES), lambda i: (0, 0)),
            pl.BlockSpec((2 * DN_HEADS, LANES), lambda i: (0, 0)),
        ],
        out_specs=[
            pl.BlockSpec((tm, PROJ_W), lambda i: (i, 0)),
            pl.BlockSpec((2 * DN_HEADS, tm), lambda i: (0, i)),
        ],
        out_shape=[
            jax.ShapeDtypeStruct((n, PROJ_W), F32),
            jax.ShapeDtypeStruct((2 * DN_HEADS, n), F32),
        ],
        compiler_params=_cparams(1),
        name="in_proj",
    )(x2, gain, w_cat, prow, pcol)


def _deltanet_kernel(*refs, tb):
    n_grp = 3 * DN_HEADS
    qkv_refs = refs[:n_grp]
    (z_ref, gb_ref, gbt_ref, conv_ref, norm_ref, o_ref,
     s_ref, halo_ref, act_ref, gcc_ref, gcr_ref,
     a_ref, p_ref, d_ref, qk_ref, rhs_ref, u_ref, wq_ref, kdt_ref) = refs[n_grp:]
    t = pl.program_id(1)
    c = DN_CHUNK
    dh = DN_HEAD_DIM

    @pl.when(t == 0)
    def _():
        s_ref[...] = jnp.zeros_like(s_ref)
        halo_ref[...] = jnp.zeros_like(halo_ref)

    sub = lax.broadcasted_iota(I32, (SUBLANES, dh), 0)
    for grp in range(n_grp):
        cs = slice(grp * dh, (grp + 1) * dh)
        x = qkv_refs[grp][...]
        prev = halo_ref[:, cs]
        halo_ref[:, cs] = x[tb - SUBLANES:, :]
        y = conv_ref[CONV_WIDTH - 1:CONV_WIDTH, cs] * x
        for k in range(1, CONV_WIDTH):
            xs = pltpu.roll(x, k, 0)
            head = jnp.where(sub < k, pltpu.roll(prev, k, 0), xs[:SUBLANES])
            xs = jnp.concatenate([head, xs[SUBLANES:]], axis=0)
            y = y + conv_ref[CONV_WIDTH - 1 - k:CONV_WIDTH - k, cs] * xs
        y = y * _sigmoid(y)
        if grp < 2 * DN_HEADS:
            y = y * lax.rsqrt(jnp.sum(y * y, axis=-1, keepdims=True) + EPS)
        if grp < DN_HEADS:
            y = y * (dh ** -0.5)
        act_ref[grp] = y

    gcol = gb_ref[...]
    rpos = lax.broadcasted_iota(I32, gcol.shape, 0) & (c - 1)
    d = 1
    while d < c:
        gcol = gcol + jnp.where(rpos >= d, pltpu.roll(gcol, d, 0), 0.0)
        d *= 2
    gcc_ref[...] = gcol
    grow = gbt_ref[...]
    lpos = lax.broadcasted_iota(I32, grow.shape, 1) & (c - 1)
    d = 1
    while d < c:
        grow = grow + jnp.where(lpos >= d, pltpu.roll(grow, d, 1), 0.0)
        d *= 2
    gcr_ref[...] = grow

    row = lax.broadcasted_iota(I32, (c, c), 0)
    col = lax.broadcasted_iota(I32, (c, c), 1)
    causal = row >= col
    strict = row > col
    eye = jnp.where(row == col, 1.0, 0.0)
    gain = norm_ref[...]
    n_chunks = tb // c
    probs = [(ci, h) for ci in range(n_chunks) for h in range(DN_HEADS)]

    decay_last = []
    for p, (ci, h) in enumerate(probs):
        rows = slice(ci * c, (ci + 1) * c)
        q = act_ref[h, rows, :]
        k = act_ref[DN_HEADS + h, rows, :]
        v = act_ref[2 * DN_HEADS + h, rows, :]
        beta = gb_ref[rows, h:h + 1]
        gc = gcc_ref[rows, DN_HEADS + h:DN_HEADS + h + 1]
        gr = gcr_ref[DN_HEADS + h:DN_HEADS + h + 1, rows]
        g_last = gc[c - 1:c, :]
        decay = jnp.where(causal, jnp.exp(jnp.where(causal, gc - gr, 0.0)), 0.0)
        kb = k * beta
        both = _mm_nt(jnp.concatenate([kb, q], axis=0), k)
        a_ref[p] = jnp.where(strict, both[:c] * decay, 0.0)
        qk_ref[p] = (both[c:] * decay).astype(BF16)
        egc = jnp.exp(gc)
        rhs_ref[p] = jnp.concatenate([v * beta, kb * egc], axis=1).astype(BF16)
        wq_ref[p, c:, :] = (q * egc).astype(BF16)
        kdt_ref[p] = (k * jnp.exp(g_last - gc)).T.astype(BF16)
        decay_last.append(jnp.exp(g_last))

    shift = INV_BASE.bit_length() - 1
    blk = (row >> shift) == (col >> shift)
    for p in range(len(probs)):
        diag = jnp.where(blk, a_ref[p], 0.0)
        p_ref[p] = eye - diag
        d_ref[p] = _mm(diag, diag).astype(BF16)
    for it in range(shift - 1):
        for p in range(len(probs)):
            pw = d_ref[p]
            inv = p_ref[p]
            p_ref[p] = inv + _mm(inv, pw)
            if it < shift - 2:
                d_ref[p] = _mm(pw, pw).astype(BF16)
    s = INV_BASE
    while s < c:
        sh = s.bit_length() - 1
        off = ((row >> (sh + 1)) == (col >> (sh + 1))) & ((row >> sh) != (col >> sh))
        for p in range(len(probs)):
            d_ref[p] = _mm(p_ref[p], jnp.where(off, a_ref[p], 0.0)).astype(BF16)
        for p in range(len(probs)):
            inv = p_ref[p]
            p_ref[p] = inv - _mm(d_ref[p], inv)
        s *= 2
    for p in range(len(probs)):
        uw = _mm(p_ref[p], rhs_ref[p])
        u_ref[p] = uw[:, :dh]
        wq_ref[p, :c, :] = uw[:, dh:].astype(BF16)

    for ci in range(n_chunks):
        rows = slice(ci * c, (ci + 1) * c)
        ps = [ci * DN_HEADS + h for h in range(DN_HEADS)]
        s_old = [s_ref[h] for h in range(DN_HEADS)]
        ws = [_mm(wq_ref[p], s_old[h]) for h, p in enumerate(ps)]
        v_new = [u_ref[p] - ws[h][:c] for h, p in enumerate(ps)]
        outs = [ws[h][c:] + _mm(qk_ref[p], v_new[h]) for h, p in enumerate(ps)]
        for h, p in enumerate(ps):
            s_ref[h] = s_old[h] * decay_last[p] + _mm(kdt_ref[p], v_new[h])
        for h in range(DN_HEADS):
            o = _rmsnorm(outs[h], gain)
            zz = z_ref[rows, h * dh:(h + 1) * dh]
            o_ref[rows, h * dh:(h + 1) * dh] = (o * (zz * _sigmoid(zz))).astype(o_ref.dtype)


def _deltanet(proj, gbt, dn_conv, dn_norm, batch, seq, tb=512):
    n = proj.shape[0]
    nt = seq // tb
    c = DN_CHUNK
    n_prob = (tb // c) * DN_HEADS
    n_grp = 3 * DN_HEADS
    grp = lambda j: pl.BlockSpec((tb, DN_HEAD_DIM), lambda b, t, j=j: (b * nt + t, j))
    return pl.pallas_call(
        functools.partial(_deltanet_kernel, tb=tb),
        grid=(batch, nt),
        in_specs=[grp(j) for j in range(n_grp)] + [
            pl.BlockSpec((tb, D_DN), lambda b, t: (b * nt + t, 3)),
            pl.BlockSpec((tb, LANES), lambda b, t: (b * nt + t, COL_GB // LANES)),
            pl.BlockSpec((2 * DN_HEADS, tb), lambda b, t: (0, b * nt + t)),
            pl.BlockSpec((CONV_WIDTH, 3 * D_DN), lambda b, t: (0, 0)),
            pl.BlockSpec((1, DN_HEAD_DIM), lambda b, t: (0, 0)),
        ],
        out_specs=pl.BlockSpec((tb, D_DN), lambda b, t: (b * nt + t, 0)),
        out_shape=jax.ShapeDtypeStruct((n, D_DN), BF16),
        scratch_shapes=[
            pltpu.VMEM((DN_HEADS, DN_HEAD_DIM, DN_HEAD_DIM), F32),
            pltpu.VMEM((SUBLANES, 3 * D_DN), F32),
            pltpu.VMEM((n_grp, tb, DN_HEAD_DIM), F32),
            pltpu.VMEM((tb, LANES), F32),
            pltpu.VMEM((2 * DN_HEADS, tb), F32),
            pltpu.VMEM((n_prob, c, c), F32),
            pltpu.VMEM((n_prob, c, c), F32),
            pltpu.VMEM((n_prob, c, c), BF16),
            pltpu.VMEM((n_prob, c, c), BF16),
            pltpu.VMEM((n_prob, c, 2 * DN_HEAD_DIM), BF16),
            pltpu.VMEM((n_prob, c, DN_HEAD_DIM), F32),
            pltpu.VMEM((n_prob, 2 * c, DN_HEAD_DIM), BF16),
            pltpu.VMEM((n_prob, DN_HEAD_DIM, c), BF16),
        ],
        compiler_params=_cparams(2),
        name="deltanet",
    )(*([proj] * (n_grp + 2)), gbt, dn_conv, dn_norm)


def _gelu_tanh(x):
    return 0.5 * x * (1.0 + jnp.tanh(0.7978845608028654 * (x + 0.044715 * (x * x * x))))


def _rglru_kernel(*refs, tb):
    rx_refs = refs[:RG_BLOCKS]
    (ry_ref, conv_ref, convb_ref, wa_ref, ba_ref, wx_ref, bx_ref, lam_ref,
     o_ref, hc_ref, halo_ref, a_ref, b_ref, h_ref) = refs[RG_BLOCKS:]
    t = pl.program_id(1)
    bw = D_RG // RG_BLOCKS
    seg = tb // SUBLANES

    @pl.when(t == 0)
    def _():
        hc_ref[...] = jnp.zeros_like(hc_ref)
        halo_ref[...] = jnp.zeros_like(halo_ref)

    log_sig = -_softplus(-lam_ref[...])
    rowi = lax.broadcasted_iota(I32, (tb, bw), 0)
    seq_start_row = jnp.where(t == 0, 0, -1)
    for nb in range(RG_BLOCKS):
        cs = slice(nb * bw, (nb + 1) * bw)
        xb = (_causal_conv_segmented(_load_segmented(rx_refs[nb]), halo_ref, cs,
                                     conv_ref[:, cs]) + convb_ref[:, cs])
        r = _sigmoid(_mm(xb, wa_ref[nb]) + ba_ref[:, cs])
        gi = _sigmoid(_mm(xb, wx_ref[nb]) + bx_ref[:, cs])
        log_a = RG_C * r * log_sig[:, cs]
        a = jnp.exp(log_a)
        y = jnp.tanh(-log_a) * (1.0 + a * a)
        mult = jnp.where(y > 0.0, y * lax.rsqrt(y), 0.0)
        mult = jnp.where(rowi == seq_start_row, 1.0, mult)
        a_ref[:, cs] = a
        b_ref[:, cs] = mult * (gi * xb)

    def scan(j, carry):
        ac, bc = carry
        rows = pl.ds(pl.multiple_of(j * SUBLANES, SUBLANES), SUBLANES)
        aj = a_ref[rows, :]
        bc = aj * bc + b_ref[rows, :]
        ac = aj * ac
        a_ref[rows, :] = ac
        b_ref[rows, :] = bc
        return ac, bc

    ac, bc = lax.fori_loop(1, seg, scan, (a_ref[0:SUBLANES, :], b_ref[0:SUBLANES, :]), unroll=7)

    h = hc_ref[...]
    h_in = []
    for s in range(SUBLANES):
        h_in.append(h)
        h = ac[s:s + 1, :] * h + bc[s:s + 1, :]
    hc_ref[...] = h
    h_in = jnp.concatenate(h_in, axis=0)

    for j in range(seg):
        rows = slice(j * SUBLANES, (j + 1) * SUBLANES)
        hj = a_ref[rows, :] * h_in + b_ref[rows, :]
        for nb in range(RG_BLOCKS):
            h_ref[nb, rows, :] = hj[:, nb * bw:(nb + 1) * bw]
    g = seg // SUBLANES
    for nb in range(RG_BLOCKS):
        cs = slice(nb * bw, (nb + 1) * bw)
        h = jnp.concatenate(
            [h_ref[nb, pl.ds((i % g) * SUBLANES * SUBLANES + i // g, SUBLANES, stride=SUBLANES), :]
             for i in range(seg)], axis=0)
        o_ref[:, cs] = (h * _gelu_tanh(ry_ref[:, cs])).astype(o_ref.dtype)


def _rglru(proj, rg_conv, rg_conv_b, w_a, b_a, w_x, b_x, lam, batch, seq, tb=512):
    n = proj.shape[0]
    nt = seq // tb
    bw = D_RG // RG_BLOCKS
    full = lambda shape: pl.BlockSpec(shape, lambda b, t: (0,) * len(shape))
    return pl.pallas_call(
        functools.partial(_rglru_kernel, tb=tb),
        grid=(batch, nt),
        in_specs=[pl.BlockSpec((tb, bw), lambda b, t, j=j: (b * nt + t, COL_RX // bw + j))
                  for j in range(RG_BLOCKS)] + [
            pl.BlockSpec((tb, D_RG), lambda b, t: (b * nt + t, COL_RY // D_RG)),
            full((CONV_WIDTH, D_RG)), full((1, D_RG)),
            full((RG_BLOCKS, bw, bw)), full((1, D_RG)),
            full((RG_BLOCKS, bw, bw)), full((1, D_RG)),
            full((1, D_RG)),
        ],
        out_specs=pl.BlockSpec((tb, D_RG), lambda b, t: (b * nt + t, 0)),
        out_shape=jax.ShapeDtypeStruct((n, D_RG), BF16),
        scratch_shapes=[
            pltpu.VMEM((1, D_RG), F32),
            pltpu.VMEM(((CONV_WIDTH - 1) * SUBLANES, D_RG), F32),
            pltpu.VMEM((tb, D_RG), F32),
            pltpu.VMEM((tb, D_RG), F32),
            pltpu.VMEM((RG_BLOCKS, tb, bw), F32),
        ],
        compiler_params=_cparams(2),
        name="rglru",
    )(*([proj] * (RG_BLOCKS + 1)), rg_conv, rg_conv_b, w_a, b_a, w_x, b_x, lam)


def _mem_kv_kernel(m_ref, g_ref, w_ref, o_ref):
    mn = _rmsnorm(m_ref[...], g_ref[...]).astype(BF16)
    for c0 in range(0, o_ref.shape[1], 512):
        o_ref[:, c0:c0 + 512] = jnp.dot(mn, w_ref[:, c0:c0 + 512].astype(BF16),
                                        preferred_element_type=F32).astype(o_ref.dtype)


def _mem_kv(mem2, gain, w_ckv):
    n, d = mem2.shape
    return pl.pallas_call(
        _mem_kv_kernel,
        grid=(1,),
        in_specs=[
            pl.BlockSpec((n, d), lambda i: (0, 0)),
            pl.BlockSpec((1, d), lambda i: (0, 0)),
            pl.BlockSpec((d, 2 * d), lambda i: (0, 0)),
        ],
        out_specs=pl.BlockSpec((n, 2 * d), lambda i: (0, 0)),
        out_shape=jax.ShapeDtypeStruct((n, 2 * d), BF16),
        compiler_params=_cparams(1),
        name="mem_kv",
    )(mem2, gain, w_ckv)


ROUTE_E = 0
ROUTE_RANK = TOP_K
ROUTE_GATE = 2 * TOP_K
ROUTE_ROWS = 16


def _post_mix_kernel(dn_ref, rg_ref, x_ref, wo_f32, gx_ref, wq_f32, kv_ref, wco_f32, gm_ref,
                     wrh_ref, wrl_ref, br_ref, h2_ref, xn_ref, route_ref, routet_ref, cnt_ref,
                     carry_ref, wo_ref, wq_ref, wco_ref):
    i = pl.program_id(0)
    tm, d = x_ref.shape
    hd = d // XA_HEADS

    @pl.when(i == 0)
    def _():
        carry_ref[...] = jnp.zeros_like(carry_ref)
        wo_ref[...] = wo_f32[...].astype(BF16)
        wq_ref[...] = wq_f32[...].astype(BF16)
        wco_ref[...] = wco_f32[...].astype(BF16)

    h1 = (x_ref[...] + jnp.dot(dn_ref[...], wo_ref[0:D_DN, :], preferred_element_type=F32)
          + jnp.dot(rg_ref[...], wo_ref[D_DN:, :], preferred_element_type=F32))

    hn = _rmsnorm(h1, gx_ref[...]).astype(BF16)
    q = jnp.dot(hn, wq_ref[...], preferred_element_type=F32)
    heads = []
    for hh in range(XA_HEADS):
        cs = slice(hh * hd, (hh + 1) * hd)
        s = _mm_nt(q[:, cs], kv_ref[:, cs]) * (hd ** -0.5)
        p = jnp.exp(s - jnp.max(s, axis=-1, keepdims=True))
        p = p / jnp.sum(p, axis=-1, keepdims=True)
        heads.append(_mm(p, kv_ref[:, d + hh * hd:d + (hh + 1) * hd]).astype(BF16))
    o = jnp.concatenate(heads, axis=1)
    h2 = h1 + jnp.dot(o, wco_ref[...], preferred_element_type=F32)
    h2_ref[...] = h2

    xn = _rmsnorm(h2, gm_ref[...])
    _store_row_tiles(xn_ref, _pack_bf16_pairs(xn))
    logits = _mm_split(xn, wrh_ref[...], wrl_ref[...]) + br_ref[...]
    lg = logits.T[0:N_EXPERTS, :]
    eidx = lax.broadcasted_iota(I32, lg.shape, 0).astype(F32)
    neg = jnp.float32(-jnp.inf)
    vals, idxs, hots = [], [], []
    for _ in range(TOP_K):
        m = jnp.max(lg, axis=0, keepdims=True)
        idx = jnp.min(jnp.where(lg == m, eidx, float(N_EXPERTS)), axis=0, keepdims=True)
        hot = eidx == idx
        lg = jnp.where(hot, neg, lg)
        vals.append(m)
        idxs.append(idx)
        hots.append(hot)
    es = [jnp.exp(v - vals[0]) for v in vals]
    den = es[0] + es[1] + es[2] + es[3]
    gates = [e / den for e in es]

    chosen = jnp.zeros(lg.shape, F32)
    for hot in hots:
        chosen = chosen + jnp.where(hot, 1.0, 0.0)
    r2 = lax.broadcasted_iota(I32, (tm, tm), 0)
    c2 = lax.broadcasted_iota(I32, (tm, tm), 1)
    before = _mm(chosen, jnp.where(r2 < c2, 1.0, 0.0)) + carry_ref[:, 0:1]
    ranks = [jnp.sum(jnp.where(hot, before, 0.0), axis=0, keepdims=True) for hot in hots]
    carry_ref[...] = carry_ref[...] + jnp.sum(chosen, axis=1, keepdims=True)
    cnt_ref[...] = carry_ref[...]

    row = lax.broadcasted_iota(I32, (ROUTE_ROWS, tm), 0)
    rect = jnp.zeros((ROUTE_ROWS, tm), F32)
    for kk in range(TOP_K):
        rect = jnp.where(row == ROUTE_E + kk, idxs[kk], rect)
        rect = jnp.where(row == ROUTE_RANK + kk, ranks[kk], rect)
        rect = jnp.where(row == ROUTE_GATE + kk, gates[kk], rect)
    routet_ref[...] = rect
    route_ref[...] = jnp.concatenate(
        [rect, jnp.zeros((LANES - ROUTE_ROWS, tm), F32)], axis=0).T


def _post_mix(dn, rg, x2, w_out, g_cross, w_cq, kv, w_co, g_moe, w_r_hi, w_r_lo, b_r, seq, mem_len,
              tm=512):
    n, d = x2.shape
    per_b = seq // tm
    rt_x = d // 2 // LANES
    full = lambda shape: pl.BlockSpec(shape, lambda i: (0,) * len(shape))
    once = lambda shape: pl.BlockSpec(shape, lambda i: (0,) * len(shape),
                                      pipeline_mode=pl.Buffered(1))
    return pl.pallas_call(
        _post_mix_kernel,
        grid=(n // tm,),
        in_specs=[
            pl.BlockSpec((tm, D_DN), lambda i: (i, 0)),
            pl.BlockSpec((tm, D_RG), lambda i: (i, 0)),
            pl.BlockSpec((tm, d), lambda i: (i, 0)),
            once((d, d)), full((1, d)), once((d, d)),
            pl.BlockSpec((mem_len, 2 * d), lambda i: (i // per_b, 0)),
            once((d, d)), full((1, d)),
            full((d, LANES)), full((d, LANES)), full((1, LANES)),
        ],
        out_specs=[
            pl.BlockSpec((tm, d), lambda i: (i, 0)),
            pl.BlockSpec((tm * rt_x, LANES), lambda i: (i, 0)),
            pl.BlockSpec((tm, LANES), lambda i: (i, 0)),
            pl.BlockSpec((ROUTE_ROWS, tm), lambda i: (0, i)),
            pl.BlockSpec((N_EXPERTS, LANES), lambda i: (0, 0)),
        ],
        out_shape=[
            jax.ShapeDtypeStruct((n, d), F32),
            jax.ShapeDtypeStruct((n * rt_x, LANES), U32),
            jax.ShapeDtypeStruct((n, LANES), F32),
            jax.ShapeDtypeStruct((ROUTE_ROWS, n), F32),
            jax.ShapeDtypeStruct((N_EXPERTS, LANES), F32),
        ],
        scratch_shapes=[pltpu.VMEM((N_EXPERTS, LANES), F32),
                        pltpu.VMEM((d, d), BF16), pltpu.VMEM((d, d), BF16), pltpu.VMEM((d, d), BF16)],
        compiler_params=_cparams(1),
        name="post_mix",
    )(dn, rg, x2, w_out, g_cross, w_cq, kv, w_co, g_moe, w_r_hi, w_r_lo, b_r)


def _dest_kernel(start_ref, routet_ref, dest_ref):
    e = routet_ref[ROUTE_E:ROUTE_E + SUBLANES, :]
    rank = routet_ref[ROUTE_RANK:ROUTE_RANK + SUBLANES, :]
    base = jnp.zeros(e.shape, F32)
    for j in range(N_EXPERTS):
        base = jnp.where(e == float(j), start_ref[j].astype(F32), base)
    row = lax.broadcasted_iota(I32, e.shape, 0)
    dest_ref[...] = jnp.where(row < TOP_K, (base + rank).astype(I32), 0)


def _dest(pad_start, route_t, tm=2048):
    n = route_t.shape[1]
    grid_spec = pltpu.PrefetchScalarGridSpec(
        num_scalar_prefetch=1,
        grid=(n // tm,),
        in_specs=[pl.BlockSpec((ROUTE_ROWS, tm), lambda i, st: (0, i))],
        out_specs=pl.BlockSpec((SUBLANES, tm), lambda i, st: (0, i)),
    )
    return pl.pallas_call(
        _dest_kernel,
        grid_spec=grid_spec,
        out_shape=jax.ShapeDtypeStruct((SUBLANES, n), I32),
        compiler_params=_cparams(1),
        name="dest",
    )(pad_start, route_t)


def _row_tile(ref, r, rt):
    return ref.at[pl.ds(pl.multiple_of(r * rt, rt), rt)]


def _dispatch_kernel(meta_ref, dest_ref, xn_ref, xs_ref, zero_ref, ring_ref, sem, zsem,
                     *, tm, rt, n_blocks):
    i = pl.program_id(0)
    bm = MOE_BM * rt
    ring = ring_ref.at[i & 1]
    ring[...] = xn_ref[...]

    def per_token(tk, carry):
        src = _row_tile(ring, tk, rt)
        for kk in range(TOP_K):
            r = dest_ref[kk, tk]
            pltpu.make_async_copy(src, _row_tile(xs_ref, r, rt),
                                  sem.at[i & 1]).start(priority=kk % 2)
        return carry

    lax.fori_loop(0, tm, per_token, 0, unroll=4)

    @pl.when(i == 0)
    def _():
        zero_ref[...] = jnp.zeros_like(zero_ref)

        def pad_copy(r):
            return pltpu.make_async_copy(zero_ref.at[pl.ds(0, rt)], _row_tile(xs_ref, r, rt), zsem)

        def per_expert(e, total):
            r0 = meta_ref[e]
            cnt = meta_ref[N_EXPERTS + e]

            def group(j, carry):
                for u in range(PAD_UNROLL):
                    pad_copy(r0 + j * PAD_UNROLL + u).start(priority=u % 2)
                return carry

            def one(r, carry):
                pad_copy(r0 + r).start()
                return carry

            groups = cnt // PAD_UNROLL
            lax.fori_loop(0, groups, group, 0)
            lax.fori_loop(groups * PAD_UNROLL, cnt, one, 0)
            return total + cnt

        total = lax.fori_loop(0, N_EXPERTS, per_expert, 0)

        def drain_block(j, carry):
            pltpu.make_async_copy(zero_ref, xs_ref.at[pl.ds(0, bm)], zsem).wait()
            return carry

        def drain_group(j, carry):
            pltpu.make_async_copy(zero_ref.at[pl.ds(0, PAD_UNROLL * rt)],
                                  xs_ref.at[pl.ds(0, PAD_UNROLL * rt)], zsem).wait()
            return carry

        def drain_row(j, carry):
            pad_copy(0).wait()
            return carry

        lax.fori_loop(0, total // MOE_BM, drain_block, 0)
        rest = total % MOE_BM
        lax.fori_loop(0, rest // PAD_UNROLL, drain_group, 0)
        lax.fori_loop(0, rest % PAD_UNROLL, drain_row, 0)

        def tail_copy(b):
            return pltpu.make_async_copy(
                zero_ref, xs_ref.at[pl.ds(pl.multiple_of(b * bm, bm), bm)], zsem)

        n_used = meta_ref[2 * N_EXPERTS]

        def tail(b, carry):
            tail_copy(b).start()
            return carry

        lax.fori_loop(n_used, n_blocks, tail, 0)

        def tail_drain(b, carry):
            tail_copy(0).wait()
            return carry

        lax.fori_loop(n_used, n_blocks, tail_drain, 0)

    def wait_step(parity):
        for _ in range(TOP_K):
            pltpu.make_async_copy(ring_ref.at[parity], xs_ref.at[pl.ds(0, tm * rt)],
                                  sem.at[parity]).wait()

    @pl.when(i > 0)
    def _():
        wait_step(1 - (i & 1))

    @pl.when(i == pl.num_programs(0) - 1)
    def _():
        wait_step(i & 1)


def _dispatch(meta, dest, xn_t, n_tok, n_rows, tm=256):
    rt = xn_t.shape[0] // n_tok
    n_blocks = n_rows // MOE_BM
    grid_spec = pltpu.PrefetchScalarGridSpec(
        num_scalar_prefetch=1,
        grid=(n_tok // tm,),
        in_specs=[
            pl.BlockSpec((SUBLANES, tm), lambda i, meta: (0, i), memory_space=pltpu.SMEM),
            pl.BlockSpec((tm * rt, LANES), lambda i, meta: (i, 0)),
        ],
        out_specs=pl.BlockSpec(memory_space=pl.ANY),
        scratch_shapes=[
            pltpu.VMEM((MOE_BM * rt, LANES), xn_t.dtype),
            pltpu.VMEM((2, tm * rt, LANES), xn_t.dtype),
            pltpu.SemaphoreType.DMA((2,)),
            pltpu.SemaphoreType.DMA,
        ],
    )
    return pl.pallas_call(
        functools.partial(_dispatch_kernel, tm=tm, rt=rt, n_blocks=n_blocks),
        grid_spec=grid_spec,
        out_shape=jax.ShapeDtypeStruct((n_rows * rt, LANES), xn_t.dtype),
        compiler_params=_cparams(1),
        name="dispatch",
    )(meta, dest, xn_t)


def _experts_kernel(be_ref, nu_ref, nxt_ref, par_ref, nv_ref, x_ref, bg_ref, bu_ref, bd_ref,
                    wg_hbm, wu_hbm, wd_hbm, y_ref, wf_ref, wb_ref, sem):
    i = pl.program_id(0)
    used = i < nu_ref[0]
    e = be_ref[i]
    changed = (i == 0) | (e != be_ref[jnp.maximum(i - 1, 0)])

    def weight_copies(expert, slot):
        return [pltpu.make_async_copy(w.at[0, expert], wf_ref.at[slot, j], sem.at[slot])
                for j, w in enumerate((wg_hbm, wu_hbm, wd_hbm))]

    @pl.when(i == 0)
    def _():
        for cp in weight_copies(e, par_ref[e]):
            cp.start()

    @pl.when(used & changed)
    def _():
        slot = par_ref[e]
        for cp in weight_copies(e, slot):
            cp.wait()
        nxt = nxt_ref[e]

        @pl.when(nxt < N_EXPERTS)
        def _():
            for cp in weight_copies(nxt, 1 - slot):
                cp.start()

        for j in range(3):
            wb_ref[j] = wf_ref[slot, j].astype(BF16)

    rt = x_ref.shape[0] // MOE_BM

    def mlp(rows):
        x = _unpack_pairs_f32(_load_row_tiles(x_ref.at[pl.ds(0, rows * rt)], rows)).astype(BF16)
        gt = jnp.minimum(jnp.dot(x, wb_ref[0], preferred_element_type=F32) + bg_ref[...],
                         SWIGLU_LIMIT)
        up = jnp.clip(jnp.dot(x, wb_ref[1], preferred_element_type=F32) + bu_ref[...],
                      -SWIGLU_LIMIT, SWIGLU_LIMIT)
        hid = (up + 1.0) * (gt * _sigmoid(SWIGLU_ALPHA * gt))
        y = jnp.dot(hid.astype(BF16), wb_ref[2], preferred_element_type=F32) + bd_ref[...]
        _store_row_tiles(y_ref.at[pl.ds(0, rows * rt)], _pack_bf16_pairs(y))

    n_valid = nv_ref[i]

    quarter = MOE_BM // 4
    for q in range(1, 5):
        rows = q * quarter

        @pl.when(used & (n_valid > rows - quarter) & ((n_valid <= rows) | (q == 4)))
        def _(rows=rows):
            mlp(rows)
            if rows < MOE_BM:
                rest = (MOE_BM - rows) * rt
                y_ref[pl.ds(rows * rt, rest), :] = jnp.zeros((rest, LANES), y_ref.dtype)

    @pl.when(jnp.logical_not(used))
    def _():
        y_ref[...] = jnp.zeros_like(y_ref)


def _experts(block_e, n_used, next_e, parity, block_valid, xs_t, w_gate, b_gate, w_up, b_up,
             w_down, b_down):
    d, d_ff = w_gate.shape[2:]
    assert d == d_ff, "weight staging buffers assume square expert matrices"
    rt = d // 2 // LANES
    n_blocks = xs_t.shape[0] // (MOE_BM * rt)
    bspec = lambda m: pl.BlockSpec((None, 1, m), lambda i, be, nu, nx, pa, nv: (be[i], 0, 0))
    hbm = pl.BlockSpec(memory_space=pl.ANY)
    grid_spec = pltpu.PrefetchScalarGridSpec(
        num_scalar_prefetch=5,
        grid=(n_blocks,),
        in_specs=[
            pl.BlockSpec((MOE_BM * rt, LANES), lambda i, be, nu, nx, pa, nv:
                         (jnp.maximum(jnp.minimum(i, nu[0] - 1), 0), 0)),
            bspec(d_ff), bspec(d_ff), bspec(d),
            hbm, hbm, hbm,
        ],
        out_specs=pl.BlockSpec((MOE_BM * rt, LANES), lambda i, be, nu, nx, pa, nv: (i, 0)),
        scratch_shapes=[
            pltpu.VMEM((2, 3, d, d_ff), F32),
            pltpu.VMEM((3, d, d_ff), BF16),
            pltpu.SemaphoreType.DMA((2,)),
        ],
    )
    return pl.pallas_call(
        _experts_kernel,
        grid_spec=grid_spec,
        out_shape=jax.ShapeDtypeStruct(xs_t.shape, U32),
        compiler_params=_cparams(1),
        name="experts",
    )(block_e, n_used, next_e, parity, block_valid, xs_t, b_gate, b_up, b_down,
      w_gate, w_up, w_down)


def _combine_kernel(dcur_ref, dnext_ref, h2_ref, route_ref, gain_ref, ys_ref, o_ref,
                    buf_ref, sem, *, tm, rt):
    i = pl.program_id(0)
    nsteps = pl.num_programs(0)
    slot = i & 1

    def issue_all(dref, s):
        def per_token(tk, carry):
            for kk in range(TOP_K):
                r = dref[kk, tk]
                pltpu.make_async_copy(_row_tile(ys_ref, r, rt),
                                      _row_tile(buf_ref.at[s, kk], tk, rt),
                                      sem.at[s]).start(priority=kk % 2)
            return carry
        lax.fori_loop(0, tm, per_token, 0, unroll=4)

    @pl.when(i == 0)
    def _():
        issue_all(dcur_ref, 0)

    @pl.when(i + 1 < nsteps)
    def _():
        issue_all(dnext_ref, 1 - slot)

    for kk in range(TOP_K):
        pltpu.make_async_copy(ys_ref.at[pl.ds(0, tm * rt)], buf_ref.at[slot, kk],
                              sem.at[slot]).wait()

    rec = route_ref[...]
    acc = h2_ref[...]
    for kk in range(TOP_K):
        acc = acc + (rec[:, ROUTE_GATE + kk:ROUTE_GATE + kk + 1]
                     * _unpack_pairs_f32(_load_row_tiles(buf_ref.at[slot, kk], tm)))
    o_ref[...] = _rmsnorm(acc, gain_ref[...])


def _combine(dest, h2, route, gain, ys_t, tm=256):
    n, d = h2.shape
    nsteps = n // tm
    rt = d // 2 // LANES
    return pl.pallas_call(
        functools.partial(_combine_kernel, tm=tm, rt=rt),
        grid=(nsteps,),
        in_specs=[
            pl.BlockSpec((SUBLANES, tm), lambda i: (0, i), memory_space=pltpu.SMEM),
            pl.BlockSpec((SUBLANES, tm), lambda i: (0, jnp.minimum(i + 1, nsteps - 1)),
                         memory_space=pltpu.SMEM),
            pl.BlockSpec((tm, d), lambda i: (i, 0)),
            pl.BlockSpec((tm, LANES), lambda i: (i, 0)),
            pl.BlockSpec((1, d), lambda i: (0, 0)),
            pl.BlockSpec(memory_space=pl.ANY),
        ],
        out_specs=pl.BlockSpec((tm, d), lambda i: (i, 0)),
        out_shape=jax.ShapeDtypeStruct((n, d), F32),
        scratch_shapes=[
            pltpu.VMEM((2, TOP_K, tm * rt, LANES), U32),
            pltpu.SemaphoreType.DMA((2,)),
        ],
        compiler_params=_cparams(1),
        name="combine",
    )(dest, dest, h2, route, gain, ys_t)


def kernel(x, mem, norm_mix, w_in, dn_conv, dn_a_log, dn_dt_bias, dn_norm, rg_conv, rg_conv_b, rg_w_a, rg_b_a, rg_w_x, rg_b_x, rg_lambda, w_out, norm_cross, norm_mem, w_cq, w_ckv, w_co, norm_moe, w_router, b_router, w_gate, b_gate, w_up, b_up, w_down, b_down, norm_final):
    batch, seq, d = x.shape
    mem_len = mem.shape[1]
    n = batch * seq
    assert w_in.shape[0] == 1, "single-layer trunk"
    x2 = x.reshape(n, d)

    wi = w_in[0]
    n_gate = 2 * DN_HEADS
    w_cat = jnp.concatenate(
        [wi[:, :4 * D_DN], wi[:, 4 * D_DN + n_gate:],
         jnp.pad(wi[:, 4 * D_DN:4 * D_DN + n_gate], ((0, 0), (0, LANES - n_gate)))],
        axis=1).astype(BF16)
    prow = (jnp.zeros((SUBLANES, LANES), F32)
            .at[0, DN_HEADS:n_gate].set(dn_a_log[0]).at[1, DN_HEADS:n_gate].set(dn_dt_bias[0]))
    pcol = (jnp.zeros((n_gate, LANES), F32)
            .at[DN_HEADS:, 0].set(dn_a_log[0]).at[DN_HEADS:, 1].set(dn_dt_bias[0]))

    proj, gbt = _in_proj(x2, norm_mix, w_cat, prow, pcol)
    dn = _deltanet(proj, gbt, dn_conv[0], dn_norm, batch, seq)
    rg = _rglru(proj, rg_conv[0], rg_conv_b, rg_w_a[0], rg_b_a[0].reshape(1, D_RG),
                rg_w_x[0], rg_b_x[0].reshape(1, D_RG), rg_lambda, batch, seq)
    kv = _mem_kv(mem.reshape(batch * mem_len, d), norm_mem, w_ckv[0])

    w_r = jnp.pad(w_router[0], ((0, 0), (0, LANES - N_EXPERTS)))
    w_r_hi = w_r.astype(BF16)
    w_r_lo = (w_r - w_r_hi.astype(F32)).astype(BF16)
    b_r = jnp.pad(b_router, ((0, 0), (0, LANES - N_EXPERTS)))
    h2, xn, route, route_t, counts = _post_mix(
        dn, rg, x2, w_out[0], norm_cross, w_cq[0], kv, w_co[0], norm_moe, w_r_hi, w_r_lo, b_r,
        seq, mem_len)

    n_blocks = n * TOP_K // MOE_BM + N_EXPERTS
    n_rows = n_blocks * MOE_BM
    cnt = counts[:, 0].astype(I32)
    padded = (cnt + MOE_BM - 1) // MOE_BM * MOE_BM
    pad_end = jnp.cumsum(padded)
    pad_start = pad_end - padded
    n_used = (pad_end[-1:] // MOE_BM).astype(I32)
    block_e = jnp.minimum(
        jnp.sum(pad_end[None, :] <= (jnp.arange(n_blocks, dtype=I32) * MOE_BM)[:, None], axis=1),
        N_EXPERTS - 1).astype(I32)
    meta = jnp.concatenate([pad_start + cnt, padded - cnt, n_used]).astype(I32)

    dest = _dest(pad_start.astype(I32), route_t)
    xs = _dispatch(meta, dest, xn, n, n_rows)
    has = cnt > 0
    eid = jnp.where(has, jnp.arange(N_EXPERTS, dtype=I32), N_EXPERTS)
    after = lax.cummin(eid, axis=0, reverse=True)
    next_e = jnp.concatenate([after[1:], jnp.full((1,), N_EXPERTS, I32)]).astype(I32)
    parity = ((jnp.cumsum(has.astype(I32)) - 1) & 1).astype(I32)
    pos = (jnp.arange(n_blocks, dtype=I32) * MOE_BM)[:, None]
    own = (pad_start[None, :] <= pos) & (pos < pad_end[None, :])
    left = jnp.sum(jnp.where(own, pad_start[None, :] + cnt[None, :], 0), axis=1) - pos[:, 0]
    block_valid = jnp.clip(left, 0, MOE_BM).astype(I32)
    ys = _experts(block_e, n_used, next_e, parity, block_valid, xs, w_gate, b_gate[0][:, None, :],
                  w_up, b_up[0][:, None, :], w_down, b_down[0][:, None, :])
    out = _combine(dest, h2, route, norm_final.reshape(1, d), ys)
    return out.reshape(batch, seq, d)
```

```python
import functools

import jax
import jax.numpy as jnp
from jax import lax
from jax.experimental import pallas as pl
from jax.experimental.pallas import tpu as pltpu

F32 = jnp.float32
BF16 = jnp.bfloat16
I32 = jnp.int32
U32 = jnp.uint32

EPS = 1e-6
LANES = 128
SUBLANES = 8
VMEM_LIMIT = 48 * 1024 * 1024

DN_HEADS = 4
DN_HEAD_DIM = 128
D_DN = DN_HEADS * DN_HEAD_DIM
D_RG = 512
RG_BLOCKS = 4
RG_C = 8.0
CONV_WIDTH = 4
XA_HEADS = 4
N_EXPERTS = 32
TOP_K = 4
SWIGLU_LIMIT = 7.0
SWIGLU_ALPHA = 1.702

DN_CHUNK = 128
INV_BASE = 16
MOE_BM = 512
MOE_PARTS = 8
PAD_UNROLL = 8

COL_RX = 4 * D_DN
COL_RY = COL_RX + D_RG
COL_GB = COL_RY + D_RG
PROJ_W = COL_GB + LANES


def _cparams(n_axes=1):
    return pltpu.CompilerParams(
        dimension_semantics=("arbitrary",) * n_axes, vmem_limit_bytes=VMEM_LIMIT)


def _mm(a, b):
    return jnp.dot(a.astype(BF16), b.astype(BF16), preferred_element_type=F32)


def _mm_nt(a, b):
    return lax.dot_general(a.astype(BF16), b.astype(BF16), (((1,), (1,)), ((), ())),
                           preferred_element_type=F32)


def _rmsnorm(x, g):
    return x * lax.rsqrt(jnp.mean(x * x, axis=-1, keepdims=True) + EPS) * g


def _sigmoid(x):
    return 0.5 * jnp.tanh(0.5 * x) + 0.5


def _mm_split(a, b_hi, b_lo):
    a_hi = a.astype(BF16)
    a_lo = (a - a_hi.astype(F32)).astype(BF16)
    return (jnp.dot(a_hi, b_hi, preferred_element_type=F32)
            + jnp.dot(a_hi, b_lo, preferred_element_type=F32)
            + jnp.dot(a_lo, b_hi, preferred_element_type=F32))


def _softplus(x):
    return jnp.maximum(x, 0.0) + jnp.log1p(jnp.exp(-jnp.abs(x)))


def _load_row_tiles(ref, rows):
    rt = ref.shape[0] // rows
    return jnp.concatenate([ref[pl.ds(s, rows, stride=rt), :] for s in range(rt)], axis=1)


def _store_row_tiles(ref, val):
    rows, w = val.shape
    rt = w // LANES
    for s in range(rt):
        ref[pl.ds(s, rows, stride=rt), :] = val[:, s * LANES:(s + 1) * LANES]


def _pack_bf16_pairs(x):
    half = x.shape[1] // 2
    hi = pltpu.bitcast(x[:, :half].astype(BF16).astype(F32), U32)
    lo = pltpu.bitcast(x[:, half:].astype(BF16).astype(F32), U32)
    return hi | (lo >> 16)


def _unpack_pairs_f32(p):
    hi = pltpu.bitcast(p & jnp.uint32(0xFFFF0000), F32)
    lo = pltpu.bitcast(p << 16, F32)
    return jnp.concatenate([hi, lo], axis=1)


def _load_segmented(ref):
    seg = ref.shape[0] // SUBLANES
    return jnp.concatenate([ref[pl.ds(j, SUBLANES, stride=seg), :] for j in range(seg)], axis=0)


def _causal_conv_segmented(xp, halo_ref, cs, w):
    tb = xp.shape[0]
    ng = w.shape[0] - 1
    prev = halo_ref[:, cs]
    last = xp[tb - ng * SUBLANES:, :]
    halo_ref[:, cs] = last
    sub = lax.broadcasted_iota(I32, (SUBLANES, xp.shape[1]), 0)
    groups = []
    for g in range(ng):
        rows = slice(g * SUBLANES, (g + 1) * SUBLANES)
        groups.append(jnp.where(sub == 0, pltpu.roll(prev[rows], 1, 0),
                                pltpu.roll(last[rows], 1, 0)))
    ext = jnp.concatenate(groups + [xp], axis=0)
    y = w[ng:ng + 1] * xp
    for k in range(1, ng + 1):
        y = y + w[ng - k:ng - k + 1] * ext[(ng - k) * SUBLANES:(ng - k) * SUBLANES + tb]
    return y


def _in_proj_kernel(x_ref, g_ref, w_ref, prow_ref, pcol_ref, proj_ref, gbt_ref):
    u = _rmsnorm(x_ref[...], g_ref[...]).astype(BF16)
    for c0 in range(0, COL_GB, 512):
        proj_ref[:, c0:c0 + 512] = jnp.dot(u, w_ref[:, c0:c0 + 512], preferred_element_type=F32)
    ba = jnp.dot(u, w_ref[:, COL_GB:PROJ_W], preferred_element_type=F32)
    lane = lax.broadcasted_iota(I32, ba.shape, 1)
    g = -jnp.exp(prow_ref[0:1, :]) * _softplus(ba + prow_ref[1:2, :])
    proj_ref[:, COL_GB:PROJ_W] = jnp.where(lane < DN_HEADS, _sigmoid(ba), g)
    bat = lax.dot_general(w_ref[:, COL_GB:PROJ_W], u, (((0,), (1,)), ((), ())),
                          preferred_element_type=F32)[0:2 * DN_HEADS, :]
    row = lax.broadcasted_iota(I32, bat.shape, 0)
    gt = -jnp.exp(pcol_ref[:, 0:1]) * _softplus(bat + pcol_ref[:, 1:2])
    gbt_ref[...] = jnp.where(row < DN_HEADS, _sigmoid(bat), gt)


def _in_proj(x2, gain, w_cat, prow, pcol, tm=512):
    n, d = x2.shape
    return pl.pallas_call(
        _in_proj_kernel,
        grid=(n // tm,),
        in_specs=[
            pl.BlockSpec((tm, d), lambda i: (i, 0)),
            pl.BlockSpec((1, d), lambda i: (0, 0)),
            pl.BlockSpec((d, PROJ_W), lambda i: (0, 0)),
            pl.BlockSpec((SUBLANES, LANES), lambda i: (0, 0)),
            pl.BlockSpec((2 * DN_HEADS, LANES), lambda i: (0, 0)),
        ],
        out_specs=[
            pl.BlockSpec((tm, PROJ_W), lambda i: (i, 0)),
            pl.BlockSpec((2 * DN_HEADS, tm), lambda i: (0, i)),
        ],
        out_shape=[
            jax.ShapeDtypeStruct((n, PROJ_W), F32),
            jax.ShapeDtypeStruct((2 * DN_HEADS, n), F32),
        ],
        compiler_params=_cparams(1),
        name="in_proj",
    )(x2, gain, w_cat, prow, pcol)


def _deltanet_kernel(*refs, tb):
    n_grp = 3 * DN_HEADS
    qkv_refs = refs[:n_grp]
    (z_ref, gb_ref, gbt_ref, conv_ref, norm_ref, o_ref,
     s_ref, halo_ref, act_ref, gcc_ref, gcr_ref,
     a_ref, p_ref, d_ref, qk_ref, rhs_ref, u_ref, wq_ref, kdt_ref) = refs[n_grp:]
    t = pl.program_id(1)
    c = DN_CHUNK
    dh = DN_HEAD_DIM

    @pl.when(t == 0)
    def _():
        s_ref[...] = jnp.zeros_like(s_ref)
        halo_ref[...] = jnp.zeros_like(halo_ref)

    sub = lax.broadcasted_iota(I32, (SUBLANES, dh), 0)
    for grp in range(n_grp):
        cs = slice(grp * dh, (grp + 1) * dh)
        x = qkv_refs[grp][...]
        prev = halo_ref[:, cs]
        halo_ref[:, cs] = x[tb - SUBLANES:, :]
        y = conv_ref[CONV_WIDTH - 1:CONV_WIDTH, cs] * x
        for k in range(1, CONV_WIDTH):
            xs = pltpu.roll(x, k, 0)
            head = jnp.where(sub < k, pltpu.roll(prev, k, 0), xs[:SUBLANES])
            xs = jnp.concatenate([head, xs[SUBLANES:]], axis=0)
            y = y + conv_ref[CONV_WIDTH - 1 - k:CONV_WIDTH - k, cs] * xs
        y = y * _sigmoid(y)
        if grp < 2 * DN_HEADS:
            y = y * lax.rsqrt(jnp.sum(y * y, axis=-1, keepdims=True) + EPS)
        if grp < DN_HEADS:
            y = y * (dh ** -0.5)
        act_ref[grp] = y

    gcol = gb_ref[...]
    rpos = lax.broadcasted_iota(I32, gcol.shape, 0) & (c - 1)
    d = 1
    while d < c:
        gcol = gcol + jnp.where(rpos >= d, pltpu.roll(gcol, d, 0), 0.0)
        d *= 2
    gcc_ref[...] = gcol
    grow = gbt_ref[...]
    lpos = lax.broadcasted_iota(I32, grow.shape, 1) & (c - 1)
    d = 1
    while d < c:
        grow = grow + jnp.where(lpos >= d, pltpu.roll(grow, d, 1), 0.0)
        d *= 2
    gcr_ref[...] = grow

    row = lax.broadcasted_iota(I32, (c, c), 0)
    col = lax.broadcasted_iota(I32, (c, c), 1)
    causal = row >= col
    strict = row > col
    eye = jnp.where(row == col, 1.0, 0.0)
    gain = norm_ref[...]
    n_chunks = tb // c
    probs = [(ci, h) for ci in range(n_chunks) for h in range(DN_HEADS)]

    decay_last = []
    for p, (ci, h) in enumerate(probs):
        rows = slice(ci * c, (ci + 1) * c)
        q = act_ref[h, rows, :]
        k = act_ref[DN_HEADS + h, rows, :]
        v = act_ref[2 * DN_HEADS + h, rows, :]
        beta = gb_ref[rows, h:h + 1]
        gc = gcc_ref[rows, DN_HEADS + h:DN_HEADS + h + 1]
        gr = gcr_ref[DN_HEADS + h:DN_HEADS + h + 1, rows]
        g_last = gc[c - 1:c, :]
        decay = jnp.where(causal, jnp.exp(jnp.where(causal, gc - gr, 0.0)), 0.0)
        kb = k * beta
        both = _mm_nt(jnp.concatenate([kb, q], axis=0), k)
        a_ref[p] = jnp.where(strict, both[:c] * decay, 0.0)
        qk_ref[p] = (both[c:] * decay).astype(BF16)
        egc = jnp.exp(gc)
        rhs_ref[p] = jnp.concatenate([v * beta, kb * egc], axis=1).astype(BF16)
        wq_ref[p, c:, :] = (q * egc).astype(BF16)
        kdt_ref[p] = (k * jnp.exp(g_last - gc)).T.astype(BF16)
        decay_last.append(jnp.exp(g_last))

    shift = INV_BASE.bit_length() - 1
    blk = (row >> shift) == (col >> shift)
    for p in range(len(probs)):
        diag = jnp.where(blk, a_ref[p], 0.0)
        p_ref[p] = eye - diag
        d_ref[p] = _mm(diag, diag).astype(BF16)
    for it in range(shift - 1):
        for p in range(len(probs)):
            pw = d_ref[p]
            inv = p_ref[p]
            p_ref[p] = inv + _mm(inv, pw)
            if it < shift - 2:
                d_ref[p] = _mm(pw, pw).astype(BF16)
    s = INV_BASE
    while s < c:
        sh = s.bit_length() - 1
        off = ((row >> (sh + 1)) == (col >> (sh + 1))) & ((row >> sh) != (col >> sh))
        for p in range(len(probs)):
            d_ref[p] = _mm(p_ref[p], jnp.where(off, a_ref[p], 0.0)).astype(BF16)
        for p in range(len(probs)):
            inv = p_ref[p]
            p_ref[p] = inv - _mm(d_ref[p], inv)
        s *= 2
    for p in range(len(probs)):
        uw = _mm(p_ref[p], rhs_ref[p])
        u_ref[p] = uw[:, :dh]
        wq_ref[p, :c, :] = uw[:, dh:].astype(BF16)

    for ci in range(n_chunks):
        rows = slice(ci * c, (ci + 1) * c)
        ps = [ci * DN_HEADS + h for h in range(DN_HEADS)]
        s_old = [s_ref[h] for h in range(DN_HEADS)]
        ws = [_mm(wq_ref[p], s_old[h]) for h, p in enumerate(ps)]
        v_new = [u_ref[p] - ws[h][:c] for h, p in enumerate(ps)]
        outs = [ws[h][c:] + _mm(qk_ref[p], v_new[h]) for h, p in enumerate(ps)]
        for h, p in enumerate(ps):
            s_ref[h] = s_old[h] * decay_last[p] + _mm(kdt_ref[p], v_new[h])
        for h in range(DN_HEADS):
            o = _rmsnorm(outs[h], gain)
            zz = z_ref[rows, h * dh:(h + 1) * dh]
            o_ref[rows, h * dh:(h + 1) * dh] = (o * (zz * _sigmoid(zz))).astype(o_ref.dtype)


def _deltanet(proj, gbt, dn_conv, dn_norm, batch, seq, tb=512):
    n = proj.shape[0]
    nt = seq // tb
    c = DN_CHUNK
    n_prob = (tb // c) * DN_HEADS
    n_grp = 3 * DN_HEADS
    grp = lambda j: pl.BlockSpec((tb, DN_HEAD_DIM), lambda b, t, j=j: (b * nt + t, j))
    return pl.pallas_call(
        functools.partial(_deltanet_kernel, tb=tb),
        grid=(batch, nt),
        in_specs=[grp(j) for j in range(n_grp)] + [
            pl.BlockSpec((tb, D_DN), lambda b, t: (b * nt + t, 3)),
            pl.BlockSpec((tb, LANES), lambda b, t: (b * nt + t, COL_GB // LANES)),
            pl.BlockSpec((2 * DN_HEADS, tb), lambda b, t: (0, b * nt + t)),
            pl.BlockSpec((CONV_WIDTH, 3 * D_DN), lambda b, t: (0, 0)),
            pl.BlockSpec((1, DN_HEAD_DIM), lambda b, t: (0, 0)),
        ],
        out_specs=pl.BlockSpec((tb, D_DN), lambda b, t: (b * nt + t, 0)),
        out_shape=jax.ShapeDtypeStruct((n, D_DN), BF16),
        scratch_shapes=[
            pltpu.VMEM((DN_HEADS, DN_HEAD_DIM, DN_HEAD_DIM), F32),
            pltpu.VMEM((SUBLANES, 3 * D_DN), F32),
            pltpu.VMEM((n_grp, tb, DN_HEAD_DIM), F32),
            pltpu.VMEM((tb, LANES), F32),
            pltpu.VMEM((2 * DN_HEADS, tb), F32),
            pltpu.VMEM((n_prob, c, c), F32),
            pltpu.VMEM((n_prob, c, c), F32),
            pltpu.VMEM((n_prob, c, c), BF16),
            pltpu.VMEM((n_prob, c, c), BF16),
            pltpu.VMEM((n_prob, c, 2 * DN_HEAD_DIM), BF16),
            pltpu.VMEM((n_prob, c, DN_HEAD_DIM), F32),
            pltpu.VMEM((n_prob, 2 * c, DN_HEAD_DIM), BF16),
            pltpu.VMEM((n_prob, DN_HEAD_DIM, c), BF16),
        ],
        compiler_params=_cparams(2),
        name="deltanet",
    )(*([proj] * (n_grp + 2)), gbt, dn_conv, dn_norm)


def _gelu_tanh(x):
    return 0.5 * x * (1.0 + jnp.tanh(0.7978845608028654 * (x + 0.044715 * (x * x * x))))


def _rglru_kernel(*refs, tb):
    rx_refs = refs[:RG_BLOCKS]
    (ry_ref, conv_ref, convb_ref, wa_ref, ba_ref, wx_ref, bx_ref, lam_ref,
     o_ref, hc_ref, halo_ref, a_ref, b_ref, h_ref) = refs[RG_BLOCKS:]
    t = pl.program_id(1)
    bw = D_RG // RG_BLOCKS
    seg = tb // SUBLANES

    @pl.when(t == 0)
    def _():
        hc_ref[...] = jnp.zeros_like(hc_ref)
        halo_ref[...] = jnp.zeros_like(halo_ref)

    log_sig = -_softplus(-lam_ref[...])
    rowi = lax.broadcasted_iota(I32, (tb, bw), 0)
    seq_start_row = jnp.where(t == 0, 0, -1)
    for nb in range(RG_BLOCKS):
        cs = slice(nb * bw, (nb + 1) * bw)
        xb = (_causal_conv_segmented(_load_segmented(rx_refs[nb]), halo_ref, cs,
                                     conv_ref[:, cs]) + convb_ref[:, cs])
        r = _sigmoid(_mm(xb, wa_ref[nb]) + ba_ref[:, cs])
        gi = _sigmoid(_mm(xb, wx_ref[nb]) + bx_ref[:, cs])
        log_a = RG_C * r * log_sig[:, cs]
        a = jnp.exp(log_a)
        y = jnp.tanh(-log_a) * (1.0 + a * a)
        mult = jnp.where(y > 0.0, y * lax.rsqrt(y), 0.0)
        mult = jnp.where(rowi == seq_start_row, 1.0, mult)
        a_ref[:, cs] = a
        b_ref[:, cs] = mult * (gi * xb)

    def scan(j, carry):
        ac, bc = carry
        rows = pl.ds(pl.multiple_of(j * SUBLANES, SUBLANES), SUBLANES)
        aj = a_ref[rows, :]
        bc = aj * bc + b_ref[rows, :]
        ac = aj * ac
        a_ref[rows, :] = ac
        b_ref[rows, :] = bc
        return ac, bc

    ac, bc = lax.fori_loop(1, seg, scan, (a_ref[0:SUBLANES, :], b_ref[0:SUBLANES, :]), unroll=7)

    h = hc_ref[...]
    h_in = []
    for s in range(SUBLANES):
        h_in.append(h)
        h = ac[s:s + 1, :] * h + bc[s:s + 1, :]
    hc_ref[...] = h
    h_in = jnp.concatenate(h_in, axis=0)

    for j in range(seg):
        rows = slice(j * SUBLANES, (j + 1) * SUBLANES)
        hj = a_ref[rows, :] * h_in + b_ref[rows, :]
        for nb in range(RG_BLOCKS):
            h_ref[nb, rows, :] = hj[:, nb * bw:(nb + 1) * bw]
    g = seg // SUBLANES
    for nb in range(RG_BLOCKS):
        cs = slice(nb * bw, (nb + 1) * bw)
        h = jnp.concatenate(
            [h_ref[nb, pl.ds((i % g) * SUBLANES * SUBLANES + i // g, SUBLANES, stride=SUBLANES), :]
             for i in range(seg)], axis=0)
        o_ref[:, cs] = (h * _gelu_tanh(ry_ref[:, cs])).astype(o_ref.dtype)


def _rglru(proj, rg_conv, rg_conv_b, w_a, b_a, w_x, b_x, lam, batch, seq, tb=512):
    n = proj.shape[0]
    nt = seq // tb
    bw = D_RG // RG_BLOCKS
    full = lambda shape: pl.BlockSpec(shape, lambda b, t: (0,) * len(shape))
    return pl.pallas_call(
        functools.partial(_rglru_kernel, tb=tb),
        grid=(batch, nt),
        in_specs=[pl.BlockSpec((tb, bw), lambda b, t, j=j: (b * nt + t, COL_RX // bw + j))
                  for j in range(RG_BLOCKS)] + [
            pl.BlockSpec((tb, D_RG), lambda b, t: (b * nt + t, COL_RY // D_RG)),
            full((CONV_WIDTH, D_RG)), full((1, D_RG)),
            full((RG_BLOCKS, bw, bw)), full((1, D_RG)),
            full((RG_BLOCKS, bw, bw)), full((1, D_RG)),
            full((1, D_RG)),
        ],
        out_specs=pl.BlockSpec((tb, D_RG), lambda b, t: (b * nt + t, 0)),
        out_shape=jax.ShapeDtypeStruct((n, D_RG), BF16),
        scratch_shapes=[
            pltpu.VMEM((1, D_RG), F32),
            pltpu.VMEM(((CONV_WIDTH - 1) * SUBLANES, D_RG), F32),
            pltpu.VMEM((tb, D_RG), F32),
            pltpu.VMEM((tb, D_RG), F32),
            pltpu.VMEM((RG_BLOCKS, tb, bw), F32),
        ],
        compiler_params=_cparams(2),
        name="rglru",
    )(*([proj] * (RG_BLOCKS + 1)), rg_conv, rg_conv_b, w_a, b_a, w_x, b_x, lam)


def _mem_kv_kernel(m_ref, g_ref, w_ref, o_ref):
    mn = _rmsnorm(m_ref[...], g_ref[...]).astype(BF16)
    for c0 in range(0, o_ref.shape[1], 512):
        o_ref[:, c0:c0 + 512] = jnp.dot(mn, w_ref[:, c0:c0 + 512].astype(BF16),
                                        preferred_element_type=F32).astype(o_ref.dtype)


def _mem_kv(mem2, gain, w_ckv):
    n, d = mem2.shape
    return pl.pallas_call(
        _mem_kv_kernel,
        grid=(1,),
        in_specs=[
            pl.BlockSpec((n, d), lambda i: (0, 0)),
            pl.BlockSpec((1, d), lambda i: (0, 0)),
            pl.BlockSpec((d, 2 * d), lambda i: (0, 0)),
        ],
        out_specs=pl.BlockSpec((n, 2 * d), lambda i: (0, 0)),
        out_shape=jax.ShapeDtypeStruct((n, 2 * d), BF16),
        compiler_params=_cparams(1),
        name="mem_kv",
    )(mem2, gain, w_ckv)


ROUTE_E = 0
ROUTE_RANK = TOP_K
ROUTE_GATE = 2 * TOP_K
ROUTE_ROWS = 16


def _post_mix_kernel(dn_ref, rg_ref, x_ref, wo_f32, gx_ref, wq_f32, kv_ref, wco_f32, gm_ref,
                     wrh_ref, wrl_ref, br_ref, h2_ref, xn_ref, route_ref, routet_ref, cnt_ref,
                     carry_ref, wo_ref, wq_ref, wco_ref):
    i = pl.program_id(0)
    tm, d = x_ref.shape
    hd = d // XA_HEADS

    @pl.when(i == 0)
    def _():
        carry_ref[...] = jnp.zeros_like(carry_ref)
        wo_ref[...] = wo_f32[...].astype(BF16)
        wq_ref[...] = wq_f32[...].astype(BF16)
        wco_ref[...] = wco_f32[...].astype(BF16)

    h1 = (x_ref[...] + jnp.dot(dn_ref[...], wo_ref[0:D_DN, :], preferred_element_type=F32)
          + jnp.dot(rg_ref[...], wo_ref[D_DN:, :], preferred_element_type=F32))

    hn = _rmsnorm(h1, gx_ref[...]).astype(BF16)
    q = jnp.dot(hn, wq_ref[...], preferred_element_type=F32)
    heads = []
    for hh in range(XA_HEADS):
        cs = slice(hh * hd, (hh + 1) * hd)
        s = _mm_nt(q[:, cs], kv_ref[:, cs]) * (hd ** -0.5)
        p = jnp.exp(s - jnp.max(s, axis=-1, keepdims=True))
        p = p / jnp.sum(p, axis=-1, keepdims=True)
        heads.append(_mm(p, kv_ref[:, d + hh * hd:d + (hh + 1) * hd]).astype(BF16))
    o = jnp.concatenate(heads, axis=1)
    h2 = h1 + jnp.dot(o, wco_ref[...], preferred_element_type=F32)
    h2_ref[...] = h2

    xn = _rmsnorm(h2, gm_ref[...])
    _store_row_tiles(xn_ref, _pack_bf16_pairs(xn))
    logits = _mm_split(xn, wrh_ref[...], wrl_ref[...]) + br_ref[...]
    lg = logits.T[0:N_EXPERTS, :]
    eidx = lax.broadcasted_iota(I32, lg.shape, 0).astype(F32)
    neg = jnp.float32(-jnp.inf)
    vals, idxs, hots = [], [], []
    for _ in range(TOP_K):
        m = jnp.max(lg, axis=0, keepdims=True)
        idx = jnp.min(jnp.where(lg == m, eidx, float(N_EXPERTS)), axis=0, keepdims=True)
        hot = eidx == idx
        lg = jnp.where(hot, neg, lg)
        vals.append(m)
        idxs.append(idx)
        hots.append(hot)
    es = [jnp.exp(v - vals[0]) for v in vals]
    den = es[0] + es[1] + es[2] + es[3]
    gates = [e / den for e in es]

    chosen = jnp.zeros(lg.shape, F32)
    for hot in hots:
        chosen = chosen + jnp.where(hot, 1.0, 0.0)
    r2 = lax.broadcasted_iota(I32, (tm, tm), 0)
    c2 = lax.broadcasted_iota(I32, (tm, tm), 1)
    before = _mm(chosen, jnp.where(r2 < c2, 1.0, 0.0)) + carry_ref[:, 0:1]
    ranks = [jnp.sum(jnp.where(hot, before, 0.0), axis=0, keepdims=True) for hot in hots]
    carry_ref[...] = carry_ref[...] + jnp.sum(chosen, axis=1, keepdims=True)
    cnt_ref[...] = carry_ref[...]

    row = lax.broadcasted_iota(I32, (ROUTE_ROWS, tm), 0)
    rect = jnp.zeros((ROUTE_ROWS, tm), F32)
    for kk in range(TOP_K):
        rect = jnp.where(row == ROUTE_E + kk, idxs[kk], rect)
        rect = jnp.where(row == ROUTE_RANK + kk, ranks[kk], rect)
        rect = jnp.where(row == ROUTE_GATE + kk, gates[kk], rect)
    routet_ref[...] = rect
    route_ref[...] = jnp.concatenate(
        [rect, jnp.zeros((LANES - ROUTE_ROWS, tm), F32)], axis=0).T


def _post_mix(dn, rg, x2, w_out, g_cross, w_cq, kv, w_co, g_moe, w_r_hi, w_r_lo, b_r, seq, mem_len,
              tm=512):
    n, d = x2.shape
    per_b = seq // tm
    rt_x = d // 2 // LANES
    full = lambda shape: pl.BlockSpec(shape, lambda i: (0,) * len(shape))
    once = lambda shape: pl.BlockSpec(shape, lambda i: (0,) * len(shape),
                                      pipeline_mode=pl.Buffered(1))
    return pl.pallas_call(
        _post_mix_kernel,
        grid=(n // tm,),
        in_specs=[
            pl.BlockSpec((tm, D_DN), lambda i: (i, 0)),
            pl.BlockSpec((tm, D_RG), lambda i: (i, 0)),
            pl.BlockSpec((tm, d), lambda i: (i, 0)),
            once((d, d)), full((1, d)), once((d, d)),
            pl.BlockSpec((mem_len, 2 * d), lambda i: (i // per_b, 0)),
            once((d, d)), full((1, d)),
            full((d, LANES)), full((d, LANES)), full((1, LANES)),
        ],
        out_specs=[
            pl.BlockSpec((tm, d), lambda i: (i, 0)),
            pl.BlockSpec((tm * rt_x, LANES), lambda i: (i, 0)),
            pl.BlockSpec((tm, LANES), lambda i: (i, 0)),
            pl.BlockSpec((ROUTE_ROWS, tm), lambda i: (0, i)),
            pl.BlockSpec((N_EXPERTS, LANES), lambda i: (0, 0)),
        ],
        out_shape=[
            jax.ShapeDtypeStruct((n, d), F32),
            jax.ShapeDtypeStruct((n * rt_x, LANES), U32),
            jax.ShapeDtypeStruct((n, LANES), F32),
            jax.ShapeDtypeStruct((ROUTE_ROWS, n), F32),
            jax.ShapeDtypeStruct((N_EXPERTS, LANES), F32),
        ],
        scratch_shapes=[pltpu.VMEM((N_EXPERTS, LANES), F32),
                        pltpu.VMEM((d, d), BF16), pltpu.VMEM((d, d), BF16), pltpu.VMEM((d, d), BF16)],
        compiler_params=_cparams(1),
        name="post_mix",
    )(dn, rg, x2, w_out, g_cross, w_cq, kv, w_co, g_moe, w_r_hi, w_r_lo, b_r)


def _dest_kernel(start_ref, routet_ref, dest_ref):
    e = routet_ref[ROUTE_E:ROUTE_E + SUBLANES, :]
    rank = routet_ref[ROUTE_RANK:ROUTE_RANK + SUBLANES, :]
    base = jnp.zeros(e.shape, F32)
    for j in range(N_EXPERTS):
        base = jnp.where(e == float(j), start_ref[j].astype(F32), base)
    row = lax.broadcasted_iota(I32, e.shape, 0)
    dest_ref[...] = jnp.where(row < TOP_K, (base + rank).astype(I32), 0)


def _dest(pad_start, route_t, tm=2048):
    n = route_t.shape[1]
    grid_spec = pltpu.PrefetchScalarGridSpec(
        num_scalar_prefetch=1,
        grid=(n // tm,),
        in_specs=[pl.BlockSpec((ROUTE_ROWS, tm), lambda i, st: (0, i))],
        out_specs=pl.BlockSpec((SUBLANES, tm), lambda i, st: (0, i)),
    )
    return pl.pallas_call(
        _dest_kernel,
        grid_spec=grid_spec,
        out_shape=jax.ShapeDtypeStruct((SUBLANES, n), I32),
        compiler_params=_cparams(1),
        name="dest",
    )(pad_start, route_t)


def _row_tile(ref, r, rt):
    return ref.at[pl.ds(pl.multiple_of(r * rt, rt), rt)]


def _dispatch_kernel(meta_ref, dest_ref, xn_ref, xs_ref, zero_ref, ring_ref, sem, zsem,
                     *, tm, rt, n_blocks):
    i = pl.program_id(0)
    bm = MOE_BM * rt
    ring = ring_ref.at[i & 1]
    ring[...] = xn_ref[...]

    def per_token(tk, carry):
        src = _row_tile(ring, tk, rt)
        for kk in range(TOP_K):
            r = dest_ref[kk, tk]
            pltpu.make_async_copy(src, _row_tile(xs_ref, r, rt),
                                  sem.at[i & 1]).start(priority=kk % 2)
        return carry

    lax.fori_loop(0, tm, per_token, 0, unroll=4)

    @pl.when(i == 0)
    def _():
        zero_ref[...] = jnp.zeros_like(zero_ref)

        def pad_copy(r):
            return pltpu.make_async_copy(zero_ref.at[pl.ds(0, rt)], _row_tile(xs_ref, r, rt), zsem)

        def per_expert(e, total):
            r0 = meta_ref[e]
            cnt = meta_ref[N_EXPERTS + e]

            def group(j, carry):
                for u in range(PAD_UNROLL):
                    pad_copy(r0 + j * PAD_UNROLL + u).start(priority=u % 2)
                return carry

            def one(r, carry):
                pad_copy(r0 + r).start()
                return carry

            groups = cnt // PAD_UNROLL
            lax.fori_loop(0, groups, group, 0)
            lax.fori_loop(groups * PAD_UNROLL, cnt, one, 0)
            return total + cnt

        total = lax.fori_loop(0, N_EXPERTS, per_expert, 0)

        def drain_block(j, carry):
            pltpu.make_async_copy(zero_ref, xs_ref.at[pl.ds(0, bm)], zsem).wait()
            return carry

        def drain_group(j, carry):
            pltpu.make_async_copy(zero_ref.at[pl.ds(0, PAD_UNROLL * rt)],
                                  xs_ref.at[pl.ds(0, PAD_UNROLL * rt)], zsem).wait()
            return carry

        def drain_row(j, carry):
            pad_copy(0).wait()
            return carry

        lax.fori_loop(0, total // MOE_BM, drain_block, 0)
        rest = total % MOE_BM
        lax.fori_loop(0, rest // PAD_UNROLL, drain_group, 0)
        lax.fori_loop(0, rest % PAD_UNROLL, drain_row, 0)

        def tail_copy(b):
            return pltpu.make_async_copy(
                zero_ref, xs_ref.at[pl.ds(pl.multiple_of(b * bm, bm), bm)], zsem)

        n_used = meta_ref[2 * N_EXPERTS]

        def tail(b, carry):
            tail_copy(b).start()
            return carry

        lax.fori_loop(n_used, n_blocks, tail, 0)

        def tail_drain(b, carry):
            tail_copy(0).wait()
            return carry

        lax.fori_loop(n_used, n_blocks, tail_drain, 0)

    def wait_step(parity):
        for _ in range(TOP_K):
            pltpu.make_async_copy(ring_ref.at[parity], xs_ref.at[pl.ds(0, tm * rt)],
                                  sem.at[parity]).wait()

    @pl.when(i > 0)
    def _():
        wait_step(1 - (i & 1))

    @pl.when(i == pl.num_programs(0) - 1)
    def _():
        wait_step(i & 1)


def _dispatch(meta, dest, xn_t, n_tok, n_rows, tm=256):
    rt = xn_t.shape[0] // n_tok
    n_blocks = n_rows // MOE_BM
    grid_spec = pltpu.PrefetchScalarGridSpec(
        num_scalar_prefetch=1,
        grid=(n_tok // tm,),
        in_specs=[
            pl.BlockSpec((SUBLANES, tm), lambda i, meta: (0, i), memory_space=pltpu.SMEM),
            pl.BlockSpec((tm * rt, LANES), lambda i, meta: (i, 0)),
        ],
        out_specs=pl.BlockSpec(memory_space=pl.ANY),
        scratch_shapes=[
            pltpu.VMEM((MOE_BM * rt, LANES), xn_t.dtype),
            pltpu.VMEM((2, tm * rt, LANES), xn_t.dtype),
            pltpu.SemaphoreType.DMA((2,)),
            pltpu.SemaphoreType.DMA,
        ],
    )
    return pl.pallas_call(
        functools.partial(_dispatch_kernel, tm=tm, rt=rt, n_blocks=n_blocks),
        grid_spec=grid_spec,
        out_shape=jax.ShapeDtypeStruct((n_rows * rt, LANES), xn_t.dtype),
        compiler_params=_cparams(1),
        name="dispatch",
    )(meta, dest, xn_t)


def _experts_kernel(be_ref, nu_ref, nxt_ref, par_ref, nv_ref, x_ref, bg_ref, bu_ref, bd_ref,
                    wg_hbm, wu_hbm, wd_hbm, y_ref, wf_ref, wb_ref, sem):
    i = pl.program_id(0)
    used = i < nu_ref[0]
    e = be_ref[i]
    changed = (i == 0) | (e != be_ref[jnp.maximum(i - 1, 0)])

    def weight_copies(expert, slot):
        return [pltpu.make_async_copy(w.at[0, expert], wf_ref.at[slot, j], sem.at[slot])
                for j, w in enumerate((wg_hbm, wu_hbm, wd_hbm))]

    @pl.when(i == 0)
    def _():
        for cp in weight_copies(e, par_ref[e]):
            cp.start()

    @pl.when(used & changed)
    def _():
        slot = par_ref[e]
        for cp in weight_copies(e, slot):
            cp.wait()
        nxt = nxt_ref[e]

        @pl.when(nxt < N_EXPERTS)
        def _():
            for cp in weight_copies(nxt, 1 - slot):
                cp.start()

        for j in range(3):
            wb_ref[j] = wf_ref[slot, j].astype(BF16)

    rt = x_ref.shape[0] // MOE_BM

    def mlp(rows):
        x = _unpack_pairs_f32(_load_row_tiles(x_ref.at[pl.ds(0, rows * rt)], rows)).astype(BF16)
        gt = jnp.minimum(jnp.dot(x, wb_ref[0], preferred_element_type=F32) + bg_ref[...],
                         SWIGLU_LIMIT)
        up = jnp.clip(jnp.dot(x, wb_ref[1], preferred_element_type=F32) + bu_ref[...],
                      -SWIGLU_LIMIT, SWIGLU_LIMIT)
        hid = (up + 1.0) * (gt * _sigmoid(SWIGLU_ALPHA * gt))
        y = jnp.dot(hid.astype(BF16), wb_ref[2], preferred_element_type=F32) + bd_ref[...]
        _store_row_tiles(y_ref.at[pl.ds(0, rows * rt)], _pack_bf16_pairs(y))

    n_valid = nv_ref[i]

    part = MOE_BM // MOE_PARTS
    for q in range(1, MOE_PARTS + 1):
        rows = q * part

        @pl.when(used & (n_valid > rows - part) & ((n_valid <= rows) | (q == MOE_PARTS)))
        def _(rows=rows):
            mlp(rows)
            if rows < MOE_BM:
                rest = (MOE_BM - rows) * rt
                y_ref[pl.ds(rows * rt, rest), :] = jnp.zeros((rest, LANES), y_ref.dtype)

    @pl.when(jnp.logical_not(used))
    def _():
        y_ref[...] = jnp.zeros_like(y_ref)


def _experts(block_e, n_used, next_e, parity, block_valid, xs_t, w_gate, b_gate, w_up, b_up,
             w_down, b_down):
    d, d_ff = w_gate.shape[2:]
    assert d == d_ff, "weight staging buffers assume square expert matrices"
    rt = d // 2 // LANES
    n_blocks = xs_t.shape[0] // (MOE_BM * rt)
    bspec = lambda m: pl.BlockSpec((None, 1, m), lambda i, be, nu, nx, pa, nv: (be[i], 0, 0))
    hbm = pl.BlockSpec(memory_space=pl.ANY)
    grid_spec = pltpu.PrefetchScalarGridSpec(
        num_scalar_prefetch=5,
        grid=(n_blocks,),
        in_specs=[
            pl.BlockSpec((MOE_BM * rt, LANES), lambda i, be, nu, nx, pa, nv:
                         (jnp.maximum(jnp.minimum(i, nu[0] - 1), 0), 0)),
            bspec(d_ff), bspec(d_ff), bspec(d),
            hbm, hbm, hbm,
        ],
        out_specs=pl.BlockSpec((MOE_BM * rt, LANES), lambda i, be, nu, nx, pa, nv: (i, 0)),
        scratch_shapes=[
            pltpu.VMEM((2, 3, d, d_ff), F32),
            pltpu.VMEM((3, d, d_ff), BF16),
            pltpu.SemaphoreType.DMA((2,)),
        ],
    )
    return pl.pallas_call(
        _experts_kernel,
        grid_spec=grid_spec,
        out_shape=jax.ShapeDtypeStruct(xs_t.shape, U32),
        compiler_params=_cparams(1),
        name="experts",
    )(block_e, n_used, next_e, parity, block_valid, xs_t, b_gate, b_up, b_down,
      w_gate, w_up, w_down)


def _combine_kernel(dcur_ref, dnext_ref, h2_ref, route_ref, gain_ref, ys_ref, o_ref,
                    buf_ref, sem, *, tm, rt):
    i = pl.program_id(0)
    nsteps = pl.num_programs(0)
    slot = i & 1

    def issue_all(dref, s):
        def per_token(tk, carry):
            for kk in range(TOP_K):
                r = dref[kk, tk]
                pltpu.make_async_copy(_row_tile(ys_ref, r, rt),
                                      _row_tile(buf_ref.at[s, kk], tk, rt),
                                      sem.at[s]).start(priority=kk % 2)
            return carry
        lax.fori_loop(0, tm, per_token, 0, unroll=4)

    @pl.when(i == 0)
    def _():
        issue_all(dcur_ref, 0)

    @pl.when(i + 1 < nsteps)
    def _():
        issue_all(dnext_ref, 1 - slot)

    for kk in range(TOP_K):
        pltpu.make_async_copy(ys_ref.at[pl.ds(0, tm * rt)], buf_ref.at[slot, kk],
                              sem.at[slot]).wait()

    rec = route_ref[...]
    acc = h2_ref[...]
    for kk in range(TOP_K):
        acc = acc + (rec[:, ROUTE_GATE + kk:ROUTE_GATE + kk + 1]
                     * _unpack_pairs_f32(_load_row_tiles(buf_ref.at[slot, kk], tm)))
    o_ref[...] = _rmsnorm(acc, gain_ref[...])


def _combine(dest, h2, route, gain, ys_t, tm=256):
    n, d = h2.shape
    nsteps = n // tm
    rt = d // 2 // LANES
    return pl.pallas_call(
        functools.partial(_combine_kernel, tm=tm, rt=rt),
        grid=(nsteps,),
        in_specs=[
            pl.BlockSpec((SUBLANES, tm), lambda i: (0, i), memory_space=pltpu.SMEM),
            pl.BlockSpec((SUBLANES, tm), lambda i: (0, jnp.minimum(i + 1, nsteps - 1)),
                         memory_space=pltpu.SMEM),
            pl.BlockSpec((tm, d), lambda i: (i, 0)),
            pl.BlockSpec((tm, LANES), lambda i: (i, 0)),
            pl.BlockSpec((1, d), lambda i: (0, 0)),
            pl.BlockSpec(memory_space=pl.ANY),
        ],
        out_specs=pl.BlockSpec((tm, d), lambda i: (i, 0)),
        out_shape=jax.ShapeDtypeStruct((n, d), F32),
        scratch_shapes=[
            pltpu.VMEM((2, TOP_K, tm * rt, LANES), U32),
            pltpu.SemaphoreType.DMA((2,)),
        ],
        compiler_params=_cparams(1),
        name="combine",
    )(dest, dest, h2, route, gain, ys_t)


def kernel(x, mem, norm_mix, w_in, dn_conv, dn_a_log, dn_dt_bias, dn_norm, rg_conv, rg_conv_b, rg_w_a, rg_b_a, rg_w_x, rg_b_x, rg_lambda, w_out, norm_cross, norm_mem, w_cq, w_ckv, w_co, norm_moe, w_router, b_router, w_gate, b_gate, w_up, b_up, w_down, b_down, norm_final):
    batch, seq, d = x.shape
    mem_len = mem.shape[1]
    n = batch * seq
    assert w_in.shape[0] == 1, "single-layer trunk"
    x2 = x.reshape(n, d)

    wi = w_in[0]
    n_gate = 2 * DN_HEADS
    w_cat = jnp.concatenate(
        [wi[:, :4 * D_DN], wi[:, 4 * D_DN + n_gate:],
         jnp.pad(wi[:, 4 * D_DN:4 * D_DN + n_gate], ((0, 0), (0, LANES - n_gate)))],
        axis=1).astype(BF16)
    prow = (jnp.zeros((SUBLANES, LANES), F32)
            .at[0, DN_HEADS:n_gate].set(dn_a_log[0]).at[1, DN_HEADS:n_gate].set(dn_dt_bias[0]))
    pcol = (jnp.zeros((n_gate, LANES), F32)
            .at[DN_HEADS:, 0].set(dn_a_log[0]).at[DN_HEADS:, 1].set(dn_dt_bias[0]))

    proj, gbt = _in_proj(x2, norm_mix, w_cat, prow, pcol)
    dn = _deltanet(proj, gbt, dn_conv[0], dn_norm, batch, seq)
    rg = _rglru(proj, rg_conv[0], rg_conv_b, rg_w_a[0], rg_b_a[0].reshape(1, D_RG),
                rg_w_x[0], rg_b_x[0].reshape(1, D_RG), rg_lambda, batch, seq)
    kv = _mem_kv(mem.reshape(batch * mem_len, d), norm_mem, w_ckv[0])

    w_r = jnp.pad(w_router[0], ((0, 0), (0, LANES - N_EXPERTS)))
    w_r_hi = w_r.astype(BF16)
    w_r_lo = (w_r - w_r_hi.astype(F32)).astype(BF16)
    b_r = jnp.pad(b_router, ((0, 0), (0, LANES - N_EXPERTS)))
    h2, xn, route, route_t, counts = _post_mix(
        dn, rg, x2, w_out[0], norm_cross, w_cq[0], kv, w_co[0], norm_moe, w_r_hi, w_r_lo, b_r,
        seq, mem_len)

    n_blocks = n * TOP_K // MOE_BM + N_EXPERTS
    n_rows = n_blocks * MOE_BM
    cnt = counts[:, 0].astype(I32)
    padded = (cnt + MOE_BM - 1) // MOE_BM * MOE_BM
    pad_end = jnp.cumsum(padded)
    pad_start = pad_end - padded
    n_used = (pad_end[-1:] // MOE_BM).astype(I32)
    block_e = jnp.minimum(
        jnp.sum(pad_end[None, :] <= (jnp.arange(n_blocks, dtype=I32) * MOE_BM)[:, None], axis=1),
        N_EXPERTS - 1).astype(I32)
    meta = jnp.concatenate([pad_start + cnt, padded - cnt, n_used]).astype(I32)

    dest = _dest(pad_start.astype(I32), route_t)
    xs = _dispatch(meta, dest, xn, n, n_rows)
    has = cnt > 0
    eid = jnp.where(has, jnp.arange(N_EXPERTS, dtype=I32), N_EXPERTS)
    after = lax.cummin(eid, axis=0, reverse=True)
    next_e = jnp.concatenate([after[1:], jnp.full((1,), N_EXPERTS, I32)]).astype(I32)
    parity = ((jnp.cumsum(has.astype(I32)) - 1) & 1).astype(I32)
    pos = (jnp.arange(n_blocks, dtype=I32) * MOE_BM)[:, None]
    own = (pad_start[None, :] <= pos) & (pos < pad_end[None, :])
    left = jnp.sum(jnp.where(own, pad_start[None, :] + cnt[None, :], 0), axis=1) - pos[:, 0]
    block_valid = jnp.clip(left, 0, MOE_BM).astype(I32)
    ys = _experts(block_e, n_used, next_e, parity, block_valid, xs, w_gate, b_gate[0][:, None, :],
                  w_up, b_up[0][:, None, :], w_down, b_down[0][:, None, :])
    out = _combine(dest, h2, route, norm_final.reshape(1, d), ys)
    return out.reshape(batch, seq, d)
```

```python
import functools

import jax
import jax.numpy as jnp
from jax import lax
from jax.experimental import pallas as pl
from jax.experimental.pallas import tpu as pltpu

F32 = jnp.float32
BF16 = jnp.bfloat16
I32 = jnp.int32
U32 = jnp.uint32

EPS = 1e-6
LANES = 128
SUBLANES = 8
VMEM_LIMIT = 48 * 1024 * 1024

DN_HEADS = 4
DN_HEAD_DIM = 128
D_DN = DN_HEADS * DN_HEAD_DIM
D_RG = 512
RG_BLOCKS = 4
RG_C = 8.0
CONV_WIDTH = 4
XA_HEADS = 4
N_EXPERTS = 32
TOP_K = 4
SWIGLU_LIMIT = 7.0
SWIGLU_ALPHA = 1.702

DN_CHUNK = 128
INV_BASE = 16
MOE_BM = 512
PAD_UNROLL = 8

COL_RX = 4 * D_DN
COL_RY = COL_RX + D_RG
COL_GB = COL_RY + D_RG
PROJ_W = COL_GB + LANES


def _cparams(n_axes=1):
    return pltpu.CompilerParams(
        dimension_semantics=("arbitrary",) * n_axes, vmem_limit_bytes=VMEM_LIMIT)


def _mm(a, b):
    return jnp.dot(a.astype(BF16), b.astype(BF16), preferred_element_type=F32)


def _mm_nt(a, b):
    return lax.dot_general(a.astype(BF16), b.astype(BF16), (((1,), (1,)), ((), ())),
                           preferred_element_type=F32)


def _rmsnorm(x, g):
    return x * lax.rsqrt(jnp.mean(x * x, axis=-1, keepdims=True) + EPS) * g


def _sigmoid(x):
    return 0.5 * jnp.tanh(0.5 * x) + 0.5


def _mm_split(a, b_hi, b_lo):
    a_hi = a.astype(BF16)
    a_lo = (a - a_hi.astype(F32)).astype(BF16)
    return (jnp.dot(a_hi, b_hi, preferred_element_type=F32)
            + jnp.dot(a_hi, b_lo, preferred_element_type=F32)
            + jnp.dot(a_lo, b_hi, preferred_element_type=F32))


def _softplus(x):
    return jnp.maximum(x, 0.0) + jnp.log1p(jnp.exp(-jnp.abs(x)))


def _load_row_tiles(ref, rows):
    rt = ref.shape[0] // rows
    return jnp.concatenate([ref[pl.ds(s, rows, stride=rt), :] for s in range(rt)], axis=1)


def _store_row_tiles(ref, val):
    rows, w = val.shape
    rt = w // LANES
    for s in range(rt):
        ref[pl.ds(s, rows, stride=rt), :] = val[:, s * LANES:(s + 1) * LANES]


def _pack_bf16_pairs(x):
    half = x.shape[1] // 2
    hi = pltpu.bitcast(x[:, :half].astype(BF16).astype(F32), U32)
    lo = pltpu.bitcast(x[:, half:].astype(BF16).astype(F32), U32)
    return hi | (lo >> 16)


def _unpack_pairs_f32(p):
    hi = pltpu.bitcast(p & jnp.uint32(0xFFFF0000), F32)
    lo = pltpu.bitcast(p << 16, F32)
    return jnp.concatenate([hi, lo], axis=1)


def _load_segmented(ref):
    seg = ref.shape[0] // SUBLANES
    return jnp.concatenate([ref[pl.ds(j, SUBLANES, stride=seg), :] for j in range(seg)], axis=0)


def _causal_conv_segmented(xp, halo_ref, cs, w):
    tb = xp.shape[0]
    ng = w.shape[0] - 1
    prev = halo_ref[:, cs]
    last = xp[tb - ng * SUBLANES:, :]
    halo_ref[:, cs] = last
    sub = lax.broadcasted_iota(I32, (SUBLANES, xp.shape[1]), 0)
    groups = []
    for g in range(ng):
        rows = slice(g * SUBLANES, (g + 1) * SUBLANES)
        groups.append(jnp.where(sub == 0, pltpu.roll(prev[rows], 1, 0),
                                pltpu.roll(last[rows], 1, 0)))
    ext = jnp.concatenate(groups + [xp], axis=0)
    y = w[ng:ng + 1] * xp
    for k in range(1, ng + 1):
        y = y + w[ng - k:ng - k + 1] * ext[(ng - k) * SUBLANES:(ng - k) * SUBLANES + tb]
    return y


def _in_proj_kernel(x_ref, g_ref, w_ref, prow_ref, pcol_ref, proj_ref, gbt_ref):
    u = _rmsnorm(x_ref[...], g_ref[...]).astype(BF16)
    for c0 in range(0, COL_GB, 512):
        proj_ref[:, c0:c0 + 512] = jnp.dot(u, w_ref[:, c0:c0 + 512], preferred_element_type=F32)
    ba = jnp.dot(u, w_ref[:, COL_GB:PROJ_W], preferred_element_type=F32)
    lane = lax.broadcasted_iota(I32, ba.shape, 1)
    g = -jnp.exp(prow_ref[0:1, :]) * _softplus(ba + prow_ref[1:2, :])
    proj_ref[:, COL_GB:PROJ_W] = jnp.where(lane < DN_HEADS, _sigmoid(ba), g)
    bat = lax.dot_general(w_ref[:, COL_GB:PROJ_W], u, (((0,), (1,)), ((), ())),
                          preferred_element_type=F32)[0:2 * DN_HEADS, :]
    row = lax.broadcasted_iota(I32, bat.shape, 0)
    gt = -jnp.exp(pcol_ref[:, 0:1]) * _softplus(bat + pcol_ref[:, 1:2])
    gbt_ref[...] = jnp.where(row < DN_HEADS, _sigmoid(bat), gt)


def _in_proj(x2, gain, w_cat, prow, pcol, tm=1024):
    n, d = x2.shape
    return pl.pallas_call(
        _in_proj_kernel,
        grid=(n // tm,),
        in_specs=[
            pl.BlockSpec((tm, d), lambda i: (i, 0)),
            pl.BlockSpec((1, d), lambda i: (0, 0)),
            pl.BlockSpec((d, PROJ_W), lambda i: (0, 0), pipeline_mode=pl.Buffered(1)),
            pl.BlockSpec((SUBLANES, LANES), lambda i: (0, 0)),
            pl.BlockSpec((2 * DN_HEADS, LANES), lambda i: (0, 0)),
        ],
        out_specs=[
            pl.BlockSpec((tm, PROJ_W), lambda i: (i, 0)),
            pl.BlockSpec((2 * DN_HEADS, tm), lambda i: (0, i)),
        ],
        out_shape=[
            jax.ShapeDtypeStruct((n, PROJ_W), F32),
            jax.ShapeDtypeStruct((2 * DN_HEADS, n), F32),
        ],
        compiler_params=_cparams(1),
        name="in_proj",
    )(x2, gain, w_cat, prow, pcol)


def _deltanet_kernel(*refs, tb):
    n_grp = 3 * DN_HEADS
    qkv_refs = refs[:n_grp]
    (z_ref, gb_ref, gbt_ref, conv_ref, norm_ref, o_ref,
     s_ref, halo_ref, act_ref, gcc_ref, gcr_ref,
     a_ref, p_ref, d_ref, qk_ref, rhs_ref, u_ref, wq_ref, kdt_ref) = refs[n_grp:]
    t = pl.program_id(1)
    c = DN_CHUNK
    dh = DN_HEAD_DIM

    @pl.when(t == 0)
    def _():
        s_ref[...] = jnp.zeros_like(s_ref)
        halo_ref[...] = jnp.zeros_like(halo_ref)

    sub = lax.broadcasted_iota(I32, (SUBLANES, dh), 0)
    for grp in range(n_grp):
        cs = slice(grp * dh, (grp + 1) * dh)
        x = qkv_refs[grp][...]
        prev = halo_ref[:, cs]
        halo_ref[:, cs] = x[tb - SUBLANES:, :]
        y = conv_ref[CONV_WIDTH - 1:CONV_WIDTH, cs] * x
        for k in range(1, CONV_WIDTH):
            xs = pltpu.roll(x, k, 0)
            head = jnp.where(sub < k, pltpu.roll(prev, k, 0), xs[:SUBLANES])
            xs = jnp.concatenate([head, xs[SUBLANES:]], axis=0)
            y = y + conv_ref[CONV_WIDTH - 1 - k:CONV_WIDTH - k, cs] * xs
        y = y * _sigmoid(y)
        if grp < 2 * DN_HEADS:
            y = y * lax.rsqrt(jnp.sum(y * y, axis=-1, keepdims=True) + EPS)
        if grp < DN_HEADS:
            y = y * (dh ** -0.5)
        act_ref[grp] = y

    gcol = gb_ref[...]
    rpos = lax.broadcasted_iota(I32, gcol.shape, 0) & (c - 1)
    d = 1
    while d < c:
        gcol = gcol + jnp.where(rpos >= d, pltpu.roll(gcol, d, 0), 0.0)
        d *= 2
    gcc_ref[...] = gcol
    grow = gbt_ref[...]
    lpos = lax.broadcasted_iota(I32, grow.shape, 1) & (c - 1)
    d = 1
    while d < c:
        grow = grow + jnp.where(lpos >= d, pltpu.roll(grow, d, 1), 0.0)
        d *= 2
    gcr_ref[...] = grow

    row = lax.broadcasted_iota(I32, (c, c), 0)
    col = lax.broadcasted_iota(I32, (c, c), 1)
    causal = row >= col
    strict = row > col
    eye = jnp.where(row == col, 1.0, 0.0)
    gain = norm_ref[...]
    n_chunks = tb // c
    probs = [(ci, h) for ci in range(n_chunks) for h in range(DN_HEADS)]

    decay_last = []
    for p, (ci, h) in enumerate(probs):
        rows = slice(ci * c, (ci + 1) * c)
        q = act_ref[h, rows, :]
        k = act_ref[DN_HEADS + h, rows, :]
        v = act_ref[2 * DN_HEADS + h, rows, :]
        beta = gb_ref[rows, h:h + 1]
        gc = gcc_ref[rows, DN_HEADS + h:DN_HEADS + h + 1]
        gr = gcr_ref[DN_HEADS + h:DN_HEADS + h + 1, rows]
        g_last = gc[c - 1:c, :]
        decay = jnp.where(causal, jnp.exp(jnp.where(causal, gc - gr, 0.0)), 0.0)
        kb = k * beta
        both = _mm_nt(jnp.concatenate([kb, q], axis=0), k)
        a_ref[p] = jnp.where(strict, both[:c] * decay, 0.0)
        qk_ref[p] = (both[c:] * decay).astype(BF16)
        egc = jnp.exp(gc)
        rhs_ref[p] = jnp.concatenate([v * beta, kb * egc], axis=1).astype(BF16)
        wq_ref[p, c:, :] = (q * egc).astype(BF16)
        kdt_ref[p] = (k * jnp.exp(g_last - gc)).T.astype(BF16)
        decay_last.append(jnp.exp(g_last))

    shift = INV_BASE.bit_length() - 1
    blk = (row >> shift) == (col >> shift)
    for p in range(len(probs)):
        diag = jnp.where(blk, a_ref[p], 0.0)
        p_ref[p] = eye - diag
        d_ref[p] = _mm(diag, diag).astype(BF16)
    for it in range(shift - 1):
        for p in range(len(probs)):
            pw = d_ref[p]
            inv = p_ref[p]
            p_ref[p] = inv + _mm(inv, pw)
            if it < shift - 2:
                d_ref[p] = _mm(pw, pw).astype(BF16)
    s = INV_BASE
    while s < c:
        sh = s.bit_length() - 1
        off = ((row >> (sh + 1)) == (col >> (sh + 1))) & ((row >> sh) != (col >> sh))
        for p in range(len(probs)):
            d_ref[p] = _mm(p_ref[p], jnp.where(off, a_ref[p], 0.0)).astype(BF16)
        for p in range(len(probs)):
            inv = p_ref[p]
            p_ref[p] = inv - _mm(d_ref[p], inv)
        s *= 2
    for p in range(len(probs)):
        uw = _mm(p_ref[p], rhs_ref[p])
        u_ref[p] = uw[:, :dh]
        wq_ref[p, :c, :] = uw[:, dh:].astype(BF16)

    for ci in range(n_chunks):
        rows = slice(ci * c, (ci + 1) * c)
        ps = [ci * DN_HEADS + h for h in range(DN_HEADS)]
        s_old = [s_ref[h] for h in range(DN_HEADS)]
        ws = [_mm(wq_ref[p], s_old[h]) for h, p in enumerate(ps)]
        v_new = [u_ref[p] - ws[h][:c] for h, p in enumerate(ps)]
        outs = [ws[h][c:] + _mm(qk_ref[p], v_new[h]) for h, p in enumerate(ps)]
        for h, p in enumerate(ps):
            s_ref[h] = s_old[h] * decay_last[p] + _mm(kdt_ref[p], v_new[h])
        for h in range(DN_HEADS):
            o = _rmsnorm(outs[h], gain)
            zz = z_ref[rows, h * dh:(h + 1) * dh]
            o_ref[rows, h * dh:(h + 1) * dh] = (o * (zz * _sigmoid(zz))).astype(o_ref.dtype)


def _deltanet(proj, gbt, dn_conv, dn_norm, batch, seq, tb=512):
    n = proj.shape[0]
    nt = seq // tb
    c = DN_CHUNK
    n_prob = (tb // c) * DN_HEADS
    n_grp = 3 * DN_HEADS
    grp = lambda j: pl.BlockSpec((tb, DN_HEAD_DIM), lambda b, t, j=j: (b * nt + t, j))
    return pl.pallas_call(
        functools.partial(_deltanet_kernel, tb=tb),
        grid=(batch, nt),
        in_specs=[grp(j) for j in range(n_grp)] + [
            pl.BlockSpec((tb, D_DN), lambda b, t: (b * nt + t, 3)),
            pl.BlockSpec((tb, LANES), lambda b, t: (b * nt + t, COL_GB // LANES)),
            pl.BlockSpec((2 * DN_HEADS, tb), lambda b, t: (0, b * nt + t)),
            pl.BlockSpec((CONV_WIDTH, 3 * D_DN), lambda b, t: (0, 0)),
            pl.BlockSpec((1, DN_HEAD_DIM), lambda b, t: (0, 0)),
        ],
        out_specs=pl.BlockSpec((tb, D_DN), lambda b, t: (b * nt + t, 0)),
        out_shape=jax.ShapeDtypeStruct((n, D_DN), BF16),
        scratch_shapes=[
            pltpu.VMEM((DN_HEADS, DN_HEAD_DIM, DN_HEAD_DIM), F32),
            pltpu.VMEM((SUBLANES, 3 * D_DN), F32),
            pltpu.VMEM((n_grp, tb, DN_HEAD_DIM), F32),
            pltpu.VMEM((tb, LANES), F32),
            pltpu.VMEM((2 * DN_HEADS, tb), F32),
            pltpu.VMEM((n_prob, c, c), F32),
            pltpu.VMEM((n_prob, c, c), F32),
            pltpu.VMEM((n_prob, c, c), BF16),
            pltpu.VMEM((n_prob, c, c), BF16),
            pltpu.VMEM((n_prob, c, 2 * DN_HEAD_DIM), BF16),
            pltpu.VMEM((n_prob, c, DN_HEAD_DIM), F32),
            pltpu.VMEM((n_prob, 2 * c, DN_HEAD_DIM), BF16),
            pltpu.VMEM((n_prob, DN_HEAD_DIM, c), BF16),
        ],
        compiler_params=_cparams(2),
        name="deltanet",
    )(*([proj] * (n_grp + 2)), gbt, dn_conv, dn_norm)


def _gelu_tanh(x):
    return 0.5 * x * (1.0 + jnp.tanh(0.7978845608028654 * (x + 0.044715 * (x * x * x))))


def _rglru_kernel(*refs, tb):
    rx_refs = refs[:RG_BLOCKS]
    (ry_ref, conv_ref, convb_ref, wa_ref, ba_ref, wx_ref, bx_ref, lam_ref,
     o_ref, hc_ref, halo_ref, a_ref, b_ref, h_ref) = refs[RG_BLOCKS:]
    t = pl.program_id(1)
    bw = D_RG // RG_BLOCKS
    seg = tb // SUBLANES

    @pl.when(t == 0)
    def _():
        hc_ref[...] = jnp.zeros_like(hc_ref)
        halo_ref[...] = jnp.zeros_like(halo_ref)

    log_sig = -_softplus(-lam_ref[...])
    rowi = lax.broadcasted_iota(I32, (tb, bw), 0)
    seq_start_row = jnp.where(t == 0, 0, -1)
    for nb in range(RG_BLOCKS):
        cs = slice(nb * bw, (nb + 1) * bw)
        xb = (_causal_conv_segmented(_load_segmented(rx_refs[nb]), halo_ref, cs,
                                     conv_ref[:, cs]) + convb_ref[:, cs])
        r = _sigmoid(_mm(xb, wa_ref[nb]) + ba_ref[:, cs])
        gi = _sigmoid(_mm(xb, wx_ref[nb]) + bx_ref[:, cs])
        log_a = RG_C * r * log_sig[:, cs]
        a = jnp.exp(log_a)
        y = jnp.tanh(-log_a) * (1.0 + a * a)
        mult = jnp.where(y > 0.0, y * lax.rsqrt(y), 0.0)
        mult = jnp.where(rowi == seq_start_row, 1.0, mult)
        a_ref[:, cs] = a
        b_ref[:, cs] = mult * (gi * xb)

    def scan(j, carry):
        ac, bc = carry
        rows = pl.ds(pl.multiple_of(j * SUBLANES, SUBLANES), SUBLANES)
        aj = a_ref[rows, :]
        bc = aj * bc + b_ref[rows, :]
        ac = aj * ac
        a_ref[rows, :] = ac
        b_ref[rows, :] = bc
        return ac, bc

    ac, bc = lax.fori_loop(1, seg, scan, (a_ref[0:SUBLANES, :], b_ref[0:SUBLANES, :]), unroll=7)

    h = hc_ref[...]
    h_in = []
    for s in range(SUBLANES):
        h_in.append(h)
        h = ac[s:s + 1, :] * h + bc[s:s + 1, :]
    hc_ref[...] = h
    h_in = jnp.concatenate(h_in, axis=0)

    for j in range(seg):
        rows = slice(j * SUBLANES, (j + 1) * SUBLANES)
        hj = a_ref[rows, :] * h_in + b_ref[rows, :]
        for nb in range(RG_BLOCKS):
            h_ref[nb, rows, :] = hj[:, nb * bw:(nb + 1) * bw]
    g = seg // SUBLANES
    for nb in range(RG_BLOCKS):
        cs = slice(nb * bw, (nb + 1) * bw)
        h = jnp.concatenate(
            [h_ref[nb, pl.ds((i % g) * SUBLANES * SUBLANES + i // g, SUBLANES, stride=SUBLANES), :]
             for i in range(seg)], axis=0)
        o_ref[:, cs] = (h * _gelu_tanh(ry_ref[:, cs])).astype(o_ref.dtype)


def _rglru(proj, rg_conv, rg_conv_b, w_a, b_a, w_x, b_x, lam, batch, seq, tb=512):
    n = proj.shape[0]
    nt = seq // tb
    bw = D_RG // RG_BLOCKS
    full = lambda shape: pl.BlockSpec(shape, lambda b, t: (0,) * len(shape))
    return pl.pallas_call(
        functools.partial(_rglru_kernel, tb=tb),
        grid=(batch, nt),
        in_specs=[pl.BlockSpec((tb, bw), lambda b, t, j=j: (b * nt + t, COL_RX // bw + j))
                  for j in range(RG_BLOCKS)] + [
            pl.BlockSpec((tb, D_RG), lambda b, t: (b * nt + t, COL_RY // D_RG)),
            full((CONV_WIDTH, D_RG)), full((1, D_RG)),
            full((RG_BLOCKS, bw, bw)), full((1, D_RG)),
            full((RG_BLOCKS, bw, bw)), full((1, D_RG)),
            full((1, D_RG)),
        ],
        out_specs=pl.BlockSpec((tb, D_RG), lambda b, t: (b * nt + t, 0)),
        out_shape=jax.ShapeDtypeStruct((n, D_RG), BF16),
        scratch_shapes=[
            pltpu.VMEM((1, D_RG), F32),
            pltpu.VMEM(((CONV_WIDTH - 1) * SUBLANES, D_RG), F32),
            pltpu.VMEM((tb, D_RG), F32),
            pltpu.VMEM((tb, D_RG), F32),
            pltpu.VMEM((RG_BLOCKS, tb, bw), F32),
        ],
        compiler_params=_cparams(2),
        name="rglru",
    )(*([proj] * (RG_BLOCKS + 1)), rg_conv, rg_conv_b, w_a, b_a, w_x, b_x, lam)


def _mem_kv_kernel(m_ref, g_ref, w_ref, o_ref):
    mn = _rmsnorm(m_ref[...], g_ref[...]).astype(BF16)
    for c0 in range(0, o_ref.shape[1], 512):
        o_ref[:, c0:c0 + 512] = jnp.dot(mn, w_ref[:, c0:c0 + 512].astype(BF16),
                                        preferred_element_type=F32).astype(o_ref.dtype)


def _mem_kv(mem2, gain, w_ckv):
    n, d = mem2.shape
    return pl.pallas_call(
        _mem_kv_kernel,
        grid=(1,),
        in_specs=[
            pl.BlockSpec((n, d), lambda i: (0, 0)),
            pl.BlockSpec((1, d), lambda i: (0, 0)),
            pl.BlockSpec((d, 2 * d), lambda i: (0, 0)),
        ],
        out_specs=pl.BlockSpec((n, 2 * d), lambda i: (0, 0)),
        out_shape=jax.ShapeDtypeStruct((n, 2 * d), BF16),
        compiler_params=_cparams(1),
        name="mem_kv",
    )(mem2, gain, w_ckv)


ROUTE_E = 0
ROUTE_RANK = TOP_K
ROUTE_GATE = 2 * TOP_K
ROUTE_ROWS = 16


def _post_mix_kernel(dn_ref, rg_ref, x_ref, wo_f32, gx_ref, wq_f32, kv_ref, wco_f32, gm_ref,
                     wrh_ref, wrl_ref, br_ref, h2_ref, xn_ref, route_ref, routet_ref, cnt_ref,
                     carry_ref, wo_ref, wq_ref, wco_ref):
    i = pl.program_id(0)
    tm, d = x_ref.shape
    hd = d // XA_HEADS

    @pl.when(i == 0)
    def _():
        carry_ref[...] = jnp.zeros_like(carry_ref)
        wo_ref[...] = wo_f32[...].astype(BF16)
        wq_ref[...] = wq_f32[...].astype(BF16)
        wco_ref[...] = wco_f32[...].astype(BF16)

    h1 = (x_ref[...] + jnp.dot(dn_ref[...], wo_ref[0:D_DN, :], preferred_element_type=F32)
          + jnp.dot(rg_ref[...], wo_ref[D_DN:, :], preferred_element_type=F32))

    hn = _rmsnorm(h1, gx_ref[...]).astype(BF16)
    q = jnp.dot(hn, wq_ref[...], preferred_element_type=F32)
    heads = []
    for hh in range(XA_HEADS):
        cs = slice(hh * hd, (hh + 1) * hd)
        s = _mm_nt(q[:, cs], kv_ref[:, cs]) * (hd ** -0.5)
        p = jnp.exp(s - jnp.max(s, axis=-1, keepdims=True))
        p = p / jnp.sum(p, axis=-1, keepdims=True)
        heads.append(_mm(p, kv_ref[:, d + hh * hd:d + (hh + 1) * hd]).astype(BF16))
    o = jnp.concatenate(heads, axis=1)
    h2 = h1 + jnp.dot(o, wco_ref[...], preferred_element_type=F32)
    h2_ref[...] = h2

    xn = _rmsnorm(h2, gm_ref[...])
    _store_row_tiles(xn_ref, _pack_bf16_pairs(xn))
    logits = _mm_split(xn, wrh_ref[...], wrl_ref[...]) + br_ref[...]
    lg = logits.T[0:N_EXPERTS, :]
    eidx = lax.broadcasted_iota(I32, lg.shape, 0).astype(F32)
    neg = jnp.float32(-jnp.inf)
    vals, idxs, hots = [], [], []
    for _ in range(TOP_K):
        m = jnp.max(lg, axis=0, keepdims=True)
        idx = jnp.min(jnp.where(lg == m, eidx, float(N_EXPERTS)), axis=0, keepdims=True)
        hot = eidx == idx
        lg = jnp.where(hot, neg, lg)
        vals.append(m)
        idxs.append(idx)
        hots.append(hot)
    es = [jnp.exp(v - vals[0]) for v in vals]
    den = es[0] + es[1] + es[2] + es[3]
    gates = [e / den for e in es]

    chosen = jnp.zeros(lg.shape, F32)
    for hot in hots:
        chosen = chosen + jnp.where(hot, 1.0, 0.0)
    r2 = lax.broadcasted_iota(I32, (tm, tm), 0)
    c2 = lax.broadcasted_iota(I32, (tm, tm), 1)
    before = _mm(chosen, jnp.where(r2 < c2, 1.0, 0.0)) + carry_ref[:, 0:1]
    ranks = [jnp.sum(jnp.where(hot, before, 0.0), axis=0, keepdims=True) for hot in hots]
    carry_ref[...] = carry_ref[...] + jnp.sum(chosen, axis=1, keepdims=True)
    cnt_ref[...] = carry_ref[...]

    row = lax.broadcasted_iota(I32, (ROUTE_ROWS, tm), 0)
    rect = jnp.zeros((ROUTE_ROWS, tm), F32)
    for kk in range(TOP_K):
        rect = jnp.where(row == ROUTE_E + kk, idxs[kk], rect)
        rect = jnp.where(row == ROUTE_RANK + kk, ranks[kk], rect)
        rect = jnp.where(row == ROUTE_GATE + kk, gates[kk], rect)
    routet_ref[...] = rect
    route_ref[...] = jnp.concatenate(
        [rect, jnp.zeros((LANES - ROUTE_ROWS, tm), F32)], axis=0).T


def _post_mix(dn, rg, x2, w_out, g_cross, w_cq, kv, w_co, g_moe, w_r_hi, w_r_lo, b_r, seq, mem_len,
              tm=512):
    n, d = x2.shape
    per_b = seq // tm
    rt_x = d // 2 // LANES
    full = lambda shape: pl.BlockSpec(shape, lambda i: (0,) * len(shape))
    once = lambda shape: pl.BlockSpec(shape, lambda i: (0,) * len(shape),
                                      pipeline_mode=pl.Buffered(1))
    return pl.pallas_call(
        _post_mix_kernel,
        grid=(n // tm,),
        in_specs=[
            pl.BlockSpec((tm, D_DN), lambda i: (i, 0)),
            pl.BlockSpec((tm, D_RG), lambda i: (i, 0)),
            pl.BlockSpec((tm, d), lambda i: (i, 0)),
            once((d, d)), full((1, d)), once((d, d)),
            pl.BlockSpec((mem_len, 2 * d), lambda i: (i // per_b, 0)),
            once((d, d)), full((1, d)),
            full((d, LANES)), full((d, LANES)), full((1, LANES)),
        ],
        out_specs=[
            pl.BlockSpec((tm, d), lambda i: (i, 0)),
            pl.BlockSpec((tm * rt_x, LANES), lambda i: (i, 0)),
            pl.BlockSpec((tm, LANES), lambda i: (i, 0)),
            pl.BlockSpec((ROUTE_ROWS, tm), lambda i: (0, i)),
            pl.BlockSpec((N_EXPERTS, LANES), lambda i: (0, 0)),
        ],
        out_shape=[
            jax.ShapeDtypeStruct((n, d), F32),
            jax.ShapeDtypeStruct((n * rt_x, LANES), U32),
            jax.ShapeDtypeStruct((n, LANES), F32),
            jax.ShapeDtypeStruct((ROUTE_ROWS, n), F32),
            jax.ShapeDtypeStruct((N_EXPERTS, LANES), F32),
        ],
        scratch_shapes=[pltpu.VMEM((N_EXPERTS, LANES), F32),
                        pltpu.VMEM((d, d), BF16), pltpu.VMEM((d, d), BF16), pltpu.VMEM((d, d), BF16)],
        compiler_params=_cparams(1),
        name="post_mix",
    )(dn, rg, x2, w_out, g_cross, w_cq, kv, w_co, g_moe, w_r_hi, w_r_lo, b_r)


def _dest_kernel(start_ref, routet_ref, dest_ref):
    e = routet_ref[ROUTE_E:ROUTE_E + SUBLANES, :]
    rank = routet_ref[ROUTE_RANK:ROUTE_RANK + SUBLANES, :]
    base = jnp.zeros(e.shape, F32)
    for j in range(N_EXPERTS):
        base = jnp.where(e == float(j), start_ref[j].astype(F32), base)
    row = lax.broadcasted_iota(I32, e.shape, 0)
    dest_ref[...] = jnp.where(row < TOP_K, (base + rank).astype(I32), 0)


def _dest(pad_start, route_t, tm=2048):
    n = route_t.shape[1]
    grid_spec = pltpu.PrefetchScalarGridSpec(
        num_scalar_prefetch=1,
        grid=(n // tm,),
        in_specs=[pl.BlockSpec((ROUTE_ROWS, tm), lambda i, st: (0, i))],
        out_specs=pl.BlockSpec((SUBLANES, tm), lambda i, st: (0, i)),
    )
    return pl.pallas_call(
        _dest_kernel,
        grid_spec=grid_spec,
        out_shape=jax.ShapeDtypeStruct((SUBLANES, n), I32),
        compiler_params=_cparams(1),
        name="dest",
    )(pad_start, route_t)


def _row_tile(ref, r, rt):
    return ref.at[pl.ds(pl.multiple_of(r * rt, rt), rt)]


def _dispatch_kernel(meta_ref, dest_ref, xn_ref, xs_ref, zero_ref, ring_ref, sem, zsem,
                     *, tm, rt, n_blocks):
    i = pl.program_id(0)
    bm = MOE_BM * rt
    ring = ring_ref.at[i & 1]
    ring[...] = xn_ref[...]

    def per_token(tk, carry):
        src = _row_tile(ring, tk, rt)
        for kk in range(TOP_K):
            r = dest_ref[kk, tk]
            pltpu.make_async_copy(src, _row_tile(xs_ref, r, rt),
                                  sem.at[i & 1]).start(priority=kk % 2)
        return carry

    lax.fori_loop(0, tm, per_token, 0, unroll=4)

    @pl.when(i == 0)
    def _():
        zero_ref[...] = jnp.zeros_like(zero_ref)

        def pad_copy(r):
            return pltpu.make_async_copy(zero_ref.at[pl.ds(0, rt)], _row_tile(xs_ref, r, rt), zsem)

        def per_expert(e, total):
            r0 = meta_ref[e]
            cnt = meta_ref[N_EXPERTS + e]

            def group(j, carry):
                for u in range(PAD_UNROLL):
                    pad_copy(r0 + j * PAD_UNROLL + u).start(priority=u % 2)
                return carry

            def one(r, carry):
                pad_copy(r0 + r).start()
                return carry

            groups = cnt // PAD_UNROLL
            lax.fori_loop(0, groups, group, 0)
            lax.fori_loop(groups * PAD_UNROLL, cnt, one, 0)
            return total + cnt

        total = lax.fori_loop(0, N_EXPERTS, per_expert, 0)

        def drain_block(j, carry):
            pltpu.make_async_copy(zero_ref, xs_ref.at[pl.ds(0, bm)], zsem).wait()
            return carry

        def drain_group(j, carry):
            pltpu.make_async_copy(zero_ref.at[pl.ds(0, PAD_UNROLL * rt)],
                                  xs_ref.at[pl.ds(0, PAD_UNROLL * rt)], zsem).wait()
            return carry

        def drain_row(j, carry):
            pad_copy(0).wait()
            return carry

        lax.fori_loop(0, total // MOE_BM, drain_block, 0)
        rest = total % MOE_BM
        lax.fori_loop(0, rest // PAD_UNROLL, drain_group, 0)
        lax.fori_loop(0, rest % PAD_UNROLL, drain_row, 0)

        def tail_copy(b):
            return pltpu.make_async_copy(
                zero_ref, xs_ref.at[pl.ds(pl.multiple_of(b * bm, bm), bm)], zsem)

        n_used = meta_ref[2 * N_EXPERTS]

        def tail(b, carry):
            tail_copy(b).start()
            return carry

        lax.fori_loop(n_used, n_blocks, tail, 0)

        def tail_drain(b, carry):
            tail_copy(0).wait()
            return carry

        lax.fori_loop(n_used, n_blocks, tail_drain, 0)

    def wait_step(parity):
        for _ in range(TOP_K):
            pltpu.make_async_copy(ring_ref.at[parity], xs_ref.at[pl.ds(0, tm * rt)],
                                  sem.at[parity]).wait()

    @pl.when(i > 0)
    def _():
        wait_step(1 - (i & 1))

    @pl.when(i == pl.num_programs(0) - 1)
    def _():
        wait_step(i & 1)


def _dispatch(meta, dest, xn_t, n_tok, n_rows, tm=256):
    rt = xn_t.shape[0] // n_tok
    n_blocks = n_rows // MOE_BM
    grid_spec = pltpu.PrefetchScalarGridSpec(
        num_scalar_prefetch=1,
        grid=(n_tok // tm,),
        in_specs=[
            pl.BlockSpec((SUBLANES, tm), lambda i, meta: (0, i), memory_space=pltpu.SMEM),
            pl.BlockSpec((tm * rt, LANES), lambda i, meta: (i, 0)),
        ],
        out_specs=pl.BlockSpec(memory_space=pl.ANY),
        scratch_shapes=[
            pltpu.VMEM((MOE_BM * rt, LANES), xn_t.dtype),
            pltpu.VMEM((2, tm * rt, LANES), xn_t.dtype),
            pltpu.SemaphoreType.DMA((2,)),
            pltpu.SemaphoreType.DMA,
        ],
    )
    return pl.pallas_call(
        functools.partial(_dispatch_kernel, tm=tm, rt=rt, n_blocks=n_blocks),
        grid_spec=grid_spec,
        out_shape=jax.ShapeDtypeStruct((n_rows * rt, LANES), xn_t.dtype),
        compiler_params=_cparams(1),
        name="dispatch",
    )(meta, dest, xn_t)


def _experts_kernel(be_ref, nu_ref, nxt_ref, par_ref, nv_ref, x_ref, bg_ref, bu_ref, bd_ref,
                    wg_hbm, wu_hbm, wd_hbm, y_ref, wf_ref, wb_ref, sem):
    i = pl.program_id(0)
    used = i < nu_ref[0]
    e = be_ref[i]
    changed = (i == 0) | (e != be_ref[jnp.maximum(i - 1, 0)])

    def weight_copies(expert, slot):
        return [pltpu.make_async_copy(w.at[0, expert], wf_ref.at[slot, j], sem.at[slot])
                for j, w in enumerate((wg_hbm, wu_hbm, wd_hbm))]

    @pl.when(i == 0)
    def _():
        for cp in weight_copies(e, par_ref[e]):
            cp.start()

    @pl.when(used & changed)
    def _():
        slot = par_ref[e]
        for cp in weight_copies(e, slot):
            cp.wait()
        nxt = nxt_ref[e]

        @pl.when(nxt < N_EXPERTS)
        def _():
            for cp in weight_copies(nxt, 1 - slot):
                cp.start()

        for j in range(3):
            wb_ref[j] = wf_ref[slot, j].astype(BF16)

    rt = x_ref.shape[0] // MOE_BM

    def mlp(rows):
        x = _unpack_pairs_f32(_load_row_tiles(x_ref.at[pl.ds(0, rows * rt)], rows)).astype(BF16)
        gt = jnp.minimum(jnp.dot(x, wb_ref[0], preferred_element_type=F32) + bg_ref[...],
                         SWIGLU_LIMIT)
        up = jnp.clip(jnp.dot(x, wb_ref[1], preferred_element_type=F32) + bu_ref[...],
                      -SWIGLU_LIMIT, SWIGLU_LIMIT)
        hid = (up + 1.0) * (gt * _sigmoid(SWIGLU_ALPHA * gt))
        y = jnp.dot(hid.astype(BF16), wb_ref[2], preferred_element_type=F32) + bd_ref[...]
        _store_row_tiles(y_ref.at[pl.ds(0, rows * rt)], _pack_bf16_pairs(y))

    n_valid = nv_ref[i]

    quarter = MOE_BM // 4
    for q in range(1, 5):
        rows = q * quarter

        @pl.when(used & (n_valid > rows - quarter) & ((n_valid <= rows) | (q == 4)))
        def _(rows=rows):
            mlp(rows)
            if rows < MOE_BM:
                rest = (MOE_BM - rows) * rt
                y_ref[pl.ds(rows * rt, rest), :] = jnp.zeros((rest, LANES), y_ref.dtype)

    @pl.when(jnp.logical_not(used))
    def _():
        y_ref[...] = jnp.zeros_like(y_ref)


def _experts(block_e, n_used, next_e, parity, block_valid, xs_t, w_gate, b_gate, w_up, b_up,
             w_down, b_down):
    d, d_ff = w_gate.shape[2:]
    assert d == d_ff, "weight staging buffers assume square expert matrices"
    rt = d // 2 // LANES
    n_blocks = xs_t.shape[0] // (MOE_BM * rt)
    bspec = lambda m: pl.BlockSpec((None, 1, m), lambda i, be, nu, nx, pa, nv: (be[i], 0, 0))
    hbm = pl.BlockSpec(memory_space=pl.ANY)
    grid_spec = pltpu.PrefetchScalarGridSpec(
        num_scalar_prefetch=5,
        grid=(n_blocks,),
        in_specs=[
            pl.BlockSpec((MOE_BM * rt, LANES), lambda i, be, nu, nx, pa, nv:
                         (jnp.maximum(jnp.minimum(i, nu[0] - 1), 0), 0)),
            bspec(d_ff), bspec(d_ff), bspec(d),
            hbm, hbm, hbm,
        ],
        out_specs=pl.BlockSpec((MOE_BM * rt, LANES), lambda i, be, nu, nx, pa, nv: (i, 0)),
        scratch_shapes=[
            pltpu.VMEM((2, 3, d, d_ff), F32),
            pltpu.VMEM((3, d, d_ff), BF16),
            pltpu.SemaphoreType.DMA((2,)),
        ],
    )
    return pl.pallas_call(
        _experts_kernel,
        grid_spec=grid_spec,
        out_shape=jax.ShapeDtypeStruct(xs_t.shape, U32),
        compiler_params=_cparams(1),
        name="experts",
    )(block_e, n_used, next_e, parity, block_valid, xs_t, b_gate, b_up, b_down,
      w_gate, w_up, w_down)


def _combine_kernel(dcur_ref, dnext_ref, h2_ref, route_ref, gain_ref, ys_ref, o_ref,
                    buf_ref, sem, *, tm, rt):
    i = pl.program_id(0)
    nsteps = pl.num_programs(0)
    slot = i & 1

    def issue_all(dref, s):
        def per_token(tk, carry):
            for kk in range(TOP_K):
                r = dref[kk, tk]
                pltpu.make_async_copy(_row_tile(ys_ref, r, rt),
                                      _row_tile(buf_ref.at[s, kk], tk, rt),
                                      sem.at[s]).start(priority=kk % 2)
            return carry
        lax.fori_loop(0, tm, per_token, 0, unroll=4)

    @pl.when(i == 0)
    def _():
        issue_all(dcur_ref, 0)

    @pl.when(i + 1 < nsteps)
    def _():
        issue_all(dnext_ref, 1 - slot)

    for kk in range(TOP_K):
        pltpu.make_async_copy(ys_ref.at[pl.ds(0, tm * rt)], buf_ref.at[slot, kk],
                              sem.at[slot]).wait()

    rec = route_ref[...]
    acc = h2_ref[...]
    for kk in range(TOP_K):
        acc = acc + (rec[:, ROUTE_GATE + kk:ROUTE_GATE + kk + 1]
                     * _unpack_pairs_f32(_load_row_tiles(buf_ref.at[slot, kk], tm)))
    o_ref[...] = _rmsnorm(acc, gain_ref[...])


def _combine(dest, h2, route, gain, ys_t, tm=256):
    n, d = h2.shape
    nsteps = n // tm
    rt = d // 2 // LANES
    return pl.pallas_call(
        functools.partial(_combine_kernel, tm=tm, rt=rt),
        grid=(nsteps,),
        in_specs=[
            pl.BlockSpec((SUBLANES, tm), lambda i: (0, i), memory_space=pltpu.SMEM),
            pl.BlockSpec((SUBLANES, tm), lambda i: (0, jnp.minimum(i + 1, nsteps - 1)),
                         memory_space=pltpu.SMEM),
            pl.BlockSpec((tm, d), lambda i: (i, 0)),
            pl.BlockSpec((tm, LANES), lambda i: (i, 0)),
            pl.BlockSpec((1, d), lambda i: (0, 0)),
            pl.BlockSpec(memory_space=pl.ANY),
        ],
        out_specs=pl.BlockSpec((tm, d), lambda i: (i, 0)),
        out_shape=jax.ShapeDtypeStruct((n, d), F32),
        scratch_shapes=[
            pltpu.VMEM((2, TOP_K, tm * rt, LANES), U32),
            pltpu.SemaphoreType.DMA((2,)),
        ],
        compiler_params=_cparams(1),
        name="combine",
    )(dest, dest, h2, route, gain, ys_t)


def kernel(x, mem, norm_mix, w_in, dn_conv, dn_a_log, dn_dt_bias, dn_norm, rg_conv, rg_conv_b, rg_w_a, rg_b_a, rg_w_x, rg_b_x, rg_lambda, w_out, norm_cross, norm_mem, w_cq, w_ckv, w_co, norm_moe, w_router, b_router, w_gate, b_gate, w_up, b_up, w_down, b_down, norm_final):
    batch, seq, d = x.shape
    mem_len = mem.shape[1]
    n = batch * seq
    assert w_in.shape[0] == 1, "single-layer trunk"
    x2 = x.reshape(n, d)

    wi = w_in[0]
    n_gate = 2 * DN_HEADS
    w_cat = jnp.concatenate(
        [wi[:, :4 * D_DN], wi[:, 4 * D_DN + n_gate:],
         jnp.pad(wi[:, 4 * D_DN:4 * D_DN + n_gate], ((0, 0), (0, LANES - n_gate)))],
        axis=1).astype(BF16)
    prow = (jnp.zeros((SUBLANES, LANES), F32)
            .at[0, DN_HEADS:n_gate].set(dn_a_log[0]).at[1, DN_HEADS:n_gate].set(dn_dt_bias[0]))
    pcol = (jnp.zeros((n_gate, LANES), F32)
            .at[DN_HEADS:, 0].set(dn_a_log[0]).at[DN_HEADS:, 1].set(dn_dt_bias[0]))

    proj, gbt = _in_proj(x2, norm_mix, w_cat, prow, pcol)
    dn = _deltanet(proj, gbt, dn_conv[0], dn_norm, batch, seq)
    rg = _rglru(proj, rg_conv[0], rg_conv_b, rg_w_a[0], rg_b_a[0].reshape(1, D_RG),
                rg_w_x[0], rg_b_x[0].reshape(1, D_RG), rg_lambda, batch, seq)
    kv = _mem_kv(mem.reshape(batch * mem_len, d), norm_mem, w_ckv[0])

    w_r = jnp.pad(w_router[0], ((0, 0), (0, LANES - N_EXPERTS)))
    w_r_hi = w_r.astype(BF16)
    w_r_lo = (w_r - w_r_hi.astype(F32)).astype(BF16)
    b_r = jnp.pad(b_router, ((0, 0), (0, LANES - N_EXPERTS)))
    h2, xn, route, route_t, counts = _post_mix(
        dn, rg, x2, w_out[0], norm_cross, w_cq[0], kv, w_co[0], norm_moe, w_r_hi, w_r_lo, b_r,
        seq, mem_len)

    n_blocks = n * TOP_K // MOE_BM + N_EXPERTS
    n_rows = n_blocks * MOE_BM
    cnt = counts[:, 0].astype(I32)
    padded = (cnt + MOE_BM - 1) // MOE_BM * MOE_BM
    pad_end = jnp.cumsum(padded)
    pad_start = pad_end - padded
    n_used = (pad_end[-1:] // MOE_BM).astype(I32)
    block_e = jnp.minimum(
        jnp.sum(pad_end[None, :] <= (jnp.arange(n_blocks, dtype=I32) * MOE_BM)[:, None], axis=1),
        N_EXPERTS - 1).astype(I32)
    meta = jnp.concatenate([pad_start + cnt, padded - cnt, n_used]).astype(I32)

    dest = _dest(pad_start.astype(I32), route_t)
    xs = _dispatch(meta, dest, xn, n, n_rows)
    has = cnt > 0
    eid = jnp.where(has, jnp.arange(N_EXPERTS, dtype=I32), N_EXPERTS)
    after = lax.cummin(eid, axis=0, reverse=True)
    next_e = jnp.concatenate([after[1:], jnp.full((1,), N_EXPERTS, I32)]).astype(I32)
    parity = ((jnp.cumsum(has.astype(I32)) - 1) & 1).astype(I32)
    pos = (jnp.arange(n_blocks, dtype=I32) * MOE_BM)[:, None]
    own = (pad_start[None, :] <= pos) & (pos < pad_end[None, :])
    left = jnp.sum(jnp.where(own, pad_start[None, :] + cnt[None, :], 0), axis=1) - pos[:, 0]
    block_valid = jnp.clip(left, 0, MOE_BM).astype(I32)
    ys = _experts(block_e, n_used, next_e, parity, block_valid, xs, w_gate, b_gate[0][:, None, :],
                  w_up, b_up[0][:, None, :], w_down, b_down[0][:, None, :])
    out = _combine(dest, h2, route, norm_final.reshape(1, d), ys)
    return out.reshape(batch, seq, d)
```
